```python
import jax, jax.numpy as jnp
from jax import lax
import numpy as np


D_MODEL = 1024
BATCH = 8
SEQ = 4096
DEPTH = 4

N_MIXERS = 3
POOL_WINDOWS = (2, 4, 8, 16)
POOL_GROUPS = len(POOL_WINDOWS)
POOL_GROUP_DIM = D_MODEL // POOL_GROUPS
SGU_CHUNK = 128
SGU_WIDTH = D_MODEL
SGU_HEAD_DIM = 128
SGU_HEADS = SGU_WIDTH // SGU_HEAD_DIM
MLA_HEADS = 16
MLA_Q_LORA = 256
MLA_KV_LORA = 128
MLA_NOPE = 128
MLA_ROPE = 64
MLA_V = 128
ROPE_THETA = 10000.0
Q_BLOCK = 128
D_FF = 4 * D_MODEL
RMS_EPS = 1e-6
LN_EPS = 1e-5
MAX_POS_OFFSET = 4096
N_POOL_LAYERS = len(range(0, DEPTH, N_MIXERS))
N_SGU_LAYERS = len(range(1, DEPTH, N_MIXERS))
N_MLA_LAYERS = len(range(2, DEPTH, N_MIXERS))

kernel_name = 'hybrid_pool_sgu_mla_decoder'


def rmsnorm(x, g):
    xf = x.astype(jnp.float32)
    y = xf * lax.rsqrt(jnp.mean(xf * xf, axis=-1, keepdims=True) + RMS_EPS)
    return (y * g.astype(jnp.float32)).astype(x.dtype)


def layernorm(x, g, b):
    xf = x.astype(jnp.float32)
    mu = jnp.mean(xf, axis=-1, keepdims=True)
    xc = xf - mu
    var = jnp.mean(xc * xc, axis=-1, keepdims=True)
    y = xc * lax.rsqrt(var + LN_EPS)
    return (y * g.astype(jnp.float32) + b.astype(jnp.float32)).astype(x.dtype)


def modulate(h, shift, scale):
    return h * (1.0 + scale[:, None, :]) + shift[:, None, :]


def pool_mixer(h, w, scale):
    B_, S_, _ = h.shape
    hf = h.astype(jnp.float32).reshape(B_, S_, POOL_GROUPS, POOL_GROUP_DIM)
    cs = jnp.cumsum(hf, axis=1)
    cs = jnp.concatenate([jnp.zeros_like(cs[:, :1]), cs], axis=1)
    t = jnp.arange(S_, dtype=jnp.float32)
    outs = []
    for gi, win in enumerate(POOL_WINDOWS):
        csg = cs[:, :, gi]
        upper = csg[:, 1:]
        lower = jnp.concatenate([jnp.zeros_like(csg[:, :win - 1]), csg[:, :S_ - win + 1]], axis=1)
        count = jnp.minimum(t + 1.0, float(win))[None, :, None]
        outs.append((upper - lower) / count - hf[:, :, gi])
    pooled = jnp.stack(outs, axis=2).astype(h.dtype)
    y = jnp.einsum('bsgc,gcd->bsgd', pooled, w).reshape(B_, S_, D_MODEL)
    return y * scale


def sgu_mixer(h, w_in, ln_g, ln_b, w_s, b_s, w_out):
    B_, S_, _ = h.shape
    z = jax.nn.gelu(h @ w_in, approximate=False)
    u, v = jnp.split(z, 2, axis=-1)
    v = layernorm(v, ln_g, ln_b)
    nc = S_ // SGU_CHUNK
    v = v.reshape(B_, nc, SGU_CHUNK, SGU_HEADS, SGU_HEAD_DIM)
    mask = jnp.tril(jnp.ones((SGU_CHUNK, SGU_CHUNK), dtype=bool))
    ws = jnp.where(mask[None], w_s, 0)
    mixed = jnp.einsum('hts,bnshc->bnthc', ws, v) + b_s.T[None, None, :, :, None]
    gated = u * mixed.reshape(B_, S_, SGU_WIDTH)
    return gated @ w_out


def apply_rope(x, cos, sin):
    x1, x2 = jnp.split(x, 2, axis=-1)
    return jnp.concatenate([x1 * cos - x2 * sin, x2 * cos + x1 * sin], axis=-1)


def mla_mixer(h, positions, w_dq_dkv, q_norm_g, kv_norm_g, w_uq, w_ukv, w_o):
    B_, S_, _ = h.shape
    lat = h @ w_dq_dkv
    c_q, c_kv, k_rope = jnp.split(lat, [MLA_Q_LORA, MLA_Q_LORA + MLA_KV_LORA], axis=-1)
    c_q = rmsnorm(c_q, q_norm_g)
    c_kv = rmsnorm(c_kv, kv_norm_g)
    q = (c_q @ w_uq).reshape(B_, S_, MLA_HEADS, MLA_NOPE + MLA_ROPE)
    q_nope, q_rope = jnp.split(q, [MLA_NOPE], axis=-1)
    kv = (c_kv @ w_ukv).reshape(B_, S_, MLA_HEADS, MLA_NOPE + MLA_V)
    k_nope, v = jnp.split(kv, [MLA_NOPE], axis=-1)
    inv_freq = ROPE_THETA ** (-jnp.arange(0, MLA_ROPE, 2, dtype=jnp.float32) / MLA_ROPE)
    ang = positions.astype(jnp.float32)[..., None] * inv_freq
    cos, sin = jnp.cos(ang), jnp.sin(ang)
    q_rope = apply_rope(q_rope.astype(jnp.float32), cos[:, :, None], sin[:, :, None]).astype(h.dtype)
    k_rope = apply_rope(k_rope.astype(jnp.float32), cos, sin).astype(h.dtype)
    sm_scale = (MLA_NOPE + MLA_ROPE) ** -0.5
    nb = S_ // Q_BLOCK

    def to_blocks(t):
        return jnp.moveaxis(t.reshape(B_, nb, Q_BLOCK, *t.shape[2:]), 1, 0)

    k_idx = jnp.arange(S_)

    def attend_block(args):
        qn, qr, blk = args
        s = jnp.einsum('bqhd,bkhd->bhqk', qn, k_nope, preferred_element_type=jnp.float32)
        s = s + jnp.einsum('bqhr,bkr->bhqk', qr, k_rope, preferred_element_type=jnp.float32)
        q_idx = blk * Q_BLOCK + jnp.arange(Q_BLOCK)
        causal = k_idx[None, :] <= q_idx[:, None]
        s = jnp.where(causal[None, None], s * sm_scale, -1e30)
        p = jax.nn.softmax(s, axis=-1).astype(v.dtype)
        return jnp.einsum('bhqk,bkhd->bqhd', p, v)

    out = lax.map(attend_block, (to_blocks(q_nope), to_blocks(q_rope), jnp.arange(nb)))
    out = jnp.moveaxis(out, 0, 1).reshape(B_, S_, MLA_HEADS * MLA_V)
    return out @ w_o


def sq_relu_mlp(h, w1, w2):
    return jnp.square(jax.nn.relu(h @ w1)) @ w2


def _fwd_setup_inputs(seed: int = 0) -> dict:
    key = jax.random.key(seed)
    ks = jax.random.split(key, 32)
    f32 = jnp.float32

    def nrm(k, shape, std):
        return jax.random.normal(k, shape, f32) * std

    def gain(k, shape):
        return 1.0 + 0.1 * jax.random.normal(k, shape, f32)

    x = jax.random.normal(ks[0], (BATCH, SEQ, D_MODEL), f32)
    c = jax.random.normal(ks[1], (BATCH, D_MODEL), f32)
    offset = jax.random.randint(ks[2], (BATCH, 1), 0, MAX_POS_OFFSET, dtype=jnp.int32)
    positions = (offset + jnp.arange(SEQ, dtype=jnp.int32)[None, :]).astype(jnp.int32)
    return {
        'x': x,
        'c': c,
        'positions': positions,
        'ada_w': nrm(ks[3], (DEPTH, D_MODEL, 6 * D_MODEL), 0.5 * D_MODEL ** -0.5),
        'ada_b': nrm(ks[4], (DEPTH, 6 * D_MODEL), 0.02),
        'norm_mix_g': gain(ks[5], (DEPTH, D_MODEL)),
        'norm_mlp_g': gain(ks[6], (DEPTH, D_MODEL)),
        'pool_w': nrm(ks[7], (N_POOL_LAYERS, POOL_GROUPS, POOL_GROUP_DIM, POOL_GROUP_DIM), POOL_GROUP_DIM ** -0.5),
        'pool_scale': gain(ks[8], (N_POOL_LAYERS, D_MODEL)),
        'sgu_w_in': nrm(ks[9], (N_SGU_LAYERS, D_MODEL, 2 * SGU_WIDTH), D_MODEL ** -0.5),
        'sgu_ln_g': gain(ks[10], (N_SGU_LAYERS, SGU_WIDTH)),
        'sgu_ln_b': nrm(ks[11], (N_SGU_LAYERS, SGU_WIDTH), 0.02),
        'sgu_w_s': nrm(ks[12], (N_SGU_LAYERS, SGU_HEADS, SGU_CHUNK, SGU_CHUNK), SGU_CHUNK ** -0.5),
        'sgu_b_s': gain(ks[13], (N_SGU_LAYERS, SGU_HEADS, SGU_CHUNK)),
        'sgu_w_out': nrm(ks[14], (N_SGU_LAYERS, SGU_WIDTH, D_MODEL), SGU_WIDTH ** -0.5),
        'mla_w_dq_dkv': nrm(ks[15], (N_MLA_LAYERS, D_MODEL, MLA_Q_LORA + MLA_KV_LORA + MLA_ROPE), D_MODEL ** -0.5),
        'mla_q_norm_g': gain(ks[16], (N_MLA_LAYERS, MLA_Q_LORA)),
        'mla_kv_norm_g': gain(ks[17], (N_MLA_LAYERS, MLA_KV_LORA)),
        'mla_w_uq': nrm(ks[18], (N_MLA_LAYERS, MLA_Q_LORA, MLA_HEADS * (MLA_NOPE + MLA_ROPE)), MLA_Q_LORA ** -0.5),
        'mla_w_ukv': nrm(ks[19], (N_MLA_LAYERS, MLA_KV_LORA, MLA_HEADS * (MLA_NOPE + MLA_V)), MLA_KV_LORA ** -0.5),
        'mla_w_o': nrm(ks[20], (N_MLA_LAYERS, MLA_HEADS * MLA_V, D_MODEL), (MLA_HEADS * MLA_V) ** -0.5),
        'mlp_w1': nrm(ks[21], (DEPTH, D_MODEL, D_FF), D_MODEL ** -0.5),
        'mlp_w2': nrm(ks[22], (DEPTH, D_FF, D_MODEL), D_FF ** -0.5),
        'final_g': gain(ks[23], (D_MODEL,)),
    }


def _fwd_reference(x, c, positions, ada_w, ada_b, norm_mix_g, norm_mlp_g, pool_w, pool_scale,
              sgu_w_in, sgu_ln_g, sgu_ln_b, sgu_w_s, sgu_b_s, sgu_w_out,
              mla_w_dq_dkv, mla_q_norm_g, mla_kv_norm_g, mla_w_uq, mla_w_ukv, mla_w_o,
              mlp_w1, mlp_w2, final_g):
    c_act = jax.nn.silu(c)
    for i in range(DEPTH):
        mod = c_act @ ada_w[i] + ada_b[i]
        sh1, sc1, g1, sh2, sc2, g2 = jnp.split(mod, 6, axis=-1)
        h = modulate(rmsnorm(x, norm_mix_g[i]), sh1, sc1)
        kind, j = i % N_MIXERS, i // N_MIXERS
        if kind == 0:
            y = pool_mixer(h, pool_w[j], pool_scale[j])
        elif kind == 1:
            y = sgu_mixer(h, sgu_w_in[j], sgu_ln_g[j], sgu_ln_b[j], sgu_w_s[j], sgu_b_s[j], sgu_w_out[j])
        else:
            y = mla_mixer(h, positions, mla_w_dq_dkv[j], mla_q_norm_g[j], mla_kv_norm_g[j],
                          mla_w_uq[j], mla_w_ukv[j], mla_w_o[j])
        x = x + g1[:, None, :] * y
        h = modulate(rmsnorm(x, norm_mlp_g[i]), sh2, sc2)
        x = x + g2[:, None, :] * sq_relu_mlp(h, mlp_w1[i], mlp_w2[i])
    return rmsnorm(x, final_g)


import jax as _jax
import jax.numpy as _jnp

TWIN_FORMAT = 'train_step'
FWD_PARAMS = ['x', 'c', 'positions', 'ada_w', 'ada_b', 'norm_mix_g', 'norm_mlp_g', 'pool_w', 'pool_scale', 'sgu_w_in', 'sgu_ln_g', 'sgu_ln_b', 'sgu_w_s', 'sgu_b_s', 'sgu_w_out', 'mla_w_dq_dkv', 'mla_q_norm_g', 'mla_kv_norm_g', 'mla_w_uq', 'mla_w_ukv', 'mla_w_o', 'mlp_w1', 'mlp_w2', 'final_g']
TWIN_WEIGHTS = ['ada_w', 'ada_b', 'norm_mix_g', 'norm_mlp_g', 'pool_w', 'pool_scale', 'sgu_w_in', 'sgu_ln_g', 'sgu_ln_b', 'sgu_w_s', 'sgu_b_s', 'sgu_w_out', 'mla_w_dq_dkv', 'mla_q_norm_g', 'mla_kv_norm_g', 'mla_w_uq', 'mla_w_ukv', 'mla_w_o', 'mlp_w1', 'mlp_w2', 'final_g']
TWIN_DIFF_INPUT = 'x'
TWIN_INPUTS = ['x', 'c', 'positions', 'ada_w', 'ada_b', 'norm_mix_g', 'norm_mlp_g', 'pool_w', 'pool_scale', 'sgu_w_in', 'sgu_ln_g', 'sgu_ln_b', 'sgu_w_s', 'sgu_b_s', 'sgu_w_out', 'mla_w_dq_dkv', 'mla_q_norm_g', 'mla_kv_norm_g', 'mla_w_uq', 'mla_w_ukv', 'mla_w_o', 'mlp_w1', 'mlp_w2', 'final_g', 'loss_target', 'm_ada_w', 'm_ada_b', 'm_norm_mix_g', 'm_norm_mlp_g', 'm_pool_w', 'm_pool_scale', 'm_sgu_w_in', 'm_sgu_ln_g', 'm_sgu_ln_b', 'm_sgu_w_s', 'm_sgu_b_s', 'm_sgu_w_out', 'm_mla_w_dq_dkv', 'm_mla_q_norm_g', 'm_mla_kv_norm_g', 'm_mla_w_uq', 'm_mla_w_ukv', 'm_mla_w_o', 'm_mlp_w1', 'm_mlp_w2', 'm_final_g', 'v_ada_w', 'v_ada_b', 'v_norm_mix_g', 'v_norm_mlp_g', 'v_pool_w', 'v_pool_scale', 'v_sgu_w_in', 'v_sgu_ln_g', 'v_sgu_ln_b', 'v_sgu_w_s', 'v_sgu_b_s', 'v_sgu_w_out', 'v_mla_w_dq_dkv', 'v_mla_q_norm_g', 'v_mla_kv_norm_g', 'v_mla_w_uq', 'v_mla_w_ukv', 'v_mla_w_o', 'v_mlp_w1', 'v_mlp_w2', 'v_final_g']
TWIN_OUTPUTS = ['loss', 'grad_x', 'grad_ada_w', 'grad_ada_b', 'grad_norm_mix_g', 'grad_norm_mlp_g', 'grad_pool_w', 'grad_pool_scale', 'grad_sgu_w_in', 'grad_sgu_ln_g', 'grad_sgu_ln_b', 'grad_sgu_w_s', 'grad_sgu_b_s', 'grad_sgu_w_out', 'grad_mla_w_dq_dkv', 'grad_mla_q_norm_g', 'grad_mla_kv_norm_g', 'grad_mla_w_uq', 'grad_mla_w_ukv', 'grad_mla_w_o', 'grad_mlp_w1', 'grad_mlp_w2', 'grad_final_g', 'delta_ada_w', 'delta_ada_b', 'delta_norm_mix_g', 'delta_norm_mlp_g', 'delta_pool_w', 'delta_pool_scale', 'delta_sgu_w_in', 'delta_sgu_ln_g', 'delta_sgu_ln_b', 'delta_sgu_w_s', 'delta_sgu_b_s', 'delta_sgu_w_out', 'delta_mla_w_dq_dkv', 'delta_mla_q_norm_g', 'delta_mla_kv_norm_g', 'delta_mla_w_uq', 'delta_mla_w_ukv', 'delta_mla_w_o', 'delta_mlp_w1', 'delta_mlp_w2', 'delta_final_g', 'new_m_ada_w', 'new_m_ada_b', 'new_m_norm_mix_g', 'new_m_norm_mlp_g', 'new_m_pool_w', 'new_m_pool_scale', 'new_m_sgu_w_in', 'new_m_sgu_ln_g', 'new_m_sgu_ln_b', 'new_m_sgu_w_s', 'new_m_sgu_b_s', 'new_m_sgu_w_out', 'new_m_mla_w_dq_dkv', 'new_m_mla_q_norm_g', 'new_m_mla_kv_norm_g', 'new_m_mla_w_uq', 'new_m_mla_w_ukv', 'new_m_mla_w_o', 'new_m_mlp_w1', 'new_m_mlp_w2', 'new_m_final_g', 'new_v_ada_w', 'new_v_ada_b', 'new_v_norm_mix_g', 'new_v_norm_mlp_g', 'new_v_pool_w', 'new_v_pool_scale', 'new_v_sgu_w_in', 'new_v_sgu_ln_g', 'new_v_sgu_ln_b', 'new_v_sgu_w_s', 'new_v_sgu_b_s', 'new_v_sgu_w_out', 'new_v_mla_w_dq_dkv', 'new_v_mla_q_norm_g', 'new_v_mla_kv_norm_g', 'new_v_mla_w_uq', 'new_v_mla_w_ukv', 'new_v_mla_w_o', 'new_v_mlp_w1', 'new_v_mlp_w2', 'new_v_final_g']
TWIN_LEAF_KINDS = {'loss': 'loss', 'grad_x': 'grad_x', 'grad_ada_w': 'grad_w', 'grad_ada_b': 'grad_w', 'grad_norm_mix_g': 'grad_w', 'grad_norm_mlp_g': 'grad_w', 'grad_pool_w': 'grad_w', 'grad_pool_scale': 'grad_w', 'grad_sgu_w_in': 'grad_w', 'grad_sgu_ln_g': 'grad_w', 'grad_sgu_ln_b': 'grad_w', 'grad_sgu_w_s': 'grad_w', 'grad_sgu_b_s': 'grad_w', 'grad_sgu_w_out': 'grad_w', 'grad_mla_w_dq_dkv': 'grad_w', 'grad_mla_q_norm_g': 'grad_w', 'grad_mla_kv_norm_g': 'grad_w', 'grad_mla_w_uq': 'grad_w', 'grad_mla_w_ukv': 'grad_w', 'grad_mla_w_o': 'grad_w', 'grad_mlp_w1': 'grad_w', 'grad_mlp_w2': 'grad_w', 'grad_final_g': 'grad_w', 'delta_ada_w': 'delta_w', 'delta_ada_b': 'delta_w', 'delta_norm_mix_g': 'delta_w', 'delta_norm_mlp_g': 'delta_w', 'delta_pool_w': 'delta_w', 'delta_pool_scale': 'delta_w', 'delta_sgu_w_in': 'delta_w', 'delta_sgu_ln_g': 'delta_w', 'delta_sgu_ln_b': 'delta_w', 'delta_sgu_w_s': 'delta_w', 'delta_sgu_b_s': 'delta_w', 'delta_sgu_w_out': 'delta_w', 'delta_mla_w_dq_dkv': 'delta_w', 'delta_mla_q_norm_g': 'delta_w', 'delta_mla_kv_norm_g': 'delta_w', 'delta_mla_w_uq': 'delta_w', 'delta_mla_w_ukv': 'delta_w', 'delta_mla_w_o': 'delta_w', 'delta_mlp_w1': 'delta_w', 'delta_mlp_w2': 'delta_w', 'delta_final_g': 'delta_w', 'new_m_ada_w': 'new_m', 'new_m_ada_b': 'new_m', 'new_m_norm_mix_g': 'new_m', 'new_m_norm_mlp_g': 'new_m', 'new_m_pool_w': 'new_m', 'new_m_pool_scale': 'new_m', 'new_m_sgu_w_in': 'new_m', 'new_m_sgu_ln_g': 'new_m', 'new_m_sgu_ln_b': 'new_m', 'new_m_sgu_w_s': 'new_m', 'new_m_sgu_b_s': 'new_m', 'new_m_sgu_w_out': 'new_m', 'new_m_mla_w_dq_dkv': 'new_m', 'new_m_mla_q_norm_g': 'new_m', 'new_m_mla_kv_norm_g': 'new_m', 'new_m_mla_w_uq': 'new_m', 'new_m_mla_w_ukv': 'new_m', 'new_m_mla_w_o': 'new_m', 'new_m_mlp_w1': 'new_m', 'new_m_mlp_w2': 'new_m', 'new_m_final_g': 'new_m', 'new_v_ada_w': 'new_v', 'new_v_ada_b': 'new_v', 'new_v_norm_mix_g': 'new_v', 'new_v_norm_mlp_g': 'new_v', 'new_v_pool_w': 'new_v', 'new_v_pool_scale': 'new_v', 'new_v_sgu_w_in': 'new_v', 'new_v_sgu_ln_g': 'new_v', 'new_v_sgu_ln_b': 'new_v', 'new_v_sgu_w_s': 'new_v', 'new_v_sgu_b_s': 'new_v', 'new_v_sgu_w_out': 'new_v', 'new_v_mla_w_dq_dkv': 'new_v', 'new_v_mla_q_norm_g': 'new_v', 'new_v_mla_kv_norm_g': 'new_v', 'new_v_mla_w_uq': 'new_v', 'new_v_mla_w_ukv': 'new_v', 'new_v_mla_w_o': 'new_v', 'new_v_mlp_w1': 'new_v', 'new_v_mlp_w2': 'new_v', 'new_v_final_g': 'new_v'}


def _forward(args):
    return _fwd_reference(*[args[k] for k in FWD_PARAMS])


def _output_shape():
    out = _jax.eval_shape(lambda: _forward(_fwd_setup_inputs(0)))
    return out.shape, out.dtype

N_MICROBATCH = 1
ADAM_LR = 0.001
ADAM_B1 = 0.9
ADAM_B2 = 0.999
ADAM_EPS = 1e-08
ADAM_WD = 0.01
ADAM_STEP = 10
PER_EXAMPLE_BATCH_AXIS = {'x': 0, 'c': 0, 'positions': 0, 'loss_target': 0}
SHARED_INPUTS = []
_WEIGHT_DTYPES = {'ada_w': _jnp.float32, 'ada_b': _jnp.float32, 'norm_mix_g': _jnp.float32, 'norm_mlp_g': _jnp.float32, 'pool_w': _jnp.float32, 'pool_scale': _jnp.float32, 'sgu_w_in': _jnp.float32, 'sgu_ln_g': _jnp.float32, 'sgu_ln_b': _jnp.float32, 'sgu_w_s': _jnp.float32, 'sgu_b_s': _jnp.float32, 'sgu_w_out': _jnp.float32, 'mla_w_dq_dkv': _jnp.float32, 'mla_q_norm_g': _jnp.float32, 'mla_kv_norm_g': _jnp.float32, 'mla_w_uq': _jnp.float32, 'mla_w_ukv': _jnp.float32, 'mla_w_o': _jnp.float32, 'mlp_w1': _jnp.float32, 'mlp_w2': _jnp.float32, 'final_g': _jnp.float32}
MOMENT_SCALE = {'ada_w': 2.520503e-01, 'ada_b': 4.684777e-01, 'norm_mix_g': 4.948901e-02, 'norm_mlp_g': 8.237068e-02, 'pool_w': 5.084733e-02, 'pool_scale': 2.402087e-01, 'sgu_w_in': 4.236161e-02, 'sgu_ln_g': 2.534083e-02, 'sgu_ln_b': 2.739634e-02, 'sgu_w_s': 2.522552e-02, 'sgu_b_s': 3.623437e-02, 'sgu_w_out': 7.483273e-02, 'mla_w_dq_dkv': 1.004236e-01, 'mla_q_norm_g': 1.687096e-02, 'mla_kv_norm_g': 2.999500e-01, 'mla_w_uq': 4.509606e-03, 'mla_w_ukv': 3.394276e-02, 'mla_w_o': 6.976646e-02, 'mlp_w1': 4.920584e-02, 'mlp_w2': 1.578344e-01, 'final_g': 3.254503e+01}


def _to_microbatches(a, axis):
    t = _jnp.moveaxis(a, axis, 0)
    t = t.reshape((N_MICROBATCH, t.shape[0] // N_MICROBATCH) + t.shape[1:])
    return _jnp.moveaxis(t, 1, axis + 1)


def setup_inputs(seed: int = 0) -> dict:
    inp = _fwd_setup_inputs(seed)
    key = _jax.random.fold_in(_jax.random.key(seed), 7919)
    shape, _ = _output_shape()
    out = dict(inp)
    out["loss_target"] = _jax.random.normal(_jax.random.fold_in(key, 0), shape, _jnp.float32)
    for i, name in enumerate(TWIN_WEIGHTS):
        w = inp[name].astype(_jnp.float32)
        if MOMENT_SCALE is None:
            s = _jnp.sqrt(_jnp.mean(_jnp.square(w)) + 1e-30)
        else:
            s = MOMENT_SCALE[name]
        km, kv = _jax.random.split(_jax.random.fold_in(key, i + 1))
        out[name] = w
        out["m_" + name] = s * _jax.random.normal(km, w.shape, _jnp.float32)
        out["v_" + name] = (s * s) * _jax.random.uniform(kv, w.shape, _jnp.float32, 0.5, 1.5)
    if N_MICROBATCH > 1:
        for name, axis in PER_EXAMPLE_BATCH_AXIS.items():
            out[name] = _to_microbatches(out[name], axis)
    return {'x': out['x'], 'c': out['c'], 'positions': out['positions'], 'ada_w': out['ada_w'], 'ada_b': out['ada_b'], 'norm_mix_g': out['norm_mix_g'], 'norm_mlp_g': out['norm_mlp_g'], 'pool_w': out['pool_w'], 'pool_scale': out['pool_scale'], 'sgu_w_in': out['sgu_w_in'], 'sgu_ln_g': out['sgu_ln_g'], 'sgu_ln_b': out['sgu_ln_b'], 'sgu_w_s': out['sgu_w_s'], 'sgu_b_s': out['sgu_b_s'], 'sgu_w_out': out['sgu_w_out'], 'mla_w_dq_dkv': out['mla_w_dq_dkv'], 'mla_q_norm_g': out['mla_q_norm_g'], 'mla_kv_norm_g': out['mla_kv_norm_g'], 'mla_w_uq': out['mla_w_uq'], 'mla_w_ukv': out['mla_w_ukv'], 'mla_w_o': out['mla_w_o'], 'mlp_w1': out['mlp_w1'], 'mlp_w2': out['mlp_w2'], 'final_g': out['final_g'], 'loss_target': out['loss_target'], 'm_ada_w': out['m_ada_w'], 'm_ada_b': out['m_ada_b'], 'm_norm_mix_g': out['m_norm_mix_g'], 'm_norm_mlp_g': out['m_norm_mlp_g'], 'm_pool_w': out['m_pool_w'], 'm_pool_scale': out['m_pool_scale'], 'm_sgu_w_in': out['m_sgu_w_in'], 'm_sgu_ln_g': out['m_sgu_ln_g'], 'm_sgu_ln_b': out['m_sgu_ln_b'], 'm_sgu_w_s': out['m_sgu_w_s'], 'm_sgu_b_s': out['m_sgu_b_s'], 'm_sgu_w_out': out['m_sgu_w_out'], 'm_mla_w_dq_dkv': out['m_mla_w_dq_dkv'], 'm_mla_q_norm_g': out['m_mla_q_norm_g'], 'm_mla_kv_norm_g': out['m_mla_kv_norm_g'], 'm_mla_w_uq': out['m_mla_w_uq'], 'm_mla_w_ukv': out['m_mla_w_ukv'], 'm_mla_w_o': out['m_mla_w_o'], 'm_mlp_w1': out['m_mlp_w1'], 'm_mlp_w2': out['m_mlp_w2'], 'm_final_g': out['m_final_g'], 'v_ada_w': out['v_ada_w'], 'v_ada_b': out['v_ada_b'], 'v_norm_mix_g': out['v_norm_mix_g'], 'v_norm_mlp_g': out['v_norm_mlp_g'], 'v_pool_w': out['v_pool_w'], 'v_pool_scale': out['v_pool_scale'], 'v_sgu_w_in': out['v_sgu_w_in'], 'v_sgu_ln_g': out['v_sgu_ln_g'], 'v_sgu_ln_b': out['v_sgu_ln_b'], 'v_sgu_w_s': out['v_sgu_w_s'], 'v_sgu_b_s': out['v_sgu_b_s'], 'v_sgu_w_out': out['v_sgu_w_out'], 'v_mla_w_dq_dkv': out['v_mla_w_dq_dkv'], 'v_mla_q_norm_g': out['v_mla_q_norm_g'], 'v_mla_kv_norm_g': out['v_mla_kv_norm_g'], 'v_mla_w_uq': out['v_mla_w_uq'], 'v_mla_w_ukv': out['v_mla_w_ukv'], 'v_mla_w_o': out['v_mla_w_o'], 'v_mlp_w1': out['v_mlp_w1'], 'v_mlp_w2': out['v_mlp_w2'], 'v_final_g': out['v_final_g']}


def _loss(weights, diff, rest, loss_target):
    with _jax.named_scope("forward"):
        args = {**rest, TWIN_DIFF_INPUT: diff, **{k: w.astype(_WEIGHT_DTYPES[k]) for k, w in weights.items()}}
        y = _forward(args)
    with _jax.named_scope("loss_head"):
        err = _jnp.square(y.astype(_jnp.float32) - loss_target)
        return 0.5 * _jnp.sum(_jnp.mean(err, axis=-1)) if err.ndim else 0.5 * err


def _adamw(w, g, m, v):
    m = ADAM_B1 * m + (1.0 - ADAM_B1) * g
    v = ADAM_B2 * v + (1.0 - ADAM_B2) * _jnp.square(g)
    m_hat = m / (1.0 - ADAM_B1 ** ADAM_STEP)
    v_hat = v / (1.0 - ADAM_B2 ** ADAM_STEP)
    delta = -ADAM_LR * (m_hat / (_jnp.sqrt(v_hat) + ADAM_EPS) + ADAM_WD * w)
    return delta, m, v


def reference(x, c, positions, ada_w, ada_b, norm_mix_g, norm_mlp_g, pool_w, pool_scale, sgu_w_in, sgu_ln_g, sgu_ln_b, sgu_w_s, sgu_b_s, sgu_w_out, mla_w_dq_dkv, mla_q_norm_g, mla_kv_norm_g, mla_w_uq, mla_w_ukv, mla_w_o, mlp_w1, mlp_w2, final_g, loss_target, m_ada_w, m_ada_b, m_norm_mix_g, m_norm_mlp_g, m_pool_w, m_pool_scale, m_sgu_w_in, m_sgu_ln_g, m_sgu_ln_b, m_sgu_w_s, m_sgu_b_s, m_sgu_w_out, m_mla_w_dq_dkv, m_mla_q_norm_g, m_mla_kv_norm_g, m_mla_w_uq, m_mla_w_ukv, m_mla_w_o, m_mlp_w1, m_mlp_w2, m_final_g, v_ada_w, v_ada_b, v_norm_mix_g, v_norm_mlp_g, v_pool_w, v_pool_scale, v_sgu_w_in, v_sgu_ln_g, v_sgu_ln_b, v_sgu_w_s, v_sgu_b_s, v_sgu_w_out, v_mla_w_dq_dkv, v_mla_q_norm_g, v_mla_kv_norm_g, v_mla_w_uq, v_mla_w_ukv, v_mla_w_o, v_mlp_w1, v_mlp_w2, v_final_g):
    given = dict(x=x, c=c, positions=positions, ada_w=ada_w, ada_b=ada_b, norm_mix_g=norm_mix_g, norm_mlp_g=norm_mlp_g, pool_w=pool_w, pool_scale=pool_scale, sgu_w_in=sgu_w_in, sgu_ln_g=sgu_ln_g, sgu_ln_b=sgu_ln_b, sgu_w_s=sgu_w_s, sgu_b_s=sgu_b_s, sgu_w_out=sgu_w_out, mla_w_dq_dkv=mla_w_dq_dkv, mla_q_norm_g=mla_q_norm_g, mla_kv_norm_g=mla_kv_norm_g, mla_w_uq=mla_w_uq, mla_w_ukv=mla_w_ukv, mla_w_o=mla_w_o, mlp_w1=mlp_w1, mlp_w2=mlp_w2, final_g=final_g, loss_target=loss_target, m_ada_w=m_ada_w, m_ada_b=m_ada_b, m_norm_mix_g=m_norm_mix_g, m_norm_mlp_g=m_norm_mlp_g, m_pool_w=m_pool_w, m_pool_scale=m_pool_scale, m_sgu_w_in=m_sgu_w_in, m_sgu_ln_g=m_sgu_ln_g, m_sgu_ln_b=m_sgu_ln_b, m_sgu_w_s=m_sgu_w_s, m_sgu_b_s=m_sgu_b_s, m_sgu_w_out=m_sgu_w_out, m_mla_w_dq_dkv=m_mla_w_dq_dkv, m_mla_q_norm_g=m_mla_q_norm_g, m_mla_kv_norm_g=m_mla_kv_norm_g, m_mla_w_uq=m_mla_w_uq, m_mla_w_ukv=m_mla_w_ukv, m_mla_w_o=m_mla_w_o, m_mlp_w1=m_mlp_w1, m_mlp_w2=m_mlp_w2, m_final_g=m_final_g, v_ada_w=v_ada_w, v_ada_b=v_ada_b, v_norm_mix_g=v_norm_mix_g, v_norm_mlp_g=v_norm_mlp_g, v_pool_w=v_pool_w, v_pool_scale=v_pool_scale, v_sgu_w_in=v_sgu_w_in, v_sgu_ln_g=v_sgu_ln_g, v_sgu_ln_b=v_sgu_ln_b, v_sgu_w_s=v_sgu_w_s, v_sgu_b_s=v_sgu_b_s, v_sgu_w_out=v_sgu_w_out, v_mla_w_dq_dkv=v_mla_w_dq_dkv, v_mla_q_norm_g=v_mla_q_norm_g, v_mla_kv_norm_g=v_mla_kv_norm_g, v_mla_w_uq=v_mla_w_uq, v_mla_w_ukv=v_mla_w_ukv, v_mla_w_o=v_mla_w_o, v_mlp_w1=v_mlp_w1, v_mlp_w2=v_mlp_w2, v_final_g=v_final_g)
    weights = {n: given[n] for n in TWIN_WEIGHTS}
    shared = {n: given[n] for n in SHARED_INPUTS}
    per_example = {n: given[n] for n in ['x', 'c', 'positions']}
    grad_fn = _jax.value_and_grad(_loss, argnums=(0, 1))

    def one_microbatch(ex, loss_target):
        ex = dict(ex)
        diff = ex.pop(TWIN_DIFF_INPUT)
        return grad_fn(weights, diff, {**shared, **ex}, loss_target)

    if N_MICROBATCH == 1:
        loss, (grad_w, grad_x) = one_microbatch(per_example, given["loss_target"])
    else:
        def body(carry, xs):
            loss_sum, grad_sum = carry
            l_k, (gw_k, gx_k) = one_microbatch(xs[0], xs[1])
            with _jax.named_scope("update"):
                return (loss_sum + l_k, _jax.tree.map(_jnp.add, grad_sum, gw_k)), gx_k

        init = (_jnp.zeros((), _jnp.float32), _jax.tree.map(_jnp.zeros_like, weights))
        (loss, grad_w), grad_x = _jax.lax.scan(body, init, (per_example, given["loss_target"]))
    with _jax.named_scope("update"):
        delta_w, new_m, new_v = {}, {}, {}
        for n in TWIN_WEIGHTS:
            delta_w[n], new_m[n], new_v[n] = _adamw(weights[n], grad_w[n], given["m_" + n], given["v_" + n])
    return (loss, grad_x, *[grad_w[n] for n in TWIN_WEIGHTS], *[delta_w[n] for n in TWIN_WEIGHTS],
            *[new_m[n] for n in TWIN_WEIGHTS], *[new_v[n] for n in TWIN_WEIGHTS])
```

```python
import functools
import math

import jax
import jax.numpy as jnp
import numpy as np
from jax import lax
from jax.experimental import pallas as pl
from jax.experimental.pallas import tpu as pltpu

F32 = jnp.float32
BF16 = jnp.bfloat16
SDS = jax.ShapeDtypeStruct
MESH = pl.DeviceIdType.MESH

NDEV = 8
V7X_VMEM_LIMIT = 56 << 20
LANES = 128
RMS_EPS = 1e-6
LN_EPS = 1e-5
POOL_WINDOWS = (2, 4, 8, 16)
HALO = 16
SGU_CHUNK = 128
SGU_HEAD = 128
MLA_NOPE, MLA_ROPE, MLA_V = 128, 64, 128
MLA_Q_LORA, MLA_KV_LORA = 256, 128
MLA_HEAD_PAD = 256
ROPE_THETA = 10000.0
SM_SCALE = (MLA_NOPE + MLA_ROPE) ** -0.5
NEG = -1e30
ADAM_LR, ADAM_B1, ADAM_B2, ADAM_EPS, ADAM_WD, ADAM_STEP = 0.001, 0.9, 0.999, 1e-08, 0.01, 10
INV_SQRT2 = 1.0 / math.sqrt(2.0)
INV_SQRT_2PI = 1.0 / math.sqrt(2.0 * math.pi)
SH1, SC1, G1, SH2, SC2, G2 = 0, 1, 2, 3, 4, 5


def _params(sem=None, vmem=V7X_VMEM_LIMIT):
    return pltpu.CompilerParams(dimension_semantics=sem, vmem_limit_bytes=vmem)


def _resident(shape):
    nd = len(shape)
    return pl.BlockSpec(shape, lambda *_: (0,) * nd, pipeline_mode=pl.Buffered(1))


def _rows1(tm, w):
    return pl.BlockSpec((tm, w), lambda i: (i, 0))


def _rms(x):
    r = lax.rsqrt(jnp.mean(x * x, axis=-1, keepdims=True) + RMS_EPS)
    return x * r, r


def _colsum(v):
    return jnp.sum(v, axis=0, keepdims=True)


def _normmod_bwd(dh, n, r, a):
    dn = dh * a
    return r * (dn - n * jnp.mean(dn * n, axis=-1, keepdims=True))


def _dot(a, b):
    return jnp.dot(a, b, preferred_element_type=F32)


def _dot_nt(a, b):
    return lax.dot_general(a, b, (((1,), (1,)), ((), ())), preferred_element_type=F32)


def _dot_tn(a, b):
    return lax.dot_general(a, b, (((0,), (0,)), ((), ())), preferred_element_type=F32)


def _gelu(x):
    return 0.5 * x * (1.0 + lax.erf(x * INV_SQRT2))


def _gelu_grad(x):
    return 0.5 * (1.0 + lax.erf(x * INV_SQRT2)) + x * jnp.exp(-0.5 * x * x) * INV_SQRT_2PI


def _swap_halves(v):
    lane = lax.broadcasted_iota(jnp.int32, v.shape, 1)
    half = MLA_ROPE // 2
    return jnp.where(lane < half, pltpu.roll(v, LANES - half, 1),
                     jnp.where(lane < MLA_ROPE, pltpu.roll(v, half, 1), 0.0))


def _mlp_up(x1, mod, gn, w1, tm, tn, name):
    S, D = x1.shape
    Fh = w1.shape[1]

    def body(x_ref, mod_ref, gn_ref, w_ref, h_ref, r_ref, hs):
        @pl.when(pl.program_id(1) == 0)
        def _():
            n, _ = _rms(x_ref[...])
            a = gn_ref[...] * (1.0 + mod_ref[SC2:SC2 + 1, :])
            h = (n * a + mod_ref[SH2:SH2 + 1, :]).astype(BF16)
            hs[...] = h
            h_ref[...] = h
        r_ref[...] = jnp.maximum(_dot(hs[...], w_ref[...]), 0.0).astype(BF16)

    return pl.pallas_call(
        body, name=name, grid=(S // tm, Fh // tn),
        in_specs=[pl.BlockSpec((tm, D), lambda i, j: (i, 0)), _resident(mod.shape), _resident(gn.shape),
                  pl.BlockSpec((D, tn), lambda i, j: (0, j))],
        out_specs=[pl.BlockSpec((tm, D), lambda i, j: (i, 0)), pl.BlockSpec((tm, tn), lambda i, j: (i, j))],
        out_shape=[SDS((S, D), BF16), SDS((S, Fh), BF16)],
        scratch_shapes=[pltpu.VMEM((tm, D), BF16)],
        compiler_params=_params(("parallel", "arbitrary")),
    )(x1, mod, gn, w1)


def _mlp_down(r, w2, x1, mod, tm, name):
    S, Fh = r.shape
    D = w2.shape[1]

    def body(r_ref, w_ref, x_ref, mod_ref, x2_ref, o_ref):
        rv = r_ref[...]
        o = _dot(rv * rv, w_ref[...])
        o_ref[...] = o.astype(BF16)
        x2_ref[...] = x_ref[...] + mod_ref[G2:G2 + 1, :] * o

    return pl.pallas_call(
        body, name=name, grid=(S // tm,),
        in_specs=[_rows1(tm, Fh), _resident(w2.shape), _rows1(tm, D), _resident(mod.shape)],
        out_specs=[_rows1(tm, D), _rows1(tm, D)],
        out_shape=[SDS((S, D), F32), SDS((S, D), BF16)],
        compiler_params=_params(("parallel",)),
    )(r, w2, x1, mod)


def _mlp_bwd_a(dx2, o, mod, w2, r, tm, tn, name):
    S, D = dx2.shape
    Fh = r.shape[1]

    def body(dx_ref, o_ref, mod_ref, w_ref, r_ref, da_ref, do_ref, st_ref, dos):
        i, j = pl.program_id(0), pl.program_id(1)

        @pl.when(jnp.logical_and(i == 0, j == 0))
        def _():
            st_ref[...] = jnp.zeros_like(st_ref)

        @pl.when(j == 0)
        def _():
            dx = dx_ref[...]
            d_o = (dx * mod_ref[G2:G2 + 1, :]).astype(BF16)
            dos[...] = d_o
            do_ref[...] = d_o
            st_ref[G2:G2 + 1, :] += _colsum(dx * o_ref[...].astype(F32))

        dz = _dot_nt(dos[...], w_ref[...])
        da_ref[...] = (dz * (2.0 * r_ref[...].astype(F32))).astype(BF16)

    return pl.pallas_call(
        body, name=name, grid=(S // tm, Fh // tn),
        in_specs=[pl.BlockSpec((tm, D), lambda i, j: (i, 0)), pl.BlockSpec((tm, D), lambda i, j: (i, 0)),
                  _resident(mod.shape), pl.BlockSpec((tn, D), lambda i, j: (j, 0)),
                  pl.BlockSpec((tm, tn), lambda i, j: (i, j))],
        out_specs=[pl.BlockSpec((tm, tn), lambda i, j: (i, j)), pl.BlockSpec((tm, D), lambda i, j: (i, 0)),
                   pl.BlockSpec((8, D), lambda i, j: (0, 0))],
        out_shape=[SDS((S, Fh), BF16), SDS((S, D), BF16), SDS((8, D), F32)],
        scratch_shapes=[pltpu.VMEM((tm, D), BF16)],
        compiler_params=_params(("arbitrary", "arbitrary")),
    )(dx2, o, mod, w2, r)


def _mlp_bwd_b(d_a, w1, x1, dx2, mod, gn, tm, name):
    S, Fh = d_a.shape
    D = w1.shape[0]

    def body(da_ref, w_ref, x_ref, dx_ref, mod_ref, gn_ref, dx1_ref, st_ref):
        @pl.when(pl.program_id(0) == 0)
        def _():
            st_ref[...] = jnp.zeros_like(st_ref)

        dh = _dot_nt(da_ref[...], w_ref[...])
        n, rr = _rms(x_ref[...])
        gn_v = gn_ref[...]
        sc1p = 1.0 + mod_ref[SC2:SC2 + 1, :]
        t = _colsum(dh * n)
        st_ref[SH2:SH2 + 1, :] += _colsum(dh)
        st_ref[SC2:SC2 + 1, :] += t * gn_v
        st_ref[6:7, :] += t * sc1p
        dx1_ref[...] = dx_ref[...] + _normmod_bwd(dh, n, rr, gn_v * sc1p)

    return pl.pallas_call(
        body, name=name, grid=(S // tm,),
        in_specs=[_rows1(tm, Fh), _resident(w1.shape), _rows1(tm, D), _rows1(tm, D), _resident(mod.shape),
                  _resident(gn.shape)],
        out_specs=[_rows1(tm, D), pl.BlockSpec((8, D), lambda i: (0, 0))],
        out_shape=[SDS((S, D), F32), SDS((8, D), F32)],
        compiler_params=_params(("arbitrary",)),
    )(d_a, w1, x1, dx2, mod, gn)


def _mm_tn(a, g, tk, tn, name, square_a=False):
    S, K1 = a.shape
    N = g.shape[1]

    def body(a_ref, g_ref, o_ref):
        av = a_ref[...]
        if square_a:
            av = av * av
        o_ref[...] = _dot_tn(av, g_ref[...]).astype(BF16)

    return pl.pallas_call(
        body, name=name, grid=(K1 // tk, N // tn),
        in_specs=[pl.BlockSpec((S, tk), lambda i, j: (0, i)), pl.BlockSpec((S, tn), lambda i, j: (0, j))],
        out_specs=pl.BlockSpec((tk, tn), lambda i, j: (i, j)),
        out_shape=SDS((K1, N), BF16),
        compiler_params=_params(("parallel", "parallel")),
    )(a, g)


def _pool_h_ext(x_ref, xp_ref, mod_ref, gn_ref, i, tm):
    ext = jnp.concatenate([xp_ref[...], x_ref[...]], axis=0)
    n, r = _rms(ext)
    a = gn_ref[...] * (1.0 + mod_ref[SC1:SC1 + 1, :])
    h = n * a + mod_ref[SH1:SH1 + 1, :]
    row = lax.broadcasted_iota(jnp.int32, (tm + HALO, 1), 0)
    h = jnp.where(jnp.logical_and(i == 0, row < HALO), 0.0, h)
    return h, n[HALO:], r[HALO:], a


def _trailing_sum(v, win):
    k = 1
    while k < win:
        v = v + pltpu.roll(v, k, 0)
        k *= 2
    return v


def _leading_sum(v, win):
    k = 1
    while k < win:
        v = v + pltpu.roll(v, v.shape[0] - k, 0)
        k *= 2
    return v


def _pool_fwd(x, mod, gn, pw, ps, tm, name):
    S, D = x.shape
    C = D // len(POOL_WINDOWS)
    hb = tm // HALO

    def body(x_ref, xp_ref, mod_ref, gn_ref, pw_ref, ps_ref, x1_ref):
        i = pl.program_id(0)
        h, _, _, _ = _pool_h_ext(x_ref, xp_ref, mod_ref, gn_ref, i, tm)
        t1 = (i * tm + lax.broadcasted_iota(jnp.int32, (tm, 1), 0)).astype(F32) + 1.0
        for g, win in enumerate(POOL_WINDOWS):
            cols = slice(g * C, (g + 1) * C)
            hg = h[:, cols]
            inv = 1.0 / jnp.minimum(t1, float(win))
            pooled = (_trailing_sum(hg, win)[HALO:] * inv - hg[HALO:]).astype(BF16)
            y = _dot(pooled, pw_ref[g]) * ps_ref[:, cols]
            x1_ref[:, cols] = x_ref[:, cols] + mod_ref[G1:G1 + 1, cols] * y

    return pl.pallas_call(
        body, name=name, grid=(S // tm,),
        in_specs=[_rows1(tm, D), pl.BlockSpec((HALO, D), lambda i: (jnp.maximum(i * hb - 1, 0), 0)),
                  _resident(mod.shape), _resident(gn.shape), _resident(pw.shape), _resident(ps.shape)],
        out_specs=_rows1(tm, D),
        out_shape=SDS((S, D), F32),
        compiler_params=_params(("parallel",)),
    )(x, x, mod, gn, pw, ps)


def _pool_bwd(x, dx1, mod, gn, pw, ps, tm, name):
    S, D = x.shape
    G = len(POOL_WINDOWS)
    C = D // G
    hb = tm // HALO
    nt = S // tm

    def body(x_ref, xp_ref, d1_ref, dn_ref, mod_ref, gn_ref, pw_ref, ps_ref, dx_ref, st_ref, dpw_ref):
        i = pl.program_id(0)

        @pl.when(i == 0)
        def _():
            st_ref[...] = jnp.zeros_like(st_ref)
            dpw_ref[...] = jnp.zeros_like(dpw_ref)

        h, n, rr, a = _pool_h_ext(x_ref, xp_ref, mod_ref, gn_ref, i, tm)
        g1 = mod_ref[G1:G1 + 1, :]
        ps_v = ps_ref[...]
        d1 = d1_ref[...]
        d1n = jnp.where(i == nt - 1, 0.0, dn_ref[...])
        dyr = (jnp.concatenate([d1, d1n], axis=0) * (g1 * ps_v)).astype(BF16)
        t1 = (i * tm + lax.broadcasted_iota(jnp.int32, (tm + HALO, 1), 0)).astype(F32) + 1.0
        parts = []
        for g, win in enumerate(POOL_WINDOWS):
            cols = slice(g * C, (g + 1) * C)
            hg = h[:, cols]
            inv = 1.0 / jnp.minimum(t1, float(win))
            pooled = (_trailing_sum(hg, win)[HALO:] * inv[:tm] - hg[HALO:]).astype(BF16)
            yraw = _dot(pooled, pw_ref[g])
            st_ref[G1:G1 + 1, cols] += _colsum(d1[:, cols] * (yraw * ps_v[:, cols]))
            st_ref[4:5, cols] += _colsum(d1[:, cols] * g1[:, cols] * yraw)
            dpw_ref[g] += _dot_tn(pooled, dyr[:tm, cols])
            dpool = _dot_nt(dyr[:, cols], pw_ref[g])
            parts.append(_leading_sum(dpool * inv, win)[:tm] - dpool[:tm])
        dh = jnp.concatenate(parts, axis=1)
        t = _colsum(dh * n)
        st_ref[SH1:SH1 + 1, :] += _colsum(dh)
        st_ref[SC1:SC1 + 1, :] += t * gn_ref[...]
        st_ref[3:4, :] += t * (1.0 + mod_ref[SC1:SC1 + 1, :])
        dx_ref[...] = d1 + _normmod_bwd(dh, n, rr, a)

    return pl.pallas_call(
        body, name=name, grid=(nt,),
        in_specs=[_rows1(tm, D), pl.BlockSpec((HALO, D), lambda i: (jnp.maximum(i * hb - 1, 0), 0)),
                  _rows1(tm, D), pl.BlockSpec((HALO, D), lambda i: (jnp.minimum((i + 1) * hb, S // HALO - 1), 0)),
                  _resident(mod.shape), _resident(gn.shape), _resident(pw.shape), _resident(ps.shape)],
        out_specs=[_rows1(tm, D), pl.BlockSpec((8, D), lambda i: (0, 0)), pl.BlockSpec((G, C, C), lambda i: (0, 0, 0))],
        out_shape=[SDS((S, D), F32), SDS((8, D), F32), SDS((G, C, C), F32)],
        compiler_params=_params(("arbitrary",)),
    )(x, x, dx1, dx1, mod, gn, pw, ps)


def _tril_bf16(w):
    row = lax.broadcasted_iota(jnp.int32, w.shape, 0)
    col = lax.broadcasted_iota(jnp.int32, w.shape, 1)
    return jnp.where(col <= row, w, 0.0).astype(BF16)


def _sgu_front(pre, lng_ref, lnb_ref, W):
    z = _gelu(pre)
    u, v = z[:, :W], z[:, W:]
    mu = jnp.mean(v, axis=-1, keepdims=True)
    xc = v - mu
    rstd = lax.rsqrt(jnp.mean(xc * xc, axis=-1, keepdims=True) + LN_EPS)
    vhat = xc * rstd
    return u, vhat, rstd, vhat * lng_ref[...] + lnb_ref[...]


def _sgu_mix(vn, ws_ref, bst_ref, mix_s, tm, W):
    for hd in range(W // SGU_HEAD):
        wm = _tril_bf16(ws_ref[hd])
        bcol = bst_ref[:, hd:hd + 1]
        for ci in range(tm // SGU_CHUNK):
            rs, cs = slice(ci * SGU_CHUNK, (ci + 1) * SGU_CHUNK), slice(hd * SGU_HEAD, (hd + 1) * SGU_HEAD)
            mix_s[rs, cs] = _dot(wm, vn[rs, cs].astype(BF16)) + bcol


def _sgu_fwd(x, mod, gn, w_in, lng, lnb, ws, bst, w_out, tm, name):
    S, D = x.shape
    W = w_out.shape[0]

    def body(x_ref, mod_ref, gn_ref, win_ref, lng_ref, lnb_ref, ws_ref, bst_ref, wout_ref,
             x1_ref, h_ref, pre_ref, y_ref, mix_s):
        n, _ = _rms(x_ref[...])
        a = gn_ref[...] * (1.0 + mod_ref[SC1:SC1 + 1, :])
        h = (n * a + mod_ref[SH1:SH1 + 1, :]).astype(BF16)
        h_ref[...] = h
        pre = _dot(h, win_ref[...])
        pre_ref[...] = pre.astype(BF16)
        u, _, _, vn = _sgu_front(pre, lng_ref, lnb_ref, W)
        _sgu_mix(vn, ws_ref, bst_ref, mix_s, tm, W)
        y = _dot((u * mix_s[...]).astype(BF16), wout_ref[...])
        y_ref[...] = y.astype(BF16)
        x1_ref[...] = x_ref[...] + mod_ref[G1:G1 + 1, :] * y

    return pl.pallas_call(
        body, name=name, grid=(S // tm,),
        in_specs=[_rows1(tm, D), _resident(mod.shape), _resident(gn.shape), _resident(w_in.shape),
                  _resident(lng.shape), _resident(lnb.shape), _resident(ws.shape), _resident(bst.shape),
                  _resident(w_out.shape)],
        out_specs=[_rows1(tm, D), _rows1(tm, D), _rows1(tm, 2 * W), _rows1(tm, D)],
        out_shape=[SDS((S, D), F32), SDS((S, D), BF16), SDS((S, 2 * W), BF16), SDS((S, D), BF16)],
        scratch_shapes=[pltpu.VMEM((tm, W), F32)],
        compiler_params=_params(("parallel",)),
    )(x, mod, gn, w_in, lng, lnb, ws, bst, w_out)


def _sgu_bwd(x, dx1, pre, y, mod, gn, w_in, lng, lnb, ws, bst, w_out, tm, name):
    S, D = x.shape
    W = w_out.shape[0]
    H = W // SGU_HEAD
    nt = S // tm

    def body(x_ref, d1_ref, pre_ref, y_ref, mod_ref, gn_ref, win_ref, lng_ref, lnb_ref, ws_ref, bst_ref, wout_ref,
             dx_ref, dy_ref, gt_ref, dpre_ref, st_ref, dws_ref, dbs_ref, mix_s, dvn_s):
        i = pl.program_id(0)

        @pl.when(i == 0)
        def _():
            st_ref[...] = jnp.zeros_like(st_ref)
            dws_ref[...] = jnp.zeros_like(dws_ref)
            dbs_ref[...] = jnp.zeros_like(dbs_ref)

        d1 = d1_ref[...]
        pre = pre_ref[...].astype(F32)
        u, vhat, rstd, vn = _sgu_front(pre, lng_ref, lnb_ref, W)
        _sgu_mix(vn, ws_ref, bst_ref, mix_s, tm, W)
        mixed = mix_s[...]
        gt_ref[...] = (u * mixed).astype(BF16)
        dyb = (d1 * mod_ref[G1:G1 + 1, :]).astype(BF16)
        dy_ref[...] = dyb
        st_ref[G1:G1 + 1, :] += _colsum(d1 * y_ref[...].astype(F32))
        dgt = _dot_nt(dyb, wout_ref[...])
        du = dgt * mixed
        dmix = dgt * u
        for hd in range(H):
            wm = _tril_bf16(ws_ref[hd])
            for ci in range(tm // SGU_CHUNK):
                rs, cs = slice(ci * SGU_CHUNK, (ci + 1) * SGU_CHUNK), slice(hd * SGU_HEAD, (hd + 1) * SGU_HEAD)
                dm = dmix[rs, cs]
                dmb = dm.astype(BF16)
                dbs_ref[hd] += jnp.broadcast_to(jnp.sum(dm, axis=1, keepdims=True), (SGU_CHUNK, LANES))
                dws_ref[hd] += _dot_nt(dmb, vn[rs, cs].astype(BF16))
                dvn_s[rs, cs] = _dot_tn(wm, dmb)
        dvn = dvn_s[...]
        st_ref[4:5, :] += _colsum(dvn * vhat)
        st_ref[5:6, :] += _colsum(dvn)
        dvh = dvn * lng_ref[...]
        dv = rstd * (dvh - jnp.mean(dvh, axis=-1, keepdims=True) - vhat * jnp.mean(dvh * vhat, axis=-1, keepdims=True))
        dpre_u = (du * _gelu_grad(pre[:, :W])).astype(BF16)
        dpre_v = (dv * _gelu_grad(pre[:, W:])).astype(BF16)
        dpre_ref[:, :W] = dpre_u
        dpre_ref[:, W:] = dpre_v
        dh = _dot_nt(dpre_u, win_ref[:, :W]) + _dot_nt(dpre_v, win_ref[:, W:])
        n, rr = _rms(x_ref[...])
        gn_v = gn_ref[...]
        sc1p = 1.0 + mod_ref[SC1:SC1 + 1, :]
        t = _colsum(dh * n)
        st_ref[SH1:SH1 + 1, :] += _colsum(dh)
        st_ref[SC1:SC1 + 1, :] += t * gn_v
        st_ref[3:4, :] += t * sc1p
        dx_ref[...] = d1 + _normmod_bwd(dh, n, rr, gn_v * sc1p)

        @pl.when(i == nt - 1)
        def _():
            for hd in range(H):
                row = lax.broadcasted_iota(jnp.int32, (SGU_CHUNK, SGU_CHUNK), 0)
                col = lax.broadcasted_iota(jnp.int32, (SGU_CHUNK, SGU_CHUNK), 1)
                dws_ref[hd] = jnp.where(col <= row, dws_ref[hd], 0.0)

    return pl.pallas_call(
        body, name=name, grid=(nt,),
        in_specs=[_rows1(tm, D), _rows1(tm, D), _rows1(tm, 2 * W), _rows1(tm, D), _resident(mod.shape),
                  _resident(gn.shape), _resident(w_in.shape), _resident(lng.shape), _resident(lnb.shape),
                  _resident(ws.shape), _resident(bst.shape), _resident(w_out.shape)],
        out_specs=[_rows1(tm, D), _rows1(tm, D), _rows1(tm, W), _rows1(tm, 2 * W),
                   pl.BlockSpec((8, D), lambda i: (0, 0)), pl.BlockSpec((H, SGU_CHUNK, SGU_CHUNK), lambda i: (0, 0, 0)),
                   pl.BlockSpec((H, SGU_CHUNK, LANES), lambda i: (0, 0, 0))],
        out_shape=[SDS((S, D), F32), SDS((S, D), BF16), SDS((S, W), BF16), SDS((S, 2 * W), BF16),
                   SDS((8, D), F32), SDS((H, SGU_CHUNK, SGU_CHUNK), F32), SDS((H, SGU_CHUNK, LANES), F32)],
        scratch_shapes=[pltpu.VMEM((tm, W), F32), pltpu.VMEM((tm, W), F32)],
        compiler_params=_params(("arbitrary",)),
    )(x, dx1, pre, y, mod, gn, w_in, lng, lnb, ws, bst, w_out)


def _mla_lat(x, mod, gn, wd, qg, kvg, cos_t, sin_t, tm, name):
    S, D = x.shape
    LW = wd.shape[1]
    QL, KL = MLA_Q_LORA, MLA_KV_LORA

    def body(x_ref, mod_ref, gn_ref, wd_ref, qg_ref, kvg_ref, c_ref, s_ref, h_ref, lat_ref, cq_ref, ckv_ref, kr_ref):
        n, _ = _rms(x_ref[...])
        a = gn_ref[...] * (1.0 + mod_ref[SC1:SC1 + 1, :])
        h = (n * a + mod_ref[SH1:SH1 + 1, :]).astype(BF16)
        h_ref[...] = h
        lat = _dot(h, wd_ref[...])
        lat_ref[...] = lat
        nq, _ = _rms(lat[:, :QL])
        cq_ref[...] = (nq * qg_ref[...]).astype(BF16)
        nkv, _ = _rms(lat[:, QL:QL + KL])
        ckv_ref[...] = (nkv * kvg_ref[...]).astype(BF16)
        kr = lat[:, QL + KL:]
        kr_ref[...] = (kr * c_ref[...] + _swap_halves(kr) * s_ref[...]).astype(BF16)

    return pl.pallas_call(
        body, name=name, grid=(S // tm,),
        in_specs=[_rows1(tm, D), _resident(mod.shape), _resident(gn.shape), _resident(wd.shape), _resident(qg.shape),
                  _resident(kvg.shape), _rows1(tm, LANES), _rows1(tm, LANES)],
        out_specs=[_rows1(tm, D), _rows1(tm, LW), _rows1(tm, QL), _rows1(tm, KL), _rows1(tm, LANES)],
        out_shape=[SDS((S, D), BF16), SDS((S, LW), F32), SDS((S, QL), BF16), SDS((S, KL), BF16), SDS((S, LANES), BF16)],
        compiler_params=_params(("parallel",)),
    )(x, mod, gn, wd, qg, kvg, cos_t, sin_t)


def _mla_qkv(cq, ckv, krp, wq, wukv, cos_t, sin_t, tm, name):
    S = cq.shape[0]
    H = wq.shape[1] // MLA_HEAD_PAD
    HP = MLA_HEAD_PAD

    def body(cq_ref, ckv_ref, kr_ref, wq_ref, wkv_ref, c_ref, s_ref, q_ref, k_ref, v_ref):
        q = _dot(cq_ref[...], wq_ref[...])
        qr = q[:, MLA_NOPE:]
        q_ref[:, :MLA_NOPE] = (q[:, :MLA_NOPE] * SM_SCALE).astype(BF16)
        q_ref[:, MLA_NOPE:] = ((qr * c_ref[...] + _swap_halves(qr) * s_ref[...]) * SM_SCALE).astype(BF16)
        kv = _dot(ckv_ref[...], wkv_ref[...])
        k_ref[:, :MLA_NOPE] = kv[:, :MLA_NOPE].astype(BF16)
        k_ref[:, MLA_NOPE:] = kr_ref[...]
        v_ref[...] = kv[:, MLA_NOPE:].astype(BF16)

    return pl.pallas_call(
        body, name=name, grid=(S // tm, H),
        in_specs=[pl.BlockSpec((tm, MLA_Q_LORA), lambda i, h: (i, 0)), pl.BlockSpec((tm, MLA_KV_LORA), lambda i, h: (i, 0)),
                  pl.BlockSpec((tm, LANES), lambda i, h: (i, 0)), pl.BlockSpec((MLA_Q_LORA, HP), lambda i, h: (0, h)),
                  pl.BlockSpec((MLA_KV_LORA, HP), lambda i, h: (0, h)), pl.BlockSpec((tm, LANES), lambda i, h: (i, 0)),
                  pl.BlockSpec((tm, LANES), lambda i, h: (i, 0))],
        out_specs=[pl.BlockSpec((tm, HP), lambda i, h: (i, h)), pl.BlockSpec((tm, HP), lambda i, h: (i, h)),
                   pl.BlockSpec((tm, MLA_V), lambda i, h: (i, h))],
        out_shape=[SDS((S, H * HP), BF16), SDS((S, H * HP), BF16), SDS((S, H * MLA_V), BF16)],
        compiler_params=_params(("parallel", "parallel")),
    )(cq, ckv, krp, wq, wukv, cos_t, sin_t)


def _causal_mask(tq):
    row = lax.broadcasted_iota(jnp.int32, (tq, tq), 0)
    col = lax.broadcasted_iota(jnp.int32, (tq, tq), 1)
    return col <= row


def _attn_fwd(q, k, v, tq, name):
    S = q.shape[0]
    HP = MLA_HEAD_PAD
    H = q.shape[1] // HP
    nq = S // tq

    def body(q_ref, k_ref, v_ref, o_ref, lse_ref):
        def q_tile(i, _):
            rows = pl.ds(pl.multiple_of(i * tq, tq), tq)
            qv = q_ref[rows, :]

            def step(j, carry, masked):
                m, l, acc = carry
                krows = pl.ds(pl.multiple_of(j * tq, tq), tq)
                s = _dot_nt(qv, k_ref[krows, :])
                if masked:
                    s = jnp.where(_causal_mask(tq), s, NEG)
                m_new = jnp.maximum(m, jnp.max(s, axis=1, keepdims=True))
                p = jnp.exp(s - m_new)
                alpha = jnp.exp(m - m_new)
                l = alpha * l + jnp.sum(p, axis=1, keepdims=True)
                acc = alpha * acc + _dot(p.astype(BF16), v_ref[krows, :])
                return m_new, l, acc

            init = (jnp.full((tq, 1), NEG, F32), jnp.zeros((tq, 1), F32), jnp.zeros((tq, MLA_V), F32))
            carry = lax.fori_loop(0, i, lambda j, c: step(j, c, False), init)
            m, l, acc = step(i, carry, True)
            o_ref[rows, :] = (acc / l).astype(BF16)
            lse_ref[0, rows, :] = jnp.broadcast_to(m + jnp.log(l), (tq, LANES))
            return 0

        lax.fori_loop(0, nq, q_tile, 0)

    return pl.pallas_call(
        body, name=name, grid=(H,),
        in_specs=[pl.BlockSpec((S, HP), lambda h: (0, h)), pl.BlockSpec((S, HP), lambda h: (0, h)),
                  pl.BlockSpec((S, MLA_V), lambda h: (0, h))],
        out_specs=[pl.BlockSpec((S, MLA_V), lambda h: (0, h)), pl.BlockSpec((1, S, LANES), lambda h: (h, 0, 0))],
        out_shape=[SDS((S, H * MLA_V), BF16), SDS((H, S, LANES), F32)],
        compiler_params=_params(("parallel",)),
    )(q, k, v)


def _attn_bwd(q, k, v, o, do, lse, cos_t, sin_t, tq, name):
    S = q.shape[0]
    HP = MLA_HEAD_PAD
    H = q.shape[1] // HP
    nq = S // tq

    def body(q_ref, k_ref, v_ref, o_ref, do_ref, lse_ref, c_ref, s_ref, dq_ref, dkv_ref, dkr_ref, dq_acc, dl_s):
        @pl.when(pl.program_id(0) == 0)
        def _():
            dkr_ref[...] = jnp.zeros_like(dkr_ref)

        dq_acc[...] = jnp.zeros_like(dq_acc)

        def delta_tile(i, _):
            rows = pl.ds(pl.multiple_of(i * tq, tq), tq)
            d = jnp.sum(do_ref[rows, :].astype(F32) * o_ref[rows, :].astype(F32), axis=1, keepdims=True)
            dl_s[rows, :] = jnp.broadcast_to(d, (tq, LANES))
            return 0

        lax.fori_loop(0, nq, delta_tile, 0)

        def kv_tile(j, _):
            krows = pl.ds(pl.multiple_of(j * tq, tq), tq)
            kv_k = k_ref[krows, :]
            kv_v = v_ref[krows, :]

            def step(i, carry, masked):
                dk, dv = carry
                rows = pl.ds(pl.multiple_of(i * tq, tq), tq)
                qv = q_ref[rows, :]
                dov = do_ref[rows, :]
                s = _dot_nt(qv, kv_k)
                if masked:
                    s = jnp.where(_causal_mask(tq), s, NEG)
                p = jnp.exp(s - lse_ref[0, rows, 0:1])
                dv = dv + _dot_tn(p.astype(BF16), dov)
                dp = _dot_nt(dov, kv_v)
                ds = (p * (dp - dl_s[rows, 0:1])).astype(BF16)
                dk = dk + _dot_tn(ds, qv)
                dq_acc[rows, :] += _dot(ds, kv_k)
                return dk, dv

            carry = step(j, (jnp.zeros((tq, HP), F32), jnp.zeros((tq, MLA_V), F32)), True)
            dk, dv = lax.fori_loop(j + 1, nq, lambda i, c: step(i, c, False), carry)
            dkv_ref[krows, :MLA_NOPE] = dk[:, :MLA_NOPE].astype(BF16)
            dkv_ref[krows, MLA_NOPE:] = dv.astype(BF16)
            dkr_ref[krows, :] += dk[:, MLA_NOPE:]
            return 0

        lax.fori_loop(0, nq, kv_tile, 0)

        def out_tile(i, _):
            rows = pl.ds(pl.multiple_of(i * tq, tq), tq)
            dq = dq_acc[rows, :] * SM_SCALE
            dqr = dq[:, MLA_NOPE:]
            dq_ref[rows, :MLA_NOPE] = dq[:, :MLA_NOPE].astype(BF16)
            dq_ref[rows, MLA_NOPE:] = (dqr * c_ref[rows, :] + _swap_halves(dqr * s_ref[rows, :])).astype(BF16)
            return 0

        lax.fori_loop(0, nq, out_tile, 0)

    return pl.pallas_call(
        body, name=name, grid=(H,),
        in_specs=[pl.BlockSpec((S, HP), lambda h: (0, h)), pl.BlockSpec((S, HP), lambda h: (0, h)),
                  pl.BlockSpec((S, MLA_V), lambda h: (0, h)), pl.BlockSpec((S, MLA_V), lambda h: (0, h)),
                  pl.BlockSpec((S, MLA_V), lambda h: (0, h)), pl.BlockSpec((1, S, LANES), lambda h: (h, 0, 0)),
                  _resident(cos_t.shape), _resident(sin_t.shape)],
        out_specs=[pl.BlockSpec((S, HP), lambda h: (0, h)), pl.BlockSpec((S, HP), lambda h: (0, h)),
                   pl.BlockSpec((S, LANES), lambda h: (0, 0))],
        out_shape=[SDS((S, H * HP), BF16), SDS((S, H * HP), BF16), SDS((S, LANES), F32)],
        scratch_shapes=[pltpu.VMEM((S, HP), F32), pltpu.VMEM((S, LANES), F32)],
        compiler_params=_params(("arbitrary",)),
    )(q, k, v, o, do, lse, cos_t, sin_t)


def _mla_out(o, w_o, x, mod, tm, name):
    S, KO = o.shape
    D = w_o.shape[1]

    def body(o_ref, w_ref, x_ref, mod_ref, x1_ref, y_ref):
        y = _dot(o_ref[...], w_ref[...])
        y_ref[...] = y.astype(BF16)
        x1_ref[...] = x_ref[...] + mod_ref[G1:G1 + 1, :] * y

    return pl.pallas_call(
        body, name=name, grid=(S // tm,),
        in_specs=[_rows1(tm, KO), _resident(w_o.shape), _rows1(tm, D), _resident(mod.shape)],
        out_specs=[_rows1(tm, D), _rows1(tm, D)],
        out_shape=[SDS((S, D), F32), SDS((S, D), BF16)],
        compiler_params=_params(("parallel",)),
    )(o, w_o, x, mod)


def _mla_bwd_o(dx1, y, mod, w_o, tm, name):
    S, D = dx1.shape
    KO = w_o.shape[0]

    def body(d1_ref, y_ref, mod_ref, w_ref, dy_ref, do_ref, st_ref):
        @pl.when(pl.program_id(0) == 0)
        def _():
            st_ref[...] = jnp.zeros_like(st_ref)

        d1 = d1_ref[...]
        dyb = (d1 * mod_ref[G1:G1 + 1, :]).astype(BF16)
        dy_ref[...] = dyb
        st_ref[G1:G1 + 1, :] += _colsum(d1 * y_ref[...].astype(F32))
        do_ref[...] = _dot_nt(dyb, w_ref[...]).astype(BF16)

    return pl.pallas_call(
        body, name=name, grid=(S // tm,),
        in_specs=[_rows1(tm, D), _rows1(tm, D), _resident(mod.shape), _resident(w_o.shape)],
        out_specs=[_rows1(tm, D), _rows1(tm, KO), pl.BlockSpec((8, D), lambda i: (0, 0))],
        out_shape=[SDS((S, D), BF16), SDS((S, KO), BF16), SDS((8, D), F32)],
        compiler_params=_params(("arbitrary",)),
    )(dx1, y, mod, w_o)


def _mla_bwd_lat(dq, dkv, dkr, lat, x, dx1, mod, gn, qg, kvg, wq, wukv, wd, cos_t, sin_t, tm, name):
    S, D = x.shape
    LW = wd.shape[1]
    QL, KL = MLA_Q_LORA, MLA_KV_LORA

    def body(dq_ref, dkv_ref, dkr_ref, lat_ref, x_ref, d1_ref, mod_ref, gn_ref, qg_ref, kvg_ref, wq_ref, wkv_ref, wd_ref,
             c_ref, s_ref, dx_ref, dlat_ref, st_ref, dqg_ref, dkvg_ref):
        @pl.when(pl.program_id(0) == 0)
        def _():
            st_ref[...] = jnp.zeros_like(st_ref)
            dqg_ref[...] = jnp.zeros_like(dqg_ref)
            dkvg_ref[...] = jnp.zeros_like(dkvg_ref)

        lat = lat_ref[...]
        dcq = _dot_nt(dq_ref[...], wq_ref[...])
        nq, rq = _rms(lat[:, :QL])
        dqg_ref[0:1, :] += _colsum(dcq * nq)
        dlat_q = _normmod_bwd(dcq, nq, rq, qg_ref[...]).astype(BF16)
        dckv = _dot_nt(dkv_ref[...], wkv_ref[...])
        nkv, rkv = _rms(lat[:, QL:QL + KL])
        dkvg_ref[0:1, :] += _colsum(dckv * nkv)
        dlat_kv = _normmod_bwd(dckv, nkv, rkv, kvg_ref[...]).astype(BF16)
        dkr = dkr_ref[...]
        dlat_kr = (dkr * c_ref[...] + _swap_halves(dkr * s_ref[...])).astype(BF16)
        dlat_ref[:, :QL] = dlat_q
        dlat_ref[:, QL:QL + KL] = dlat_kv
        dlat_ref[:, QL + KL:] = dlat_kr
        dh = (_dot_nt(dlat_q, wd_ref[:, :QL]) + _dot_nt(dlat_kv, wd_ref[:, QL:QL + KL])
              + _dot_nt(dlat_kr, wd_ref[:, QL + KL:]))
        n, rr = _rms(x_ref[...])
        gn_v = gn_ref[...]
        sc1p = 1.0 + mod_ref[SC1:SC1 + 1, :]
        t = _colsum(dh * n)
        st_ref[SH1:SH1 + 1, :] += _colsum(dh)
        st_ref[SC1:SC1 + 1, :] += t * gn_v
        st_ref[3:4, :] += t * sc1p
        dx_ref[...] = d1_ref[...] + _normmod_bwd(dh, n, rr, gn_v * sc1p)

    HW = wq.shape[1]
    return pl.pallas_call(
        body, name=name, grid=(S // tm,),
        in_specs=[_rows1(tm, HW), _rows1(tm, HW), _rows1(tm, LANES), _rows1(tm, LW), _rows1(tm, D), _rows1(tm, D),
                  _resident(mod.shape), _resident(gn.shape), _resident(qg.shape), _resident(kvg.shape),
                  _resident(wq.shape), _resident(wukv.shape), _resident(wd.shape), _rows1(tm, LANES), _rows1(tm, LANES)],
        out_specs=[_rows1(tm, D), _rows1(tm, LW), pl.BlockSpec((8, D), lambda i: (0, 0)),
                   pl.BlockSpec((8, QL), lambda i: (0, 0)), pl.BlockSpec((8, KL), lambda i: (0, 0))],
        out_shape=[SDS((S, D), F32), SDS((S, LW), BF16), SDS((8, D), F32), SDS((8, QL), F32), SDS((8, KL), F32)],
        compiler_params=_params(("arbitrary",)),
    )(dq, dkv, dkr, lat, x, dx1, mod, gn, qg, kvg, wq, wukv, wd, cos_t, sin_t)


def _loss_head(x, tgt, fg, tm, name):
    S, D = x.shape
    nt = S // tm

    def body(x_ref, t_ref, g_ref, dx_ref, acc_ref):
        i = pl.program_id(0)

        @pl.when(i == 0)
        def _():
            acc_ref[...] = jnp.zeros_like(acc_ref)

        n, rr = _rms(x_ref[...])
        g = g_ref[...]
        err = n * g - t_ref[...]
        acc_ref[1:2, :] += _colsum(err * err) * (0.5 / D)
        dy = err * (1.0 / D)
        acc_ref[0:1, :] += _colsum(dy * n)
        dx_ref[...] = _normmod_bwd(dy, n, rr, g)

        @pl.when(i == nt - 1)
        def _():
            acc_ref[2:3, :] = jnp.broadcast_to(jnp.sum(acc_ref[1:2, :], axis=1, keepdims=True), (1, D))

    return pl.pallas_call(
        body, name=name, grid=(nt,),
        in_specs=[_rows1(tm, D), _rows1(tm, D), _resident(fg.shape)],
        out_specs=[_rows1(tm, D), pl.BlockSpec((8, D), lambda i: (0, 0))],
        out_shape=[SDS((S, D), F32), SDS((8, D), F32)],
        compiler_params=_params(("arbitrary",)),
    )(x, tgt, fg)


def _rope_tables(positions, S):
    inv_freq = ROPE_THETA ** (-jnp.arange(0, MLA_ROPE, 2, dtype=F32) / MLA_ROPE)
    ang = positions.reshape(S, 1).astype(F32) * inv_freq
    cos, sin = jnp.cos(ang), jnp.sin(ang)
    z = jnp.zeros((S, LANES - MLA_ROPE), F32)
    return jnp.concatenate([cos, cos, z], axis=1), jnp.concatenate([-sin, sin, z], axis=1)


def _pad_heads(w, per_head):
    K = w.shape[0]
    H = w.shape[1] // per_head
    w3 = w.reshape(K, H, per_head)
    return jnp.pad(w3, ((0, 0), (0, 0), (0, MLA_HEAD_PAD - per_head))).reshape(K, H * MLA_HEAD_PAD)


def _unpad_heads(w, per_head):
    K = w.shape[0]
    H = w.shape[1] // MLA_HEAD_PAD
    return w.reshape(K, H, MLA_HEAD_PAD)[:, :, :per_head].reshape(K, H * per_head)


def _local_step(x, tgt, positions, mod, gmix, gmlp, fg, wts, small):
    S, D = x.shape
    L = mod.shape[0]
    tm = min(512, S)
    tms = min(256, S)
    tq = min(512, S)
    cos_t, sin_t = _rope_tables(positions, S)
    saved = []
    for i in range(L):
        kind, w = i % 3, wts[i]
        sv = {"x": x}
        if kind == 0:
            x1 = _pool_fwd(x, mod[i], gmix[i], w["pool_w"], small["pool_scale"][i // 3], tm, f"pool_fwd_{i}")
        elif kind == 1:
            x1, sv["h"], sv["pre"], sv["y"] = _sgu_fwd(
                x, mod[i], gmix[i], w["sgu_w_in"], small["sgu_ln_g"], small["sgu_ln_b"], small["sgu_w_s"],
                small["sgu_b_s_t"], w["sgu_w_out"], tms, f"sgu_fwd_{i}")
        else:
            sv["h"], sv["lat"], sv["cq"], sv["ckv"], krp = _mla_lat(
                x, mod[i], gmix[i], w["mla_wd"], small["mla_q_norm_g"], small["mla_kv_norm_g"], cos_t, sin_t, tm,
                f"mla_lat_{i}")
            sv["q"], sv["k"], sv["v"] = _mla_qkv(sv["cq"], sv["ckv"], krp, w["mla_wq"], w["mla_w_ukv"], cos_t, sin_t,
                                                 tm, f"mla_qkv_{i}")
            sv["o"], sv["lse"] = _attn_fwd(sv["q"], sv["k"], sv["v"], tq, f"attn_fwd_{i}")
            x1, sv["y"] = _mla_out(sv["o"], w["mla_w_o"], x, mod[i], tm, f"mla_out_{i}")
        sv["x1"] = x1
        Fh = w["mlp_w1"].shape[1]
        sv["h2"], sv["r"] = _mlp_up(x1, mod[i], gmlp[i], w["mlp_w1"], tm, min(1024, Fh), f"mlp_up_{i}")
        x, sv["o2"] = _mlp_down(sv["r"], w["mlp_w2"], x1, mod[i], tm, f"mlp_down_{i}")
        saved.append(sv)

    dx, loss_acc = _loss_head(x, tgt, fg, tm, "loss_head")

    stats, grads, sgrads = [None] * L, [None] * L, {}
    for i in reversed(range(L)):
        kind, w, sv = i % 3, wts[i], saved[i]
        Fh = w["mlp_w1"].shape[1]
        g = {}
        d_a, d_o, st_a = _mlp_bwd_a(dx, sv["o2"], mod[i], w["mlp_w2"], sv["r"], tm, min(1024, Fh), f"mlp_bwd_a_{i}")
        g["mlp_w2"] = _mm_tn(sv["r"], d_o, min(512, Fh), D, f"mlp_dw2_{i}", square_a=True)
        g["mlp_w1"] = _mm_tn(sv["h2"], d_a, D, min(512, Fh), f"mlp_dw1_{i}")
        dx1, st_b = _mlp_bwd_b(d_a, w["mlp_w1"], sv["x1"], dx, mod[i], gmlp[i], tm, f"mlp_bwd_b_{i}")
        if kind == 0:
            dx, st_m, dpw = _pool_bwd(sv["x"], dx1, mod[i], gmix[i], w["pool_w"], small["pool_scale"][i // 3], tm,
                                      f"pool_bwd_{i}")
            g["pool_w"] = dpw
            sgrads[f"pool_scale_{i // 3}"] = st_m[4:5]
        elif kind == 1:
            dx, dyb, gated, dpre, st_m, dws, dbs = _sgu_bwd(
                sv["x"], dx1, sv["pre"], sv["y"], mod[i], gmix[i], w["sgu_w_in"], small["sgu_ln_g"], small["sgu_ln_b"],
                small["sgu_w_s"], small["sgu_b_s_t"], w["sgu_w_out"], tms, f"sgu_bwd_{i}")
            W = gated.shape[1]
            g["sgu_w_out"] = _mm_tn(gated, dyb, min(512, W), D, f"sgu_dwout_{i}")
            g["sgu_w_in"] = _mm_tn(sv["h"], dpre, D, min(512, 2 * W), f"sgu_dwin_{i}")
            sgrads["sgu_ln_g"], sgrads["sgu_ln_b"] = st_m[4:5], st_m[5:6]
            sgrads["sgu_w_s"], sgrads["sgu_b_s"] = dws, dbs[:, :, 0]
        else:
            dyb, do, st_o = _mla_bwd_o(dx1, sv["y"], mod[i], w["mla_w_o"], tm, f"mla_bwd_o_{i}")
            KO = do.shape[1]
            g["mla_w_o"] = _mm_tn(sv["o"], dyb, min(512, KO), D, f"mla_dwo_{i}")
            dq, dkv, dkr = _attn_bwd(sv["q"], sv["k"], sv["v"], sv["o"], do, sv["lse"], cos_t, sin_t, tq, f"attn_bwd_{i}")
            dx, dlat, st_m, dqg, dkvg = _mla_bwd_lat(
                dq, dkv, dkr, sv["lat"], sv["x"], dx1, mod[i], gmix[i], small["mla_q_norm_g"], small["mla_kv_norm_g"],
                w["mla_wq"], w["mla_w_ukv"], w["mla_wd"], cos_t, sin_t, tm, f"mla_bwd_lat_{i}")
            HW = dq.shape[1]
            g["mla_wq"] = _mm_tn(sv["cq"], dq, MLA_Q_LORA, min(1024, HW), f"mla_dwq_{i}")
            g["mla_w_ukv"] = _mm_tn(sv["ckv"], dkv, MLA_KV_LORA, min(1024, HW), f"mla_dwukv_{i}")
            g["mla_wd"] = _mm_tn(sv["h"], dlat, D, dlat.shape[1], f"mla_dwd_{i}")
            st_m = jnp.concatenate([st_m[0:2], st_o[2:3], st_m[3:]], axis=0)
            sgrads["mla_q_norm_g"], sgrads["mla_kv_norm_g"] = dqg[0:1], dkvg[0:1]
        stats[i] = jnp.concatenate([st_m[0:3], st_b[3:5], st_a[5:6], st_m[3:4], st_b[6:7]], axis=0)
        grads[i] = g
    return loss_acc, dx, stats, grads, sgrads


HBM_SPEC = pl.BlockSpec(memory_space=pltpu.HBM)
VMEM_SPEC = pl.BlockSpec(memory_space=pltpu.VMEM)


def _my_place():
    return lax.axis_index("x"), lax.axis_index("y"), lax.axis_index("c")


def _flip(v, bit):
    return 1 - v if bit else v


def _small_all_gather(v, name):
    R, C = v.shape

    def body(x_ref, out_ref, send_sems, recv_sems):
        x, y, c = _my_place()
        me = 4 * x + 2 * y + c
        out_ref[me] = x_ref[...]
        sends = []
        for k in range(1, NDEV):
            peer = (_flip(x, k & 4), _flip(y, k & 2), _flip(c, k & 1))
            cp = pltpu.make_async_remote_copy(src_ref=x_ref, dst_ref=out_ref.at[me], send_sem=send_sems.at[k - 1],
                                              recv_sem=recv_sems.at[k - 1], device_id=peer, device_id_type=MESH)
            cp.start()
            sends.append(cp)
        for k in range(1, NDEV):
            src = 4 * _flip(x, k & 4) + 2 * _flip(y, k & 2) + _flip(c, k & 1)
            pltpu.make_async_remote_copy(src_ref=x_ref, dst_ref=out_ref.at[src], send_sem=send_sems.at[k - 1],
                                         recv_sem=recv_sems.at[k - 1], device_id=(x, y, c), device_id_type=MESH).wait_recv()
        for cp in sends:
            cp.wait_send()

    return pl.pallas_call(
        body, name=name, out_shape=SDS((NDEV, R, C), v.dtype), in_specs=[VMEM_SPEC], out_specs=VMEM_SPEC,
        scratch_shapes=[pltpu.SemaphoreType.DMA((NDEV - 1,)), pltpu.SemaphoreType.DMA((NDEV - 1,))],
        compiler_params=pltpu.CompilerParams(vmem_limit_bytes=V7X_VMEM_LIMIT),
    )(v)


def _slab(ref, axis, width, dev):
    idx = [slice(None)] * len(ref.shape)
    idx[axis] = pl.ds(pl.multiple_of(dev * width, width), width)
    return ref.at[tuple(idx)]


def _all_gather_group(shards, axes, name):
    nt = len(shards)
    out_shapes = [SDS(tuple(s * NDEV if a == ax else s for a, s in enumerate(sh.shape)), sh.dtype)
                  for sh, ax in zip(shards, axes)]

    def body(*refs):
        ins, outs = refs[:nt], refs[nt:2 * nt]
        send_sems, recv_sems, local_sems = refs[2 * nt:]
        x, y, c = _my_place()
        me = 4 * x + 2 * y + c
        sibling = (x, y, 1 - c)
        chips = [(1 - x, y), (x, 1 - y), (1 - x, 1 - y)]

        def block(t, dev):
            return _slab(outs[t], axes[t], ins[t].shape[axes[t]], dev)

        def copy(t, k, dev, to, src=None):
            return pltpu.make_async_remote_copy(
                src_ref=block(t, dev) if src is None else src, dst_ref=block(t, dev), send_sem=send_sems.at[t, k],
                recv_sem=recv_sems.at[t, k], device_id=to, device_id_type=MESH)

        mine = [pltpu.make_async_copy(ins[t], block(t, me), local_sems.at[t]) for t in range(nt)]
        for cp in mine:
            cp.start()
        first = []
        for t in range(nt):
            first.append(copy(t, 0, me, sibling, src=ins[t]))
            first += [copy(t, 1 + j, me, (cx, cy, c), src=ins[t]) for j, (cx, cy) in enumerate(chips)]
        for cp in first:
            cp.start()
        passed = []
        for j, (cx, cy) in enumerate(chips):
            for t in range(nt):
                copy(t, 1 + j, 4 * cx + 2 * cy + c, (x, y, c)).wait_recv()
                cp = copy(t, 4 + j, 4 * cx + 2 * cy + c, sibling)
                cp.start()
                passed.append(cp)
        for t in range(nt):
            copy(t, 0, 4 * x + 2 * y + (1 - c), (x, y, c)).wait_recv()
        for j, (cx, cy) in enumerate(chips):
            for t in range(nt):
                copy(t, 4 + j, 4 * cx + 2 * cy + (1 - c), (x, y, c)).wait_recv()
        for cp in first + passed:
            cp.wait_send()
        for cp in mine:
            cp.wait()

    return pl.pallas_call(
        body, name=name, out_shape=out_shapes, in_specs=[HBM_SPEC] * nt, out_specs=[HBM_SPEC] * nt,
        scratch_shapes=[pltpu.SemaphoreType.DMA((nt, NDEV - 1)), pltpu.SemaphoreType.DMA((nt, NDEV - 1)),
                        pltpu.SemaphoreType.DMA((nt,))],
    )(*shards)


def _reduce_scatter_sibling(grads, axes, name):
    nt = len(grads)
    NCH = NDEV // 2

    def shard_shape(gr, ax):
        return tuple(s // NDEV if a == ax else s for a, s in enumerate(gr.shape))

    out_shapes = ([SDS((NCH,) + shard_shape(gr, ax), gr.dtype) for gr, ax in zip(grads, axes)]
                  + [SDS((NCH,) + shard_shape(gr, ax), gr.dtype) for gr, ax in zip(grads, axes)])

    def body(*refs):
        ins, lands, owns = refs[:nt], refs[nt:2 * nt], refs[2 * nt:3 * nt]
        send_sems, recv_sems, local_sems = refs[3 * nt:]
        x, y, c = _my_place()
        sends, locs = [], []
        for t in range(nt):
            w = ins[t].shape[axes[t]] // NDEV
            for k in range(NCH):
                px, py = k >> 1, k & 1
                cp = pltpu.make_async_remote_copy(
                    src_ref=_slab(ins[t], axes[t], w, 4 * px + 2 * py + (1 - c)), dst_ref=lands[t].at[k],
                    send_sem=send_sems.at[t, k], recv_sem=recv_sems.at[t, k], device_id=(x, y, 1 - c), device_id_type=MESH)
                cp.start()
                sends.append(cp)
                lc = pltpu.make_async_copy(_slab(ins[t], axes[t], w, 4 * px + 2 * py + c), owns[t].at[k], local_sems.at[t, k])
                lc.start()
                locs.append(lc)
        for cp in sends:
            cp.wait_recv()
        for cp in sends:
            cp.wait_send()
        for lc in locs:
            lc.wait()

    res = pl.pallas_call(
        body, name=name, out_shape=out_shapes, in_specs=[HBM_SPEC] * nt, out_specs=[HBM_SPEC] * (2 * nt),
        scratch_shapes=[pltpu.SemaphoreType.DMA((nt, NCH)), pltpu.SemaphoreType.DMA((nt, NCH)),
                        pltpu.SemaphoreType.DMA((nt, NCH))],
    )(*grads)
    return res[:nt], res[nt:]


def _reduce_scatter_chips(parts, name):
    nt = len(parts)
    out_shapes = [SDS((3,) + p.shape[1:], p.dtype) for p in parts]

    def body(*refs):
        ins, outs = refs[:nt], refs[nt:2 * nt]
        send_sems, recv_sems = refs[2 * nt:]
        x, y, c = _my_place()
        sends = []
        for t in range(nt):
            for m in range(1, 4):
                px, py = _flip(x, m & 2), _flip(y, m & 1)
                cp = pltpu.make_async_remote_copy(
                    src_ref=ins[t].at[2 * px + py], dst_ref=outs[t].at[m - 1], send_sem=send_sems.at[t, m - 1],
                    recv_sem=recv_sems.at[t, m - 1], device_id=(px, py, c), device_id_type=MESH)
                cp.start()
                sends.append(cp)
        for cp in sends:
            cp.wait_recv()
        for cp in sends:
            cp.wait_send()

    return pl.pallas_call(
        body, name=name, out_shape=out_shapes, in_specs=[HBM_SPEC] * nt, out_specs=[HBM_SPEC] * nt,
        scratch_shapes=[pltpu.SemaphoreType.DMA((nt, 3)), pltpu.SemaphoreType.DMA((nt, 3))],
    )(*parts)


def _row_tile(R, C, itemsize=4, target=1 << 20):
    best = R
    for tr in range(8, R, 8):
        if R % tr == 0 and tr * C * itemsize <= target:
            best = tr
    return best if best * C * itemsize <= target or best == R else R


def _as2d(a):
    return a.reshape(-1, a.shape[-1])


def _add_pairs(a, b, name):
    shp = a.shape
    a2, b2 = _as2d(a), _as2d(b)
    R, C = a2.shape
    tr = _row_tile(R, C, 2)

    def body(a_ref, b_ref, o_ref):
        o_ref[...] = (a_ref[...].astype(F32) + b_ref[...].astype(F32)).astype(o_ref.dtype)

    out = pl.pallas_call(
        body, name=name, grid=(R // tr,), in_specs=[_rows1(tr, C), _rows1(tr, C)], out_specs=_rows1(tr, C),
        out_shape=SDS((R, C), a.dtype), compiler_params=_params(("parallel",)),
    )(a2, b2)
    return out.reshape(shp)


def _adamw_math(g, w, m, v):
    m2 = ADAM_B1 * m + (1.0 - ADAM_B1) * g
    v2 = ADAM_B2 * v + (1.0 - ADAM_B2) * (g * g)
    m_hat = m2 / (1.0 - ADAM_B1 ** ADAM_STEP)
    v_hat = v2 / (1.0 - ADAM_B2 ** ADAM_STEP)
    delta = -ADAM_LR * (m_hat / (jnp.sqrt(v_hat) + ADAM_EPS) + ADAM_WD * w)
    return delta, m2, v2


def _adamw(parts, w, m, v, name):
    shp = w.shape
    w2, m2, v2 = _as2d(w), _as2d(m), _as2d(v)
    R, C = w2.shape
    tr = _row_tile(R, C)
    p3 = [p.reshape((-1, R, C)) for p in parts]
    npart = len(p3)

    def body(*refs):
        prefs = refs[:npart]
        w_ref, m_ref, v_ref, g_ref, d_ref, nm_ref, nv_ref = refs[npart:]
        g = None
        for pr in prefs:
            for k in range(pr.shape[0]):
                term = pr[k].astype(F32)
                g = term if g is None else g + term
        g_ref[...] = g
        d_ref[...], nm_ref[...], nv_ref[...] = _adamw_math(g, w_ref[...], m_ref[...], v_ref[...])

    outs = pl.pallas_call(
        body, name=name, grid=(R // tr,),
        in_specs=[pl.BlockSpec((p.shape[0], tr, C), lambda i: (0, i, 0)) for p in p3] + [_rows1(tr, C)] * 3,
        out_specs=[_rows1(tr, C)] * 4, out_shape=[SDS((R, C), F32)] * 4,
        compiler_params=_params(("parallel",)),
    )(*p3, w2, m2, v2)
    return [o.reshape(shp) for o in outs]


def _ada_fwd(c_all, ada_w, ada_b_mine, name):
    L, D, Wc = ada_w.shape

    def body(c_ref, w_ref, b_ref, o_ref):
        cv = c_ref[...]
        act = cv * (1.0 / (1.0 + jnp.exp(-cv)))
        o_ref[0] = jnp.dot(act, w_ref[0], preferred_element_type=F32, precision=lax.Precision.HIGHEST) + b_ref[0]

    return pl.pallas_call(
        body, name=name, grid=(L,),
        in_specs=[_resident(c_all.shape), pl.BlockSpec((1, D, Wc), lambda l: (l, 0, 0)), pl.BlockSpec((1, 1, Wc), lambda l: (l, 0, 0))],
        out_specs=pl.BlockSpec((1, NDEV, Wc), lambda l: (l, 0, 0)), out_shape=SDS((L, NDEV, Wc), F32),
        compiler_params=_params(("parallel",)),
    )(c_all, ada_w, ada_b_mine.reshape(L, 1, Wc))


def _ada_bwd(c_all, dmod_mine, name):
    L, _, Wc = dmod_mine.shape
    D = c_all.shape[1]

    def body(c_ref, d_ref, o_ref):
        cv = c_ref[...]
        act = cv * (1.0 / (1.0 + jnp.exp(-cv)))
        o_ref[0] = lax.dot_general(act, d_ref[0], (((0,), (0,)), ((), ())), preferred_element_type=F32,
                                   precision=lax.Precision.HIGHEST)

    return pl.pallas_call(
        body, name=name, grid=(L,),
        in_specs=[_resident(c_all.shape), pl.BlockSpec((1, NDEV, Wc), lambda l: (l, 0, 0))],
        out_specs=pl.BlockSpec((1, D, Wc), lambda l: (l, 0, 0)), out_shape=SDS((L, D, Wc), F32),
        compiler_params=_params(("parallel",)),
    )(c_all, dmod_mine)


WEIGHT_NAMES = ['ada_w', 'ada_b', 'norm_mix_g', 'norm_mlp_g', 'pool_w', 'pool_scale', 'sgu_w_in', 'sgu_ln_g', 'sgu_ln_b',
                'sgu_w_s', 'sgu_b_s', 'sgu_w_out', 'mla_w_dq_dkv', 'mla_q_norm_g', 'mla_kv_norm_g', 'mla_w_uq', 'mla_w_ukv',
                'mla_w_o', 'mlp_w1', 'mlp_w2', 'final_g']
REPLICATED = ['ada_b', 'norm_mix_g', 'norm_mlp_g', 'sgu_ln_g', 'sgu_ln_b', 'sgu_w_s', 'sgu_b_s', 'mla_kv_norm_g', 'final_g']
PACK_ROWS = 64
Q_HEAD = MLA_NOPE + MLA_ROPE


def _layer_matrices(i):
    kind, j = i % 3, i // 3
    if kind == 0:
        mats = [("pool_w", j, 1)]
    elif kind == 1:
        mats = [("sgu_w_in", j, 1), ("sgu_w_out", j, 0)]
    else:
        mats = [("mla_w_dq_dkv", j, 0), ("mla_w_uq", j, 1), ("mla_w_ukv", j, 1), ("mla_w_o", j, 0)]
    return mats + [("mlp_w1", i, 1), ("mlp_w2", i, 0)]


def _pack(arrays):
    flat = jnp.concatenate([a.reshape(-1).astype(F32) for a in arrays])
    rows = -(-flat.size // (LANES * PACK_ROWS)) * PACK_ROWS
    return jnp.pad(flat, (0, rows * LANES - flat.size)).reshape(rows, LANES)


def kernel(x, c, positions, ada_w, ada_b, norm_mix_g, norm_mlp_g, pool_w, pool_scale, sgu_w_in, sgu_ln_g, sgu_ln_b, sgu_w_s, sgu_b_s, sgu_w_out, mla_w_dq_dkv, mla_q_norm_g, mla_kv_norm_g, mla_w_uq, mla_w_ukv, mla_w_o, mlp_w1, mlp_w2, final_g, loss_target, m_ada_w, m_ada_b, m_norm_mix_g, m_norm_mlp_g, m_pool_w, m_pool_scale, m_sgu_w_in, m_sgu_ln_g, m_sgu_ln_b, m_sgu_w_s, m_sgu_b_s, m_sgu_w_out, m_mla_w_dq_dkv, m_mla_q_norm_g, m_mla_kv_norm_g, m_mla_w_uq, m_mla_w_ukv, m_mla_w_o, m_mlp_w1, m_mlp_w2, m_final_g, v_ada_w, v_ada_b, v_norm_mix_g, v_norm_mlp_g, v_pool_w, v_pool_scale, v_sgu_w_in, v_sgu_ln_g, v_sgu_ln_b, v_sgu_w_s, v_sgu_b_s, v_sgu_w_out, v_mla_w_dq_dkv, v_mla_q_norm_g, v_mla_kv_norm_g, v_mla_w_uq, v_mla_w_ukv, v_mla_w_o, v_mlp_w1, v_mlp_w2, v_final_g):
    a = dict(locals())
    S, D = x.shape[1], x.shape[2]
    L = ada_w.shape[0]
    Wc = ada_w.shape[2]
    me = 4 * lax.axis_index("x") + 2 * lax.axis_index("y") + lax.axis_index("c")
    my_chip = 2 * lax.axis_index("x") + lax.axis_index("y")

    v0 = _pack([c, pool_scale, mla_q_norm_g])
    g0 = _small_all_gather(v0, "gather_c").reshape(NDEV, -1)
    n_ps, n_qg = pool_scale.size, mla_q_norm_g.size
    c_all = g0[:, :D]
    ps_w = pool_scale.shape[1]
    ps_full = g0[:, D:D + n_ps].reshape(NDEV, -1, ps_w).transpose(1, 0, 2).reshape(-1, 1, D)
    qg_full = g0[:, D + n_ps:D + n_ps + n_qg].reshape(1, -1)

    ada_b_mine = lax.dynamic_slice_in_dim(ada_b, me * Wc, Wc, axis=1)
    modp = _ada_fwd(c_all, ada_w, ada_b_mine, "ada_fwd")
    ga = _small_all_gather(modp.reshape(-1, LANES), "gather_mod").reshape(NDEV, L, NDEV, Wc)
    mod = lax.dynamic_index_in_dim(ga, me, axis=2, keepdims=False).transpose(1, 0, 2).reshape(L, 6, D)
    mod8 = jnp.pad(mod, ((0, 0), (0, 2), (0, 0)))

    wts = []
    for i in range(L):
        mats = _layer_matrices(i)
        full = _all_gather_group([a[n][j].astype(BF16) for n, j, _ in mats], [ax for _, _, ax in mats], f"gather_w_{i}")
        w = {n: f for (n, _, _), f in zip(mats, full)}
        if "mla_w_uq" in w:
            lat_w = w["mla_w_dq_dkv"].shape[1]
            w["mla_wd"] = jnp.pad(w.pop("mla_w_dq_dkv"), ((0, 0), (0, -lat_w % LANES)))
            w["mla_wq"] = _pad_heads(w.pop("mla_w_uq"), Q_HEAD)
        wts.append(w)

    small = {"pool_scale": ps_full, "sgu_ln_g": sgu_ln_g, "sgu_ln_b": sgu_ln_b, "sgu_w_s": sgu_w_s[0],
             "sgu_b_s_t": sgu_b_s[0].T, "mla_q_norm_g": qg_full, "mla_kv_norm_g": mla_kv_norm_g}
    loss_acc, dx, stats, grads, sgrads = _local_step(
        x[0], loss_target[0], positions, mod8, norm_mix_g.reshape(L, 1, D), norm_mlp_g.reshape(L, 1, D),
        final_g.reshape(1, D), wts, small)

    res = {n: [None] * a[n].shape[0] for n in WEIGHT_NAMES if a[n].ndim > 1}
    for i in reversed(range(L)):
        mats = _layer_matrices(i)
        g = dict(grads[i])
        if "mla_wq" in g:
            g["mla_w_dq_dkv"] = g.pop("mla_wd")[:, :mla_w_dq_dkv.shape[2]]
            g["mla_w_uq"] = _unpad_heads(g.pop("mla_wq"), Q_HEAD)
        gl = [g[n].astype(BF16) for n, _, _ in mats]
        lands, owns = _reduce_scatter_sibling(gl, [ax for _, _, ax in mats], f"rs_sibling_{i}")
        parts = [_add_pairs(o, l, f"rs_add_{i}_{n}") for o, l, (n, _, _) in zip(owns, lands, mats)]
        recv = _reduce_scatter_chips(parts, f"rs_chips_{i}")
        for (n, j, _), p, r in zip(mats, parts, recv):
            p_own = lax.dynamic_index_in_dim(p, my_chip, axis=0, keepdims=False)
            res[n][j] = _adamw([p_own, r], a[n][j], a["m_" + n][j], a["v_" + n][j], f"adamw_{n}_{j}")

    sg = {"ada_b": jnp.stack([s[0:6] for s in stats]), "norm_mix_g": jnp.stack([s[6] for s in stats]),
          "norm_mlp_g": jnp.stack([s[7] for s in stats]), "final_g": loss_acc[0], "sgu_ln_g": sgrads["sgu_ln_g"],
          "sgu_ln_b": sgrads["sgu_ln_b"], "sgu_w_s": sgrads["sgu_w_s"], "sgu_b_s": sgrads["sgu_b_s"],
          "mla_kv_norm_g": sgrads["mla_kv_norm_g"]}
    ps_grad = jnp.concatenate([sgrads[f"pool_scale_{j}"] for j in range(pool_scale.shape[0])])
    tail = [ps_grad, sgrads["mla_q_norm_g"], loss_acc[2, :LANES]]
    packed = _pack([sg[n] for n in REPLICATED] + tail)
    gathered = _small_all_gather(packed, "gather_small")
    zeros_tail = [jnp.zeros_like(t) for t in tail]
    g_p, d_p, m_p, v_p = _adamw([gathered], _pack([a[n] for n in REPLICATED] + zeros_tail),
                                _pack([a["m_" + n] for n in REPLICATED] + zeros_tail),
                                _pack([a["v_" + n] for n in REPLICATED] + zeros_tail), "adamw_replicated")
    flat = [t.reshape(-1) for t in (g_p, d_p, m_p, v_p)]
    off = 0
    for n in REPLICATED:
        res[n] = [f[off:off + a[n].size].reshape(a[n].shape) for f in flat]
        off += a[n].size
    g_ps = flat[0][off:off + ps_grad.size].reshape(ps_grad.shape)
    off += ps_grad.size
    g_qg = flat[0][off:off + qg_full.size].reshape(1, -1)
    off += qg_full.size
    loss = flat[0][off]
    res["pool_scale"] = _adamw([lax.dynamic_slice_in_dim(g_ps, me * ps_w, ps_w, axis=1)], pool_scale, m_pool_scale,
                               v_pool_scale, "adamw_pool_scale")
    qg_w = mla_q_norm_g.shape[1]
    res["mla_q_norm_g"] = _adamw([lax.dynamic_slice_in_dim(g_qg, me * qg_w, qg_w, axis=1)], mla_q_norm_g, m_mla_q_norm_g,
                                 v_mla_q_norm_g, "adamw_q_norm_g")

    n_mod = L * 6 * D
    dmod_all = gathered.reshape(NDEV, -1)[:, :n_mod].reshape(NDEV, L, 6 * D)
    dmod_mine = lax.dynamic_slice_in_dim(dmod_all, me * Wc, Wc, axis=2).transpose(1, 0, 2)
    res["ada_w"] = _adamw([_ada_bwd(c_all, dmod_mine, "ada_bwd")], ada_w, m_ada_w, v_ada_w, "adamw_ada_w")

    outs = []
    for k in range(4):
        for n in WEIGHT_NAMES:
            r = res[n]
            outs.append(jnp.stack([lay[k] for lay in r]) if isinstance(r[0], list) else r[k])
    return (loss, dx.reshape(x.shape), *outs)
```

```python
import functools
import math

import jax
import jax.numpy as jnp
import numpy as np
from jax import lax
from jax.experimental import pallas as pl
from jax.experimental.pallas import tpu as pltpu

F32 = jnp.float32
BF16 = jnp.bfloat16
SDS = jax.ShapeDtypeStruct
MESH = pl.DeviceIdType.MESH

NDEV = 8
V7X_VMEM_LIMIT = 56 << 20
LANES = 128
RMS_EPS = 1e-6
LN_EPS = 1e-5
POOL_WINDOWS = (2, 4, 8, 16)
HALO = 16
SGU_CHUNK = 128
SGU_HEAD = 128
MLA_NOPE, MLA_ROPE, MLA_V = 128, 64, 128
MLA_Q_LORA, MLA_KV_LORA = 256, 128
MLA_HEAD_PAD = 256
ROPE_THETA = 10000.0
SM_SCALE = (MLA_NOPE + MLA_ROPE) ** -0.5
NEG = -1e30
ADAM_LR, ADAM_B1, ADAM_B2, ADAM_EPS, ADAM_WD, ADAM_STEP = 0.001, 0.9, 0.999, 1e-08, 0.01, 10
INV_SQRT2 = 1.0 / math.sqrt(2.0)
INV_SQRT_2PI = 1.0 / math.sqrt(2.0 * math.pi)
SH1, SC1, G1, SH2, SC2, G2 = 0, 1, 2, 3, 4, 5


def _params(sem=None, vmem=V7X_VMEM_LIMIT):
    return pltpu.CompilerParams(dimension_semantics=sem, vmem_limit_bytes=vmem)


def _resident(shape):
    nd = len(shape)
    return pl.BlockSpec(shape, lambda *_: (0,) * nd, pipeline_mode=pl.Buffered(1))


def _rows1(tm, w):
    return pl.BlockSpec((tm, w), lambda i: (i, 0))


def _rms(x):
    r = lax.rsqrt(jnp.mean(x * x, axis=-1, keepdims=True) + RMS_EPS)
    return x * r, r


def _colsum(v):
    return jnp.sum(v, axis=0, keepdims=True)


def _normmod_bwd(dh, n, r, a):
    dn = dh * a
    return r * (dn - n * jnp.mean(dn * n, axis=-1, keepdims=True))


def _dot(a, b):
    return jnp.dot(a, b, preferred_element_type=F32)


def _dot_nt(a, b):
    return lax.dot_general(a, b, (((1,), (1,)), ((), ())), preferred_element_type=F32)


def _dot_tn(a, b):
    return lax.dot_general(a, b, (((0,), (0,)), ((), ())), preferred_element_type=F32)


def _gelu(x):
    return 0.5 * x * (1.0 + lax.erf(x * INV_SQRT2))


def _gelu_grad(x):
    return 0.5 * (1.0 + lax.erf(x * INV_SQRT2)) + x * jnp.exp(-0.5 * x * x) * INV_SQRT_2PI


def _swap_halves(v):
    lane = lax.broadcasted_iota(jnp.int32, v.shape, 1)
    half = MLA_ROPE // 2
    return jnp.where(lane < half, pltpu.roll(v, LANES - half, 1),
                     jnp.where(lane < MLA_ROPE, pltpu.roll(v, half, 1), 0.0))


def _mlp_up(x1, mod, gn, w1, tm, tn, name):
    S, D = x1.shape
    Fh = w1.shape[1]

    def body(x_ref, mod_ref, gn_ref, w_ref, h_ref, r_ref):
        n, _ = _rms(x_ref[...])
        a = gn_ref[...] * (1.0 + mod_ref[SC2:SC2 + 1, :])
        h = (n * a + mod_ref[SH2:SH2 + 1, :]).astype(BF16)
        h_ref[...] = h
        for j in range(Fh // tn):
            cols = slice(j * tn, (j + 1) * tn)
            r_ref[:, cols] = jnp.maximum(_dot(h, w_ref[:, cols]), 0.0).astype(BF16)

    return pl.pallas_call(
        body, name=name, grid=(S // tm,),
        in_specs=[_rows1(tm, D), _resident(mod.shape), _resident(gn.shape), _resident(w1.shape)],
        out_specs=[_rows1(tm, D), _rows1(tm, Fh)],
        out_shape=[SDS((S, D), BF16), SDS((S, Fh), BF16)],
        compiler_params=_params(("parallel",)),
    )(x1, mod, gn, w1)


def _mlp_down(r, w2, x1, mod, tm, name):
    S, Fh = r.shape
    D = w2.shape[1]

    def body(r_ref, w_ref, x_ref, mod_ref, x2_ref, o_ref):
        rv = r_ref[...]
        o = _dot(rv * rv, w_ref[...])
        o_ref[...] = o.astype(BF16)
        x2_ref[...] = x_ref[...] + mod_ref[G2:G2 + 1, :] * o

    return pl.pallas_call(
        body, name=name, grid=(S // tm,),
        in_specs=[_rows1(tm, Fh), _resident(w2.shape), _rows1(tm, D), _resident(mod.shape)],
        out_specs=[_rows1(tm, D), _rows1(tm, D)],
        out_shape=[SDS((S, D), F32), SDS((S, D), BF16)],
        compiler_params=_params(("parallel",)),
    )(r, w2, x1, mod)


def _mlp_bwd_a(dx2, o, mod, w2, r, tm, tn, name):
    S, D = dx2.shape
    Fh = r.shape[1]

    def body(dx_ref, o_ref, mod_ref, w_ref, r_ref, da_ref, do_ref, st_ref):
        @pl.when(pl.program_id(0) == 0)
        def _():
            st_ref[...] = jnp.zeros_like(st_ref)

        dx = dx_ref[...]
        d_o = (dx * mod_ref[G2:G2 + 1, :]).astype(BF16)
        do_ref[...] = d_o
        st_ref[G2:G2 + 1, :] += _colsum(dx * o_ref[...].astype(F32))
        for j in range(Fh // tn):
            cols = slice(j * tn, (j + 1) * tn)
            dz = _dot_nt(d_o, w_ref[cols, :])
            da_ref[:, cols] = (dz * (2.0 * r_ref[:, cols].astype(F32))).astype(BF16)

    return pl.pallas_call(
        body, name=name, grid=(S // tm,),
        in_specs=[_rows1(tm, D), _rows1(tm, D), _resident(mod.shape), _resident(w2.shape), _rows1(tm, Fh)],
        out_specs=[_rows1(tm, Fh), _rows1(tm, D), pl.BlockSpec((8, D), lambda i: (0, 0))],
        out_shape=[SDS((S, Fh), BF16), SDS((S, D), BF16), SDS((8, D), F32)],
        compiler_params=_params(("arbitrary",)),
    )(dx2, o, mod, w2, r)


def _mlp_bwd_b(d_a, w1, x1, dx2, mod, gn, tm, name):
    S, Fh = d_a.shape
    D = w1.shape[0]

    def body(da_ref, w_ref, x_ref, dx_ref, mod_ref, gn_ref, dx1_ref, st_ref):
        @pl.when(pl.program_id(0) == 0)
        def _():
            st_ref[...] = jnp.zeros_like(st_ref)

        dh = _dot_nt(da_ref[...], w_ref[...])
        n, rr = _rms(x_ref[...])
        gn_v = gn_ref[...]
        sc1p = 1.0 + mod_ref[SC2:SC2 + 1, :]
        t = _colsum(dh * n)
        st_ref[SH2:SH2 + 1, :] += _colsum(dh)
        st_ref[SC2:SC2 + 1, :] += t * gn_v
        st_ref[6:7, :] += t * sc1p
        dx1_ref[...] = dx_ref[...] + _normmod_bwd(dh, n, rr, gn_v * sc1p)

    return pl.pallas_call(
        body, name=name, grid=(S // tm,),
        in_specs=[_rows1(tm, Fh), _resident(w1.shape), _rows1(tm, D), _rows1(tm, D), _resident(mod.shape),
                  _resident(gn.shape)],
        out_specs=[_rows1(tm, D), pl.BlockSpec((8, D), lambda i: (0, 0))],
        out_shape=[SDS((S, D), F32), SDS((8, D), F32)],
        compiler_params=_params(("arbitrary",)),
    )(d_a, w1, x1, dx2, mod, gn)


def _mm_tn(a, g, tk, tn, name, square_a=False, col_shards=False):
    S, K1 = a.shape
    N = g.shape[1]
    w = N // NDEV
    per = tn // w if col_shards else 1

    def body(a_ref, g_ref, o_ref):
        av = a_ref[...]
        if square_a:
            av = av * av
        res = _dot_tn(av, g_ref[...]).astype(BF16)
        if col_shards:
            for s in range(per):
                o_ref[s] = res[:, s * w:(s + 1) * w]
        else:
            o_ref[...] = res

    if col_shards:
        out_spec, out_shape = pl.BlockSpec((per, tk, w), lambda i, j: (j, i, 0)), SDS((NDEV, K1, w), BF16)
    else:
        out_spec, out_shape = pl.BlockSpec((tk, tn), lambda i, j: (i, j)), SDS((K1, N), BF16)
    return pl.pallas_call(
        body, name=name, grid=(K1 // tk, N // tn),
        in_specs=[pl.BlockSpec((S, tk), lambda i, j: (0, i)), pl.BlockSpec((S, tn), lambda i, j: (0, j))],
        out_specs=out_spec, out_shape=out_shape,
        compiler_params=_params(("parallel", "parallel")),
    )(a, g)


def _pool_h_ext(x_ref, xp_ref, mod_ref, gn_ref, i, tm):
    ext = jnp.concatenate([xp_ref[...], x_ref[...]], axis=0)
    n, r = _rms(ext)
    a = gn_ref[...] * (1.0 + mod_ref[SC1:SC1 + 1, :])
    h = n * a + mod_ref[SH1:SH1 + 1, :]
    row = lax.broadcasted_iota(jnp.int32, (tm + HALO, 1), 0)
    h = jnp.where(jnp.logical_and(i == 0, row < HALO), 0.0, h)
    return h, n[HALO:], r[HALO:], a


def _trailing_sum(v, win):
    k = 1
    while k < win:
        v = v + pltpu.roll(v, k, 0)
        k *= 2
    return v


def _leading_sum(v, win):
    k = 1
    while k < win:
        v = v + pltpu.roll(v, v.shape[0] - k, 0)
        k *= 2
    return v


def _pool_fwd(x, mod, gn, pw, ps, tm, name):
    S, D = x.shape
    C = D // len(POOL_WINDOWS)
    hb = tm // HALO

    def body(x_ref, xp_ref, mod_ref, gn_ref, pw_ref, ps_ref, x1_ref):
        i = pl.program_id(0)
        h, _, _, _ = _pool_h_ext(x_ref, xp_ref, mod_ref, gn_ref, i, tm)
        t1 = (i * tm + lax.broadcasted_iota(jnp.int32, (tm, 1), 0)).astype(F32) + 1.0
        for g, win in enumerate(POOL_WINDOWS):
            cols = slice(g * C, (g + 1) * C)
            hg = h[:, cols]
            inv = 1.0 / jnp.minimum(t1, float(win))
            pooled = (_trailing_sum(hg, win)[HALO:] * inv - hg[HALO:]).astype(BF16)
            y = _dot(pooled, pw_ref[g]) * ps_ref[:, cols]
            x1_ref[:, cols] = x_ref[:, cols] + mod_ref[G1:G1 + 1, cols] * y

    return pl.pallas_call(
        body, name=name, grid=(S // tm,),
        in_specs=[_rows1(tm, D), pl.BlockSpec((HALO, D), lambda i: (jnp.maximum(i * hb - 1, 0), 0)),
                  _resident(mod.shape), _resident(gn.shape), _resident(pw.shape), _resident(ps.shape)],
        out_specs=_rows1(tm, D),
        out_shape=SDS((S, D), F32),
        compiler_params=_params(("parallel",)),
    )(x, x, mod, gn, pw, ps)


def _pool_bwd(x, dx1, mod, gn, pw, ps, tm, name):
    S, D = x.shape
    G = len(POOL_WINDOWS)
    C = D // G
    hb = tm // HALO
    nt = S // tm

    def body(x_ref, xp_ref, d1_ref, dn_ref, mod_ref, gn_ref, pw_ref, ps_ref, dx_ref, st_ref, dpw_ref):
        i = pl.program_id(0)

        @pl.when(i == 0)
        def _():
            st_ref[...] = jnp.zeros_like(st_ref)
            dpw_ref[...] = jnp.zeros_like(dpw_ref)

        h, n, rr, a = _pool_h_ext(x_ref, xp_ref, mod_ref, gn_ref, i, tm)
        g1 = mod_ref[G1:G1 + 1, :]
        ps_v = ps_ref[...]
        d1 = d1_ref[...]
        d1n = jnp.where(i == nt - 1, 0.0, dn_ref[...])
        dyr = (jnp.concatenate([d1, d1n], axis=0) * (g1 * ps_v)).astype(BF16)
        t1 = (i * tm + lax.broadcasted_iota(jnp.int32, (tm + HALO, 1), 0)).astype(F32) + 1.0
        parts = []
        for g, win in enumerate(POOL_WINDOWS):
            cols = slice(g * C, (g + 1) * C)
            hg = h[:, cols]
            inv = 1.0 / jnp.minimum(t1, float(win))
            pooled = (_trailing_sum(hg, win)[HALO:] * inv[:tm] - hg[HALO:]).astype(BF16)
            yraw = _dot(pooled, pw_ref[g])
            st_ref[G1:G1 + 1, cols] += _colsum(d1[:, cols] * (yraw * ps_v[:, cols]))
            st_ref[4:5, cols] += _colsum(d1[:, cols] * g1[:, cols] * yraw)
            dpw_ref[g] += _dot_tn(pooled, dyr[:tm, cols])
            dpool = _dot_nt(dyr[:, cols], pw_ref[g])
            parts.append(_leading_sum(dpool * inv, win)[:tm] - dpool[:tm])
        dh = jnp.concatenate(parts, axis=1)
        t = _colsum(dh * n)
        st_ref[SH1:SH1 + 1, :] += _colsum(dh)
        st_ref[SC1:SC1 + 1, :] += t * gn_ref[...]
        st_ref[3:4, :] += t * (1.0 + mod_ref[SC1:SC1 + 1, :])
        dx_ref[...] = d1 + _normmod_bwd(dh, n, rr, a)

    return pl.pallas_call(
        body, name=name, grid=(nt,),
        in_specs=[_rows1(tm, D), pl.BlockSpec((HALO, D), lambda i: (jnp.maximum(i * hb - 1, 0), 0)),
                  _rows1(tm, D), pl.BlockSpec((HALO, D), lambda i: (jnp.minimum((i + 1) * hb, S // HALO - 1), 0)),
                  _resident(mod.shape), _resident(gn.shape), _resident(pw.shape), _resident(ps.shape)],
        out_specs=[_rows1(tm, D), pl.BlockSpec((8, D), lambda i: (0, 0)), pl.BlockSpec((G, C, C), lambda i: (0, 0, 0))],
        out_shape=[SDS((S, D), F32), SDS((8, D), F32), SDS((G, C, C), F32)],
        compiler_params=_params(("arbitrary",)),
    )(x, x, dx1, dx1, mod, gn, pw, ps)


def _tril_bf16(w):
    row = lax.broadcasted_iota(jnp.int32, w.shape, 0)
    col = lax.broadcasted_iota(jnp.int32, w.shape, 1)
    return jnp.where(col <= row, w, 0.0).astype(BF16)


def _sgu_front(pre, lng_ref, lnb_ref, W):
    z = _gelu(pre)
    u, v = z[:, :W], z[:, W:]
    mu = jnp.mean(v, axis=-1, keepdims=True)
    xc = v - mu
    rstd = lax.rsqrt(jnp.mean(xc * xc, axis=-1, keepdims=True) + LN_EPS)
    vhat = xc * rstd
    return u, vhat, rstd, vhat * lng_ref[...] + lnb_ref[...]


def _sgu_mix(vn, ws_ref, bst_ref, mix_s, tm, W):
    for hd in range(W // SGU_HEAD):
        wm = _tril_bf16(ws_ref[hd])
        bcol = bst_ref[:, hd:hd + 1]
        for ci in range(tm // SGU_CHUNK):
            rs, cs = slice(ci * SGU_CHUNK, (ci + 1) * SGU_CHUNK), slice(hd * SGU_HEAD, (hd + 1) * SGU_HEAD)
            mix_s[rs, cs] = _dot(wm, vn[rs, cs].astype(BF16)) + bcol


def _sgu_fwd(x, mod, gn, w_in, lng, lnb, ws, bst, w_out, tm, name):
    S, D = x.shape
    W = w_out.shape[0]

    def body(x_ref, mod_ref, gn_ref, win_ref, lng_ref, lnb_ref, ws_ref, bst_ref, wout_ref,
             x1_ref, h_ref, pre_ref, y_ref, mix_s):
        n, _ = _rms(x_ref[...])
        a = gn_ref[...] * (1.0 + mod_ref[SC1:SC1 + 1, :])
        h = (n * a + mod_ref[SH1:SH1 + 1, :]).astype(BF16)
        h_ref[...] = h
        pre = _dot(h, win_ref[...])
        pre_ref[...] = pre.astype(BF16)
        u, _, _, vn = _sgu_front(pre, lng_ref, lnb_ref, W)
        _sgu_mix(vn, ws_ref, bst_ref, mix_s, tm, W)
        y = _dot((u * mix_s[...]).astype(BF16), wout_ref[...])
        y_ref[...] = y.astype(BF16)
        x1_ref[...] = x_ref[...] + mod_ref[G1:G1 + 1, :] * y

    return pl.pallas_call(
        body, name=name, grid=(S // tm,),
        in_specs=[_rows1(tm, D), _resident(mod.shape), _resident(gn.shape), _resident(w_in.shape),
                  _resident(lng.shape), _resident(lnb.shape), _resident(ws.shape), _resident(bst.shape),
                  _resident(w_out.shape)],
        out_specs=[_rows1(tm, D), _rows1(tm, D), _rows1(tm, 2 * W), _rows1(tm, D)],
        out_shape=[SDS((S, D), F32), SDS((S, D), BF16), SDS((S, 2 * W), BF16), SDS((S, D), BF16)],
        scratch_shapes=[pltpu.VMEM((tm, W), F32)],
        compiler_params=_params(("parallel",)),
    )(x, mod, gn, w_in, lng, lnb, ws, bst, w_out)


def _sgu_bwd(x, dx1, pre, y, mod, gn, w_in, lng, lnb, ws, bst, w_out, tm, name):
    S, D = x.shape
    W = w_out.shape[0]
    H = W // SGU_HEAD
    nt = S // tm

    def body(x_ref, d1_ref, pre_ref, y_ref, mod_ref, gn_ref, win_ref, lng_ref, lnb_ref, ws_ref, bst_ref, wout_ref,
             dx_ref, dy_ref, gt_ref, dpre_ref, st_ref, dws_ref, dbs_ref, mix_s, dvn_s):
        i = pl.program_id(0)

        @pl.when(i == 0)
        def _():
            st_ref[...] = jnp.zeros_like(st_ref)
            dws_ref[...] = jnp.zeros_like(dws_ref)
            dbs_ref[...] = jnp.zeros_like(dbs_ref)

        d1 = d1_ref[...]
        pre = pre_ref[...].astype(F32)
        u, vhat, rstd, vn = _sgu_front(pre, lng_ref, lnb_ref, W)
        _sgu_mix(vn, ws_ref, bst_ref, mix_s, tm, W)
        mixed = mix_s[...]
        gt_ref[...] = (u * mixed).astype(BF16)
        dyb = (d1 * mod_ref[G1:G1 + 1, :]).astype(BF16)
        dy_ref[...] = dyb
        st_ref[G1:G1 + 1, :] += _colsum(d1 * y_ref[...].astype(F32))
        dgt = _dot_nt(dyb, wout_ref[...])
        du = dgt * mixed
        dmix = dgt * u
        for hd in range(H):
            wm = _tril_bf16(ws_ref[hd])
            for ci in range(tm // SGU_CHUNK):
                rs, cs = slice(ci * SGU_CHUNK, (ci + 1) * SGU_CHUNK), slice(hd * SGU_HEAD, (hd + 1) * SGU_HEAD)
                dm = dmix[rs, cs]
                dmb = dm.astype(BF16)
                dbs_ref[hd] += jnp.broadcast_to(jnp.sum(dm, axis=1, keepdims=True), (SGU_CHUNK, LANES))
                dws_ref[hd] += _dot_nt(dmb, vn[rs, cs].astype(BF16))
                dvn_s[rs, cs] = _dot_tn(wm, dmb)
        dvn = dvn_s[...]
        st_ref[4:5, :] += _colsum(dvn * vhat)
        st_ref[5:6, :] += _colsum(dvn)
        dvh = dvn * lng_ref[...]
        dv = rstd * (dvh - jnp.mean(dvh, axis=-1, keepdims=True) - vhat * jnp.mean(dvh * vhat, axis=-1, keepdims=True))
        dpre_u = (du * _gelu_grad(pre[:, :W])).astype(BF16)
        dpre_v = (dv * _gelu_grad(pre[:, W:])).astype(BF16)
        dpre_ref[:, :W] = dpre_u
        dpre_ref[:, W:] = dpre_v
        dh = _dot_nt(dpre_u, win_ref[:, :W]) + _dot_nt(dpre_v, win_ref[:, W:])
        n, rr = _rms(x_ref[...])
        gn_v = gn_ref[...]
        sc1p = 1.0 + mod_ref[SC1:SC1 + 1, :]
        t = _colsum(dh * n)
        st_ref[SH1:SH1 + 1, :] += _colsum(dh)
        st_ref[SC1:SC1 + 1, :] += t * gn_v
        st_ref[3:4, :] += t * sc1p
        dx_ref[...] = d1 + _normmod_bwd(dh, n, rr, gn_v * sc1p)

        @pl.when(i == nt - 1)
        def _():
            for hd in range(H):
                row = lax.broadcasted_iota(jnp.int32, (SGU_CHUNK, SGU_CHUNK), 0)
                col = lax.broadcasted_iota(jnp.int32, (SGU_CHUNK, SGU_CHUNK), 1)
                dws_ref[hd] = jnp.where(col <= row, dws_ref[hd], 0.0)

    return pl.pallas_call(
        body, name=name, grid=(nt,),
        in_specs=[_rows1(tm, D), _rows1(tm, D), _rows1(tm, 2 * W), _rows1(tm, D), _resident(mod.shape),
                  _resident(gn.shape), _resident(w_in.shape), _resident(lng.shape), _resident(lnb.shape),
                  _resident(ws.shape), _resident(bst.shape), _resident(w_out.shape)],
        out_specs=[_rows1(tm, D), _rows1(tm, D), _rows1(tm, W), _rows1(tm, 2 * W),
                   pl.BlockSpec((8, D), lambda i: (0, 0)), pl.BlockSpec((H, SGU_CHUNK, SGU_CHUNK), lambda i: (0, 0, 0)),
                   pl.BlockSpec((H, SGU_CHUNK, LANES), lambda i: (0, 0, 0))],
        out_shape=[SDS((S, D), F32), SDS((S, D), BF16), SDS((S, W), BF16), SDS((S, 2 * W), BF16),
                   SDS((8, D), F32), SDS((H, SGU_CHUNK, SGU_CHUNK), F32), SDS((H, SGU_CHUNK, LANES), F32)],
        scratch_shapes=[pltpu.VMEM((tm, W), F32), pltpu.VMEM((tm, W), F32)],
        compiler_params=_params(("arbitrary",)),
    )(x, dx1, pre, y, mod, gn, w_in, lng, lnb, ws, bst, w_out)


def _mla_lat(x, mod, gn, wd, qg, kvg, cos_t, sin_t, tm, name):
    S, D = x.shape
    LW = wd.shape[1]
    QL, KL = MLA_Q_LORA, MLA_KV_LORA

    def body(x_ref, mod_ref, gn_ref, wd_ref, qg_ref, kvg_ref, c_ref, s_ref, h_ref, lat_ref, cq_ref, ckv_ref, kr_ref):
        n, _ = _rms(x_ref[...])
        a = gn_ref[...] * (1.0 + mod_ref[SC1:SC1 + 1, :])
        h = (n * a + mod_ref[SH1:SH1 + 1, :]).astype(BF16)
        h_ref[...] = h
        lat = _dot(h, wd_ref[...])
        lat_ref[...] = lat
        nq, _ = _rms(lat[:, :QL])
        cq_ref[...] = (nq * qg_ref[...]).astype(BF16)
        nkv, _ = _rms(lat[:, QL:QL + KL])
        ckv_ref[...] = (nkv * kvg_ref[...]).astype(BF16)
        kr = lat[:, QL + KL:]
        kr_ref[...] = (kr * c_ref[...] + _swap_halves(kr) * s_ref[...]).astype(BF16)

    return pl.pallas_call(
        body, name=name, grid=(S // tm,),
        in_specs=[_rows1(tm, D), _resident(mod.shape), _resident(gn.shape), _resident(wd.shape), _resident(qg.shape),
                  _resident(kvg.shape), _rows1(tm, LANES), _rows1(tm, LANES)],
        out_specs=[_rows1(tm, D), _rows1(tm, LW), _rows1(tm, QL), _rows1(tm, KL), _rows1(tm, LANES)],
        out_shape=[SDS((S, D), BF16), SDS((S, LW), F32), SDS((S, QL), BF16), SDS((S, KL), BF16), SDS((S, LANES), BF16)],
        compiler_params=_params(("parallel",)),
    )(x, mod, gn, wd, qg, kvg, cos_t, sin_t)


def _mla_qkv(cq, ckv, krp, wq, wukv, cos_t, sin_t, tm, name):
    S = cq.shape[0]
    H = wq.shape[1] // MLA_HEAD_PAD
    HP = MLA_HEAD_PAD

    def body(cq_ref, ckv_ref, kr_ref, wq_ref, wkv_ref, c_ref, s_ref, q_ref, k_ref, v_ref):
        q = _dot(cq_ref[...], wq_ref[...])
        kv = _dot(ckv_ref[...], wkv_ref[...])
        cv, sv, krv = c_ref[...], s_ref[...], kr_ref[...]
        for h in range(H):
            qr = q[:, h * HP + MLA_NOPE:(h + 1) * HP]
            q_ref[:, h * HP:h * HP + MLA_NOPE] = (q[:, h * HP:h * HP + MLA_NOPE] * SM_SCALE).astype(BF16)
            q_ref[:, h * HP + MLA_NOPE:(h + 1) * HP] = ((qr * cv + _swap_halves(qr) * sv) * SM_SCALE).astype(BF16)
            k_ref[:, h * HP:h * HP + MLA_NOPE] = kv[:, h * HP:h * HP + MLA_NOPE].astype(BF16)
            k_ref[:, h * HP + MLA_NOPE:(h + 1) * HP] = krv
            v_ref[:, h * MLA_V:(h + 1) * MLA_V] = kv[:, h * HP + MLA_NOPE:(h + 1) * HP].astype(BF16)

    return pl.pallas_call(
        body, name=name, grid=(S // tm,),
        in_specs=[_rows1(tm, MLA_Q_LORA), _rows1(tm, MLA_KV_LORA), _rows1(tm, LANES), _resident(wq.shape),
                  _resident(wukv.shape), _rows1(tm, LANES), _rows1(tm, LANES)],
        out_specs=[_rows1(tm, H * HP), _rows1(tm, H * HP), _rows1(tm, H * MLA_V)],
        out_shape=[SDS((S, H * HP), BF16), SDS((S, H * HP), BF16), SDS((S, H * MLA_V), BF16)],
        compiler_params=_params(("parallel",)),
    )(cq, ckv, krp, wq, wukv, cos_t, sin_t)


def _causal_mask(tq):
    row = lax.broadcasted_iota(jnp.int32, (tq, tq), 0)
    col = lax.broadcasted_iota(jnp.int32, (tq, tq), 1)
    return col <= row


def _attn_fwd(q, k, v, tq, name):
    S = q.shape[0]
    HP = MLA_HEAD_PAD
    H = q.shape[1] // HP
    nq = S // tq

    def body(q_ref, k_ref, v_ref, o_ref, lse_ref):
        def q_tile(i, _):
            rows = pl.ds(pl.multiple_of(i * tq, tq), tq)
            qv = q_ref[rows, :]

            def step(j, carry, masked):
                m, l, acc = carry
                krows = pl.ds(pl.multiple_of(j * tq, tq), tq)
                s = _dot_nt(qv, k_ref[krows, :])
                if masked:
                    s = jnp.where(_causal_mask(tq), s, NEG)
                m_new = jnp.maximum(m, jnp.max(s, axis=1, keepdims=True))
                p = jnp.exp(s - m_new)
                alpha = jnp.exp(m - m_new)
                l = alpha * l + jnp.sum(p, axis=1, keepdims=True)
                acc = alpha * acc + _dot(p.astype(BF16), v_ref[krows, :])
                return m_new, l, acc

            init = (jnp.full((tq, 1), NEG, F32), jnp.zeros((tq, 1), F32), jnp.zeros((tq, MLA_V), F32))
            carry = lax.fori_loop(0, i, lambda j, c: step(j, c, False), init)
            m, l, acc = step(i, carry, True)
            o_ref[rows, :] = (acc / l).astype(BF16)
            lse_ref[0, rows, :] = jnp.broadcast_to(m + jnp.log(l), (tq, LANES))
            return 0

        lax.fori_loop(0, nq, q_tile, 0)

    return pl.pallas_call(
        body, name=name, grid=(H,),
        in_specs=[pl.BlockSpec((S, HP), lambda h: (0, h)), pl.BlockSpec((S, HP), lambda h: (0, h)),
                  pl.BlockSpec((S, MLA_V), lambda h: (0, h))],
        out_specs=[pl.BlockSpec((S, MLA_V), lambda h: (0, h)), pl.BlockSpec((1, S, LANES), lambda h: (h, 0, 0))],
        out_shape=[SDS((S, H * MLA_V), BF16), SDS((H, S, LANES), F32)],
        compiler_params=_params(("parallel",)),
    )(q, k, v)


def _attn_bwd(q, k, v, o, do, lse, cos_t, sin_t, tq, name):
    S = q.shape[0]
    HP = MLA_HEAD_PAD
    H = q.shape[1] // HP
    nq = S // tq

    def body(q_ref, k_ref, v_ref, o_ref, do_ref, lse_ref, c_ref, s_ref, dq_ref, dkv_ref, dkr_ref, dq_acc, dl_s):
        @pl.when(pl.program_id(0) == 0)
        def _():
            dkr_ref[...] = jnp.zeros_like(dkr_ref)

        dq_acc[...] = jnp.zeros_like(dq_acc)

        def delta_tile(i, _):
            rows = pl.ds(pl.multiple_of(i * tq, tq), tq)
            d = jnp.sum(do_ref[rows, :].astype(F32) * o_ref[rows, :].astype(F32), axis=1, keepdims=True)
            dl_s[rows, :] = jnp.broadcast_to(d, (tq, LANES))
            return 0

        lax.fori_loop(0, nq, delta_tile, 0)

        def kv_tile(j, _):
            krows = pl.ds(pl.multiple_of(j * tq, tq), tq)
            kv_k = k_ref[krows, :]
            kv_v = v_ref[krows, :]

            def step(i, carry, masked):
                dk, dv = carry
                rows = pl.ds(pl.multiple_of(i * tq, tq), tq)
                qv = q_ref[rows, :]
                dov = do_ref[rows, :]
                s = _dot_nt(qv, kv_k)
                if masked:
                    s = jnp.where(_causal_mask(tq), s, NEG)
                p = jnp.exp(s - lse_ref[0, rows, 0:1])
                dv = dv + _dot_tn(p.astype(BF16), dov)
                dp = _dot_nt(dov, kv_v)
                ds = (p * (dp - dl_s[rows, 0:1])).astype(BF16)
                dk = dk + _dot_tn(ds, qv)
                dq_acc[rows, :] += _dot(ds, kv_k)
                return dk, dv

            carry = step(j, (jnp.zeros((tq, HP), F32), jnp.zeros((tq, MLA_V), F32)), True)
            dk, dv = lax.fori_loop(j + 1, nq, lambda i, c: step(i, c, False), carry)
            dkv_ref[krows, :MLA_NOPE] = dk[:, :MLA_NOPE].astype(BF16)
            dkv_ref[krows, MLA_NOPE:] = dv.astype(BF16)
            dkr_ref[krows, :] += dk[:, MLA_NOPE:]
            return 0

        lax.fori_loop(0, nq, kv_tile, 0)

        def out_tile(i, _):
            rows = pl.ds(pl.multiple_of(i * tq, tq), tq)
            dq = dq_acc[rows, :] * SM_SCALE
            dqr = dq[:, MLA_NOPE:]
            dq_ref[rows, :MLA_NOPE] = dq[:, :MLA_NOPE].astype(BF16)
            dq_ref[rows, MLA_NOPE:] = (dqr * c_ref[rows, :] + _swap_halves(dqr * s_ref[rows, :])).astype(BF16)
            return 0

        lax.fori_loop(0, nq, out_tile, 0)

    return pl.pallas_call(
        body, name=name, grid=(H,),
        in_specs=[pl.BlockSpec((S, HP), lambda h: (0, h)), pl.BlockSpec((S, HP), lambda h: (0, h)),
                  pl.BlockSpec((S, MLA_V), lambda h: (0, h)), pl.BlockSpec((S, MLA_V), lambda h: (0, h)),
                  pl.BlockSpec((S, MLA_V), lambda h: (0, h)), pl.BlockSpec((1, S, LANES), lambda h: (h, 0, 0)),
                  _resident(cos_t.shape), _resident(sin_t.shape)],
        out_specs=[pl.BlockSpec((S, HP), lambda h: (0, h)), pl.BlockSpec((S, HP), lambda h: (0, h)),
                   pl.BlockSpec((S, LANES), lambda h: (0, 0))],
        out_shape=[SDS((S, H * HP), BF16), SDS((S, H * HP), BF16), SDS((S, LANES), F32)],
        scratch_shapes=[pltpu.VMEM((S, HP), F32), pltpu.VMEM((S, LANES), F32)],
        compiler_params=_params(("arbitrary",)),
    )(q, k, v, o, do, lse, cos_t, sin_t)


def _mla_out(o, w_o, x, mod, tm, name):
    S, KO = o.shape
    D = w_o.shape[1]

    def body(o_ref, w_ref, x_ref, mod_ref, x1_ref, y_ref):
        y = _dot(o_ref[...], w_ref[...])
        y_ref[...] = y.astype(BF16)
        x1_ref[...] = x_ref[...] + mod_ref[G1:G1 + 1, :] * y

    return pl.pallas_call(
        body, name=name, grid=(S // tm,),
        in_specs=[_rows1(tm, KO), _resident(w_o.shape), _rows1(tm, D), _resident(mod.shape)],
        out_specs=[_rows1(tm, D), _rows1(tm, D)],
        out_shape=[SDS((S, D), F32), SDS((S, D), BF16)],
        compiler_params=_params(("parallel",)),
    )(o, w_o, x, mod)


def _mla_bwd_o(dx1, y, mod, w_o, tm, name):
    S, D = dx1.shape
    KO = w_o.shape[0]

    def body(d1_ref, y_ref, mod_ref, w_ref, dy_ref, do_ref, st_ref):
        @pl.when(pl.program_id(0) == 0)
        def _():
            st_ref[...] = jnp.zeros_like(st_ref)

        d1 = d1_ref[...]
        dyb = (d1 * mod_ref[G1:G1 + 1, :]).astype(BF16)
        dy_ref[...] = dyb
        st_ref[G1:G1 + 1, :] += _colsum(d1 * y_ref[...].astype(F32))
        do_ref[...] = _dot_nt(dyb, w_ref[...]).astype(BF16)

    return pl.pallas_call(
        body, name=name, grid=(S // tm,),
        in_specs=[_rows1(tm, D), _rows1(tm, D), _resident(mod.shape), _resident(w_o.shape)],
        out_specs=[_rows1(tm, D), _rows1(tm, KO), pl.BlockSpec((8, D), lambda i: (0, 0))],
        out_shape=[SDS((S, D), BF16), SDS((S, KO), BF16), SDS((8, D), F32)],
        compiler_params=_params(("arbitrary",)),
    )(dx1, y, mod, w_o)


def _mla_bwd_lat(dq, dkv, dkr, lat, x, dx1, mod, gn, qg, kvg, wq, wukv, wd, cos_t, sin_t, tm, name):
    S, D = x.shape
    LW = wd.shape[1]
    QL, KL = MLA_Q_LORA, MLA_KV_LORA

    def body(dq_ref, dkv_ref, dkr_ref, lat_ref, x_ref, d1_ref, mod_ref, gn_ref, qg_ref, kvg_ref, wq_ref, wkv_ref, wd_ref,
             c_ref, s_ref, dx_ref, dlat_ref, st_ref, dqg_ref, dkvg_ref):
        @pl.when(pl.program_id(0) == 0)
        def _():
            st_ref[...] = jnp.zeros_like(st_ref)
            dqg_ref[...] = jnp.zeros_like(dqg_ref)
            dkvg_ref[...] = jnp.zeros_like(dkvg_ref)

        lat = lat_ref[...]
        dcq = _dot_nt(dq_ref[...], wq_ref[...])
        nq, rq = _rms(lat[:, :QL])
        dqg_ref[0:1, :] += _colsum(dcq * nq)
        dlat_q = _normmod_bwd(dcq, nq, rq, qg_ref[...]).astype(BF16)
        dckv = _dot_nt(dkv_ref[...], wkv_ref[...])
        nkv, rkv = _rms(lat[:, QL:QL + KL])
        dkvg_ref[0:1, :] += _colsum(dckv * nkv)
        dlat_kv = _normmod_bwd(dckv, nkv, rkv, kvg_ref[...]).astype(BF16)
        dkr = dkr_ref[...]
        dlat_kr = (dkr * c_ref[...] + _swap_halves(dkr * s_ref[...])).astype(BF16)
        dlat_ref[:, :QL] = dlat_q
        dlat_ref[:, QL:QL + KL] = dlat_kv
        dlat_ref[:, QL + KL:] = dlat_kr
        dh = (_dot_nt(dlat_q, wd_ref[:, :QL]) + _dot_nt(dlat_kv, wd_ref[:, QL:QL + KL])
              + _dot_nt(dlat_kr, wd_ref[:, QL + KL:]))
        n, rr = _rms(x_ref[...])
        gn_v = gn_ref[...]
        sc1p = 1.0 + mod_ref[SC1:SC1 + 1, :]
        t = _colsum(dh * n)
        st_ref[SH1:SH1 + 1, :] += _colsum(dh)
        st_ref[SC1:SC1 + 1, :] += t * gn_v
        st_ref[3:4, :] += t * sc1p
        dx_ref[...] = d1_ref[...] + _normmod_bwd(dh, n, rr, gn_v * sc1p)

    HW = wq.shape[1]
    return pl.pallas_call(
        body, name=name, grid=(S // tm,),
        in_specs=[_rows1(tm, HW), _rows1(tm, HW), _rows1(tm, LANES), _rows1(tm, LW), _rows1(tm, D), _rows1(tm, D),
                  _resident(mod.shape), _resident(gn.shape), _resident(qg.shape), _resident(kvg.shape),
                  _resident(wq.shape), _resident(wukv.shape), _resident(wd.shape), _rows1(tm, LANES), _rows1(tm, LANES)],
        out_specs=[_rows1(tm, D), _rows1(tm, LW), pl.BlockSpec((8, D), lambda i: (0, 0)),
                   pl.BlockSpec((8, QL), lambda i: (0, 0)), pl.BlockSpec((8, KL), lambda i: (0, 0))],
        out_shape=[SDS((S, D), F32), SDS((S, LW), BF16), SDS((8, D), F32), SDS((8, QL), F32), SDS((8, KL), F32)],
        compiler_params=_params(("arbitrary",)),
    )(dq, dkv, dkr, lat, x, dx1, mod, gn, qg, kvg, wq, wukv, wd, cos_t, sin_t)


def _loss_head(x, tgt, fg, tm, name):
    S, D = x.shape
    nt = S // tm

    def body(x_ref, t_ref, g_ref, dx_ref, acc_ref):
        i = pl.program_id(0)

        @pl.when(i == 0)
        def _():
            acc_ref[...] = jnp.zeros_like(acc_ref)

        n, rr = _rms(x_ref[...])
        g = g_ref[...]
        err = n * g - t_ref[...]
        acc_ref[1:2, :] += _colsum(err * err) * (0.5 / D)
        dy = err * (1.0 / D)
        acc_ref[0:1, :] += _colsum(dy * n)
        dx_ref[...] = _normmod_bwd(dy, n, rr, g)

        @pl.when(i == nt - 1)
        def _():
            acc_ref[2:3, :] = jnp.broadcast_to(jnp.sum(acc_ref[1:2, :], axis=1, keepdims=True), (1, D))

    return pl.pallas_call(
        body, name=name, grid=(nt,),
        in_specs=[_rows1(tm, D), _rows1(tm, D), _resident(fg.shape)],
        out_specs=[_rows1(tm, D), pl.BlockSpec((8, D), lambda i: (0, 0))],
        out_shape=[SDS((S, D), F32), SDS((8, D), F32)],
        compiler_params=_params(("arbitrary",)),
    )(x, tgt, fg)


def _rope_tables(positions, S):
    inv_freq = ROPE_THETA ** (-jnp.arange(0, MLA_ROPE, 2, dtype=F32) / MLA_ROPE)
    ang = positions.reshape(S, 1).astype(F32) * inv_freq
    cos, sin = jnp.cos(ang), jnp.sin(ang)
    z = jnp.zeros((S, LANES - MLA_ROPE), F32)
    return jnp.concatenate([cos, cos, z], axis=1), jnp.concatenate([-sin, sin, z], axis=1)


def _pad_heads(w, per_head):
    K = w.shape[0]
    H = w.shape[1] // per_head
    w3 = w.reshape(K, H, per_head)
    return jnp.pad(w3, ((0, 0), (0, 0), (0, MLA_HEAD_PAD - per_head))).reshape(K, H * MLA_HEAD_PAD)


def _unpad_heads(w, per_head):
    K = w.shape[0]
    H = w.shape[1] // MLA_HEAD_PAD
    return w.reshape(K, H, MLA_HEAD_PAD)[:, :, :per_head].reshape(K, H * per_head)


def _local_step(x, tgt, positions, mod, gmix, gmlp, fg, wts, small):
    S, D = x.shape
    L = mod.shape[0]
    tm = min(512, S)
    tms = min(256, S)
    tq = min(512, S)
    cos_t, sin_t = _rope_tables(positions, S)
    saved = []
    for i in range(L):
        kind, w = i % 3, wts[i]
        sv = {"x": x}
        if kind == 0:
            x1 = _pool_fwd(x, mod[i], gmix[i], w["pool_w"], small["pool_scale"][i // 3], tm, f"pool_fwd_{i}")
        elif kind == 1:
            x1, sv["h"], sv["pre"], sv["y"] = _sgu_fwd(
                x, mod[i], gmix[i], w["sgu_w_in"], small["sgu_ln_g"], small["sgu_ln_b"], small["sgu_w_s"],
                small["sgu_b_s_t"], w["sgu_w_out"], tms, f"sgu_fwd_{i}")
        else:
            sv["h"], sv["lat"], sv["cq"], sv["ckv"], krp = _mla_lat(
                x, mod[i], gmix[i], w["mla_wd"], small["mla_q_norm_g"], small["mla_kv_norm_g"], cos_t, sin_t, tm,
                f"mla_lat_{i}")
            sv["q"], sv["k"], sv["v"] = _mla_qkv(sv["cq"], sv["ckv"], krp, w["mla_wq"], w["mla_w_ukv"], cos_t, sin_t,
                                                 tm, f"mla_qkv_{i}")
            sv["o"], sv["lse"] = _attn_fwd(sv["q"], sv["k"], sv["v"], tq, f"attn_fwd_{i}")
            x1, sv["y"] = _mla_out(sv["o"], w["mla_w_o"], x, mod[i], tm, f"mla_out_{i}")
        sv["x1"] = x1
        Fh = w["mlp_w1"].shape[1]
        sv["h2"], sv["r"] = _mlp_up(x1, mod[i], gmlp[i], w["mlp_w1"], tm, min(2048, Fh), f"mlp_up_{i}")
        x, sv["o2"] = _mlp_down(sv["r"], w["mlp_w2"], x1, mod[i], tm, f"mlp_down_{i}")
        saved.append(sv)

    dx, loss_acc = _loss_head(x, tgt, fg, tm, "loss_head")

    stats, grads, sgrads = [None] * L, [None] * L, {}
    for i in reversed(range(L)):
        kind, w, sv = i % 3, wts[i], saved[i]
        Fh = w["mlp_w1"].shape[1]
        g = {}
        d_a, d_o, st_a = _mlp_bwd_a(dx, sv["o2"], mod[i], w["mlp_w2"], sv["r"], tm, min(2048, Fh), f"mlp_bwd_a_{i}")
        g["mlp_w2"] = _mm_tn(sv["r"], d_o, min(512, Fh), D, f"mlp_dw2_{i}", square_a=True)
        g["mlp_w1"] = _mm_tn(sv["h2"], d_a, D, min(512, Fh), f"mlp_dw1_{i}", col_shards=True)
        dx1, st_b = _mlp_bwd_b(d_a, w["mlp_w1"], sv["x1"], dx, mod[i], gmlp[i], tm, f"mlp_bwd_b_{i}")
        if kind == 0:
            dx, st_m, dpw = _pool_bwd(sv["x"], dx1, mod[i], gmix[i], w["pool_w"], small["pool_scale"][i // 3], tm,
                                      f"pool_bwd_{i}")
            g["pool_w"] = dpw
            sgrads[f"pool_scale_{i // 3}"] = st_m[4:5]
        elif kind == 1:
            dx, dyb, gated, dpre, st_m, dws, dbs = _sgu_bwd(
                sv["x"], dx1, sv["pre"], sv["y"], mod[i], gmix[i], w["sgu_w_in"], small["sgu_ln_g"], small["sgu_ln_b"],
                small["sgu_w_s"], small["sgu_b_s_t"], w["sgu_w_out"], tms, f"sgu_bwd_{i}")
            W = gated.shape[1]
            g["sgu_w_out"] = _mm_tn(gated, dyb, min(512, W), D, f"sgu_dwout_{i}")
            g["sgu_w_in"] = _mm_tn(sv["h"], dpre, D, min(512, 2 * W), f"sgu_dwin_{i}", col_shards=True)
            sgrads["sgu_ln_g"], sgrads["sgu_ln_b"] = st_m[4:5], st_m[5:6]
            sgrads["sgu_w_s"], sgrads["sgu_b_s"] = dws, dbs[:, :, 0]
        else:
            dyb, do, st_o = _mla_bwd_o(dx1, sv["y"], mod[i], w["mla_w_o"], tm, f"mla_bwd_o_{i}")
            KO = do.shape[1]
            g["mla_w_o"] = _mm_tn(sv["o"], dyb, min(512, KO), D, f"mla_dwo_{i}")
            dq, dkv, dkr = _attn_bwd(sv["q"], sv["k"], sv["v"], sv["o"], do, sv["lse"], cos_t, sin_t, tq, f"attn_bwd_{i}")
            dx, dlat, st_m, dqg, dkvg = _mla_bwd_lat(
                dq, dkv, dkr, sv["lat"], sv["x"], dx1, mod[i], gmix[i], small["mla_q_norm_g"], small["mla_kv_norm_g"],
                w["mla_wq"], w["mla_w_ukv"], w["mla_wd"], cos_t, sin_t, tm, f"mla_bwd_lat_{i}")
            HW = dq.shape[1]
            g["mla_wq"] = _mm_tn(sv["cq"], dq, MLA_Q_LORA, min(1024, HW), f"mla_dwq_{i}")
            g["mla_w_ukv"] = _mm_tn(sv["ckv"], dkv, MLA_KV_LORA, min(1024, HW), f"mla_dwukv_{i}", col_shards=True)
            g["mla_wd"] = _mm_tn(sv["h"], dlat, D, dlat.shape[1], f"mla_dwd_{i}")
            st_m = jnp.concatenate([st_m[0:2], st_o[2:3], st_m[3:]], axis=0)
            sgrads["mla_q_norm_g"], sgrads["mla_kv_norm_g"] = dqg[0:1], dkvg[0:1]
        stats[i] = jnp.concatenate([st_m[0:3], st_b[3:5], st_a[5:6], st_m[3:4], st_b[6:7]], axis=0)
        grads[i] = g
    return loss_acc, dx, stats, grads, sgrads


HBM_SPEC = pl.BlockSpec(memory_space=pltpu.HBM)
VMEM_SPEC = pl.BlockSpec(memory_space=pltpu.VMEM)


def _my_place():
    return lax.axis_index("x"), lax.axis_index("y"), lax.axis_index("c")


def _flip(v, bit):
    return 1 - v if bit else v


def _small_all_gather(v, name):
    R, C = v.shape

    def body(x_ref, out_ref, send_sems, recv_sems):
        x, y, c = _my_place()
        me = 4 * x + 2 * y + c
        out_ref[me] = x_ref[...]
        sends = []
        for k in range(1, NDEV):
            peer = (_flip(x, k & 4), _flip(y, k & 2), _flip(c, k & 1))
            cp = pltpu.make_async_remote_copy(src_ref=x_ref, dst_ref=out_ref.at[me], send_sem=send_sems.at[k - 1],
                                              recv_sem=recv_sems.at[k - 1], device_id=peer, device_id_type=MESH)
            cp.start()
            sends.append(cp)
        for k in range(1, NDEV):
            src = 4 * _flip(x, k & 4) + 2 * _flip(y, k & 2) + _flip(c, k & 1)
            pltpu.make_async_remote_copy(src_ref=x_ref, dst_ref=out_ref.at[src], send_sem=send_sems.at[k - 1],
                                         recv_sem=recv_sems.at[k - 1], device_id=(x, y, c), device_id_type=MESH).wait_recv()
        for cp in sends:
            cp.wait_send()

    return pl.pallas_call(
        body, name=name, out_shape=SDS((NDEV, R, C), v.dtype), in_specs=[VMEM_SPEC], out_specs=VMEM_SPEC,
        scratch_shapes=[pltpu.SemaphoreType.DMA((NDEV - 1,)), pltpu.SemaphoreType.DMA((NDEV - 1,))],
        compiler_params=pltpu.CompilerParams(vmem_limit_bytes=V7X_VMEM_LIMIT),
    )(v)


def _slab(ref, axis, width, dev):
    idx = [slice(None)] * len(ref.shape)
    idx[axis] = pl.ds(pl.multiple_of(dev * width, width), width)
    return ref.at[tuple(idx)]


def _all_gather_group(shards, axes, name):
    nt = len(shards)
    out_shapes = [SDS(tuple(s * NDEV if a == ax else s for a, s in enumerate(sh.shape)), sh.dtype)
                  for sh, ax in zip(shards, axes)]

    def body(*refs):
        ins, outs = refs[:nt], refs[nt:2 * nt]
        send_sems, recv_sems, local_sems = refs[2 * nt:]
        x, y, c = _my_place()
        me = 4 * x + 2 * y + c
        sibling = (x, y, 1 - c)
        chips = [(1 - x, y), (x, 1 - y), (1 - x, 1 - y)]

        def block(t, dev):
            return _slab(outs[t], axes[t], ins[t].shape[axes[t]], dev)

        def copy(t, k, dev, to, src=None):
            return pltpu.make_async_remote_copy(
                src_ref=block(t, dev) if src is None else src, dst_ref=block(t, dev), send_sem=send_sems.at[t, k],
                recv_sem=recv_sems.at[t, k], device_id=to, device_id_type=MESH)

        mine = [pltpu.make_async_copy(ins[t], block(t, me), local_sems.at[t]) for t in range(nt)]
        for cp in mine:
            cp.start()
        first = []
        for t in range(nt):
            first.append(copy(t, 0, me, sibling, src=ins[t]))
            first += [copy(t, 1 + j, me, (cx, cy, c), src=ins[t]) for j, (cx, cy) in enumerate(chips)]
        for cp in first:
            cp.start()
        passed = []
        for j, (cx, cy) in enumerate(chips):
            for t in range(nt):
                copy(t, 1 + j, 4 * cx + 2 * cy + c, (x, y, c)).wait_recv()
                cp = copy(t, 4 + j, 4 * cx + 2 * cy + c, sibling)
                cp.start()
                passed.append(cp)
        for t in range(nt):
            copy(t, 0, 4 * x + 2 * y + (1 - c), (x, y, c)).wait_recv()
        for j, (cx, cy) in enumerate(chips):
            for t in range(nt):
                copy(t, 4 + j, 4 * cx + 2 * cy + (1 - c), (x, y, c)).wait_recv()
        for cp in first + passed:
            cp.wait_send()
        for cp in mine:
            cp.wait()

    return pl.pallas_call(
        body, name=name, out_shape=out_shapes, in_specs=[HBM_SPEC] * nt, out_specs=[HBM_SPEC] * nt,
        scratch_shapes=[pltpu.SemaphoreType.DMA((nt, NDEV - 1)), pltpu.SemaphoreType.DMA((nt, NDEV - 1)),
                        pltpu.SemaphoreType.DMA((nt,))],
    )(*shards)


def _reduce_scatter_sibling(grads, axes, name):
    nt = len(grads)
    NCH = NDEV // 2

    def shard_shape(gr, ax):
        return tuple(s // NDEV if a == ax else s for a, s in enumerate(gr.shape))

    out_shapes = ([SDS((NCH,) + shard_shape(gr, ax), gr.dtype) for gr, ax in zip(grads, axes)]
                  + [SDS((NCH,) + shard_shape(gr, ax), gr.dtype) for gr, ax in zip(grads, axes)])

    def body(*refs):
        ins, lands, owns = refs[:nt], refs[nt:2 * nt], refs[2 * nt:3 * nt]
        send_sems, recv_sems, local_sems = refs[3 * nt:]
        x, y, c = _my_place()
        sends, locs = [], []
        for t in range(nt):
            w = ins[t].shape[axes[t]] // NDEV
            for k in range(NCH):
                px, py = k >> 1, k & 1
                cp = pltpu.make_async_remote_copy(
                    src_ref=_slab(ins[t], axes[t], w, 4 * px + 2 * py + (1 - c)), dst_ref=lands[t].at[k],
                    send_sem=send_sems.at[t, k], recv_sem=recv_sems.at[t, k], device_id=(x, y, 1 - c), device_id_type=MESH)
                cp.start()
                sends.append(cp)
                lc = pltpu.make_async_copy(_slab(ins[t], axes[t], w, 4 * px + 2 * py + c), owns[t].at[k], local_sems.at[t, k])
                lc.start()
                locs.append(lc)
        for cp in sends:
            cp.wait_recv()
        for cp in sends:
            cp.wait_send()
        for lc in locs:
            lc.wait()

    res = pl.pallas_call(
        body, name=name, out_shape=out_shapes, in_specs=[HBM_SPEC] * nt, out_specs=[HBM_SPEC] * (2 * nt),
        scratch_shapes=[pltpu.SemaphoreType.DMA((nt, NCH)), pltpu.SemaphoreType.DMA((nt, NCH)),
                        pltpu.SemaphoreType.DMA((nt, NCH))],
    )(*grads)
    return res[:nt], res[nt:]


def _reduce_scatter_chips(parts, name):
    nt = len(parts)
    out_shapes = [SDS((3,) + p.shape[1:], p.dtype) for p in parts]

    def body(*refs):
        ins, outs = refs[:nt], refs[nt:2 * nt]
        send_sems, recv_sems = refs[2 * nt:]
        x, y, c = _my_place()
        sends = []
        for t in range(nt):
            for m in range(1, 4):
                px, py = _flip(x, m & 2), _flip(y, m & 1)
                cp = pltpu.make_async_remote_copy(
                    src_ref=ins[t].at[2 * px + py], dst_ref=outs[t].at[m - 1], send_sem=send_sems.at[t, m - 1],
                    recv_sem=recv_sems.at[t, m - 1], device_id=(px, py, c), device_id_type=MESH)
                cp.start()
                sends.append(cp)
        for cp in sends:
            cp.wait_recv()
        for cp in sends:
            cp.wait_send()

    return pl.pallas_call(
        body, name=name, out_shape=out_shapes, in_specs=[HBM_SPEC] * nt, out_specs=[HBM_SPEC] * nt,
        scratch_shapes=[pltpu.SemaphoreType.DMA((nt, 3)), pltpu.SemaphoreType.DMA((nt, 3))],
    )(*parts)


def _row_tile(R, C, itemsize=4, target=1 << 20):
    best = R
    for tr in range(8, R, 8):
        if R % tr == 0 and tr * C * itemsize <= target:
            best = tr
    return best if best * C * itemsize <= target or best == R else R


def _as2d(a):
    return a.reshape(-1, a.shape[-1])


def _add_pairs(a, b, name):
    shp = a.shape
    a2, b2 = _as2d(a), _as2d(b)
    R, C = a2.shape
    tr = _row_tile(R, C, 2)

    def body(a_ref, b_ref, o_ref):
        o_ref[...] = (a_ref[...].astype(F32) + b_ref[...].astype(F32)).astype(o_ref.dtype)

    out = pl.pallas_call(
        body, name=name, grid=(R // tr,), in_specs=[_rows1(tr, C), _rows1(tr, C)], out_specs=_rows1(tr, C),
        out_shape=SDS((R, C), a.dtype), compiler_params=_params(("parallel",)),
    )(a2, b2)
    return out.reshape(shp)


def _adamw_math(g, w, m, v):
    m2 = ADAM_B1 * m + (1.0 - ADAM_B1) * g
    v2 = ADAM_B2 * v + (1.0 - ADAM_B2) * (g * g)
    m_hat = m2 / (1.0 - ADAM_B1 ** ADAM_STEP)
    v_hat = v2 / (1.0 - ADAM_B2 ** ADAM_STEP)
    delta = -ADAM_LR * (m_hat / (jnp.sqrt(v_hat) + ADAM_EPS) + ADAM_WD * w)
    return delta, m2, v2


def _adamw(parts, w, m, v, name):
    shp = w.shape
    w2, m2, v2 = _as2d(w), _as2d(m), _as2d(v)
    R, C = w2.shape
    tr = _row_tile(R, C)
    p3 = [p.reshape((-1, R, C)) for p in parts]
    npart = len(p3)

    def body(*refs):
        prefs = refs[:npart]
        w_ref, m_ref, v_ref, g_ref, d_ref, nm_ref, nv_ref = refs[npart:]
        g = None
        for pr in prefs:
            for k in range(pr.shape[0]):
                term = pr[k].astype(F32)
                g = term if g is None else g + term
        g_ref[...] = g
        d_ref[...], nm_ref[...], nv_ref[...] = _adamw_math(g, w_ref[...], m_ref[...], v_ref[...])

    outs = pl.pallas_call(
        body, name=name, grid=(R // tr,),
        in_specs=[pl.BlockSpec((p.shape[0], tr, C), lambda i: (0, i, 0)) for p in p3] + [_rows1(tr, C)] * 3,
        out_specs=[_rows1(tr, C)] * 4, out_shape=[SDS((R, C), F32)] * 4,
        compiler_params=_params(("parallel",)),
    )(*p3, w2, m2, v2)
    return [o.reshape(shp) for o in outs]


def _ada_fwd(c_all, ada_w, ada_b_mine, name):
    L, D, Wc = ada_w.shape

    def body(c_ref, w_ref, b_ref, o_ref):
        cv = c_ref[...]
        act = cv * (1.0 / (1.0 + jnp.exp(-cv)))
        o_ref[0] = jnp.dot(act, w_ref[0], preferred_element_type=F32, precision=lax.Precision.HIGHEST) + b_ref[0]

    return pl.pallas_call(
        body, name=name, grid=(L,),
        in_specs=[_resident(c_all.shape), pl.BlockSpec((1, D, Wc), lambda l: (l, 0, 0)), pl.BlockSpec((1, 1, Wc), lambda l: (l, 0, 0))],
        out_specs=pl.BlockSpec((1, NDEV, Wc), lambda l: (l, 0, 0)), out_shape=SDS((L, NDEV, Wc), F32),
        compiler_params=_params(("parallel",)),
    )(c_all, ada_w, ada_b_mine.reshape(L, 1, Wc))


def _ada_bwd(c_all, dmod_mine, name):
    L, _, Wc = dmod_mine.shape
    D = c_all.shape[1]

    def body(c_ref, d_ref, o_ref):
        cv = c_ref[...]
        act = cv * (1.0 / (1.0 + jnp.exp(-cv)))
        o_ref[0] = lax.dot_general(act, d_ref[0], (((0,), (0,)), ((), ())), preferred_element_type=F32,
                                   precision=lax.Precision.HIGHEST)

    return pl.pallas_call(
        body, name=name, grid=(L,),
        in_specs=[_resident(c_all.shape), pl.BlockSpec((1, NDEV, Wc), lambda l: (l, 0, 0))],
        out_specs=pl.BlockSpec((1, D, Wc), lambda l: (l, 0, 0)), out_shape=SDS((L, D, Wc), F32),
        compiler_params=_params(("parallel",)),
    )(c_all, dmod_mine)


WEIGHT_NAMES = ['ada_w', 'ada_b', 'norm_mix_g', 'norm_mlp_g', 'pool_w', 'pool_scale', 'sgu_w_in', 'sgu_ln_g', 'sgu_ln_b',
                'sgu_w_s', 'sgu_b_s', 'sgu_w_out', 'mla_w_dq_dkv', 'mla_q_norm_g', 'mla_kv_norm_g', 'mla_w_uq', 'mla_w_ukv',
                'mla_w_o', 'mlp_w1', 'mlp_w2', 'final_g']
REPLICATED = ['ada_b', 'norm_mix_g', 'norm_mlp_g', 'sgu_ln_g', 'sgu_ln_b', 'sgu_w_s', 'sgu_b_s', 'mla_kv_norm_g', 'final_g']
PACK_ROWS = 64
Q_HEAD = MLA_NOPE + MLA_ROPE


def _layer_matrices(i):
    kind, j = i % 3, i // 3
    if kind == 0:
        mats = [("pool_w", j, 1)]
    elif kind == 1:
        mats = [("sgu_w_in", j, 1), ("sgu_w_out", j, 0)]
    else:
        mats = [("mla_w_dq_dkv", j, 0), ("mla_w_uq", j, 1), ("mla_w_ukv", j, 1), ("mla_w_o", j, 0)]
    return mats + [("mlp_w1", i, 1), ("mlp_w2", i, 0)]


def _pack(arrays):
    flat = jnp.concatenate([a.reshape(-1).astype(F32) for a in arrays])
    rows = -(-flat.size // (LANES * PACK_ROWS)) * PACK_ROWS
    return jnp.pad(flat, (0, rows * LANES - flat.size)).reshape(rows, LANES)


def kernel(x, c, positions, ada_w, ada_b, norm_mix_g, norm_mlp_g, pool_w, pool_scale, sgu_w_in, sgu_ln_g, sgu_ln_b, sgu_w_s, sgu_b_s, sgu_w_out, mla_w_dq_dkv, mla_q_norm_g, mla_kv_norm_g, mla_w_uq, mla_w_ukv, mla_w_o, mlp_w1, mlp_w2, final_g, loss_target, m_ada_w, m_ada_b, m_norm_mix_g, m_norm_mlp_g, m_pool_w, m_pool_scale, m_sgu_w_in, m_sgu_ln_g, m_sgu_ln_b, m_sgu_w_s, m_sgu_b_s, m_sgu_w_out, m_mla_w_dq_dkv, m_mla_q_norm_g, m_mla_kv_norm_g, m_mla_w_uq, m_mla_w_ukv, m_mla_w_o, m_mlp_w1, m_mlp_w2, m_final_g, v_ada_w, v_ada_b, v_norm_mix_g, v_norm_mlp_g, v_pool_w, v_pool_scale, v_sgu_w_in, v_sgu_ln_g, v_sgu_ln_b, v_sgu_w_s, v_sgu_b_s, v_sgu_w_out, v_mla_w_dq_dkv, v_mla_q_norm_g, v_mla_kv_norm_g, v_mla_w_uq, v_mla_w_ukv, v_mla_w_o, v_mlp_w1, v_mlp_w2, v_final_g):
    a = dict(locals())
    S, D = x.shape[1], x.shape[2]
    L = ada_w.shape[0]
    Wc = ada_w.shape[2]
    me = 4 * lax.axis_index("x") + 2 * lax.axis_index("y") + lax.axis_index("c")
    my_chip = 2 * lax.axis_index("x") + lax.axis_index("y")

    v0 = _pack([c, pool_scale, mla_q_norm_g])
    g0 = _small_all_gather(v0, "gather_c").reshape(NDEV, -1)
    n_ps, n_qg = pool_scale.size, mla_q_norm_g.size
    c_all = g0[:, :D]
    ps_w = pool_scale.shape[1]
    ps_full = g0[:, D:D + n_ps].reshape(NDEV, -1, ps_w).transpose(1, 0, 2).reshape(-1, 1, D)
    qg_full = g0[:, D + n_ps:D + n_ps + n_qg].reshape(1, -1)

    ada_b_mine = lax.dynamic_slice_in_dim(ada_b, me * Wc, Wc, axis=1)
    modp = _ada_fwd(c_all, ada_w, ada_b_mine, "ada_fwd")
    ga = _small_all_gather(modp.reshape(-1, LANES), "gather_mod").reshape(NDEV, L, NDEV, Wc)
    mod = lax.dynamic_index_in_dim(ga, me, axis=2, keepdims=False).transpose(1, 0, 2).reshape(L, 6, D)
    mod8 = jnp.pad(mod, ((0, 0), (0, 2), (0, 0)))

    wts = []
    for i in range(L):
        mats = _layer_matrices(i)
        full = _all_gather_group([a[n][j].astype(BF16) for n, j, _ in mats], [ax for _, _, ax in mats], f"gather_w_{i}")
        w = {n: f for (n, _, _), f in zip(mats, full)}
        if "mla_w_uq" in w:
            lat_w = w["mla_w_dq_dkv"].shape[1]
            w["mla_wd"] = jnp.pad(w.pop("mla_w_dq_dkv"), ((0, 0), (0, -lat_w % LANES)))
            w["mla_wq"] = _pad_heads(w.pop("mla_w_uq"), Q_HEAD)
        wts.append(w)

    small = {"pool_scale": ps_full, "sgu_ln_g": sgu_ln_g, "sgu_ln_b": sgu_ln_b, "sgu_w_s": sgu_w_s[0],
             "sgu_b_s_t": sgu_b_s[0].T, "mla_q_norm_g": qg_full, "mla_kv_norm_g": mla_kv_norm_g}
    loss_acc, dx, stats, grads, sgrads = _local_step(
        x[0], loss_target[0], positions, mod8, norm_mix_g.reshape(L, 1, D), norm_mlp_g.reshape(L, 1, D),
        final_g.reshape(1, D), wts, small)

    res = {n: [None] * a[n].shape[0] for n in WEIGHT_NAMES if a[n].ndim > 1}
    for i in reversed(range(L)):
        mats = _layer_matrices(i)
        g = dict(grads[i])
        if "mla_wq" in g:
            g["mla_w_dq_dkv"] = g.pop("mla_wd")[:, :mla_w_dq_dkv.shape[2]]
            g["mla_w_uq"] = _unpad_heads(g.pop("mla_wq"), Q_HEAD)
        gl, rs_axes = [], []
        for n, _, ax in mats:
            gm = g[n].astype(BF16)
            if gm.ndim == 3 and gm.shape[0] == NDEV and n != "pool_w":
                gm, ax = gm.reshape(-1, gm.shape[-1]), 0
            elif n == "mla_w_uq":
                kq, nq = gm.shape
                gm, ax = gm.reshape(kq, NDEV, nq // NDEV).transpose(1, 0, 2).reshape(NDEV * kq, nq // NDEV), 0
            gl.append(gm)
            rs_axes.append(ax)
        lands, owns = _reduce_scatter_sibling(gl, rs_axes, f"rs_sibling_{i}")
        parts = [_add_pairs(o, l, f"rs_add_{i}_{n}") for o, l, (n, _, _) in zip(owns, lands, mats)]
        recv = _reduce_scatter_chips(parts, f"rs_chips_{i}")
        for (n, j, _), p, r in zip(mats, parts, recv):
            p_own = lax.dynamic_index_in_dim(p, my_chip, axis=0, keepdims=False)
            res[n][j] = _adamw([p_own, r], a[n][j], a["m_" + n][j], a["v_" + n][j], f"adamw_{n}_{j}")

    sg = {"ada_b": jnp.stack([s[0:6] for s in stats]), "norm_mix_g": jnp.stack([s[6] for s in stats]),
          "norm_mlp_g": jnp.stack([s[7] for s in stats]), "final_g": loss_acc[0], "sgu_ln_g": sgrads["sgu_ln_g"],
          "sgu_ln_b": sgrads["sgu_ln_b"], "sgu_w_s": sgrads["sgu_w_s"], "sgu_b_s": sgrads["sgu_b_s"],
          "mla_kv_norm_g": sgrads["mla_kv_norm_g"]}
    ps_grad = jnp.concatenate([sgrads[f"pool_scale_{j}"] for j in range(pool_scale.shape[0])])
    tail = [ps_grad, sgrads["mla_q_norm_g"], loss_acc[2, :LANES]]
    packed = _pack([sg[n] for n in REPLICATED] + tail)
    gathered = _small_all_gather(packed, "gather_small")
    zeros_tail = [jnp.zeros_like(t) for t in tail]
    g_p, d_p, m_p, v_p = _adamw([gathered], _pack([a[n] for n in REPLICATED] + zeros_tail),
                                _pack([a["m_" + n] for n in REPLICATED] + zeros_tail),
                                _pack([a["v_" + n] for n in REPLICATED] + zeros_tail), "adamw_replicated")
    flat = [t.reshape(-1) for t in (g_p, d_p, m_p, v_p)]
    off = 0
    for n in REPLICATED:
        res[n] = [f[off:off + a[n].size].reshape(a[n].shape) for f in flat]
        off += a[n].size
    g_ps = flat[0][off:off + ps_grad.size].reshape(ps_grad.shape)
    off += ps_grad.size
    g_qg = flat[0][off:off + qg_full.size].reshape(1, -1)
    off += qg_full.size
    loss = flat[0][off]
    res["pool_scale"] = _adamw([lax.dynamic_slice_in_dim(g_ps, me * ps_w, ps_w, axis=1)], pool_scale, m_pool_scale,
                               v_pool_scale, "adamw_pool_scale")
    qg_w = mla_q_norm_g.shape[1]
    res["mla_q_norm_g"] = _adamw([lax.dynamic_slice_in_dim(g_qg, me * qg_w, qg_w, axis=1)], mla_q_norm_g, m_mla_q_norm_g,
                                 v_mla_q_norm_g, "adamw_q_norm_g")

    n_mod = L * 6 * D
    dmod_all = gathered.reshape(NDEV, -1)[:, :n_mod].reshape(NDEV, L, 6 * D)
    dmod_mine = lax.dynamic_slice_in_dim(dmod_all, me * Wc, Wc, axis=2).transpose(1, 0, 2)
    res["ada_w"] = _adamw([_ada_bwd(c_all, dmod_mine, "ada_bwd")], ada_w, m_ada_w, v_ada_w, "adamw_ada_w")

    outs = []
    for k in range(4):
        for n in WEIGHT_NAMES:
            r = res[n]
            outs.append(jnp.stack([lay[k] for lay in r]) if isinstance(r[0], list) else r[k])
    return (loss, dx.reshape(x.shape), *outs)
```

```python
import functools
import math

import jax
import jax.numpy as jnp
import numpy as np
from jax import lax
from jax.experimental import pallas as pl
from jax.experimental.pallas import tpu as pltpu

F32 = jnp.float32
BF16 = jnp.bfloat16
SDS = jax.ShapeDtypeStruct
MESH = pl.DeviceIdType.MESH

NDEV = 8
V7X_VMEM_LIMIT = 56 << 20
LANES = 128
RMS_EPS = 1e-6
LN_EPS = 1e-5
POOL_WINDOWS = (2, 4, 8, 16)
HALO = 16
SGU_CHUNK = 128
SGU_HEAD = 128
MLA_NOPE, MLA_ROPE, MLA_V = 128, 64, 128
MLA_Q_LORA, MLA_KV_LORA = 256, 128
MLA_HEAD_PAD = 256
ROPE_THETA = 10000.0
SM_SCALE = (MLA_NOPE + MLA_ROPE) ** -0.5
NEG = -1e30
ADAM_LR, ADAM_B1, ADAM_B2, ADAM_EPS, ADAM_WD, ADAM_STEP = 0.001, 0.9, 0.999, 1e-08, 0.01, 10
INV_SQRT2 = 1.0 / math.sqrt(2.0)
INV_SQRT_2PI = 1.0 / math.sqrt(2.0 * math.pi)
SH1, SC1, G1, SH2, SC2, G2 = 0, 1, 2, 3, 4, 5


def _params(sem=None, vmem=V7X_VMEM_LIMIT):
    return pltpu.CompilerParams(dimension_semantics=sem, vmem_limit_bytes=vmem)


def _resident(shape):
    nd = len(shape)
    return pl.BlockSpec(shape, lambda *_: (0,) * nd, pipeline_mode=pl.Buffered(1))


def _rows1(tm, w):
    return pl.BlockSpec((tm, w), lambda i: (i, 0))


def _rms(x):
    r = lax.rsqrt(jnp.mean(x * x, axis=-1, keepdims=True) + RMS_EPS)
    return x * r, r


def _colsum(v):
    return jnp.sum(v, axis=0, keepdims=True)


def _normmod_bwd(dh, n, r, a):
    dn = dh * a
    return r * (dn - n * jnp.mean(dn * n, axis=-1, keepdims=True))


def _dot(a, b):
    return jnp.dot(a, b, preferred_element_type=F32)


def _dot_nt(a, b):
    return lax.dot_general(a, b, (((1,), (1,)), ((), ())), preferred_element_type=F32)


def _dot_tn(a, b):
    return lax.dot_general(a, b, (((0,), (0,)), ((), ())), preferred_element_type=F32)


def _gelu(x):
    return 0.5 * x * (1.0 + lax.erf(x * INV_SQRT2))


def _gelu_grad(x):
    return 0.5 * (1.0 + lax.erf(x * INV_SQRT2)) + x * jnp.exp(-0.5 * x * x) * INV_SQRT_2PI


def _swap_halves(v):
    lane = lax.broadcasted_iota(jnp.int32, v.shape, 1)
    half = MLA_ROPE // 2
    return jnp.where(lane < half, pltpu.roll(v, LANES - half, 1),
                     jnp.where(lane < MLA_ROPE, pltpu.roll(v, half, 1), 0.0))


def _mlp_up(x1, mod, gn, w1, tm, tn, name):
    S, D = x1.shape
    Fh = w1.shape[1]

    def body(x_ref, mod_ref, gn_ref, w_ref, h_ref, r_ref):
        n, _ = _rms(x_ref[...])
        a = gn_ref[...] * (1.0 + mod_ref[SC2:SC2 + 1, :])
        h = (n * a + mod_ref[SH2:SH2 + 1, :]).astype(BF16)
        h_ref[...] = h
        for j in range(Fh // tn):
            cols = slice(j * tn, (j + 1) * tn)
            r_ref[:, cols] = jnp.maximum(_dot(h, w_ref[:, cols]), 0.0).astype(BF16)

    return pl.pallas_call(
        body, name=name, grid=(S // tm,),
        in_specs=[_rows1(tm, D), _resident(mod.shape), _resident(gn.shape), _resident(w1.shape)],
        out_specs=[_rows1(tm, D), _rows1(tm, Fh)],
        out_shape=[SDS((S, D), BF16), SDS((S, Fh), BF16)],
        compiler_params=_params(("parallel",)),
    )(x1, mod, gn, w1)


def _mlp_down(r, w2, x1, mod, tm, name):
    S, Fh = r.shape
    D = w2.shape[1]

    def body(r_ref, w_ref, x_ref, mod_ref, x2_ref, o_ref):
        rv = r_ref[...]
        o = _dot(rv * rv, w_ref[...])
        o_ref[...] = o.astype(BF16)
        x2_ref[...] = x_ref[...] + mod_ref[G2:G2 + 1, :] * o

    return pl.pallas_call(
        body, name=name, grid=(S // tm,),
        in_specs=[_rows1(tm, Fh), _resident(w2.shape), _rows1(tm, D), _resident(mod.shape)],
        out_specs=[_rows1(tm, D), _rows1(tm, D)],
        out_shape=[SDS((S, D), F32), SDS((S, D), BF16)],
        compiler_params=_params(("parallel",)),
    )(r, w2, x1, mod)


def _mlp_bwd_a(dx2, o, mod, w2, r, tm, tn, name):
    S, D = dx2.shape
    Fh = r.shape[1]

    def body(dx_ref, o_ref, mod_ref, w_ref, r_ref, da_ref, do_ref, st_ref):
        @pl.when(pl.program_id(0) == 0)
        def _():
            st_ref[...] = jnp.zeros_like(st_ref)

        dx = dx_ref[...]
        d_o = (dx * mod_ref[G2:G2 + 1, :]).astype(BF16)
        do_ref[...] = d_o
        st_ref[G2:G2 + 1, :] += _colsum(dx * o_ref[...].astype(F32))
        for j in range(Fh // tn):
            cols = slice(j * tn, (j + 1) * tn)
            dz = _dot_nt(d_o, w_ref[cols, :])
            da_ref[:, cols] = (dz * (2.0 * r_ref[:, cols].astype(F32))).astype(BF16)

    return pl.pallas_call(
        body, name=name, grid=(S // tm,),
        in_specs=[_rows1(tm, D), _rows1(tm, D), _resident(mod.shape), _resident(w2.shape), _rows1(tm, Fh)],
        out_specs=[_rows1(tm, Fh), _rows1(tm, D), pl.BlockSpec((8, D), lambda i: (0, 0))],
        out_shape=[SDS((S, Fh), BF16), SDS((S, D), BF16), SDS((8, D), F32)],
        compiler_params=_params(("arbitrary",)),
    )(dx2, o, mod, w2, r)


def _mlp_bwd_b(d_a, w1, x1, dx2, mod, gn, tm, name):
    S, Fh = d_a.shape
    D = w1.shape[0]

    def body(da_ref, w_ref, x_ref, dx_ref, mod_ref, gn_ref, dx1_ref, st_ref):
        @pl.when(pl.program_id(0) == 0)
        def _():
            st_ref[...] = jnp.zeros_like(st_ref)

        dh = _dot_nt(da_ref[...], w_ref[...])
        n, rr = _rms(x_ref[...])
        gn_v = gn_ref[...]
        sc1p = 1.0 + mod_ref[SC2:SC2 + 1, :]
        t = _colsum(dh * n)
        st_ref[SH2:SH2 + 1, :] += _colsum(dh)
        st_ref[SC2:SC2 + 1, :] += t * gn_v
        st_ref[6:7, :] += t * sc1p
        dx1_ref[...] = dx_ref[...] + _normmod_bwd(dh, n, rr, gn_v * sc1p)

    return pl.pallas_call(
        body, name=name, grid=(S // tm,),
        in_specs=[_rows1(tm, Fh), _resident(w1.shape), _rows1(tm, D), _rows1(tm, D), _resident(mod.shape),
                  _resident(gn.shape)],
        out_specs=[_rows1(tm, D), pl.BlockSpec((8, D), lambda i: (0, 0))],
        out_shape=[SDS((S, D), F32), SDS((8, D), F32)],
        compiler_params=_params(("arbitrary",)),
    )(d_a, w1, x1, dx2, mod, gn)


def _mm_tn(a, g, tk, tn, name, square_a=False, col_shards=False):
    S, K1 = a.shape
    N = g.shape[1]
    w = N // NDEV
    per = tn // w if col_shards else 1

    def body(a_ref, g_ref, o_ref):
        av = a_ref[...]
        if square_a:
            av = av * av
        res = _dot_tn(av, g_ref[...]).astype(BF16)
        if col_shards:
            for s in range(per):
                o_ref[s] = res[:, s * w:(s + 1) * w]
        else:
            o_ref[...] = res

    if col_shards:
        out_spec, out_shape = pl.BlockSpec((per, tk, w), lambda i, j: (j, i, 0)), SDS((NDEV, K1, w), BF16)
    else:
        out_spec, out_shape = pl.BlockSpec((tk, tn), lambda i, j: (i, j)), SDS((K1, N), BF16)
    return pl.pallas_call(
        body, name=name, grid=(K1 // tk, N // tn),
        in_specs=[pl.BlockSpec((S, tk), lambda i, j: (0, i)), pl.BlockSpec((S, tn), lambda i, j: (0, j))],
        out_specs=out_spec, out_shape=out_shape,
        compiler_params=_params(("parallel", "parallel")),
    )(a, g)


def _pool_h_ext(x_ref, xp_ref, mod_ref, gn_ref, i, tm):
    ext = jnp.concatenate([xp_ref[...], x_ref[...]], axis=0)
    n, r = _rms(ext)
    a = gn_ref[...] * (1.0 + mod_ref[SC1:SC1 + 1, :])
    h = n * a + mod_ref[SH1:SH1 + 1, :]
    row = lax.broadcasted_iota(jnp.int32, (tm + HALO, 1), 0)
    h = jnp.where(jnp.logical_and(i == 0, row < HALO), 0.0, h)
    return h, n[HALO:], r[HALO:], a


def _trailing_sum(v, win):
    k = 1
    while k < win:
        v = v + pltpu.roll(v, k, 0)
        k *= 2
    return v


def _leading_sum(v, win):
    k = 1
    while k < win:
        v = v + pltpu.roll(v, v.shape[0] - k, 0)
        k *= 2
    return v


def _pool_fwd(x, mod, gn, pw, ps, tm, name):
    S, D = x.shape
    C = D // len(POOL_WINDOWS)
    hb = tm // HALO

    def body(x_ref, xp_ref, mod_ref, gn_ref, pw_ref, ps_ref, x1_ref):
        i = pl.program_id(0)
        h, _, _, _ = _pool_h_ext(x_ref, xp_ref, mod_ref, gn_ref, i, tm)
        t1 = (i * tm + lax.broadcasted_iota(jnp.int32, (tm, 1), 0)).astype(F32) + 1.0
        for g, win in enumerate(POOL_WINDOWS):
            cols = slice(g * C, (g + 1) * C)
            hg = h[:, cols]
            inv = 1.0 / jnp.minimum(t1, float(win))
            pooled = (_trailing_sum(hg, win)[HALO:] * inv - hg[HALO:]).astype(BF16)
            y = _dot(pooled, pw_ref[g]) * ps_ref[:, cols]
            x1_ref[:, cols] = x_ref[:, cols] + mod_ref[G1:G1 + 1, cols] * y

    return pl.pallas_call(
        body, name=name, grid=(S // tm,),
        in_specs=[_rows1(tm, D), pl.BlockSpec((HALO, D), lambda i: (jnp.maximum(i * hb - 1, 0), 0)),
                  _resident(mod.shape), _resident(gn.shape), _resident(pw.shape), _resident(ps.shape)],
        out_specs=_rows1(tm, D),
        out_shape=SDS((S, D), F32),
        compiler_params=_params(("parallel",)),
    )(x, x, mod, gn, pw, ps)


def _pool_bwd(x, dx1, mod, gn, pw, ps, tm, name):
    S, D = x.shape
    G = len(POOL_WINDOWS)
    C = D // G
    hb = tm // HALO
    nt = S // tm

    def body(x_ref, xp_ref, d1_ref, dn_ref, mod_ref, gn_ref, pw_ref, ps_ref, dx_ref, st_ref, dpw_ref):
        i = pl.program_id(0)

        @pl.when(i == 0)
        def _():
            st_ref[...] = jnp.zeros_like(st_ref)
            dpw_ref[...] = jnp.zeros_like(dpw_ref)

        h, n, rr, a = _pool_h_ext(x_ref, xp_ref, mod_ref, gn_ref, i, tm)
        g1 = mod_ref[G1:G1 + 1, :]
        ps_v = ps_ref[...]
        d1 = d1_ref[...]
        d1n = jnp.where(i == nt - 1, 0.0, dn_ref[...])
        dyr = (jnp.concatenate([d1, d1n], axis=0) * (g1 * ps_v)).astype(BF16)
        t1 = (i * tm + lax.broadcasted_iota(jnp.int32, (tm + HALO, 1), 0)).astype(F32) + 1.0
        parts = []
        for g, win in enumerate(POOL_WINDOWS):
            cols = slice(g * C, (g + 1) * C)
            hg = h[:, cols]
            inv = 1.0 / jnp.minimum(t1, float(win))
            pooled = (_trailing_sum(hg, win)[HALO:] * inv[:tm] - hg[HALO:]).astype(BF16)
            yraw = _dot(pooled, pw_ref[g])
            st_ref[G1:G1 + 1, cols] += _colsum(d1[:, cols] * (yraw * ps_v[:, cols]))
            st_ref[4:5, cols] += _colsum(d1[:, cols] * g1[:, cols] * yraw)
            dpw_ref[g] += _dot_tn(pooled, dyr[:tm, cols])
            dpool = _dot_nt(dyr[:, cols], pw_ref[g])
            parts.append(_leading_sum(dpool * inv, win)[:tm] - dpool[:tm])
        dh = jnp.concatenate(parts, axis=1)
        t = _colsum(dh * n)
        st_ref[SH1:SH1 + 1, :] += _colsum(dh)
        st_ref[SC1:SC1 + 1, :] += t * gn_ref[...]
        st_ref[3:4, :] += t * (1.0 + mod_ref[SC1:SC1 + 1, :])
        dx_ref[...] = d1 + _normmod_bwd(dh, n, rr, a)

    return pl.pallas_call(
        body, name=name, grid=(nt,),
        in_specs=[_rows1(tm, D), pl.BlockSpec((HALO, D), lambda i: (jnp.maximum(i * hb - 1, 0), 0)),
                  _rows1(tm, D), pl.BlockSpec((HALO, D), lambda i: (jnp.minimum((i + 1) * hb, S // HALO - 1), 0)),
                  _resident(mod.shape), _resident(gn.shape), _resident(pw.shape), _resident(ps.shape)],
        out_specs=[_rows1(tm, D), pl.BlockSpec((8, D), lambda i: (0, 0)), pl.BlockSpec((G, C, C), lambda i: (0, 0, 0))],
        out_shape=[SDS((S, D), F32), SDS((8, D), F32), SDS((G, C, C), F32)],
        compiler_params=_params(("arbitrary",)),
    )(x, x, dx1, dx1, mod, gn, pw, ps)


def _tril_bf16(w):
    row = lax.broadcasted_iota(jnp.int32, w.shape, 0)
    col = lax.broadcasted_iota(jnp.int32, w.shape, 1)
    return jnp.where(col <= row, w, 0.0).astype(BF16)


def _sgu_front(pre, lng_ref, lnb_ref, W):
    z = _gelu(pre)
    u, v = z[:, :W], z[:, W:]
    mu = jnp.mean(v, axis=-1, keepdims=True)
    xc = v - mu
    rstd = lax.rsqrt(jnp.mean(xc * xc, axis=-1, keepdims=True) + LN_EPS)
    vhat = xc * rstd
    return u, vhat, rstd, vhat * lng_ref[...] + lnb_ref[...]


def _sgu_mix(vn, ws_ref, bst_ref, mix_s, tm, W):
    for hd in range(W // SGU_HEAD):
        wm = _tril_bf16(ws_ref[hd])
        bcol = bst_ref[:, hd:hd + 1]
        for ci in range(tm // SGU_CHUNK):
            rs, cs = slice(ci * SGU_CHUNK, (ci + 1) * SGU_CHUNK), slice(hd * SGU_HEAD, (hd + 1) * SGU_HEAD)
            mix_s[rs, cs] = _dot(wm, vn[rs, cs].astype(BF16)) + bcol


def _sgu_fwd(x, mod, gn, w_in, lng, lnb, ws, bst, w_out, tm, name):
    S, D = x.shape
    W = w_out.shape[0]

    def body(x_ref, mod_ref, gn_ref, win_ref, lng_ref, lnb_ref, ws_ref, bst_ref, wout_ref,
             x1_ref, h_ref, pre_ref, y_ref, mix_s):
        n, _ = _rms(x_ref[...])
        a = gn_ref[...] * (1.0 + mod_ref[SC1:SC1 + 1, :])
        h = (n * a + mod_ref[SH1:SH1 + 1, :]).astype(BF16)
        h_ref[...] = h
        pre = _dot(h, win_ref[...])
        pre_ref[...] = pre.astype(BF16)
        u, _, _, vn = _sgu_front(pre, lng_ref, lnb_ref, W)
        _sgu_mix(vn, ws_ref, bst_ref, mix_s, tm, W)
        y = _dot((u * mix_s[...]).astype(BF16), wout_ref[...])
        y_ref[...] = y.astype(BF16)
        x1_ref[...] = x_ref[...] + mod_ref[G1:G1 + 1, :] * y

    return pl.pallas_call(
        body, name=name, grid=(S // tm,),
        in_specs=[_rows1(tm, D), _resident(mod.shape), _resident(gn.shape), _resident(w_in.shape),
                  _resident(lng.shape), _resident(lnb.shape), _resident(ws.shape), _resident(bst.shape),
                  _resident(w_out.shape)],
        out_specs=[_rows1(tm, D), _rows1(tm, D), _rows1(tm, 2 * W), _rows1(tm, D)],
        out_shape=[SDS((S, D), F32), SDS((S, D), BF16), SDS((S, 2 * W), BF16), SDS((S, D), BF16)],
        scratch_shapes=[pltpu.VMEM((tm, W), F32)],
        compiler_params=_params(("parallel",)),
    )(x, mod, gn, w_in, lng, lnb, ws, bst, w_out)


def _sgu_bwd(x, dx1, pre, y, mod, gn, w_in, lng, lnb, ws, bst, w_out, tm, name):
    S, D = x.shape
    W = w_out.shape[0]
    H = W // SGU_HEAD
    nt = S // tm

    def body(x_ref, d1_ref, pre_ref, y_ref, mod_ref, gn_ref, win_ref, lng_ref, lnb_ref, ws_ref, bst_ref, wout_ref,
             dx_ref, dy_ref, gt_ref, dpre_ref, st_ref, dws_ref, dbs_ref, mix_s, dvn_s):
        i = pl.program_id(0)

        @pl.when(i == 0)
        def _():
            st_ref[...] = jnp.zeros_like(st_ref)
            dws_ref[...] = jnp.zeros_like(dws_ref)
            dbs_ref[...] = jnp.zeros_like(dbs_ref)

        d1 = d1_ref[...]
        pre = pre_ref[...].astype(F32)
        u, vhat, rstd, vn = _sgu_front(pre, lng_ref, lnb_ref, W)
        _sgu_mix(vn, ws_ref, bst_ref, mix_s, tm, W)
        mixed = mix_s[...]
        gt_ref[...] = (u * mixed).astype(BF16)
        dyb = (d1 * mod_ref[G1:G1 + 1, :]).astype(BF16)
        dy_ref[...] = dyb
        st_ref[G1:G1 + 1, :] += _colsum(d1 * y_ref[...].astype(F32))
        dgt = _dot_nt(dyb, wout_ref[...])
        du = dgt * mixed
        dmix = dgt * u
        for hd in range(H):
            wm = _tril_bf16(ws_ref[hd])
            for ci in range(tm // SGU_CHUNK):
                rs, cs = slice(ci * SGU_CHUNK, (ci + 1) * SGU_CHUNK), slice(hd * SGU_HEAD, (hd + 1) * SGU_HEAD)
                dm = dmix[rs, cs]
                dmb = dm.astype(BF16)
                dbs_ref[hd] += jnp.broadcast_to(jnp.sum(dm, axis=1, keepdims=True), (SGU_CHUNK, LANES))
                dws_ref[hd] += _dot_nt(dmb, vn[rs, cs].astype(BF16))
                dvn_s[rs, cs] = _dot_tn(wm, dmb)
        dvn = dvn_s[...]
        st_ref[4:5, :] += _colsum(dvn * vhat)
        st_ref[5:6, :] += _colsum(dvn)
        dvh = dvn * lng_ref[...]
        dv = rstd * (dvh - jnp.mean(dvh, axis=-1, keepdims=True) - vhat * jnp.mean(dvh * vhat, axis=-1, keepdims=True))
        dpre_u = (du * _gelu_grad(pre[:, :W])).astype(BF16)
        dpre_v = (dv * _gelu_grad(pre[:, W:])).astype(BF16)
        dpre_ref[:, :W] = dpre_u
        dpre_ref[:, W:] = dpre_v
        dh = _dot_nt(dpre_u, win_ref[:, :W]) + _dot_nt(dpre_v, win_ref[:, W:])
        n, rr = _rms(x_ref[...])
        gn_v = gn_ref[...]
        sc1p = 1.0 + mod_ref[SC1:SC1 + 1, :]
        t = _colsum(dh * n)
        st_ref[SH1:SH1 + 1, :] += _colsum(dh)
        st_ref[SC1:SC1 + 1, :] += t * gn_v
        st_ref[3:4, :] += t * sc1p
        dx_ref[...] = d1 + _normmod_bwd(dh, n, rr, gn_v * sc1p)

        @pl.when(i == nt - 1)
        def _():
            for hd in range(H):
                row = lax.broadcasted_iota(jnp.int32, (SGU_CHUNK, SGU_CHUNK), 0)
                col = lax.broadcasted_iota(jnp.int32, (SGU_CHUNK, SGU_CHUNK), 1)
                dws_ref[hd] = jnp.where(col <= row, dws_ref[hd], 0.0)

    return pl.pallas_call(
        body, name=name, grid=(nt,),
        in_specs=[_rows1(tm, D), _rows1(tm, D), _rows1(tm, 2 * W), _rows1(tm, D), _resident(mod.shape),
                  _resident(gn.shape), _resident(w_in.shape), _resident(lng.shape), _resident(lnb.shape),
                  _resident(ws.shape), _resident(bst.shape), _resident(w_out.shape)],
        out_specs=[_rows1(tm, D), _rows1(tm, D), _rows1(tm, W), _rows1(tm, 2 * W),
                   pl.BlockSpec((8, D), lambda i: (0, 0)), pl.BlockSpec((H, SGU_CHUNK, SGU_CHUNK), lambda i: (0, 0, 0)),
                   pl.BlockSpec((H, SGU_CHUNK, LANES), lambda i: (0, 0, 0))],
        out_shape=[SDS((S, D), F32), SDS((S, D), BF16), SDS((S, W), BF16), SDS((S, 2 * W), BF16),
                   SDS((8, D), F32), SDS((H, SGU_CHUNK, SGU_CHUNK), F32), SDS((H, SGU_CHUNK, LANES), F32)],
        scratch_shapes=[pltpu.VMEM((tm, W), F32), pltpu.VMEM((tm, W), F32)],
        compiler_params=_params(("arbitrary",)),
    )(x, dx1, pre, y, mod, gn, w_in, lng, lnb, ws, bst, w_out)


def _mla_lat(x, mod, gn, wd, qg, kvg, cos_t, sin_t, tm, name):
    S, D = x.shape
    LW = wd.shape[1]
    QL, KL = MLA_Q_LORA, MLA_KV_LORA

    def body(x_ref, mod_ref, gn_ref, wd_ref, qg_ref, kvg_ref, c_ref, s_ref, h_ref, lat_ref, cq_ref, ckv_ref, kr_ref):
        n, _ = _rms(x_ref[...])
        a = gn_ref[...] * (1.0 + mod_ref[SC1:SC1 + 1, :])
        h = (n * a + mod_ref[SH1:SH1 + 1, :]).astype(BF16)
        h_ref[...] = h
        lat = _dot(h, wd_ref[...])
        lat_ref[...] = lat
        nq, _ = _rms(lat[:, :QL])
        cq_ref[...] = (nq * qg_ref[...]).astype(BF16)
        nkv, _ = _rms(lat[:, QL:QL + KL])
        ckv_ref[...] = (nkv * kvg_ref[...]).astype(BF16)
        kr = lat[:, QL + KL:]
        kr_ref[...] = (kr * c_ref[...] + _swap_halves(kr) * s_ref[...]).astype(BF16)

    return pl.pallas_call(
        body, name=name, grid=(S // tm,),
        in_specs=[_rows1(tm, D), _resident(mod.shape), _resident(gn.shape), _resident(wd.shape), _resident(qg.shape),
                  _resident(kvg.shape), _rows1(tm, LANES), _rows1(tm, LANES)],
        out_specs=[_rows1(tm, D), _rows1(tm, LW), _rows1(tm, QL), _rows1(tm, KL), _rows1(tm, LANES)],
        out_shape=[SDS((S, D), BF16), SDS((S, LW), F32), SDS((S, QL), BF16), SDS((S, KL), BF16), SDS((S, LANES), BF16)],
        compiler_params=_params(("parallel",)),
    )(x, mod, gn, wd, qg, kvg, cos_t, sin_t)


def _mla_qkv(cq, ckv, krp, wq, wukv, cos_t, sin_t, tm, name):
    S = cq.shape[0]
    H = wq.shape[1] // MLA_HEAD_PAD
    HP = MLA_HEAD_PAD

    def body(cq_ref, ckv_ref, kr_ref, wq_ref, wkv_ref, c_ref, s_ref, q_ref, k_ref, v_ref):
        q = _dot(cq_ref[...], wq_ref[...])
        kv = _dot(ckv_ref[...], wkv_ref[...])
        cv, sv, krv = c_ref[...], s_ref[...], kr_ref[...]
        for h in range(H):
            qr = q[:, h * HP + MLA_NOPE:(h + 1) * HP]
            q_ref[:, h * HP:h * HP + MLA_NOPE] = (q[:, h * HP:h * HP + MLA_NOPE] * SM_SCALE).astype(BF16)
            q_ref[:, h * HP + MLA_NOPE:(h + 1) * HP] = ((qr * cv + _swap_halves(qr) * sv) * SM_SCALE).astype(BF16)
            k_ref[:, h * HP:h * HP + MLA_NOPE] = kv[:, h * HP:h * HP + MLA_NOPE].astype(BF16)
            k_ref[:, h * HP + MLA_NOPE:(h + 1) * HP] = krv
            v_ref[:, h * MLA_V:(h + 1) * MLA_V] = kv[:, h * HP + MLA_NOPE:(h + 1) * HP].astype(BF16)

    return pl.pallas_call(
        body, name=name, grid=(S // tm,),
        in_specs=[_rows1(tm, MLA_Q_LORA), _rows1(tm, MLA_KV_LORA), _rows1(tm, LANES), _resident(wq.shape),
                  _resident(wukv.shape), _rows1(tm, LANES), _rows1(tm, LANES)],
        out_specs=[_rows1(tm, H * HP), _rows1(tm, H * HP), _rows1(tm, H * MLA_V)],
        out_shape=[SDS((S, H * HP), BF16), SDS((S, H * HP), BF16), SDS((S, H * MLA_V), BF16)],
        compiler_params=_params(("parallel",)),
    )(cq, ckv, krp, wq, wukv, cos_t, sin_t)


def _causal_mask(tq):
    row = lax.broadcasted_iota(jnp.int32, (tq, tq), 0)
    col = lax.broadcasted_iota(jnp.int32, (tq, tq), 1)
    return col <= row


def _attn_fwd(q, k, v, tq, name):
    S = q.shape[0]
    HP = MLA_HEAD_PAD
    H = q.shape[1] // HP
    nq = S // tq

    def body(q_ref, k_ref, v_ref, o_ref, lse_ref):
        def q_tile(i, _):
            rows = pl.ds(pl.multiple_of(i * tq, tq), tq)
            qv = q_ref[rows, :]

            def step(j, carry, masked):
                m, l, acc = carry
                krows = pl.ds(pl.multiple_of(j * tq, tq), tq)
                s = _dot_nt(qv, k_ref[krows, :])
                if masked:
                    s = jnp.where(_causal_mask(tq), s, NEG)
                m_new = jnp.maximum(m, jnp.max(s, axis=1, keepdims=True))
                p = jnp.exp(s - m_new)
                alpha = jnp.exp(m - m_new)
                l = alpha * l + jnp.sum(p, axis=1, keepdims=True)
                acc = alpha * acc + _dot(p.astype(BF16), v_ref[krows, :])
                return m_new, l, acc

            init = (jnp.full((tq, 1), NEG, F32), jnp.zeros((tq, 1), F32), jnp.zeros((tq, MLA_V), F32))
            carry = lax.fori_loop(0, i, lambda j, c: step(j, c, False), init)
            m, l, acc = step(i, carry, True)
            o_ref[rows, :] = (acc / l).astype(BF16)
            lse_ref[0, rows, :] = jnp.broadcast_to(m + jnp.log(l), (tq, LANES))
            return 0

        lax.fori_loop(0, nq, q_tile, 0)

    return pl.pallas_call(
        body, name=name, grid=(H,),
        in_specs=[pl.BlockSpec((S, HP), lambda h: (0, h)), pl.BlockSpec((S, HP), lambda h: (0, h)),
                  pl.BlockSpec((S, MLA_V), lambda h: (0, h))],
        out_specs=[pl.BlockSpec((S, MLA_V), lambda h: (0, h)), pl.BlockSpec((1, S, LANES), lambda h: (h, 0, 0))],
        out_shape=[SDS((S, H * MLA_V), BF16), SDS((H, S, LANES), F32)],
        compiler_params=_params(("parallel",)),
    )(q, k, v)


def _attn_bwd(q, k, v, o, do, lse, cos_t, sin_t, tq, name):
    S = q.shape[0]
    HP = MLA_HEAD_PAD
    H = q.shape[1] // HP
    nq = S // tq

    def body(q_ref, k_ref, v_ref, o_ref, do_ref, lse_ref, c_ref, s_ref, dq_ref, dkv_ref, dkr_ref, dq_acc, dl_s):
        @pl.when(pl.program_id(0) == 0)
        def _():
            dkr_ref[...] = jnp.zeros_like(dkr_ref)

        dq_acc[...] = jnp.zeros_like(dq_acc)

        def delta_tile(i, _):
            rows = pl.ds(pl.multiple_of(i * tq, tq), tq)
            d = jnp.sum(do_ref[rows, :].astype(F32) * o_ref[rows, :].astype(F32), axis=1, keepdims=True)
            dl_s[rows, :] = jnp.broadcast_to(d, (tq, LANES))
            return 0

        lax.fori_loop(0, nq, delta_tile, 0)

        def kv_tile(j, _):
            krows = pl.ds(pl.multiple_of(j * tq, tq), tq)
            kv_k = k_ref[krows, :]
            kv_v = v_ref[krows, :]

            def step(i, carry, masked):
                dk, dv = carry
                rows = pl.ds(pl.multiple_of(i * tq, tq), tq)
                qv = q_ref[rows, :]
                dov = do_ref[rows, :]
                s = _dot_nt(qv, kv_k)
                if masked:
                    s = jnp.where(_causal_mask(tq), s, NEG)
                p = jnp.exp(s - lse_ref[0, rows, 0:1])
                dv = dv + _dot_tn(p.astype(BF16), dov)
                dp = _dot_nt(dov, kv_v)
                ds = (p * (dp - dl_s[rows, 0:1])).astype(BF16)
                dk = dk + _dot_tn(ds, qv)
                dq_acc[rows, :] += _dot(ds, kv_k)
                return dk, dv

            carry = step(j, (jnp.zeros((tq, HP), F32), jnp.zeros((tq, MLA_V), F32)), True)
            dk, dv = lax.fori_loop(j + 1, nq, lambda i, c: step(i, c, False), carry)
            dkv_ref[krows, :MLA_NOPE] = dk[:, :MLA_NOPE].astype(BF16)
            dkv_ref[krows, MLA_NOPE:] = dv.astype(BF16)
            dkr_ref[krows, :] += dk[:, MLA_NOPE:]
            return 0

        lax.fori_loop(0, nq, kv_tile, 0)

        def out_tile(i, _):
            rows = pl.ds(pl.multiple_of(i * tq, tq), tq)
            dq = dq_acc[rows, :] * SM_SCALE
            dqr = dq[:, MLA_NOPE:]
            dq_ref[rows, :MLA_NOPE] = dq[:, :MLA_NOPE].astype(BF16)
            dq_ref[rows, MLA_NOPE:] = (dqr * c_ref[rows, :] + _swap_halves(dqr * s_ref[rows, :])).astype(BF16)
            return 0

        lax.fori_loop(0, nq, out_tile, 0)

    return pl.pallas_call(
        body, name=name, grid=(H,),
        in_specs=[pl.BlockSpec((S, HP), lambda h: (0, h)), pl.BlockSpec((S, HP), lambda h: (0, h)),
                  pl.BlockSpec((S, MLA_V), lambda h: (0, h)), pl.BlockSpec((S, MLA_V), lambda h: (0, h)),
                  pl.BlockSpec((S, MLA_V), lambda h: (0, h)), pl.BlockSpec((1, S, LANES), lambda h: (h, 0, 0)),
                  _resident(cos_t.shape), _resident(sin_t.shape)],
        out_specs=[pl.BlockSpec((S, HP), lambda h: (0, h)), pl.BlockSpec((S, HP), lambda h: (0, h)),
                   pl.BlockSpec((S, LANES), lambda h: (0, 0))],
        out_shape=[SDS((S, H * HP), BF16), SDS((S, H * HP), BF16), SDS((S, LANES), F32)],
        scratch_shapes=[pltpu.VMEM((S, HP), F32), pltpu.VMEM((S, LANES), F32)],
        compiler_params=_params(("arbitrary",)),
    )(q, k, v, o, do, lse, cos_t, sin_t)


def _mla_out(o, w_o, x, mod, tm, name):
    S, KO = o.shape
    D = w_o.shape[1]

    def body(o_ref, w_ref, x_ref, mod_ref, x1_ref, y_ref):
        y = _dot(o_ref[...], w_ref[...])
        y_ref[...] = y.astype(BF16)
        x1_ref[...] = x_ref[...] + mod_ref[G1:G1 + 1, :] * y

    return pl.pallas_call(
        body, name=name, grid=(S // tm,),
        in_specs=[_rows1(tm, KO), _resident(w_o.shape), _rows1(tm, D), _resident(mod.shape)],
        out_specs=[_rows1(tm, D), _rows1(tm, D)],
        out_shape=[SDS((S, D), F32), SDS((S, D), BF16)],
        compiler_params=_params(("parallel",)),
    )(o, w_o, x, mod)


def _mla_bwd_o(dx1, y, mod, w_o, tm, name):
    S, D = dx1.shape
    KO = w_o.shape[0]

    def body(d1_ref, y_ref, mod_ref, w_ref, dy_ref, do_ref, st_ref):
        @pl.when(pl.program_id(0) == 0)
        def _():
            st_ref[...] = jnp.zeros_like(st_ref)

        d1 = d1_ref[...]
        dyb = (d1 * mod_ref[G1:G1 + 1, :]).astype(BF16)
        dy_ref[...] = dyb
        st_ref[G1:G1 + 1, :] += _colsum(d1 * y_ref[...].astype(F32))
        do_ref[...] = _dot_nt(dyb, w_ref[...]).astype(BF16)

    return pl.pallas_call(
        body, name=name, grid=(S // tm,),
        in_specs=[_rows1(tm, D), _rows1(tm, D), _resident(mod.shape), _resident(w_o.shape)],
        out_specs=[_rows1(tm, D), _rows1(tm, KO), pl.BlockSpec((8, D), lambda i: (0, 0))],
        out_shape=[SDS((S, D), BF16), SDS((S, KO), BF16), SDS((8, D), F32)],
        compiler_params=_params(("arbitrary",)),
    )(dx1, y, mod, w_o)


def _mla_bwd_lat(dq, dkv, dkr, lat, x, dx1, mod, gn, qg, kvg, wq, wukv, wd, cos_t, sin_t, tm, name):
    S, D = x.shape
    LW = wd.shape[1]
    QL, KL = MLA_Q_LORA, MLA_KV_LORA

    def body(dq_ref, dkv_ref, dkr_ref, lat_ref, x_ref, d1_ref, mod_ref, gn_ref, qg_ref, kvg_ref, wq_ref, wkv_ref, wd_ref,
             c_ref, s_ref, dx_ref, dlat_ref, st_ref, dqg_ref, dkvg_ref):
        @pl.when(pl.program_id(0) == 0)
        def _():
            st_ref[...] = jnp.zeros_like(st_ref)
            dqg_ref[...] = jnp.zeros_like(dqg_ref)
            dkvg_ref[...] = jnp.zeros_like(dkvg_ref)

        lat = lat_ref[...]
        dcq = _dot_nt(dq_ref[...], wq_ref[...])
        nq, rq = _rms(lat[:, :QL])
        dqg_ref[0:1, :] += _colsum(dcq * nq)
        dlat_q = _normmod_bwd(dcq, nq, rq, qg_ref[...]).astype(BF16)
        dckv = _dot_nt(dkv_ref[...], wkv_ref[...])
        nkv, rkv = _rms(lat[:, QL:QL + KL])
        dkvg_ref[0:1, :] += _colsum(dckv * nkv)
        dlat_kv = _normmod_bwd(dckv, nkv, rkv, kvg_ref[...]).astype(BF16)
        dkr = dkr_ref[...]
        dlat_kr = (dkr * c_ref[...] + _swap_halves(dkr * s_ref[...])).astype(BF16)
        dlat_ref[:, :QL] = dlat_q
        dlat_ref[:, QL:QL + KL] = dlat_kv
        dlat_ref[:, QL + KL:] = dlat_kr
        dh = (_dot_nt(dlat_q, wd_ref[:, :QL]) + _dot_nt(dlat_kv, wd_ref[:, QL:QL + KL])
              + _dot_nt(dlat_kr, wd_ref[:, QL + KL:]))
        n, rr = _rms(x_ref[...])
        gn_v = gn_ref[...]
        sc1p = 1.0 + mod_ref[SC1:SC1 + 1, :]
        t = _colsum(dh * n)
        st_ref[SH1:SH1 + 1, :] += _colsum(dh)
        st_ref[SC1:SC1 + 1, :] += t * gn_v
        st_ref[3:4, :] += t * sc1p
        dx_ref[...] = d1_ref[...] + _normmod_bwd(dh, n, rr, gn_v * sc1p)

    HW = wq.shape[1]
    return pl.pallas_call(
        body, name=name, grid=(S // tm,),
        in_specs=[_rows1(tm, HW), _rows1(tm, HW), _rows1(tm, LANES), _rows1(tm, LW), _rows1(tm, D), _rows1(tm, D),
                  _resident(mod.shape), _resident(gn.shape), _resident(qg.shape), _resident(kvg.shape),
                  _resident(wq.shape), _resident(wukv.shape), _resident(wd.shape), _rows1(tm, LANES), _rows1(tm, LANES)],
        out_specs=[_rows1(tm, D), _rows1(tm, LW), pl.BlockSpec((8, D), lambda i: (0, 0)),
                   pl.BlockSpec((8, QL), lambda i: (0, 0)), pl.BlockSpec((8, KL), lambda i: (0, 0))],
        out_shape=[SDS((S, D), F32), SDS((S, LW), BF16), SDS((8, D), F32), SDS((8, QL), F32), SDS((8, KL), F32)],
        compiler_params=_params(("arbitrary",)),
    )(dq, dkv, dkr, lat, x, dx1, mod, gn, qg, kvg, wq, wukv, wd, cos_t, sin_t)


def _loss_head(x, tgt, fg, tm, name):
    S, D = x.shape
    nt = S // tm

    def body(x_ref, t_ref, g_ref, dx_ref, acc_ref):
        i = pl.program_id(0)

        @pl.when(i == 0)
        def _():
            acc_ref[...] = jnp.zeros_like(acc_ref)

        n, rr = _rms(x_ref[...])
        g = g_ref[...]
        err = n * g - t_ref[...]
        acc_ref[1:2, :] += _colsum(err * err) * (0.5 / D)
        dy = err * (1.0 / D)
        acc_ref[0:1, :] += _colsum(dy * n)
        dx_ref[...] = _normmod_bwd(dy, n, rr, g)

        @pl.when(i == nt - 1)
        def _():
            acc_ref[2:3, :] = jnp.broadcast_to(jnp.sum(acc_ref[1:2, :], axis=1, keepdims=True), (1, D))

    return pl.pallas_call(
        body, name=name, grid=(nt,),
        in_specs=[_rows1(tm, D), _rows1(tm, D), _resident(fg.shape)],
        out_specs=[_rows1(tm, D), pl.BlockSpec((8, D), lambda i: (0, 0))],
        out_shape=[SDS((S, D), F32), SDS((8, D), F32)],
        compiler_params=_params(("arbitrary",)),
    )(x, tgt, fg)


def _rope_tables(positions, S):
    inv_freq = ROPE_THETA ** (-jnp.arange(0, MLA_ROPE, 2, dtype=F32) / MLA_ROPE)
    ang = positions.reshape(S, 1).astype(F32) * inv_freq
    cos, sin = jnp.cos(ang), jnp.sin(ang)
    z = jnp.zeros((S, LANES - MLA_ROPE), F32)
    return jnp.concatenate([cos, cos, z], axis=1), jnp.concatenate([-sin, sin, z], axis=1)


def _pad_heads(w, per_head):
    K = w.shape[0]
    H = w.shape[1] // per_head
    w3 = w.reshape(K, H, per_head)
    return jnp.pad(w3, ((0, 0), (0, 0), (0, MLA_HEAD_PAD - per_head))).reshape(K, H * MLA_HEAD_PAD)


def _unpad_heads(w, per_head):
    K = w.shape[0]
    H = w.shape[1] // MLA_HEAD_PAD
    return w.reshape(K, H, MLA_HEAD_PAD)[:, :, :per_head].reshape(K, H * per_head)


def _tiles(S):
    return min(512, S), min(256, S), min(512, S)


def _layer_forward(i, x, mod, gmix, gmlp, w, small, tables):
    if True:
        S, D = x.shape
        tm, tms, tq = _tiles(S)
        cos_t, sin_t = tables
        mod, gmix, gmlp = {i: mod}, {i: gmix}, {i: gmlp}
        kind = i % 3
        sv = {"x": x}
        if kind == 0:
            x1 = _pool_fwd(x, mod[i], gmix[i], w["pool_w"], small["pool_scale"][i // 3], tm, f"pool_fwd_{i}")
        elif kind == 1:
            x1, sv["h"], sv["pre"], sv["y"] = _sgu_fwd(
                x, mod[i], gmix[i], w["sgu_w_in"], small["sgu_ln_g"], small["sgu_ln_b"], small["sgu_w_s"],
                small["sgu_b_s_t"], w["sgu_w_out"], tms, f"sgu_fwd_{i}")
        else:
            sv["h"], sv["lat"], sv["cq"], sv["ckv"], krp = _mla_lat(
                x, mod[i], gmix[i], w["mla_wd"], small["mla_q_norm_g"], small["mla_kv_norm_g"], cos_t, sin_t, tm,
                f"mla_lat_{i}")
            sv["q"], sv["k"], sv["v"] = _mla_qkv(sv["cq"], sv["ckv"], krp, w["mla_wq"], w["mla_w_ukv"], cos_t, sin_t,
                                                 tm, f"mla_qkv_{i}")
            sv["o"], sv["lse"] = _attn_fwd(sv["q"], sv["k"], sv["v"], tq, f"attn_fwd_{i}")
            x1, sv["y"] = _mla_out(sv["o"], w["mla_w_o"], x, mod[i], tm, f"mla_out_{i}")
        sv["x1"] = x1
        Fh = w["mlp_w1"].shape[1]
        sv["h2"], sv["r"] = _mlp_up(x1, mod[i], gmlp[i], w["mlp_w1"], tm, min(2048, Fh), f"mlp_up_{i}")
        x, sv["o2"] = _mlp_down(sv["r"], w["mlp_w2"], x1, mod[i], tm, f"mlp_down_{i}")
        return x, sv


def _layer_backward(i, dx, sv, mod, gmix, gmlp, w, small, tables):
    if True:
        S, D = dx.shape
        tm, tms, tq = _tiles(S)
        cos_t, sin_t = tables
        mod, gmix, gmlp = {i: mod}, {i: gmix}, {i: gmlp}
        kind = i % 3
        sgrads = {}
        Fh = w["mlp_w1"].shape[1]
        g = {}
        d_a, d_o, st_a = _mlp_bwd_a(dx, sv["o2"], mod[i], w["mlp_w2"], sv["r"], tm, min(2048, Fh), f"mlp_bwd_a_{i}")
        g["mlp_w2"] = _mm_tn(sv["r"], d_o, min(512, Fh), D, f"mlp_dw2_{i}", square_a=True)
        g["mlp_w1"] = _mm_tn(sv["h2"], d_a, D, min(512, Fh), f"mlp_dw1_{i}", col_shards=True)
        dx1, st_b = _mlp_bwd_b(d_a, w["mlp_w1"], sv["x1"], dx, mod[i], gmlp[i], tm, f"mlp_bwd_b_{i}")
        if kind == 0:
            dx, st_m, dpw = _pool_bwd(sv["x"], dx1, mod[i], gmix[i], w["pool_w"], small["pool_scale"][i // 3], tm,
                                      f"pool_bwd_{i}")
            g["pool_w"] = dpw
            sgrads[f"pool_scale_{i // 3}"] = st_m[4:5]
        elif kind == 1:
            dx, dyb, gated, dpre, st_m, dws, dbs = _sgu_bwd(
                sv["x"], dx1, sv["pre"], sv["y"], mod[i], gmix[i], w["sgu_w_in"], small["sgu_ln_g"], small["sgu_ln_b"],
                small["sgu_w_s"], small["sgu_b_s_t"], w["sgu_w_out"], tms, f"sgu_bwd_{i}")
            W = gated.shape[1]
            g["sgu_w_out"] = _mm_tn(gated, dyb, min(512, W), D, f"sgu_dwout_{i}")
            g["sgu_w_in"] = _mm_tn(sv["h"], dpre, D, min(512, 2 * W), f"sgu_dwin_{i}", col_shards=True)
            sgrads["sgu_ln_g"], sgrads["sgu_ln_b"] = st_m[4:5], st_m[5:6]
            sgrads["sgu_w_s"], sgrads["sgu_b_s"] = dws, dbs[:, :, 0]
        else:
            dyb, do, st_o = _mla_bwd_o(dx1, sv["y"], mod[i], w["mla_w_o"], tm, f"mla_bwd_o_{i}")
            KO = do.shape[1]
            g["mla_w_o"] = _mm_tn(sv["o"], dyb, min(512, KO), D, f"mla_dwo_{i}")
            dq, dkv, dkr = _attn_bwd(sv["q"], sv["k"], sv["v"], sv["o"], do, sv["lse"], cos_t, sin_t, tq, f"attn_bwd_{i}")
            dx, dlat, st_m, dqg, dkvg = _mla_bwd_lat(
                dq, dkv, dkr, sv["lat"], sv["x"], dx1, mod[i], gmix[i], small["mla_q_norm_g"], small["mla_kv_norm_g"],
                w["mla_wq"], w["mla_w_ukv"], w["mla_wd"], cos_t, sin_t, tm, f"mla_bwd_lat_{i}")
            HW = dq.shape[1]
            g["mla_wq"] = _mm_tn(sv["cq"], dq, MLA_Q_LORA, min(1024, HW), f"mla_dwq_{i}")
            g["mla_w_ukv"] = _mm_tn(sv["ckv"], dkv, MLA_KV_LORA, min(1024, HW), f"mla_dwukv_{i}", col_shards=True)
            g["mla_wd"] = _mm_tn(sv["h"], dlat, D, dlat.shape[1], f"mla_dwd_{i}")
            st_m = jnp.concatenate([st_m[0:2], st_o[2:3], st_m[3:]], axis=0)
            sgrads["mla_q_norm_g"], sgrads["mla_kv_norm_g"] = dqg[0:1], dkvg[0:1]
        stats = jnp.concatenate([st_m[0:3], st_b[3:5], st_a[5:6], st_m[3:4], st_b[6:7]], axis=0)
        return dx, stats, g, sgrads


def _local_step(x, tgt, positions, mod, gmix, gmlp, fg, wts, small):
    S = x.shape[0]
    L = mod.shape[0]
    tables = _rope_tables(positions, S)
    saved = []
    for i in range(L):
        x, sv = _layer_forward(i, x, mod[i], gmix[i], gmlp[i], wts[i], small, tables)
        saved.append(sv)
    dx, loss_acc = _loss_head(x, tgt, fg, _tiles(S)[0], "loss_head")
    stats, grads, sgrads = [None] * L, [None] * L, {}
    for i in reversed(range(L)):
        dx, stats[i], grads[i], sg = _layer_backward(i, dx, saved[i], mod[i], gmix[i], gmlp[i], wts[i], small, tables)
        sgrads.update(sg)
    return loss_acc, dx, stats, grads, sgrads


HBM_SPEC = pl.BlockSpec(memory_space=pltpu.HBM)
VMEM_SPEC = pl.BlockSpec(memory_space=pltpu.VMEM)


def _my_place():
    return lax.axis_index("x"), lax.axis_index("y"), lax.axis_index("c")


def _flip(v, bit):
    return 1 - v if bit else v


def _small_all_gather(v, name):
    R, C = v.shape

    def body(x_ref, out_ref, send_sems, recv_sems):
        x, y, c = _my_place()
        me = 4 * x + 2 * y + c
        out_ref[me] = x_ref[...]
        sends = []
        for k in range(1, NDEV):
            peer = (_flip(x, k & 4), _flip(y, k & 2), _flip(c, k & 1))
            cp = pltpu.make_async_remote_copy(src_ref=x_ref, dst_ref=out_ref.at[me], send_sem=send_sems.at[k - 1],
                                              recv_sem=recv_sems.at[k - 1], device_id=peer, device_id_type=MESH)
            cp.start()
            sends.append(cp)
        for k in range(1, NDEV):
            src = 4 * _flip(x, k & 4) + 2 * _flip(y, k & 2) + _flip(c, k & 1)
            pltpu.make_async_remote_copy(src_ref=x_ref, dst_ref=out_ref.at[src], send_sem=send_sems.at[k - 1],
                                         recv_sem=recv_sems.at[k - 1], device_id=(x, y, c), device_id_type=MESH).wait_recv()
        for cp in sends:
            cp.wait_send()

    return pl.pallas_call(
        body, name=name, out_shape=SDS((NDEV, R, C), v.dtype), in_specs=[VMEM_SPEC], out_specs=VMEM_SPEC,
        scratch_shapes=[pltpu.SemaphoreType.DMA((NDEV - 1,)), pltpu.SemaphoreType.DMA((NDEV - 1,))],
        compiler_params=pltpu.CompilerParams(vmem_limit_bytes=V7X_VMEM_LIMIT),
    )(v)


def _slab(ref, axis, width, dev):
    idx = [slice(None)] * len(ref.shape)
    idx[axis] = pl.ds(pl.multiple_of(dev * width, width), width)
    return ref.at[tuple(idx)]


def _all_gather_group(shards, axes, name):
    nt = len(shards)
    out_shapes = [SDS(tuple(s * NDEV if a == ax else s for a, s in enumerate(sh.shape)), sh.dtype)
                  for sh, ax in zip(shards, axes)]

    def body(*refs):
        ins, outs = refs[:nt], refs[nt:2 * nt]
        send_sems, recv_sems, local_sems = refs[2 * nt:]
        x, y, c = _my_place()
        me = 4 * x + 2 * y + c
        sibling = (x, y, 1 - c)
        chips = [(1 - x, y), (x, 1 - y), (1 - x, 1 - y)]

        def block(t, dev):
            return _slab(outs[t], axes[t], ins[t].shape[axes[t]], dev)

        def copy(t, k, dev, to, src=None):
            return pltpu.make_async_remote_copy(
                src_ref=block(t, dev) if src is None else src, dst_ref=block(t, dev), send_sem=send_sems.at[t, k],
                recv_sem=recv_sems.at[t, k], device_id=to, device_id_type=MESH)

        mine = [pltpu.make_async_copy(ins[t], block(t, me), local_sems.at[t]) for t in range(nt)]
        for cp in mine:
            cp.start()
        first = []
        for t in range(nt):
            first.append(copy(t, 0, me, sibling, src=ins[t]))
            first += [copy(t, 1 + j, me, (cx, cy, c), src=ins[t]) for j, (cx, cy) in enumerate(chips)]
        for cp in first:
            cp.start()
        passed = []
        for j, (cx, cy) in enumerate(chips):
            for t in range(nt):
                copy(t, 1 + j, 4 * cx + 2 * cy + c, (x, y, c)).wait_recv()
                cp = copy(t, 4 + j, 4 * cx + 2 * cy + c, sibling)
                cp.start()
                passed.append(cp)
        for t in range(nt):
            copy(t, 0, 4 * x + 2 * y + (1 - c), (x, y, c)).wait_recv()
        for j, (cx, cy) in enumerate(chips):
            for t in range(nt):
                copy(t, 4 + j, 4 * cx + 2 * cy + (1 - c), (x, y, c)).wait_recv()
        for cp in first + passed:
            cp.wait_send()
        for cp in mine:
            cp.wait()

    return pl.pallas_call(
        body, name=name, out_shape=out_shapes, in_specs=[HBM_SPEC] * nt, out_specs=[HBM_SPEC] * nt,
        scratch_shapes=[pltpu.SemaphoreType.DMA((nt, NDEV - 1)), pltpu.SemaphoreType.DMA((nt, NDEV - 1)),
                        pltpu.SemaphoreType.DMA((nt,))],
    )(*shards)


def _reduce_scatter_sibling(grads, name):
    nt = len(grads)
    NCH = NDEV // 2
    out_shapes = [SDS((NCH,) + gr.shape[1:], gr.dtype) for gr in grads]

    def body(*refs):
        ins, lands = refs[:nt], refs[nt:2 * nt]
        send_sems, recv_sems = refs[2 * nt:]
        x, y, c = _my_place()
        sends = []
        for t in range(nt):
            for k in range(NCH):
                cp = pltpu.make_async_remote_copy(
                    src_ref=ins[t].at[2 * k + (1 - c)], dst_ref=lands[t].at[k], send_sem=send_sems.at[t, k],
                    recv_sem=recv_sems.at[t, k], device_id=(x, y, 1 - c), device_id_type=MESH)
                cp.start()
                sends.append(cp)
        for cp in sends:
            cp.wait_recv()
        for cp in sends:
            cp.wait_send()

    return pl.pallas_call(
        body, name=name, out_shape=out_shapes, in_specs=[HBM_SPEC] * nt, out_specs=[HBM_SPEC] * nt,
        scratch_shapes=[pltpu.SemaphoreType.DMA((nt, NCH)), pltpu.SemaphoreType.DMA((nt, NCH))],
    )(*grads)


SEM_SPEC = pl.BlockSpec(memory_space=pltpu.SEMAPHORE)
ANY_SPEC = pl.BlockSpec(memory_space=pl.ANY)
SPLIT_PARAMS = pltpu.CompilerParams(has_side_effects=pltpu.SideEffectType.DATAFLOW_SIDE_EFFECTING)
TOKEN = SDS((8, LANES), F32)


def _in_hbm(arrays):
    return [pltpu.with_memory_space_constraint(v, pltpu.HBM) for v in arrays]


def _split_start(body, srcs, lands, after, n_sem, name):
    ns, nl = len(srcs), len(lands)
    bufs = list(srcs) + list(lands)
    res = pl.pallas_call(
        body, name=name,
        out_shape=(pltpu.SemaphoreType.DMA((ns * n_sem,)), pltpu.SemaphoreType.DMA((ns * n_sem,)),
                   *[pltpu.HBM(v.shape, v.dtype) for v in bufs], TOKEN),
        in_specs=[HBM_SPEC] * (ns + nl) + [ANY_SPEC],
        out_specs=(SEM_SPEC, SEM_SPEC, *[HBM_SPEC] * (ns + nl), VMEM_SPEC),
        input_output_aliases={t: 2 + t for t in range(ns + nl)}, compiler_params=SPLIT_PARAMS,
    )(*_in_hbm(bufs), after)
    return res[0], res[1], list(res[2:2 + ns]), list(res[2 + ns:2 + ns + nl]), res[-1]


def _split_wait(body, send_sems, recv_sems, srcs, lands, after, name):
    ns, nl = len(srcs), len(lands)
    bufs = list(srcs) + list(lands)
    res = pl.pallas_call(
        body, name=name, out_shape=tuple(pltpu.HBM(v.shape, v.dtype) for v in bufs),
        in_specs=[HBM_SPEC] * (ns + nl) + [SEM_SPEC, SEM_SPEC, ANY_SPEC], out_specs=tuple([HBM_SPEC] * (ns + nl)),
        input_output_aliases={t: t for t in range(ns + nl)}, compiler_params=SPLIT_PARAMS,
    )(*bufs, send_sems, recv_sems, after)
    return list(res[:ns]), list(res[ns:])


def _chips_exchange_start(parts, after, name):
    nt = len(parts)
    lands = [lax.empty((3,) + p.shape[1:], p.dtype) for p in parts]

    def body(*refs):
        ins, lnd = refs[:nt], refs[nt:2 * nt]
        send_sems, recv_sems, token = refs[2 * nt + 1], refs[2 * nt + 2], refs[-1]
        x, y, c = _my_place()
        for t in range(nt):
            for m in range(1, 4):
                px, py = _flip(x, m & 2), _flip(y, m & 1)
                pltpu.make_async_remote_copy(
                    src_ref=ins[t].at[2 * px + py], dst_ref=lnd[t].at[m - 1], send_sem=send_sems.at[3 * t + m - 1],
                    recv_sem=recv_sems.at[3 * t + m - 1], device_id=(px, py, c), device_id_type=MESH).start()
        token[...] = jnp.zeros_like(token)

    return _split_start(body, parts, lands, after, 3, name)


def _chips_exchange_wait(send_sems, recv_sems, parts, lands, after, name):
    nt = len(parts)

    def body(*refs):
        ins, lnd = refs[:nt], refs[nt:2 * nt]
        s_sems, r_sems = refs[2 * nt], refs[2 * nt + 1]
        x, y, c = _my_place()
        for t in range(nt):
            for m in range(1, 4):
                cp = pltpu.make_async_remote_copy(
                    src_ref=ins[t].at[0], dst_ref=lnd[t].at[m - 1], send_sem=s_sems.at[3 * t + m - 1],
                    recv_sem=r_sems.at[3 * t + m - 1], device_id=(x, y, c), device_id_type=MESH)
                cp.wait_send()
                cp.wait_recv()

    return _split_wait(body, send_sems, recv_sems, parts, lands, after, name)


def _gather_start(shards, axes, me, after, name):
    nt = len(shards)
    fulls = []
    for sh, ax in zip(shards, axes):
        shape = tuple(s * NDEV if a == ax else s for a, s in enumerate(sh.shape))
        start = [me * sh.shape[ax] if a == ax else 0 for a in range(sh.ndim)]
        fulls.append(lax.dynamic_update_slice(lax.empty(shape, sh.dtype), sh, start))

    def body(*refs):
        ins, outs = refs[:nt], refs[nt:2 * nt]
        send_sems, recv_sems, token = refs[2 * nt + 1], refs[2 * nt + 2], refs[-1]
        x, y, c = _my_place()
        dev = 4 * x + 2 * y + c
        peers = [(x, y, 1 - c), (1 - x, y, c), (x, 1 - y, c), (1 - x, 1 - y, c)]
        for t in range(nt):
            dst = _slab(outs[t], axes[t], ins[t].shape[axes[t]], dev)
            for k, peer in enumerate(peers):
                pltpu.make_async_remote_copy(src_ref=ins[t], dst_ref=dst, send_sem=send_sems.at[4 * t + k],
                                             recv_sem=recv_sems.at[4 * t + k], device_id=peer, device_id_type=MESH).start()
        token[...] = jnp.zeros_like(token)

    return _split_start(body, shards, fulls, after, 4, name)


def _gather_wait(send_sems, recv_sems, shards, fulls, axes, after, name):
    nt = len(shards)

    def body(*refs):
        ins, outs = refs[:nt], refs[nt:2 * nt]
        s_sems, r_sems = refs[2 * nt], refs[2 * nt + 1]
        x, y, c = _my_place()
        senders = [4 * x + 2 * y + (1 - c), 4 * (1 - x) + 2 * y + c, 4 * x + 2 * (1 - y) + c, 4 * (1 - x) + 2 * (1 - y) + c]
        for t in range(nt):
            for k, src_dev in enumerate(senders):
                cp = pltpu.make_async_remote_copy(
                    src_ref=ins[t], dst_ref=_slab(outs[t], axes[t], ins[t].shape[axes[t]], src_dev),
                    send_sem=s_sems.at[4 * t + k], recv_sem=r_sems.at[4 * t + k], device_id=(x, y, c), device_id_type=MESH)
                cp.wait_send()
                cp.wait_recv()

    return _split_wait(body, send_sems, recv_sems, shards, fulls, after, name)[1]


def _gather_pass_on(fulls, axes, name):
    nt = len(fulls)

    def body(*refs):
        outs = refs[nt:2 * nt]
        send_sems, recv_sems = refs[2 * nt:]
        x, y, c = _my_place()
        chips = [(1 - x, y), (x, 1 - y), (1 - x, 1 - y)]

        def copy(t, j, pc):
            cx, cy = chips[j]
            blk = _slab(outs[t], axes[t], outs[t].shape[axes[t]] // NDEV, 4 * cx + 2 * cy + pc)
            return pltpu.make_async_remote_copy(src_ref=blk, dst_ref=blk, send_sem=send_sems.at[t, j],
                                                recv_sem=recv_sems.at[t, j], device_id=(x, y, 1 - c), device_id_type=MESH)

        sends = [copy(t, j, c) for t in range(nt) for j in range(3)]
        for cp in sends:
            cp.start()
        for t in range(nt):
            for j in range(3):
                copy(t, j, 1 - c).wait_recv()
        for cp in sends:
            cp.wait_send()

    return pl.pallas_call(
        body, name=name, out_shape=[SDS(f.shape, f.dtype) for f in fulls], in_specs=[HBM_SPEC] * nt,
        out_specs=[HBM_SPEC] * nt, input_output_aliases={t: t for t in range(nt)},
        scratch_shapes=[pltpu.SemaphoreType.DMA((nt, 3)), pltpu.SemaphoreType.DMA((nt, 3))],
    )(*fulls)


def _row_tile(R, C, itemsize=4, target=1 << 20):
    best = R
    for tr in range(8, R, 8):
        if R % tr == 0 and tr * C * itemsize <= target:
            best = tr
    return best if best * C * itemsize <= target or best == R else R


def _as2d(a):
    return a.reshape(-1, a.shape[-1])


def _add_pairs(a, b, name):
    shp = a.shape
    a2, b2 = _as2d(a), _as2d(b)
    R, C = a2.shape
    tr = _row_tile(R, C, 2)

    def body(a_ref, b_ref, o_ref):
        o_ref[...] = (a_ref[...].astype(F32) + b_ref[...].astype(F32)).astype(o_ref.dtype)

    out = pl.pallas_call(
        body, name=name, grid=(R // tr,), in_specs=[_rows1(tr, C), _rows1(tr, C)], out_specs=_rows1(tr, C),
        out_shape=SDS((R, C), a.dtype), compiler_params=_params(("parallel",)),
    )(a2, b2)
    return out.reshape(shp)


def _adamw_math(g, w, m, v):
    m2 = ADAM_B1 * m + (1.0 - ADAM_B1) * g
    v2 = ADAM_B2 * v + (1.0 - ADAM_B2) * (g * g)
    m_hat = m2 / (1.0 - ADAM_B1 ** ADAM_STEP)
    v_hat = v2 / (1.0 - ADAM_B2 ** ADAM_STEP)
    delta = -ADAM_LR * (m_hat / (jnp.sqrt(v_hat) + ADAM_EPS) + ADAM_WD * w)
    return delta, m2, v2


def _adamw(parts, w, m, v, name):
    shp = w.shape
    w2, m2, v2 = _as2d(w), _as2d(m), _as2d(v)
    R, C = w2.shape
    tr = _row_tile(R, C)
    p3 = [p.reshape((-1, R, C)) for p in parts]
    npart = len(p3)

    def body(*refs):
        prefs = refs[:npart]
        w_ref, m_ref, v_ref, g_ref, d_ref, nm_ref, nv_ref = refs[npart:]
        g = None
        for pr in prefs:
            for k in range(pr.shape[0]):
                term = pr[k].astype(F32)
                g = term if g is None else g + term
        g_ref[...] = g
        d_ref[...], nm_ref[...], nv_ref[...] = _adamw_math(g, w_ref[...], m_ref[...], v_ref[...])

    outs = pl.pallas_call(
        body, name=name, grid=(R // tr,),
        in_specs=[pl.BlockSpec((p.shape[0], tr, C), lambda i: (0, i, 0)) for p in p3] + [_rows1(tr, C)] * 3,
        out_specs=[_rows1(tr, C)] * 4, out_shape=[SDS((R, C), F32)] * 4,
        compiler_params=_params(("parallel",)),
    )(*p3, w2, m2, v2)
    return [o.reshape(shp) for o in outs]


def _ada_fwd(c_all, ada_w, ada_b_mine, name):
    L, D, Wc = ada_w.shape

    def body(c_ref, w_ref, b_ref, o_ref):
        cv = c_ref[...]
        act = cv * (1.0 / (1.0 + jnp.exp(-cv)))
        o_ref[0] = jnp.dot(act, w_ref[0], preferred_element_type=F32, precision=lax.Precision.HIGHEST) + b_ref[0]

    return pl.pallas_call(
        body, name=name, grid=(L,),
        in_specs=[_resident(c_all.shape), pl.BlockSpec((1, D, Wc), lambda l: (l, 0, 0)), pl.BlockSpec((1, 1, Wc), lambda l: (l, 0, 0))],
        out_specs=pl.BlockSpec((1, NDEV, Wc), lambda l: (l, 0, 0)), out_shape=SDS((L, NDEV, Wc), F32),
        compiler_params=_params(("parallel",)),
    )(c_all, ada_w, ada_b_mine.reshape(L, 1, Wc))


def _ada_bwd(c_all, dmod_mine, name):
    L, _, Wc = dmod_mine.shape
    D = c_all.shape[1]

    def body(c_ref, d_ref, o_ref):
        cv = c_ref[...]
        act = cv * (1.0 / (1.0 + jnp.exp(-cv)))
        o_ref[0] = lax.dot_general(act, d_ref[0], (((0,), (0,)), ((), ())), preferred_element_type=F32,
                                   precision=lax.Precision.HIGHEST)

    return pl.pallas_call(
        body, name=name, grid=(L,),
        in_specs=[_resident(c_all.shape), pl.BlockSpec((1, NDEV, Wc), lambda l: (l, 0, 0))],
        out_specs=pl.BlockSpec((1, D, Wc), lambda l: (l, 0, 0)), out_shape=SDS((L, D, Wc), F32),
        compiler_params=_params(("parallel",)),
    )(c_all, dmod_mine)


WEIGHT_NAMES = ['ada_w', 'ada_b', 'norm_mix_g', 'norm_mlp_g', 'pool_w', 'pool_scale', 'sgu_w_in', 'sgu_ln_g', 'sgu_ln_b',
                'sgu_w_s', 'sgu_b_s', 'sgu_w_out', 'mla_w_dq_dkv', 'mla_q_norm_g', 'mla_kv_norm_g', 'mla_w_uq', 'mla_w_ukv',
                'mla_w_o', 'mlp_w1', 'mlp_w2', 'final_g']
REPLICATED = ['ada_b', 'norm_mix_g', 'norm_mlp_g', 'sgu_ln_g', 'sgu_ln_b', 'sgu_w_s', 'sgu_b_s', 'mla_kv_norm_g', 'final_g']
PACK_ROWS = 64
Q_HEAD = MLA_NOPE + MLA_ROPE


def _layer_matrices(i):
    kind, j = i % 3, i // 3
    if kind == 0:
        mats = [("pool_w", j, 1)]
    elif kind == 1:
        mats = [("sgu_w_in", j, 1), ("sgu_w_out", j, 0)]
    else:
        mats = [("mla_w_dq_dkv", j, 0), ("mla_w_uq", j, 1), ("mla_w_ukv", j, 1), ("mla_w_o", j, 0)]
    return mats + [("mlp_w1", i, 1), ("mlp_w2", i, 0)]


def _pack(arrays):
    flat = jnp.concatenate([a.reshape(-1).astype(F32) for a in arrays])
    rows = -(-flat.size // (LANES * PACK_ROWS)) * PACK_ROWS
    return jnp.pad(flat, (0, rows * LANES - flat.size)).reshape(rows, LANES)


def kernel(x, c, positions, ada_w, ada_b, norm_mix_g, norm_mlp_g, pool_w, pool_scale, sgu_w_in, sgu_ln_g, sgu_ln_b, sgu_w_s, sgu_b_s, sgu_w_out, mla_w_dq_dkv, mla_q_norm_g, mla_kv_norm_g, mla_w_uq, mla_w_ukv, mla_w_o, mlp_w1, mlp_w2, final_g, loss_target, m_ada_w, m_ada_b, m_norm_mix_g, m_norm_mlp_g, m_pool_w, m_pool_scale, m_sgu_w_in, m_sgu_ln_g, m_sgu_ln_b, m_sgu_w_s, m_sgu_b_s, m_sgu_w_out, m_mla_w_dq_dkv, m_mla_q_norm_g, m_mla_kv_norm_g, m_mla_w_uq, m_mla_w_ukv, m_mla_w_o, m_mlp_w1, m_mlp_w2, m_final_g, v_ada_w, v_ada_b, v_norm_mix_g, v_norm_mlp_g, v_pool_w, v_pool_scale, v_sgu_w_in, v_sgu_ln_g, v_sgu_ln_b, v_sgu_w_s, v_sgu_b_s, v_sgu_w_out, v_mla_w_dq_dkv, v_mla_q_norm_g, v_mla_kv_norm_g, v_mla_w_uq, v_mla_w_ukv, v_mla_w_o, v_mlp_w1, v_mlp_w2, v_final_g):
    a = dict(locals())
    S, D = x.shape[1], x.shape[2]
    L = ada_w.shape[0]
    Wc = ada_w.shape[2]
    me = 4 * lax.axis_index("x") + 2 * lax.axis_index("y") + lax.axis_index("c")
    my_chip = 2 * lax.axis_index("x") + lax.axis_index("y")

    v0 = _pack([c, pool_scale, mla_q_norm_g])
    g0 = _small_all_gather(v0, "gather_c").reshape(NDEV, -1)
    n_ps, n_qg = pool_scale.size, mla_q_norm_g.size
    c_all = g0[:, :D]
    ps_w = pool_scale.shape[1]
    ps_full = g0[:, D:D + n_ps].reshape(NDEV, -1, ps_w).transpose(1, 0, 2).reshape(-1, 1, D)
    qg_full = g0[:, D + n_ps:D + n_ps + n_qg].reshape(1, -1)

    ada_b_mine = lax.dynamic_slice_in_dim(ada_b, me * Wc, Wc, axis=1)
    modp = _ada_fwd(c_all, ada_w, ada_b_mine, "ada_fwd")
    ga = _small_all_gather(modp.reshape(-1, LANES), "gather_mod").reshape(NDEV, L, NDEV, Wc)
    mod = lax.dynamic_index_in_dim(ga, me, axis=2, keepdims=False).transpose(1, 0, 2).reshape(L, 6, D)
    mod8 = jnp.pad(mod, ((0, 0), (0, 2), (0, 0)))

    small = {"pool_scale": ps_full, "sgu_ln_g": sgu_ln_g, "sgu_ln_b": sgu_ln_b, "sgu_w_s": sgu_w_s[0],
             "sgu_b_s_t": sgu_b_s[0].T, "mla_q_norm_g": qg_full, "mla_kv_norm_g": mla_kv_norm_g}
    gmix, gmlp = norm_mix_g.reshape(L, 1, D), norm_mlp_g.reshape(L, 1, D)
    tables = _rope_tables(positions, S)

    def shards_of(i):
        mats = _layer_matrices(i)
        return [a[n][j].astype(BF16) for n, j, _ in mats], [ax for _, _, ax in mats]

    def as_weights(i, fulls):
        w = {n: f for (n, _, _), f in zip(_layer_matrices(i), fulls)}
        if "mla_w_uq" in w:
            lat_w = w["mla_w_dq_dkv"].shape[1]
            w["mla_wd"] = jnp.pad(w.pop("mla_w_dq_dkv"), ((0, 0), (0, -lat_w % LANES)))
            w["mla_wq"] = _pad_heads(w.pop("mla_w_uq"), Q_HEAD)
        return w

    xc = x[0]
    wts, saved, flying = [], [], None
    for i in range(L):
        shards, axes = shards_of(i)
        if flying is None:
            fulls = _all_gather_group(shards, axes, f"gather_w_{i}")
        else:
            fulls = _gather_wait(*flying[:4], axes, xc, f"gather_wait_{i}")
            fulls = _gather_pass_on(fulls, axes, f"gather_pass_{i}")
        wts.append(as_weights(i, fulls))
        mod_i = mod8[i]
        if i + 1 < L:
            flying = _gather_start(*shards_of(i + 1), me, fulls[0], f"gather_start_{i + 1}")
            mod_i = mod_i + flying[4][0, 0]
        xc, sv = _layer_forward(i, xc, mod_i, gmix[i], gmlp[i], wts[i], small, tables)
        saved.append(sv)
    dx, loss_acc = _loss_head(xc, loss_target[0], final_g.reshape(1, D), _tiles(S)[0], "loss_head")

    res = {n: [None] * a[n].shape[0] for n in WEIGHT_NAMES if a[n].ndim > 1}
    c_me = lax.axis_index("c")

    def start_reduce(i, g, after):
        mats = _layer_matrices(i)
        g = dict(g)
        if "mla_wq" in g:
            g["mla_w_dq_dkv"] = g.pop("mla_wd")[:, :mla_w_dq_dkv.shape[2]]
            g["mla_w_uq"] = _unpad_heads(g.pop("mla_wq"), Q_HEAD)
        gl = []
        for n, j, ax in mats:
            gm, blk = g[n].astype(BF16), a[n][j].shape
            if gm.shape != (NDEV,) + blk:
                gm = jnp.moveaxis(gm.reshape(blk[:ax] + (NDEV,) + blk[ax:]), ax, 0)
            gl.append(gm)
        lands = _reduce_scatter_sibling(gl, f"rs_sibling_{i}")
        owns = [lax.dynamic_index_in_dim(gm.reshape((NDEV // 2, 2) + gm.shape[1:]), c_me, axis=1, keepdims=False) for gm in gl]
        parts = [_add_pairs(o, l, f"rs_add_{i}_{n}") for o, l, (n, _, _) in zip(owns, lands, mats)]
        return _chips_exchange_start(parts, after, f"rs_chips_start_{i}")

    def finish_reduce(i, fly, after):
        parts, recv = _chips_exchange_wait(*fly[:4], after, f"rs_chips_wait_{i}")
        for (n, j, _), p, r in zip(_layer_matrices(i), parts, recv):
            p_own = lax.dynamic_index_in_dim(p, my_chip, axis=0, keepdims=False)
            res[n][j] = _adamw([p_own, r], a[n][j], a["m_" + n][j], a["v_" + n][j], f"adamw_{n}_{j}")

    stats, sgrads, flying = [None] * L, {}, None
    for i in reversed(range(L)):
        mod_i = mod8[i] if flying is None else mod8[i] + flying[4][0, 0]
        dx, stats[i], g, sgr = _layer_backward(i, dx, saved[i], mod_i, gmix[i], gmlp[i], wts[i], small, tables)
        sgrads.update(sgr)
        if flying is not None:
            finish_reduce(i + 1, flying, dx)
        flying = start_reduce(i, g, dx)

    sg = {"ada_b": jnp.stack([s[0:6] for s in stats]), "norm_mix_g": jnp.stack([s[6] for s in stats]),
          "norm_mlp_g": jnp.stack([s[7] for s in stats]), "final_g": loss_acc[0], "sgu_ln_g": sgrads["sgu_ln_g"],
          "sgu_ln_b": sgrads["sgu_ln_b"], "sgu_w_s": sgrads["sgu_w_s"], "sgu_b_s": sgrads["sgu_b_s"],
          "mla_kv_norm_g": sgrads["mla_kv_norm_g"]}
    ps_grad = jnp.concatenate([sgrads[f"pool_scale_{j}"] for j in range(pool_scale.shape[0])])
    tail = [ps_grad, sgrads["mla_q_norm_g"], loss_acc[2, :LANES]]
    packed = _pack([sg[n] for n in REPLICATED] + tail) + flying[4][0, 0]
    gathered = _small_all_gather(packed, "gather_small")
    zeros_tail = [jnp.zeros_like(t) for t in tail]
    g_p, d_p, m_p, v_p = _adamw([gathered], _pack([a[n] for n in REPLICATED] + zeros_tail),
                                _pack([a["m_" + n] for n in REPLICATED] + zeros_tail),
                                _pack([a["v_" + n] for n in REPLICATED] + zeros_tail), "adamw_replicated")
    finish_reduce(0, flying, g_p)
    flat = [t.reshape(-1) for t in (g_p, d_p, m_p, v_p)]
    off = 0
    for n in REPLICATED:
        res[n] = [f[off:off + a[n].size].reshape(a[n].shape) for f in flat]
        off += a[n].size
    g_ps = flat[0][off:off + ps_grad.size].reshape(ps_grad.shape)
    off += ps_grad.size
    g_qg = flat[0][off:off + qg_full.size].reshape(1, -1)
    off += qg_full.size
    loss = flat[0][off]
    res["pool_scale"] = _adamw([lax.dynamic_slice_in_dim(g_ps, me * ps_w, ps_w, axis=1)], pool_scale, m_pool_scale,
                               v_pool_scale, "adamw_pool_scale")
    qg_w = mla_q_norm_g.shape[1]
    res["mla_q_norm_g"] = _adamw([lax.dynamic_slice_in_dim(g_qg, me * qg_w, qg_w, axis=1)], mla_q_norm_g, m_mla_q_norm_g,
                                 v_mla_q_norm_g, "adamw_q_norm_g")

    n_mod = L * 6 * D
    dmod_all = gathered.reshape(NDEV, -1)[:, :n_mod].reshape(NDEV, L, 6 * D)
    dmod_mine = lax.dynamic_slice_in_dim(dmod_all, me * Wc, Wc, axis=2).transpose(1, 0, 2)
    res["ada_w"] = _adamw([_ada_bwd(c_all, dmod_mine, "ada_bwd")], ada_w, m_ada_w, v_ada_w, "adamw_ada_w")

    outs = []
    for k in range(4):
        for n in WEIGHT_NAMES:
            r = res[n]
            outs.append(jnp.stack([lay[k] for lay in r]) if isinstance(r[0], list) else r[k])
    return (loss, dx.reshape(x.shape), *outs)
```

```python
import functools
import math

import jax
import jax.numpy as jnp
import numpy as np
from jax import lax
from jax.experimental import pallas as pl
from jax.experimental.pallas import tpu as pltpu

F32 = jnp.float32
BF16 = jnp.bfloat16
SDS = jax.ShapeDtypeStruct
MESH = pl.DeviceIdType.MESH

NDEV = 8
V7X_VMEM_LIMIT = 56 << 20
LANES = 128
RMS_EPS = 1e-6
LN_EPS = 1e-5
POOL_WINDOWS = (2, 4, 8, 16)
HALO = 16
SGU_CHUNK = 128
SGU_HEAD = 128
MLA_NOPE, MLA_ROPE, MLA_V = 128, 64, 128
MLA_Q_LORA, MLA_KV_LORA = 256, 128
MLA_HEAD_PAD = 256
ROPE_THETA = 10000.0
SM_SCALE = (MLA_NOPE + MLA_ROPE) ** -0.5
NEG = -1e30
ADAM_LR, ADAM_B1, ADAM_B2, ADAM_EPS, ADAM_WD, ADAM_STEP = 0.001, 0.9, 0.999, 1e-08, 0.01, 10
INV_SQRT2 = 1.0 / math.sqrt(2.0)
INV_SQRT_2PI = 1.0 / math.sqrt(2.0 * math.pi)
SH1, SC1, G1, SH2, SC2, G2 = 0, 1, 2, 3, 4, 5


def _params(sem=None, vmem=V7X_VMEM_LIMIT):
    return pltpu.CompilerParams(dimension_semantics=sem, vmem_limit_bytes=vmem)


def _resident(shape):
    nd = len(shape)
    return pl.BlockSpec(shape, lambda *_: (0,) * nd, pipeline_mode=pl.Buffered(1))


def _rows1(tm, w):
    return pl.BlockSpec((tm, w), lambda i: (i, 0))


def _rms(x):
    r = lax.rsqrt(jnp.mean(x * x, axis=-1, keepdims=True) + RMS_EPS)
    return x * r, r


def _colsum(v):
    return jnp.sum(v, axis=0, keepdims=True)


def _normmod_bwd(dh, n, r, a):
    dn = dh * a
    return r * (dn - n * jnp.mean(dn * n, axis=-1, keepdims=True))


def _dot(a, b):
    return jnp.dot(a, b, preferred_element_type=F32)


def _dot_nt(a, b):
    return lax.dot_general(a, b, (((1,), (1,)), ((), ())), preferred_element_type=F32)


def _dot_tn(a, b):
    return lax.dot_general(a, b, (((0,), (0,)), ((), ())), preferred_element_type=F32)


def _gelu(x):
    return 0.5 * x * (1.0 + lax.erf(x * INV_SQRT2))


def _gelu_grad(x):
    return 0.5 * (1.0 + lax.erf(x * INV_SQRT2)) + x * jnp.exp(-0.5 * x * x) * INV_SQRT_2PI


def _swap_halves(v):
    lane = lax.broadcasted_iota(jnp.int32, v.shape, 1)
    half = MLA_ROPE // 2
    return jnp.where(lane < half, pltpu.roll(v, LANES - half, 1),
                     jnp.where(lane < MLA_ROPE, pltpu.roll(v, half, 1), 0.0))


def _mlp_up(x1, mod, gn, w1, tm, tn, name):
    S, D = x1.shape
    Fh = w1.shape[1]

    def body(x_ref, mod_ref, gn_ref, w_ref, h_ref, r_ref):
        n, _ = _rms(x_ref[...])
        a = gn_ref[...] * (1.0 + mod_ref[SC2:SC2 + 1, :])
        h = (n * a + mod_ref[SH2:SH2 + 1, :]).astype(BF16)
        h_ref[...] = h
        for j in range(Fh // tn):
            cols = slice(j * tn, (j + 1) * tn)
            r_ref[:, cols] = jnp.maximum(_dot(h, w_ref[:, cols]), 0.0).astype(BF16)

    return pl.pallas_call(
        body, name=name, grid=(S // tm,),
        in_specs=[_rows1(tm, D), _resident(mod.shape), _resident(gn.shape), _resident(w1.shape)],
        out_specs=[_rows1(tm, D), _rows1(tm, Fh)],
        out_shape=[SDS((S, D), BF16), SDS((S, Fh), BF16)],
        compiler_params=_params(("parallel",)),
    )(x1, mod, gn, w1)


def _mlp_down(r, w2, x1, mod, tm, name):
    S, Fh = r.shape
    D = w2.shape[1]

    def body(r_ref, w_ref, x_ref, mod_ref, x2_ref, o_ref):
        rv = r_ref[...]
        o = _dot(rv * rv, w_ref[...])
        o_ref[...] = o.astype(BF16)
        x2_ref[...] = x_ref[...] + mod_ref[G2:G2 + 1, :] * o

    return pl.pallas_call(
        body, name=name, grid=(S // tm,),
        in_specs=[_rows1(tm, Fh), _resident(w2.shape), _rows1(tm, D), _resident(mod.shape)],
        out_specs=[_rows1(tm, D), _rows1(tm, D)],
        out_shape=[SDS((S, D), F32), SDS((S, D), BF16)],
        compiler_params=_params(("parallel",)),
    )(r, w2, x1, mod)


def _mlp_bwd_a(dx2, o, mod, w2, r, tm, tn, name):
    S, D = dx2.shape
    Fh = r.shape[1]

    def body(dx_ref, o_ref, mod_ref, w_ref, r_ref, da_ref, do_ref, st_ref):
        @pl.when(pl.program_id(0) == 0)
        def _():
            st_ref[...] = jnp.zeros_like(st_ref)

        dx = dx_ref[...]
        d_o = (dx * mod_ref[G2:G2 + 1, :]).astype(BF16)
        do_ref[...] = d_o
        st_ref[G2:G2 + 1, :] += _colsum(dx * o_ref[...].astype(F32))
        for j in range(Fh // tn):
            cols = slice(j * tn, (j + 1) * tn)
            dz = _dot_nt(d_o, w_ref[cols, :])
            da_ref[:, cols] = (dz * (2.0 * r_ref[:, cols].astype(F32))).astype(BF16)

    return pl.pallas_call(
        body, name=name, grid=(S // tm,),
        in_specs=[_rows1(tm, D), _rows1(tm, D), _resident(mod.shape), _resident(w2.shape), _rows1(tm, Fh)],
        out_specs=[_rows1(tm, Fh), _rows1(tm, D), pl.BlockSpec((8, D), lambda i: (0, 0))],
        out_shape=[SDS((S, Fh), BF16), SDS((S, D), BF16), SDS((8, D), F32)],
        compiler_params=_params(("arbitrary",)),
    )(dx2, o, mod, w2, r)


def _mlp_bwd_b(d_a, w1, x1, dx2, mod, gn, tm, name):
    S, Fh = d_a.shape
    D = w1.shape[0]

    def body(da_ref, w_ref, x_ref, dx_ref, mod_ref, gn_ref, dx1_ref, st_ref):
        @pl.when(pl.program_id(0) == 0)
        def _():
            st_ref[...] = jnp.zeros_like(st_ref)

        dh = _dot_nt(da_ref[...], w_ref[...])
        n, rr = _rms(x_ref[...])
        gn_v = gn_ref[...]
        sc1p = 1.0 + mod_ref[SC2:SC2 + 1, :]
        t = _colsum(dh * n)
        st_ref[SH2:SH2 + 1, :] += _colsum(dh)
        st_ref[SC2:SC2 + 1, :] += t * gn_v
        st_ref[6:7, :] += t * sc1p
        dx1_ref[...] = dx_ref[...] + _normmod_bwd(dh, n, rr, gn_v * sc1p)

    return pl.pallas_call(
        body, name=name, grid=(S // tm,),
        in_specs=[_rows1(tm, Fh), _resident(w1.shape), _rows1(tm, D), _rows1(tm, D), _resident(mod.shape),
                  _resident(gn.shape)],
        out_specs=[_rows1(tm, D), pl.BlockSpec((8, D), lambda i: (0, 0))],
        out_shape=[SDS((S, D), F32), SDS((8, D), F32)],
        compiler_params=_params(("arbitrary",)),
    )(d_a, w1, x1, dx2, mod, gn)


def _mm_tn(a, g, tk, tn, name, square_a=False, col_shards=False):
    S, K1 = a.shape
    N = g.shape[1]
    w = N // NDEV
    per = tn // w if col_shards else 1

    def body(a_ref, g_ref, o_ref):
        av = a_ref[...]
        if square_a:
            av = av * av
        res = _dot_tn(av, g_ref[...]).astype(BF16)
        if col_shards:
            for s in range(per):
                o_ref[s] = res[:, s * w:(s + 1) * w]
        else:
            o_ref[...] = res

    if col_shards:
        out_spec, out_shape = pl.BlockSpec((per, tk, w), lambda i, j: (j, i, 0)), SDS((NDEV, K1, w), BF16)
    else:
        out_spec, out_shape = pl.BlockSpec((tk, tn), lambda i, j: (i, j)), SDS((K1, N), BF16)
    return pl.pallas_call(
        body, name=name, grid=(K1 // tk, N // tn),
        in_specs=[pl.BlockSpec((S, tk), lambda i, j: (0, i)), pl.BlockSpec((S, tn), lambda i, j: (0, j))],
        out_specs=out_spec, out_shape=out_shape,
        compiler_params=_params(("parallel", "parallel")),
    )(a, g)


def _pool_h_ext(x_ref, xp_ref, mod_ref, gn_ref, i, tm):
    ext = jnp.concatenate([xp_ref[...], x_ref[...]], axis=0)
    n, r = _rms(ext)
    a = gn_ref[...] * (1.0 + mod_ref[SC1:SC1 + 1, :])
    h = n * a + mod_ref[SH1:SH1 + 1, :]
    row = lax.broadcasted_iota(jnp.int32, (tm + HALO, 1), 0)
    h = jnp.where(jnp.logical_and(i == 0, row < HALO), 0.0, h)
    return h, n[HALO:], r[HALO:], a


def _trailing_sum(v, win):
    k = 1
    while k < win:
        v = v + pltpu.roll(v, k, 0)
        k *= 2
    return v


def _leading_sum(v, win):
    k = 1
    while k < win:
        v = v + pltpu.roll(v, v.shape[0] - k, 0)
        k *= 2
    return v


def _pool_fwd(x, mod, gn, pw, ps, tm, name):
    S, D = x.shape
    C = D // len(POOL_WINDOWS)
    hb = tm // HALO

    def body(x_ref, xp_ref, mod_ref, gn_ref, pw_ref, ps_ref, x1_ref):
        i = pl.program_id(0)
        h, _, _, _ = _pool_h_ext(x_ref, xp_ref, mod_ref, gn_ref, i, tm)
        t1 = (i * tm + lax.broadcasted_iota(jnp.int32, (tm, 1), 0)).astype(F32) + 1.0
        for g, win in enumerate(POOL_WINDOWS):
            cols = slice(g * C, (g + 1) * C)
            hg = h[:, cols]
            inv = 1.0 / jnp.minimum(t1, float(win))
            pooled = (_trailing_sum(hg, win)[HALO:] * inv - hg[HALO:]).astype(BF16)
            y = _dot(pooled, pw_ref[g]) * ps_ref[:, cols]
            x1_ref[:, cols] = x_ref[:, cols] + mod_ref[G1:G1 + 1, cols] * y

    return pl.pallas_call(
        body, name=name, grid=(S // tm,),
        in_specs=[_rows1(tm, D), pl.BlockSpec((HALO, D), lambda i: (jnp.maximum(i * hb - 1, 0), 0)),
                  _resident(mod.shape), _resident(gn.shape), _resident(pw.shape), _resident(ps.shape)],
        out_specs=_rows1(tm, D),
        out_shape=SDS((S, D), F32),
        compiler_params=_params(("parallel",)),
    )(x, x, mod, gn, pw, ps)


def _pool_bwd(x, dx1, mod, gn, pw, ps, tm, name):
    S, D = x.shape
    G = len(POOL_WINDOWS)
    C = D // G
    hb = tm // HALO
    nt = S // tm

    def body(x_ref, xp_ref, d1_ref, dn_ref, mod_ref, gn_ref, pw_ref, ps_ref, dx_ref, st_ref, dpw_ref):
        i = pl.program_id(0)

        @pl.when(i == 0)
        def _():
            st_ref[...] = jnp.zeros_like(st_ref)
            dpw_ref[...] = jnp.zeros_like(dpw_ref)

        h, n, rr, a = _pool_h_ext(x_ref, xp_ref, mod_ref, gn_ref, i, tm)
        g1 = mod_ref[G1:G1 + 1, :]
        ps_v = ps_ref[...]
        d1 = d1_ref[...]
        d1n = jnp.where(i == nt - 1, 0.0, dn_ref[...])
        dyr = (jnp.concatenate([d1, d1n], axis=0) * (g1 * ps_v)).astype(BF16)
        t1 = (i * tm + lax.broadcasted_iota(jnp.int32, (tm + HALO, 1), 0)).astype(F32) + 1.0
        parts = []
        for g, win in enumerate(POOL_WINDOWS):
            cols = slice(g * C, (g + 1) * C)
            hg = h[:, cols]
            inv = 1.0 / jnp.minimum(t1, float(win))
            pooled = (_trailing_sum(hg, win)[HALO:] * inv[:tm] - hg[HALO:]).astype(BF16)
            yraw = _dot(pooled, pw_ref[g])
            st_ref[G1:G1 + 1, cols] += _colsum(d1[:, cols] * (yraw * ps_v[:, cols]))
            st_ref[4:5, cols] += _colsum(d1[:, cols] * g1[:, cols] * yraw)
            dpw_ref[g] += _dot_tn(pooled, dyr[:tm, cols])
            dpool = _dot_nt(dyr[:, cols], pw_ref[g])
            parts.append(_leading_sum(dpool * inv, win)[:tm] - dpool[:tm])
        dh = jnp.concatenate(parts, axis=1)
        t = _colsum(dh * n)
        st_ref[SH1:SH1 + 1, :] += _colsum(dh)
        st_ref[SC1:SC1 + 1, :] += t * gn_ref[...]
        st_ref[3:4, :] += t * (1.0 + mod_ref[SC1:SC1 + 1, :])
        dx_ref[...] = d1 + _normmod_bwd(dh, n, rr, a)

    return pl.pallas_call(
        body, name=name, grid=(nt,),
        in_specs=[_rows1(tm, D), pl.BlockSpec((HALO, D), lambda i: (jnp.maximum(i * hb - 1, 0), 0)),
                  _rows1(tm, D), pl.BlockSpec((HALO, D), lambda i: (jnp.minimum((i + 1) * hb, S // HALO - 1), 0)),
                  _resident(mod.shape), _resident(gn.shape), _resident(pw.shape), _resident(ps.shape)],
        out_specs=[_rows1(tm, D), pl.BlockSpec((8, D), lambda i: (0, 0)), pl.BlockSpec((G, C, C), lambda i: (0, 0, 0))],
        out_shape=[SDS((S, D), F32), SDS((8, D), F32), SDS((G, C, C), F32)],
        compiler_params=_params(("arbitrary",)),
    )(x, x, dx1, dx1, mod, gn, pw, ps)


def _tril_bf16(w):
    row = lax.broadcasted_iota(jnp.int32, w.shape, 0)
    col = lax.broadcasted_iota(jnp.int32, w.shape, 1)
    return jnp.where(col <= row, w, 0.0).astype(BF16)


def _sgu_front(pre, lng_ref, lnb_ref, W):
    z = _gelu(pre)
    u, v = z[:, :W], z[:, W:]
    mu = jnp.mean(v, axis=-1, keepdims=True)
    xc = v - mu
    rstd = lax.rsqrt(jnp.mean(xc * xc, axis=-1, keepdims=True) + LN_EPS)
    vhat = xc * rstd
    return u, vhat, rstd, vhat * lng_ref[...] + lnb_ref[...]


def _sgu_mix(vn, ws_ref, bst_ref, mix_s, tm, W):
    for hd in range(W // SGU_HEAD):
        wm = _tril_bf16(ws_ref[hd])
        bcol = bst_ref[:, hd:hd + 1]
        for ci in range(tm // SGU_CHUNK):
            rs, cs = slice(ci * SGU_CHUNK, (ci + 1) * SGU_CHUNK), slice(hd * SGU_HEAD, (hd + 1) * SGU_HEAD)
            mix_s[rs, cs] = _dot(wm, vn[rs, cs].astype(BF16)) + bcol


def _sgu_fwd(x, mod, gn, w_in, lng, lnb, ws, bst, w_out, tm, name):
    S, D = x.shape
    W = w_out.shape[0]

    def body(x_ref, mod_ref, gn_ref, win_ref, lng_ref, lnb_ref, ws_ref, bst_ref, wout_ref,
             x1_ref, h_ref, pre_ref, y_ref, mix_s):
        n, _ = _rms(x_ref[...])
        a = gn_ref[...] * (1.0 + mod_ref[SC1:SC1 + 1, :])
        h = (n * a + mod_ref[SH1:SH1 + 1, :]).astype(BF16)
        h_ref[...] = h
        pre = _dot(h, win_ref[...])
        pre_ref[...] = pre.astype(BF16)
        u, _, _, vn = _sgu_front(pre, lng_ref, lnb_ref, W)
        _sgu_mix(vn, ws_ref, bst_ref, mix_s, tm, W)
        y = _dot((u * mix_s[...]).astype(BF16), wout_ref[...])
        y_ref[...] = y.astype(BF16)
        x1_ref[...] = x_ref[...] + mod_ref[G1:G1 + 1, :] * y

    return pl.pallas_call(
        body, name=name, grid=(S // tm,),
        in_specs=[_rows1(tm, D), _resident(mod.shape), _resident(gn.shape), _resident(w_in.shape),
                  _resident(lng.shape), _resident(lnb.shape), _resident(ws.shape), _resident(bst.shape),
                  _resident(w_out.shape)],
        out_specs=[_rows1(tm, D), _rows1(tm, D), _rows1(tm, 2 * W), _rows1(tm, D)],
        out_shape=[SDS((S, D), F32), SDS((S, D), BF16), SDS((S, 2 * W), BF16), SDS((S, D), BF16)],
        scratch_shapes=[pltpu.VMEM((tm, W), F32)],
        compiler_params=_params(("parallel",)),
    )(x, mod, gn, w_in, lng, lnb, ws, bst, w_out)


def _sgu_bwd(x, dx1, pre, y, mod, gn, w_in, lng, lnb, ws, bst, w_out, tm, name):
    S, D = x.shape
    W = w_out.shape[0]
    H = W // SGU_HEAD
    nt = S // tm

    def body(x_ref, d1_ref, pre_ref, y_ref, mod_ref, gn_ref, win_ref, lng_ref, lnb_ref, ws_ref, bst_ref, wout_ref,
             dx_ref, dy_ref, gt_ref, dpre_ref, st_ref, dws_ref, dbs_ref, mix_s, dvn_s):
        i = pl.program_id(0)

        @pl.when(i == 0)
        def _():
            st_ref[...] = jnp.zeros_like(st_ref)
            dws_ref[...] = jnp.zeros_like(dws_ref)
            dbs_ref[...] = jnp.zeros_like(dbs_ref)

        d1 = d1_ref[...]
        pre = pre_ref[...].astype(F32)
        u, vhat, rstd, vn = _sgu_front(pre, lng_ref, lnb_ref, W)
        _sgu_mix(vn, ws_ref, bst_ref, mix_s, tm, W)
        mixed = mix_s[...]
        gt_ref[...] = (u * mixed).astype(BF16)
        dyb = (d1 * mod_ref[G1:G1 + 1, :]).astype(BF16)
        dy_ref[...] = dyb
        st_ref[G1:G1 + 1, :] += _colsum(d1 * y_ref[...].astype(F32))
        dgt = _dot_nt(dyb, wout_ref[...])
        du = dgt * mixed
        dmix = dgt * u
        for hd in range(H):
            wm = _tril_bf16(ws_ref[hd])
            for ci in range(tm // SGU_CHUNK):
                rs, cs = slice(ci * SGU_CHUNK, (ci + 1) * SGU_CHUNK), slice(hd * SGU_HEAD, (hd + 1) * SGU_HEAD)
                dm = dmix[rs, cs]
                dmb = dm.astype(BF16)
                dbs_ref[hd] += jnp.broadcast_to(jnp.sum(dm, axis=1, keepdims=True), (SGU_CHUNK, LANES))
                dws_ref[hd] += _dot_nt(dmb, vn[rs, cs].astype(BF16))
                dvn_s[rs, cs] = _dot_tn(wm, dmb)
        dvn = dvn_s[...]
        st_ref[4:5, :] += _colsum(dvn * vhat)
        st_ref[5:6, :] += _colsum(dvn)
        dvh = dvn * lng_ref[...]
        dv = rstd * (dvh - jnp.mean(dvh, axis=-1, keepdims=True) - vhat * jnp.mean(dvh * vhat, axis=-1, keepdims=True))
        dpre_u = (du * _gelu_grad(pre[:, :W])).astype(BF16)
        dpre_v = (dv * _gelu_grad(pre[:, W:])).astype(BF16)
        dpre_ref[:, :W] = dpre_u
        dpre_ref[:, W:] = dpre_v
        dh = _dot_nt(dpre_u, win_ref[:, :W]) + _dot_nt(dpre_v, win_ref[:, W:])
        n, rr = _rms(x_ref[...])
        gn_v = gn_ref[...]
        sc1p = 1.0 + mod_ref[SC1:SC1 + 1, :]
        t = _colsum(dh * n)
        st_ref[SH1:SH1 + 1, :] += _colsum(dh)
        st_ref[SC1:SC1 + 1, :] += t * gn_v
        st_ref[3:4, :] += t * sc1p
        dx_ref[...] = d1 + _normmod_bwd(dh, n, rr, gn_v * sc1p)

        @pl.when(i == nt - 1)
        def _():
            for hd in range(H):
                row = lax.broadcasted_iota(jnp.int32, (SGU_CHUNK, SGU_CHUNK), 0)
                col = lax.broadcasted_iota(jnp.int32, (SGU_CHUNK, SGU_CHUNK), 1)
                dws_ref[hd] = jnp.where(col <= row, dws_ref[hd], 0.0)

    return pl.pallas_call(
        body, name=name, grid=(nt,),
        in_specs=[_rows1(tm, D), _rows1(tm, D), _rows1(tm, 2 * W), _rows1(tm, D), _resident(mod.shape),
                  _resident(gn.shape), _resident(w_in.shape), _resident(lng.shape), _resident(lnb.shape),
                  _resident(ws.shape), _resident(bst.shape), _resident(w_out.shape)],
        out_specs=[_rows1(tm, D), _rows1(tm, D), _rows1(tm, W), _rows1(tm, 2 * W),
                   pl.BlockSpec((8, D), lambda i: (0, 0)), pl.BlockSpec((H, SGU_CHUNK, SGU_CHUNK), lambda i: (0, 0, 0)),
                   pl.BlockSpec((H, SGU_CHUNK, LANES), lambda i: (0, 0, 0))],
        out_shape=[SDS((S, D), F32), SDS((S, D), BF16), SDS((S, W), BF16), SDS((S, 2 * W), BF16),
                   SDS((8, D), F32), SDS((H, SGU_CHUNK, SGU_CHUNK), F32), SDS((H, SGU_CHUNK, LANES), F32)],
        scratch_shapes=[pltpu.VMEM((tm, W), F32), pltpu.VMEM((tm, W), F32)],
        compiler_params=_params(("arbitrary",)),
    )(x, dx1, pre, y, mod, gn, w_in, lng, lnb, ws, bst, w_out)


def _mla_lat(x, mod, gn, wd, qg, kvg, cos_t, sin_t, tm, name):
    S, D = x.shape
    LW = wd.shape[1]
    QL, KL = MLA_Q_LORA, MLA_KV_LORA

    def body(x_ref, mod_ref, gn_ref, wd_ref, qg_ref, kvg_ref, c_ref, s_ref, h_ref, lat_ref, cq_ref, ckv_ref, kr_ref):
        n, _ = _rms(x_ref[...])
        a = gn_ref[...] * (1.0 + mod_ref[SC1:SC1 + 1, :])
        h = (n * a + mod_ref[SH1:SH1 + 1, :]).astype(BF16)
        h_ref[...] = h
        lat = _dot(h, wd_ref[...])
        lat_ref[...] = lat
        nq, _ = _rms(lat[:, :QL])
        cq_ref[...] = (nq * qg_ref[...]).astype(BF16)
        nkv, _ = _rms(lat[:, QL:QL + KL])
        ckv_ref[...] = (nkv * kvg_ref[...]).astype(BF16)
        kr = lat[:, QL + KL:]
        kr_ref[...] = (kr * c_ref[...] + _swap_halves(kr) * s_ref[...]).astype(BF16)

    return pl.pallas_call(
        body, name=name, grid=(S // tm,),
        in_specs=[_rows1(tm, D), _resident(mod.shape), _resident(gn.shape), _resident(wd.shape), _resident(qg.shape),
                  _resident(kvg.shape), _rows1(tm, LANES), _rows1(tm, LANES)],
        out_specs=[_rows1(tm, D), _rows1(tm, LW), _rows1(tm, QL), _rows1(tm, KL), _rows1(tm, LANES)],
        out_shape=[SDS((S, D), BF16), SDS((S, LW), F32), SDS((S, QL), BF16), SDS((S, KL), BF16), SDS((S, LANES), BF16)],
        compiler_params=_params(("parallel",)),
    )(x, mod, gn, wd, qg, kvg, cos_t, sin_t)


def _mla_qkv(cq, ckv, krp, wq, wukv, cos_t, sin_t, tm, name):
    S = cq.shape[0]
    H = wq.shape[1] // MLA_HEAD_PAD
    HP = MLA_HEAD_PAD

    def body(cq_ref, ckv_ref, kr_ref, wq_ref, wkv_ref, c_ref, s_ref, q_ref, k_ref, v_ref):
        q = _dot(cq_ref[...], wq_ref[...])
        kv = _dot(ckv_ref[...], wkv_ref[...])
        cv, sv, krv = c_ref[...], s_ref[...], kr_ref[...]
        for h in range(H):
            qr = q[:, h * HP + MLA_NOPE:(h + 1) * HP]
            q_ref[:, h * HP:h * HP + MLA_NOPE] = (q[:, h * HP:h * HP + MLA_NOPE] * SM_SCALE).astype(BF16)
            q_ref[:, h * HP + MLA_NOPE:(h + 1) * HP] = ((qr * cv + _swap_halves(qr) * sv) * SM_SCALE).astype(BF16)
            k_ref[:, h * HP:h * HP + MLA_NOPE] = kv[:, h * HP:h * HP + MLA_NOPE].astype(BF16)
            k_ref[:, h * HP + MLA_NOPE:(h + 1) * HP] = krv
            v_ref[:, h * MLA_V:(h + 1) * MLA_V] = kv[:, h * HP + MLA_NOPE:(h + 1) * HP].astype(BF16)

    return pl.pallas_call(
        body, name=name, grid=(S // tm,),
        in_specs=[_rows1(tm, MLA_Q_LORA), _rows1(tm, MLA_KV_LORA), _rows1(tm, LANES), _resident(wq.shape),
                  _resident(wukv.shape), _rows1(tm, LANES), _rows1(tm, LANES)],
        out_specs=[_rows1(tm, H * HP), _rows1(tm, H * HP), _rows1(tm, H * MLA_V)],
        out_shape=[SDS((S, H * HP), BF16), SDS((S, H * HP), BF16), SDS((S, H * MLA_V), BF16)],
        compiler_params=_params(("parallel",)),
    )(cq, ckv, krp, wq, wukv, cos_t, sin_t)


def _causal_mask(tq):
    row = lax.broadcasted_iota(jnp.int32, (tq, tq), 0)
    col = lax.broadcasted_iota(jnp.int32, (tq, tq), 1)
    return col <= row


def _tile_rows(j, tq):
    return slice(j * tq, (j + 1) * tq) if isinstance(j, int) else pl.ds(pl.multiple_of(j * tq, tq), tq)


def _attn_fwd(q, k, v, tq, name):
    S = q.shape[0]
    HP = MLA_HEAD_PAD
    H = q.shape[1] // HP
    nq = S // tq

    def body(q_ref, k_ref, v_ref, o_ref, lse_ref):
        for i in range(nq):
            rows = slice(i * tq, (i + 1) * tq)
            qv = q_ref[rows, :]

            def step(j, carry, masked, qv=qv):
                m, l, acc = carry
                krows = _tile_rows(j, tq)
                s = _dot_nt(qv, k_ref[krows, :])
                if masked:
                    s = jnp.where(_causal_mask(tq), s, NEG)
                m_new = jnp.maximum(m, jnp.max(s, axis=1, keepdims=True))
                p = jnp.exp(s - m_new)
                alpha = jnp.exp(m - m_new)
                l = alpha * l + jnp.sum(p, axis=1, keepdims=True)
                acc = alpha * acc + _dot(p.astype(BF16), v_ref[krows, :])
                return m_new, l, acc

            carry = (jnp.full((tq, 1), NEG, F32), jnp.zeros((tq, 1), F32), jnp.zeros((tq, MLA_V), F32))
            for j in range(i):
                carry = step(j, carry, False)
            m, l, acc = step(i, carry, True)
            o_ref[rows, :] = (acc / l).astype(BF16)
            lse_ref[0, rows, :] = jnp.broadcast_to(m + jnp.log(l), (tq, LANES))

    return pl.pallas_call(
        body, name=name, grid=(H,),
        in_specs=[pl.BlockSpec((S, HP), lambda h: (0, h)), pl.BlockSpec((S, HP), lambda h: (0, h)),
                  pl.BlockSpec((S, MLA_V), lambda h: (0, h))],
        out_specs=[pl.BlockSpec((S, MLA_V), lambda h: (0, h)), pl.BlockSpec((1, S, LANES), lambda h: (h, 0, 0))],
        out_shape=[SDS((S, H * MLA_V), BF16), SDS((H, S, LANES), F32)],
        compiler_params=_params(("parallel",)),
    )(q, k, v)


def _attn_bwd(q, k, v, o, do, lse, cos_t, sin_t, tq, name):
    S = q.shape[0]
    HP = MLA_HEAD_PAD
    H = q.shape[1] // HP
    nq = S // tq

    def body(q_ref, k_ref, v_ref, o_ref, do_ref, lse_ref, c_ref, s_ref, dq_ref, dkv_ref, dkr_ref, dq_acc, dl_s):
        @pl.when(pl.program_id(0) == 0)
        def _():
            dkr_ref[...] = jnp.zeros_like(dkr_ref)

        dq_acc[...] = jnp.zeros_like(dq_acc)

        def delta_tile(i, _):
            rows = pl.ds(pl.multiple_of(i * tq, tq), tq)
            d = jnp.sum(do_ref[rows, :].astype(F32) * o_ref[rows, :].astype(F32), axis=1, keepdims=True)
            dl_s[rows, :] = jnp.broadcast_to(d, (tq, LANES))
            return 0

        lax.fori_loop(0, nq, delta_tile, 0)

        for j in range(nq):
            krows = slice(j * tq, (j + 1) * tq)
            kv_k = k_ref[krows, :]
            kv_v = v_ref[krows, :]

            def step(i, carry, masked, kv_k=kv_k, kv_v=kv_v):
                dk, dv = carry
                rows = _tile_rows(i, tq)
                qv = q_ref[rows, :]
                dov = do_ref[rows, :]
                s = _dot_nt(qv, kv_k)
                if masked:
                    s = jnp.where(_causal_mask(tq), s, NEG)
                p = jnp.exp(s - lse_ref[0, rows, 0:1])
                dv = dv + _dot_tn(p.astype(BF16), dov)
                dp = _dot_nt(dov, kv_v)
                ds = (p * (dp - dl_s[rows, 0:1])).astype(BF16)
                dk = dk + _dot_tn(ds, qv)
                dq_acc[rows, :] += _dot(ds, kv_k)
                return dk, dv

            carry = step(j, (jnp.zeros((tq, HP), F32), jnp.zeros((tq, MLA_V), F32)), True)
            for i in range(j + 1, nq):
                carry = step(i, carry, False)
            dk, dv = carry
            dkv_ref[krows, :MLA_NOPE] = dk[:, :MLA_NOPE].astype(BF16)
            dkv_ref[krows, MLA_NOPE:] = dv.astype(BF16)
            dkr_ref[krows, :] += dk[:, MLA_NOPE:]

        def out_tile(i, _):
            rows = pl.ds(pl.multiple_of(i * tq, tq), tq)
            dq = dq_acc[rows, :] * SM_SCALE
            dqr = dq[:, MLA_NOPE:]
            dq_ref[rows, :MLA_NOPE] = dq[:, :MLA_NOPE].astype(BF16)
            dq_ref[rows, MLA_NOPE:] = (dqr * c_ref[rows, :] + _swap_halves(dqr * s_ref[rows, :])).astype(BF16)
            return 0

        lax.fori_loop(0, nq, out_tile, 0)

    return pl.pallas_call(
        body, name=name, grid=(H,),
        in_specs=[pl.BlockSpec((S, HP), lambda h: (0, h)), pl.BlockSpec((S, HP), lambda h: (0, h)),
                  pl.BlockSpec((S, MLA_V), lambda h: (0, h)), pl.BlockSpec((S, MLA_V), lambda h: (0, h)),
                  pl.BlockSpec((S, MLA_V), lambda h: (0, h)), pl.BlockSpec((1, S, LANES), lambda h: (h, 0, 0)),
                  _resident(cos_t.shape), _resident(sin_t.shape)],
        out_specs=[pl.BlockSpec((S, HP), lambda h: (0, h)), pl.BlockSpec((S, HP), lambda h: (0, h)),
                   pl.BlockSpec((S, LANES), lambda h: (0, 0))],
        out_shape=[SDS((S, H * HP), BF16), SDS((S, H * HP), BF16), SDS((S, LANES), F32)],
        scratch_shapes=[pltpu.VMEM((S, HP), F32), pltpu.VMEM((S, LANES), F32)],
        compiler_params=_params(("arbitrary",)),
    )(q, k, v, o, do, lse, cos_t, sin_t)


def _mla_out(o, w_o, x, mod, tm, name):
    S, KO = o.shape
    D = w_o.shape[1]

    def body(o_ref, w_ref, x_ref, mod_ref, x1_ref, y_ref):
        y = _dot(o_ref[...], w_ref[...])
        y_ref[...] = y.astype(BF16)
        x1_ref[...] = x_ref[...] + mod_ref[G1:G1 + 1, :] * y

    return pl.pallas_call(
        body, name=name, grid=(S // tm,),
        in_specs=[_rows1(tm, KO), _resident(w_o.shape), _rows1(tm, D), _resident(mod.shape)],
        out_specs=[_rows1(tm, D), _rows1(tm, D)],
        out_shape=[SDS((S, D), F32), SDS((S, D), BF16)],
        compiler_params=_params(("parallel",)),
    )(o, w_o, x, mod)


def _mla_bwd_o(dx1, y, mod, w_o, tm, name):
    S, D = dx1.shape
    KO = w_o.shape[0]

    def body(d1_ref, y_ref, mod_ref, w_ref, dy_ref, do_ref, st_ref):
        @pl.when(pl.program_id(0) == 0)
        def _():
            st_ref[...] = jnp.zeros_like(st_ref)

        d1 = d1_ref[...]
        dyb = (d1 * mod_ref[G1:G1 + 1, :]).astype(BF16)
        dy_ref[...] = dyb
        st_ref[G1:G1 + 1, :] += _colsum(d1 * y_ref[...].astype(F32))
        do_ref[...] = _dot_nt(dyb, w_ref[...]).astype(BF16)

    return pl.pallas_call(
        body, name=name, grid=(S // tm,),
        in_specs=[_rows1(tm, D), _rows1(tm, D), _resident(mod.shape), _resident(w_o.shape)],
        out_specs=[_rows1(tm, D), _rows1(tm, KO), pl.BlockSpec((8, D), lambda i: (0, 0))],
        out_shape=[SDS((S, D), BF16), SDS((S, KO), BF16), SDS((8, D), F32)],
        compiler_params=_params(("arbitrary",)),
    )(dx1, y, mod, w_o)


def _mla_bwd_lat(dq, dkv, dkr, lat, x, dx1, mod, gn, qg, kvg, wq, wukv, wd, cos_t, sin_t, tm, name):
    S, D = x.shape
    LW = wd.shape[1]
    QL, KL = MLA_Q_LORA, MLA_KV_LORA

    def body(dq_ref, dkv_ref, dkr_ref, lat_ref, x_ref, d1_ref, mod_ref, gn_ref, qg_ref, kvg_ref, wq_ref, wkv_ref, wd_ref,
             c_ref, s_ref, dx_ref, dlat_ref, st_ref, dqg_ref, dkvg_ref):
        @pl.when(pl.program_id(0) == 0)
        def _():
            st_ref[...] = jnp.zeros_like(st_ref)
            dqg_ref[...] = jnp.zeros_like(dqg_ref)
            dkvg_ref[...] = jnp.zeros_like(dkvg_ref)

        lat = lat_ref[...]
        dcq = _dot_nt(dq_ref[...], wq_ref[...])
        nq, rq = _rms(lat[:, :QL])
        dqg_ref[0:1, :] += _colsum(dcq * nq)
        dlat_q = _normmod_bwd(dcq, nq, rq, qg_ref[...]).astype(BF16)
        dckv = _dot_nt(dkv_ref[...], wkv_ref[...])
        nkv, rkv = _rms(lat[:, QL:QL + KL])
        dkvg_ref[0:1, :] += _colsum(dckv * nkv)
        dlat_kv = _normmod_bwd(dckv, nkv, rkv, kvg_ref[...]).astype(BF16)
        dkr = dkr_ref[...]
        dlat_kr = (dkr * c_ref[...] + _swap_halves(dkr * s_ref[...])).astype(BF16)
        dlat_ref[:, :QL] = dlat_q
        dlat_ref[:, QL:QL + KL] = dlat_kv
        dlat_ref[:, QL + KL:] = dlat_kr
        dh = (_dot_nt(dlat_q, wd_ref[:, :QL]) + _dot_nt(dlat_kv, wd_ref[:, QL:QL + KL])
              + _dot_nt(dlat_kr, wd_ref[:, QL + KL:]))
        n, rr = _rms(x_ref[...])
        gn_v = gn_ref[...]
        sc1p = 1.0 + mod_ref[SC1:SC1 + 1, :]
        t = _colsum(dh * n)
        st_ref[SH1:SH1 + 1, :] += _colsum(dh)
        st_ref[SC1:SC1 + 1, :] += t * gn_v
        st_ref[3:4, :] += t * sc1p
        dx_ref[...] = d1_ref[...] + _normmod_bwd(dh, n, rr, gn_v * sc1p)

    HW = wq.shape[1]
    return pl.pallas_call(
        body, name=name, grid=(S // tm,),
        in_specs=[_rows1(tm, HW), _rows1(tm, HW), _rows1(tm, LANES), _rows1(tm, LW), _rows1(tm, D), _rows1(tm, D),
                  _resident(mod.shape), _resident(gn.shape), _resident(qg.shape), _resident(kvg.shape),
                  _resident(wq.shape), _resident(wukv.shape), _resident(wd.shape), _rows1(tm, LANES), _rows1(tm, LANES)],
        out_specs=[_rows1(tm, D), _rows1(tm, LW), pl.BlockSpec((8, D), lambda i: (0, 0)),
                   pl.BlockSpec((8, QL), lambda i: (0, 0)), pl.BlockSpec((8, KL), lambda i: (0, 0))],
        out_shape=[SDS((S, D), F32), SDS((S, LW), BF16), SDS((8, D), F32), SDS((8, QL), F32), SDS((8, KL), F32)],
        compiler_params=_params(("arbitrary",)),
    )(dq, dkv, dkr, lat, x, dx1, mod, gn, qg, kvg, wq, wukv, wd, cos_t, sin_t)


def _loss_head(x, tgt, fg, tm, name):
    S, D = x.shape
    nt = S // tm

    def body(x_ref, t_ref, g_ref, dx_ref, acc_ref):
        i = pl.program_id(0)

        @pl.when(i == 0)
        def _():
            acc_ref[...] = jnp.zeros_like(acc_ref)

        n, rr = _rms(x_ref[...])
        g = g_ref[...]
        err = n * g - t_ref[...]
        acc_ref[1:2, :] += _colsum(err * err) * (0.5 / D)
        dy = err * (1.0 / D)
        acc_ref[0:1, :] += _colsum(dy * n)
        dx_ref[...] = _normmod_bwd(dy, n, rr, g)

        @pl.when(i == nt - 1)
        def _():
            acc_ref[2:3, :] = jnp.broadcast_to(jnp.sum(acc_ref[1:2, :], axis=1, keepdims=True), (1, D))

    return pl.pallas_call(
        body, name=name, grid=(nt,),
        in_specs=[_rows1(tm, D), _rows1(tm, D), _resident(fg.shape)],
        out_specs=[_rows1(tm, D), pl.BlockSpec((8, D), lambda i: (0, 0))],
        out_shape=[SDS((S, D), F32), SDS((8, D), F32)],
        compiler_params=_params(("arbitrary",)),
    )(x, tgt, fg)


def _rope_tables(positions, S):
    inv_freq = ROPE_THETA ** (-jnp.arange(0, MLA_ROPE, 2, dtype=F32) / MLA_ROPE)
    ang = positions.reshape(S, 1).astype(F32) * inv_freq
    cos, sin = jnp.cos(ang), jnp.sin(ang)
    z = jnp.zeros((S, LANES - MLA_ROPE), F32)
    return jnp.concatenate([cos, cos, z], axis=1), jnp.concatenate([-sin, sin, z], axis=1)


def _pad_heads(w, per_head):
    K = w.shape[0]
    H = w.shape[1] // per_head
    w3 = w.reshape(K, H, per_head)
    return jnp.pad(w3, ((0, 0), (0, 0), (0, MLA_HEAD_PAD - per_head))).reshape(K, H * MLA_HEAD_PAD)


def _unpad_heads(w, per_head):
    K = w.shape[0]
    H = w.shape[1] // MLA_HEAD_PAD
    return w.reshape(K, H, MLA_HEAD_PAD)[:, :, :per_head].reshape(K, H * per_head)


def _tiles(S):
    return min(512, S), min(256, S), min(512, S)


def _layer_forward(i, x, mod, gmix, gmlp, w, small, tables):
    if True:
        S, D = x.shape
        tm, tms, tq = _tiles(S)
        cos_t, sin_t = tables
        mod, gmix, gmlp = {i: mod}, {i: gmix}, {i: gmlp}
        kind = i % 3
        sv = {"x": x}
        if kind == 0:
            x1 = _pool_fwd(x, mod[i], gmix[i], w["pool_w"], small["pool_scale"][i // 3], tm, f"pool_fwd_{i}")
        elif kind == 1:
            x1, sv["h"], sv["pre"], sv["y"] = _sgu_fwd(
                x, mod[i], gmix[i], w["sgu_w_in"], small["sgu_ln_g"], small["sgu_ln_b"], small["sgu_w_s"],
                small["sgu_b_s_t"], w["sgu_w_out"], tms, f"sgu_fwd_{i}")
        else:
            sv["h"], sv["lat"], sv["cq"], sv["ckv"], krp = _mla_lat(
                x, mod[i], gmix[i], w["mla_wd"], small["mla_q_norm_g"], small["mla_kv_norm_g"], cos_t, sin_t, tm,
                f"mla_lat_{i}")
            sv["q"], sv["k"], sv["v"] = _mla_qkv(sv["cq"], sv["ckv"], krp, w["mla_wq"], w["mla_w_ukv"], cos_t, sin_t,
                                                 tm, f"mla_qkv_{i}")
            sv["o"], sv["lse"] = _attn_fwd(sv["q"], sv["k"], sv["v"], tq, f"attn_fwd_{i}")
            x1, sv["y"] = _mla_out(sv["o"], w["mla_w_o"], x, mod[i], tm, f"mla_out_{i}")
        sv["x1"] = x1
        Fh = w["mlp_w1"].shape[1]
        sv["h2"], sv["r"] = _mlp_up(x1, mod[i], gmlp[i], w["mlp_w1"], tm, min(2048, Fh), f"mlp_up_{i}")
        x, sv["o2"] = _mlp_down(sv["r"], w["mlp_w2"], x1, mod[i], tm, f"mlp_down_{i}")
        return x, sv


def _layer_backward(i, dx, sv, mod, gmix, gmlp, w, small, tables):
    if True:
        S, D = dx.shape
        tm, tms, tq = _tiles(S)
        cos_t, sin_t = tables
        mod, gmix, gmlp = {i: mod}, {i: gmix}, {i: gmlp}
        kind = i % 3
        sgrads = {}
        Fh = w["mlp_w1"].shape[1]
        g = {}
        d_a, d_o, st_a = _mlp_bwd_a(dx, sv["o2"], mod[i], w["mlp_w2"], sv["r"], tm, min(2048, Fh), f"mlp_bwd_a_{i}")
        g["mlp_w2"] = _mm_tn(sv["r"], d_o, min(512, Fh), D, f"mlp_dw2_{i}", square_a=True)
        g["mlp_w1"] = _mm_tn(sv["h2"], d_a, D, min(512, Fh), f"mlp_dw1_{i}", col_shards=True)
        dx1, st_b = _mlp_bwd_b(d_a, w["mlp_w1"], sv["x1"], dx, mod[i], gmlp[i], tm, f"mlp_bwd_b_{i}")
        if kind == 0:
            dx, st_m, dpw = _pool_bwd(sv["x"], dx1, mod[i], gmix[i], w["pool_w"], small["pool_scale"][i // 3], tm,
                                      f"pool_bwd_{i}")
            g["pool_w"] = dpw
            sgrads[f"pool_scale_{i // 3}"] = st_m[4:5]
        elif kind == 1:
            dx, dyb, gated, dpre, st_m, dws, dbs = _sgu_bwd(
                sv["x"], dx1, sv["pre"], sv["y"], mod[i], gmix[i], w["sgu_w_in"], small["sgu_ln_g"], small["sgu_ln_b"],
                small["sgu_w_s"], small["sgu_b_s_t"], w["sgu_w_out"], tms, f"sgu_bwd_{i}")
            W = gated.shape[1]
            g["sgu_w_out"] = _mm_tn(gated, dyb, min(512, W), D, f"sgu_dwout_{i}")
            g["sgu_w_in"] = _mm_tn(sv["h"], dpre, D, min(512, 2 * W), f"sgu_dwin_{i}", col_shards=True)
            sgrads["sgu_ln_g"], sgrads["sgu_ln_b"] = st_m[4:5], st_m[5:6]
            sgrads["sgu_w_s"], sgrads["sgu_b_s"] = dws, dbs[:, :, 0]
        else:
            dyb, do, st_o = _mla_bwd_o(dx1, sv["y"], mod[i], w["mla_w_o"], tm, f"mla_bwd_o_{i}")
            KO = do.shape[1]
            g["mla_w_o"] = _mm_tn(sv["o"], dyb, min(512, KO), D, f"mla_dwo_{i}")
            dq, dkv, dkr = _attn_bwd(sv["q"], sv["k"], sv["v"], sv["o"], do, sv["lse"], cos_t, sin_t, tq, f"attn_bwd_{i}")
            dx, dlat, st_m, dqg, dkvg = _mla_bwd_lat(
                dq, dkv, dkr, sv["lat"], sv["x"], dx1, mod[i], gmix[i], small["mla_q_norm_g"], small["mla_kv_norm_g"],
                w["mla_wq"], w["mla_w_ukv"], w["mla_wd"], cos_t, sin_t, tm, f"mla_bwd_lat_{i}")
            HW = dq.shape[1]
            g["mla_wq"] = _mm_tn(sv["cq"], dq, MLA_Q_LORA, min(1024, HW), f"mla_dwq_{i}")
            g["mla_w_ukv"] = _mm_tn(sv["ckv"], dkv, MLA_KV_LORA, min(1024, HW), f"mla_dwukv_{i}", col_shards=True)
            g["mla_wd"] = _mm_tn(sv["h"], dlat, D, dlat.shape[1], f"mla_dwd_{i}")
            st_m = jnp.concatenate([st_m[0:2], st_o[2:3], st_m[3:]], axis=0)
            sgrads["mla_q_norm_g"], sgrads["mla_kv_norm_g"] = dqg[0:1], dkvg[0:1]
        stats = jnp.concatenate([st_m[0:3], st_b[3:5], st_a[5:6], st_m[3:4], st_b[6:7]], axis=0)
        return dx, stats, g, sgrads


def _local_step(x, tgt, positions, mod, gmix, gmlp, fg, wts, small):
    S = x.shape[0]
    L = mod.shape[0]
    tables = _rope_tables(positions, S)
    saved = []
    for i in range(L):
        x, sv = _layer_forward(i, x, mod[i], gmix[i], gmlp[i], wts[i], small, tables)
        saved.append(sv)
    dx, loss_acc = _loss_head(x, tgt, fg, _tiles(S)[0], "loss_head")
    stats, grads, sgrads = [None] * L, [None] * L, {}
    for i in reversed(range(L)):
        dx, stats[i], grads[i], sg = _layer_backward(i, dx, saved[i], mod[i], gmix[i], gmlp[i], wts[i], small, tables)
        sgrads.update(sg)
    return loss_acc, dx, stats, grads, sgrads


HBM_SPEC = pl.BlockSpec(memory_space=pltpu.HBM)
VMEM_SPEC = pl.BlockSpec(memory_space=pltpu.VMEM)


def _my_place():
    return lax.axis_index("x"), lax.axis_index("y"), lax.axis_index("c")


def _flip(v, bit):
    return 1 - v if bit else v


def _small_all_gather(v, name):
    R, C = v.shape

    def body(x_ref, out_ref, send_sems, recv_sems):
        x, y, c = _my_place()
        me = 4 * x + 2 * y + c
        out_ref[me] = x_ref[...]
        sends = []
        for k in range(1, NDEV):
            peer = (_flip(x, k & 4), _flip(y, k & 2), _flip(c, k & 1))
            cp = pltpu.make_async_remote_copy(src_ref=x_ref, dst_ref=out_ref.at[me], send_sem=send_sems.at[k - 1],
                                              recv_sem=recv_sems.at[k - 1], device_id=peer, device_id_type=MESH)
            cp.start()
            sends.append(cp)
        for k in range(1, NDEV):
            src = 4 * _flip(x, k & 4) + 2 * _flip(y, k & 2) + _flip(c, k & 1)
            pltpu.make_async_remote_copy(src_ref=x_ref, dst_ref=out_ref.at[src], send_sem=send_sems.at[k - 1],
                                         recv_sem=recv_sems.at[k - 1], device_id=(x, y, c), device_id_type=MESH).wait_recv()
        for cp in sends:
            cp.wait_send()

    return pl.pallas_call(
        body, name=name, out_shape=SDS((NDEV, R, C), v.dtype), in_specs=[VMEM_SPEC], out_specs=VMEM_SPEC,
        scratch_shapes=[pltpu.SemaphoreType.DMA((NDEV - 1,)), pltpu.SemaphoreType.DMA((NDEV - 1,))],
        compiler_params=pltpu.CompilerParams(vmem_limit_bytes=V7X_VMEM_LIMIT),
    )(v)


def _slab(ref, axis, width, dev):
    idx = [slice(None)] * len(ref.shape)
    idx[axis] = pl.ds(pl.multiple_of(dev * width, width), width)
    return ref.at[tuple(idx)]


def _all_gather_group(shards, axes, after, name):
    nt = len(shards)
    out_shapes = [SDS(tuple(s * NDEV if a == ax else s for a, s in enumerate(sh.shape)), sh.dtype)
                  for sh, ax in zip(shards, axes)]

    def body(*refs):
        ins, outs = refs[:nt], refs[nt + 1:2 * nt + 1]
        send_sems, recv_sems, local_sems = refs[2 * nt + 1:]
        x, y, c = _my_place()
        me = 4 * x + 2 * y + c
        sibling = (x, y, 1 - c)
        chips = [(1 - x, y), (x, 1 - y), (1 - x, 1 - y)]

        def block(t, dev):
            return _slab(outs[t], axes[t], ins[t].shape[axes[t]], dev)

        def copy(t, k, dev, to, src=None):
            return pltpu.make_async_remote_copy(
                src_ref=block(t, dev) if src is None else src, dst_ref=block(t, dev), send_sem=send_sems.at[t, k],
                recv_sem=recv_sems.at[t, k], device_id=to, device_id_type=MESH)

        mine = [pltpu.make_async_copy(ins[t], block(t, me), local_sems.at[t]) for t in range(nt)]
        for cp in mine:
            cp.start()
        first = []
        for t in range(nt):
            first.append(copy(t, 0, me, sibling, src=ins[t]))
            first += [copy(t, 1 + j, me, (cx, cy, c), src=ins[t]) for j, (cx, cy) in enumerate(chips)]
        for cp in first:
            cp.start()
        passed = []
        for j, (cx, cy) in enumerate(chips):
            for t in range(nt):
                copy(t, 1 + j, 4 * cx + 2 * cy + c, (x, y, c)).wait_recv()
                cp = copy(t, 4 + j, 4 * cx + 2 * cy + c, sibling)
                cp.start()
                passed.append(cp)
        for t in range(nt):
            copy(t, 0, 4 * x + 2 * y + (1 - c), (x, y, c)).wait_recv()
        for j, (cx, cy) in enumerate(chips):
            for t in range(nt):
                copy(t, 4 + j, 4 * cx + 2 * cy + (1 - c), (x, y, c)).wait_recv()
        for cp in first + passed:
            cp.wait_send()
        for cp in mine:
            cp.wait()

    return pl.pallas_call(
        body, name=name, out_shape=out_shapes, in_specs=[HBM_SPEC] * nt + [ANY_SPEC], out_specs=[HBM_SPEC] * nt,
        scratch_shapes=[pltpu.SemaphoreType.DMA((nt, NDEV - 1)), pltpu.SemaphoreType.DMA((nt, NDEV - 1)),
                        pltpu.SemaphoreType.DMA((nt,))],
    )(*shards, after)


def _reduce_scatter_sibling(grads, name):
    nt = len(grads)
    NCH = NDEV // 2
    out_shapes = [SDS((NCH,) + gr.shape[1:], gr.dtype) for gr in grads]

    def body(*refs):
        ins, lands = refs[:nt], refs[nt:2 * nt]
        send_sems, recv_sems = refs[2 * nt:]
        x, y, c = _my_place()
        sends = []
        for t in range(nt):
            for k in range(NCH):
                cp = pltpu.make_async_remote_copy(
                    src_ref=ins[t].at[2 * k + (1 - c)], dst_ref=lands[t].at[k], send_sem=send_sems.at[t, k],
                    recv_sem=recv_sems.at[t, k], device_id=(x, y, 1 - c), device_id_type=MESH)
                cp.start()
                sends.append(cp)
        for cp in sends:
            cp.wait_recv()
        for cp in sends:
            cp.wait_send()

    return pl.pallas_call(
        body, name=name, out_shape=out_shapes, in_specs=[HBM_SPEC] * nt, out_specs=[HBM_SPEC] * nt,
        scratch_shapes=[pltpu.SemaphoreType.DMA((nt, NCH)), pltpu.SemaphoreType.DMA((nt, NCH))],
    )(*grads)


SEM_SPEC = pl.BlockSpec(memory_space=pltpu.SEMAPHORE)
ANY_SPEC = pl.BlockSpec(memory_space=pl.ANY)
SPLIT_PARAMS = pltpu.CompilerParams(has_side_effects=pltpu.SideEffectType.DATAFLOW_SIDE_EFFECTING)
TOKEN = SDS((8, LANES), F32)


def _in_hbm(arrays):
    return [pltpu.with_memory_space_constraint(v, pltpu.HBM) for v in arrays]


def _split_start(body, srcs, lands, after, n_sem, name):
    ns, nl = len(srcs), len(lands)
    bufs = list(srcs) + list(lands)
    res = pl.pallas_call(
        body, name=name,
        out_shape=(pltpu.SemaphoreType.DMA((ns * n_sem,)), pltpu.SemaphoreType.DMA((ns * n_sem,)),
                   *[pltpu.HBM(v.shape, v.dtype) for v in bufs], TOKEN),
        in_specs=[HBM_SPEC] * (ns + nl) + [ANY_SPEC],
        out_specs=(SEM_SPEC, SEM_SPEC, *[HBM_SPEC] * (ns + nl), VMEM_SPEC),
        input_output_aliases={t: 2 + t for t in range(ns + nl)}, compiler_params=SPLIT_PARAMS,
    )(*_in_hbm(bufs), after)
    return res[0], res[1], list(res[2:2 + ns]), list(res[2 + ns:2 + ns + nl]), res[-1]


def _split_wait(body, send_sems, recv_sems, srcs, lands, after, name):
    ns, nl = len(srcs), len(lands)
    bufs = list(srcs) + list(lands)
    res = pl.pallas_call(
        body, name=name, out_shape=tuple(pltpu.HBM(v.shape, v.dtype) for v in bufs),
        in_specs=[HBM_SPEC] * (ns + nl) + [SEM_SPEC, SEM_SPEC, ANY_SPEC], out_specs=tuple([HBM_SPEC] * (ns + nl)),
        input_output_aliases={t: t for t in range(ns + nl)}, compiler_params=SPLIT_PARAMS,
    )(*bufs, send_sems, recv_sems, after)
    return list(res[:ns]), list(res[ns:])


def _chips_exchange_start(parts, after, name):
    nt = len(parts)
    lands = [lax.empty((3,) + p.shape[1:], p.dtype) for p in parts]

    def body(*refs):
        ins, lnd = refs[:nt], refs[nt:2 * nt]
        send_sems, recv_sems, token = refs[2 * nt + 1], refs[2 * nt + 2], refs[-1]
        x, y, c = _my_place()
        for t in range(nt):
            for m in range(1, 4):
                px, py = _flip(x, m & 2), _flip(y, m & 1)
                pltpu.make_async_remote_copy(
                    src_ref=ins[t].at[2 * px + py], dst_ref=lnd[t].at[m - 1], send_sem=send_sems.at[3 * t + m - 1],
                    recv_sem=recv_sems.at[3 * t + m - 1], device_id=(px, py, c), device_id_type=MESH).start()
        token[...] = jnp.zeros_like(token)

    return _split_start(body, parts, lands, after, 3, name)


def _chips_exchange_wait(send_sems, recv_sems, parts, lands, after, name):
    nt = len(parts)

    def body(*refs):
        ins, lnd = refs[:nt], refs[nt:2 * nt]
        s_sems, r_sems = refs[2 * nt], refs[2 * nt + 1]
        x, y, c = _my_place()
        for t in range(nt):
            for m in range(1, 4):
                cp = pltpu.make_async_remote_copy(
                    src_ref=ins[t].at[0], dst_ref=lnd[t].at[m - 1], send_sem=s_sems.at[3 * t + m - 1],
                    recv_sem=r_sems.at[3 * t + m - 1], device_id=(x, y, c), device_id_type=MESH)
                cp.wait_send()
                cp.wait_recv()

    return _split_wait(body, send_sems, recv_sems, parts, lands, after, name)


def _gather_start(shards, axes, me, after, name):
    nt = len(shards)
    fulls = []
    for sh, ax in zip(shards, axes):
        shape = tuple(s * NDEV if a == ax else s for a, s in enumerate(sh.shape))
        start = [me * sh.shape[ax] if a == ax else 0 for a in range(sh.ndim)]
        fulls.append(lax.dynamic_update_slice(lax.empty(shape, sh.dtype), sh, start))

    def body(*refs):
        ins, outs = refs[:nt], refs[nt:2 * nt]
        send_sems, recv_sems, token = refs[2 * nt + 1], refs[2 * nt + 2], refs[-1]
        x, y, c = _my_place()
        dev = 4 * x + 2 * y + c
        peers = [(x, y, 1 - c), (1 - x, y, c), (x, 1 - y, c), (1 - x, 1 - y, c)]
        for t in range(nt):
            dst = _slab(outs[t], axes[t], ins[t].shape[axes[t]], dev)
            for k, peer in enumerate(peers):
                pltpu.make_async_remote_copy(src_ref=ins[t], dst_ref=dst, send_sem=send_sems.at[4 * t + k],
                                             recv_sem=recv_sems.at[4 * t + k], device_id=peer, device_id_type=MESH).start()
        token[...] = jnp.zeros_like(token)

    return _split_start(body, shards, fulls, after, 4, name)


def _gather_wait(send_sems, recv_sems, shards, fulls, axes, after, name):
    nt = len(shards)

    def body(*refs):
        ins, outs = refs[:nt], refs[nt:2 * nt]
        s_sems, r_sems = refs[2 * nt], refs[2 * nt + 1]
        x, y, c = _my_place()
        senders = [4 * x + 2 * y + (1 - c), 4 * (1 - x) + 2 * y + c, 4 * x + 2 * (1 - y) + c, 4 * (1 - x) + 2 * (1 - y) + c]
        for t in range(nt):
            for k, src_dev in enumerate(senders):
                cp = pltpu.make_async_remote_copy(
                    src_ref=ins[t], dst_ref=_slab(outs[t], axes[t], ins[t].shape[axes[t]], src_dev),
                    send_sem=s_sems.at[4 * t + k], recv_sem=r_sems.at[4 * t + k], device_id=(x, y, c), device_id_type=MESH)
                cp.wait_send()
                cp.wait_recv()

    return _split_wait(body, send_sems, recv_sems, shards, fulls, after, name)[1]


def _gather_pass_on(fulls, axes, name):
    nt = len(fulls)

    def body(*refs):
        outs = refs[nt:2 * nt]
        send_sems, recv_sems = refs[2 * nt:]
        x, y, c = _my_place()
        chips = [(1 - x, y), (x, 1 - y), (1 - x, 1 - y)]

        def copy(t, j, pc):
            cx, cy = chips[j]
            blk = _slab(outs[t], axes[t], outs[t].shape[axes[t]] // NDEV, 4 * cx + 2 * cy + pc)
            return pltpu.make_async_remote_copy(src_ref=blk, dst_ref=blk, send_sem=send_sems.at[t, j],
                                                recv_sem=recv_sems.at[t, j], device_id=(x, y, 1 - c), device_id_type=MESH)

        sends = [copy(t, j, c) for t in range(nt) for j in range(3)]
        for cp in sends:
            cp.start()
        for t in range(nt):
            for j in range(3):
                copy(t, j, 1 - c).wait_recv()
        for cp in sends:
            cp.wait_send()

    return pl.pallas_call(
        body, name=name, out_shape=[SDS(f.shape, f.dtype) for f in fulls], in_specs=[HBM_SPEC] * nt,
        out_specs=[HBM_SPEC] * nt, input_output_aliases={t: t for t in range(nt)},
        scratch_shapes=[pltpu.SemaphoreType.DMA((nt, 3)), pltpu.SemaphoreType.DMA((nt, 3))],
    )(*fulls)


def _row_tile(R, C, itemsize=4, target=1 << 20):
    best = R
    for tr in range(8, R, 8):
        if R % tr == 0 and tr * C * itemsize <= target:
            best = tr
    return best if best * C * itemsize <= target or best == R else R


def _as2d(a):
    return a.reshape(-1, a.shape[-1])


def _add_pairs(a, b, name):
    shp = a.shape
    a2, b2 = _as2d(a), _as2d(b)
    R, C = a2.shape
    tr = _row_tile(R, C, 2)

    def body(a_ref, b_ref, o_ref):
        o_ref[...] = (a_ref[...].astype(F32) + b_ref[...].astype(F32)).astype(o_ref.dtype)

    out = pl.pallas_call(
        body, name=name, grid=(R // tr,), in_specs=[_rows1(tr, C), _rows1(tr, C)], out_specs=_rows1(tr, C),
        out_shape=SDS((R, C), a.dtype), compiler_params=_params(("parallel",)),
    )(a2, b2)
    return out.reshape(shp)


def _adamw_math(g, w, m, v):
    m2 = ADAM_B1 * m + (1.0 - ADAM_B1) * g
    v2 = ADAM_B2 * v + (1.0 - ADAM_B2) * (g * g)
    m_hat = m2 / (1.0 - ADAM_B1 ** ADAM_STEP)
    v_hat = v2 / (1.0 - ADAM_B2 ** ADAM_STEP)
    delta = -ADAM_LR * (m_hat / (jnp.sqrt(v_hat) + ADAM_EPS) + ADAM_WD * w)
    return delta, m2, v2


def _adamw(parts, w, m, v, name):
    shp = w.shape
    w2, m2, v2 = _as2d(w), _as2d(m), _as2d(v)
    R, C = w2.shape
    tr = _row_tile(R, C)
    p3 = [p.reshape((-1, R, C)) for p in parts]
    npart = len(p3)

    def body(*refs):
        prefs = refs[:npart]
        w_ref, m_ref, v_ref, g_ref, d_ref, nm_ref, nv_ref = refs[npart:]
        g = None
        for pr in prefs:
            for k in range(pr.shape[0]):
                term = pr[k].astype(F32)
                g = term if g is None else g + term
        g_ref[...] = g
        d_ref[...], nm_ref[...], nv_ref[...] = _adamw_math(g, w_ref[...], m_ref[...], v_ref[...])

    outs = pl.pallas_call(
        body, name=name, grid=(R // tr,),
        in_specs=[pl.BlockSpec((p.shape[0], tr, C), lambda i: (0, i, 0)) for p in p3] + [_rows1(tr, C)] * 3,
        out_specs=[_rows1(tr, C)] * 4, out_shape=[SDS((R, C), F32)] * 4,
        compiler_params=_params(("parallel",)),
    )(*p3, w2, m2, v2)
    return [o.reshape(shp) for o in outs]


def _ada_fwd(c_all, ada_w, ada_b_mine, name):
    L, D, Wc = ada_w.shape

    def body(c_ref, w_ref, b_ref, o_ref):
        cv = c_ref[...]
        act = cv * (1.0 / (1.0 + jnp.exp(-cv)))
        o_ref[0] = jnp.dot(act, w_ref[0], preferred_element_type=F32, precision=lax.Precision.HIGHEST) + b_ref[0]

    return pl.pallas_call(
        body, name=name, grid=(L,),
        in_specs=[_resident(c_all.shape), pl.BlockSpec((1, D, Wc), lambda l: (l, 0, 0)), pl.BlockSpec((1, 1, Wc), lambda l: (l, 0, 0))],
        out_specs=pl.BlockSpec((1, NDEV, Wc), lambda l: (l, 0, 0)), out_shape=SDS((L, NDEV, Wc), F32),
        compiler_params=_params(("parallel",)),
    )(c_all, ada_w, ada_b_mine.reshape(L, 1, Wc))


def _ada_bwd(c_all, dmod_mine, name):
    L, _, Wc = dmod_mine.shape
    D = c_all.shape[1]

    def body(c_ref, d_ref, o_ref):
        cv = c_ref[...]
        act = cv * (1.0 / (1.0 + jnp.exp(-cv)))
        o_ref[0] = lax.dot_general(act, d_ref[0], (((0,), (0,)), ((), ())), preferred_element_type=F32,
                                   precision=lax.Precision.HIGHEST)

    return pl.pallas_call(
        body, name=name, grid=(L,),
        in_specs=[_resident(c_all.shape), pl.BlockSpec((1, NDEV, Wc), lambda l: (l, 0, 0))],
        out_specs=pl.BlockSpec((1, D, Wc), lambda l: (l, 0, 0)), out_shape=SDS((L, D, Wc), F32),
        compiler_params=_params(("parallel",)),
    )(c_all, dmod_mine)


WEIGHT_NAMES = ['ada_w', 'ada_b', 'norm_mix_g', 'norm_mlp_g', 'pool_w', 'pool_scale', 'sgu_w_in', 'sgu_ln_g', 'sgu_ln_b',
                'sgu_w_s', 'sgu_b_s', 'sgu_w_out', 'mla_w_dq_dkv', 'mla_q_norm_g', 'mla_kv_norm_g', 'mla_w_uq', 'mla_w_ukv',
                'mla_w_o', 'mlp_w1', 'mlp_w2', 'final_g']
REPLICATED = ['ada_b', 'norm_mix_g', 'norm_mlp_g', 'sgu_ln_g', 'sgu_ln_b', 'sgu_w_s', 'sgu_b_s', 'mla_kv_norm_g', 'final_g']
PACK_ROWS = 64
Q_HEAD = MLA_NOPE + MLA_ROPE


def _layer_matrices(i):
    kind, j = i % 3, i // 3
    if kind == 0:
        mats = [("pool_w", j, 1)]
    elif kind == 1:
        mats = [("sgu_w_in", j, 1), ("sgu_w_out", j, 0)]
    else:
        mats = [("mla_w_dq_dkv", j, 0), ("mla_w_uq", j, 1), ("mla_w_ukv", j, 1), ("mla_w_o", j, 0)]
    return mats + [("mlp_w1", i, 1), ("mlp_w2", i, 0)]


def _pack(arrays):
    flat = jnp.concatenate([a.reshape(-1).astype(F32) for a in arrays])
    rows = -(-flat.size // (LANES * PACK_ROWS)) * PACK_ROWS
    return jnp.pad(flat, (0, rows * LANES - flat.size)).reshape(rows, LANES)


def kernel(x, c, positions, ada_w, ada_b, norm_mix_g, norm_mlp_g, pool_w, pool_scale, sgu_w_in, sgu_ln_g, sgu_ln_b, sgu_w_s, sgu_b_s, sgu_w_out, mla_w_dq_dkv, mla_q_norm_g, mla_kv_norm_g, mla_w_uq, mla_w_ukv, mla_w_o, mlp_w1, mlp_w2, final_g, loss_target, m_ada_w, m_ada_b, m_norm_mix_g, m_norm_mlp_g, m_pool_w, m_pool_scale, m_sgu_w_in, m_sgu_ln_g, m_sgu_ln_b, m_sgu_w_s, m_sgu_b_s, m_sgu_w_out, m_mla_w_dq_dkv, m_mla_q_norm_g, m_mla_kv_norm_g, m_mla_w_uq, m_mla_w_ukv, m_mla_w_o, m_mlp_w1, m_mlp_w2, m_final_g, v_ada_w, v_ada_b, v_norm_mix_g, v_norm_mlp_g, v_pool_w, v_pool_scale, v_sgu_w_in, v_sgu_ln_g, v_sgu_ln_b, v_sgu_w_s, v_sgu_b_s, v_sgu_w_out, v_mla_w_dq_dkv, v_mla_q_norm_g, v_mla_kv_norm_g, v_mla_w_uq, v_mla_w_ukv, v_mla_w_o, v_mlp_w1, v_mlp_w2, v_final_g):
    a = dict(locals())
    S, D = x.shape[1], x.shape[2]
    L = ada_w.shape[0]
    Wc = ada_w.shape[2]
    me = 4 * lax.axis_index("x") + 2 * lax.axis_index("y") + lax.axis_index("c")
    my_chip = 2 * lax.axis_index("x") + lax.axis_index("y")

    v0 = _pack([c, pool_scale, mla_q_norm_g])
    g0 = _small_all_gather(v0, "gather_c").reshape(NDEV, -1)
    n_ps, n_qg = pool_scale.size, mla_q_norm_g.size
    c_all = g0[:, :D]
    ps_w = pool_scale.shape[1]
    ps_full = g0[:, D:D + n_ps].reshape(NDEV, -1, ps_w).transpose(1, 0, 2).reshape(-1, 1, D)
    qg_full = g0[:, D + n_ps:D + n_ps + n_qg].reshape(1, -1)

    ada_b_mine = lax.dynamic_slice_in_dim(ada_b, me * Wc, Wc, axis=1)
    modp = _ada_fwd(c_all, ada_w, ada_b_mine, "ada_fwd")
    ga = _small_all_gather(modp.reshape(-1, LANES), "gather_mod").reshape(NDEV, L, NDEV, Wc)
    mod = lax.dynamic_index_in_dim(ga, me, axis=2, keepdims=False).transpose(1, 0, 2).reshape(L, 6, D)
    mod8 = jnp.pad(mod, ((0, 0), (0, 2), (0, 0)))

    small = {"pool_scale": ps_full, "sgu_ln_g": sgu_ln_g, "sgu_ln_b": sgu_ln_b, "sgu_w_s": sgu_w_s[0],
             "sgu_b_s_t": sgu_b_s[0].T, "mla_q_norm_g": qg_full, "mla_kv_norm_g": mla_kv_norm_g}
    gmix, gmlp = norm_mix_g.reshape(L, 1, D), norm_mlp_g.reshape(L, 1, D)
    tables = _rope_tables(positions, S)

    def shards_of(i):
        mats = _layer_matrices(i)
        return [a[n][j].astype(BF16) for n, j, _ in mats], [ax for _, _, ax in mats]

    def as_weights(i, fulls):
        w = {n: f for (n, _, _), f in zip(_layer_matrices(i), fulls)}
        if "mla_w_uq" in w:
            lat_w = w["mla_w_dq_dkv"].shape[1]
            w["mla_wd"] = jnp.pad(w.pop("mla_w_dq_dkv"), ((0, 0), (0, -lat_w % LANES)))
            w["mla_wq"] = _pad_heads(w.pop("mla_w_uq"), Q_HEAD)
        return w

    xc = x[0]
    wts, saved, flying = [], [], None
    for i in range(L):
        shards, axes = shards_of(i)
        if flying is None:
            fulls = _all_gather_group(shards, axes, mod8, f"gather_w_{i}")
        else:
            fulls = _gather_wait(*flying[:4], axes, xc, f"gather_wait_{i}")
            fulls = _gather_pass_on(fulls, axes, f"gather_pass_{i}")
        wts.append(as_weights(i, fulls))
        mod_i = mod8[i]
        if i + 1 < L:
            flying = _gather_start(*shards_of(i + 1), me, fulls[0], f"gather_start_{i + 1}")
            mod_i = mod_i + flying[4][0, 0]
        xc, sv = _layer_forward(i, xc, mod_i, gmix[i], gmlp[i], wts[i], small, tables)
        saved.append(sv)
    dx, loss_acc = _loss_head(xc, loss_target[0], final_g.reshape(1, D), _tiles(S)[0], "loss_head")

    res = {n: [None] * a[n].shape[0] for n in WEIGHT_NAMES if a[n].ndim > 1}
    c_me = lax.axis_index("c")

    def start_reduce(i, g, after):
        mats = _layer_matrices(i)
        g = dict(g)
        if "mla_wq" in g:
            g["mla_w_dq_dkv"] = g.pop("mla_wd")[:, :mla_w_dq_dkv.shape[2]]
            g["mla_w_uq"] = _unpad_heads(g.pop("mla_wq"), Q_HEAD)
        gl = []
        for n, j, ax in mats:
            gm, blk = g[n].astype(BF16), a[n][j].shape
            if gm.shape != (NDEV,) + blk:
                gm = jnp.moveaxis(gm.reshape(blk[:ax] + (NDEV,) + blk[ax:]), ax, 0)
            gl.append(gm)
        lands = _reduce_scatter_sibling(gl, f"rs_sibling_{i}")
        owns = [lax.dynamic_index_in_dim(gm.reshape((NDEV // 2, 2) + gm.shape[1:]), c_me, axis=1, keepdims=False) for gm in gl]
        parts = [_add_pairs(o, l, f"rs_add_{i}_{n}") for o, l, (n, _, _) in zip(owns, lands, mats)]
        return _chips_exchange_start(parts, after, f"rs_chips_start_{i}")

    def finish_reduce(i, fly, after):
        parts, recv = _chips_exchange_wait(*fly[:4], after, f"rs_chips_wait_{i}")
        for (n, j, _), p, r in zip(_layer_matrices(i), parts, recv):
            p_own = lax.dynamic_index_in_dim(p, my_chip, axis=0, keepdims=False)
            res[n][j] = _adamw([p_own, r], a[n][j], a["m_" + n][j], a["v_" + n][j], f"adamw_{n}_{j}")

    stats, sgrads, flying = [None] * L, {}, None
    for i in reversed(range(L)):
        mod_i = mod8[i] if flying is None else mod8[i] + flying[4][0, 0]
        dx, stats[i], g, sgr = _layer_backward(i, dx, saved[i], mod_i, gmix[i], gmlp[i], wts[i], small, tables)
        sgrads.update(sgr)
        if flying is not None:
            finish_reduce(i + 1, flying, dx)
        flying = start_reduce(i, g, dx)

    sg = {"ada_b": jnp.stack([s[0:6] for s in stats]), "norm_mix_g": jnp.stack([s[6] for s in stats]),
          "norm_mlp_g": jnp.stack([s[7] for s in stats]), "final_g": loss_acc[0], "sgu_ln_g": sgrads["sgu_ln_g"],
          "sgu_ln_b": sgrads["sgu_ln_b"], "sgu_w_s": sgrads["sgu_w_s"], "sgu_b_s": sgrads["sgu_b_s"],
          "mla_kv_norm_g": sgrads["mla_kv_norm_g"]}
    ps_grad = jnp.concatenate([sgrads[f"pool_scale_{j}"] for j in range(pool_scale.shape[0])])
    tail = [ps_grad, sgrads["mla_q_norm_g"], loss_acc[2, :LANES]]
    packed = _pack([sg[n] for n in REPLICATED] + tail) + flying[4][0, 0]
    gathered = _small_all_gather(packed, "gather_small")
    zeros_tail = [jnp.zeros_like(t) for t in tail]
    g_p, d_p, m_p, v_p = _adamw([gathered], _pack([a[n] for n in REPLICATED] + zeros_tail),
                                _pack([a["m_" + n] for n in REPLICATED] + zeros_tail),
                                _pack([a["v_" + n] for n in REPLICATED] + zeros_tail), "adamw_replicated")
    finish_reduce(0, flying, g_p)
    flat = [t.reshape(-1) for t in (g_p, d_p, m_p, v_p)]
    off = 0
    for n in REPLICATED:
        res[n] = [f[off:off + a[n].size].reshape(a[n].shape) for f in flat]
        off += a[n].size
    g_ps = flat[0][off:off + ps_grad.size].reshape(ps_grad.shape)
    off += ps_grad.size
    g_qg = flat[0][off:off + qg_full.size].reshape(1, -1)
    off += qg_full.size
    loss = flat[0][off]
    res["pool_scale"] = _adamw([lax.dynamic_slice_in_dim(g_ps, me * ps_w, ps_w, axis=1)], pool_scale, m_pool_scale,
                               v_pool_scale, "adamw_pool_scale")
    qg_w = mla_q_norm_g.shape[1]
    res["mla_q_norm_g"] = _adamw([lax.dynamic_slice_in_dim(g_qg, me * qg_w, qg_w, axis=1)], mla_q_norm_g, m_mla_q_norm_g,
                                 v_mla_q_norm_g, "adamw_q_norm_g")

    n_mod = L * 6 * D
    dmod_all = gathered.reshape(NDEV, -1)[:, :n_mod].reshape(NDEV, L, 6 * D)
    dmod_mine = lax.dynamic_slice_in_dim(dmod_all, me * Wc, Wc, axis=2).transpose(1, 0, 2)
    res["ada_w"] = _adamw([_ada_bwd(c_all, dmod_mine, "ada_bwd")], ada_w, m_ada_w, v_ada_w, "adamw_ada_w")

    outs = []
    for k in range(4):
        for n in WEIGHT_NAMES:
            r = res[n]
            outs.append(jnp.stack([lay[k] for lay in r]) if isinstance(r[0], list) else r[k])
    return (loss, dx.reshape(x.shape), *outs)
```

```python
import functools
import math

import jax
import jax.numpy as jnp
import numpy as np
from jax import lax
from jax.experimental import pallas as pl
from jax.experimental.pallas import tpu as pltpu

F32 = jnp.float32
BF16 = jnp.bfloat16
SDS = jax.ShapeDtypeStruct
MESH = pl.DeviceIdType.MESH

NDEV = 8
V7X_VMEM_LIMIT = 56 << 20
LANES = 128
RMS_EPS = 1e-6
LN_EPS = 1e-5
POOL_WINDOWS = (2, 4, 8, 16)
HALO = 16
SGU_CHUNK = 128
SGU_HEAD = 128
MLA_NOPE, MLA_ROPE, MLA_V = 128, 64, 128
MLA_Q_LORA, MLA_KV_LORA = 256, 128
MLA_HEAD_PAD = 256
ROPE_THETA = 10000.0
SM_SCALE = (MLA_NOPE + MLA_ROPE) ** -0.5
NEG = -1e30
ADAM_LR, ADAM_B1, ADAM_B2, ADAM_EPS, ADAM_WD, ADAM_STEP = 0.001, 0.9, 0.999, 1e-08, 0.01, 10
INV_SQRT2 = 1.0 / math.sqrt(2.0)
INV_SQRT_2PI = 1.0 / math.sqrt(2.0 * math.pi)
SH1, SC1, G1, SH2, SC2, G2 = 0, 1, 2, 3, 4, 5


def _params(sem=None, vmem=V7X_VMEM_LIMIT):
    return pltpu.CompilerParams(dimension_semantics=sem, vmem_limit_bytes=vmem)


def _resident(shape):
    nd = len(shape)
    return pl.BlockSpec(shape, lambda *_: (0,) * nd, pipeline_mode=pl.Buffered(1))


def _rows1(tm, w):
    return pl.BlockSpec((tm, w), lambda i: (i, 0))


def _rms(x):
    r = lax.rsqrt(jnp.mean(x * x, axis=-1, keepdims=True) + RMS_EPS)
    return x * r, r


def _colsum(v):
    return jnp.sum(v, axis=0, keepdims=True)


def _normmod_bwd(dh, n, r, a):
    dn = dh * a
    return r * (dn - n * jnp.mean(dn * n, axis=-1, keepdims=True))


def _dot(a, b):
    return jnp.dot(a, b, preferred_element_type=F32)


def _dot_nt(a, b):
    return lax.dot_general(a, b, (((1,), (1,)), ((), ())), preferred_element_type=F32)


def _dot_tn(a, b):
    return lax.dot_general(a, b, (((0,), (0,)), ((), ())), preferred_element_type=F32)


def _gelu(x):
    return 0.5 * x * (1.0 + lax.erf(x * INV_SQRT2))


def _gelu_grad(x):
    return 0.5 * (1.0 + lax.erf(x * INV_SQRT2)) + x * jnp.exp(-0.5 * x * x) * INV_SQRT_2PI


def _swap_halves(v):
    lane = lax.broadcasted_iota(jnp.int32, v.shape, 1)
    half = MLA_ROPE // 2
    return jnp.where(lane < half, pltpu.roll(v, LANES - half, 1),
                     jnp.where(lane < MLA_ROPE, pltpu.roll(v, half, 1), 0.0))


def _mlp_up(x1, mod, gn, w1, tm, tn, name):
    S, D = x1.shape
    Fh = w1.shape[1]

    def body(x_ref, mod_ref, gn_ref, w_ref, h_ref, r_ref):
        n, _ = _rms(x_ref[...])
        a = gn_ref[...] * (1.0 + mod_ref[SC2:SC2 + 1, :])
        h = (n * a + mod_ref[SH2:SH2 + 1, :]).astype(BF16)
        h_ref[...] = h
        for j in range(Fh // tn):
            cols = slice(j * tn, (j + 1) * tn)
            r_ref[:, cols] = jnp.maximum(_dot(h, w_ref[:, cols]), 0.0).astype(BF16)

    return pl.pallas_call(
        body, name=name, grid=(S // tm,),
        in_specs=[_rows1(tm, D), _resident(mod.shape), _resident(gn.shape), _resident(w1.shape)],
        out_specs=[_rows1(tm, D), _rows1(tm, Fh)],
        out_shape=[SDS((S, D), BF16), SDS((S, Fh), BF16)],
        compiler_params=_params(("parallel",)),
    )(x1, mod, gn, w1)


def _mlp_down(r, w2, x1, mod, tm, name):
    S, Fh = r.shape
    D = w2.shape[1]

    def body(r_ref, w_ref, x_ref, mod_ref, x2_ref, o_ref):
        rv = r_ref[...]
        o = _dot(rv * rv, w_ref[...])
        o_ref[...] = o.astype(BF16)
        x2_ref[...] = x_ref[...] + mod_ref[G2:G2 + 1, :] * o

    return pl.pallas_call(
        body, name=name, grid=(S // tm,),
        in_specs=[_rows1(tm, Fh), _resident(w2.shape), _rows1(tm, D), _resident(mod.shape)],
        out_specs=[_rows1(tm, D), _rows1(tm, D)],
        out_shape=[SDS((S, D), F32), SDS((S, D), BF16)],
        compiler_params=_params(("parallel",)),
    )(r, w2, x1, mod)


def _mlp_bwd_a(dx2, o, mod, w2, r, tm, tn, name):
    S, D = dx2.shape
    Fh = r.shape[1]

    def body(dx_ref, o_ref, mod_ref, w_ref, r_ref, da_ref, do_ref, st_ref):
        @pl.when(pl.program_id(0) == 0)
        def _():
            st_ref[...] = jnp.zeros_like(st_ref)

        dx = dx_ref[...]
        d_o = (dx * mod_ref[G2:G2 + 1, :]).astype(BF16)
        do_ref[...] = d_o
        st_ref[G2:G2 + 1, :] += _colsum(dx * o_ref[...].astype(F32))
        for j in range(Fh // tn):
            cols = slice(j * tn, (j + 1) * tn)
            dz = _dot_nt(d_o, w_ref[cols, :])
            da_ref[:, cols] = (dz * (2.0 * r_ref[:, cols].astype(F32))).astype(BF16)

    return pl.pallas_call(
        body, name=name, grid=(S // tm,),
        in_specs=[_rows1(tm, D), _rows1(tm, D), _resident(mod.shape), _resident(w2.shape), _rows1(tm, Fh)],
        out_specs=[_rows1(tm, Fh), _rows1(tm, D), pl.BlockSpec((8, D), lambda i: (0, 0))],
        out_shape=[SDS((S, Fh), BF16), SDS((S, D), BF16), SDS((8, D), F32)],
        compiler_params=_params(("arbitrary",)),
    )(dx2, o, mod, w2, r)


def _mlp_bwd_b(d_a, w1, x1, dx2, mod, gn, tm, name):
    S, Fh = d_a.shape
    D = w1.shape[0]

    def body(da_ref, w_ref, x_ref, dx_ref, mod_ref, gn_ref, dx1_ref, st_ref):
        @pl.when(pl.program_id(0) == 0)
        def _():
            st_ref[...] = jnp.zeros_like(st_ref)

        dh = _dot_nt(da_ref[...], w_ref[...])
        n, rr = _rms(x_ref[...])
        gn_v = gn_ref[...]
        sc1p = 1.0 + mod_ref[SC2:SC2 + 1, :]
        t = _colsum(dh * n)
        st_ref[SH2:SH2 + 1, :] += _colsum(dh)
        st_ref[SC2:SC2 + 1, :] += t * gn_v
        st_ref[6:7, :] += t * sc1p
        dx1_ref[...] = dx_ref[...] + _normmod_bwd(dh, n, rr, gn_v * sc1p)

    return pl.pallas_call(
        body, name=name, grid=(S // tm,),
        in_specs=[_rows1(tm, Fh), _resident(w1.shape), _rows1(tm, D), _rows1(tm, D), _resident(mod.shape),
                  _resident(gn.shape)],
        out_specs=[_rows1(tm, D), pl.BlockSpec((8, D), lambda i: (0, 0))],
        out_shape=[SDS((S, D), F32), SDS((8, D), F32)],
        compiler_params=_params(("arbitrary",)),
    )(d_a, w1, x1, dx2, mod, gn)


def _mm_tn(a, g, tk, tn, name, square_a=False, col_shards=False):
    S, K1 = a.shape
    N = g.shape[1]
    w = N // NDEV
    per = tn // w if col_shards else 1

    def body(a_ref, g_ref, o_ref):
        av = a_ref[...]
        if square_a:
            av = av * av
        res = _dot_tn(av, g_ref[...]).astype(BF16)
        if col_shards:
            for s in range(per):
                o_ref[s] = res[:, s * w:(s + 1) * w]
        else:
            o_ref[...] = res

    if col_shards:
        out_spec, out_shape = pl.BlockSpec((per, tk, w), lambda i, j: (j, i, 0)), SDS((NDEV, K1, w), BF16)
    else:
        out_spec, out_shape = pl.BlockSpec((tk, tn), lambda i, j: (i, j)), SDS((K1, N), BF16)
    return pl.pallas_call(
        body, name=name, grid=(K1 // tk, N // tn),
        in_specs=[pl.BlockSpec((S, tk), lambda i, j: (0, i)), pl.BlockSpec((S, tn), lambda i, j: (0, j))],
        out_specs=out_spec, out_shape=out_shape,
        compiler_params=_params(("parallel", "parallel")),
    )(a, g)


def _pool_h_ext(x_ref, xp_ref, mod_ref, gn_ref, i, tm):
    ext = jnp.concatenate([xp_ref[...], x_ref[...]], axis=0)
    n, r = _rms(ext)
    a = gn_ref[...] * (1.0 + mod_ref[SC1:SC1 + 1, :])
    h = n * a + mod_ref[SH1:SH1 + 1, :]
    row = lax.broadcasted_iota(jnp.int32, (tm + HALO, 1), 0)
    h = jnp.where(jnp.logical_and(i == 0, row < HALO), 0.0, h)
    return h, n[HALO:], r[HALO:], a


def _trailing_sum(v, win):
    k = 1
    while k < win:
        v = v + pltpu.roll(v, k, 0)
        k *= 2
    return v


def _leading_sum(v, win):
    k = 1
    while k < win:
        v = v + pltpu.roll(v, v.shape[0] - k, 0)
        k *= 2
    return v


def _pool_fwd(x, mod, gn, pw, ps, tm, name):
    S, D = x.shape
    C = D // len(POOL_WINDOWS)
    hb = tm // HALO

    def body(x_ref, xp_ref, mod_ref, gn_ref, pw_ref, ps_ref, x1_ref):
        i = pl.program_id(0)
        h, _, _, _ = _pool_h_ext(x_ref, xp_ref, mod_ref, gn_ref, i, tm)
        t1 = (i * tm + lax.broadcasted_iota(jnp.int32, (tm, 1), 0)).astype(F32) + 1.0
        for g, win in enumerate(POOL_WINDOWS):
            cols = slice(g * C, (g + 1) * C)
            hg = h[:, cols]
            inv = 1.0 / jnp.minimum(t1, float(win))
            pooled = (_trailing_sum(hg, win)[HALO:] * inv - hg[HALO:]).astype(BF16)
            y = _dot(pooled, pw_ref[g]) * ps_ref[:, cols]
            x1_ref[:, cols] = x_ref[:, cols] + mod_ref[G1:G1 + 1, cols] * y

    return pl.pallas_call(
        body, name=name, grid=(S // tm,),
        in_specs=[_rows1(tm, D), pl.BlockSpec((HALO, D), lambda i: (jnp.maximum(i * hb - 1, 0), 0)),
                  _resident(mod.shape), _resident(gn.shape), _resident(pw.shape), _resident(ps.shape)],
        out_specs=_rows1(tm, D),
        out_shape=SDS((S, D), F32),
        compiler_params=_params(("parallel",)),
    )(x, x, mod, gn, pw, ps)


def _pool_bwd(x, dx1, mod, gn, pw, ps, tm, name):
    S, D = x.shape
    G = len(POOL_WINDOWS)
    C = D // G
    hb = tm // HALO
    nt = S // tm

    def body(x_ref, xp_ref, d1_ref, dn_ref, mod_ref, gn_ref, pw_ref, ps_ref, dx_ref, st_ref, dpw_ref):
        i = pl.program_id(0)

        @pl.when(i == 0)
        def _():
            st_ref[...] = jnp.zeros_like(st_ref)
            dpw_ref[...] = jnp.zeros_like(dpw_ref)

        h, n, rr, a = _pool_h_ext(x_ref, xp_ref, mod_ref, gn_ref, i, tm)
        g1 = mod_ref[G1:G1 + 1, :]
        ps_v = ps_ref[...]
        d1 = d1_ref[...]
        d1n = jnp.where(i == nt - 1, 0.0, dn_ref[...])
        dyr = (jnp.concatenate([d1, d1n], axis=0) * (g1 * ps_v)).astype(BF16)
        t1 = (i * tm + lax.broadcasted_iota(jnp.int32, (tm + HALO, 1), 0)).astype(F32) + 1.0
        parts = []
        for g, win in enumerate(POOL_WINDOWS):
            cols = slice(g * C, (g + 1) * C)
            hg = h[:, cols]
            inv = 1.0 / jnp.minimum(t1, float(win))
            pooled = (_trailing_sum(hg, win)[HALO:] * inv[:tm] - hg[HALO:]).astype(BF16)
            yraw = _dot(pooled, pw_ref[g])
            st_ref[G1:G1 + 1, cols] += _colsum(d1[:, cols] * (yraw * ps_v[:, cols]))
            st_ref[4:5, cols] += _colsum(d1[:, cols] * g1[:, cols] * yraw)
            dpw_ref[g] += _dot_tn(pooled, dyr[:tm, cols])
            dpool = _dot_nt(dyr[:, cols], pw_ref[g])
            parts.append(_leading_sum(dpool * inv, win)[:tm] - dpool[:tm])
        dh = jnp.concatenate(parts, axis=1)
        t = _colsum(dh * n)
        st_ref[SH1:SH1 + 1, :] += _colsum(dh)
        st_ref[SC1:SC1 + 1, :] += t * gn_ref[...]
        st_ref[3:4, :] += t * (1.0 + mod_ref[SC1:SC1 + 1, :])
        dx_ref[...] = d1 + _normmod_bwd(dh, n, rr, a)

    return pl.pallas_call(
        body, name=name, grid=(nt,),
        in_specs=[_rows1(tm, D), pl.BlockSpec((HALO, D), lambda i: (jnp.maximum(i * hb - 1, 0), 0)),
                  _rows1(tm, D), pl.BlockSpec((HALO, D), lambda i: (jnp.minimum((i + 1) * hb, S // HALO - 1), 0)),
                  _resident(mod.shape), _resident(gn.shape), _resident(pw.shape), _resident(ps.shape)],
        out_specs=[_rows1(tm, D), pl.BlockSpec((8, D), lambda i: (0, 0)), pl.BlockSpec((G, C, C), lambda i: (0, 0, 0))],
        out_shape=[SDS((S, D), F32), SDS((8, D), F32), SDS((G, C, C), F32)],
        compiler_params=_params(("arbitrary",)),
    )(x, x, dx1, dx1, mod, gn, pw, ps)


def _tril_bf16(w):
    row = lax.broadcasted_iota(jnp.int32, w.shape, 0)
    col = lax.broadcasted_iota(jnp.int32, w.shape, 1)
    return jnp.where(col <= row, w, 0.0).astype(BF16)


def _sgu_front(pre, lng_ref, lnb_ref, W):
    z = _gelu(pre)
    u, v = z[:, :W], z[:, W:]
    mu = jnp.mean(v, axis=-1, keepdims=True)
    xc = v - mu
    rstd = lax.rsqrt(jnp.mean(xc * xc, axis=-1, keepdims=True) + LN_EPS)
    vhat = xc * rstd
    return u, vhat, rstd, vhat * lng_ref[...] + lnb_ref[...]


def _sgu_mix(vn, ws_ref, bst_ref, mix_s, tm, W):
    for hd in range(W // SGU_HEAD):
        wm = _tril_bf16(ws_ref[hd])
        bcol = bst_ref[:, hd:hd + 1]
        for ci in range(tm // SGU_CHUNK):
            rs, cs = slice(ci * SGU_CHUNK, (ci + 1) * SGU_CHUNK), slice(hd * SGU_HEAD, (hd + 1) * SGU_HEAD)
            mix_s[rs, cs] = _dot(wm, vn[rs, cs].astype(BF16)) + bcol


def _sgu_fwd(x, mod, gn, w_in, lng, lnb, ws, bst, w_out, tm, name):
    S, D = x.shape
    W = w_out.shape[0]

    def body(x_ref, mod_ref, gn_ref, win_ref, lng_ref, lnb_ref, ws_ref, bst_ref, wout_ref,
             x1_ref, h_ref, pre_ref, y_ref, mix_s):
        n, _ = _rms(x_ref[...])
        a = gn_ref[...] * (1.0 + mod_ref[SC1:SC1 + 1, :])
        h = (n * a + mod_ref[SH1:SH1 + 1, :]).astype(BF16)
        h_ref[...] = h
        pre = _dot(h, win_ref[...])
        pre_ref[...] = pre.astype(BF16)
        u, _, _, vn = _sgu_front(pre, lng_ref, lnb_ref, W)
        _sgu_mix(vn, ws_ref, bst_ref, mix_s, tm, W)
        y = _dot((u * mix_s[...]).astype(BF16), wout_ref[...])
        y_ref[...] = y.astype(BF16)
        x1_ref[...] = x_ref[...] + mod_ref[G1:G1 + 1, :] * y

    return pl.pallas_call(
        body, name=name, grid=(S // tm,),
        in_specs=[_rows1(tm, D), _resident(mod.shape), _resident(gn.shape), _resident(w_in.shape),
                  _resident(lng.shape), _resident(lnb.shape), _resident(ws.shape), _resident(bst.shape),
                  _resident(w_out.shape)],
        out_specs=[_rows1(tm, D), _rows1(tm, D), _rows1(tm, 2 * W), _rows1(tm, D)],
        out_shape=[SDS((S, D), F32), SDS((S, D), BF16), SDS((S, 2 * W), BF16), SDS((S, D), BF16)],
        scratch_shapes=[pltpu.VMEM((tm, W), F32)],
        compiler_params=_params(("parallel",)),
    )(x, mod, gn, w_in, lng, lnb, ws, bst, w_out)


def _sgu_bwd(x, dx1, pre, y, mod, gn, w_in, lng, lnb, ws, bst, w_out, tm, name):
    S, D = x.shape
    W = w_out.shape[0]
    H = W // SGU_HEAD
    nt = S // tm

    def body(x_ref, d1_ref, pre_ref, y_ref, mod_ref, gn_ref, win_ref, lng_ref, lnb_ref, ws_ref, bst_ref, wout_ref,
             dx_ref, dy_ref, gt_ref, dpre_ref, st_ref, dws_ref, dbs_ref, mix_s, dvn_s):
        i = pl.program_id(0)

        @pl.when(i == 0)
        def _():
            st_ref[...] = jnp.zeros_like(st_ref)
            dws_ref[...] = jnp.zeros_like(dws_ref)
            dbs_ref[...] = jnp.zeros_like(dbs_ref)

        d1 = d1_ref[...]
        pre = pre_ref[...].astype(F32)
        u, vhat, rstd, vn = _sgu_front(pre, lng_ref, lnb_ref, W)
        _sgu_mix(vn, ws_ref, bst_ref, mix_s, tm, W)
        mixed = mix_s[...]
        gt_ref[...] = (u * mixed).astype(BF16)
        dyb = (d1 * mod_ref[G1:G1 + 1, :]).astype(BF16)
        dy_ref[...] = dyb
        st_ref[G1:G1 + 1, :] += _colsum(d1 * y_ref[...].astype(F32))
        dgt = _dot_nt(dyb, wout_ref[...])
        du = dgt * mixed
        dmix = dgt * u
        for hd in range(H):
            wm = _tril_bf16(ws_ref[hd])
            for ci in range(tm // SGU_CHUNK):
                rs, cs = slice(ci * SGU_CHUNK, (ci + 1) * SGU_CHUNK), slice(hd * SGU_HEAD, (hd + 1) * SGU_HEAD)
                dm = dmix[rs, cs]
                dmb = dm.astype(BF16)
                dbs_ref[hd] += jnp.broadcast_to(jnp.sum(dm, axis=1, keepdims=True), (SGU_CHUNK, LANES))
                dws_ref[hd] += _dot_nt(dmb, vn[rs, cs].astype(BF16))
                dvn_s[rs, cs] = _dot_tn(wm, dmb)
        dvn = dvn_s[...]
        st_ref[4:5, :] += _colsum(dvn * vhat)
        st_ref[5:6, :] += _colsum(dvn)
        dvh = dvn * lng_ref[...]
        dv = rstd * (dvh - jnp.mean(dvh, axis=-1, keepdims=True) - vhat * jnp.mean(dvh * vhat, axis=-1, keepdims=True))
        dpre_u = (du * _gelu_grad(pre[:, :W])).astype(BF16)
        dpre_v = (dv * _gelu_grad(pre[:, W:])).astype(BF16)
        dpre_ref[:, :W] = dpre_u
        dpre_ref[:, W:] = dpre_v
        dh = _dot_nt(dpre_u, win_ref[:, :W]) + _dot_nt(dpre_v, win_ref[:, W:])
        n, rr = _rms(x_ref[...])
        gn_v = gn_ref[...]
        sc1p = 1.0 + mod_ref[SC1:SC1 + 1, :]
        t = _colsum(dh * n)
        st_ref[SH1:SH1 + 1, :] += _colsum(dh)
        st_ref[SC1:SC1 + 1, :] += t * gn_v
        st_ref[3:4, :] += t * sc1p
        dx_ref[...] = d1 + _normmod_bwd(dh, n, rr, gn_v * sc1p)

        @pl.when(i == nt - 1)
        def _():
            for hd in range(H):
                row = lax.broadcasted_iota(jnp.int32, (SGU_CHUNK, SGU_CHUNK), 0)
                col = lax.broadcasted_iota(jnp.int32, (SGU_CHUNK, SGU_CHUNK), 1)
                dws_ref[hd] = jnp.where(col <= row, dws_ref[hd], 0.0)

    return pl.pallas_call(
        body, name=name, grid=(nt,),
        in_specs=[_rows1(tm, D), _rows1(tm, D), _rows1(tm, 2 * W), _rows1(tm, D), _resident(mod.shape),
                  _resident(gn.shape), _resident(w_in.shape), _resident(lng.shape), _resident(lnb.shape),
                  _resident(ws.shape), _resident(bst.shape), _resident(w_out.shape)],
        out_specs=[_rows1(tm, D), _rows1(tm, D), _rows1(tm, W), _rows1(tm, 2 * W),
                   pl.BlockSpec((8, D), lambda i: (0, 0)), pl.BlockSpec((H, SGU_CHUNK, SGU_CHUNK), lambda i: (0, 0, 0)),
                   pl.BlockSpec((H, SGU_CHUNK, LANES), lambda i: (0, 0, 0))],
        out_shape=[SDS((S, D), F32), SDS((S, D), BF16), SDS((S, W), BF16), SDS((S, 2 * W), BF16),
                   SDS((8, D), F32), SDS((H, SGU_CHUNK, SGU_CHUNK), F32), SDS((H, SGU_CHUNK, LANES), F32)],
        scratch_shapes=[pltpu.VMEM((tm, W), F32), pltpu.VMEM((tm, W), F32)],
        compiler_params=_params(("arbitrary",)),
    )(x, dx1, pre, y, mod, gn, w_in, lng, lnb, ws, bst, w_out)


def _mla_lat(x, mod, gn, wd, qg, kvg, cos_t, sin_t, tm, name):
    S, D = x.shape
    LW = wd.shape[1]
    QL, KL = MLA_Q_LORA, MLA_KV_LORA

    def body(x_ref, mod_ref, gn_ref, wd_ref, qg_ref, kvg_ref, c_ref, s_ref, h_ref, lat_ref, cq_ref, ckv_ref, kr_ref):
        n, _ = _rms(x_ref[...])
        a = gn_ref[...] * (1.0 + mod_ref[SC1:SC1 + 1, :])
        h = (n * a + mod_ref[SH1:SH1 + 1, :]).astype(BF16)
        h_ref[...] = h
        lat = _dot(h, wd_ref[...])
        lat_ref[...] = lat
        nq, _ = _rms(lat[:, :QL])
        cq_ref[...] = (nq * qg_ref[...]).astype(BF16)
        nkv, _ = _rms(lat[:, QL:QL + KL])
        ckv_ref[...] = (nkv * kvg_ref[...]).astype(BF16)
        kr = lat[:, QL + KL:]
        kr_ref[...] = (kr * c_ref[...] + _swap_halves(kr) * s_ref[...]).astype(BF16)

    return pl.pallas_call(
        body, name=name, grid=(S // tm,),
        in_specs=[_rows1(tm, D), _resident(mod.shape), _resident(gn.shape), _resident(wd.shape), _resident(qg.shape),
                  _resident(kvg.shape), _rows1(tm, LANES), _rows1(tm, LANES)],
        out_specs=[_rows1(tm, D), _rows1(tm, LW), _rows1(tm, QL), _rows1(tm, KL), _rows1(tm, LANES)],
        out_shape=[SDS((S, D), BF16), SDS((S, LW), F32), SDS((S, QL), BF16), SDS((S, KL), BF16), SDS((S, LANES), BF16)],
        compiler_params=_params(("parallel",)),
    )(x, mod, gn, wd, qg, kvg, cos_t, sin_t)


def _mla_qkv(cq, ckv, krp, wq, wukv, cos_t, sin_t, tm, name):
    S = cq.shape[0]
    H = wq.shape[1] // MLA_HEAD_PAD
    HP = MLA_HEAD_PAD

    def body(cq_ref, ckv_ref, kr_ref, wq_ref, wkv_ref, c_ref, s_ref, q_ref, k_ref, v_ref):
        q = _dot(cq_ref[...], wq_ref[...])
        kv = _dot(ckv_ref[...], wkv_ref[...])
        cv, sv, krv = c_ref[...], s_ref[...], kr_ref[...]
        for h in range(H):
            qr = q[:, h * HP + MLA_NOPE:(h + 1) * HP]
            q_ref[:, h * HP:h * HP + MLA_NOPE] = (q[:, h * HP:h * HP + MLA_NOPE] * SM_SCALE).astype(BF16)
            q_ref[:, h * HP + MLA_NOPE:(h + 1) * HP] = ((qr * cv + _swap_halves(qr) * sv) * SM_SCALE).astype(BF16)
            k_ref[:, h * HP:h * HP + MLA_NOPE] = kv[:, h * HP:h * HP + MLA_NOPE].astype(BF16)
            k_ref[:, h * HP + MLA_NOPE:(h + 1) * HP] = krv
            v_ref[:, h * MLA_V:(h + 1) * MLA_V] = kv[:, h * HP + MLA_NOPE:(h + 1) * HP].astype(BF16)

    return pl.pallas_call(
        body, name=name, grid=(S // tm,),
        in_specs=[_rows1(tm, MLA_Q_LORA), _rows1(tm, MLA_KV_LORA), _rows1(tm, LANES), _resident(wq.shape),
                  _resident(wukv.shape), _rows1(tm, LANES), _rows1(tm, LANES)],
        out_specs=[_rows1(tm, H * HP), _rows1(tm, H * HP), _rows1(tm, H * MLA_V)],
        out_shape=[SDS((S, H * HP), BF16), SDS((S, H * HP), BF16), SDS((S, H * MLA_V), BF16)],
        compiler_params=_params(("parallel",)),
    )(cq, ckv, krp, wq, wukv, cos_t, sin_t)


def _causal_mask(tq):
    row = lax.broadcasted_iota(jnp.int32, (tq, tq), 0)
    col = lax.broadcasted_iota(jnp.int32, (tq, tq), 1)
    return col <= row


def _tile_rows(j, tq):
    return slice(j * tq, (j + 1) * tq) if isinstance(j, int) else pl.ds(pl.multiple_of(j * tq, tq), tq)


def _attn_fwd(q, k, v, tq, name):
    S = q.shape[0]
    HP = MLA_HEAD_PAD
    H = q.shape[1] // HP
    nq = S // tq

    def body(q_ref, k_ref, v_ref, o_ref, lse_ref):
        for i in range(nq):
            rows = slice(i * tq, (i + 1) * tq)
            qv = q_ref[rows, :]

            def step(j, carry, masked, qv=qv):
                m, l, acc = carry
                krows = _tile_rows(j, tq)
                s = _dot_nt(qv, k_ref[krows, :])
                if masked:
                    s = jnp.where(_causal_mask(tq), s, NEG)
                m_new = jnp.maximum(m, jnp.max(s, axis=1, keepdims=True))
                p = jnp.exp(s - m_new)
                alpha = jnp.exp(m - m_new)
                l = alpha * l + jnp.sum(p, axis=1, keepdims=True)
                acc = alpha * acc + _dot(p.astype(BF16), v_ref[krows, :])
                return m_new, l, acc

            carry = (jnp.full((tq, 1), NEG, F32), jnp.zeros((tq, 1), F32), jnp.zeros((tq, MLA_V), F32))
            for j in range(i):
                carry = step(j, carry, False)
            m, l, acc = step(i, carry, True)
            o_ref[rows, :] = (acc / l).astype(BF16)
            lse_ref[0, rows, :] = jnp.broadcast_to(m + jnp.log(l), (tq, LANES))

    return pl.pallas_call(
        body, name=name, grid=(H,),
        in_specs=[pl.BlockSpec((S, HP), lambda h: (0, h)), pl.BlockSpec((S, HP), lambda h: (0, h)),
                  pl.BlockSpec((S, MLA_V), lambda h: (0, h))],
        out_specs=[pl.BlockSpec((S, MLA_V), lambda h: (0, h)), pl.BlockSpec((1, S, LANES), lambda h: (h, 0, 0))],
        out_shape=[SDS((S, H * MLA_V), BF16), SDS((H, S, LANES), F32)],
        compiler_params=_params(("parallel",)),
    )(q, k, v)


def _attn_bwd(q, k, v, o, do, lse, cos_t, sin_t, tq, name):
    S = q.shape[0]
    HP = MLA_HEAD_PAD
    H = q.shape[1] // HP
    nq = S // tq

    def body(q_ref, k_ref, v_ref, o_ref, do_ref, lse_ref, c_ref, s_ref, dq_ref, dkv_ref, dkr_ref, dq_acc, dl_s):
        @pl.when(pl.program_id(0) == 0)
        def _():
            dkr_ref[...] = jnp.zeros_like(dkr_ref)

        dq_acc[...] = jnp.zeros_like(dq_acc)

        def delta_tile(i, _):
            rows = pl.ds(pl.multiple_of(i * tq, tq), tq)
            d = jnp.sum(do_ref[rows, :].astype(F32) * o_ref[rows, :].astype(F32), axis=1, keepdims=True)
            dl_s[rows, :] = jnp.broadcast_to(d, (tq, LANES))
            return 0

        lax.fori_loop(0, nq, delta_tile, 0)

        for j in range(nq):
            krows = slice(j * tq, (j + 1) * tq)
            kv_k = k_ref[krows, :]
            kv_v = v_ref[krows, :]

            def step(i, carry, masked, kv_k=kv_k, kv_v=kv_v):
                dk, dv = carry
                rows = _tile_rows(i, tq)
                qv = q_ref[rows, :]
                dov = do_ref[rows, :]
                s = _dot_nt(qv, kv_k)
                if masked:
                    s = jnp.where(_causal_mask(tq), s, NEG)
                p = jnp.exp(s - lse_ref[0, rows, 0:1])
                dv = dv + _dot_tn(p.astype(BF16), dov)
                dp = _dot_nt(dov, kv_v)
                ds = (p * (dp - dl_s[rows, 0:1])).astype(BF16)
                dk = dk + _dot_tn(ds, qv)
                dq_acc[rows, :] += _dot(ds, kv_k)
                return dk, dv

            carry = step(j, (jnp.zeros((tq, HP), F32), jnp.zeros((tq, MLA_V), F32)), True)
            for i in range(j + 1, nq):
                carry = step(i, carry, False)
            dk, dv = carry
            dkv_ref[krows, :MLA_NOPE] = dk[:, :MLA_NOPE].astype(BF16)
            dkv_ref[krows, MLA_NOPE:] = dv.astype(BF16)
            dkr_ref[krows, :] += dk[:, MLA_NOPE:]

        def out_tile(i, _):
            rows = pl.ds(pl.multiple_of(i * tq, tq), tq)
            dq = dq_acc[rows, :] * SM_SCALE
            dqr = dq[:, MLA_NOPE:]
            dq_ref[rows, :MLA_NOPE] = dq[:, :MLA_NOPE].astype(BF16)
            dq_ref[rows, MLA_NOPE:] = (dqr * c_ref[rows, :] + _swap_halves(dqr * s_ref[rows, :])).astype(BF16)
            return 0

        lax.fori_loop(0, nq, out_tile, 0)

    return pl.pallas_call(
        body, name=name, grid=(H,),
        in_specs=[pl.BlockSpec((S, HP), lambda h: (0, h)), pl.BlockSpec((S, HP), lambda h: (0, h)),
                  pl.BlockSpec((S, MLA_V), lambda h: (0, h)), pl.BlockSpec((S, MLA_V), lambda h: (0, h)),
                  pl.BlockSpec((S, MLA_V), lambda h: (0, h)), pl.BlockSpec((1, S, LANES), lambda h: (h, 0, 0)),
                  _resident(cos_t.shape), _resident(sin_t.shape)],
        out_specs=[pl.BlockSpec((S, HP), lambda h: (0, h)), pl.BlockSpec((S, HP), lambda h: (0, h)),
                   pl.BlockSpec((S, LANES), lambda h: (0, 0))],
        out_shape=[SDS((S, H * HP), BF16), SDS((S, H * HP), BF16), SDS((S, LANES), F32)],
        scratch_shapes=[pltpu.VMEM((S, HP), F32), pltpu.VMEM((S, LANES), F32)],
        compiler_params=_params(("arbitrary",)),
    )(q, k, v, o, do, lse, cos_t, sin_t)


def _mla_out(o, w_o, x, mod, tm, name):
    S, KO = o.shape
    D = w_o.shape[1]

    def body(o_ref, w_ref, x_ref, mod_ref, x1_ref, y_ref):
        y = _dot(o_ref[...], w_ref[...])
        y_ref[...] = y.astype(BF16)
        x1_ref[...] = x_ref[...] + mod_ref[G1:G1 + 1, :] * y

    return pl.pallas_call(
        body, name=name, grid=(S // tm,),
        in_specs=[_rows1(tm, KO), _resident(w_o.shape), _rows1(tm, D), _resident(mod.shape)],
        out_specs=[_rows1(tm, D), _rows1(tm, D)],
        out_shape=[SDS((S, D), F32), SDS((S, D), BF16)],
        compiler_params=_params(("parallel",)),
    )(o, w_o, x, mod)


def _mla_bwd_o(dx1, y, mod, w_o, tm, name):
    S, D = dx1.shape
    KO = w_o.shape[0]

    def body(d1_ref, y_ref, mod_ref, w_ref, dy_ref, do_ref, st_ref):
        @pl.when(pl.program_id(0) == 0)
        def _():
            st_ref[...] = jnp.zeros_like(st_ref)

        d1 = d1_ref[...]
        dyb = (d1 * mod_ref[G1:G1 + 1, :]).astype(BF16)
        dy_ref[...] = dyb
        st_ref[G1:G1 + 1, :] += _colsum(d1 * y_ref[...].astype(F32))
        do_ref[...] = _dot_nt(dyb, w_ref[...]).astype(BF16)

    return pl.pallas_call(
        body, name=name, grid=(S // tm,),
        in_specs=[_rows1(tm, D), _rows1(tm, D), _resident(mod.shape), _resident(w_o.shape)],
        out_specs=[_rows1(tm, D), _rows1(tm, KO), pl.BlockSpec((8, D), lambda i: (0, 0))],
        out_shape=[SDS((S, D), BF16), SDS((S, KO), BF16), SDS((8, D), F32)],
        compiler_params=_params(("arbitrary",)),
    )(dx1, y, mod, w_o)


def _mla_bwd_lat(dq, dkv, dkr, lat, x, dx1, mod, gn, qg, kvg, wq, wukv, wd, cos_t, sin_t, tm, name):
    S, D = x.shape
    LW = wd.shape[1]
    QL, KL = MLA_Q_LORA, MLA_KV_LORA

    def body(dq_ref, dkv_ref, dkr_ref, lat_ref, x_ref, d1_ref, mod_ref, gn_ref, qg_ref, kvg_ref, wq_ref, wkv_ref, wd_ref,
             c_ref, s_ref, dx_ref, dlat_ref, st_ref, dqg_ref, dkvg_ref):
        @pl.when(pl.program_id(0) == 0)
        def _():
            st_ref[...] = jnp.zeros_like(st_ref)
            dqg_ref[...] = jnp.zeros_like(dqg_ref)
            dkvg_ref[...] = jnp.zeros_like(dkvg_ref)

        lat = lat_ref[...]
        dcq = _dot_nt(dq_ref[...], wq_ref[...])
        nq, rq = _rms(lat[:, :QL])
        dqg_ref[0:1, :] += _colsum(dcq * nq)
        dlat_q = _normmod_bwd(dcq, nq, rq, qg_ref[...]).astype(BF16)
        dckv = _dot_nt(dkv_ref[...], wkv_ref[...])
        nkv, rkv = _rms(lat[:, QL:QL + KL])
        dkvg_ref[0:1, :] += _colsum(dckv * nkv)
        dlat_kv = _normmod_bwd(dckv, nkv, rkv, kvg_ref[...]).astype(BF16)
        dkr = dkr_ref[...]
        dlat_kr = (dkr * c_ref[...] + _swap_halves(dkr * s_ref[...])).astype(BF16)
        dlat_ref[:, :QL] = dlat_q
        dlat_ref[:, QL:QL + KL] = dlat_kv
        dlat_ref[:, QL + KL:] = dlat_kr
        dh = (_dot_nt(dlat_q, wd_ref[:, :QL]) + _dot_nt(dlat_kv, wd_ref[:, QL:QL + KL])
              + _dot_nt(dlat_kr, wd_ref[:, QL + KL:]))
        n, rr = _rms(x_ref[...])
        gn_v = gn_ref[...]
        sc1p = 1.0 + mod_ref[SC1:SC1 + 1, :]
        t = _colsum(dh * n)
        st_ref[SH1:SH1 + 1, :] += _colsum(dh)
        st_ref[SC1:SC1 + 1, :] += t * gn_v
        st_ref[3:4, :] += t * sc1p
        dx_ref[...] = d1_ref[...] + _normmod_bwd(dh, n, rr, gn_v * sc1p)

    HW = wq.shape[1]
    return pl.pallas_call(
        body, name=name, grid=(S // tm,),
        in_specs=[_rows1(tm, HW), _rows1(tm, HW), _rows1(tm, LANES), _rows1(tm, LW), _rows1(tm, D), _rows1(tm, D),
                  _resident(mod.shape), _resident(gn.shape), _resident(qg.shape), _resident(kvg.shape),
                  _resident(wq.shape), _resident(wukv.shape), _resident(wd.shape), _rows1(tm, LANES), _rows1(tm, LANES)],
        out_specs=[_rows1(tm, D), _rows1(tm, LW), pl.BlockSpec((8, D), lambda i: (0, 0)),
                   pl.BlockSpec((8, QL), lambda i: (0, 0)), pl.BlockSpec((8, KL), lambda i: (0, 0))],
        out_shape=[SDS((S, D), F32), SDS((S, LW), BF16), SDS((8, D), F32), SDS((8, QL), F32), SDS((8, KL), F32)],
        compiler_params=_params(("arbitrary",)),
    )(dq, dkv, dkr, lat, x, dx1, mod, gn, qg, kvg, wq, wukv, wd, cos_t, sin_t)


def _loss_head(x, tgt, fg, tm, name):
    S, D = x.shape
    nt = S // tm

    def body(x_ref, t_ref, g_ref, dx_ref, acc_ref):
        i = pl.program_id(0)

        @pl.when(i == 0)
        def _():
            acc_ref[...] = jnp.zeros_like(acc_ref)

        n, rr = _rms(x_ref[...])
        g = g_ref[...]
        err = n * g - t_ref[...]
        acc_ref[1:2, :] += _colsum(err * err) * (0.5 / D)
        dy = err * (1.0 / D)
        acc_ref[0:1, :] += _colsum(dy * n)
        dx_ref[...] = _normmod_bwd(dy, n, rr, g)

        @pl.when(i == nt - 1)
        def _():
            acc_ref[2:3, :] = jnp.broadcast_to(jnp.sum(acc_ref[1:2, :], axis=1, keepdims=True), (1, D))

    return pl.pallas_call(
        body, name=name, grid=(nt,),
        in_specs=[_rows1(tm, D), _rows1(tm, D), _resident(fg.shape)],
        out_specs=[_rows1(tm, D), pl.BlockSpec((8, D), lambda i: (0, 0))],
        out_shape=[SDS((S, D), F32), SDS((8, D), F32)],
        compiler_params=_params(("arbitrary",)),
    )(x, tgt, fg)


def _rope_tables(positions, S):
    inv_freq = ROPE_THETA ** (-jnp.arange(0, MLA_ROPE, 2, dtype=F32) / MLA_ROPE)
    ang = positions.reshape(S, 1).astype(F32) * inv_freq
    cos, sin = jnp.cos(ang), jnp.sin(ang)
    z = jnp.zeros((S, LANES - MLA_ROPE), F32)
    return jnp.concatenate([cos, cos, z], axis=1), jnp.concatenate([-sin, sin, z], axis=1)


def _pad_heads(w, per_head):
    K = w.shape[0]
    H = w.shape[1] // per_head
    w3 = w.reshape(K, H, per_head)
    return jnp.pad(w3, ((0, 0), (0, 0), (0, MLA_HEAD_PAD - per_head))).reshape(K, H * MLA_HEAD_PAD)


def _unpad_heads(w, per_head):
    K = w.shape[0]
    H = w.shape[1] // MLA_HEAD_PAD
    return w.reshape(K, H, MLA_HEAD_PAD)[:, :, :per_head].reshape(K, H * per_head)


def _tiles(S):
    return min(512, S), min(256, S), min(512, S)


def _layer_forward(i, x, mod, gmix, gmlp, w, small, tables):
    if True:
        S, D = x.shape
        tm, tms, tq = _tiles(S)
        cos_t, sin_t = tables
        mod, gmix, gmlp = {i: mod}, {i: gmix}, {i: gmlp}
        kind = i % 3
        sv = {"x": x}
        if kind == 0:
            x1 = _pool_fwd(x, mod[i], gmix[i], w["pool_w"], small["pool_scale"][i // 3], tm, f"pool_fwd_{i}")
        elif kind == 1:
            x1, sv["h"], sv["pre"], sv["y"] = _sgu_fwd(
                x, mod[i], gmix[i], w["sgu_w_in"], small["sgu_ln_g"], small["sgu_ln_b"], small["sgu_w_s"],
                small["sgu_b_s_t"], w["sgu_w_out"], tms, f"sgu_fwd_{i}")
        else:
            sv["h"], sv["lat"], sv["cq"], sv["ckv"], krp = _mla_lat(
                x, mod[i], gmix[i], w["mla_wd"], small["mla_q_norm_g"], small["mla_kv_norm_g"], cos_t, sin_t, tm,
                f"mla_lat_{i}")
            sv["q"], sv["k"], sv["v"] = _mla_qkv(sv["cq"], sv["ckv"], krp, w["mla_wq"], w["mla_w_ukv"], cos_t, sin_t,
                                                 tm, f"mla_qkv_{i}")
            sv["o"], sv["lse"] = _attn_fwd(sv["q"], sv["k"], sv["v"], tq, f"attn_fwd_{i}")
            x1, sv["y"] = _mla_out(sv["o"], w["mla_w_o"], x, mod[i], tm, f"mla_out_{i}")
        sv["x1"] = x1
        Fh = w["mlp_w1"].shape[1]
        sv["h2"], sv["r"] = _mlp_up(x1, mod[i], gmlp[i], w["mlp_w1"], tm, min(2048, Fh), f"mlp_up_{i}")
        if callable(w["mlp_w2"]):
            w["mlp_w2"] = w["mlp_w2"](sv["r"])
        x, sv["o2"] = _mlp_down(sv["r"], w["mlp_w2"], x1, mod[i], tm, f"mlp_down_{i}")
        return x, sv


def _layer_backward(i, dx, sv, mod, gmix, gmlp, w, small, tables, on_mlp_grads=None):
    if True:
        S, D = dx.shape
        tm, tms, tq = _tiles(S)
        cos_t, sin_t = tables
        kind = i % 3
        sgrads = {}
        Fh = w["mlp_w1"].shape[1]
        g = {}
        d_a, d_o, st_a = _mlp_bwd_a(dx, sv["o2"], mod, w["mlp_w2"], sv["r"], tm, min(2048, Fh), f"mlp_bwd_a_{i}")
        g["mlp_w2"] = _mm_tn(sv["r"], d_o, min(512, Fh), D, f"mlp_dw2_{i}", square_a=True)
        g["mlp_w1"] = _mm_tn(sv["h2"], d_a, D, min(512, Fh), f"mlp_dw1_{i}", col_shards=True)
        if on_mlp_grads is not None:
            mod = mod + on_mlp_grads({n: g.pop(n) for n in ("mlp_w1", "mlp_w2")})
        mod, gmix, gmlp = {i: mod}, {i: gmix}, {i: gmlp}
        dx1, st_b = _mlp_bwd_b(d_a, w["mlp_w1"], sv["x1"], dx, mod[i], gmlp[i], tm, f"mlp_bwd_b_{i}")
        if kind == 0:
            dx, st_m, dpw = _pool_bwd(sv["x"], dx1, mod[i], gmix[i], w["pool_w"], small["pool_scale"][i // 3], tm,
                                      f"pool_bwd_{i}")
            g["pool_w"] = dpw
            sgrads[f"pool_scale_{i // 3}"] = st_m[4:5]
        elif kind == 1:
            dx, dyb, gated, dpre, st_m, dws, dbs = _sgu_bwd(
                sv["x"], dx1, sv["pre"], sv["y"], mod[i], gmix[i], w["sgu_w_in"], small["sgu_ln_g"], small["sgu_ln_b"],
                small["sgu_w_s"], small["sgu_b_s_t"], w["sgu_w_out"], tms, f"sgu_bwd_{i}")
            W = gated.shape[1]
            g["sgu_w_out"] = _mm_tn(gated, dyb, min(512, W), D, f"sgu_dwout_{i}")
            g["sgu_w_in"] = _mm_tn(sv["h"], dpre, D, min(512, 2 * W), f"sgu_dwin_{i}", col_shards=True)
            sgrads["sgu_ln_g"], sgrads["sgu_ln_b"] = st_m[4:5], st_m[5:6]
            sgrads["sgu_w_s"], sgrads["sgu_b_s"] = dws, dbs[:, :, 0]
        else:
            dyb, do, st_o = _mla_bwd_o(dx1, sv["y"], mod[i], w["mla_w_o"], tm, f"mla_bwd_o_{i}")
            KO = do.shape[1]
            g["mla_w_o"] = _mm_tn(sv["o"], dyb, min(512, KO), D, f"mla_dwo_{i}")
            dq, dkv, dkr = _attn_bwd(sv["q"], sv["k"], sv["v"], sv["o"], do, sv["lse"], cos_t, sin_t, tq, f"attn_bwd_{i}")
            dx, dlat, st_m, dqg, dkvg = _mla_bwd_lat(
                dq, dkv, dkr, sv["lat"], sv["x"], dx1, mod[i], gmix[i], small["mla_q_norm_g"], small["mla_kv_norm_g"],
                w["mla_wq"], w["mla_w_ukv"], w["mla_wd"], cos_t, sin_t, tm, f"mla_bwd_lat_{i}")
            HW = dq.shape[1]
            g["mla_wq"] = _mm_tn(sv["cq"], dq, MLA_Q_LORA, min(1024, HW), f"mla_dwq_{i}")
            g["mla_w_ukv"] = _mm_tn(sv["ckv"], dkv, MLA_KV_LORA, min(1024, HW), f"mla_dwukv_{i}", col_shards=True)
            g["mla_wd"] = _mm_tn(sv["h"], dlat, D, dlat.shape[1], f"mla_dwd_{i}")
            st_m = jnp.concatenate([st_m[0:2], st_o[2:3], st_m[3:]], axis=0)
            sgrads["mla_q_norm_g"], sgrads["mla_kv_norm_g"] = dqg[0:1], dkvg[0:1]
        stats = jnp.concatenate([st_m[0:3], st_b[3:5], st_a[5:6], st_m[3:4], st_b[6:7]], axis=0)
        return dx, stats, g, sgrads


def _local_step(x, tgt, positions, mod, gmix, gmlp, fg, wts, small):
    S = x.shape[0]
    L = mod.shape[0]
    tables = _rope_tables(positions, S)
    saved = []
    for i in range(L):
        x, sv = _layer_forward(i, x, mod[i], gmix[i], gmlp[i], wts[i], small, tables)
        saved.append(sv)
    dx, loss_acc = _loss_head(x, tgt, fg, _tiles(S)[0], "loss_head")
    stats, grads, sgrads = [None] * L, [None] * L, {}
    for i in reversed(range(L)):
        dx, stats[i], grads[i], sg = _layer_backward(i, dx, saved[i], mod[i], gmix[i], gmlp[i], wts[i], small, tables)
        sgrads.update(sg)
    return loss_acc, dx, stats, grads, sgrads


HBM_SPEC = pl.BlockSpec(memory_space=pltpu.HBM)
VMEM_SPEC = pl.BlockSpec(memory_space=pltpu.VMEM)


def _my_place():
    return lax.axis_index("x"), lax.axis_index("y"), lax.axis_index("c")


def _flip(v, bit):
    return 1 - v if bit else v


def _small_all_gather(v, name):
    R, C = v.shape

    def body(x_ref, out_ref, send_sems, recv_sems):
        x, y, c = _my_place()
        me = 4 * x + 2 * y + c
        out_ref[me] = x_ref[...]
        sends = []
        for k in range(1, NDEV):
            peer = (_flip(x, k & 4), _flip(y, k & 2), _flip(c, k & 1))
            cp = pltpu.make_async_remote_copy(src_ref=x_ref, dst_ref=out_ref.at[me], send_sem=send_sems.at[k - 1],
                                              recv_sem=recv_sems.at[k - 1], device_id=peer, device_id_type=MESH)
            cp.start()
            sends.append(cp)
        for k in range(1, NDEV):
            src = 4 * _flip(x, k & 4) + 2 * _flip(y, k & 2) + _flip(c, k & 1)
            pltpu.make_async_remote_copy(src_ref=x_ref, dst_ref=out_ref.at[src], send_sem=send_sems.at[k - 1],
                                         recv_sem=recv_sems.at[k - 1], device_id=(x, y, c), device_id_type=MESH).wait_recv()
        for cp in sends:
            cp.wait_send()

    return pl.pallas_call(
        body, name=name, out_shape=SDS((NDEV, R, C), v.dtype), in_specs=[VMEM_SPEC], out_specs=VMEM_SPEC,
        scratch_shapes=[pltpu.SemaphoreType.DMA((NDEV - 1,)), pltpu.SemaphoreType.DMA((NDEV - 1,))],
        compiler_params=pltpu.CompilerParams(vmem_limit_bytes=V7X_VMEM_LIMIT),
    )(v)


def _slab(ref, axis, width, dev):
    idx = [slice(None)] * len(ref.shape)
    idx[axis] = pl.ds(pl.multiple_of(dev * width, width), width)
    return ref.at[tuple(idx)]


def _all_gather_group(shards, axes, after, name):
    nt = len(shards)
    out_shapes = [SDS(tuple(s * NDEV if a == ax else s for a, s in enumerate(sh.shape)), sh.dtype)
                  for sh, ax in zip(shards, axes)]

    def body(*refs):
        ins, outs = refs[:nt], refs[nt + 1:2 * nt + 1]
        send_sems, recv_sems, local_sems = refs[2 * nt + 1:]
        x, y, c = _my_place()
        me = 4 * x + 2 * y + c
        sibling = (x, y, 1 - c)
        chips = [(1 - x, y), (x, 1 - y), (1 - x, 1 - y)]

        def block(t, dev):
            return _slab(outs[t], axes[t], ins[t].shape[axes[t]], dev)

        def copy(t, k, dev, to, src=None):
            return pltpu.make_async_remote_copy(
                src_ref=block(t, dev) if src is None else src, dst_ref=block(t, dev), send_sem=send_sems.at[t, k],
                recv_sem=recv_sems.at[t, k], device_id=to, device_id_type=MESH)

        mine = [pltpu.make_async_copy(ins[t], block(t, me), local_sems.at[t]) for t in range(nt)]
        for cp in mine:
            cp.start()
        first = []
        for t in range(nt):
            first.append(copy(t, 0, me, sibling, src=ins[t]))
            first += [copy(t, 1 + j, me, (cx, cy, c), src=ins[t]) for j, (cx, cy) in enumerate(chips)]
        for cp in first:
            cp.start()
        passed = []
        for j, (cx, cy) in enumerate(chips):
            for t in range(nt):
                copy(t, 1 + j, 4 * cx + 2 * cy + c, (x, y, c)).wait_recv()
                cp = copy(t, 4 + j, 4 * cx + 2 * cy + c, sibling)
                cp.start()
                passed.append(cp)
        for t in range(nt):
            copy(t, 0, 4 * x + 2 * y + (1 - c), (x, y, c)).wait_recv()
        for j, (cx, cy) in enumerate(chips):
            for t in range(nt):
                copy(t, 4 + j, 4 * cx + 2 * cy + (1 - c), (x, y, c)).wait_recv()
        for cp in first + passed:
            cp.wait_send()
        for cp in mine:
            cp.wait()

    return pl.pallas_call(
        body, name=name, out_shape=out_shapes, in_specs=[HBM_SPEC] * nt + [ANY_SPEC], out_specs=[HBM_SPEC] * nt,
        scratch_shapes=[pltpu.SemaphoreType.DMA((nt, NDEV - 1)), pltpu.SemaphoreType.DMA((nt, NDEV - 1)),
                        pltpu.SemaphoreType.DMA((nt,))],
    )(*shards, after)


def _reduce_scatter_sibling(grads, name):
    nt = len(grads)
    NCH = NDEV // 2
    out_shapes = [SDS((NCH,) + gr.shape[1:], gr.dtype) for gr in grads]

    def body(*refs):
        ins, lands = refs[:nt], refs[nt:2 * nt]
        send_sems, recv_sems = refs[2 * nt:]
        x, y, c = _my_place()
        sends = []
        for t in range(nt):
            for k in range(NCH):
                cp = pltpu.make_async_remote_copy(
                    src_ref=ins[t].at[2 * k + (1 - c)], dst_ref=lands[t].at[k], send_sem=send_sems.at[t, k],
                    recv_sem=recv_sems.at[t, k], device_id=(x, y, 1 - c), device_id_type=MESH)
                cp.start()
                sends.append(cp)
        for cp in sends:
            cp.wait_recv()
        for cp in sends:
            cp.wait_send()

    return pl.pallas_call(
        body, name=name, out_shape=out_shapes, in_specs=[HBM_SPEC] * nt, out_specs=[HBM_SPEC] * nt,
        scratch_shapes=[pltpu.SemaphoreType.DMA((nt, NCH)), pltpu.SemaphoreType.DMA((nt, NCH))],
    )(*grads)


SEM_SPEC = pl.BlockSpec(memory_space=pltpu.SEMAPHORE)
ANY_SPEC = pl.BlockSpec(memory_space=pl.ANY)
SPLIT_PARAMS = pltpu.CompilerParams(has_side_effects=pltpu.SideEffectType.DATAFLOW_SIDE_EFFECTING)
TOKEN = SDS((8, LANES), F32)


def _in_hbm(arrays):
    return [pltpu.with_memory_space_constraint(v, pltpu.HBM) for v in arrays]


def _split_start(body, srcs, lands, after, n_sem, name):
    ns, nl = len(srcs), len(lands)
    bufs = list(srcs) + list(lands)
    res = pl.pallas_call(
        body, name=name,
        out_shape=(pltpu.SemaphoreType.DMA((ns * n_sem,)), pltpu.SemaphoreType.DMA((ns * n_sem,)),
                   *[pltpu.HBM(v.shape, v.dtype) for v in bufs], TOKEN),
        in_specs=[HBM_SPEC] * (ns + nl) + [ANY_SPEC],
        out_specs=(SEM_SPEC, SEM_SPEC, *[HBM_SPEC] * (ns + nl), VMEM_SPEC),
        input_output_aliases={t: 2 + t for t in range(ns + nl)}, compiler_params=SPLIT_PARAMS,
    )(*_in_hbm(bufs), after)
    return res[0], res[1], list(res[2:2 + ns]), list(res[2 + ns:2 + ns + nl]), res[-1]


def _split_wait(body, send_sems, recv_sems, srcs, lands, after, name):
    ns, nl = len(srcs), len(lands)
    bufs = list(srcs) + list(lands)
    res = pl.pallas_call(
        body, name=name, out_shape=tuple(pltpu.HBM(v.shape, v.dtype) for v in bufs),
        in_specs=[HBM_SPEC] * (ns + nl) + [SEM_SPEC, SEM_SPEC, ANY_SPEC], out_specs=tuple([HBM_SPEC] * (ns + nl)),
        input_output_aliases={t: t for t in range(ns + nl)}, compiler_params=SPLIT_PARAMS,
    )(*bufs, send_sems, recv_sems, after)
    return list(res[:ns]), list(res[ns:])


def _chips_exchange_start(parts, after, name):
    nt = len(parts)
    lands = [lax.empty((3,) + p.shape[1:], p.dtype) for p in parts]

    def body(*refs):
        ins, lnd = refs[:nt], refs[nt:2 * nt]
        send_sems, recv_sems, token = refs[2 * nt + 1], refs[2 * nt + 2], refs[-1]
        x, y, c = _my_place()
        for t in range(nt):
            for m in range(1, 4):
                px, py = _flip(x, m & 2), _flip(y, m & 1)
                pltpu.make_async_remote_copy(
                    src_ref=ins[t].at[2 * px + py], dst_ref=lnd[t].at[m - 1], send_sem=send_sems.at[3 * t + m - 1],
                    recv_sem=recv_sems.at[3 * t + m - 1], device_id=(px, py, c), device_id_type=MESH).start()
        token[...] = jnp.zeros_like(token)

    return _split_start(body, parts, lands, after, 3, name)


def _chips_exchange_wait(send_sems, recv_sems, parts, lands, after, name):
    nt = len(parts)

    def body(*refs):
        ins, lnd = refs[:nt], refs[nt:2 * nt]
        s_sems, r_sems = refs[2 * nt], refs[2 * nt + 1]
        x, y, c = _my_place()
        for t in range(nt):
            for m in range(1, 4):
                cp = pltpu.make_async_remote_copy(
                    src_ref=ins[t].at[0], dst_ref=lnd[t].at[m - 1], send_sem=s_sems.at[3 * t + m - 1],
                    recv_sem=r_sems.at[3 * t + m - 1], device_id=(x, y, c), device_id_type=MESH)
                cp.wait_send()
                cp.wait_recv()

    return _split_wait(body, send_sems, recv_sems, parts, lands, after, name)


def _place_own(shards, axes, name):
    nt = len(shards)
    out_shapes = [SDS(tuple(s * NDEV if a == ax else s for a, s in enumerate(sh.shape)), sh.dtype)
                  for sh, ax in zip(shards, axes)]

    def body(*refs):
        ins, outs, sems = refs[:nt], refs[nt:2 * nt], refs[2 * nt]
        x, y, c = _my_place()
        copies = [pltpu.make_async_copy(ins[t], _slab(outs[t], axes[t], ins[t].shape[axes[t]], 4 * x + 2 * y + c), sems.at[t])
                  for t in range(nt)]
        for cp in copies:
            cp.start()
        for cp in copies:
            cp.wait()

    return pl.pallas_call(
        body, name=name, out_shape=out_shapes, in_specs=[VMEM_SPEC] * nt, out_specs=[HBM_SPEC] * nt,
        scratch_shapes=[pltpu.SemaphoreType.DMA((nt,))],
        compiler_params=pltpu.CompilerParams(vmem_limit_bytes=V7X_VMEM_LIMIT),
    )(*shards)


def _small_gather_start(v, me, after, name):
    land = lax.dynamic_update_slice(lax.empty((NDEV,) + v.shape, v.dtype), v[None], (me, 0, 0))

    def body(*refs):
        src, lnd = refs[0], refs[1]
        send_sems, recv_sems, token = refs[3], refs[4], refs[-1]
        x, y, c = _my_place()
        for k in range(1, NDEV):
            peer = (_flip(x, k & 4), _flip(y, k & 2), _flip(c, k & 1))
            pltpu.make_async_remote_copy(src_ref=src, dst_ref=lnd.at[4 * x + 2 * y + c], send_sem=send_sems.at[k - 1],
                                         recv_sem=recv_sems.at[k - 1], device_id=peer, device_id_type=MESH).start()
        token[...] = jnp.zeros_like(token)

    return _split_start(body, [v], [land], after, NDEV - 1, name)


def _small_gather_wait(send_sems, recv_sems, srcs, lands, after, name):
    def body(*refs):
        src, lnd, s_sems, r_sems = refs[0], refs[1], refs[2], refs[3]
        x, y, c = _my_place()
        for k in range(1, NDEV):
            sender = 4 * _flip(x, k & 4) + 2 * _flip(y, k & 2) + _flip(c, k & 1)
            cp = pltpu.make_async_remote_copy(src_ref=src, dst_ref=lnd.at[sender], send_sem=s_sems.at[k - 1],
                                              recv_sem=r_sems.at[k - 1], device_id=(x, y, c), device_id_type=MESH)
            cp.wait_send()
            cp.wait_recv()

    return _split_wait(body, send_sems, recv_sems, srcs, lands, after, name)[1][0]


def _gather_start(shards, axes, me, after, name):
    nt = len(shards)
    fulls = _place_own(shards, axes, name + "_own")

    def body(*refs):
        ins, outs = refs[:nt], refs[nt:2 * nt]
        send_sems, recv_sems, token = refs[2 * nt + 1], refs[2 * nt + 2], refs[-1]
        x, y, c = _my_place()
        dev = 4 * x + 2 * y + c
        peers = [(x, y, 1 - c), (1 - x, y, c), (x, 1 - y, c), (1 - x, 1 - y, c)]
        for t in range(nt):
            dst = _slab(outs[t], axes[t], ins[t].shape[axes[t]], dev)
            for k, peer in enumerate(peers):
                pltpu.make_async_remote_copy(src_ref=ins[t], dst_ref=dst, send_sem=send_sems.at[4 * t + k],
                                             recv_sem=recv_sems.at[4 * t + k], device_id=peer, device_id_type=MESH).start()
        token[...] = jnp.zeros_like(token)

    return _split_start(body, shards, fulls, after, 4, name)


def _gather_wait(send_sems, recv_sems, shards, fulls, axes, after, name):
    nt = len(shards)

    def body(*refs):
        ins, outs = refs[:nt], refs[nt:2 * nt]
        s_sems, r_sems = refs[2 * nt], refs[2 * nt + 1]
        x, y, c = _my_place()
        senders = [4 * x + 2 * y + (1 - c), 4 * (1 - x) + 2 * y + c, 4 * x + 2 * (1 - y) + c, 4 * (1 - x) + 2 * (1 - y) + c]
        for t in range(nt):
            for k, src_dev in enumerate(senders):
                cp = pltpu.make_async_remote_copy(
                    src_ref=ins[t], dst_ref=_slab(outs[t], axes[t], ins[t].shape[axes[t]], src_dev),
                    send_sem=s_sems.at[4 * t + k], recv_sem=r_sems.at[4 * t + k], device_id=(x, y, c), device_id_type=MESH)
                cp.wait_send()
                cp.wait_recv()

    return _split_wait(body, send_sems, recv_sems, shards, fulls, after, name)[1]


def _gather_pass_on(fulls, axes, name):
    nt = len(fulls)

    def body(*refs):
        outs = refs[nt:2 * nt]
        send_sems, recv_sems = refs[2 * nt:]
        x, y, c = _my_place()
        chips = [(1 - x, y), (x, 1 - y), (1 - x, 1 - y)]

        def copy(t, j, pc):
            cx, cy = chips[j]
            blk = _slab(outs[t], axes[t], outs[t].shape[axes[t]] // NDEV, 4 * cx + 2 * cy + pc)
            return pltpu.make_async_remote_copy(src_ref=blk, dst_ref=blk, send_sem=send_sems.at[t, j],
                                                recv_sem=recv_sems.at[t, j], device_id=(x, y, 1 - c), device_id_type=MESH)

        sends = [copy(t, j, c) for t in range(nt) for j in range(3)]
        for cp in sends:
            cp.start()
        for t in range(nt):
            for j in range(3):
                copy(t, j, 1 - c).wait_recv()
        for cp in sends:
            cp.wait_send()

    return pl.pallas_call(
        body, name=name, out_shape=[SDS(f.shape, f.dtype) for f in fulls], in_specs=[HBM_SPEC] * nt,
        out_specs=[HBM_SPEC] * nt, input_output_aliases={t: t for t in range(nt)},
        scratch_shapes=[pltpu.SemaphoreType.DMA((nt, 3)), pltpu.SemaphoreType.DMA((nt, 3))],
    )(*fulls)


def _row_tile(R, C, itemsize=4, target=1 << 20):
    best = R
    for tr in range(8, R, 8):
        if R % tr == 0 and tr * C * itemsize <= target:
            best = tr
    return best if best * C * itemsize <= target or best == R else R


def _as2d(a):
    return a.reshape(-1, a.shape[-1])


def _add_pairs(grad, land, c_me, name):
    blk = grad.shape[1:]
    NCH = NDEV // 2
    C = blk[-1]
    R = math.prod(blk[:-1])
    tr = _row_tile(R, C, 2)

    def body(c_ref, a_ref, b_ref, o_ref):
        o_ref[...] = (a_ref[...].astype(F32) + b_ref[...].astype(F32)).astype(o_ref.dtype)

    out = pl.pallas_call(
        body, name=name, out_shape=SDS((NCH, R, C), grad.dtype),
        grid_spec=pltpu.PrefetchScalarGridSpec(
            num_scalar_prefetch=1, grid=(NCH, R // tr),
            in_specs=[pl.BlockSpec((None, None, tr, C), lambda k, i, c: (k, c[0], i, 0)),
                      pl.BlockSpec((None, tr, C), lambda k, i, c: (k, i, 0))],
            out_specs=pl.BlockSpec((None, tr, C), lambda k, i, c: (k, i, 0))),
        compiler_params=_params(("parallel", "parallel")),
    )(c_me.reshape(1).astype(jnp.int32), grad.reshape(NCH, 2, R, C), land.reshape(NCH, R, C))
    return out.reshape((NCH,) + blk)


def _adamw_math(g, w, m, v):
    m2 = ADAM_B1 * m + (1.0 - ADAM_B1) * g
    v2 = ADAM_B2 * v + (1.0 - ADAM_B2) * (g * g)
    m_hat = m2 / (1.0 - ADAM_B1 ** ADAM_STEP)
    v_hat = v2 / (1.0 - ADAM_B2 ** ADAM_STEP)
    delta = -ADAM_LR * (m_hat / (jnp.sqrt(v_hat) + ADAM_EPS) + ADAM_WD * w)
    return delta, m2, v2


def _adamw(parts, w, m, v, name):
    shp = w.shape
    w2, m2, v2 = _as2d(w), _as2d(m), _as2d(v)
    R, C = w2.shape
    tr = _row_tile(R, C)
    p3, specs, picks = [], [], []
    for p in parts:
        if isinstance(p, tuple):
            specs.append(pl.BlockSpec((1, tr, C), lambda i, s, k=len(picks): (s[k], i, 0)))
            picks.append(p[1])
            p = p[0]
        else:
            specs.append(pl.BlockSpec((p.size // (R * C), tr, C), lambda i, s: (0, i, 0)))
        p3.append(p.reshape((-1, R, C)))
    npart = len(p3)
    picks = jnp.stack(picks).astype(jnp.int32) if picks else jnp.zeros((1,), jnp.int32)
    rows = pl.BlockSpec((tr, C), lambda i, s: (i, 0))

    def body(s_ref, *refs):
        prefs = refs[:npart]
        w_ref, m_ref, v_ref, g_ref, d_ref, nm_ref, nv_ref = refs[npart:]
        g = None
        for pr in prefs:
            for k in range(pr.shape[0]):
                term = pr[k].astype(F32)
                g = term if g is None else g + term
        g_ref[...] = g
        d_ref[...], nm_ref[...], nv_ref[...] = _adamw_math(g, w_ref[...], m_ref[...], v_ref[...])

    outs = pl.pallas_call(
        body, name=name, out_shape=[SDS((R, C), F32)] * 4,
        grid_spec=pltpu.PrefetchScalarGridSpec(num_scalar_prefetch=1, grid=(R // tr,), in_specs=specs + [rows] * 3,
                                               out_specs=[rows] * 4),
        compiler_params=_params(("parallel",)),
    )(picks, *p3, w2, m2, v2)
    return [o.reshape(shp) for o in outs]


def _ada_fwd(c_all, ada_w, ada_b_mine, name):
    L, D, Wc = ada_w.shape

    def body(c_ref, w_ref, b_ref, o_ref):
        cv = c_ref[...]
        act = cv * (1.0 / (1.0 + jnp.exp(-cv)))
        o_ref[0] = jnp.dot(act, w_ref[0], preferred_element_type=F32, precision=lax.Precision.HIGHEST) + b_ref[0]

    return pl.pallas_call(
        body, name=name, grid=(L,),
        in_specs=[_resident(c_all.shape), pl.BlockSpec((1, D, Wc), lambda l: (l, 0, 0)), pl.BlockSpec((1, 1, Wc), lambda l: (l, 0, 0))],
        out_specs=pl.BlockSpec((1, NDEV, Wc), lambda l: (l, 0, 0)), out_shape=SDS((L, NDEV, Wc), F32),
        compiler_params=_params(("parallel",)),
    )(c_all, ada_w, ada_b_mine.reshape(L, 1, Wc))


def _ada_bwd(c_all, dmod_mine, name):
    L, _, Wc = dmod_mine.shape
    D = c_all.shape[1]

    def body(c_ref, d_ref, o_ref):
        cv = c_ref[...]
        act = cv * (1.0 / (1.0 + jnp.exp(-cv)))
        o_ref[0] = lax.dot_general(act, d_ref[0], (((0,), (0,)), ((), ())), preferred_element_type=F32,
                                   precision=lax.Precision.HIGHEST)

    return pl.pallas_call(
        body, name=name, grid=(L,),
        in_specs=[_resident(c_all.shape), pl.BlockSpec((1, NDEV, Wc), lambda l: (l, 0, 0))],
        out_specs=pl.BlockSpec((1, D, Wc), lambda l: (l, 0, 0)), out_shape=SDS((L, D, Wc), F32),
        compiler_params=_params(("parallel",)),
    )(c_all, dmod_mine)


WEIGHT_NAMES = ['ada_w', 'ada_b', 'norm_mix_g', 'norm_mlp_g', 'pool_w', 'pool_scale', 'sgu_w_in', 'sgu_ln_g', 'sgu_ln_b',
                'sgu_w_s', 'sgu_b_s', 'sgu_w_out', 'mla_w_dq_dkv', 'mla_q_norm_g', 'mla_kv_norm_g', 'mla_w_uq', 'mla_w_ukv',
                'mla_w_o', 'mlp_w1', 'mlp_w2', 'final_g']
REPLICATED_EARLY = ['sgu_ln_g', 'sgu_ln_b', 'sgu_w_s', 'sgu_b_s', 'mla_kv_norm_g']
REPLICATED = ['ada_b', 'norm_mix_g', 'norm_mlp_g', 'final_g']
PACK_ROWS = 64
Q_HEAD = MLA_NOPE + MLA_ROPE


def _layer_matrices(i):
    kind, j = i % 3, i // 3
    if kind == 0:
        mats = [("pool_w", j, 1)]
    elif kind == 1:
        mats = [("sgu_w_in", j, 1), ("sgu_w_out", j, 0)]
    else:
        mats = [("mla_w_dq_dkv", j, 0), ("mla_w_uq", j, 1), ("mla_w_ukv", j, 1), ("mla_w_o", j, 0)]
    return mats + [("mlp_w1", i, 1), ("mlp_w2", i, 0)]


def _pack(arrays):
    flat = jnp.concatenate([a.reshape(-1).astype(F32) for a in arrays])
    rows = -(-flat.size // (LANES * PACK_ROWS)) * PACK_ROWS
    return jnp.pad(flat, (0, rows * LANES - flat.size)).reshape(rows, LANES)


def kernel(x, c, positions, ada_w, ada_b, norm_mix_g, norm_mlp_g, pool_w, pool_scale, sgu_w_in, sgu_ln_g, sgu_ln_b, sgu_w_s, sgu_b_s, sgu_w_out, mla_w_dq_dkv, mla_q_norm_g, mla_kv_norm_g, mla_w_uq, mla_w_ukv, mla_w_o, mlp_w1, mlp_w2, final_g, loss_target, m_ada_w, m_ada_b, m_norm_mix_g, m_norm_mlp_g, m_pool_w, m_pool_scale, m_sgu_w_in, m_sgu_ln_g, m_sgu_ln_b, m_sgu_w_s, m_sgu_b_s, m_sgu_w_out, m_mla_w_dq_dkv, m_mla_q_norm_g, m_mla_kv_norm_g, m_mla_w_uq, m_mla_w_ukv, m_mla_w_o, m_mlp_w1, m_mlp_w2, m_final_g, v_ada_w, v_ada_b, v_norm_mix_g, v_norm_mlp_g, v_pool_w, v_pool_scale, v_sgu_w_in, v_sgu_ln_g, v_sgu_ln_b, v_sgu_w_s, v_sgu_b_s, v_sgu_w_out, v_mla_w_dq_dkv, v_mla_q_norm_g, v_mla_kv_norm_g, v_mla_w_uq, v_mla_w_ukv, v_mla_w_o, v_mlp_w1, v_mlp_w2, v_final_g):
    a = dict(locals())
    S, D = x.shape[1], x.shape[2]
    L = ada_w.shape[0]
    Wc = ada_w.shape[2]
    me = 4 * lax.axis_index("x") + 2 * lax.axis_index("y") + lax.axis_index("c")
    my_chip = 2 * lax.axis_index("x") + lax.axis_index("y")

    v0 = _pack([c, pool_scale, mla_q_norm_g])
    g0 = _small_all_gather(v0, "gather_c").reshape(NDEV, -1)
    n_ps, n_qg = pool_scale.size, mla_q_norm_g.size
    c_all = g0[:, :D]
    ps_w = pool_scale.shape[1]
    ps_full = g0[:, D:D + n_ps].reshape(NDEV, -1, ps_w).transpose(1, 0, 2).reshape(-1, 1, D)
    qg_full = g0[:, D + n_ps:D + n_ps + n_qg].reshape(1, -1)

    ada_b_mine = lax.dynamic_slice_in_dim(ada_b, me * Wc, Wc, axis=1)
    modp = _ada_fwd(c_all, ada_w, ada_b_mine, "ada_fwd")
    ga = _small_all_gather(modp.reshape(-1, LANES), "gather_mod").reshape(NDEV, L, NDEV, Wc)
    mod = lax.dynamic_index_in_dim(ga, me, axis=2, keepdims=False).transpose(1, 0, 2).reshape(L, 6, D)
    mod8 = jnp.pad(mod, ((0, 0), (0, 2), (0, 0)))

    small = {"pool_scale": ps_full, "sgu_ln_g": sgu_ln_g, "sgu_ln_b": sgu_ln_b, "sgu_w_s": sgu_w_s[0],
             "sgu_b_s_t": sgu_b_s[0].T, "mla_q_norm_g": qg_full, "mla_kv_norm_g": mla_kv_norm_g}
    gmix, gmlp = norm_mix_g.reshape(L, 1, D), norm_mlp_g.reshape(L, 1, D)
    tables = _rope_tables(positions, S)

    def shards_of(mats):
        return [a[n][j].astype(BF16) for n, j, _ in mats], [ax for _, _, ax in mats]

    def as_weights(mats, fulls):
        w = {n: f for (n, _, _), f in zip(mats, fulls)}
        if "mla_w_uq" in w:
            lat_w = w["mla_w_dq_dkv"].shape[1]
            w["mla_wd"] = jnp.pad(w.pop("mla_w_dq_dkv"), ((0, 0), (0, -lat_w % LANES)))
            w["mla_wq"] = _pad_heads(w.pop("mla_w_uq"), Q_HEAD)
        return w

    xc = x[0]
    wts, saved, flying = [], [], None
    for i in range(L):
        mats = _layer_matrices(i)
        mod_i = mod8[i]
        if flying is None:
            now, late = mats[:-1], mats[-1:]
            shards, axes = shards_of(now)
            fulls = _all_gather_group(shards, axes, mod8, f"gather_w_{i}")
            w = as_weights(now, fulls)
            late_axes = shards_of(late)[1]
            late_fly = _gather_start(*shards_of(late), me, fulls[0], f"gather_start_{i}b")
            mod_i = mod_i + late_fly[4][0, 0]
            w[late[0][0]] = lambda after, fly=late_fly, axes=late_axes, i=i: _gather_pass_on(
                _gather_wait(*fly[:4], axes, after, f"gather_wait_{i}b"), axes, f"gather_pass_{i}b")[0]
            order = late_fly[4]
        else:
            axes = shards_of(mats)[1]
            fulls = _gather_wait(*flying[:4], axes, xc, f"gather_wait_{i}")
            fulls = _gather_pass_on(fulls, axes, f"gather_pass_{i}")
            w = as_weights(mats, fulls)
            order = fulls[0]
        wts.append(w)
        if i + 1 < L:
            flying = _gather_start(*shards_of(_layer_matrices(i + 1)), me, order, f"gather_start_{i + 1}")
            mod_i = mod_i + flying[4][0, 0]
        xc, sv = _layer_forward(i, xc, mod_i, gmix[i], gmlp[i], wts[i], small, tables)
        saved.append(sv)
    dx, loss_acc = _loss_head(xc, loss_target[0], final_g.reshape(1, D), _tiles(S)[0], "loss_head")

    res = {n: [None] * a[n].shape[0] for n in WEIGHT_NAMES if a[n].ndim > 1}
    c_me = lax.axis_index("c")

    def start_reduce(i, mats, g, after, tag=""):
        g = dict(g)
        if "mla_wq" in g:
            g["mla_w_dq_dkv"] = g.pop("mla_wd")[:, :mla_w_dq_dkv.shape[2]]
            g["mla_w_uq"] = _unpad_heads(g.pop("mla_wq"), Q_HEAD)
        gl = []
        for n, j, ax in mats:
            gm, blk = g[n].astype(BF16), a[n][j].shape
            if gm.shape != (NDEV,) + blk:
                gm = jnp.moveaxis(gm.reshape(blk[:ax] + (NDEV,) + blk[ax:]), ax, 0)
            gl.append(gm)
        lands = _reduce_scatter_sibling(gl, f"rs_sibling_{i}{tag}")
        parts = [_add_pairs(gm, l, c_me, f"rs_add_{i}_{n}") for gm, l, (n, _, _) in zip(gl, lands, mats)]
        return _chips_exchange_start(parts, after, f"rs_chips_start_{i}{tag}") + (mats, f"{i}{tag}")

    def finish_reduce(fly, after):
        parts, recv = _chips_exchange_wait(*fly[:4], after, f"rs_chips_wait_{fly[6]}")
        for (n, j, _), p, r in zip(fly[5], parts, recv):
            res[n][j] = _adamw([(p, my_chip), r], a[n][j], a["m_" + n][j], a["v_" + n][j], f"adamw_{n}_{j}")

    def update_replicated(gathered, names, tail, label):
        zeros_tail = [jnp.zeros_like(t) for t in tail]
        packs = _adamw([gathered], _pack([a[n] for n in names] + zeros_tail), _pack([a["m_" + n] for n in names] + zeros_tail),
                       _pack([a["v_" + n] for n in names] + zeros_tail), label)
        flat = [t.reshape(-1) for t in packs]
        off = 0
        for n in names:
            res[n] = [f[off:off + a[n].size].reshape(a[n].shape) for f in flat]
            off += a[n].size
        sums = []
        for t in tail:
            sums.append(flat[0][off:off + t.size].reshape(t.shape))
            off += t.size
        return sums, packs[0]

    stats, sgrads, flying, early, last_mlp = [None] * L, {}, None, None, []
    for i in reversed(range(L)):
        mats = _layer_matrices(i)
        mod_i = mod8[i]
        if flying is not None:
            mod_i = mod_i + flying[4][0, 0]
        if early is not None:
            mod_i = mod_i + early[4][0, 0]
        hook = None
        if i == 0:
            def hook(g_mlp, mats=mats, dx=dx):
                last_mlp.append(start_reduce(0, mats[-2:], g_mlp, dx, "a"))
                return last_mlp[0][4][0, 0]
            mats = mats[:-2]
        dx, stats[i], g, sgr = _layer_backward(i, dx, saved[i], mod_i, gmix[i], gmlp[i], wts[i], small, tables, hook)
        sgrads.update(sgr)
        if flying is not None:
            finish_reduce(flying, dx)
        flying = start_reduce(i, mats, g, dx)
        if i == 1:
            early_tail = [sgrads["mla_q_norm_g"]]
            early = _small_gather_start(_pack([sgrads[n] for n in REPLICATED_EARLY] + early_tail), me, flying[4],
                                        "gather_small_start")
    (g_qg,), _ = update_replicated(_small_gather_wait(*early[:4], dx, "gather_small_wait"), REPLICATED_EARLY, early_tail,
                                   "adamw_replicated_early")

    sg = {"ada_b": jnp.stack([s[0:6] for s in stats]), "norm_mix_g": jnp.stack([s[6] for s in stats]),
          "norm_mlp_g": jnp.stack([s[7] for s in stats]), "final_g": loss_acc[0]}
    ps_grad = jnp.concatenate([sgrads[f"pool_scale_{j}"] for j in range(pool_scale.shape[0])])
    tail = [ps_grad, loss_acc[2, :LANES]]
    packed = _pack([sg[n] for n in REPLICATED] + tail) + flying[4][0, 0]
    gathered = _small_all_gather(packed, "gather_small")
    (g_ps, loss_lanes), g_p = update_replicated(gathered, REPLICATED, tail, "adamw_replicated")
    finish_reduce(last_mlp[0], g_p)
    finish_reduce(flying, g_p)
    loss = loss_lanes[0]
    res["pool_scale"] = _adamw([lax.dynamic_slice_in_dim(g_ps, me * ps_w, ps_w, axis=1)], pool_scale, m_pool_scale,
                               v_pool_scale, "adamw_pool_scale")
    qg_w = mla_q_norm_g.shape[1]
    res["mla_q_norm_g"] = _adamw([lax.dynamic_slice_in_dim(g_qg, me * qg_w, qg_w, axis=1)], mla_q_norm_g, m_mla_q_norm_g,
                                 v_mla_q_norm_g, "adamw_q_norm_g")

    n_mod = L * 6 * D
    dmod_all = gathered.reshape(NDEV, -1)[:, :n_mod].reshape(NDEV, L, 6 * D)
    dmod_mine = lax.dynamic_slice_in_dim(dmod_all, me * Wc, Wc, axis=2).transpose(1, 0, 2)
    res["ada_w"] = _adamw([_ada_bwd(c_all, dmod_mine, "ada_bwd")], ada_w, m_ada_w, v_ada_w, "adamw_ada_w")

    outs = []
    for k in range(4):
        for n in WEIGHT_NAMES:
            r = res[n]
            outs.append(jnp.stack([lay[k] for lay in r]) if isinstance(r[0], list) else r[k])
    return (loss, dx.reshape(x.shape), *outs)
```

```python
import functools
import math

import jax
import jax.numpy as jnp
import numpy as np
from jax import lax
from jax.experimental import pallas as pl
from jax.experimental.pallas import tpu as pltpu

F32 = jnp.float32
BF16 = jnp.bfloat16
SDS = jax.ShapeDtypeStruct
MESH = pl.DeviceIdType.MESH

NDEV = 8
V7X_VMEM_LIMIT = 56 << 20
LANES = 128
RMS_EPS = 1e-6
LN_EPS = 1e-5
POOL_WINDOWS = (2, 4, 8, 16)
HALO = 16
SGU_CHUNK = 128
SGU_HEAD = 128
MLA_NOPE, MLA_ROPE, MLA_V = 128, 64, 128
MLA_Q_LORA, MLA_KV_LORA = 256, 128
MLA_HEAD_PAD = 256
ROPE_THETA = 10000.0
SM_SCALE = (MLA_NOPE + MLA_ROPE) ** -0.5
NEG = -1e30
ADAM_LR, ADAM_B1, ADAM_B2, ADAM_EPS, ADAM_WD, ADAM_STEP = 0.001, 0.9, 0.999, 1e-08, 0.01, 10
INV_SQRT2 = 1.0 / math.sqrt(2.0)
INV_SQRT_2PI = 1.0 / math.sqrt(2.0 * math.pi)
SH1, SC1, G1, SH2, SC2, G2 = 0, 1, 2, 3, 4, 5


def _params(sem=None, vmem=V7X_VMEM_LIMIT):
    return pltpu.CompilerParams(dimension_semantics=sem, vmem_limit_bytes=vmem)


def _resident(shape):
    nd = len(shape)
    return pl.BlockSpec(shape, lambda *_: (0,) * nd, pipeline_mode=pl.Buffered(1))


def _rows1(tm, w):
    return pl.BlockSpec((tm, w), lambda i: (i, 0))


def _rms(x):
    r = lax.rsqrt(jnp.mean(x * x, axis=-1, keepdims=True) + RMS_EPS)
    return x * r, r


def _colsum(v):
    return jnp.sum(v, axis=0, keepdims=True)


def _normmod_bwd(dh, n, r, a):
    dn = dh * a
    return r * (dn - n * jnp.mean(dn * n, axis=-1, keepdims=True))


def _dot(a, b):
    return jnp.dot(a, b, preferred_element_type=F32)


def _dot_nt(a, b):
    return lax.dot_general(a, b, (((1,), (1,)), ((), ())), preferred_element_type=F32)


def _dot_tn(a, b):
    return lax.dot_general(a, b, (((0,), (0,)), ((), ())), preferred_element_type=F32)


def _gelu(x):
    return 0.5 * x * (1.0 + lax.erf(x * INV_SQRT2))


def _gelu_grad(x):
    return 0.5 * (1.0 + lax.erf(x * INV_SQRT2)) + x * jnp.exp(-0.5 * x * x) * INV_SQRT_2PI


def _swap_halves(v):
    lane = lax.broadcasted_iota(jnp.int32, v.shape, 1)
    half = MLA_ROPE // 2
    return jnp.where(lane < half, pltpu.roll(v, LANES - half, 1),
                     jnp.where(lane < MLA_ROPE, pltpu.roll(v, half, 1), 0.0))


def _mlp_up(x1, mod, gn, w1, tm, tn, name):
    S, D = x1.shape
    Fh = w1.shape[1]

    def body(x_ref, mod_ref, gn_ref, w_ref, h_ref, r_ref):
        n, _ = _rms(x_ref[...])
        a = gn_ref[...] * (1.0 + mod_ref[SC2:SC2 + 1, :])
        h = (n * a + mod_ref[SH2:SH2 + 1, :]).astype(BF16)
        h_ref[...] = h
        for j in range(Fh // tn):
            cols = slice(j * tn, (j + 1) * tn)
            r_ref[:, cols] = jnp.maximum(_dot(h, w_ref[:, cols]), 0.0).astype(BF16)

    return pl.pallas_call(
        body, name=name, grid=(S // tm,),
        in_specs=[_rows1(tm, D), _resident(mod.shape), _resident(gn.shape), _resident(w1.shape)],
        out_specs=[_rows1(tm, D), _rows1(tm, Fh)],
        out_shape=[SDS((S, D), BF16), SDS((S, Fh), BF16)],
        compiler_params=_params(("parallel",)),
    )(x1, mod, gn, w1)


def _mlp_down(r, w2, x1, mod, tm, name):
    S, Fh = r.shape
    D = w2.shape[1]

    def body(r_ref, w_ref, x_ref, mod_ref, x2_ref, o_ref):
        rv = r_ref[...]
        o = _dot(rv * rv, w_ref[...])
        o_ref[...] = o.astype(BF16)
        x2_ref[...] = x_ref[...] + mod_ref[G2:G2 + 1, :] * o

    return pl.pallas_call(
        body, name=name, grid=(S // tm,),
        in_specs=[_rows1(tm, Fh), _resident(w2.shape), _rows1(tm, D), _resident(mod.shape)],
        out_specs=[_rows1(tm, D), _rows1(tm, D)],
        out_shape=[SDS((S, D), F32), SDS((S, D), BF16)],
        compiler_params=_params(("parallel",)),
    )(r, w2, x1, mod)


def _mlp_bwd_a(dx2, o, mod, w2, r, tm, tn, name):
    S, D = dx2.shape
    Fh = r.shape[1]

    def body(dx_ref, o_ref, mod_ref, w_ref, r_ref, da_ref, do_ref, st_ref):
        @pl.when(pl.program_id(0) == 0)
        def _():
            st_ref[...] = jnp.zeros_like(st_ref)

        dx = dx_ref[...]
        d_o = (dx * mod_ref[G2:G2 + 1, :]).astype(BF16)
        do_ref[...] = d_o
        st_ref[G2:G2 + 1, :] += _colsum(dx * o_ref[...].astype(F32))
        for j in range(Fh // tn):
            cols = slice(j * tn, (j + 1) * tn)
            dz = _dot_nt(d_o, w_ref[cols, :])
            da_ref[:, cols] = (dz * (2.0 * r_ref[:, cols].astype(F32))).astype(BF16)

    return pl.pallas_call(
        body, name=name, grid=(S // tm,),
        in_specs=[_rows1(tm, D), _rows1(tm, D), _resident(mod.shape), _resident(w2.shape), _rows1(tm, Fh)],
        out_specs=[_rows1(tm, Fh), _rows1(tm, D), pl.BlockSpec((8, D), lambda i: (0, 0))],
        out_shape=[SDS((S, Fh), BF16), SDS((S, D), BF16), SDS((8, D), F32)],
        compiler_params=_params(("arbitrary",)),
    )(dx2, o, mod, w2, r)


def _mlp_bwd_b(d_a, w1, x1, dx2, mod, gn, tm, name):
    S, Fh = d_a.shape
    D = w1.shape[0]

    def body(da_ref, w_ref, x_ref, dx_ref, mod_ref, gn_ref, dx1_ref, st_ref):
        @pl.when(pl.program_id(0) == 0)
        def _():
            st_ref[...] = jnp.zeros_like(st_ref)

        dh = _dot_nt(da_ref[...], w_ref[...])
        n, rr = _rms(x_ref[...])
        gn_v = gn_ref[...]
        sc1p = 1.0 + mod_ref[SC2:SC2 + 1, :]
        t = _colsum(dh * n)
        st_ref[SH2:SH2 + 1, :] += _colsum(dh)
        st_ref[SC2:SC2 + 1, :] += t * gn_v
        st_ref[6:7, :] += t * sc1p
        dx1_ref[...] = dx_ref[...] + _normmod_bwd(dh, n, rr, gn_v * sc1p)

    return pl.pallas_call(
        body, name=name, grid=(S // tm,),
        in_specs=[_rows1(tm, Fh), _resident(w1.shape), _rows1(tm, D), _rows1(tm, D), _resident(mod.shape),
                  _resident(gn.shape)],
        out_specs=[_rows1(tm, D), pl.BlockSpec((8, D), lambda i: (0, 0))],
        out_shape=[SDS((S, D), F32), SDS((8, D), F32)],
        compiler_params=_params(("arbitrary",)),
    )(d_a, w1, x1, dx2, mod, gn)


def _mm_tn(a, g, tk, tn, name, square_a=False, col_shards=False):
    S, K1 = a.shape
    N = g.shape[1]
    w = N // NDEV
    per = tn // w if col_shards else 1

    def body(a_ref, g_ref, o_ref):
        av = a_ref[...]
        if square_a:
            av = av * av
        res = _dot_tn(av, g_ref[...]).astype(BF16)
        if col_shards:
            for s in range(per):
                o_ref[s] = res[:, s * w:(s + 1) * w]
        else:
            o_ref[...] = res

    if col_shards:
        out_spec, out_shape = pl.BlockSpec((per, tk, w), lambda i, j: (j, i, 0)), SDS((NDEV, K1, w), BF16)
    else:
        out_spec, out_shape = pl.BlockSpec((tk, tn), lambda i, j: (i, j)), SDS((K1, N), BF16)
    return pl.pallas_call(
        body, name=name, grid=(K1 // tk, N // tn),
        in_specs=[pl.BlockSpec((S, tk), lambda i, j: (0, i)), pl.BlockSpec((S, tn), lambda i, j: (0, j))],
        out_specs=out_spec, out_shape=out_shape,
        compiler_params=_params(("parallel", "parallel")),
    )(a, g)


def _pool_h_ext(x_ref, xp_ref, mod_ref, gn_ref, i, tm):
    ext = jnp.concatenate([xp_ref[...], x_ref[...]], axis=0)
    n, r = _rms(ext)
    a = gn_ref[...] * (1.0 + mod_ref[SC1:SC1 + 1, :])
    h = n * a + mod_ref[SH1:SH1 + 1, :]
    row = lax.broadcasted_iota(jnp.int32, (tm + HALO, 1), 0)
    h = jnp.where(jnp.logical_and(i == 0, row < HALO), 0.0, h)
    return h, n[HALO:], r[HALO:], a


def _trailing_sum(v, win):
    k = 1
    while k < win:
        v = v + pltpu.roll(v, k, 0)
        k *= 2
    return v


def _leading_sum(v, win):
    k = 1
    while k < win:
        v = v + pltpu.roll(v, v.shape[0] - k, 0)
        k *= 2
    return v


def _pool_fwd(x, mod, gn, pw, ps, tm, name):
    S, D = x.shape
    C = D // len(POOL_WINDOWS)
    hb = tm // HALO

    def body(x_ref, xp_ref, mod_ref, gn_ref, pw_ref, ps_ref, x1_ref):
        i = pl.program_id(0)
        h, _, _, _ = _pool_h_ext(x_ref, xp_ref, mod_ref, gn_ref, i, tm)
        t1 = (i * tm + lax.broadcasted_iota(jnp.int32, (tm, 1), 0)).astype(F32) + 1.0
        for g, win in enumerate(POOL_WINDOWS):
            cols = slice(g * C, (g + 1) * C)
            hg = h[:, cols]
            inv = 1.0 / jnp.minimum(t1, float(win))
            pooled = (_trailing_sum(hg, win)[HALO:] * inv - hg[HALO:]).astype(BF16)
            y = _dot(pooled, pw_ref[g]) * ps_ref[:, cols]
            x1_ref[:, cols] = x_ref[:, cols] + mod_ref[G1:G1 + 1, cols] * y

    return pl.pallas_call(
        body, name=name, grid=(S // tm,),
        in_specs=[_rows1(tm, D), pl.BlockSpec((HALO, D), lambda i: (jnp.maximum(i * hb - 1, 0), 0)),
                  _resident(mod.shape), _resident(gn.shape), _resident(pw.shape), _resident(ps.shape)],
        out_specs=_rows1(tm, D),
        out_shape=SDS((S, D), F32),
        compiler_params=_params(("parallel",)),
    )(x, x, mod, gn, pw, ps)


def _pool_bwd(x, dx1, mod, gn, pw, ps, tm, name):
    S, D = x.shape
    G = len(POOL_WINDOWS)
    C = D // G
    hb = tm // HALO
    nt = S // tm

    def body(x_ref, xp_ref, d1_ref, dn_ref, mod_ref, gn_ref, pw_ref, ps_ref, dx_ref, st_ref, dpw_ref):
        i = pl.program_id(0)

        @pl.when(i == 0)
        def _():
            st_ref[...] = jnp.zeros_like(st_ref)
            dpw_ref[...] = jnp.zeros_like(dpw_ref)

        h, n, rr, a = _pool_h_ext(x_ref, xp_ref, mod_ref, gn_ref, i, tm)
        g1 = mod_ref[G1:G1 + 1, :]
        ps_v = ps_ref[...]
        d1 = d1_ref[...]
        d1n = jnp.where(i == nt - 1, 0.0, dn_ref[...])
        dyr = (jnp.concatenate([d1, d1n], axis=0) * (g1 * ps_v)).astype(BF16)
        t1 = (i * tm + lax.broadcasted_iota(jnp.int32, (tm + HALO, 1), 0)).astype(F32) + 1.0
        parts = []
        for g, win in enumerate(POOL_WINDOWS):
            cols = slice(g * C, (g + 1) * C)
            hg = h[:, cols]
            inv = 1.0 / jnp.minimum(t1, float(win))
            pooled = (_trailing_sum(hg, win)[HALO:] * inv[:tm] - hg[HALO:]).astype(BF16)
            yraw = _dot(pooled, pw_ref[g])
            st_ref[G1:G1 + 1, cols] += _colsum(d1[:, cols] * (yraw * ps_v[:, cols]))
            st_ref[4:5, cols] += _colsum(d1[:, cols] * g1[:, cols] * yraw)
            dpw_ref[g] += _dot_tn(pooled, dyr[:tm, cols])
            dpool = _dot_nt(dyr[:, cols], pw_ref[g])
            parts.append(_leading_sum(dpool * inv, win)[:tm] - dpool[:tm])
        dh = jnp.concatenate(parts, axis=1)
        t = _colsum(dh * n)
        st_ref[SH1:SH1 + 1, :] += _colsum(dh)
        st_ref[SC1:SC1 + 1, :] += t * gn_ref[...]
        st_ref[3:4, :] += t * (1.0 + mod_ref[SC1:SC1 + 1, :])
        dx_ref[...] = d1 + _normmod_bwd(dh, n, rr, a)

    return pl.pallas_call(
        body, name=name, grid=(nt,),
        in_specs=[_rows1(tm, D), pl.BlockSpec((HALO, D), lambda i: (jnp.maximum(i * hb - 1, 0), 0)),
                  _rows1(tm, D), pl.BlockSpec((HALO, D), lambda i: (jnp.minimum((i + 1) * hb, S // HALO - 1), 0)),
                  _resident(mod.shape), _resident(gn.shape), _resident(pw.shape), _resident(ps.shape)],
        out_specs=[_rows1(tm, D), pl.BlockSpec((8, D), lambda i: (0, 0)), pl.BlockSpec((G, C, C), lambda i: (0, 0, 0))],
        out_shape=[SDS((S, D), F32), SDS((8, D), F32), SDS((G, C, C), F32)],
        compiler_params=_params(("arbitrary",)),
    )(x, x, dx1, dx1, mod, gn, pw, ps)


def _tril_bf16(w):
    row = lax.broadcasted_iota(jnp.int32, w.shape, 0)
    col = lax.broadcasted_iota(jnp.int32, w.shape, 1)
    return jnp.where(col <= row, w, 0.0).astype(BF16)


def _sgu_front(pre, lng_ref, lnb_ref, W):
    z = _gelu(pre)
    u, v = z[:, :W], z[:, W:]
    mu = jnp.mean(v, axis=-1, keepdims=True)
    xc = v - mu
    rstd = lax.rsqrt(jnp.mean(xc * xc, axis=-1, keepdims=True) + LN_EPS)
    vhat = xc * rstd
    return u, vhat, rstd, vhat * lng_ref[...] + lnb_ref[...]


def _sgu_mix(vn, ws_ref, bst_ref, mix_s, tm, W):
    for hd in range(W // SGU_HEAD):
        wm = _tril_bf16(ws_ref[hd])
        bcol = bst_ref[:, hd:hd + 1]
        for ci in range(tm // SGU_CHUNK):
            rs, cs = slice(ci * SGU_CHUNK, (ci + 1) * SGU_CHUNK), slice(hd * SGU_HEAD, (hd + 1) * SGU_HEAD)
            mix_s[rs, cs] = _dot(wm, vn[rs, cs].astype(BF16)) + bcol


def _sgu_fwd(x, mod, gn, w_in, lng, lnb, ws, bst, w_out, tm, name):
    S, D = x.shape
    W = w_out.shape[0]

    def body(x_ref, mod_ref, gn_ref, win_ref, lng_ref, lnb_ref, ws_ref, bst_ref, wout_ref,
             x1_ref, h_ref, pre_ref, y_ref, mix_s):
        n, _ = _rms(x_ref[...])
        a = gn_ref[...] * (1.0 + mod_ref[SC1:SC1 + 1, :])
        h = (n * a + mod_ref[SH1:SH1 + 1, :]).astype(BF16)
        h_ref[...] = h
        pre = _dot(h, win_ref[...])
        pre_ref[...] = pre.astype(BF16)
        u, _, _, vn = _sgu_front(pre, lng_ref, lnb_ref, W)
        _sgu_mix(vn, ws_ref, bst_ref, mix_s, tm, W)
        y = _dot((u * mix_s[...]).astype(BF16), wout_ref[...])
        y_ref[...] = y.astype(BF16)
        x1_ref[...] = x_ref[...] + mod_ref[G1:G1 + 1, :] * y

    return pl.pallas_call(
        body, name=name, grid=(S // tm,),
        in_specs=[_rows1(tm, D), _resident(mod.shape), _resident(gn.shape), _resident(w_in.shape),
                  _resident(lng.shape), _resident(lnb.shape), _resident(ws.shape), _resident(bst.shape),
                  _resident(w_out.shape)],
        out_specs=[_rows1(tm, D), _rows1(tm, D), _rows1(tm, 2 * W), _rows1(tm, D)],
        out_shape=[SDS((S, D), F32), SDS((S, D), BF16), SDS((S, 2 * W), BF16), SDS((S, D), BF16)],
        scratch_shapes=[pltpu.VMEM((tm, W), F32)],
        compiler_params=_params(("parallel",)),
    )(x, mod, gn, w_in, lng, lnb, ws, bst, w_out)


def _sgu_bwd(x, dx1, pre, y, mod, gn, w_in, lng, lnb, ws, bst, w_out, tm, name):
    S, D = x.shape
    W = w_out.shape[0]
    H = W // SGU_HEAD
    nt = S // tm

    def body(x_ref, d1_ref, pre_ref, y_ref, mod_ref, gn_ref, win_ref, lng_ref, lnb_ref, ws_ref, bst_ref, wout_ref,
             dx_ref, dy_ref, gt_ref, dpre_ref, st_ref, dws_ref, dbs_ref, mix_s, dvn_s):
        i = pl.program_id(0)

        @pl.when(i == 0)
        def _():
            st_ref[...] = jnp.zeros_like(st_ref)
            dws_ref[...] = jnp.zeros_like(dws_ref)
            dbs_ref[...] = jnp.zeros_like(dbs_ref)

        d1 = d1_ref[...]
        pre = pre_ref[...].astype(F32)
        u, vhat, rstd, vn = _sgu_front(pre, lng_ref, lnb_ref, W)
        _sgu_mix(vn, ws_ref, bst_ref, mix_s, tm, W)
        mixed = mix_s[...]
        gt_ref[...] = (u * mixed).astype(BF16)
        dyb = (d1 * mod_ref[G1:G1 + 1, :]).astype(BF16)
        dy_ref[...] = dyb
        st_ref[G1:G1 + 1, :] += _colsum(d1 * y_ref[...].astype(F32))
        dgt = _dot_nt(dyb, wout_ref[...])
        du = dgt * mixed
        dmix = dgt * u
        for hd in range(H):
            wm = _tril_bf16(ws_ref[hd])
            for ci in range(tm // SGU_CHUNK):
                rs, cs = slice(ci * SGU_CHUNK, (ci + 1) * SGU_CHUNK), slice(hd * SGU_HEAD, (hd + 1) * SGU_HEAD)
                dm = dmix[rs, cs]
                dmb = dm.astype(BF16)
                dbs_ref[hd] += jnp.broadcast_to(jnp.sum(dm, axis=1, keepdims=True), (SGU_CHUNK, LANES))
                dws_ref[hd] += _dot_nt(dmb, vn[rs, cs].astype(BF16))
                dvn_s[rs, cs] = _dot_tn(wm, dmb)
        dvn = dvn_s[...]
        st_ref[4:5, :] += _colsum(dvn * vhat)
        st_ref[5:6, :] += _colsum(dvn)
        dvh = dvn * lng_ref[...]
        dv = rstd * (dvh - jnp.mean(dvh, axis=-1, keepdims=True) - vhat * jnp.mean(dvh * vhat, axis=-1, keepdims=True))
        dpre_u = (du * _gelu_grad(pre[:, :W])).astype(BF16)
        dpre_v = (dv * _gelu_grad(pre[:, W:])).astype(BF16)
        dpre_ref[:, :W] = dpre_u
        dpre_ref[:, W:] = dpre_v
        dh = _dot_nt(dpre_u, win_ref[:, :W]) + _dot_nt(dpre_v, win_ref[:, W:])
        n, rr = _rms(x_ref[...])
        gn_v = gn_ref[...]
        sc1p = 1.0 + mod_ref[SC1:SC1 + 1, :]
        t = _colsum(dh * n)
        st_ref[SH1:SH1 + 1, :] += _colsum(dh)
        st_ref[SC1:SC1 + 1, :] += t * gn_v
        st_ref[3:4, :] += t * sc1p
        dx_ref[...] = d1 + _normmod_bwd(dh, n, rr, gn_v * sc1p)

        @pl.when(i == nt - 1)
        def _():
            for hd in range(H):
                row = lax.broadcasted_iota(jnp.int32, (SGU_CHUNK, SGU_CHUNK), 0)
                col = lax.broadcasted_iota(jnp.int32, (SGU_CHUNK, SGU_CHUNK), 1)
                dws_ref[hd] = jnp.where(col <= row, dws_ref[hd], 0.0)

    return pl.pallas_call(
        body, name=name, grid=(nt,),
        in_specs=[_rows1(tm, D), _rows1(tm, D), _rows1(tm, 2 * W), _rows1(tm, D), _resident(mod.shape),
                  _resident(gn.shape), _resident(w_in.shape), _resident(lng.shape), _resident(lnb.shape),
                  _resident(ws.shape), _resident(bst.shape), _resident(w_out.shape)],
        out_specs=[_rows1(tm, D), _rows1(tm, D), _rows1(tm, W), _rows1(tm, 2 * W),
                   pl.BlockSpec((8, D), lambda i: (0, 0)), pl.BlockSpec((H, SGU_CHUNK, SGU_CHUNK), lambda i: (0, 0, 0)),
                   pl.BlockSpec((H, SGU_CHUNK, LANES), lambda i: (0, 0, 0))],
        out_shape=[SDS((S, D), F32), SDS((S, D), BF16), SDS((S, W), BF16), SDS((S, 2 * W), BF16),
                   SDS((8, D), F32), SDS((H, SGU_CHUNK, SGU_CHUNK), F32), SDS((H, SGU_CHUNK, LANES), F32)],
        scratch_shapes=[pltpu.VMEM((tm, W), F32), pltpu.VMEM((tm, W), F32)],
        compiler_params=_params(("arbitrary",)),
    )(x, dx1, pre, y, mod, gn, w_in, lng, lnb, ws, bst, w_out)


def _mla_lat(x, mod, gn, wd, qg, kvg, cos_t, sin_t, tm, name):
    S, D = x.shape
    LW = wd.shape[1]
    QL, KL = MLA_Q_LORA, MLA_KV_LORA

    def body(x_ref, mod_ref, gn_ref, wd_ref, qg_ref, kvg_ref, c_ref, s_ref, h_ref, lat_ref, cq_ref, ckv_ref, kr_ref):
        n, _ = _rms(x_ref[...])
        a = gn_ref[...] * (1.0 + mod_ref[SC1:SC1 + 1, :])
        h = (n * a + mod_ref[SH1:SH1 + 1, :]).astype(BF16)
        h_ref[...] = h
        lat = _dot(h, wd_ref[...])
        lat_ref[...] = lat
        nq, _ = _rms(lat[:, :QL])
        cq_ref[...] = (nq * qg_ref[...]).astype(BF16)
        nkv, _ = _rms(lat[:, QL:QL + KL])
        ckv_ref[...] = (nkv * kvg_ref[...]).astype(BF16)
        kr = lat[:, QL + KL:]
        kr_ref[...] = (kr * c_ref[...] + _swap_halves(kr) * s_ref[...]).astype(BF16)

    return pl.pallas_call(
        body, name=name, grid=(S // tm,),
        in_specs=[_rows1(tm, D), _resident(mod.shape), _resident(gn.shape), _resident(wd.shape), _resident(qg.shape),
                  _resident(kvg.shape), _rows1(tm, LANES), _rows1(tm, LANES)],
        out_specs=[_rows1(tm, D), _rows1(tm, LW), _rows1(tm, QL), _rows1(tm, KL), _rows1(tm, LANES)],
        out_shape=[SDS((S, D), BF16), SDS((S, LW), F32), SDS((S, QL), BF16), SDS((S, KL), BF16), SDS((S, LANES), BF16)],
        compiler_params=_params(("parallel",)),
    )(x, mod, gn, wd, qg, kvg, cos_t, sin_t)


def _mla_qkv(cq, ckv, krp, wq, wukv, cos_t, sin_t, tm, name):
    S = cq.shape[0]
    H = wq.shape[1] // MLA_HEAD_PAD
    HP = MLA_HEAD_PAD

    def body(cq_ref, ckv_ref, kr_ref, wq_ref, wkv_ref, c_ref, s_ref, q_ref, k_ref, v_ref):
        q = _dot(cq_ref[...], wq_ref[...])
        kv = _dot(ckv_ref[...], wkv_ref[...])
        cv, sv, krv = c_ref[...], s_ref[...], kr_ref[...]
        for h in range(H):
            qr = q[:, h * HP + MLA_NOPE:(h + 1) * HP]
            q_ref[:, h * HP:h * HP + MLA_NOPE] = (q[:, h * HP:h * HP + MLA_NOPE] * SM_SCALE).astype(BF16)
            q_ref[:, h * HP + MLA_NOPE:(h + 1) * HP] = ((qr * cv + _swap_halves(qr) * sv) * SM_SCALE).astype(BF16)
            k_ref[:, h * HP:h * HP + MLA_NOPE] = kv[:, h * HP:h * HP + MLA_NOPE].astype(BF16)
            k_ref[:, h * HP + MLA_NOPE:(h + 1) * HP] = krv
            v_ref[:, h * MLA_V:(h + 1) * MLA_V] = kv[:, h * HP + MLA_NOPE:(h + 1) * HP].astype(BF16)

    return pl.pallas_call(
        body, name=name, grid=(S // tm,),
        in_specs=[_rows1(tm, MLA_Q_LORA), _rows1(tm, MLA_KV_LORA), _rows1(tm, LANES), _resident(wq.shape),
                  _resident(wukv.shape), _rows1(tm, LANES), _rows1(tm, LANES)],
        out_specs=[_rows1(tm, H * HP), _rows1(tm, H * HP), _rows1(tm, H * MLA_V)],
        out_shape=[SDS((S, H * HP), BF16), SDS((S, H * HP), BF16), SDS((S, H * MLA_V), BF16)],
        compiler_params=_params(("parallel",)),
    )(cq, ckv, krp, wq, wukv, cos_t, sin_t)


def _causal_mask(tq):
    row = lax.broadcasted_iota(jnp.int32, (tq, tq), 0)
    col = lax.broadcasted_iota(jnp.int32, (tq, tq), 1)
    return col <= row


def _tile_rows(j, tq):
    return slice(j * tq, (j + 1) * tq) if isinstance(j, int) else pl.ds(pl.multiple_of(j * tq, tq), tq)


def _attn_fwd(q, k, v, tq, name):
    S = q.shape[0]
    HP = MLA_HEAD_PAD
    H = q.shape[1] // HP
    nq = S // tq

    def body(q_ref, k_ref, v_ref, o_ref, lse_ref):
        for i in range(nq):
            rows = slice(i * tq, (i + 1) * tq)
            qv = q_ref[rows, :]

            def step(j, carry, masked, qv=qv):
                m, l, acc = carry
                krows = _tile_rows(j, tq)
                s = _dot_nt(qv, k_ref[krows, :])
                if masked:
                    s = jnp.where(_causal_mask(tq), s, NEG)
                m_new = jnp.maximum(m, jnp.max(s, axis=1, keepdims=True))
                p = jnp.exp(s - m_new)
                alpha = jnp.exp(m - m_new)
                l = alpha * l + jnp.sum(p, axis=1, keepdims=True)
                acc = alpha * acc + _dot(p.astype(BF16), v_ref[krows, :])
                return m_new, l, acc

            carry = (jnp.full((tq, 1), NEG, F32), jnp.zeros((tq, 1), F32), jnp.zeros((tq, MLA_V), F32))
            for j in range(i):
                carry = step(j, carry, False)
            m, l, acc = step(i, carry, True)
            o_ref[rows, :] = (acc / l).astype(BF16)
            lse_ref[0, rows, :] = jnp.broadcast_to(m + jnp.log(l), (tq, LANES))

    return pl.pallas_call(
        body, name=name, grid=(H,),
        in_specs=[pl.BlockSpec((S, HP), lambda h: (0, h)), pl.BlockSpec((S, HP), lambda h: (0, h)),
                  pl.BlockSpec((S, MLA_V), lambda h: (0, h))],
        out_specs=[pl.BlockSpec((S, MLA_V), lambda h: (0, h)), pl.BlockSpec((1, S, LANES), lambda h: (h, 0, 0))],
        out_shape=[SDS((S, H * MLA_V), BF16), SDS((H, S, LANES), F32)],
        compiler_params=_params(("parallel",)),
    )(q, k, v)


def _attn_bwd(q, k, v, o, do, lse, cos_t, sin_t, tq, name):
    S = q.shape[0]
    HP = MLA_HEAD_PAD
    H = q.shape[1] // HP
    nq = S // tq

    def body(q_ref, k_ref, v_ref, o_ref, do_ref, lse_ref, c_ref, s_ref, dq_ref, dkv_ref, dkr_ref, dq_acc, dl_s):
        @pl.when(pl.program_id(0) == 0)
        def _():
            dkr_ref[...] = jnp.zeros_like(dkr_ref)

        dq_acc[...] = jnp.zeros_like(dq_acc)

        def delta_tile(i, _):
            rows = pl.ds(pl.multiple_of(i * tq, tq), tq)
            d = jnp.sum(do_ref[rows, :].astype(F32) * o_ref[rows, :].astype(F32), axis=1, keepdims=True)
            dl_s[rows, :] = jnp.broadcast_to(d, (tq, LANES))
            return 0

        lax.fori_loop(0, nq, delta_tile, 0)

        for j in range(nq):
            krows = slice(j * tq, (j + 1) * tq)
            kv_k = k_ref[krows, :]
            kv_v = v_ref[krows, :]

            def step(i, carry, masked, kv_k=kv_k, kv_v=kv_v):
                dk, dv = carry
                rows = _tile_rows(i, tq)
                qv = q_ref[rows, :]
                dov = do_ref[rows, :]
                s = _dot_nt(qv, kv_k)
                if masked:
                    s = jnp.where(_causal_mask(tq), s, NEG)
                p = jnp.exp(s - lse_ref[0, rows, 0:1])
                dv = dv + _dot_tn(p.astype(BF16), dov)
                dp = _dot_nt(dov, kv_v)
                ds = (p * (dp - dl_s[rows, 0:1])).astype(BF16)
                dk = dk + _dot_tn(ds, qv)
                dq_acc[rows, :] += _dot(ds, kv_k)
                return dk, dv

            carry = step(j, (jnp.zeros((tq, HP), F32), jnp.zeros((tq, MLA_V), F32)), True)
            for i in range(j + 1, nq):
                carry = step(i, carry, False)
            dk, dv = carry
            dkv_ref[krows, :MLA_NOPE] = dk[:, :MLA_NOPE].astype(BF16)
            dkv_ref[krows, MLA_NOPE:] = dv.astype(BF16)
            dkr_ref[krows, :] += dk[:, MLA_NOPE:]

        def out_tile(i, _):
            rows = pl.ds(pl.multiple_of(i * tq, tq), tq)
            dq = dq_acc[rows, :] * SM_SCALE
            dqr = dq[:, MLA_NOPE:]
            dq_ref[rows, :MLA_NOPE] = dq[:, :MLA_NOPE].astype(BF16)
            dq_ref[rows, MLA_NOPE:] = (dqr * c_ref[rows, :] + _swap_halves(dqr * s_ref[rows, :])).astype(BF16)
            return 0

        lax.fori_loop(0, nq, out_tile, 0)

    return pl.pallas_call(
        body, name=name, grid=(H,),
        in_specs=[pl.BlockSpec((S, HP), lambda h: (0, h)), pl.BlockSpec((S, HP), lambda h: (0, h)),
                  pl.BlockSpec((S, MLA_V), lambda h: (0, h)), pl.BlockSpec((S, MLA_V), lambda h: (0, h)),
                  pl.BlockSpec((S, MLA_V), lambda h: (0, h)), pl.BlockSpec((1, S, LANES), lambda h: (h, 0, 0)),
                  _resident(cos_t.shape), _resident(sin_t.shape)],
        out_specs=[pl.BlockSpec((S, HP), lambda h: (0, h)), pl.BlockSpec((S, HP), lambda h: (0, h)),
                   pl.BlockSpec((S, LANES), lambda h: (0, 0))],
        out_shape=[SDS((S, H * HP), BF16), SDS((S, H * HP), BF16), SDS((S, LANES), F32)],
        scratch_shapes=[pltpu.VMEM((S, HP), F32), pltpu.VMEM((S, LANES), F32)],
        compiler_params=_params(("arbitrary",)),
    )(q, k, v, o, do, lse, cos_t, sin_t)


def _mla_out(o, w_o, x, mod, tm, name):
    S, KO = o.shape
    D = w_o.shape[1]

    def body(o_ref, w_ref, x_ref, mod_ref, x1_ref, y_ref):
        y = _dot(o_ref[...], w_ref[...])
        y_ref[...] = y.astype(BF16)
        x1_ref[...] = x_ref[...] + mod_ref[G1:G1 + 1, :] * y

    return pl.pallas_call(
        body, name=name, grid=(S // tm,),
        in_specs=[_rows1(tm, KO), _resident(w_o.shape), _rows1(tm, D), _resident(mod.shape)],
        out_specs=[_rows1(tm, D), _rows1(tm, D)],
        out_shape=[SDS((S, D), F32), SDS((S, D), BF16)],
        compiler_params=_params(("parallel",)),
    )(o, w_o, x, mod)


def _mla_bwd_o(dx1, y, mod, w_o, tm, name):
    S, D = dx1.shape
    KO = w_o.shape[0]

    def body(d1_ref, y_ref, mod_ref, w_ref, dy_ref, do_ref, st_ref):
        @pl.when(pl.program_id(0) == 0)
        def _():
            st_ref[...] = jnp.zeros_like(st_ref)

        d1 = d1_ref[...]
        dyb = (d1 * mod_ref[G1:G1 + 1, :]).astype(BF16)
        dy_ref[...] = dyb
        st_ref[G1:G1 + 1, :] += _colsum(d1 * y_ref[...].astype(F32))
        do_ref[...] = _dot_nt(dyb, w_ref[...]).astype(BF16)

    return pl.pallas_call(
        body, name=name, grid=(S // tm,),
        in_specs=[_rows1(tm, D), _rows1(tm, D), _resident(mod.shape), _resident(w_o.shape)],
        out_specs=[_rows1(tm, D), _rows1(tm, KO), pl.BlockSpec((8, D), lambda i: (0, 0))],
        out_shape=[SDS((S, D), BF16), SDS((S, KO), BF16), SDS((8, D), F32)],
        compiler_params=_params(("arbitrary",)),
    )(dx1, y, mod, w_o)


def _mla_bwd_lat(dq, dkv, dkr, lat, x, dx1, mod, gn, qg, kvg, wq, wukv, wd, cos_t, sin_t, tm, name):
    S, D = x.shape
    LW = wd.shape[1]
    QL, KL = MLA_Q_LORA, MLA_KV_LORA

    def body(dq_ref, dkv_ref, dkr_ref, lat_ref, x_ref, d1_ref, mod_ref, gn_ref, qg_ref, kvg_ref, wq_ref, wkv_ref, wd_ref,
             c_ref, s_ref, dx_ref, dlat_ref, st_ref, dqg_ref, dkvg_ref):
        @pl.when(pl.program_id(0) == 0)
        def _():
            st_ref[...] = jnp.zeros_like(st_ref)
            dqg_ref[...] = jnp.zeros_like(dqg_ref)
            dkvg_ref[...] = jnp.zeros_like(dkvg_ref)

        lat = lat_ref[...]
        dcq = _dot_nt(dq_ref[...], wq_ref[...])
        nq, rq = _rms(lat[:, :QL])
        dqg_ref[0:1, :] += _colsum(dcq * nq)
        dlat_q = _normmod_bwd(dcq, nq, rq, qg_ref[...]).astype(BF16)
        dckv = _dot_nt(dkv_ref[...], wkv_ref[...])
        nkv, rkv = _rms(lat[:, QL:QL + KL])
        dkvg_ref[0:1, :] += _colsum(dckv * nkv)
        dlat_kv = _normmod_bwd(dckv, nkv, rkv, kvg_ref[...]).astype(BF16)
        dkr = dkr_ref[...]
        dlat_kr = (dkr * c_ref[...] + _swap_halves(dkr * s_ref[...])).astype(BF16)
        dlat_ref[:, :QL] = dlat_q
        dlat_ref[:, QL:QL + KL] = dlat_kv
        dlat_ref[:, QL + KL:] = dlat_kr
        dh = (_dot_nt(dlat_q, wd_ref[:, :QL]) + _dot_nt(dlat_kv, wd_ref[:, QL:QL + KL])
              + _dot_nt(dlat_kr, wd_ref[:, QL + KL:]))
        n, rr = _rms(x_ref[...])
        gn_v = gn_ref[...]
        sc1p = 1.0 + mod_ref[SC1:SC1 + 1, :]
        t = _colsum(dh * n)
        st_ref[SH1:SH1 + 1, :] += _colsum(dh)
        st_ref[SC1:SC1 + 1, :] += t * gn_v
        st_ref[3:4, :] += t * sc1p
        dx_ref[...] = d1_ref[...] + _normmod_bwd(dh, n, rr, gn_v * sc1p)

    HW = wq.shape[1]
    return pl.pallas_call(
        body, name=name, grid=(S // tm,),
        in_specs=[_rows1(tm, HW), _rows1(tm, HW), _rows1(tm, LANES), _rows1(tm, LW), _rows1(tm, D), _rows1(tm, D),
                  _resident(mod.shape), _resident(gn.shape), _resident(qg.shape), _resident(kvg.shape),
                  _resident(wq.shape), _resident(wukv.shape), _resident(wd.shape), _rows1(tm, LANES), _rows1(tm, LANES)],
        out_specs=[_rows1(tm, D), _rows1(tm, LW), pl.BlockSpec((8, D), lambda i: (0, 0)),
                   pl.BlockSpec((8, QL), lambda i: (0, 0)), pl.BlockSpec((8, KL), lambda i: (0, 0))],
        out_shape=[SDS((S, D), F32), SDS((S, LW), BF16), SDS((8, D), F32), SDS((8, QL), F32), SDS((8, KL), F32)],
        compiler_params=_params(("arbitrary",)),
    )(dq, dkv, dkr, lat, x, dx1, mod, gn, qg, kvg, wq, wukv, wd, cos_t, sin_t)


def _loss_head(x, tgt, fg, tm, name):
    S, D = x.shape
    nt = S // tm

    def body(x_ref, t_ref, g_ref, dx_ref, acc_ref):
        i = pl.program_id(0)

        @pl.when(i == 0)
        def _():
            acc_ref[...] = jnp.zeros_like(acc_ref)

        n, rr = _rms(x_ref[...])
        g = g_ref[...]
        err = n * g - t_ref[...]
        acc_ref[1:2, :] += _colsum(err * err) * (0.5 / D)
        dy = err * (1.0 / D)
        acc_ref[0:1, :] += _colsum(dy * n)
        dx_ref[...] = _normmod_bwd(dy, n, rr, g)

        @pl.when(i == nt - 1)
        def _():
            acc_ref[2:3, :] = jnp.broadcast_to(jnp.sum(acc_ref[1:2, :], axis=1, keepdims=True), (1, D))

    return pl.pallas_call(
        body, name=name, grid=(nt,),
        in_specs=[_rows1(tm, D), _rows1(tm, D), _resident(fg.shape)],
        out_specs=[_rows1(tm, D), pl.BlockSpec((8, D), lambda i: (0, 0))],
        out_shape=[SDS((S, D), F32), SDS((8, D), F32)],
        compiler_params=_params(("arbitrary",)),
    )(x, tgt, fg)


def _rope_tables(positions, S):
    inv_freq = ROPE_THETA ** (-jnp.arange(0, MLA_ROPE, 2, dtype=F32) / MLA_ROPE)
    ang = positions.reshape(S, 1).astype(F32) * inv_freq
    cos, sin = jnp.cos(ang), jnp.sin(ang)
    z = jnp.zeros((S, LANES - MLA_ROPE), F32)
    return jnp.concatenate([cos, cos, z], axis=1), jnp.concatenate([-sin, sin, z], axis=1)


def _pad_heads(w, per_head):
    K = w.shape[0]
    H = w.shape[1] // per_head
    w3 = w.reshape(K, H, per_head)
    return jnp.pad(w3, ((0, 0), (0, 0), (0, MLA_HEAD_PAD - per_head))).reshape(K, H * MLA_HEAD_PAD)


def _unpad_heads(w, per_head):
    K = w.shape[0]
    H = w.shape[1] // MLA_HEAD_PAD
    return w.reshape(K, H, MLA_HEAD_PAD)[:, :, :per_head].reshape(K, H * per_head)


def _tiles(S):
    return min(512, S), min(256, S), min(512, S)


def _layer_forward(i, x, mod, gmix, gmlp, w, small, tables):
    if True:
        S, D = x.shape
        tm, tms, tq = _tiles(S)
        cos_t, sin_t = tables
        mod, gmix, gmlp = {i: mod}, {i: gmix}, {i: gmlp}
        kind = i % 3
        sv = {"x": x}
        if kind == 0:
            x1 = _pool_fwd(x, mod[i], gmix[i], w["pool_w"], small["pool_scale"][i // 3], tm, f"pool_fwd_{i}")
        elif kind == 1:
            x1, sv["h"], sv["pre"], sv["y"] = _sgu_fwd(
                x, mod[i], gmix[i], w["sgu_w_in"], small["sgu_ln_g"], small["sgu_ln_b"], small["sgu_w_s"],
                small["sgu_b_s_t"], w["sgu_w_out"], tms, f"sgu_fwd_{i}")
        else:
            sv["h"], sv["lat"], sv["cq"], sv["ckv"], krp = _mla_lat(
                x, mod[i], gmix[i], w["mla_wd"], small["mla_q_norm_g"], small["mla_kv_norm_g"], cos_t, sin_t, tm,
                f"mla_lat_{i}")
            sv["q"], sv["k"], sv["v"] = _mla_qkv(sv["cq"], sv["ckv"], krp, w["mla_wq"], w["mla_w_ukv"], cos_t, sin_t,
                                                 tm, f"mla_qkv_{i}")
            sv["o"], sv["lse"] = _attn_fwd(sv["q"], sv["k"], sv["v"], tq, f"attn_fwd_{i}")
            x1, sv["y"] = _mla_out(sv["o"], w["mla_w_o"], x, mod[i], tm, f"mla_out_{i}")
        sv["x1"] = x1
        Fh = w["mlp_w1"].shape[1]
        sv["h2"], sv["r"] = _mlp_up(x1, mod[i], gmlp[i], w["mlp_w1"], tm, min(2048, Fh), f"mlp_up_{i}")
        if callable(w["mlp_w2"]):
            w["mlp_w2"] = w["mlp_w2"](sv["r"])
        x, sv["o2"] = _mlp_down(sv["r"], w["mlp_w2"], x1, mod[i], tm, f"mlp_down_{i}")
        return x, sv


def _layer_backward(i, dx, sv, mod, gmix, gmlp, w, small, tables, on_mlp_grads=None):
    if True:
        S, D = dx.shape
        tm, tms, tq = _tiles(S)
        cos_t, sin_t = tables
        kind = i % 3
        sgrads = {}
        Fh = w["mlp_w1"].shape[1]
        g = {}
        d_a, d_o, st_a = _mlp_bwd_a(dx, sv["o2"], mod, w["mlp_w2"], sv["r"], tm, min(2048, Fh), f"mlp_bwd_a_{i}")
        g["mlp_w2"] = _mm_tn(sv["r"], d_o, min(512, Fh), D, f"mlp_dw2_{i}", square_a=True)
        g["mlp_w1"] = _mm_tn(sv["h2"], d_a, D, min(512, Fh), f"mlp_dw1_{i}", col_shards=True)
        if on_mlp_grads is not None:
            mod = mod + on_mlp_grads({n: g.pop(n) for n in ("mlp_w1", "mlp_w2")})
        mod, gmix, gmlp = {i: mod}, {i: gmix}, {i: gmlp}
        dx1, st_b = _mlp_bwd_b(d_a, w["mlp_w1"], sv["x1"], dx, mod[i], gmlp[i], tm, f"mlp_bwd_b_{i}")
        if kind == 0:
            dx, st_m, dpw = _pool_bwd(sv["x"], dx1, mod[i], gmix[i], w["pool_w"], small["pool_scale"][i // 3], tm,
                                      f"pool_bwd_{i}")
            g["pool_w"] = dpw
            sgrads[f"pool_scale_{i // 3}"] = st_m[4:5]
        elif kind == 1:
            dx, dyb, gated, dpre, st_m, dws, dbs = _sgu_bwd(
                sv["x"], dx1, sv["pre"], sv["y"], mod[i], gmix[i], w["sgu_w_in"], small["sgu_ln_g"], small["sgu_ln_b"],
                small["sgu_w_s"], small["sgu_b_s_t"], w["sgu_w_out"], tms, f"sgu_bwd_{i}")
            W = gated.shape[1]
            g["sgu_w_out"] = _mm_tn(gated, dyb, min(512, W), D, f"sgu_dwout_{i}")
            g["sgu_w_in"] = _mm_tn(sv["h"], dpre, D, min(512, 2 * W), f"sgu_dwin_{i}", col_shards=True)
            sgrads["sgu_ln_g"], sgrads["sgu_ln_b"] = st_m[4:5], st_m[5:6]
            sgrads["sgu_w_s"], sgrads["sgu_b_s"] = dws, dbs[:, :, 0]
        else:
            dyb, do, st_o = _mla_bwd_o(dx1, sv["y"], mod[i], w["mla_w_o"], tm, f"mla_bwd_o_{i}")
            KO = do.shape[1]
            g["mla_w_o"] = _mm_tn(sv["o"], dyb, min(512, KO), D, f"mla_dwo_{i}")
            dq, dkv, dkr = _attn_bwd(sv["q"], sv["k"], sv["v"], sv["o"], do, sv["lse"], cos_t, sin_t, tq, f"attn_bwd_{i}")
            dx, dlat, st_m, dqg, dkvg = _mla_bwd_lat(
                dq, dkv, dkr, sv["lat"], sv["x"], dx1, mod[i], gmix[i], small["mla_q_norm_g"], small["mla_kv_norm_g"],
                w["mla_wq"], w["mla_w_ukv"], w["mla_wd"], cos_t, sin_t, tm, f"mla_bwd_lat_{i}")
            HW = dq.shape[1]
            g["mla_wq"] = _mm_tn(sv["cq"], dq, MLA_Q_LORA, min(1024, HW), f"mla_dwq_{i}")
            g["mla_w_ukv"] = _mm_tn(sv["ckv"], dkv, MLA_KV_LORA, min(1024, HW), f"mla_dwukv_{i}", col_shards=True)
            g["mla_wd"] = _mm_tn(sv["h"], dlat, D, dlat.shape[1], f"mla_dwd_{i}")
            st_m = jnp.concatenate([st_m[0:2], st_o[2:3], st_m[3:]], axis=0)
            sgrads["mla_q_norm_g"], sgrads["mla_kv_norm_g"] = dqg[0:1], dkvg[0:1]
        stats = jnp.concatenate([st_m[0:3], st_b[3:5], st_a[5:6], st_m[3:4], st_b[6:7]], axis=0)
        return dx, stats, g, sgrads


def _local_step(x, tgt, positions, mod, gmix, gmlp, fg, wts, small):
    S = x.shape[0]
    L = mod.shape[0]
    tables = _rope_tables(positions, S)
    saved = []
    for i in range(L):
        x, sv = _layer_forward(i, x, mod[i], gmix[i], gmlp[i], wts[i], small, tables)
        saved.append(sv)
    dx, loss_acc = _loss_head(x, tgt, fg, _tiles(S)[0], "loss_head")
    stats, grads, sgrads = [None] * L, [None] * L, {}
    for i in reversed(range(L)):
        dx, stats[i], grads[i], sg = _layer_backward(i, dx, saved[i], mod[i], gmix[i], gmlp[i], wts[i], small, tables)
        sgrads.update(sg)
    return loss_acc, dx, stats, grads, sgrads


HBM_SPEC = pl.BlockSpec(memory_space=pltpu.HBM)
VMEM_SPEC = pl.BlockSpec(memory_space=pltpu.VMEM)


def _my_place():
    return lax.axis_index("x"), lax.axis_index("y"), lax.axis_index("c")


def _flip(v, bit):
    return 1 - v if bit else v


def _small_all_gather(v, name):
    R, C = v.shape

    def body(x_ref, out_ref, send_sems, recv_sems):
        x, y, c = _my_place()
        me = 4 * x + 2 * y + c
        out_ref[me] = x_ref[...]
        sends = []
        for k in range(1, NDEV):
            peer = (_flip(x, k & 4), _flip(y, k & 2), _flip(c, k & 1))
            cp = pltpu.make_async_remote_copy(src_ref=x_ref, dst_ref=out_ref.at[me], send_sem=send_sems.at[k - 1],
                                              recv_sem=recv_sems.at[k - 1], device_id=peer, device_id_type=MESH)
            cp.start()
            sends.append(cp)
        for k in range(1, NDEV):
            src = 4 * _flip(x, k & 4) + 2 * _flip(y, k & 2) + _flip(c, k & 1)
            pltpu.make_async_remote_copy(src_ref=x_ref, dst_ref=out_ref.at[src], send_sem=send_sems.at[k - 1],
                                         recv_sem=recv_sems.at[k - 1], device_id=(x, y, c), device_id_type=MESH).wait_recv()
        for cp in sends:
            cp.wait_send()

    return pl.pallas_call(
        body, name=name, out_shape=SDS((NDEV, R, C), v.dtype), in_specs=[VMEM_SPEC], out_specs=VMEM_SPEC,
        scratch_shapes=[pltpu.SemaphoreType.DMA((NDEV - 1,)), pltpu.SemaphoreType.DMA((NDEV - 1,))],
        compiler_params=pltpu.CompilerParams(vmem_limit_bytes=V7X_VMEM_LIMIT),
    )(v)


def _slab(ref, axis, width, dev):
    idx = [slice(None)] * len(ref.shape)
    idx[axis] = pl.ds(pl.multiple_of(dev * width, width), width)
    return ref.at[tuple(idx)]


def _all_gather_group(shards, axes, after, name):
    nt = len(shards)
    out_shapes = [SDS(tuple(s * NDEV if a == ax else s for a, s in enumerate(sh.shape)), sh.dtype)
                  for sh, ax in zip(shards, axes)]

    def body(*refs):
        ins, outs = refs[:nt], refs[nt + 1:2 * nt + 1]
        send_sems, recv_sems, local_sems = refs[2 * nt + 1:]
        x, y, c = _my_place()
        me = 4 * x + 2 * y + c
        sibling = (x, y, 1 - c)
        chips = [(1 - x, y), (x, 1 - y), (1 - x, 1 - y)]

        def block(t, dev):
            return _slab(outs[t], axes[t], ins[t].shape[axes[t]], dev)

        def copy(t, k, dev, to, src=None):
            return pltpu.make_async_remote_copy(
                src_ref=block(t, dev) if src is None else src, dst_ref=block(t, dev), send_sem=send_sems.at[t, k],
                recv_sem=recv_sems.at[t, k], device_id=to, device_id_type=MESH)

        mine = [pltpu.make_async_copy(ins[t], block(t, me), local_sems.at[t]) for t in range(nt)]
        for cp in mine:
            cp.start()
        first = []
        for t in range(nt):
            first.append(copy(t, 0, me, sibling, src=ins[t]))
            first += [copy(t, 1 + j, me, (cx, cy, c), src=ins[t]) for j, (cx, cy) in enumerate(chips)]
        for cp in first:
            cp.start()
        passed = []
        for j, (cx, cy) in enumerate(chips):
            for t in range(nt):
                copy(t, 1 + j, 4 * cx + 2 * cy + c, (x, y, c)).wait_recv()
                cp = copy(t, 4 + j, 4 * cx + 2 * cy + c, sibling)
                cp.start()
                passed.append(cp)
        for t in range(nt):
            copy(t, 0, 4 * x + 2 * y + (1 - c), (x, y, c)).wait_recv()
        for j, (cx, cy) in enumerate(chips):
            for t in range(nt):
                copy(t, 4 + j, 4 * cx + 2 * cy + (1 - c), (x, y, c)).wait_recv()
        for cp in first + passed:
            cp.wait_send()
        for cp in mine:
            cp.wait()

    return pl.pallas_call(
        body, name=name, out_shape=out_shapes, in_specs=[HBM_SPEC] * nt + [ANY_SPEC], out_specs=[HBM_SPEC] * nt,
        scratch_shapes=[pltpu.SemaphoreType.DMA((nt, NDEV - 1)), pltpu.SemaphoreType.DMA((nt, NDEV - 1)),
                        pltpu.SemaphoreType.DMA((nt,))],
    )(*shards, after)


def _reduce_scatter_sibling(grads, name):
    nt = len(grads)
    NCH = NDEV // 2
    out_shapes = [SDS((NCH,) + gr.shape[1:], gr.dtype) for gr in grads]

    def body(*refs):
        ins, lands = refs[:nt], refs[nt:2 * nt]
        send_sems, recv_sems = refs[2 * nt:]
        x, y, c = _my_place()
        sends = []
        for t in range(nt):
            for k in range(NCH):
                cp = pltpu.make_async_remote_copy(
                    src_ref=ins[t].at[2 * k + (1 - c)], dst_ref=lands[t].at[k], send_sem=send_sems.at[t, k],
                    recv_sem=recv_sems.at[t, k], device_id=(x, y, 1 - c), device_id_type=MESH)
                cp.start()
                sends.append(cp)
        for cp in sends:
            cp.wait_recv()
        for cp in sends:
            cp.wait_send()

    return pl.pallas_call(
        body, name=name, out_shape=out_shapes, in_specs=[HBM_SPEC] * nt, out_specs=[HBM_SPEC] * nt,
        scratch_shapes=[pltpu.SemaphoreType.DMA((nt, NCH)), pltpu.SemaphoreType.DMA((nt, NCH))],
    )(*grads)


SEM_SPEC = pl.BlockSpec(memory_space=pltpu.SEMAPHORE)
ANY_SPEC = pl.BlockSpec(memory_space=pl.ANY)
SPLIT_PARAMS = pltpu.CompilerParams(has_side_effects=pltpu.SideEffectType.DATAFLOW_SIDE_EFFECTING)
TOKEN = SDS((8, LANES), F32)


def _in_hbm(arrays):
    return [pltpu.with_memory_space_constraint(v, pltpu.HBM) for v in arrays]


def _split_start(body, srcs, lands, after, n_sem, name):
    ns, nl = len(srcs), len(lands)
    bufs = list(srcs) + list(lands)
    res = pl.pallas_call(
        body, name=name,
        out_shape=(pltpu.SemaphoreType.DMA((ns * n_sem,)), pltpu.SemaphoreType.DMA((ns * n_sem,)),
                   *[pltpu.HBM(v.shape, v.dtype) for v in bufs], TOKEN),
        in_specs=[HBM_SPEC] * (ns + nl) + [ANY_SPEC],
        out_specs=(SEM_SPEC, SEM_SPEC, *[HBM_SPEC] * (ns + nl), VMEM_SPEC),
        input_output_aliases={t: 2 + t for t in range(ns + nl)}, compiler_params=SPLIT_PARAMS,
    )(*_in_hbm(bufs), after)
    return res[0], res[1], list(res[2:2 + ns]), list(res[2 + ns:2 + ns + nl]), res[-1]


def _split_wait(body, send_sems, recv_sems, srcs, lands, after, name):
    ns, nl = len(srcs), len(lands)
    bufs = list(srcs) + list(lands)
    res = pl.pallas_call(
        body, name=name, out_shape=tuple(pltpu.HBM(v.shape, v.dtype) for v in bufs),
        in_specs=[HBM_SPEC] * (ns + nl) + [SEM_SPEC, SEM_SPEC, ANY_SPEC], out_specs=tuple([HBM_SPEC] * (ns + nl)),
        input_output_aliases={t: t for t in range(ns + nl)}, compiler_params=SPLIT_PARAMS,
    )(*bufs, send_sems, recv_sems, after)
    return list(res[:ns]), list(res[ns:])


def _chips_exchange_start(parts, after, name):
    nt = len(parts)
    lands = [lax.empty((3,) + p.shape[1:], p.dtype) for p in parts]

    def body(*refs):
        ins, lnd = refs[:nt], refs[nt:2 * nt]
        send_sems, recv_sems, token = refs[2 * nt + 1], refs[2 * nt + 2], refs[-1]
        x, y, c = _my_place()
        for t in range(nt):
            for m in range(1, 4):
                px, py = _flip(x, m & 2), _flip(y, m & 1)
                pltpu.make_async_remote_copy(
                    src_ref=ins[t].at[2 * px + py], dst_ref=lnd[t].at[m - 1], send_sem=send_sems.at[3 * t + m - 1],
                    recv_sem=recv_sems.at[3 * t + m - 1], device_id=(px, py, c), device_id_type=MESH).start()
        token[...] = jnp.zeros_like(token)

    return _split_start(body, parts, lands, after, 3, name)


def _chips_exchange_wait(send_sems, recv_sems, parts, lands, after, name):
    nt = len(parts)

    def body(*refs):
        ins, lnd = refs[:nt], refs[nt:2 * nt]
        s_sems, r_sems = refs[2 * nt], refs[2 * nt + 1]
        x, y, c = _my_place()
        for t in range(nt):
            for m in range(1, 4):
                cp = pltpu.make_async_remote_copy(
                    src_ref=ins[t].at[0], dst_ref=lnd[t].at[m - 1], send_sem=s_sems.at[3 * t + m - 1],
                    recv_sem=r_sems.at[3 * t + m - 1], device_id=(x, y, c), device_id_type=MESH)
                cp.wait_send()
                cp.wait_recv()

    return _split_wait(body, send_sems, recv_sems, parts, lands, after, name)


def _direct_exchange_start(grads, after, name):
    nt = len(grads)
    lands = [lax.empty((NDEV - 1,) + gr.shape[1:], gr.dtype) for gr in grads]

    def body(*refs):
        ins, lnd = refs[:nt], refs[nt:2 * nt]
        send_sems, recv_sems, token = refs[2 * nt + 1], refs[2 * nt + 2], refs[-1]
        x, y, c = _my_place()
        for t in range(nt):
            for k in range(1, NDEV):
                px, py, pc = _flip(x, k & 4), _flip(y, k & 2), _flip(c, k & 1)
                pltpu.make_async_remote_copy(
                    src_ref=ins[t].at[4 * px + 2 * py + pc], dst_ref=lnd[t].at[k - 1],
                    send_sem=send_sems.at[(NDEV - 1) * t + k - 1], recv_sem=recv_sems.at[(NDEV - 1) * t + k - 1],
                    device_id=(px, py, pc), device_id_type=MESH).start()
        token[...] = jnp.zeros_like(token)

    return _split_start(body, grads, lands, after, NDEV - 1, name)


def _direct_exchange_wait(send_sems, recv_sems, grads, lands, after, name):
    nt = len(grads)

    def body(*refs):
        ins, lnd = refs[:nt], refs[nt:2 * nt]
        s_sems, r_sems = refs[2 * nt], refs[2 * nt + 1]
        x, y, c = _my_place()
        for t in range(nt):
            for k in range(1, NDEV):
                cp = pltpu.make_async_remote_copy(
                    src_ref=ins[t].at[0], dst_ref=lnd[t].at[k - 1], send_sem=s_sems.at[(NDEV - 1) * t + k - 1],
                    recv_sem=r_sems.at[(NDEV - 1) * t + k - 1], device_id=(x, y, c), device_id_type=MESH)
                cp.wait_send()
                cp.wait_recv()

    return _split_wait(body, send_sems, recv_sems, grads, lands, after, name)


def _place_own(shards, axes, name):
    nt = len(shards)
    out_shapes = [SDS(tuple(s * NDEV if a == ax else s for a, s in enumerate(sh.shape)), sh.dtype)
                  for sh, ax in zip(shards, axes)]

    def body(*refs):
        ins, outs, sems = refs[:nt], refs[nt:2 * nt], refs[2 * nt]
        x, y, c = _my_place()
        copies = [pltpu.make_async_copy(ins[t], _slab(outs[t], axes[t], ins[t].shape[axes[t]], 4 * x + 2 * y + c), sems.at[t])
                  for t in range(nt)]
        for cp in copies:
            cp.start()
        for cp in copies:
            cp.wait()

    return pl.pallas_call(
        body, name=name, out_shape=out_shapes, in_specs=[VMEM_SPEC] * nt, out_specs=[HBM_SPEC] * nt,
        scratch_shapes=[pltpu.SemaphoreType.DMA((nt,))],
        compiler_params=pltpu.CompilerParams(vmem_limit_bytes=V7X_VMEM_LIMIT),
    )(*shards)


def _small_gather_start(v, me, after, name):
    land = lax.dynamic_update_slice(lax.empty((NDEV,) + v.shape, v.dtype), v[None], (me, 0, 0))

    def body(*refs):
        src, lnd = refs[0], refs[1]
        send_sems, recv_sems, token = refs[3], refs[4], refs[-1]
        x, y, c = _my_place()
        for k in range(1, NDEV):
            peer = (_flip(x, k & 4), _flip(y, k & 2), _flip(c, k & 1))
            pltpu.make_async_remote_copy(src_ref=src, dst_ref=lnd.at[4 * x + 2 * y + c], send_sem=send_sems.at[k - 1],
                                         recv_sem=recv_sems.at[k - 1], device_id=peer, device_id_type=MESH).start()
        token[...] = jnp.zeros_like(token)

    return _split_start(body, [v], [land], after, NDEV - 1, name)


def _small_gather_wait(send_sems, recv_sems, srcs, lands, after, name):
    def body(*refs):
        src, lnd, s_sems, r_sems = refs[0], refs[1], refs[2], refs[3]
        x, y, c = _my_place()
        for k in range(1, NDEV):
            sender = 4 * _flip(x, k & 4) + 2 * _flip(y, k & 2) + _flip(c, k & 1)
            cp = pltpu.make_async_remote_copy(src_ref=src, dst_ref=lnd.at[sender], send_sem=s_sems.at[k - 1],
                                              recv_sem=r_sems.at[k - 1], device_id=(x, y, c), device_id_type=MESH)
            cp.wait_send()
            cp.wait_recv()

    return _split_wait(body, send_sems, recv_sems, srcs, lands, after, name)[1][0]


def _gather_start(shards, axes, me, after, name):
    nt = len(shards)
    fulls = _place_own(shards, axes, name + "_own")

    def body(*refs):
        ins, outs = refs[:nt], refs[nt:2 * nt]
        send_sems, recv_sems, token = refs[2 * nt + 1], refs[2 * nt + 2], refs[-1]
        x, y, c = _my_place()
        dev = 4 * x + 2 * y + c
        peers = [(x, y, 1 - c), (1 - x, y, c), (x, 1 - y, c), (1 - x, 1 - y, c)]
        for t in range(nt):
            dst = _slab(outs[t], axes[t], ins[t].shape[axes[t]], dev)
            for k, peer in enumerate(peers):
                pltpu.make_async_remote_copy(src_ref=ins[t], dst_ref=dst, send_sem=send_sems.at[4 * t + k],
                                             recv_sem=recv_sems.at[4 * t + k], device_id=peer, device_id_type=MESH).start()
        token[...] = jnp.zeros_like(token)

    return _split_start(body, shards, fulls, after, 4, name)


def _gather_wait(send_sems, recv_sems, shards, fulls, axes, after, name):
    nt = len(shards)

    def body(*refs):
        ins, outs = refs[:nt], refs[nt:2 * nt]
        s_sems, r_sems = refs[2 * nt], refs[2 * nt + 1]
        x, y, c = _my_place()
        senders = [4 * x + 2 * y + (1 - c), 4 * (1 - x) + 2 * y + c, 4 * x + 2 * (1 - y) + c, 4 * (1 - x) + 2 * (1 - y) + c]
        for t in range(nt):
            for k, src_dev in enumerate(senders):
                cp = pltpu.make_async_remote_copy(
                    src_ref=ins[t], dst_ref=_slab(outs[t], axes[t], ins[t].shape[axes[t]], src_dev),
                    send_sem=s_sems.at[4 * t + k], recv_sem=r_sems.at[4 * t + k], device_id=(x, y, c), device_id_type=MESH)
                cp.wait_send()
                cp.wait_recv()

    return _split_wait(body, send_sems, recv_sems, shards, fulls, after, name)[1]


def _gather_pass_on(fulls, axes, name):
    nt = len(fulls)

    def body(*refs):
        outs = refs[nt:2 * nt]
        send_sems, recv_sems = refs[2 * nt:]
        x, y, c = _my_place()
        chips = [(1 - x, y), (x, 1 - y), (1 - x, 1 - y)]

        def copy(t, j, pc):
            cx, cy = chips[j]
            blk = _slab(outs[t], axes[t], outs[t].shape[axes[t]] // NDEV, 4 * cx + 2 * cy + pc)
            return pltpu.make_async_remote_copy(src_ref=blk, dst_ref=blk, send_sem=send_sems.at[t, j],
                                                recv_sem=recv_sems.at[t, j], device_id=(x, y, 1 - c), device_id_type=MESH)

        sends = [copy(t, j, c) for t in range(nt) for j in range(3)]
        for cp in sends:
            cp.start()
        for t in range(nt):
            for j in range(3):
                copy(t, j, 1 - c).wait_recv()
        for cp in sends:
            cp.wait_send()

    return pl.pallas_call(
        body, name=name, out_shape=[SDS(f.shape, f.dtype) for f in fulls], in_specs=[HBM_SPEC] * nt,
        out_specs=[HBM_SPEC] * nt, input_output_aliases={t: t for t in range(nt)},
        scratch_shapes=[pltpu.SemaphoreType.DMA((nt, 3)), pltpu.SemaphoreType.DMA((nt, 3))],
    )(*fulls)


def _row_tile(R, C, itemsize=4, target=1 << 20):
    best = R
    for tr in range(8, R, 8):
        if R % tr == 0 and tr * C * itemsize <= target:
            best = tr
    return best if best * C * itemsize <= target or best == R else R


def _as2d(a):
    return a.reshape(-1, a.shape[-1])


def _add_pairs(grad, land, c_me, name):
    blk = grad.shape[1:]
    NCH = NDEV // 2
    C = blk[-1]
    R = math.prod(blk[:-1])
    tr = _row_tile(R, C, 2)

    def body(c_ref, a_ref, b_ref, o_ref):
        o_ref[...] = (a_ref[...].astype(F32) + b_ref[...].astype(F32)).astype(o_ref.dtype)

    out = pl.pallas_call(
        body, name=name, out_shape=SDS((NCH, R, C), grad.dtype),
        grid_spec=pltpu.PrefetchScalarGridSpec(
            num_scalar_prefetch=1, grid=(NCH, R // tr),
            in_specs=[pl.BlockSpec((None, None, tr, C), lambda k, i, c: (k, c[0], i, 0)),
                      pl.BlockSpec((None, tr, C), lambda k, i, c: (k, i, 0))],
            out_specs=pl.BlockSpec((None, tr, C), lambda k, i, c: (k, i, 0))),
        compiler_params=_params(("parallel", "parallel")),
    )(c_me.reshape(1).astype(jnp.int32), grad.reshape(NCH, 2, R, C), land.reshape(NCH, R, C))
    return out.reshape((NCH,) + blk)


def _adamw_math(g, w, m, v):
    m2 = ADAM_B1 * m + (1.0 - ADAM_B1) * g
    v2 = ADAM_B2 * v + (1.0 - ADAM_B2) * (g * g)
    m_hat = m2 / (1.0 - ADAM_B1 ** ADAM_STEP)
    v_hat = v2 / (1.0 - ADAM_B2 ** ADAM_STEP)
    delta = -ADAM_LR * (m_hat / (jnp.sqrt(v_hat) + ADAM_EPS) + ADAM_WD * w)
    return delta, m2, v2


def _adamw(parts, w, m, v, name):
    shp = w.shape
    w2, m2, v2 = _as2d(w), _as2d(m), _as2d(v)
    R, C = w2.shape
    tr = _row_tile(R, C)
    p3, specs, picks = [], [], []
    for p in parts:
        if isinstance(p, tuple):
            specs.append(pl.BlockSpec((1, tr, C), lambda i, s, k=len(picks): (s[k], i, 0)))
            picks.append(p[1])
            p = p[0]
        else:
            specs.append(pl.BlockSpec((p.size // (R * C), tr, C), lambda i, s: (0, i, 0)))
        p3.append(p.reshape((-1, R, C)))
    npart = len(p3)
    picks = jnp.stack(picks).astype(jnp.int32) if picks else jnp.zeros((1,), jnp.int32)
    rows = pl.BlockSpec((tr, C), lambda i, s: (i, 0))

    def body(s_ref, *refs):
        prefs = refs[:npart]
        w_ref, m_ref, v_ref, g_ref, d_ref, nm_ref, nv_ref = refs[npart:]
        g = None
        for pr in prefs:
            for k in range(pr.shape[0]):
                term = pr[k].astype(F32)
                g = term if g is None else g + term
        g_ref[...] = g
        d_ref[...], nm_ref[...], nv_ref[...] = _adamw_math(g, w_ref[...], m_ref[...], v_ref[...])

    outs = pl.pallas_call(
        body, name=name, out_shape=[SDS((R, C), F32)] * 4,
        grid_spec=pltpu.PrefetchScalarGridSpec(num_scalar_prefetch=1, grid=(R // tr,), in_specs=specs + [rows] * 3,
                                               out_specs=[rows] * 4),
        compiler_params=_params(("parallel",)),
    )(picks, *p3, w2, m2, v2)
    return [o.reshape(shp) for o in outs]


def _ada_fwd(c_all, ada_w, ada_b_mine, name):
    L, D, Wc = ada_w.shape

    def body(c_ref, w_ref, b_ref, o_ref):
        cv = c_ref[...]
        act = cv * (1.0 / (1.0 + jnp.exp(-cv)))
        o_ref[0] = jnp.dot(act, w_ref[0], preferred_element_type=F32, precision=lax.Precision.HIGHEST) + b_ref[0]

    return pl.pallas_call(
        body, name=name, grid=(L,),
        in_specs=[_resident(c_all.shape), pl.BlockSpec((1, D, Wc), lambda l: (l, 0, 0)), pl.BlockSpec((1, 1, Wc), lambda l: (l, 0, 0))],
        out_specs=pl.BlockSpec((1, NDEV, Wc), lambda l: (l, 0, 0)), out_shape=SDS((L, NDEV, Wc), F32),
        compiler_params=_params(("parallel",)),
    )(c_all, ada_w, ada_b_mine.reshape(L, 1, Wc))


def _ada_bwd(c_all, dmod_mine, name):
    L, _, Wc = dmod_mine.shape
    D = c_all.shape[1]

    def body(c_ref, d_ref, o_ref):
        cv = c_ref[...]
        act = cv * (1.0 / (1.0 + jnp.exp(-cv)))
        o_ref[0] = lax.dot_general(act, d_ref[0], (((0,), (0,)), ((), ())), preferred_element_type=F32,
                                   precision=lax.Precision.HIGHEST)

    return pl.pallas_call(
        body, name=name, grid=(L,),
        in_specs=[_resident(c_all.shape), pl.BlockSpec((1, NDEV, Wc), lambda l: (l, 0, 0))],
        out_specs=pl.BlockSpec((1, D, Wc), lambda l: (l, 0, 0)), out_shape=SDS((L, D, Wc), F32),
        compiler_params=_params(("parallel",)),
    )(c_all, dmod_mine)


WEIGHT_NAMES = ['ada_w', 'ada_b', 'norm_mix_g', 'norm_mlp_g', 'pool_w', 'pool_scale', 'sgu_w_in', 'sgu_ln_g', 'sgu_ln_b',
                'sgu_w_s', 'sgu_b_s', 'sgu_w_out', 'mla_w_dq_dkv', 'mla_q_norm_g', 'mla_kv_norm_g', 'mla_w_uq', 'mla_w_ukv',
                'mla_w_o', 'mlp_w1', 'mlp_w2', 'final_g']
REPLICATED_EARLY = ['sgu_ln_g', 'sgu_ln_b', 'sgu_w_s', 'sgu_b_s', 'mla_kv_norm_g']
REPLICATED = ['ada_b', 'norm_mix_g', 'norm_mlp_g', 'final_g']
PACK_ROWS = 64
DIRECT_FROM = 2
Q_HEAD = MLA_NOPE + MLA_ROPE


def _layer_matrices(i):
    kind, j = i % 3, i // 3
    if kind == 0:
        mats = [("pool_w", j, 1)]
    elif kind == 1:
        mats = [("sgu_w_in", j, 1), ("sgu_w_out", j, 0)]
    else:
        mats = [("mla_w_dq_dkv", j, 0), ("mla_w_uq", j, 1), ("mla_w_ukv", j, 1), ("mla_w_o", j, 0)]
    return mats + [("mlp_w1", i, 1), ("mlp_w2", i, 0)]


def _pack(arrays):
    flat = jnp.concatenate([a.reshape(-1).astype(F32) for a in arrays])
    rows = -(-flat.size // (LANES * PACK_ROWS)) * PACK_ROWS
    return jnp.pad(flat, (0, rows * LANES - flat.size)).reshape(rows, LANES)


def kernel(x, c, positions, ada_w, ada_b, norm_mix_g, norm_mlp_g, pool_w, pool_scale, sgu_w_in, sgu_ln_g, sgu_ln_b, sgu_w_s, sgu_b_s, sgu_w_out, mla_w_dq_dkv, mla_q_norm_g, mla_kv_norm_g, mla_w_uq, mla_w_ukv, mla_w_o, mlp_w1, mlp_w2, final_g, loss_target, m_ada_w, m_ada_b, m_norm_mix_g, m_norm_mlp_g, m_pool_w, m_pool_scale, m_sgu_w_in, m_sgu_ln_g, m_sgu_ln_b, m_sgu_w_s, m_sgu_b_s, m_sgu_w_out, m_mla_w_dq_dkv, m_mla_q_norm_g, m_mla_kv_norm_g, m_mla_w_uq, m_mla_w_ukv, m_mla_w_o, m_mlp_w1, m_mlp_w2, m_final_g, v_ada_w, v_ada_b, v_norm_mix_g, v_norm_mlp_g, v_pool_w, v_pool_scale, v_sgu_w_in, v_sgu_ln_g, v_sgu_ln_b, v_sgu_w_s, v_sgu_b_s, v_sgu_w_out, v_mla_w_dq_dkv, v_mla_q_norm_g, v_mla_kv_norm_g, v_mla_w_uq, v_mla_w_ukv, v_mla_w_o, v_mlp_w1, v_mlp_w2, v_final_g):
    a = dict(locals())
    S, D = x.shape[1], x.shape[2]
    L = ada_w.shape[0]
    Wc = ada_w.shape[2]
    me = 4 * lax.axis_index("x") + 2 * lax.axis_index("y") + lax.axis_index("c")
    my_chip = 2 * lax.axis_index("x") + lax.axis_index("y")

    v0 = _pack([c, pool_scale, mla_q_norm_g])
    g0 = _small_all_gather(v0, "gather_c").reshape(NDEV, -1)
    n_ps, n_qg = pool_scale.size, mla_q_norm_g.size
    c_all = g0[:, :D]
    ps_w = pool_scale.shape[1]
    ps_full = g0[:, D:D + n_ps].reshape(NDEV, -1, ps_w).transpose(1, 0, 2).reshape(-1, 1, D)
    qg_full = g0[:, D + n_ps:D + n_ps + n_qg].reshape(1, -1)

    ada_b_mine = lax.dynamic_slice_in_dim(ada_b, me * Wc, Wc, axis=1)
    modp = _ada_fwd(c_all, ada_w, ada_b_mine, "ada_fwd")
    ga = _small_all_gather(modp.reshape(-1, LANES), "gather_mod").reshape(NDEV, L, NDEV, Wc)
    mod = lax.dynamic_index_in_dim(ga, me, axis=2, keepdims=False).transpose(1, 0, 2).reshape(L, 6, D)
    mod8 = jnp.pad(mod, ((0, 0), (0, 2), (0, 0)))

    small = {"pool_scale": ps_full, "sgu_ln_g": sgu_ln_g, "sgu_ln_b": sgu_ln_b, "sgu_w_s": sgu_w_s[0],
             "sgu_b_s_t": sgu_b_s[0].T, "mla_q_norm_g": qg_full, "mla_kv_norm_g": mla_kv_norm_g}
    gmix, gmlp = norm_mix_g.reshape(L, 1, D), norm_mlp_g.reshape(L, 1, D)
    tables = _rope_tables(positions, S)

    def shards_of(mats):
        return [a[n][j].astype(BF16) for n, j, _ in mats], [ax for _, _, ax in mats]

    def as_weights(mats, fulls):
        w = {n: f for (n, _, _), f in zip(mats, fulls)}
        if "mla_w_uq" in w:
            lat_w = w["mla_w_dq_dkv"].shape[1]
            w["mla_wd"] = jnp.pad(w.pop("mla_w_dq_dkv"), ((0, 0), (0, -lat_w % LANES)))
            w["mla_wq"] = _pad_heads(w.pop("mla_w_uq"), Q_HEAD)
        return w

    xc = x[0]
    wts, saved, flying = [], [], None
    for i in range(L):
        mats = _layer_matrices(i)
        mod_i = mod8[i]
        if flying is None:
            now, late = mats[:-1], mats[-1:]
            shards, axes = shards_of(now)
            fulls = _all_gather_group(shards, axes, mod8, f"gather_w_{i}")
            w = as_weights(now, fulls)
            late_axes = shards_of(late)[1]
            late_fly = _gather_start(*shards_of(late), me, fulls[0], f"gather_start_{i}b")
            mod_i = mod_i + late_fly[4][0, 0]
            w[late[0][0]] = lambda after, fly=late_fly, axes=late_axes, i=i: _gather_pass_on(
                _gather_wait(*fly[:4], axes, after, f"gather_wait_{i}b"), axes, f"gather_pass_{i}b")[0]
            order = late_fly[4]
        else:
            axes = shards_of(mats)[1]
            fulls = _gather_wait(*flying[:4], axes, xc, f"gather_wait_{i}")
            fulls = _gather_pass_on(fulls, axes, f"gather_pass_{i}")
            w = as_weights(mats, fulls)
            order = fulls[0]
        wts.append(w)
        if i + 1 < L:
            flying = _gather_start(*shards_of(_layer_matrices(i + 1)), me, order, f"gather_start_{i + 1}")
            mod_i = mod_i + flying[4][0, 0]
        xc, sv = _layer_forward(i, xc, mod_i, gmix[i], gmlp[i], wts[i], small, tables)
        saved.append(sv)
    dx, loss_acc = _loss_head(xc, loss_target[0], final_g.reshape(1, D), _tiles(S)[0], "loss_head")

    res = {n: [None] * a[n].shape[0] for n in WEIGHT_NAMES if a[n].ndim > 1}
    c_me = lax.axis_index("c")

    def start_reduce(i, mats, g, after, tag=""):
        g = dict(g)
        if "mla_wq" in g:
            g["mla_w_dq_dkv"] = g.pop("mla_wd")[:, :mla_w_dq_dkv.shape[2]]
            g["mla_w_uq"] = _unpad_heads(g.pop("mla_wq"), Q_HEAD)
        gl = []
        for n, j, ax in mats:
            gm, blk = g[n].astype(BF16), a[n][j].shape
            if gm.shape != (NDEV,) + blk:
                gm = jnp.moveaxis(gm.reshape(blk[:ax] + (NDEV,) + blk[ax:]), ax, 0)
            gl.append(gm)
        if i >= DIRECT_FROM:
            return _direct_exchange_start(gl, after, f"rs_direct_start_{i}") + (mats, f"{i}", True)
        lands = _reduce_scatter_sibling(gl, f"rs_sibling_{i}{tag}")
        parts = [_add_pairs(gm, l, c_me, f"rs_add_{i}_{n}") for gm, l, (n, _, _) in zip(gl, lands, mats)]
        return _chips_exchange_start(parts, after, f"rs_chips_start_{i}{tag}") + (mats, f"{i}{tag}", False)

    def finish_reduce(fly, after):
        if fly[7]:
            parts, recv = _direct_exchange_wait(*fly[:4], after, f"rs_direct_wait_{fly[6]}")
            mine = me
        else:
            parts, recv = _chips_exchange_wait(*fly[:4], after, f"rs_chips_wait_{fly[6]}")
            mine = my_chip
        for (n, j, _), p, r in zip(fly[5], parts, recv):
            res[n][j] = _adamw([(p, mine), r], a[n][j], a["m_" + n][j], a["v_" + n][j], f"adamw_{n}_{j}")

    def update_replicated(gathered, names, tail, label):
        zeros_tail = [jnp.zeros_like(t) for t in tail]
        packs = _adamw([gathered], _pack([a[n] for n in names] + zeros_tail), _pack([a["m_" + n] for n in names] + zeros_tail),
                       _pack([a["v_" + n] for n in names] + zeros_tail), label)
        flat = [t.reshape(-1) for t in packs]
        off = 0
        for n in names:
            res[n] = [f[off:off + a[n].size].reshape(a[n].shape) for f in flat]
            off += a[n].size
        sums = []
        for t in tail:
            sums.append(flat[0][off:off + t.size].reshape(t.shape))
            off += t.size
        return sums, packs[0]

    stats, sgrads, flying, early, last_mlp = [None] * L, {}, None, None, []
    for i in reversed(range(L)):
        mats = _layer_matrices(i)
        mod_i = mod8[i]
        if flying is not None:
            mod_i = mod_i + flying[4][0, 0]
        if early is not None:
            mod_i = mod_i + early[4][0, 0]
        hook = None
        if i == 0:
            def hook(g_mlp, mats=mats, dx=dx):
                last_mlp.append(start_reduce(0, mats[-2:], g_mlp, dx, "a"))
                return last_mlp[0][4][0, 0]
            mats = mats[:-2]
        dx, stats[i], g, sgr = _layer_backward(i, dx, saved[i], mod_i, gmix[i], gmlp[i], wts[i], small, tables, hook)
        sgrads.update(sgr)
        if flying is not None:
            finish_reduce(flying, dx)
        flying = start_reduce(i, mats, g, dx)
        if i == 1:
            early_tail = [sgrads["mla_q_norm_g"]]
            early = _small_gather_start(_pack([sgrads[n] for n in REPLICATED_EARLY] + early_tail), me, flying[4],
                                        "gather_small_start")
    (g_qg,), _ = update_replicated(_small_gather_wait(*early[:4], dx, "gather_small_wait"), REPLICATED_EARLY, early_tail,
                                   "adamw_replicated_early")

    sg = {"ada_b": jnp.stack([s[0:6] for s in stats]), "norm_mix_g": jnp.stack([s[6] for s in stats]),
          "norm_mlp_g": jnp.stack([s[7] for s in stats]), "final_g": loss_acc[0]}
    ps_grad = jnp.concatenate([sgrads[f"pool_scale_{j}"] for j in range(pool_scale.shape[0])])
    tail = [ps_grad, loss_acc[2, :LANES]]
    packed = _pack([sg[n] for n in REPLICATED] + tail) + flying[4][0, 0]
    gathered = _small_all_gather(packed, "gather_small")
    (g_ps, loss_lanes), g_p = update_replicated(gathered, REPLICATED, tail, "adamw_replicated")
    finish_reduce(last_mlp[0], g_p)
    finish_reduce(flying, g_p)
    loss = loss_lanes[0]
    res["pool_scale"] = _adamw([lax.dynamic_slice_in_dim(g_ps, me * ps_w, ps_w, axis=1)], pool_scale, m_pool_scale,
                               v_pool_scale, "adamw_pool_scale")
    qg_w = mla_q_norm_g.shape[1]
    res["mla_q_norm_g"] = _adamw([lax.dynamic_slice_in_dim(g_qg, me * qg_w, qg_w, axis=1)], mla_q_norm_g, m_mla_q_norm_g,
                                 v_mla_q_norm_g, "adamw_q_norm_g")

    n_mod = L * 6 * D
    dmod_all = gathered.reshape(NDEV, -1)[:, :n_mod].reshape(NDEV, L, 6 * D)
    dmod_mine = lax.dynamic_slice_in_dim(dmod_all, me * Wc, Wc, axis=2).transpose(1, 0, 2)
    res["ada_w"] = _adamw([_ada_bwd(c_all, dmod_mine, "ada_bwd")], ada_w, m_ada_w, v_ada_w, "adamw_ada_w")

    outs = []
    for k in range(4):
        for n in WEIGHT_NAMES:
            r = res[n]
            outs.append(jnp.stack([lay[k] for lay in r]) if isinstance(r[0], list) else r[k])
    return (loss, dx.reshape(x.shape), *outs)
```

```python
import functools
import math

import jax
import jax.numpy as jnp
import numpy as np
from jax import lax
from jax.experimental import pallas as pl
from jax.experimental.pallas import tpu as pltpu

F32 = jnp.float32
BF16 = jnp.bfloat16
SDS = jax.ShapeDtypeStruct
MESH = pl.DeviceIdType.MESH

NDEV = 8
V7X_VMEM_LIMIT = 56 << 20
LANES = 128
RMS_EPS = 1e-6
LN_EPS = 1e-5
POOL_WINDOWS = (2, 4, 8, 16)
HALO = 16
SGU_CHUNK = 128
SGU_HEAD = 128
MLA_NOPE, MLA_ROPE, MLA_V = 128, 64, 128
MLA_Q_LORA, MLA_KV_LORA = 256, 128
MLA_HEAD_PAD = 256
ROPE_THETA = 10000.0
SM_SCALE = (MLA_NOPE + MLA_ROPE) ** -0.5
NEG = -1e30
ADAM_LR, ADAM_B1, ADAM_B2, ADAM_EPS, ADAM_WD, ADAM_STEP = 0.001, 0.9, 0.999, 1e-08, 0.01, 10
INV_SQRT2 = 1.0 / math.sqrt(2.0)
INV_SQRT_2PI = 1.0 / math.sqrt(2.0 * math.pi)
SH1, SC1, G1, SH2, SC2, G2 = 0, 1, 2, 3, 4, 5


def _params(sem=None, vmem=V7X_VMEM_LIMIT):
    return pltpu.CompilerParams(dimension_semantics=sem, vmem_limit_bytes=vmem)


def _resident(shape):
    nd = len(shape)
    return pl.BlockSpec(shape, lambda *_: (0,) * nd, pipeline_mode=pl.Buffered(1))


def _rows1(tm, w):
    return pl.BlockSpec((tm, w), lambda i: (i, 0))


def _rms(x):
    r = lax.rsqrt(jnp.mean(x * x, axis=-1, keepdims=True) + RMS_EPS)
    return x * r, r


def _colsum(v):
    return jnp.sum(v, axis=0, keepdims=True)


def _normmod_bwd(dh, n, r, a):
    dn = dh * a
    return r * (dn - n * jnp.mean(dn * n, axis=-1, keepdims=True))


def _dot(a, b):
    return jnp.dot(a, b, preferred_element_type=F32)


def _dot_nt(a, b):
    return lax.dot_general(a, b, (((1,), (1,)), ((), ())), preferred_element_type=F32)


def _dot_tn(a, b):
    return lax.dot_general(a, b, (((0,), (0,)), ((), ())), preferred_element_type=F32)


def _gelu(x):
    return 0.5 * x * (1.0 + lax.erf(x * INV_SQRT2))


def _gelu_grad(x):
    return 0.5 * (1.0 + lax.erf(x * INV_SQRT2)) + x * jnp.exp(-0.5 * x * x) * INV_SQRT_2PI


def _swap_halves(v):
    lane = lax.broadcasted_iota(jnp.int32, v.shape, 1)
    half = MLA_ROPE // 2
    return jnp.where(lane < half, pltpu.roll(v, LANES - half, 1),
                     jnp.where(lane < MLA_ROPE, pltpu.roll(v, half, 1), 0.0))


def _mlp_up(x1, mod, gn, w1, tm, tn, name):
    S, D = x1.shape
    Fh = w1.shape[1]

    def body(x_ref, mod_ref, gn_ref, w_ref, h_ref, r_ref):
        n, _ = _rms(x_ref[...])
        a = gn_ref[...] * (1.0 + mod_ref[SC2:SC2 + 1, :])
        h = (n * a + mod_ref[SH2:SH2 + 1, :]).astype(BF16)
        h_ref[...] = h
        for j in range(Fh // tn):
            cols = slice(j * tn, (j + 1) * tn)
            r_ref[:, cols] = jnp.maximum(_dot(h, w_ref[:, cols]), 0.0).astype(BF16)

    return pl.pallas_call(
        body, name=name, grid=(S // tm,),
        in_specs=[_rows1(tm, D), _resident(mod.shape), _resident(gn.shape), _resident(w1.shape)],
        out_specs=[_rows1(tm, D), _rows1(tm, Fh)],
        out_shape=[SDS((S, D), BF16), SDS((S, Fh), BF16)],
        compiler_params=_params(("parallel",)),
    )(x1, mod, gn, w1)


def _mlp_down(r, w2, x1, mod, tm, name):
    S, Fh = r.shape
    D = w2.shape[1]

    def body(r_ref, w_ref, x_ref, mod_ref, x2_ref, o_ref):
        rv = r_ref[...]
        o = _dot(rv * rv, w_ref[...])
        o_ref[...] = o.astype(BF16)
        x2_ref[...] = x_ref[...] + mod_ref[G2:G2 + 1, :] * o

    return pl.pallas_call(
        body, name=name, grid=(S // tm,),
        in_specs=[_rows1(tm, Fh), _resident(w2.shape), _rows1(tm, D), _resident(mod.shape)],
        out_specs=[_rows1(tm, D), _rows1(tm, D)],
        out_shape=[SDS((S, D), F32), SDS((S, D), BF16)],
        compiler_params=_params(("parallel",)),
    )(r, w2, x1, mod)


def _mlp_bwd_a(dx2, o, mod, w2, r, tm, tn, name):
    S, D = dx2.shape
    Fh = r.shape[1]

    def body(dx_ref, o_ref, mod_ref, w_ref, r_ref, da_ref, do_ref, st_ref):
        @pl.when(pl.program_id(0) == 0)
        def _():
            st_ref[...] = jnp.zeros_like(st_ref)

        dx = dx_ref[...]
        d_o = (dx * mod_ref[G2:G2 + 1, :]).astype(BF16)
        do_ref[...] = d_o
        st_ref[G2:G2 + 1, :] += _colsum(dx * o_ref[...].astype(F32))
        for j in range(Fh // tn):
            cols = slice(j * tn, (j + 1) * tn)
            dz = _dot_nt(d_o, w_ref[cols, :])
            da_ref[:, cols] = (dz * (2.0 * r_ref[:, cols].astype(F32))).astype(BF16)

    return pl.pallas_call(
        body, name=name, grid=(S // tm,),
        in_specs=[_rows1(tm, D), _rows1(tm, D), _resident(mod.shape), _resident(w2.shape), _rows1(tm, Fh)],
        out_specs=[_rows1(tm, Fh), _rows1(tm, D), pl.BlockSpec((8, D), lambda i: (0, 0))],
        out_shape=[SDS((S, Fh), BF16), SDS((S, D), BF16), SDS((8, D), F32)],
        compiler_params=_params(("arbitrary",)),
    )(dx2, o, mod, w2, r)


def _mlp_bwd_b(d_a, w1, x1, dx2, mod, gn, tm, name):
    S, Fh = d_a.shape
    D = w1.shape[0]

    def body(da_ref, w_ref, x_ref, dx_ref, mod_ref, gn_ref, dx1_ref, st_ref):
        @pl.when(pl.program_id(0) == 0)
        def _():
            st_ref[...] = jnp.zeros_like(st_ref)

        dh = _dot_nt(da_ref[...], w_ref[...])
        n, rr = _rms(x_ref[...])
        gn_v = gn_ref[...]
        sc1p = 1.0 + mod_ref[SC2:SC2 + 1, :]
        t = _colsum(dh * n)
        st_ref[SH2:SH2 + 1, :] += _colsum(dh)
        st_ref[SC2:SC2 + 1, :] += t * gn_v
        st_ref[6:7, :] += t * sc1p
        dx1_ref[...] = dx_ref[...] + _normmod_bwd(dh, n, rr, gn_v * sc1p)

    return pl.pallas_call(
        body, name=name, grid=(S // tm,),
        in_specs=[_rows1(tm, Fh), _resident(w1.shape), _rows1(tm, D), _rows1(tm, D), _resident(mod.shape),
                  _resident(gn.shape)],
        out_specs=[_rows1(tm, D), pl.BlockSpec((8, D), lambda i: (0, 0))],
        out_shape=[SDS((S, D), F32), SDS((8, D), F32)],
        compiler_params=_params(("arbitrary",)),
    )(d_a, w1, x1, dx2, mod, gn)


def _mm_tn(a, g, tk, tn, name, square_a=False, col_shards=False):
    S, K1 = a.shape
    N = g.shape[1]
    w = N // NDEV
    per = tn // w if col_shards else 1

    def body(a_ref, g_ref, o_ref):
        av = a_ref[...]
        if square_a:
            av = av * av
        res = _dot_tn(av, g_ref[...]).astype(BF16)
        if col_shards:
            for s in range(per):
                o_ref[s] = res[:, s * w:(s + 1) * w]
        else:
            o_ref[...] = res

    if col_shards:
        out_spec, out_shape = pl.BlockSpec((per, tk, w), lambda i, j: (j, i, 0)), SDS((NDEV, K1, w), BF16)
    else:
        out_spec, out_shape = pl.BlockSpec((tk, tn), lambda i, j: (i, j)), SDS((K1, N), BF16)
    return pl.pallas_call(
        body, name=name, grid=(K1 // tk, N // tn),
        in_specs=[pl.BlockSpec((S, tk), lambda i, j: (0, i)), pl.BlockSpec((S, tn), lambda i, j: (0, j))],
        out_specs=out_spec, out_shape=out_shape,
        compiler_params=_params(("parallel", "parallel")),
    )(a, g)


def _pool_h_ext(x_ref, xp_ref, mod_ref, gn_ref, i, tm):
    ext = jnp.concatenate([xp_ref[...], x_ref[...]], axis=0)
    n, r = _rms(ext)
    a = gn_ref[...] * (1.0 + mod_ref[SC1:SC1 + 1, :])
    h = n * a + mod_ref[SH1:SH1 + 1, :]
    row = lax.broadcasted_iota(jnp.int32, (tm + HALO, 1), 0)
    h = jnp.where(jnp.logical_and(i == 0, row < HALO), 0.0, h)
    return h, n[HALO:], r[HALO:], a


def _trailing_sum(v, win):
    k = 1
    while k < win:
        v = v + pltpu.roll(v, k, 0)
        k *= 2
    return v


def _leading_sum(v, win):
    k = 1
    while k < win:
        v = v + pltpu.roll(v, v.shape[0] - k, 0)
        k *= 2
    return v


def _pool_fwd(x, mod, gn, pw, ps, tm, name):
    S, D = x.shape
    C = D // len(POOL_WINDOWS)
    hb = tm // HALO

    def body(x_ref, xp_ref, mod_ref, gn_ref, pw_ref, ps_ref, x1_ref):
        i = pl.program_id(0)
        h, _, _, _ = _pool_h_ext(x_ref, xp_ref, mod_ref, gn_ref, i, tm)
        t1 = (i * tm + lax.broadcasted_iota(jnp.int32, (tm, 1), 0)).astype(F32) + 1.0
        for g, win in enumerate(POOL_WINDOWS):
            cols = slice(g * C, (g + 1) * C)
            hg = h[:, cols]
            inv = 1.0 / jnp.minimum(t1, float(win))
            pooled = (_trailing_sum(hg, win)[HALO:] * inv - hg[HALO:]).astype(BF16)
            y = _dot(pooled, pw_ref[g]) * ps_ref[:, cols]
            x1_ref[:, cols] = x_ref[:, cols] + mod_ref[G1:G1 + 1, cols] * y

    return pl.pallas_call(
        body, name=name, grid=(S // tm,),
        in_specs=[_rows1(tm, D), pl.BlockSpec((HALO, D), lambda i: (jnp.maximum(i * hb - 1, 0), 0)),
                  _resident(mod.shape), _resident(gn.shape), _resident(pw.shape), _resident(ps.shape)],
        out_specs=_rows1(tm, D),
        out_shape=SDS((S, D), F32),
        compiler_params=_params(("parallel",)),
    )(x, x, mod, gn, pw, ps)


def _pool_bwd(x, dx1, mod, gn, pw, ps, tm, name):
    S, D = x.shape
    G = len(POOL_WINDOWS)
    C = D // G
    hb = tm // HALO
    nt = S // tm

    def body(x_ref, xp_ref, d1_ref, dn_ref, mod_ref, gn_ref, pw_ref, ps_ref, dx_ref, st_ref, dpw_ref):
        i = pl.program_id(0)

        @pl.when(i == 0)
        def _():
            st_ref[...] = jnp.zeros_like(st_ref)
            dpw_ref[...] = jnp.zeros_like(dpw_ref)

        h, n, rr, a = _pool_h_ext(x_ref, xp_ref, mod_ref, gn_ref, i, tm)
        g1 = mod_ref[G1:G1 + 1, :]
        ps_v = ps_ref[...]
        d1 = d1_ref[...]
        d1n = jnp.where(i == nt - 1, 0.0, dn_ref[...])
        dyr = (jnp.concatenate([d1, d1n], axis=0) * (g1 * ps_v)).astype(BF16)
        t1 = (i * tm + lax.broadcasted_iota(jnp.int32, (tm + HALO, 1), 0)).astype(F32) + 1.0
        parts = []
        for g, win in enumerate(POOL_WINDOWS):
            cols = slice(g * C, (g + 1) * C)
            hg = h[:, cols]
            inv = 1.0 / jnp.minimum(t1, float(win))
            pooled = (_trailing_sum(hg, win)[HALO:] * inv[:tm] - hg[HALO:]).astype(BF16)
            yraw = _dot(pooled, pw_ref[g])
            st_ref[G1:G1 + 1, cols] += _colsum(d1[:, cols] * (yraw * ps_v[:, cols]))
            st_ref[4:5, cols] += _colsum(d1[:, cols] * g1[:, cols] * yraw)
            dpw_ref[g] += _dot_tn(pooled, dyr[:tm, cols])
            dpool = _dot_nt(dyr[:, cols], pw_ref[g])
            parts.append(_leading_sum(dpool * inv, win)[:tm] - dpool[:tm])
        dh = jnp.concatenate(parts, axis=1)
        t = _colsum(dh * n)
        st_ref[SH1:SH1 + 1, :] += _colsum(dh)
        st_ref[SC1:SC1 + 1, :] += t * gn_ref[...]
        st_ref[3:4, :] += t * (1.0 + mod_ref[SC1:SC1 + 1, :])
        dx_ref[...] = d1 + _normmod_bwd(dh, n, rr, a)

    return pl.pallas_call(
        body, name=name, grid=(nt,),
        in_specs=[_rows1(tm, D), pl.BlockSpec((HALO, D), lambda i: (jnp.maximum(i * hb - 1, 0), 0)),
                  _rows1(tm, D), pl.BlockSpec((HALO, D), lambda i: (jnp.minimum((i + 1) * hb, S // HALO - 1), 0)),
                  _resident(mod.shape), _resident(gn.shape), _resident(pw.shape), _resident(ps.shape)],
        out_specs=[_rows1(tm, D), pl.BlockSpec((8, D), lambda i: (0, 0)), pl.BlockSpec((G, C, C), lambda i: (0, 0, 0))],
        out_shape=[SDS((S, D), F32), SDS((8, D), F32), SDS((G, C, C), F32)],
        compiler_params=_params(("arbitrary",)),
    )(x, x, dx1, dx1, mod, gn, pw, ps)


def _tril_bf16(w):
    row = lax.broadcasted_iota(jnp.int32, w.shape, 0)
    col = lax.broadcasted_iota(jnp.int32, w.shape, 1)
    return jnp.where(col <= row, w, 0.0).astype(BF16)


def _sgu_front(pre, lng_ref, lnb_ref, W):
    z = _gelu(pre)
    u, v = z[:, :W], z[:, W:]
    mu = jnp.mean(v, axis=-1, keepdims=True)
    xc = v - mu
    rstd = lax.rsqrt(jnp.mean(xc * xc, axis=-1, keepdims=True) + LN_EPS)
    vhat = xc * rstd
    return u, vhat, rstd, vhat * lng_ref[...] + lnb_ref[...]


def _sgu_mix(vn, ws_ref, bst_ref, mix_s, tm, W):
    for hd in range(W // SGU_HEAD):
        wm = _tril_bf16(ws_ref[hd])
        bcol = bst_ref[:, hd:hd + 1]
        for ci in range(tm // SGU_CHUNK):
            rs, cs = slice(ci * SGU_CHUNK, (ci + 1) * SGU_CHUNK), slice(hd * SGU_HEAD, (hd + 1) * SGU_HEAD)
            mix_s[rs, cs] = _dot(wm, vn[rs, cs].astype(BF16)) + bcol


def _sgu_fwd(x, mod, gn, w_in, lng, lnb, ws, bst, w_out, tm, name):
    S, D = x.shape
    W = w_out.shape[0]

    def body(x_ref, mod_ref, gn_ref, win_ref, lng_ref, lnb_ref, ws_ref, bst_ref, wout_ref,
             x1_ref, h_ref, pre_ref, y_ref, mix_s):
        n, _ = _rms(x_ref[...])
        a = gn_ref[...] * (1.0 + mod_ref[SC1:SC1 + 1, :])
        h = (n * a + mod_ref[SH1:SH1 + 1, :]).astype(BF16)
        h_ref[...] = h
        pre = _dot(h, win_ref[...])
        pre_ref[...] = pre.astype(BF16)
        u, _, _, vn = _sgu_front(pre, lng_ref, lnb_ref, W)
        _sgu_mix(vn, ws_ref, bst_ref, mix_s, tm, W)
        y = _dot((u * mix_s[...]).astype(BF16), wout_ref[...])
        y_ref[...] = y.astype(BF16)
        x1_ref[...] = x_ref[...] + mod_ref[G1:G1 + 1, :] * y

    return pl.pallas_call(
        body, name=name, grid=(S // tm,),
        in_specs=[_rows1(tm, D), _resident(mod.shape), _resident(gn.shape), _resident(w_in.shape),
                  _resident(lng.shape), _resident(lnb.shape), _resident(ws.shape), _resident(bst.shape),
                  _resident(w_out.shape)],
        out_specs=[_rows1(tm, D), _rows1(tm, D), _rows1(tm, 2 * W), _rows1(tm, D)],
        out_shape=[SDS((S, D), F32), SDS((S, D), BF16), SDS((S, 2 * W), BF16), SDS((S, D), BF16)],
        scratch_shapes=[pltpu.VMEM((tm, W), F32)],
        compiler_params=_params(("parallel",)),
    )(x, mod, gn, w_in, lng, lnb, ws, bst, w_out)


def _sgu_bwd(x, dx1, pre, y, mod, gn, w_in, lng, lnb, ws, bst, w_out, tm, name):
    S, D = x.shape
    W = w_out.shape[0]
    H = W // SGU_HEAD
    nt = S // tm

    def body(x_ref, d1_ref, pre_ref, y_ref, mod_ref, gn_ref, win_ref, lng_ref, lnb_ref, ws_ref, bst_ref, wout_ref,
             dx_ref, dy_ref, gt_ref, dpre_ref, st_ref, dws_ref, dbs_ref, mix_s, dvn_s):
        i = pl.program_id(0)

        @pl.when(i == 0)
        def _():
            st_ref[...] = jnp.zeros_like(st_ref)
            dws_ref[...] = jnp.zeros_like(dws_ref)
            dbs_ref[...] = jnp.zeros_like(dbs_ref)

        d1 = d1_ref[...]
        pre = pre_ref[...].astype(F32)
        u, vhat, rstd, vn = _sgu_front(pre, lng_ref, lnb_ref, W)
        _sgu_mix(vn, ws_ref, bst_ref, mix_s, tm, W)
        mixed = mix_s[...]
        gt_ref[...] = (u * mixed).astype(BF16)
        dyb = (d1 * mod_ref[G1:G1 + 1, :]).astype(BF16)
        dy_ref[...] = dyb
        st_ref[G1:G1 + 1, :] += _colsum(d1 * y_ref[...].astype(F32))
        dgt = _dot_nt(dyb, wout_ref[...])
        du = dgt * mixed
        dmix = dgt * u
        for hd in range(H):
            wm = _tril_bf16(ws_ref[hd])
            for ci in range(tm // SGU_CHUNK):
                rs, cs = slice(ci * SGU_CHUNK, (ci + 1) * SGU_CHUNK), slice(hd * SGU_HEAD, (hd + 1) * SGU_HEAD)
                dm = dmix[rs, cs]
                dmb = dm.astype(BF16)
                dbs_ref[hd] += jnp.broadcast_to(jnp.sum(dm, axis=1, keepdims=True), (SGU_CHUNK, LANES))
                dws_ref[hd] += _dot_nt(dmb, vn[rs, cs].astype(BF16))
                dvn_s[rs, cs] = _dot_tn(wm, dmb)
        dvn = dvn_s[...]
        st_ref[4:5, :] += _colsum(dvn * vhat)
        st_ref[5:6, :] += _colsum(dvn)
        dvh = dvn * lng_ref[...]
        dv = rstd * (dvh - jnp.mean(dvh, axis=-1, keepdims=True) - vhat * jnp.mean(dvh * vhat, axis=-1, keepdims=True))
        dpre_u = (du * _gelu_grad(pre[:, :W])).astype(BF16)
        dpre_v = (dv * _gelu_grad(pre[:, W:])).astype(BF16)
        dpre_ref[:, :W] = dpre_u
        dpre_ref[:, W:] = dpre_v
        dh = _dot_nt(dpre_u, win_ref[:, :W]) + _dot_nt(dpre_v, win_ref[:, W:])
        n, rr = _rms(x_ref[...])
        gn_v = gn_ref[...]
        sc1p = 1.0 + mod_ref[SC1:SC1 + 1, :]
        t = _colsum(dh * n)
        st_ref[SH1:SH1 + 1, :] += _colsum(dh)
        st_ref[SC1:SC1 + 1, :] += t * gn_v
        st_ref[3:4, :] += t * sc1p
        dx_ref[...] = d1 + _normmod_bwd(dh, n, rr, gn_v * sc1p)

        @pl.when(i == nt - 1)
        def _():
            for hd in range(H):
                row = lax.broadcasted_iota(jnp.int32, (SGU_CHUNK, SGU_CHUNK), 0)
                col = lax.broadcasted_iota(jnp.int32, (SGU_CHUNK, SGU_CHUNK), 1)
                dws_ref[hd] = jnp.where(col <= row, dws_ref[hd], 0.0)

    return pl.pallas_call(
        body, name=name, grid=(nt,),
        in_specs=[_rows1(tm, D), _rows1(tm, D), _rows1(tm, 2 * W), _rows1(tm, D), _resident(mod.shape),
                  _resident(gn.shape), _resident(w_in.shape), _resident(lng.shape), _resident(lnb.shape),
                  _resident(ws.shape), _resident(bst.shape), _resident(w_out.shape)],
        out_specs=[_rows1(tm, D), _rows1(tm, D), _rows1(tm, W), _rows1(tm, 2 * W),
                   pl.BlockSpec((8, D), lambda i: (0, 0)), pl.BlockSpec((H, SGU_CHUNK, SGU_CHUNK), lambda i: (0, 0, 0)),
                   pl.BlockSpec((H, SGU_CHUNK, LANES), lambda i: (0, 0, 0))],
        out_shape=[SDS((S, D), F32), SDS((S, D), BF16), SDS((S, W), BF16), SDS((S, 2 * W), BF16),
                   SDS((8, D), F32), SDS((H, SGU_CHUNK, SGU_CHUNK), F32), SDS((H, SGU_CHUNK, LANES), F32)],
        scratch_shapes=[pltpu.VMEM((tm, W), F32), pltpu.VMEM((tm, W), F32)],
        compiler_params=_params(("arbitrary",)),
    )(x, dx1, pre, y, mod, gn, w_in, lng, lnb, ws, bst, w_out)


def _mla_lat(x, mod, gn, wd, qg, kvg, cos_t, sin_t, tm, name):
    S, D = x.shape
    LW = wd.shape[1]
    QL, KL = MLA_Q_LORA, MLA_KV_LORA

    def body(x_ref, mod_ref, gn_ref, wd_ref, qg_ref, kvg_ref, c_ref, s_ref, h_ref, lat_ref, cq_ref, ckv_ref, kr_ref):
        n, _ = _rms(x_ref[...])
        a = gn_ref[...] * (1.0 + mod_ref[SC1:SC1 + 1, :])
        h = (n * a + mod_ref[SH1:SH1 + 1, :]).astype(BF16)
        h_ref[...] = h
        lat = _dot(h, wd_ref[...])
        lat_ref[...] = lat
        nq, _ = _rms(lat[:, :QL])
        cq_ref[...] = (nq * qg_ref[...]).astype(BF16)
        nkv, _ = _rms(lat[:, QL:QL + KL])
        ckv_ref[...] = (nkv * kvg_ref[...]).astype(BF16)
        kr = lat[:, QL + KL:]
        kr_ref[...] = (kr * c_ref[...] + _swap_halves(kr) * s_ref[...]).astype(BF16)

    return pl.pallas_call(
        body, name=name, grid=(S // tm,),
        in_specs=[_rows1(tm, D), _resident(mod.shape), _resident(gn.shape), _resident(wd.shape), _resident(qg.shape),
                  _resident(kvg.shape), _rows1(tm, LANES), _rows1(tm, LANES)],
        out_specs=[_rows1(tm, D), _rows1(tm, LW), _rows1(tm, QL), _rows1(tm, KL), _rows1(tm, LANES)],
        out_shape=[SDS((S, D), BF16), SDS((S, LW), F32), SDS((S, QL), BF16), SDS((S, KL), BF16), SDS((S, LANES), BF16)],
        compiler_params=_params(("parallel",)),
    )(x, mod, gn, wd, qg, kvg, cos_t, sin_t)


def _mla_qkv(cq, ckv, krp, wq, wukv, cos_t, sin_t, tm, name):
    S = cq.shape[0]
    H = wq.shape[1] // MLA_HEAD_PAD
    HP = MLA_HEAD_PAD

    def body(cq_ref, ckv_ref, kr_ref, wq_ref, wkv_ref, c_ref, s_ref, q_ref, k_ref, v_ref):
        q = _dot(cq_ref[...], wq_ref[...])
        kv = _dot(ckv_ref[...], wkv_ref[...])
        cv, sv, krv = c_ref[...], s_ref[...], kr_ref[...]
        for h in range(H):
            qr = q[:, h * HP + MLA_NOPE:(h + 1) * HP]
            q_ref[:, h * HP:h * HP + MLA_NOPE] = (q[:, h * HP:h * HP + MLA_NOPE] * SM_SCALE).astype(BF16)
            q_ref[:, h * HP + MLA_NOPE:(h + 1) * HP] = ((qr * cv + _swap_halves(qr) * sv) * SM_SCALE).astype(BF16)
            k_ref[:, h * HP:h * HP + MLA_NOPE] = kv[:, h * HP:h * HP + MLA_NOPE].astype(BF16)
            k_ref[:, h * HP + MLA_NOPE:(h + 1) * HP] = krv
            v_ref[:, h * MLA_V:(h + 1) * MLA_V] = kv[:, h * HP + MLA_NOPE:(h + 1) * HP].astype(BF16)

    return pl.pallas_call(
        body, name=name, grid=(S // tm,),
        in_specs=[_rows1(tm, MLA_Q_LORA), _rows1(tm, MLA_KV_LORA), _rows1(tm, LANES), _resident(wq.shape),
                  _resident(wukv.shape), _rows1(tm, LANES), _rows1(tm, LANES)],
        out_specs=[_rows1(tm, H * HP), _rows1(tm, H * HP), _rows1(tm, H * MLA_V)],
        out_shape=[SDS((S, H * HP), BF16), SDS((S, H * HP), BF16), SDS((S, H * MLA_V), BF16)],
        compiler_params=_params(("parallel",)),
    )(cq, ckv, krp, wq, wukv, cos_t, sin_t)


def _causal_mask(tq):
    row = lax.broadcasted_iota(jnp.int32, (tq, tq), 0)
    col = lax.broadcasted_iota(jnp.int32, (tq, tq), 1)
    return col <= row


def _tile_rows(j, tq):
    return slice(j * tq, (j + 1) * tq) if isinstance(j, int) else pl.ds(pl.multiple_of(j * tq, tq), tq)


def _attn_fwd(q, k, v, tq, name):
    S = q.shape[0]
    HP = MLA_HEAD_PAD
    H = q.shape[1] // HP
    nq = S // tq

    def body(q_ref, k_ref, v_ref, o_ref, lse_ref):
        for i in range(nq):
            rows = slice(i * tq, (i + 1) * tq)
            qv = q_ref[rows, :]

            def step(j, carry, masked, qv=qv):
                m, l, acc = carry
                krows = _tile_rows(j, tq)
                s = _dot_nt(qv, k_ref[krows, :])
                if masked:
                    s = jnp.where(_causal_mask(tq), s, NEG)
                m_new = jnp.maximum(m, jnp.max(s, axis=1, keepdims=True))
                p = jnp.exp(s - m_new)
                alpha = jnp.exp(m - m_new)
                l = alpha * l + jnp.sum(p, axis=1, keepdims=True)
                acc = alpha * acc + _dot(p.astype(BF16), v_ref[krows, :])
                return m_new, l, acc

            carry = (jnp.full((tq, 1), NEG, F32), jnp.zeros((tq, 1), F32), jnp.zeros((tq, MLA_V), F32))
            for j in range(i):
                carry = step(j, carry, False)
            m, l, acc = step(i, carry, True)
            o_ref[rows, :] = (acc / l).astype(BF16)
            lse_ref[0, rows, :] = jnp.broadcast_to(m + jnp.log(l), (tq, LANES))

    return pl.pallas_call(
        body, name=name, grid=(H,),
        in_specs=[pl.BlockSpec((S, HP), lambda h: (0, h)), pl.BlockSpec((S, HP), lambda h: (0, h)),
                  pl.BlockSpec((S, MLA_V), lambda h: (0, h))],
        out_specs=[pl.BlockSpec((S, MLA_V), lambda h: (0, h)), pl.BlockSpec((1, S, LANES), lambda h: (h, 0, 0))],
        out_shape=[SDS((S, H * MLA_V), BF16), SDS((H, S, LANES), F32)],
        compiler_params=_params(("parallel",)),
    )(q, k, v)


def _attn_bwd(q, k, v, o, do, lse, cos_t, sin_t, tq, name):
    S = q.shape[0]
    HP = MLA_HEAD_PAD
    H = q.shape[1] // HP
    nq = S // tq

    def body(q_ref, k_ref, v_ref, o_ref, do_ref, lse_ref, c_ref, s_ref, dq_ref, dkv_ref, dkr_ref, dq_acc, dl_s):
        @pl.when(pl.program_id(0) == 0)
        def _():
            dkr_ref[...] = jnp.zeros_like(dkr_ref)

        dq_acc[...] = jnp.zeros_like(dq_acc)

        def delta_tile(i, _):
            rows = pl.ds(pl.multiple_of(i * tq, tq), tq)
            d = jnp.sum(do_ref[rows, :].astype(F32) * o_ref[rows, :].astype(F32), axis=1, keepdims=True)
            dl_s[rows, :] = jnp.broadcast_to(d, (tq, LANES))
            return 0

        lax.fori_loop(0, nq, delta_tile, 0)

        for j in range(nq):
            krows = slice(j * tq, (j + 1) * tq)
            kv_k = k_ref[krows, :]
            kv_v = v_ref[krows, :]

            def step(i, carry, masked, kv_k=kv_k, kv_v=kv_v):
                dk, dv = carry
                rows = _tile_rows(i, tq)
                qv = q_ref[rows, :]
                dov = do_ref[rows, :]
                s = _dot_nt(qv, kv_k)
                if masked:
                    s = jnp.where(_causal_mask(tq), s, NEG)
                p = jnp.exp(s - lse_ref[0, rows, 0:1])
                dv = dv + _dot_tn(p.astype(BF16), dov)
                dp = _dot_nt(dov, kv_v)
                ds = (p * (dp - dl_s[rows, 0:1])).astype(BF16)
                dk = dk + _dot_tn(ds, qv)
                dq_acc[rows, :] += _dot(ds, kv_k)
                return dk, dv

            carry = step(j, (jnp.zeros((tq, HP), F32), jnp.zeros((tq, MLA_V), F32)), True)
            for i in range(j + 1, nq):
                carry = step(i, carry, False)
            dk, dv = carry
            dkv_ref[krows, :MLA_NOPE] = dk[:, :MLA_NOPE].astype(BF16)
            dkv_ref[krows, MLA_NOPE:] = dv.astype(BF16)
            dkr_ref[krows, :] += dk[:, MLA_NOPE:]

        def out_tile(i, _):
            rows = pl.ds(pl.multiple_of(i * tq, tq), tq)
            dq = dq_acc[rows, :] * SM_SCALE
            dqr = dq[:, MLA_NOPE:]
            dq_ref[rows, :MLA_NOPE] = dq[:, :MLA_NOPE].astype(BF16)
            dq_ref[rows, MLA_NOPE:] = (dqr * c_ref[rows, :] + _swap_halves(dqr * s_ref[rows, :])).astype(BF16)
            return 0

        lax.fori_loop(0, nq, out_tile, 0)

    return pl.pallas_call(
        body, name=name, grid=(H,),
        in_specs=[pl.BlockSpec((S, HP), lambda h: (0, h)), pl.BlockSpec((S, HP), lambda h: (0, h)),
                  pl.BlockSpec((S, MLA_V), lambda h: (0, h)), pl.BlockSpec((S, MLA_V), lambda h: (0, h)),
                  pl.BlockSpec((S, MLA_V), lambda h: (0, h)), pl.BlockSpec((1, S, LANES), lambda h: (h, 0, 0)),
                  _resident(cos_t.shape), _resident(sin_t.shape)],
        out_specs=[pl.BlockSpec((S, HP), lambda h: (0, h)), pl.BlockSpec((S, HP), lambda h: (0, h)),
                   pl.BlockSpec((S, LANES), lambda h: (0, 0))],
        out_shape=[SDS((S, H * HP), BF16), SDS((S, H * HP), BF16), SDS((S, LANES), F32)],
        scratch_shapes=[pltpu.VMEM((S, HP), F32), pltpu.VMEM((S, LANES), F32)],
        compiler_params=_params(("arbitrary",)),
    )(q, k, v, o, do, lse, cos_t, sin_t)


def _mla_out(o, w_o, x, mod, tm, name):
    S, KO = o.shape
    D = w_o.shape[1]

    def body(o_ref, w_ref, x_ref, mod_ref, x1_ref, y_ref):
        y = _dot(o_ref[...], w_ref[...])
        y_ref[...] = y.astype(BF16)
        x1_ref[...] = x_ref[...] + mod_ref[G1:G1 + 1, :] * y

    return pl.pallas_call(
        body, name=name, grid=(S // tm,),
        in_specs=[_rows1(tm, KO), _resident(w_o.shape), _rows1(tm, D), _resident(mod.shape)],
        out_specs=[_rows1(tm, D), _rows1(tm, D)],
        out_shape=[SDS((S, D), F32), SDS((S, D), BF16)],
        compiler_params=_params(("parallel",)),
    )(o, w_o, x, mod)


def _mla_bwd_o(dx1, y, mod, w_o, tm, name):
    S, D = dx1.shape
    KO = w_o.shape[0]

    def body(d1_ref, y_ref, mod_ref, w_ref, dy_ref, do_ref, st_ref):
        @pl.when(pl.program_id(0) == 0)
        def _():
            st_ref[...] = jnp.zeros_like(st_ref)

        d1 = d1_ref[...]
        dyb = (d1 * mod_ref[G1:G1 + 1, :]).astype(BF16)
        dy_ref[...] = dyb
        st_ref[G1:G1 + 1, :] += _colsum(d1 * y_ref[...].astype(F32))
        do_ref[...] = _dot_nt(dyb, w_ref[...]).astype(BF16)

    return pl.pallas_call(
        body, name=name, grid=(S // tm,),
        in_specs=[_rows1(tm, D), _rows1(tm, D), _resident(mod.shape), _resident(w_o.shape)],
        out_specs=[_rows1(tm, D), _rows1(tm, KO), pl.BlockSpec((8, D), lambda i: (0, 0))],
        out_shape=[SDS((S, D), BF16), SDS((S, KO), BF16), SDS((8, D), F32)],
        compiler_params=_params(("arbitrary",)),
    )(dx1, y, mod, w_o)


def _mla_bwd_lat(dq, dkv, dkr, lat, x, dx1, mod, gn, qg, kvg, wq, wukv, wd, cos_t, sin_t, tm, name):
    S, D = x.shape
    LW = wd.shape[1]
    QL, KL = MLA_Q_LORA, MLA_KV_LORA

    def body(dq_ref, dkv_ref, dkr_ref, lat_ref, x_ref, d1_ref, mod_ref, gn_ref, qg_ref, kvg_ref, wq_ref, wkv_ref, wd_ref,
             c_ref, s_ref, dx_ref, dlat_ref, st_ref, dqg_ref, dkvg_ref):
        @pl.when(pl.program_id(0) == 0)
        def _():
            st_ref[...] = jnp.zeros_like(st_ref)
            dqg_ref[...] = jnp.zeros_like(dqg_ref)
            dkvg_ref[...] = jnp.zeros_like(dkvg_ref)

        lat = lat_ref[...]
        dcq = _dot_nt(dq_ref[...], wq_ref[...])
        nq, rq = _rms(lat[:, :QL])
        dqg_ref[0:1, :] += _colsum(dcq * nq)
        dlat_q = _normmod_bwd(dcq, nq, rq, qg_ref[...]).astype(BF16)
        dckv = _dot_nt(dkv_ref[...], wkv_ref[...])
        nkv, rkv = _rms(lat[:, QL:QL + KL])
        dkvg_ref[0:1, :] += _colsum(dckv * nkv)
        dlat_kv = _normmod_bwd(dckv, nkv, rkv, kvg_ref[...]).astype(BF16)
        dkr = dkr_ref[...]
        dlat_kr = (dkr * c_ref[...] + _swap_halves(dkr * s_ref[...])).astype(BF16)
        dlat_ref[:, :QL] = dlat_q
        dlat_ref[:, QL:QL + KL] = dlat_kv
        dlat_ref[:, QL + KL:] = dlat_kr
        dh = (_dot_nt(dlat_q, wd_ref[:, :QL]) + _dot_nt(dlat_kv, wd_ref[:, QL:QL + KL])
              + _dot_nt(dlat_kr, wd_ref[:, QL + KL:]))
        n, rr = _rms(x_ref[...])
        gn_v = gn_ref[...]
        sc1p = 1.0 + mod_ref[SC1:SC1 + 1, :]
        t = _colsum(dh * n)
        st_ref[SH1:SH1 + 1, :] += _colsum(dh)
        st_ref[SC1:SC1 + 1, :] += t * gn_v
        st_ref[3:4, :] += t * sc1p
        dx_ref[...] = d1_ref[...] + _normmod_bwd(dh, n, rr, gn_v * sc1p)

    HW = wq.shape[1]
    return pl.pallas_call(
        body, name=name, grid=(S // tm,),
        in_specs=[_rows1(tm, HW), _rows1(tm, HW), _rows1(tm, LANES), _rows1(tm, LW), _rows1(tm, D), _rows1(tm, D),
                  _resident(mod.shape), _resident(gn.shape), _resident(qg.shape), _resident(kvg.shape),
                  _resident(wq.shape), _resident(wukv.shape), _resident(wd.shape), _rows1(tm, LANES), _rows1(tm, LANES)],
        out_specs=[_rows1(tm, D), _rows1(tm, LW), pl.BlockSpec((8, D), lambda i: (0, 0)),
                   pl.BlockSpec((8, QL), lambda i: (0, 0)), pl.BlockSpec((8, KL), lambda i: (0, 0))],
        out_shape=[SDS((S, D), F32), SDS((S, LW), BF16), SDS((8, D), F32), SDS((8, QL), F32), SDS((8, KL), F32)],
        compiler_params=_params(("arbitrary",)),
    )(dq, dkv, dkr, lat, x, dx1, mod, gn, qg, kvg, wq, wukv, wd, cos_t, sin_t)


def _loss_head(x, tgt, fg, tm, name):
    S, D = x.shape
    nt = S // tm

    def body(x_ref, t_ref, g_ref, dx_ref, acc_ref):
        i = pl.program_id(0)

        @pl.when(i == 0)
        def _():
            acc_ref[...] = jnp.zeros_like(acc_ref)

        n, rr = _rms(x_ref[...])
        g = g_ref[...]
        err = n * g - t_ref[...]
        acc_ref[1:2, :] += _colsum(err * err) * (0.5 / D)
        dy = err * (1.0 / D)
        acc_ref[0:1, :] += _colsum(dy * n)
        dx_ref[...] = _normmod_bwd(dy, n, rr, g)

        @pl.when(i == nt - 1)
        def _():
            acc_ref[2:3, :] = jnp.broadcast_to(jnp.sum(acc_ref[1:2, :], axis=1, keepdims=True), (1, D))

    return pl.pallas_call(
        body, name=name, grid=(nt,),
        in_specs=[_rows1(tm, D), _rows1(tm, D), _resident(fg.shape)],
        out_specs=[_rows1(tm, D), pl.BlockSpec((8, D), lambda i: (0, 0))],
        out_shape=[SDS((S, D), F32), SDS((8, D), F32)],
        compiler_params=_params(("arbitrary",)),
    )(x, tgt, fg)


def _rope_tables(positions, S):
    inv_freq = ROPE_THETA ** (-jnp.arange(0, MLA_ROPE, 2, dtype=F32) / MLA_ROPE)
    ang = positions.reshape(S, 1).astype(F32) * inv_freq
    cos, sin = jnp.cos(ang), jnp.sin(ang)
    z = jnp.zeros((S, LANES - MLA_ROPE), F32)
    return jnp.concatenate([cos, cos, z], axis=1), jnp.concatenate([-sin, sin, z], axis=1)


def _pad_heads(w, per_head):
    K = w.shape[0]
    H = w.shape[1] // per_head
    w3 = w.reshape(K, H, per_head)
    return jnp.pad(w3, ((0, 0), (0, 0), (0, MLA_HEAD_PAD - per_head))).reshape(K, H * MLA_HEAD_PAD)


def _unpad_heads(w, per_head):
    K = w.shape[0]
    H = w.shape[1] // MLA_HEAD_PAD
    return w.reshape(K, H, MLA_HEAD_PAD)[:, :, :per_head].reshape(K, H * per_head)


def _tiles(S):
    return min(512, S), min(256, S), min(512, S)


def _layer_forward(i, x, mod, gmix, gmlp, w, small, tables):
    if True:
        S, D = x.shape
        tm, tms, tq = _tiles(S)
        cos_t, sin_t = tables
        mod, gmix, gmlp = {i: mod}, {i: gmix}, {i: gmlp}
        kind = i % 3
        sv = {"x": x}
        if kind == 0:
            x1 = _pool_fwd(x, mod[i], gmix[i], w["pool_w"], small["pool_scale"][i // 3], tm, f"pool_fwd_{i}")
        elif kind == 1:
            x1, sv["h"], sv["pre"], sv["y"] = _sgu_fwd(
                x, mod[i], gmix[i], w["sgu_w_in"], small["sgu_ln_g"], small["sgu_ln_b"], small["sgu_w_s"],
                small["sgu_b_s_t"], w["sgu_w_out"], tms, f"sgu_fwd_{i}")
        else:
            sv["h"], sv["lat"], sv["cq"], sv["ckv"], krp = _mla_lat(
                x, mod[i], gmix[i], w["mla_wd"], small["mla_q_norm_g"], small["mla_kv_norm_g"], cos_t, sin_t, tm,
                f"mla_lat_{i}")
            sv["q"], sv["k"], sv["v"] = _mla_qkv(sv["cq"], sv["ckv"], krp, w["mla_wq"], w["mla_w_ukv"], cos_t, sin_t,
                                                 tm, f"mla_qkv_{i}")
            sv["o"], sv["lse"] = _attn_fwd(sv["q"], sv["k"], sv["v"], tq, f"attn_fwd_{i}")
            x1, sv["y"] = _mla_out(sv["o"], w["mla_w_o"], x, mod[i], tm, f"mla_out_{i}")
        sv["x1"] = x1
        Fh = w["mlp_w1"].shape[1]
        sv["h2"], sv["r"] = _mlp_up(x1, mod[i], gmlp[i], w["mlp_w1"], tm, min(2048, Fh), f"mlp_up_{i}")
        if callable(w["mlp_w2"]):
            w["mlp_w2"] = w["mlp_w2"](sv["r"])
        x, sv["o2"] = _mlp_down(sv["r"], w["mlp_w2"], x1, mod[i], tm, f"mlp_down_{i}")
        return x, sv


def _layer_backward(i, dx, sv, mod, gmix, gmlp, w, small, tables, on_mlp_grads=None):
    if True:
        S, D = dx.shape
        tm, tms, tq = _tiles(S)
        cos_t, sin_t = tables
        kind = i % 3
        sgrads = {}
        Fh = w["mlp_w1"].shape[1]
        g = {}
        d_a, d_o, st_a = _mlp_bwd_a(dx, sv["o2"], mod, w["mlp_w2"], sv["r"], tm, min(2048, Fh), f"mlp_bwd_a_{i}")
        g["mlp_w2"] = _mm_tn(sv["r"], d_o, min(512, Fh), D, f"mlp_dw2_{i}", square_a=True)
        g["mlp_w1"] = _mm_tn(sv["h2"], d_a, D, min(512, Fh), f"mlp_dw1_{i}", col_shards=True)
        if on_mlp_grads is not None:
            mod = mod + on_mlp_grads({n: g.pop(n) for n in ("mlp_w1", "mlp_w2")})
        mod, gmix, gmlp = {i: mod}, {i: gmix}, {i: gmlp}
        dx1, st_b = _mlp_bwd_b(d_a, w["mlp_w1"], sv["x1"], dx, mod[i], gmlp[i], tm, f"mlp_bwd_b_{i}")
        if kind == 0:
            dx, st_m, dpw = _pool_bwd(sv["x"], dx1, mod[i], gmix[i], w["pool_w"], small["pool_scale"][i // 3], tm,
                                      f"pool_bwd_{i}")
            g["pool_w"] = dpw
            sgrads[f"pool_scale_{i // 3}"] = st_m[4:5]
        elif kind == 1:
            dx, dyb, gated, dpre, st_m, dws, dbs = _sgu_bwd(
                sv["x"], dx1, sv["pre"], sv["y"], mod[i], gmix[i], w["sgu_w_in"], small["sgu_ln_g"], small["sgu_ln_b"],
                small["sgu_w_s"], small["sgu_b_s_t"], w["sgu_w_out"], tms, f"sgu_bwd_{i}")
            W = gated.shape[1]
            g["sgu_w_out"] = _mm_tn(gated, dyb, min(512, W), D, f"sgu_dwout_{i}")
            g["sgu_w_in"] = _mm_tn(sv["h"], dpre, D, min(512, 2 * W), f"sgu_dwin_{i}", col_shards=True)
            sgrads["sgu_ln_g"], sgrads["sgu_ln_b"] = st_m[4:5], st_m[5:6]
            sgrads["sgu_w_s"], sgrads["sgu_b_s"] = dws, dbs[:, :, 0]
        else:
            dyb, do, st_o = _mla_bwd_o(dx1, sv["y"], mod[i], w["mla_w_o"], tm, f"mla_bwd_o_{i}")
            KO = do.shape[1]
            g["mla_w_o"] = _mm_tn(sv["o"], dyb, min(512, KO), D, f"mla_dwo_{i}")
            dq, dkv, dkr = _attn_bwd(sv["q"], sv["k"], sv["v"], sv["o"], do, sv["lse"], cos_t, sin_t, tq, f"attn_bwd_{i}")
            dx, dlat, st_m, dqg, dkvg = _mla_bwd_lat(
                dq, dkv, dkr, sv["lat"], sv["x"], dx1, mod[i], gmix[i], small["mla_q_norm_g"], small["mla_kv_norm_g"],
                w["mla_wq"], w["mla_w_ukv"], w["mla_wd"], cos_t, sin_t, tm, f"mla_bwd_lat_{i}")
            HW = dq.shape[1]
            g["mla_wq"] = _mm_tn(sv["cq"], dq, MLA_Q_LORA, min(1024, HW), f"mla_dwq_{i}")
            g["mla_w_ukv"] = _mm_tn(sv["ckv"], dkv, MLA_KV_LORA, min(1024, HW), f"mla_dwukv_{i}", col_shards=True)
            g["mla_wd"] = _mm_tn(sv["h"], dlat, D, dlat.shape[1], f"mla_dwd_{i}")
            st_m = jnp.concatenate([st_m[0:2], st_o[2:3], st_m[3:]], axis=0)
            sgrads["mla_q_norm_g"], sgrads["mla_kv_norm_g"] = dqg[0:1], dkvg[0:1]
        stats = jnp.concatenate([st_m[0:3], st_b[3:5], st_a[5:6], st_m[3:4], st_b[6:7]], axis=0)
        return dx, stats, g, sgrads


def _local_step(x, tgt, positions, mod, gmix, gmlp, fg, wts, small):
    S = x.shape[0]
    L = mod.shape[0]
    tables = _rope_tables(positions, S)
    saved = []
    for i in range(L):
        x, sv = _layer_forward(i, x, mod[i], gmix[i], gmlp[i], wts[i], small, tables)
        saved.append(sv)
    dx, loss_acc = _loss_head(x, tgt, fg, _tiles(S)[0], "loss_head")
    stats, grads, sgrads = [None] * L, [None] * L, {}
    for i in reversed(range(L)):
        dx, stats[i], grads[i], sg = _layer_backward(i, dx, saved[i], mod[i], gmix[i], gmlp[i], wts[i], small, tables)
        sgrads.update(sg)
    return loss_acc, dx, stats, grads, sgrads


HBM_SPEC = pl.BlockSpec(memory_space=pltpu.HBM)
VMEM_SPEC = pl.BlockSpec(memory_space=pltpu.VMEM)


def _my_place():
    return lax.axis_index("x"), lax.axis_index("y"), lax.axis_index("c")


def _flip(v, bit):
    return 1 - v if bit else v


def _small_all_gather(v, name):
    R, C = v.shape

    def body(x_ref, out_ref, send_sems, recv_sems):
        x, y, c = _my_place()
        me = 4 * x + 2 * y + c
        out_ref[me] = x_ref[...]
        sends = []
        for k in range(1, NDEV):
            peer = (_flip(x, k & 4), _flip(y, k & 2), _flip(c, k & 1))
            cp = pltpu.make_async_remote_copy(src_ref=x_ref, dst_ref=out_ref.at[me], send_sem=send_sems.at[k - 1],
                                              recv_sem=recv_sems.at[k - 1], device_id=peer, device_id_type=MESH)
            cp.start()
            sends.append(cp)
        for k in range(1, NDEV):
            src = 4 * _flip(x, k & 4) + 2 * _flip(y, k & 2) + _flip(c, k & 1)
            pltpu.make_async_remote_copy(src_ref=x_ref, dst_ref=out_ref.at[src], send_sem=send_sems.at[k - 1],
                                         recv_sem=recv_sems.at[k - 1], device_id=(x, y, c), device_id_type=MESH).wait_recv()
        for cp in sends:
            cp.wait_send()

    return pl.pallas_call(
        body, name=name, out_shape=SDS((NDEV, R, C), v.dtype), in_specs=[VMEM_SPEC], out_specs=VMEM_SPEC,
        scratch_shapes=[pltpu.SemaphoreType.DMA((NDEV - 1,)), pltpu.SemaphoreType.DMA((NDEV - 1,))],
        compiler_params=pltpu.CompilerParams(vmem_limit_bytes=V7X_VMEM_LIMIT),
    )(v)


def _slab(ref, axis, width, dev):
    idx = [slice(None)] * len(ref.shape)
    idx[axis] = pl.ds(pl.multiple_of(dev * width, width), width)
    return ref.at[tuple(idx)]


def _all_gather_group(shards, axes, after, name):
    nt = len(shards)
    out_shapes = [SDS(tuple(s * NDEV if a == ax else s for a, s in enumerate(sh.shape)), sh.dtype)
                  for sh, ax in zip(shards, axes)]

    def body(*refs):
        ins, outs = refs[:nt], refs[nt + 1:2 * nt + 1]
        send_sems, recv_sems, local_sems = refs[2 * nt + 1:]
        x, y, c = _my_place()
        me = 4 * x + 2 * y + c
        sibling = (x, y, 1 - c)
        chips = [(1 - x, y), (x, 1 - y), (1 - x, 1 - y)]

        def block(t, dev):
            return _slab(outs[t], axes[t], ins[t].shape[axes[t]], dev)

        def copy(t, k, dev, to, src=None):
            return pltpu.make_async_remote_copy(
                src_ref=block(t, dev) if src is None else src, dst_ref=block(t, dev), send_sem=send_sems.at[t, k],
                recv_sem=recv_sems.at[t, k], device_id=to, device_id_type=MESH)

        mine = [pltpu.make_async_copy(ins[t], block(t, me), local_sems.at[t]) for t in range(nt)]
        for cp in mine:
            cp.start()
        first = []
        for t in range(nt):
            first.append(copy(t, 0, me, sibling, src=ins[t]))
            first += [copy(t, 1 + j, me, (cx, cy, c), src=ins[t]) for j, (cx, cy) in enumerate(chips)]
        for cp in first:
            cp.start()
        passed = []
        for j, (cx, cy) in enumerate(chips):
            for t in range(nt):
                copy(t, 1 + j, 4 * cx + 2 * cy + c, (x, y, c)).wait_recv()
                cp = copy(t, 4 + j, 4 * cx + 2 * cy + c, sibling)
                cp.start()
                passed.append(cp)
        for t in range(nt):
            copy(t, 0, 4 * x + 2 * y + (1 - c), (x, y, c)).wait_recv()
        for j, (cx, cy) in enumerate(chips):
            for t in range(nt):
                copy(t, 4 + j, 4 * cx + 2 * cy + (1 - c), (x, y, c)).wait_recv()
        for cp in first + passed:
            cp.wait_send()
        for cp in mine:
            cp.wait()

    return pl.pallas_call(
        body, name=name, out_shape=out_shapes, in_specs=[HBM_SPEC] * nt + [ANY_SPEC], out_specs=[HBM_SPEC] * nt,
        scratch_shapes=[pltpu.SemaphoreType.DMA((nt, NDEV - 1)), pltpu.SemaphoreType.DMA((nt, NDEV - 1)),
                        pltpu.SemaphoreType.DMA((nt,))],
    )(*shards, after)


def _reduce_scatter_sibling(grads, name):
    nt = len(grads)
    NCH = NDEV // 2
    out_shapes = [SDS((NCH,) + gr.shape[1:], gr.dtype) for gr in grads]

    def body(*refs):
        ins, lands = refs[:nt], refs[nt:2 * nt]
        send_sems, recv_sems = refs[2 * nt:]
        x, y, c = _my_place()
        sends = []
        for t in range(nt):
            for k in range(NCH):
                cp = pltpu.make_async_remote_copy(
                    src_ref=ins[t].at[2 * k + (1 - c)], dst_ref=lands[t].at[k], send_sem=send_sems.at[t, k],
                    recv_sem=recv_sems.at[t, k], device_id=(x, y, 1 - c), device_id_type=MESH)
                cp.start()
                sends.append(cp)
        for cp in sends:
            cp.wait_recv()
        for cp in sends:
            cp.wait_send()

    return pl.pallas_call(
        body, name=name, out_shape=out_shapes, in_specs=[HBM_SPEC] * nt, out_specs=[HBM_SPEC] * nt,
        scratch_shapes=[pltpu.SemaphoreType.DMA((nt, NCH)), pltpu.SemaphoreType.DMA((nt, NCH))],
    )(*grads)


SEM_SPEC = pl.BlockSpec(memory_space=pltpu.SEMAPHORE)
ANY_SPEC = pl.BlockSpec(memory_space=pl.ANY)
SPLIT_PARAMS = pltpu.CompilerParams(has_side_effects=pltpu.SideEffectType.DATAFLOW_SIDE_EFFECTING)
TOKEN = SDS((8, LANES), F32)


def _in_hbm(arrays):
    return [pltpu.with_memory_space_constraint(v, pltpu.HBM) for v in arrays]


def _split_start(body, srcs, lands, after, n_sem, name):
    ns, nl = len(srcs), len(lands)
    bufs = list(srcs) + list(lands)
    res = pl.pallas_call(
        body, name=name,
        out_shape=(pltpu.SemaphoreType.DMA((ns * n_sem,)), pltpu.SemaphoreType.DMA((ns * n_sem,)),
                   *[pltpu.HBM(v.shape, v.dtype) for v in bufs], TOKEN),
        in_specs=[HBM_SPEC] * (ns + nl) + [ANY_SPEC],
        out_specs=(SEM_SPEC, SEM_SPEC, *[HBM_SPEC] * (ns + nl), VMEM_SPEC),
        input_output_aliases={t: 2 + t for t in range(ns + nl)}, compiler_params=SPLIT_PARAMS,
    )(*_in_hbm(bufs), after)
    return res[0], res[1], list(res[2:2 + ns]), list(res[2 + ns:2 + ns + nl]), res[-1]


def _split_wait(body, send_sems, recv_sems, srcs, lands, after, name):
    ns, nl = len(srcs), len(lands)
    bufs = list(srcs) + list(lands)
    res = pl.pallas_call(
        body, name=name, out_shape=tuple(pltpu.HBM(v.shape, v.dtype) for v in bufs),
        in_specs=[HBM_SPEC] * (ns + nl) + [SEM_SPEC, SEM_SPEC, ANY_SPEC], out_specs=tuple([HBM_SPEC] * (ns + nl)),
        input_output_aliases={t: t for t in range(ns + nl)}, compiler_params=SPLIT_PARAMS,
    )(*bufs, send_sems, recv_sems, after)
    return list(res[:ns]), list(res[ns:])


def _chips_exchange_start(parts, after, name):
    nt = len(parts)
    lands = [lax.empty((3,) + p.shape[1:], p.dtype) for p in parts]

    def body(*refs):
        ins, lnd = refs[:nt], refs[nt:2 * nt]
        send_sems, recv_sems, token = refs[2 * nt + 1], refs[2 * nt + 2], refs[-1]
        x, y, c = _my_place()
        for t in range(nt):
            for m in range(1, 4):
                px, py = _flip(x, m & 2), _flip(y, m & 1)
                pltpu.make_async_remote_copy(
                    src_ref=ins[t].at[2 * px + py], dst_ref=lnd[t].at[m - 1], send_sem=send_sems.at[3 * t + m - 1],
                    recv_sem=recv_sems.at[3 * t + m - 1], device_id=(px, py, c), device_id_type=MESH).start()
        token[...] = jnp.zeros_like(token)

    return _split_start(body, parts, lands, after, 3, name)


def _chips_exchange_wait(send_sems, recv_sems, parts, lands, after, name):
    nt = len(parts)

    def body(*refs):
        ins, lnd = refs[:nt], refs[nt:2 * nt]
        s_sems, r_sems = refs[2 * nt], refs[2 * nt + 1]
        x, y, c = _my_place()
        for t in range(nt):
            for m in range(1, 4):
                cp = pltpu.make_async_remote_copy(
                    src_ref=ins[t].at[0], dst_ref=lnd[t].at[m - 1], send_sem=s_sems.at[3 * t + m - 1],
                    recv_sem=r_sems.at[3 * t + m - 1], device_id=(x, y, c), device_id_type=MESH)
                cp.wait_send()
                cp.wait_recv()

    return _split_wait(body, send_sems, recv_sems, parts, lands, after, name)


def _direct_exchange_start(grads, after, name):
    nt = len(grads)
    lands = [lax.empty((NDEV - 1,) + gr.shape[1:], gr.dtype) for gr in grads]

    def body(*refs):
        ins, lnd = refs[:nt], refs[nt:2 * nt]
        send_sems, recv_sems, token = refs[2 * nt + 1], refs[2 * nt + 2], refs[-1]
        x, y, c = _my_place()
        for t in range(nt):
            for k in range(1, NDEV):
                px, py, pc = _flip(x, k & 4), _flip(y, k & 2), _flip(c, k & 1)
                pltpu.make_async_remote_copy(
                    src_ref=ins[t].at[4 * px + 2 * py + pc], dst_ref=lnd[t].at[k - 1],
                    send_sem=send_sems.at[(NDEV - 1) * t + k - 1], recv_sem=recv_sems.at[(NDEV - 1) * t + k - 1],
                    device_id=(px, py, pc), device_id_type=MESH).start()
        token[...] = jnp.zeros_like(token)

    return _split_start(body, grads, lands, after, NDEV - 1, name)


def _direct_exchange_wait(send_sems, recv_sems, grads, lands, after, name):
    nt = len(grads)

    def body(*refs):
        ins, lnd = refs[:nt], refs[nt:2 * nt]
        s_sems, r_sems = refs[2 * nt], refs[2 * nt + 1]
        x, y, c = _my_place()
        for t in range(nt):
            for k in range(1, NDEV):
                cp = pltpu.make_async_remote_copy(
                    src_ref=ins[t].at[0], dst_ref=lnd[t].at[k - 1], send_sem=s_sems.at[(NDEV - 1) * t + k - 1],
                    recv_sem=r_sems.at[(NDEV - 1) * t + k - 1], device_id=(x, y, c), device_id_type=MESH)
                cp.wait_send()
                cp.wait_recv()

    return _split_wait(body, send_sems, recv_sems, grads, lands, after, name)


def _place_own(shards, axes, name):
    nt = len(shards)
    out_shapes = [SDS(tuple(s * NDEV if a == ax else s for a, s in enumerate(sh.shape)), sh.dtype)
                  for sh, ax in zip(shards, axes)]

    def body(*refs):
        ins, outs, sems = refs[:nt], refs[nt:2 * nt], refs[2 * nt]
        x, y, c = _my_place()
        copies = [pltpu.make_async_copy(ins[t], _slab(outs[t], axes[t], ins[t].shape[axes[t]], 4 * x + 2 * y + c), sems.at[t])
                  for t in range(nt)]
        for cp in copies:
            cp.start()
        for cp in copies:
            cp.wait()

    return pl.pallas_call(
        body, name=name, out_shape=out_shapes, in_specs=[VMEM_SPEC] * nt, out_specs=[HBM_SPEC] * nt,
        scratch_shapes=[pltpu.SemaphoreType.DMA((nt,))],
        compiler_params=pltpu.CompilerParams(vmem_limit_bytes=V7X_VMEM_LIMIT),
    )(*shards)


def _small_gather_start(v, me, after, name):
    land = lax.dynamic_update_slice(lax.empty((NDEV,) + v.shape, v.dtype), v[None], (me, 0, 0))

    def body(*refs):
        src, lnd = refs[0], refs[1]
        send_sems, recv_sems, token = refs[3], refs[4], refs[-1]
        x, y, c = _my_place()
        for k in range(1, NDEV):
            peer = (_flip(x, k & 4), _flip(y, k & 2), _flip(c, k & 1))
            pltpu.make_async_remote_copy(src_ref=src, dst_ref=lnd.at[4 * x + 2 * y + c], send_sem=send_sems.at[k - 1],
                                         recv_sem=recv_sems.at[k - 1], device_id=peer, device_id_type=MESH).start()
        token[...] = jnp.zeros_like(token)

    return _split_start(body, [v], [land], after, NDEV - 1, name)


def _small_gather_wait(send_sems, recv_sems, srcs, lands, after, name):
    def body(*refs):
        src, lnd, s_sems, r_sems = refs[0], refs[1], refs[2], refs[3]
        x, y, c = _my_place()
        for k in range(1, NDEV):
            sender = 4 * _flip(x, k & 4) + 2 * _flip(y, k & 2) + _flip(c, k & 1)
            cp = pltpu.make_async_remote_copy(src_ref=src, dst_ref=lnd.at[sender], send_sem=s_sems.at[k - 1],
                                              recv_sem=r_sems.at[k - 1], device_id=(x, y, c), device_id_type=MESH)
            cp.wait_send()
            cp.wait_recv()

    return _split_wait(body, send_sems, recv_sems, srcs, lands, after, name)[1][0]


def _gather_start(shards, axes, me, after, name):
    nt = len(shards)
    fulls = _place_own(shards, axes, name + "_own")

    def body(*refs):
        ins, outs = refs[:nt], refs[nt:2 * nt]
        send_sems, recv_sems, token = refs[2 * nt + 1], refs[2 * nt + 2], refs[-1]
        x, y, c = _my_place()
        dev = 4 * x + 2 * y + c
        peers = [(x, y, 1 - c), (1 - x, y, c), (x, 1 - y, c), (1 - x, 1 - y, c)]
        for t in range(nt):
            dst = _slab(outs[t], axes[t], ins[t].shape[axes[t]], dev)
            for k, peer in enumerate(peers):
                pltpu.make_async_remote_copy(src_ref=ins[t], dst_ref=dst, send_sem=send_sems.at[4 * t + k],
                                             recv_sem=recv_sems.at[4 * t + k], device_id=peer, device_id_type=MESH).start()
        token[...] = jnp.zeros_like(token)

    return _split_start(body, shards, fulls, after, 4, name)


def _gather_wait(send_sems, recv_sems, shards, fulls, axes, after, name):
    nt = len(shards)

    def body(*refs):
        ins, outs = refs[:nt], refs[nt:2 * nt]
        s_sems, r_sems = refs[2 * nt], refs[2 * nt + 1]
        x, y, c = _my_place()
        senders = [4 * x + 2 * y + (1 - c), 4 * (1 - x) + 2 * y + c, 4 * x + 2 * (1 - y) + c, 4 * (1 - x) + 2 * (1 - y) + c]
        for t in range(nt):
            for k, src_dev in enumerate(senders):
                cp = pltpu.make_async_remote_copy(
                    src_ref=ins[t], dst_ref=_slab(outs[t], axes[t], ins[t].shape[axes[t]], src_dev),
                    send_sem=s_sems.at[4 * t + k], recv_sem=r_sems.at[4 * t + k], device_id=(x, y, c), device_id_type=MESH)
                cp.wait_send()
                cp.wait_recv()

    return _split_wait(body, send_sems, recv_sems, shards, fulls, after, name)[1]


def _gather_pass_on(fulls, axes, name):
    nt = len(fulls)

    def body(*refs):
        outs = refs[nt:2 * nt]
        send_sems, recv_sems = refs[2 * nt:]
        x, y, c = _my_place()
        chips = [(1 - x, y), (x, 1 - y), (1 - x, 1 - y)]

        def copy(t, j, pc):
            cx, cy = chips[j]
            blk = _slab(outs[t], axes[t], outs[t].shape[axes[t]] // NDEV, 4 * cx + 2 * cy + pc)
            return pltpu.make_async_remote_copy(src_ref=blk, dst_ref=blk, send_sem=send_sems.at[t, j],
                                                recv_sem=recv_sems.at[t, j], device_id=(x, y, 1 - c), device_id_type=MESH)

        sends = [copy(t, j, c) for t in range(nt) for j in range(3)]
        for cp in sends:
            cp.start()
        for t in range(nt):
            for j in range(3):
                copy(t, j, 1 - c).wait_recv()
        for cp in sends:
            cp.wait_send()

    return pl.pallas_call(
        body, name=name, out_shape=[SDS(f.shape, f.dtype) for f in fulls], in_specs=[HBM_SPEC] * nt,
        out_specs=[HBM_SPEC] * nt, input_output_aliases={t: t for t in range(nt)},
        scratch_shapes=[pltpu.SemaphoreType.DMA((nt, 3)), pltpu.SemaphoreType.DMA((nt, 3))],
    )(*fulls)


def _row_tile(R, C, itemsize=4, target=1 << 20):
    best = R
    for tr in range(8, R, 8):
        if R % tr == 0 and tr * C * itemsize <= target:
            best = tr
    return best if best * C * itemsize <= target or best == R else R


def _as2d(a):
    return a.reshape(-1, a.shape[-1])


def _add_pairs(grad, land, c_me, name):
    blk = grad.shape[1:]
    NCH = NDEV // 2
    C = blk[-1]
    R = math.prod(blk[:-1])
    tr = _row_tile(R, C, 2)

    def body(c_ref, a_ref, b_ref, o_ref):
        o_ref[...] = (a_ref[...].astype(F32) + b_ref[...].astype(F32)).astype(o_ref.dtype)

    out = pl.pallas_call(
        body, name=name, out_shape=SDS((NCH, R, C), grad.dtype),
        grid_spec=pltpu.PrefetchScalarGridSpec(
            num_scalar_prefetch=1, grid=(NCH, R // tr),
            in_specs=[pl.BlockSpec((None, None, tr, C), lambda k, i, c: (k, c[0], i, 0)),
                      pl.BlockSpec((None, tr, C), lambda k, i, c: (k, i, 0))],
            out_specs=pl.BlockSpec((None, tr, C), lambda k, i, c: (k, i, 0))),
        compiler_params=_params(("parallel", "parallel")),
    )(c_me.reshape(1).astype(jnp.int32), grad.reshape(NCH, 2, R, C), land.reshape(NCH, R, C))
    return out.reshape((NCH,) + blk)


def _adamw_math(g, w, m, v):
    m2 = ADAM_B1 * m + (1.0 - ADAM_B1) * g
    v2 = ADAM_B2 * v + (1.0 - ADAM_B2) * (g * g)
    m_hat = m2 / (1.0 - ADAM_B1 ** ADAM_STEP)
    v_hat = v2 / (1.0 - ADAM_B2 ** ADAM_STEP)
    delta = -ADAM_LR * (m_hat / (jnp.sqrt(v_hat) + ADAM_EPS) + ADAM_WD * w)
    return delta, m2, v2


def _adamw(parts, w, m, v, name, layer=None, into=None):
    shp = w.shape
    w2, m2, v2 = _as2d(w), _as2d(m), _as2d(v)
    nlay = 1 if layer is None else shp[0]
    j = 0 if layer is None else layer
    RT, C = w2.shape
    R = RT // nlay
    tr = _row_tile(R, C)
    nblk = R // tr
    p3, specs, picks = [], [], []
    for p in parts:
        if isinstance(p, tuple):
            specs.append(pl.BlockSpec((1, tr, C), lambda i, s, k=len(picks): (s[k], i, 0)))
            picks.append(p[1])
            p = p[0]
        else:
            specs.append(pl.BlockSpec((p.size // (R * C), tr, C), lambda i, s: (0, i, 0)))
        p3.append(p.reshape((-1, R, C)))
    npart = len(p3)
    picks = jnp.stack(picks).astype(jnp.int32) if picks else jnp.zeros((1,), jnp.int32)
    rows = pl.BlockSpec((tr, C), lambda i, s: (j * nblk + i, 0))
    prior = [] if into is None else [_as2d(o) for o in into]

    def body(s_ref, *refs):
        prefs = refs[:npart]
        w_ref, m_ref, v_ref = refs[npart:npart + 3]
        g_ref, d_ref, nm_ref, nv_ref = refs[npart + 3 + len(prior):]
        g = None
        for pr in prefs:
            for k in range(pr.shape[0]):
                term = pr[k].astype(F32)
                g = term if g is None else g + term
        g_ref[...] = g
        d_ref[...], nm_ref[...], nv_ref[...] = _adamw_math(g, w_ref[...], m_ref[...], v_ref[...])

    outs = pl.pallas_call(
        body, name=name, out_shape=[SDS((RT, C), F32)] * 4,
        grid_spec=pltpu.PrefetchScalarGridSpec(
            num_scalar_prefetch=1, grid=(nblk,), in_specs=specs + [rows] * 3 + [ANY_SPEC] * len(prior), out_specs=[rows] * 4),
        input_output_aliases={1 + npart + 3 + k: k for k in range(len(prior))},
        compiler_params=_params(("parallel",)),
    )(picks, *p3, w2, m2, v2, *prior)
    return [o.reshape(shp) for o in outs]


def _ada_fwd(c_all, ada_w, ada_b_mine, name):
    L, D, Wc = ada_w.shape

    def body(c_ref, w_ref, b_ref, o_ref):
        cv = c_ref[...]
        act = cv * (1.0 / (1.0 + jnp.exp(-cv)))
        o_ref[0] = jnp.dot(act, w_ref[0], preferred_element_type=F32, precision=lax.Precision.HIGHEST) + b_ref[0]

    return pl.pallas_call(
        body, name=name, grid=(L,),
        in_specs=[_resident(c_all.shape), pl.BlockSpec((1, D, Wc), lambda l: (l, 0, 0)), pl.BlockSpec((1, 1, Wc), lambda l: (l, 0, 0))],
        out_specs=pl.BlockSpec((1, NDEV, Wc), lambda l: (l, 0, 0)), out_shape=SDS((L, NDEV, Wc), F32),
        compiler_params=_params(("parallel",)),
    )(c_all, ada_w, ada_b_mine.reshape(L, 1, Wc))


def _ada_bwd(c_all, dmod_mine, name):
    L, _, Wc = dmod_mine.shape
    D = c_all.shape[1]

    def body(c_ref, d_ref, o_ref):
        cv = c_ref[...]
        act = cv * (1.0 / (1.0 + jnp.exp(-cv)))
        o_ref[0] = lax.dot_general(act, d_ref[0], (((0,), (0,)), ((), ())), preferred_element_type=F32,
                                   precision=lax.Precision.HIGHEST)

    return pl.pallas_call(
        body, name=name, grid=(L,),
        in_specs=[_resident(c_all.shape), pl.BlockSpec((1, NDEV, Wc), lambda l: (l, 0, 0))],
        out_specs=pl.BlockSpec((1, D, Wc), lambda l: (l, 0, 0)), out_shape=SDS((L, D, Wc), F32),
        compiler_params=_params(("parallel",)),
    )(c_all, dmod_mine)


WEIGHT_NAMES = ['ada_w', 'ada_b', 'norm_mix_g', 'norm_mlp_g', 'pool_w', 'pool_scale', 'sgu_w_in', 'sgu_ln_g', 'sgu_ln_b',
                'sgu_w_s', 'sgu_b_s', 'sgu_w_out', 'mla_w_dq_dkv', 'mla_q_norm_g', 'mla_kv_norm_g', 'mla_w_uq', 'mla_w_ukv',
                'mla_w_o', 'mlp_w1', 'mlp_w2', 'final_g']
REPLICATED_EARLY = ['sgu_ln_g', 'sgu_ln_b', 'sgu_w_s', 'sgu_b_s', 'mla_kv_norm_g']
REPLICATED = ['ada_b', 'norm_mix_g', 'norm_mlp_g', 'final_g']
PACK_ROWS = 64
DIRECT_FROM = 1
Q_HEAD = MLA_NOPE + MLA_ROPE


def _layer_matrices(i):
    kind, j = i % 3, i // 3
    if kind == 0:
        mats = [("pool_w", j, 1)]
    elif kind == 1:
        mats = [("sgu_w_in", j, 1), ("sgu_w_out", j, 0)]
    else:
        mats = [("mla_w_dq_dkv", j, 0), ("mla_w_uq", j, 1), ("mla_w_ukv", j, 1), ("mla_w_o", j, 0)]
    return mats + [("mlp_w1", i, 1), ("mlp_w2", i, 0)]


def _pack(arrays):
    flat = jnp.concatenate([a.reshape(-1).astype(F32) for a in arrays])
    rows = -(-flat.size // (LANES * PACK_ROWS)) * PACK_ROWS
    return jnp.pad(flat, (0, rows * LANES - flat.size)).reshape(rows, LANES)


def kernel(x, c, positions, ada_w, ada_b, norm_mix_g, norm_mlp_g, pool_w, pool_scale, sgu_w_in, sgu_ln_g, sgu_ln_b, sgu_w_s, sgu_b_s, sgu_w_out, mla_w_dq_dkv, mla_q_norm_g, mla_kv_norm_g, mla_w_uq, mla_w_ukv, mla_w_o, mlp_w1, mlp_w2, final_g, loss_target, m_ada_w, m_ada_b, m_norm_mix_g, m_norm_mlp_g, m_pool_w, m_pool_scale, m_sgu_w_in, m_sgu_ln_g, m_sgu_ln_b, m_sgu_w_s, m_sgu_b_s, m_sgu_w_out, m_mla_w_dq_dkv, m_mla_q_norm_g, m_mla_kv_norm_g, m_mla_w_uq, m_mla_w_ukv, m_mla_w_o, m_mlp_w1, m_mlp_w2, m_final_g, v_ada_w, v_ada_b, v_norm_mix_g, v_norm_mlp_g, v_pool_w, v_pool_scale, v_sgu_w_in, v_sgu_ln_g, v_sgu_ln_b, v_sgu_w_s, v_sgu_b_s, v_sgu_w_out, v_mla_w_dq_dkv, v_mla_q_norm_g, v_mla_kv_norm_g, v_mla_w_uq, v_mla_w_ukv, v_mla_w_o, v_mlp_w1, v_mlp_w2, v_final_g):
    a = dict(locals())
    S, D = x.shape[1], x.shape[2]
    L = ada_w.shape[0]
    Wc = ada_w.shape[2]
    me = 4 * lax.axis_index("x") + 2 * lax.axis_index("y") + lax.axis_index("c")
    my_chip = 2 * lax.axis_index("x") + lax.axis_index("y")

    v0 = _pack([c, pool_scale, mla_q_norm_g])
    g0 = _small_all_gather(v0, "gather_c").reshape(NDEV, -1)
    n_ps, n_qg = pool_scale.size, mla_q_norm_g.size
    c_all = g0[:, :D]
    ps_w = pool_scale.shape[1]
    ps_full = g0[:, D:D + n_ps].reshape(NDEV, -1, ps_w).transpose(1, 0, 2).reshape(-1, 1, D)
    qg_full = g0[:, D + n_ps:D + n_ps + n_qg].reshape(1, -1)

    ada_b_mine = lax.dynamic_slice_in_dim(ada_b, me * Wc, Wc, axis=1)
    modp = _ada_fwd(c_all, ada_w, ada_b_mine, "ada_fwd")
    ga = _small_all_gather(modp.reshape(-1, LANES), "gather_mod").reshape(NDEV, L, NDEV, Wc)
    mod = lax.dynamic_index_in_dim(ga, me, axis=2, keepdims=False).transpose(1, 0, 2).reshape(L, 6, D)
    mod8 = jnp.pad(mod, ((0, 0), (0, 2), (0, 0)))

    small = {"pool_scale": ps_full, "sgu_ln_g": sgu_ln_g, "sgu_ln_b": sgu_ln_b, "sgu_w_s": sgu_w_s[0],
             "sgu_b_s_t": sgu_b_s[0].T, "mla_q_norm_g": qg_full, "mla_kv_norm_g": mla_kv_norm_g}
    gmix, gmlp = norm_mix_g.reshape(L, 1, D), norm_mlp_g.reshape(L, 1, D)
    tables = _rope_tables(positions, S)

    def shards_of(mats):
        return [a[n][j].astype(BF16) for n, j, _ in mats], [ax for _, _, ax in mats]

    def as_weights(mats, fulls):
        w = {n: f for (n, _, _), f in zip(mats, fulls)}
        if "mla_w_uq" in w:
            lat_w = w["mla_w_dq_dkv"].shape[1]
            w["mla_wd"] = jnp.pad(w.pop("mla_w_dq_dkv"), ((0, 0), (0, -lat_w % LANES)))
            w["mla_wq"] = _pad_heads(w.pop("mla_w_uq"), Q_HEAD)
        return w

    xc = x[0]
    wts, saved, flying = [], [], None
    for i in range(L):
        mats = _layer_matrices(i)
        mod_i = mod8[i]
        if flying is None:
            now, late = mats[:-1], mats[-1:]
            shards, axes = shards_of(now)
            fulls = _all_gather_group(shards, axes, mod8, f"gather_w_{i}")
            w = as_weights(now, fulls)
            late_axes = shards_of(late)[1]
            late_fly = _gather_start(*shards_of(late), me, fulls[0], f"gather_start_{i}b")
            mod_i = mod_i + late_fly[4][0, 0]
            w[late[0][0]] = lambda after, fly=late_fly, axes=late_axes, i=i: _gather_pass_on(
                _gather_wait(*fly[:4], axes, after, f"gather_wait_{i}b"), axes, f"gather_pass_{i}b")[0]
            order = late_fly[4]
        else:
            axes = shards_of(mats)[1]
            fulls = _gather_wait(*flying[:4], axes, xc, f"gather_wait_{i}")
            fulls = _gather_pass_on(fulls, axes, f"gather_pass_{i}")
            w = as_weights(mats, fulls)
            order = fulls[0]
        wts.append(w)
        if i + 1 < L:
            flying = _gather_start(*shards_of(_layer_matrices(i + 1)), me, order, f"gather_start_{i + 1}")
            mod_i = mod_i + flying[4][0, 0]
        xc, sv = _layer_forward(i, xc, mod_i, gmix[i], gmlp[i], wts[i], small, tables)
        saved.append(sv)
    dx, loss_acc = _loss_head(xc, loss_target[0], final_g.reshape(1, D), _tiles(S)[0], "loss_head")

    res = {}
    c_me = lax.axis_index("c")

    def start_reduce(i, mats, g, after, tag=""):
        g = dict(g)
        if "mla_wq" in g:
            g["mla_w_dq_dkv"] = g.pop("mla_wd")[:, :mla_w_dq_dkv.shape[2]]
            g["mla_w_uq"] = _unpad_heads(g.pop("mla_wq"), Q_HEAD)
        gl = []
        for n, j, ax in mats:
            gm, blk = g[n].astype(BF16), a[n][j].shape
            if gm.shape != (NDEV,) + blk:
                gm = jnp.moveaxis(gm.reshape(blk[:ax] + (NDEV,) + blk[ax:]), ax, 0)
            gl.append(gm)
        if i >= DIRECT_FROM:
            return _direct_exchange_start(gl, after, f"rs_direct_start_{i}{tag}") + (mats, f"{i}{tag}", True)
        lands = _reduce_scatter_sibling(gl, f"rs_sibling_{i}{tag}")
        parts = [_add_pairs(gm, l, c_me, f"rs_add_{i}_{n}") for gm, l, (n, _, _) in zip(gl, lands, mats)]
        return _chips_exchange_start(parts, after, f"rs_chips_start_{i}{tag}") + (mats, f"{i}{tag}", False)

    def finish_reduce(fly, after):
        if fly[7]:
            parts, recv = _direct_exchange_wait(*fly[:4], after, f"rs_direct_wait_{fly[6]}")
            mine = me
        else:
            parts, recv = _chips_exchange_wait(*fly[:4], after, f"rs_chips_wait_{fly[6]}")
            mine = my_chip
        for (n, j, _), p, r in zip(fly[5], parts, recv):
            res[n] = _adamw([(p, mine), r], a[n], a["m_" + n], a["v_" + n], f"adamw_{n}_{j}", layer=j, into=res.get(n))

    def update_replicated(gathered, names, tail, label):
        zeros_tail = [jnp.zeros_like(t) for t in tail]
        packs = _adamw([gathered], _pack([a[n] for n in names] + zeros_tail), _pack([a["m_" + n] for n in names] + zeros_tail),
                       _pack([a["v_" + n] for n in names] + zeros_tail), label)
        flat = [t.reshape(-1) for t in packs]
        off = 0
        for n in names:
            res[n] = [f[off:off + a[n].size].reshape(a[n].shape) for f in flat]
            off += a[n].size
        sums = []
        for t in tail:
            sums.append(flat[0][off:off + t.size].reshape(t.shape))
            off += t.size
        return sums, packs[0]

    stats, sgrads, flying, early = [None] * L, {}, [], None
    for i in reversed(range(L)):
        mats = _layer_matrices(i)
        mod_i = mod8[i]
        for fly in flying:
            mod_i = mod_i + fly[4][0, 0]
        if early is not None:
            mod_i = mod_i + early[4][0, 0]
        sent = []

        def hook(g_mlp, i=i, mats=mats, dx=dx, sent=sent):
            sent.append(start_reduce(i, mats[-2:], g_mlp, dx, "a"))
            return sent[0][4][0, 0]

        dx, stats[i], g, sgr = _layer_backward(i, dx, saved[i], mod_i, gmix[i], gmlp[i], wts[i], small, tables, hook)
        sgrads.update(sgr)
        for fly in flying:
            finish_reduce(fly, dx)
        flying = sent + [start_reduce(i, mats[:-2], g, dx)]
        if i == 1:
            early_tail = [sgrads["mla_q_norm_g"]]
            early = _small_gather_start(_pack([sgrads[n] for n in REPLICATED_EARLY] + early_tail), me, flying[-1][4],
                                        "gather_small_start")
    (g_qg,), _ = update_replicated(_small_gather_wait(*early[:4], dx, "gather_small_wait"), REPLICATED_EARLY, early_tail,
                                   "adamw_replicated_early")

    sg = {"ada_b": jnp.stack([s[0:6] for s in stats]), "norm_mix_g": jnp.stack([s[6] for s in stats]),
          "norm_mlp_g": jnp.stack([s[7] for s in stats]), "final_g": loss_acc[0]}
    ps_grad = jnp.concatenate([sgrads[f"pool_scale_{j}"] for j in range(pool_scale.shape[0])])
    tail = [ps_grad, loss_acc[2, :LANES]]
    packed = _pack([sg[n] for n in REPLICATED] + tail) + flying[-1][4][0, 0]
    gathered = _small_all_gather(packed, "gather_small")
    (g_ps, loss_lanes), g_p = update_replicated(gathered, REPLICATED, tail, "adamw_replicated")
    for fly in flying:
        finish_reduce(fly, g_p)
    loss = loss_lanes[0]
    res["pool_scale"] = _adamw([lax.dynamic_slice_in_dim(g_ps, me * ps_w, ps_w, axis=1)], pool_scale, m_pool_scale,
                               v_pool_scale, "adamw_pool_scale")
    qg_w = mla_q_norm_g.shape[1]
    res["mla_q_norm_g"] = _adamw([lax.dynamic_slice_in_dim(g_qg, me * qg_w, qg_w, axis=1)], mla_q_norm_g, m_mla_q_norm_g,
                                 v_mla_q_norm_g, "adamw_q_norm_g")

    n_mod = L * 6 * D
    dmod_all = gathered.reshape(NDEV, -1)[:, :n_mod].reshape(NDEV, L, 6 * D)
    dmod_mine = lax.dynamic_slice_in_dim(dmod_all, me * Wc, Wc, axis=2).transpose(1, 0, 2)
    res["ada_w"] = _adamw([_ada_bwd(c_all, dmod_mine, "ada_bwd")], ada_w, m_ada_w, v_ada_w, "adamw_ada_w")

    return (loss, dx.reshape(x.shape), *[res[n][k] for k in range(4) for n in WEIGHT_NAMES])
```

```python
import functools
import math

import jax
import jax.numpy as jnp
import numpy as np
from jax import lax
from jax.experimental import pallas as pl
from jax.experimental.pallas import tpu as pltpu

F32 = jnp.float32
BF16 = jnp.bfloat16
SDS = jax.ShapeDtypeStruct
MESH = pl.DeviceIdType.MESH

NDEV = 8
V7X_VMEM_LIMIT = 56 << 20
LANES = 128
RMS_EPS = 1e-6
LN_EPS = 1e-5
POOL_WINDOWS = (2, 4, 8, 16)
HALO = 16
SGU_CHUNK = 128
SGU_HEAD = 128
MLA_NOPE, MLA_ROPE, MLA_V = 128, 64, 128
MLA_Q_LORA, MLA_KV_LORA = 256, 128
MLA_HEAD_PAD = 256
ROPE_THETA = 10000.0
SM_SCALE = (MLA_NOPE + MLA_ROPE) ** -0.5
LOG2E, LN2 = 1.0 / math.log(2.0), math.log(2.0)
Q_SCALE = SM_SCALE * LOG2E
NEG = -1e30
ADAM_LR, ADAM_B1, ADAM_B2, ADAM_EPS, ADAM_WD, ADAM_STEP = 0.001, 0.9, 0.999, 1e-08, 0.01, 10
INV_SQRT2 = 1.0 / math.sqrt(2.0)
INV_SQRT_2PI = 1.0 / math.sqrt(2.0 * math.pi)
SH1, SC1, G1, SH2, SC2, G2 = 0, 1, 2, 3, 4, 5


def _params(sem=None, vmem=V7X_VMEM_LIMIT):
    return pltpu.CompilerParams(dimension_semantics=sem, vmem_limit_bytes=vmem)


def _resident(shape):
    nd = len(shape)
    return pl.BlockSpec(shape, lambda *_: (0,) * nd, pipeline_mode=pl.Buffered(1))


def _rows1(tm, w):
    return pl.BlockSpec((tm, w), lambda i: (i, 0))


def _rms(x):
    r = lax.rsqrt(jnp.mean(x * x, axis=-1, keepdims=True) + RMS_EPS)
    return x * r, r


def _colsum(v):
    return jnp.sum(v, axis=0, keepdims=True)


def _normmod_bwd(dh, n, r, a):
    dn = dh * a
    return r * (dn - n * jnp.mean(dn * n, axis=-1, keepdims=True))


def _dot(a, b):
    return jnp.dot(a, b, preferred_element_type=F32)


def _dot_nt(a, b):
    return lax.dot_general(a, b, (((1,), (1,)), ((), ())), preferred_element_type=F32)


def _dot_tn(a, b):
    return lax.dot_general(a, b, (((0,), (0,)), ((), ())), preferred_element_type=F32)


def _gelu(x):
    return 0.5 * x * (1.0 + lax.erf(x * INV_SQRT2))


def _gelu_grad(x):
    return 0.5 * (1.0 + lax.erf(x * INV_SQRT2)) + x * jnp.exp(-0.5 * x * x) * INV_SQRT_2PI


def _swap_halves(v):
    lane = lax.broadcasted_iota(jnp.int32, v.shape, 1)
    half = MLA_ROPE // 2
    return jnp.where(lane < half, pltpu.roll(v, LANES - half, 1),
                     jnp.where(lane < MLA_ROPE, pltpu.roll(v, half, 1), 0.0))


def _mlp_up(x1, mod, gn, w1, tm, tn, name):
    S, D = x1.shape
    Fh = w1.shape[1]

    def body(x_ref, mod_ref, gn_ref, w_ref, h_ref, r_ref):
        n, _ = _rms(x_ref[...])
        a = gn_ref[...] * (1.0 + mod_ref[SC2:SC2 + 1, :])
        h = (n * a + mod_ref[SH2:SH2 + 1, :]).astype(BF16)
        h_ref[...] = h
        for j in range(Fh // tn):
            cols = slice(j * tn, (j + 1) * tn)
            r_ref[:, cols] = jnp.maximum(_dot(h, w_ref[:, cols]), 0.0).astype(BF16)

    return pl.pallas_call(
        body, name=name, grid=(S // tm,),
        in_specs=[_rows1(tm, D), _resident(mod.shape), _resident(gn.shape), _resident(w1.shape)],
        out_specs=[_rows1(tm, D), _rows1(tm, Fh)],
        out_shape=[SDS((S, D), BF16), SDS((S, Fh), BF16)],
        compiler_params=_params(("parallel",)),
    )(x1, mod, gn, w1)


def _mlp_down(r, w2, x1, mod, tm, name):
    S, Fh = r.shape
    D = w2.shape[1]

    def body(r_ref, w_ref, x_ref, mod_ref, x2_ref, o_ref):
        rv = r_ref[...]
        o = _dot(rv * rv, w_ref[...])
        o_ref[...] = o.astype(BF16)
        x2_ref[...] = x_ref[...] + mod_ref[G2:G2 + 1, :] * o

    return pl.pallas_call(
        body, name=name, grid=(S // tm,),
        in_specs=[_rows1(tm, Fh), _resident(w2.shape), _rows1(tm, D), _resident(mod.shape)],
        out_specs=[_rows1(tm, D), _rows1(tm, D)],
        out_shape=[SDS((S, D), F32), SDS((S, D), BF16)],
        compiler_params=_params(("parallel",)),
    )(r, w2, x1, mod)


def _mlp_bwd_a(dx2, o, mod, w2, r, tm, tn, name):
    S, D = dx2.shape
    Fh = r.shape[1]

    def body(dx_ref, o_ref, mod_ref, w_ref, r_ref, da_ref, do_ref, st_ref):
        @pl.when(pl.program_id(0) == 0)
        def _():
            st_ref[...] = jnp.zeros_like(st_ref)

        dx = dx_ref[...]
        d_o = (dx * mod_ref[G2:G2 + 1, :]).astype(BF16)
        do_ref[...] = d_o
        st_ref[G2:G2 + 1, :] += _colsum(dx * o_ref[...].astype(F32))
        for j in range(Fh // tn):
            cols = slice(j * tn, (j + 1) * tn)
            dz = _dot_nt(d_o, w_ref[cols, :])
            da_ref[:, cols] = (dz * (2.0 * r_ref[:, cols].astype(F32))).astype(BF16)

    return pl.pallas_call(
        body, name=name, grid=(S // tm,),
        in_specs=[_rows1(tm, D), _rows1(tm, D), _resident(mod.shape), _resident(w2.shape), _rows1(tm, Fh)],
        out_specs=[_rows1(tm, Fh), _rows1(tm, D), pl.BlockSpec((8, D), lambda i: (0, 0))],
        out_shape=[SDS((S, Fh), BF16), SDS((S, D), BF16), SDS((8, D), F32)],
        compiler_params=_params(("arbitrary",)),
    )(dx2, o, mod, w2, r)


def _mlp_bwd_b(d_a, w1, x1, dx2, mod, gn, tm, name):
    S, Fh = d_a.shape
    D = w1.shape[0]

    def body(da_ref, w_ref, x_ref, dx_ref, mod_ref, gn_ref, dx1_ref, st_ref):
        @pl.when(pl.program_id(0) == 0)
        def _():
            st_ref[...] = jnp.zeros_like(st_ref)

        dh = _dot_nt(da_ref[...], w_ref[...])
        n, rr = _rms(x_ref[...])
        gn_v = gn_ref[...]
        sc1p = 1.0 + mod_ref[SC2:SC2 + 1, :]
        t = _colsum(dh * n)
        st_ref[SH2:SH2 + 1, :] += _colsum(dh)
        st_ref[SC2:SC2 + 1, :] += t * gn_v
        st_ref[6:7, :] += t * sc1p
        dx1_ref[...] = dx_ref[...] + _normmod_bwd(dh, n, rr, gn_v * sc1p)

    return pl.pallas_call(
        body, name=name, grid=(S // tm,),
        in_specs=[_rows1(tm, Fh), _resident(w1.shape), _rows1(tm, D), _rows1(tm, D), _resident(mod.shape),
                  _resident(gn.shape)],
        out_specs=[_rows1(tm, D), pl.BlockSpec((8, D), lambda i: (0, 0))],
        out_shape=[SDS((S, D), F32), SDS((8, D), F32)],
        compiler_params=_params(("arbitrary",)),
    )(d_a, w1, x1, dx2, mod, gn)


def _mm_tn(a, g, tk, tn, name, square_a=False, col_shards=False):
    S, K1 = a.shape
    N = g.shape[1]
    w = N // NDEV
    per = tn // w if col_shards else 1

    def body(a_ref, g_ref, o_ref):
        av = a_ref[...]
        if square_a:
            av = av * av
        res = _dot_tn(av, g_ref[...]).astype(BF16)
        if col_shards:
            for s in range(per):
                o_ref[s] = res[:, s * w:(s + 1) * w]
        else:
            o_ref[...] = res

    if col_shards:
        out_spec, out_shape = pl.BlockSpec((per, tk, w), lambda i, j: (j, i, 0)), SDS((NDEV, K1, w), BF16)
    else:
        out_spec, out_shape = pl.BlockSpec((tk, tn), lambda i, j: (i, j)), SDS((K1, N), BF16)
    return pl.pallas_call(
        body, name=name, grid=(K1 // tk, N // tn),
        in_specs=[pl.BlockSpec((S, tk), lambda i, j: (0, i)), pl.BlockSpec((S, tn), lambda i, j: (0, j))],
        out_specs=out_spec, out_shape=out_shape,
        compiler_params=_params(("parallel", "parallel")),
    )(a, g)


def _pool_h_ext(x_ref, xp_ref, mod_ref, gn_ref, i, tm):
    ext = jnp.concatenate([xp_ref[...], x_ref[...]], axis=0)
    n, r = _rms(ext)
    a = gn_ref[...] * (1.0 + mod_ref[SC1:SC1 + 1, :])
    h = n * a + mod_ref[SH1:SH1 + 1, :]
    row = lax.broadcasted_iota(jnp.int32, (tm + HALO, 1), 0)
    h = jnp.where(jnp.logical_and(i == 0, row < HALO), 0.0, h)
    return h, n[HALO:], r[HALO:], a


def _trailing_sum(v, win):
    k = 1
    while k < win:
        v = v + pltpu.roll(v, k, 0)
        k *= 2
    return v


def _leading_sum(v, win):
    k = 1
    while k < win:
        v = v + pltpu.roll(v, v.shape[0] - k, 0)
        k *= 2
    return v


def _pool_fwd(x, mod, gn, pw, ps, tm, name):
    S, D = x.shape
    C = D // len(POOL_WINDOWS)
    hb = tm // HALO

    def body(x_ref, xp_ref, mod_ref, gn_ref, pw_ref, ps_ref, x1_ref):
        i = pl.program_id(0)
        h, _, _, _ = _pool_h_ext(x_ref, xp_ref, mod_ref, gn_ref, i, tm)
        t1 = (i * tm + lax.broadcasted_iota(jnp.int32, (tm, 1), 0)).astype(F32) + 1.0
        for g, win in enumerate(POOL_WINDOWS):
            cols = slice(g * C, (g + 1) * C)
            hg = h[:, cols]
            inv = 1.0 / jnp.minimum(t1, float(win))
            pooled = (_trailing_sum(hg, win)[HALO:] * inv - hg[HALO:]).astype(BF16)
            y = _dot(pooled, pw_ref[g]) * ps_ref[:, cols]
            x1_ref[:, cols] = x_ref[:, cols] + mod_ref[G1:G1 + 1, cols] * y

    return pl.pallas_call(
        body, name=name, grid=(S // tm,),
        in_specs=[_rows1(tm, D), pl.BlockSpec((HALO, D), lambda i: (jnp.maximum(i * hb - 1, 0), 0)),
                  _resident(mod.shape), _resident(gn.shape), _resident(pw.shape), _resident(ps.shape)],
        out_specs=_rows1(tm, D),
        out_shape=SDS((S, D), F32),
        compiler_params=_params(("parallel",)),
    )(x, x, mod, gn, pw, ps)


def _pool_bwd(x, dx1, mod, gn, pw, ps, tm, name):
    S, D = x.shape
    G = len(POOL_WINDOWS)
    C = D // G
    hb = tm // HALO
    nt = S // tm

    def body(x_ref, xp_ref, d1_ref, dn_ref, mod_ref, gn_ref, pw_ref, ps_ref, dx_ref, st_ref, dpw_ref):
        i = pl.program_id(0)

        @pl.when(i == 0)
        def _():
            st_ref[...] = jnp.zeros_like(st_ref)
            dpw_ref[...] = jnp.zeros_like(dpw_ref)

        h, n, rr, a = _pool_h_ext(x_ref, xp_ref, mod_ref, gn_ref, i, tm)
        g1 = mod_ref[G1:G1 + 1, :]
        ps_v = ps_ref[...]
        d1 = d1_ref[...]
        d1n = jnp.where(i == nt - 1, 0.0, dn_ref[...])
        dyr = (jnp.concatenate([d1, d1n], axis=0) * (g1 * ps_v)).astype(BF16)
        t1 = (i * tm + lax.broadcasted_iota(jnp.int32, (tm + HALO, 1), 0)).astype(F32) + 1.0
        parts = []
        for g, win in enumerate(POOL_WINDOWS):
            cols = slice(g * C, (g + 1) * C)
            hg = h[:, cols]
            inv = 1.0 / jnp.minimum(t1, float(win))
            pooled = (_trailing_sum(hg, win)[HALO:] * inv[:tm] - hg[HALO:]).astype(BF16)
            yraw = _dot(pooled, pw_ref[g])
            st_ref[G1:G1 + 1, cols] += _colsum(d1[:, cols] * (yraw * ps_v[:, cols]))
            st_ref[4:5, cols] += _colsum(d1[:, cols] * g1[:, cols] * yraw)
            dpw_ref[g] += _dot_tn(pooled, dyr[:tm, cols])
            dpool = _dot_nt(dyr[:, cols], pw_ref[g])
            parts.append(_leading_sum(dpool * inv, win)[:tm] - dpool[:tm])
        dh = jnp.concatenate(parts, axis=1)
        t = _colsum(dh * n)
        st_ref[SH1:SH1 + 1, :] += _colsum(dh)
        st_ref[SC1:SC1 + 1, :] += t * gn_ref[...]
        st_ref[3:4, :] += t * (1.0 + mod_ref[SC1:SC1 + 1, :])
        dx_ref[...] = d1 + _normmod_bwd(dh, n, rr, a)

    return pl.pallas_call(
        body, name=name, grid=(nt,),
        in_specs=[_rows1(tm, D), pl.BlockSpec((HALO, D), lambda i: (jnp.maximum(i * hb - 1, 0), 0)),
                  _rows1(tm, D), pl.BlockSpec((HALO, D), lambda i: (jnp.minimum((i + 1) * hb, S // HALO - 1), 0)),
                  _resident(mod.shape), _resident(gn.shape), _resident(pw.shape), _resident(ps.shape)],
        out_specs=[_rows1(tm, D), pl.BlockSpec((8, D), lambda i: (0, 0)), pl.BlockSpec((G, C, C), lambda i: (0, 0, 0))],
        out_shape=[SDS((S, D), F32), SDS((8, D), F32), SDS((G, C, C), F32)],
        compiler_params=_params(("arbitrary",)),
    )(x, x, dx1, dx1, mod, gn, pw, ps)


def _tril_bf16(w):
    row = lax.broadcasted_iota(jnp.int32, w.shape, 0)
    col = lax.broadcasted_iota(jnp.int32, w.shape, 1)
    return jnp.where(col <= row, w, 0.0).astype(BF16)


def _sgu_front(pre, lng_ref, lnb_ref, W):
    z = _gelu(pre)
    u, v = z[:, :W], z[:, W:]
    mu = jnp.mean(v, axis=-1, keepdims=True)
    xc = v - mu
    rstd = lax.rsqrt(jnp.mean(xc * xc, axis=-1, keepdims=True) + LN_EPS)
    vhat = xc * rstd
    return u, vhat, rstd, vhat * lng_ref[...] + lnb_ref[...]


def _sgu_mix(vn, ws_ref, bst_ref, mix_s, tm, W):
    for hd in range(W // SGU_HEAD):
        wm = _tril_bf16(ws_ref[hd])
        bcol = bst_ref[:, hd:hd + 1]
        for ci in range(tm // SGU_CHUNK):
            rs, cs = slice(ci * SGU_CHUNK, (ci + 1) * SGU_CHUNK), slice(hd * SGU_HEAD, (hd + 1) * SGU_HEAD)
            mix_s[rs, cs] = _dot(wm, vn[rs, cs].astype(BF16)) + bcol


def _sgu_fwd(x, mod, gn, w_in, lng, lnb, ws, bst, w_out, tm, name):
    S, D = x.shape
    W = w_out.shape[0]

    def body(x_ref, mod_ref, gn_ref, win_ref, lng_ref, lnb_ref, ws_ref, bst_ref, wout_ref,
             x1_ref, h_ref, pre_ref, y_ref, mix_s):
        n, _ = _rms(x_ref[...])
        a = gn_ref[...] * (1.0 + mod_ref[SC1:SC1 + 1, :])
        h = (n * a + mod_ref[SH1:SH1 + 1, :]).astype(BF16)
        h_ref[...] = h
        pre = _dot(h, win_ref[...])
        pre_ref[...] = pre.astype(BF16)
        u, _, _, vn = _sgu_front(pre, lng_ref, lnb_ref, W)
        _sgu_mix(vn, ws_ref, bst_ref, mix_s, tm, W)
        y = _dot((u * mix_s[...]).astype(BF16), wout_ref[...])
        y_ref[...] = y.astype(BF16)
        x1_ref[...] = x_ref[...] + mod_ref[G1:G1 + 1, :] * y

    return pl.pallas_call(
        body, name=name, grid=(S // tm,),
        in_specs=[_rows1(tm, D), _resident(mod.shape), _resident(gn.shape), _resident(w_in.shape),
                  _resident(lng.shape), _resident(lnb.shape), _resident(ws.shape), _resident(bst.shape),
                  _resident(w_out.shape)],
        out_specs=[_rows1(tm, D), _rows1(tm, D), _rows1(tm, 2 * W), _rows1(tm, D)],
        out_shape=[SDS((S, D), F32), SDS((S, D), BF16), SDS((S, 2 * W), BF16), SDS((S, D), BF16)],
        scratch_shapes=[pltpu.VMEM((tm, W), F32)],
        compiler_params=_params(("parallel",)),
    )(x, mod, gn, w_in, lng, lnb, ws, bst, w_out)


def _sgu_bwd(x, dx1, pre, y, mod, gn, w_in, lng, lnb, ws, bst, w_out, tm, name):
    S, D = x.shape
    W = w_out.shape[0]
    H = W // SGU_HEAD
    nt = S // tm

    def body(x_ref, d1_ref, pre_ref, y_ref, mod_ref, gn_ref, win_ref, lng_ref, lnb_ref, ws_ref, bst_ref, wout_ref,
             dx_ref, dy_ref, gt_ref, dpre_ref, st_ref, dws_ref, dbs_ref, mix_s, dvn_s):
        i = pl.program_id(0)

        @pl.when(i == 0)
        def _():
            st_ref[...] = jnp.zeros_like(st_ref)
            dws_ref[...] = jnp.zeros_like(dws_ref)
            dbs_ref[...] = jnp.zeros_like(dbs_ref)

        d1 = d1_ref[...]
        pre = pre_ref[...].astype(F32)
        u, vhat, rstd, vn = _sgu_front(pre, lng_ref, lnb_ref, W)
        _sgu_mix(vn, ws_ref, bst_ref, mix_s, tm, W)
        mixed = mix_s[...]
        gt_ref[...] = (u * mixed).astype(BF16)
        dyb = (d1 * mod_ref[G1:G1 + 1, :]).astype(BF16)
        dy_ref[...] = dyb
        st_ref[G1:G1 + 1, :] += _colsum(d1 * y_ref[...].astype(F32))
        dgt = _dot_nt(dyb, wout_ref[...])
        du = dgt * mixed
        dmix = dgt * u
        for hd in range(H):
            wm = _tril_bf16(ws_ref[hd])
            for ci in range(tm // SGU_CHUNK):
                rs, cs = slice(ci * SGU_CHUNK, (ci + 1) * SGU_CHUNK), slice(hd * SGU_HEAD, (hd + 1) * SGU_HEAD)
                dm = dmix[rs, cs]
                dmb = dm.astype(BF16)
                dbs_ref[hd] += jnp.broadcast_to(jnp.sum(dm, axis=1, keepdims=True), (SGU_CHUNK, LANES))
                dws_ref[hd] += _dot_nt(dmb, vn[rs, cs].astype(BF16))
                dvn_s[rs, cs] = _dot_tn(wm, dmb)
        dvn = dvn_s[...]
        st_ref[4:5, :] += _colsum(dvn * vhat)
        st_ref[5:6, :] += _colsum(dvn)
        dvh = dvn * lng_ref[...]
        dv = rstd * (dvh - jnp.mean(dvh, axis=-1, keepdims=True) - vhat * jnp.mean(dvh * vhat, axis=-1, keepdims=True))
        dpre_u = (du * _gelu_grad(pre[:, :W])).astype(BF16)
        dpre_v = (dv * _gelu_grad(pre[:, W:])).astype(BF16)
        dpre_ref[:, :W] = dpre_u
        dpre_ref[:, W:] = dpre_v
        dh = _dot_nt(dpre_u, win_ref[:, :W]) + _dot_nt(dpre_v, win_ref[:, W:])
        n, rr = _rms(x_ref[...])
        gn_v = gn_ref[...]
        sc1p = 1.0 + mod_ref[SC1:SC1 + 1, :]
        t = _colsum(dh * n)
        st_ref[SH1:SH1 + 1, :] += _colsum(dh)
        st_ref[SC1:SC1 + 1, :] += t * gn_v
        st_ref[3:4, :] += t * sc1p
        dx_ref[...] = d1 + _normmod_bwd(dh, n, rr, gn_v * sc1p)

        @pl.when(i == nt - 1)
        def _():
            for hd in range(H):
                row = lax.broadcasted_iota(jnp.int32, (SGU_CHUNK, SGU_CHUNK), 0)
                col = lax.broadcasted_iota(jnp.int32, (SGU_CHUNK, SGU_CHUNK), 1)
                dws_ref[hd] = jnp.where(col <= row, dws_ref[hd], 0.0)

    return pl.pallas_call(
        body, name=name, grid=(nt,),
        in_specs=[_rows1(tm, D), _rows1(tm, D), _rows1(tm, 2 * W), _rows1(tm, D), _resident(mod.shape),
                  _resident(gn.shape), _resident(w_in.shape), _resident(lng.shape), _resident(lnb.shape),
                  _resident(ws.shape), _resident(bst.shape), _resident(w_out.shape)],
        out_specs=[_rows1(tm, D), _rows1(tm, D), _rows1(tm, W), _rows1(tm, 2 * W),
                   pl.BlockSpec((8, D), lambda i: (0, 0)), pl.BlockSpec((H, SGU_CHUNK, SGU_CHUNK), lambda i: (0, 0, 0)),
                   pl.BlockSpec((H, SGU_CHUNK, LANES), lambda i: (0, 0, 0))],
        out_shape=[SDS((S, D), F32), SDS((S, D), BF16), SDS((S, W), BF16), SDS((S, 2 * W), BF16),
                   SDS((8, D), F32), SDS((H, SGU_CHUNK, SGU_CHUNK), F32), SDS((H, SGU_CHUNK, LANES), F32)],
        scratch_shapes=[pltpu.VMEM((tm, W), F32), pltpu.VMEM((tm, W), F32)],
        compiler_params=_params(("arbitrary",)),
    )(x, dx1, pre, y, mod, gn, w_in, lng, lnb, ws, bst, w_out)


def _mla_lat(x, mod, gn, wd, qg, kvg, cos_t, sin_t, tm, name):
    S, D = x.shape
    LW = wd.shape[1]
    QL, KL = MLA_Q_LORA, MLA_KV_LORA

    def body(x_ref, mod_ref, gn_ref, wd_ref, qg_ref, kvg_ref, c_ref, s_ref, h_ref, lat_ref, cq_ref, ckv_ref, kr_ref):
        n, _ = _rms(x_ref[...])
        a = gn_ref[...] * (1.0 + mod_ref[SC1:SC1 + 1, :])
        h = (n * a + mod_ref[SH1:SH1 + 1, :]).astype(BF16)
        h_ref[...] = h
        lat = _dot(h, wd_ref[...])
        lat_ref[...] = lat
        nq, _ = _rms(lat[:, :QL])
        cq_ref[...] = (nq * qg_ref[...]).astype(BF16)
        nkv, _ = _rms(lat[:, QL:QL + KL])
        ckv_ref[...] = (nkv * kvg_ref[...]).astype(BF16)
        kr = lat[:, QL + KL:]
        kr_ref[...] = (kr * c_ref[...] + _swap_halves(kr) * s_ref[...]).astype(BF16)

    return pl.pallas_call(
        body, name=name, grid=(S // tm,),
        in_specs=[_rows1(tm, D), _resident(mod.shape), _resident(gn.shape), _resident(wd.shape), _resident(qg.shape),
                  _resident(kvg.shape), _rows1(tm, LANES), _rows1(tm, LANES)],
        out_specs=[_rows1(tm, D), _rows1(tm, LW), _rows1(tm, QL), _rows1(tm, KL), _rows1(tm, LANES)],
        out_shape=[SDS((S, D), BF16), SDS((S, LW), F32), SDS((S, QL), BF16), SDS((S, KL), BF16), SDS((S, LANES), BF16)],
        compiler_params=_params(("parallel",)),
    )(x, mod, gn, wd, qg, kvg, cos_t, sin_t)


def _mla_qkv(cq, ckv, krp, wq, wukv, cos_t, sin_t, tm, name):
    S = cq.shape[0]
    H = wq.shape[1] // MLA_HEAD_PAD
    HP = MLA_HEAD_PAD

    def body(cq_ref, ckv_ref, kr_ref, wq_ref, wkv_ref, c_ref, s_ref, q_ref, k_ref, v_ref):
        q = _dot(cq_ref[...], wq_ref[...])
        kv = _dot(ckv_ref[...], wkv_ref[...])
        cv, sv, krv = c_ref[...], s_ref[...], kr_ref[...]
        for h in range(H):
            qr = q[:, h * HP + MLA_NOPE:(h + 1) * HP]
            q_ref[:, h * HP:h * HP + MLA_NOPE] = (q[:, h * HP:h * HP + MLA_NOPE] * Q_SCALE).astype(BF16)
            q_ref[:, h * HP + MLA_NOPE:(h + 1) * HP] = ((qr * cv + _swap_halves(qr) * sv) * Q_SCALE).astype(BF16)
            k_ref[:, h * HP:h * HP + MLA_NOPE] = kv[:, h * HP:h * HP + MLA_NOPE].astype(BF16)
            k_ref[:, h * HP + MLA_NOPE:(h + 1) * HP] = krv
            v_ref[:, h * MLA_V:(h + 1) * MLA_V] = kv[:, h * HP + MLA_NOPE:(h + 1) * HP].astype(BF16)

    return pl.pallas_call(
        body, name=name, grid=(S // tm,),
        in_specs=[_rows1(tm, MLA_Q_LORA), _rows1(tm, MLA_KV_LORA), _rows1(tm, LANES), _resident(wq.shape),
                  _resident(wukv.shape), _rows1(tm, LANES), _rows1(tm, LANES)],
        out_specs=[_rows1(tm, H * HP), _rows1(tm, H * HP), _rows1(tm, H * MLA_V)],
        out_shape=[SDS((S, H * HP), BF16), SDS((S, H * HP), BF16), SDS((S, H * MLA_V), BF16)],
        compiler_params=_params(("parallel",)),
    )(cq, ckv, krp, wq, wukv, cos_t, sin_t)


def _causal_mask(tq):
    row = lax.broadcasted_iota(jnp.int32, (tq, tq), 0)
    col = lax.broadcasted_iota(jnp.int32, (tq, tq), 1)
    return col <= row


def _tile_rows(j, tq):
    return slice(j * tq, (j + 1) * tq) if isinstance(j, int) else pl.ds(pl.multiple_of(j * tq, tq), tq)


def _attn_fwd(q, k, v, tq, name):
    S = q.shape[0]
    HP = MLA_HEAD_PAD
    H = q.shape[1] // HP
    nq = S // tq

    def body(q_ref, k_ref, v_ref, o_ref, lse_ref, v1):
        v1[:, :MLA_V] = v_ref[...]
        v1[:, MLA_V:] = jnp.ones((S, LANES), BF16)
        for i in range(nq):
            rows = slice(i * tq, (i + 1) * tq)
            qv = q_ref[rows, :]

            def step(j, carry, masked, qv=qv):
                m, acc = carry
                krows = _tile_rows(j, tq)
                s = _dot_nt(qv, k_ref[krows, :])
                if masked:
                    s = jnp.where(_causal_mask(tq), s, NEG)
                m_new = jnp.maximum(m, jnp.max(s, axis=1, keepdims=True))
                p = jnp.exp2(s - m_new)
                acc = jnp.exp2(m - m_new) * acc + _dot(p.astype(BF16), v1[krows, :])
                return m_new, acc

            carry = (jnp.full((tq, 1), NEG, F32), jnp.zeros((tq, MLA_V + LANES), F32))
            for j in range(i):
                carry = step(j, carry, False)
            m, acc = step(i, carry, True)
            l = acc[:, MLA_V:MLA_V + 1]
            o_ref[rows, :] = (acc[:, :MLA_V] / l).astype(BF16)
            lse_ref[0, rows, :] = jnp.broadcast_to(m + jnp.log2(l), (tq, LANES))

    return pl.pallas_call(
        body, name=name, grid=(H,),
        in_specs=[pl.BlockSpec((S, HP), lambda h: (0, h)), pl.BlockSpec((S, HP), lambda h: (0, h)),
                  pl.BlockSpec((S, MLA_V), lambda h: (0, h))],
        out_specs=[pl.BlockSpec((S, MLA_V), lambda h: (0, h)), pl.BlockSpec((1, S, LANES), lambda h: (h, 0, 0))],
        out_shape=[SDS((S, H * MLA_V), BF16), SDS((H, S, LANES), F32)],
        scratch_shapes=[pltpu.VMEM((S, MLA_V + LANES), BF16)],
        compiler_params=_params(("parallel",)),
    )(q, k, v)


def _attn_bwd(q, k, v, o, do, lse, cos_t, sin_t, tq, name):
    S = q.shape[0]
    HP = MLA_HEAD_PAD
    H = q.shape[1] // HP
    nq = S // tq

    def body(q_ref, k_ref, v_ref, o_ref, do_ref, lse_ref, c_ref, s_ref, dq_ref, dkv_ref, dkr_ref, dq_acc, dl_s):
        @pl.when(pl.program_id(0) == 0)
        def _():
            dkr_ref[...] = jnp.zeros_like(dkr_ref)

        dq_acc[...] = jnp.zeros_like(dq_acc)

        def delta_tile(i, _):
            rows = pl.ds(pl.multiple_of(i * tq, tq), tq)
            d = jnp.sum(do_ref[rows, :].astype(F32) * o_ref[rows, :].astype(F32), axis=1, keepdims=True)
            dl_s[rows, :] = jnp.broadcast_to(d, (tq, LANES))
            return 0

        lax.fori_loop(0, nq, delta_tile, 0)

        for j in range(nq):
            krows = slice(j * tq, (j + 1) * tq)
            kv_k = k_ref[krows, :]
            kv_v = v_ref[krows, :]

            def step(i, carry, masked, kv_k=kv_k, kv_v=kv_v):
                dk, dv = carry
                rows = _tile_rows(i, tq)
                qv = q_ref[rows, :]
                dov = do_ref[rows, :]
                s = _dot_nt(qv, kv_k)
                if masked:
                    s = jnp.where(_causal_mask(tq), s, NEG)
                p = jnp.exp2(s - lse_ref[0, rows, 0:1])
                dv = dv + _dot_tn(p.astype(BF16), dov)
                dp = _dot_nt(dov, kv_v)
                ds = (p * (dp - dl_s[rows, 0:1])).astype(BF16)
                dk = dk + _dot_tn(ds, qv)
                dq_acc[rows, :] += _dot(ds, kv_k)
                return dk, dv

            carry = step(j, (jnp.zeros((tq, HP), F32), jnp.zeros((tq, MLA_V), F32)), True)
            for i in range(j + 1, nq):
                carry = step(i, carry, False)
            dk, dv = carry
            dk = dk * LN2
            dkv_ref[krows, :MLA_NOPE] = dk[:, :MLA_NOPE].astype(BF16)
            dkv_ref[krows, MLA_NOPE:] = dv.astype(BF16)
            dkr_ref[krows, :] += dk[:, MLA_NOPE:]

        def out_tile(i, _):
            rows = pl.ds(pl.multiple_of(i * tq, tq), tq)
            dq = dq_acc[rows, :] * SM_SCALE
            dqr = dq[:, MLA_NOPE:]
            dq_ref[rows, :MLA_NOPE] = dq[:, :MLA_NOPE].astype(BF16)
            dq_ref[rows, MLA_NOPE:] = (dqr * c_ref[rows, :] + _swap_halves(dqr * s_ref[rows, :])).astype(BF16)
            return 0

        lax.fori_loop(0, nq, out_tile, 0)

    return pl.pallas_call(
        body, name=name, grid=(H,),
        in_specs=[pl.BlockSpec((S, HP), lambda h: (0, h)), pl.BlockSpec((S, HP), lambda h: (0, h)),
                  pl.BlockSpec((S, MLA_V), lambda h: (0, h)), pl.BlockSpec((S, MLA_V), lambda h: (0, h)),
                  pl.BlockSpec((S, MLA_V), lambda h: (0, h)), pl.BlockSpec((1, S, LANES), lambda h: (h, 0, 0)),
                  _resident(cos_t.shape), _resident(sin_t.shape)],
        out_specs=[pl.BlockSpec((S, HP), lambda h: (0, h)), pl.BlockSpec((S, HP), lambda h: (0, h)),
                   pl.BlockSpec((S, LANES), lambda h: (0, 0))],
        out_shape=[SDS((S, H * HP), BF16), SDS((S, H * HP), BF16), SDS((S, LANES), F32)],
        scratch_shapes=[pltpu.VMEM((S, HP), F32), pltpu.VMEM((S, LANES), F32)],
        compiler_params=_params(("arbitrary",)),
    )(q, k, v, o, do, lse, cos_t, sin_t)


def _mla_out(o, w_o, x, mod, tm, name):
    S, KO = o.shape
    D = w_o.shape[1]

    def body(o_ref, w_ref, x_ref, mod_ref, x1_ref, y_ref):
        y = _dot(o_ref[...], w_ref[...])
        y_ref[...] = y.astype(BF16)
        x1_ref[...] = x_ref[...] + mod_ref[G1:G1 + 1, :] * y

    return pl.pallas_call(
        body, name=name, grid=(S // tm,),
        in_specs=[_rows1(tm, KO), _resident(w_o.shape), _rows1(tm, D), _resident(mod.shape)],
        out_specs=[_rows1(tm, D), _rows1(tm, D)],
        out_shape=[SDS((S, D), F32), SDS((S, D), BF16)],
        compiler_params=_params(("parallel",)),
    )(o, w_o, x, mod)


def _mla_bwd_o(dx1, y, mod, w_o, tm, name):
    S, D = dx1.shape
    KO = w_o.shape[0]

    def body(d1_ref, y_ref, mod_ref, w_ref, dy_ref, do_ref, st_ref):
        @pl.when(pl.program_id(0) == 0)
        def _():
            st_ref[...] = jnp.zeros_like(st_ref)

        d1 = d1_ref[...]
        dyb = (d1 * mod_ref[G1:G1 + 1, :]).astype(BF16)
        dy_ref[...] = dyb
        st_ref[G1:G1 + 1, :] += _colsum(d1 * y_ref[...].astype(F32))
        do_ref[...] = _dot_nt(dyb, w_ref[...]).astype(BF16)

    return pl.pallas_call(
        body, name=name, grid=(S // tm,),
        in_specs=[_rows1(tm, D), _rows1(tm, D), _resident(mod.shape), _resident(w_o.shape)],
        out_specs=[_rows1(tm, D), _rows1(tm, KO), pl.BlockSpec((8, D), lambda i: (0, 0))],
        out_shape=[SDS((S, D), BF16), SDS((S, KO), BF16), SDS((8, D), F32)],
        compiler_params=_params(("arbitrary",)),
    )(dx1, y, mod, w_o)


def _mla_bwd_lat(dq, dkv, dkr, lat, x, dx1, mod, gn, qg, kvg, wq, wukv, wd, cos_t, sin_t, tm, name):
    S, D = x.shape
    LW = wd.shape[1]
    QL, KL = MLA_Q_LORA, MLA_KV_LORA

    def body(dq_ref, dkv_ref, dkr_ref, lat_ref, x_ref, d1_ref, mod_ref, gn_ref, qg_ref, kvg_ref, wq_ref, wkv_ref, wd_ref,
             c_ref, s_ref, dx_ref, dlat_ref, st_ref, dqg_ref, dkvg_ref):
        @pl.when(pl.program_id(0) == 0)
        def _():
            st_ref[...] = jnp.zeros_like(st_ref)
            dqg_ref[...] = jnp.zeros_like(dqg_ref)
            dkvg_ref[...] = jnp.zeros_like(dkvg_ref)

        lat = lat_ref[...]
        dcq = _dot_nt(dq_ref[...], wq_ref[...])
        nq, rq = _rms(lat[:, :QL])
        dqg_ref[0:1, :] += _colsum(dcq * nq)
        dlat_q = _normmod_bwd(dcq, nq, rq, qg_ref[...]).astype(BF16)
        dckv = _dot_nt(dkv_ref[...], wkv_ref[...])
        nkv, rkv = _rms(lat[:, QL:QL + KL])
        dkvg_ref[0:1, :] += _colsum(dckv * nkv)
        dlat_kv = _normmod_bwd(dckv, nkv, rkv, kvg_ref[...]).astype(BF16)
        dkr = dkr_ref[...]
        dlat_kr = (dkr * c_ref[...] + _swap_halves(dkr * s_ref[...])).astype(BF16)
        dlat_ref[:, :QL] = dlat_q
        dlat_ref[:, QL:QL + KL] = dlat_kv
        dlat_ref[:, QL + KL:] = dlat_kr
        dh = (_dot_nt(dlat_q, wd_ref[:, :QL]) + _dot_nt(dlat_kv, wd_ref[:, QL:QL + KL])
              + _dot_nt(dlat_kr, wd_ref[:, QL + KL:]))
        n, rr = _rms(x_ref[...])
        gn_v = gn_ref[...]
        sc1p = 1.0 + mod_ref[SC1:SC1 + 1, :]
        t = _colsum(dh * n)
        st_ref[SH1:SH1 + 1, :] += _colsum(dh)
        st_ref[SC1:SC1 + 1, :] += t * gn_v
        st_ref[3:4, :] += t * sc1p
        dx_ref[...] = d1_ref[...] + _normmod_bwd(dh, n, rr, gn_v * sc1p)

    HW = wq.shape[1]
    return pl.pallas_call(
        body, name=name, grid=(S // tm,),
        in_specs=[_rows1(tm, HW), _rows1(tm, HW), _rows1(tm, LANES), _rows1(tm, LW), _rows1(tm, D), _rows1(tm, D),
                  _resident(mod.shape), _resident(gn.shape), _resident(qg.shape), _resident(kvg.shape),
                  _resident(wq.shape), _resident(wukv.shape), _resident(wd.shape), _rows1(tm, LANES), _rows1(tm, LANES)],
        out_specs=[_rows1(tm, D), _rows1(tm, LW), pl.BlockSpec((8, D), lambda i: (0, 0)),
                   pl.BlockSpec((8, QL), lambda i: (0, 0)), pl.BlockSpec((8, KL), lambda i: (0, 0))],
        out_shape=[SDS((S, D), F32), SDS((S, LW), BF16), SDS((8, D), F32), SDS((8, QL), F32), SDS((8, KL), F32)],
        compiler_params=_params(("arbitrary",)),
    )(dq, dkv, dkr, lat, x, dx1, mod, gn, qg, kvg, wq, wukv, wd, cos_t, sin_t)


def _loss_head(x, tgt, fg, tm, name):
    S, D = x.shape
    nt = S // tm

    def body(x_ref, t_ref, g_ref, dx_ref, acc_ref):
        i = pl.program_id(0)

        @pl.when(i == 0)
        def _():
            acc_ref[...] = jnp.zeros_like(acc_ref)

        n, rr = _rms(x_ref[...])
        g = g_ref[...]
        err = n * g - t_ref[...]
        acc_ref[1:2, :] += _colsum(err * err) * (0.5 / D)
        dy = err * (1.0 / D)
        acc_ref[0:1, :] += _colsum(dy * n)
        dx_ref[...] = _normmod_bwd(dy, n, rr, g)

        @pl.when(i == nt - 1)
        def _():
            acc_ref[2:3, :] = jnp.broadcast_to(jnp.sum(acc_ref[1:2, :], axis=1, keepdims=True), (1, D))

    return pl.pallas_call(
        body, name=name, grid=(nt,),
        in_specs=[_rows1(tm, D), _rows1(tm, D), _resident(fg.shape)],
        out_specs=[_rows1(tm, D), pl.BlockSpec((8, D), lambda i: (0, 0))],
        out_shape=[SDS((S, D), F32), SDS((8, D), F32)],
        compiler_params=_params(("arbitrary",)),
    )(x, tgt, fg)


def _rope_tables(positions, S):
    inv_freq = ROPE_THETA ** (-jnp.arange(0, MLA_ROPE, 2, dtype=F32) / MLA_ROPE)
    ang = positions.reshape(S, 1).astype(F32) * inv_freq
    cos, sin = jnp.cos(ang), jnp.sin(ang)
    z = jnp.zeros((S, LANES - MLA_ROPE), F32)
    return jnp.concatenate([cos, cos, z], axis=1), jnp.concatenate([-sin, sin, z], axis=1)


def _pad_heads(w, per_head):
    K = w.shape[0]
    H = w.shape[1] // per_head
    w3 = w.reshape(K, H, per_head)
    return jnp.pad(w3, ((0, 0), (0, 0), (0, MLA_HEAD_PAD - per_head))).reshape(K, H * MLA_HEAD_PAD)


def _unpad_heads(w, per_head):
    K = w.shape[0]
    H = w.shape[1] // MLA_HEAD_PAD
    return w.reshape(K, H, MLA_HEAD_PAD)[:, :, :per_head].reshape(K, H * per_head)


def _tiles(S):
    return min(512, S), min(256, S), min(512, S)


def _layer_forward(i, x, mod, gmix, gmlp, w, small, tables):
    if True:
        S, D = x.shape
        tm, tms, tq = _tiles(S)
        cos_t, sin_t = tables
        mod, gmix, gmlp = {i: mod}, {i: gmix}, {i: gmlp}
        kind = i % 3
        sv = {"x": x}
        if kind == 0:
            x1 = _pool_fwd(x, mod[i], gmix[i], w["pool_w"], small["pool_scale"][i // 3], tm, f"pool_fwd_{i}")
        elif kind == 1:
            x1, sv["h"], sv["pre"], sv["y"] = _sgu_fwd(
                x, mod[i], gmix[i], w["sgu_w_in"], small["sgu_ln_g"], small["sgu_ln_b"], small["sgu_w_s"],
                small["sgu_b_s_t"], w["sgu_w_out"], tms, f"sgu_fwd_{i}")
        else:
            sv["h"], sv["lat"], sv["cq"], sv["ckv"], krp = _mla_lat(
                x, mod[i], gmix[i], w["mla_wd"], small["mla_q_norm_g"], small["mla_kv_norm_g"], cos_t, sin_t, tm,
                f"mla_lat_{i}")
            sv["q"], sv["k"], sv["v"] = _mla_qkv(sv["cq"], sv["ckv"], krp, w["mla_wq"], w["mla_w_ukv"], cos_t, sin_t,
                                                 tm, f"mla_qkv_{i}")
            sv["o"], sv["lse"] = _attn_fwd(sv["q"], sv["k"], sv["v"], tq, f"attn_fwd_{i}")
            x1, sv["y"] = _mla_out(sv["o"], w["mla_w_o"], x, mod[i], tm, f"mla_out_{i}")
        sv["x1"] = x1
        Fh = w["mlp_w1"].shape[1]
        sv["h2"], sv["r"] = _mlp_up(x1, mod[i], gmlp[i], w["mlp_w1"], tm, min(2048, Fh), f"mlp_up_{i}")
        if callable(w["mlp_w2"]):
            w["mlp_w2"] = w["mlp_w2"](sv["r"])
        x, sv["o2"] = _mlp_down(sv["r"], w["mlp_w2"], x1, mod[i], tm, f"mlp_down_{i}")
        return x, sv


def _layer_backward(i, dx, sv, mod, gmix, gmlp, w, small, tables, on_mlp_grads=None):
    if True:
        S, D = dx.shape
        tm, tms, tq = _tiles(S)
        cos_t, sin_t = tables
        kind = i % 3
        sgrads = {}
        Fh = w["mlp_w1"].shape[1]
        g = {}
        d_a, d_o, st_a = _mlp_bwd_a(dx, sv["o2"], mod, w["mlp_w2"], sv["r"], tm, min(2048, Fh), f"mlp_bwd_a_{i}")
        g["mlp_w2"] = _mm_tn(sv["r"], d_o, min(512, Fh), D, f"mlp_dw2_{i}", square_a=True)
        g["mlp_w1"] = _mm_tn(sv["h2"], d_a, D, min(512, Fh), f"mlp_dw1_{i}", col_shards=True)
        if on_mlp_grads is not None:
            mod = mod + on_mlp_grads({n: g.pop(n) for n in ("mlp_w1", "mlp_w2")})
        mod, gmix, gmlp = {i: mod}, {i: gmix}, {i: gmlp}
        dx1, st_b = _mlp_bwd_b(d_a, w["mlp_w1"], sv["x1"], dx, mod[i], gmlp[i], tm, f"mlp_bwd_b_{i}")
        if kind == 0:
            dx, st_m, dpw = _pool_bwd(sv["x"], dx1, mod[i], gmix[i], w["pool_w"], small["pool_scale"][i // 3], tm,
                                      f"pool_bwd_{i}")
            g["pool_w"] = dpw
            sgrads[f"pool_scale_{i // 3}"] = st_m[4:5]
        elif kind == 1:
            dx, dyb, gated, dpre, st_m, dws, dbs = _sgu_bwd(
                sv["x"], dx1, sv["pre"], sv["y"], mod[i], gmix[i], w["sgu_w_in"], small["sgu_ln_g"], small["sgu_ln_b"],
                small["sgu_w_s"], small["sgu_b_s_t"], w["sgu_w_out"], tms, f"sgu_bwd_{i}")
            W = gated.shape[1]
            g["sgu_w_out"] = _mm_tn(gated, dyb, min(512, W), D, f"sgu_dwout_{i}")
            g["sgu_w_in"] = _mm_tn(sv["h"], dpre, D, min(512, 2 * W), f"sgu_dwin_{i}", col_shards=True)
            sgrads["sgu_ln_g"], sgrads["sgu_ln_b"] = st_m[4:5], st_m[5:6]
            sgrads["sgu_w_s"], sgrads["sgu_b_s"] = dws, dbs[:, :, 0]
        else:
            dyb, do, st_o = _mla_bwd_o(dx1, sv["y"], mod[i], w["mla_w_o"], tm, f"mla_bwd_o_{i}")
            KO = do.shape[1]
            g["mla_w_o"] = _mm_tn(sv["o"], dyb, min(512, KO), D, f"mla_dwo_{i}")
            dq, dkv, dkr = _attn_bwd(sv["q"], sv["k"], sv["v"], sv["o"], do, sv["lse"], cos_t, sin_t, tq, f"attn_bwd_{i}")
            dx, dlat, st_m, dqg, dkvg = _mla_bwd_lat(
                dq, dkv, dkr, sv["lat"], sv["x"], dx1, mod[i], gmix[i], small["mla_q_norm_g"], small["mla_kv_norm_g"],
                w["mla_wq"], w["mla_w_ukv"], w["mla_wd"], cos_t, sin_t, tm, f"mla_bwd_lat_{i}")
            HW = dq.shape[1]
            g["mla_wq"] = _mm_tn(sv["cq"], dq, MLA_Q_LORA, min(1024, HW), f"mla_dwq_{i}")
            g["mla_w_ukv"] = _mm_tn(sv["ckv"], dkv, MLA_KV_LORA, min(1024, HW), f"mla_dwukv_{i}", col_shards=True)
            g["mla_wd"] = _mm_tn(sv["h"], dlat, D, dlat.shape[1], f"mla_dwd_{i}")
            st_m = jnp.concatenate([st_m[0:2], st_o[2:3], st_m[3:]], axis=0)
            sgrads["mla_q_norm_g"], sgrads["mla_kv_norm_g"] = dqg[0:1], dkvg[0:1]
        stats = jnp.concatenate([st_m[0:3], st_b[3:5], st_a[5:6], st_m[3:4], st_b[6:7]], axis=0)
        return dx, stats, g, sgrads


def _local_step(x, tgt, positions, mod, gmix, gmlp, fg, wts, small):
    S = x.shape[0]
    L = mod.shape[0]
    tables = _rope_tables(positions, S)
    saved = []
    for i in range(L):
        x, sv = _layer_forward(i, x, mod[i], gmix[i], gmlp[i], wts[i], small, tables)
        saved.append(sv)
    dx, loss_acc = _loss_head(x, tgt, fg, _tiles(S)[0], "loss_head")
    stats, grads, sgrads = [None] * L, [None] * L, {}
    for i in reversed(range(L)):
        dx, stats[i], grads[i], sg = _layer_backward(i, dx, saved[i], mod[i], gmix[i], gmlp[i], wts[i], small, tables)
        sgrads.update(sg)
    return loss_acc, dx, stats, grads, sgrads


HBM_SPEC = pl.BlockSpec(memory_space=pltpu.HBM)
VMEM_SPEC = pl.BlockSpec(memory_space=pltpu.VMEM)


def _my_place():
    return lax.axis_index("x"), lax.axis_index("y"), lax.axis_index("c")


def _flip(v, bit):
    return 1 - v if bit else v


def _small_all_gather(v, name):
    R, C = v.shape

    def body(x_ref, out_ref, send_sems, recv_sems):
        x, y, c = _my_place()
        me = 4 * x + 2 * y + c
        out_ref[me] = x_ref[...]
        sends = []
        for k in range(1, NDEV):
            peer = (_flip(x, k & 4), _flip(y, k & 2), _flip(c, k & 1))
            cp = pltpu.make_async_remote_copy(src_ref=x_ref, dst_ref=out_ref.at[me], send_sem=send_sems.at[k - 1],
                                              recv_sem=recv_sems.at[k - 1], device_id=peer, device_id_type=MESH)
            cp.start()
            sends.append(cp)
        for k in range(1, NDEV):
            src = 4 * _flip(x, k & 4) + 2 * _flip(y, k & 2) + _flip(c, k & 1)
            pltpu.make_async_remote_copy(src_ref=x_ref, dst_ref=out_ref.at[src], send_sem=send_sems.at[k - 1],
                                         recv_sem=recv_sems.at[k - 1], device_id=(x, y, c), device_id_type=MESH).wait_recv()
        for cp in sends:
            cp.wait_send()

    return pl.pallas_call(
        body, name=name, out_shape=SDS((NDEV, R, C), v.dtype), in_specs=[VMEM_SPEC], out_specs=VMEM_SPEC,
        scratch_shapes=[pltpu.SemaphoreType.DMA((NDEV - 1,)), pltpu.SemaphoreType.DMA((NDEV - 1,))],
        compiler_params=pltpu.CompilerParams(vmem_limit_bytes=V7X_VMEM_LIMIT),
    )(v)


def _slab(ref, axis, width, dev):
    idx = [slice(None)] * len(ref.shape)
    idx[axis] = pl.ds(pl.multiple_of(dev * width, width), width)
    return ref.at[tuple(idx)]


def _all_gather_group(shards, axes, after, name):
    nt = len(shards)
    out_shapes = [SDS(tuple(s * NDEV if a == ax else s for a, s in enumerate(sh.shape)), sh.dtype)
                  for sh, ax in zip(shards, axes)]

    def body(*refs):
        ins, outs = refs[:nt], refs[nt + 1:2 * nt + 1]
        send_sems, recv_sems, local_sems = refs[2 * nt + 1:]
        x, y, c = _my_place()
        me = 4 * x + 2 * y + c
        sibling = (x, y, 1 - c)
        chips = [(1 - x, y), (x, 1 - y), (1 - x, 1 - y)]

        def block(t, dev):
            return _slab(outs[t], axes[t], ins[t].shape[axes[t]], dev)

        def copy(t, k, dev, to, src=None):
            return pltpu.make_async_remote_copy(
                src_ref=block(t, dev) if src is None else src, dst_ref=block(t, dev), send_sem=send_sems.at[t, k],
                recv_sem=recv_sems.at[t, k], device_id=to, device_id_type=MESH)

        mine = [pltpu.make_async_copy(ins[t], block(t, me), local_sems.at[t]) for t in range(nt)]
        for cp in mine:
            cp.start()
        first = []
        for t in range(nt):
            first.append(copy(t, 0, me, sibling, src=ins[t]))
            first += [copy(t, 1 + j, me, (cx, cy, c), src=ins[t]) for j, (cx, cy) in enumerate(chips)]
        for cp in first:
            cp.start()
        passed = []
        for j, (cx, cy) in enumerate(chips):
            for t in range(nt):
                copy(t, 1 + j, 4 * cx + 2 * cy + c, (x, y, c)).wait_recv()
                cp = copy(t, 4 + j, 4 * cx + 2 * cy + c, sibling)
                cp.start()
                passed.append(cp)
        for t in range(nt):
            copy(t, 0, 4 * x + 2 * y + (1 - c), (x, y, c)).wait_recv()
        for j, (cx, cy) in enumerate(chips):
            for t in range(nt):
                copy(t, 4 + j, 4 * cx + 2 * cy + (1 - c), (x, y, c)).wait_recv()
        for cp in first + passed:
            cp.wait_send()
        for cp in mine:
            cp.wait()

    return pl.pallas_call(
        body, name=name, out_shape=out_shapes, in_specs=[HBM_SPEC] * nt + [ANY_SPEC], out_specs=[HBM_SPEC] * nt,
        scratch_shapes=[pltpu.SemaphoreType.DMA((nt, NDEV - 1)), pltpu.SemaphoreType.DMA((nt, NDEV - 1)),
                        pltpu.SemaphoreType.DMA((nt,))],
    )(*shards, after)


def _reduce_scatter_sibling(grads, name):
    nt = len(grads)
    NCH = NDEV // 2
    out_shapes = [SDS((NCH,) + gr.shape[1:], gr.dtype) for gr in grads]

    def body(*refs):
        ins, lands = refs[:nt], refs[nt:2 * nt]
        send_sems, recv_sems = refs[2 * nt:]
        x, y, c = _my_place()
        sends = []
        for t in range(nt):
            for k in range(NCH):
                cp = pltpu.make_async_remote_copy(
                    src_ref=ins[t].at[2 * k + (1 - c)], dst_ref=lands[t].at[k], send_sem=send_sems.at[t, k],
                    recv_sem=recv_sems.at[t, k], device_id=(x, y, 1 - c), device_id_type=MESH)
                cp.start()
                sends.append(cp)
        for cp in sends:
            cp.wait_recv()
        for cp in sends:
            cp.wait_send()

    return pl.pallas_call(
        body, name=name, out_shape=out_shapes, in_specs=[HBM_SPEC] * nt, out_specs=[HBM_SPEC] * nt,
        scratch_shapes=[pltpu.SemaphoreType.DMA((nt, NCH)), pltpu.SemaphoreType.DMA((nt, NCH))],
    )(*grads)


SEM_SPEC = pl.BlockSpec(memory_space=pltpu.SEMAPHORE)
ANY_SPEC = pl.BlockSpec(memory_space=pl.ANY)
SPLIT_PARAMS = pltpu.CompilerParams(has_side_effects=pltpu.SideEffectType.DATAFLOW_SIDE_EFFECTING)
TOKEN = SDS((8, LANES), F32)


def _in_hbm(arrays):
    return [pltpu.with_memory_space_constraint(v, pltpu.HBM) for v in arrays]


def _split_start(body, srcs, lands, after, n_sem, name):
    ns, nl = len(srcs), len(lands)
    bufs = list(srcs) + list(lands)
    res = pl.pallas_call(
        body, name=name,
        out_shape=(pltpu.SemaphoreType.DMA((ns * n_sem,)), pltpu.SemaphoreType.DMA((ns * n_sem,)),
                   *[pltpu.HBM(v.shape, v.dtype) for v in bufs], TOKEN),
        in_specs=[HBM_SPEC] * (ns + nl) + [ANY_SPEC],
        out_specs=(SEM_SPEC, SEM_SPEC, *[HBM_SPEC] * (ns + nl), VMEM_SPEC),
        input_output_aliases={t: 2 + t for t in range(ns + nl)}, compiler_params=SPLIT_PARAMS,
    )(*_in_hbm(bufs), after)
    return res[0], res[1], list(res[2:2 + ns]), list(res[2 + ns:2 + ns + nl]), res[-1]


def _split_wait(body, send_sems, recv_sems, srcs, lands, after, name):
    ns, nl = len(srcs), len(lands)
    bufs = list(srcs) + list(lands)
    res = pl.pallas_call(
        body, name=name, out_shape=tuple(pltpu.HBM(v.shape, v.dtype) for v in bufs),
        in_specs=[HBM_SPEC] * (ns + nl) + [SEM_SPEC, SEM_SPEC, ANY_SPEC], out_specs=tuple([HBM_SPEC] * (ns + nl)),
        input_output_aliases={t: t for t in range(ns + nl)}, compiler_params=SPLIT_PARAMS,
    )(*bufs, send_sems, recv_sems, after)
    return list(res[:ns]), list(res[ns:])


def _chips_exchange_start(parts, after, name):
    nt = len(parts)
    lands = [lax.empty((3,) + p.shape[1:], p.dtype) for p in parts]

    def body(*refs):
        ins, lnd = refs[:nt], refs[nt:2 * nt]
        send_sems, recv_sems, token = refs[2 * nt + 1], refs[2 * nt + 2], refs[-1]
        x, y, c = _my_place()
        for t in range(nt):
            for m in range(1, 4):
                px, py = _flip(x, m & 2), _flip(y, m & 1)
                pltpu.make_async_remote_copy(
                    src_ref=ins[t].at[2 * px + py], dst_ref=lnd[t].at[m - 1], send_sem=send_sems.at[3 * t + m - 1],
                    recv_sem=recv_sems.at[3 * t + m - 1], device_id=(px, py, c), device_id_type=MESH).start()
        token[...] = jnp.zeros_like(token)

    return _split_start(body, parts, lands, after, 3, name)


def _chips_exchange_wait(send_sems, recv_sems, parts, lands, after, name):
    nt = len(parts)

    def body(*refs):
        ins, lnd = refs[:nt], refs[nt:2 * nt]
        s_sems, r_sems = refs[2 * nt], refs[2 * nt + 1]
        x, y, c = _my_place()
        for t in range(nt):
            for m in range(1, 4):
                cp = pltpu.make_async_remote_copy(
                    src_ref=ins[t].at[0], dst_ref=lnd[t].at[m - 1], send_sem=s_sems.at[3 * t + m - 1],
                    recv_sem=r_sems.at[3 * t + m - 1], device_id=(x, y, c), device_id_type=MESH)
                cp.wait_send()
                cp.wait_recv()

    return _split_wait(body, send_sems, recv_sems, parts, lands, after, name)


def _direct_exchange_start(grads, after, name):
    nt = len(grads)
    lands = [lax.empty((NDEV - 1,) + gr.shape[1:], gr.dtype) for gr in grads]

    def body(*refs):
        ins, lnd = refs[:nt], refs[nt:2 * nt]
        send_sems, recv_sems, token = refs[2 * nt + 1], refs[2 * nt + 2], refs[-1]
        x, y, c = _my_place()
        for t in range(nt):
            for k in range(1, NDEV):
                px, py, pc = _flip(x, k & 4), _flip(y, k & 2), _flip(c, k & 1)
                pltpu.make_async_remote_copy(
                    src_ref=ins[t].at[4 * px + 2 * py + pc], dst_ref=lnd[t].at[k - 1],
                    send_sem=send_sems.at[(NDEV - 1) * t + k - 1], recv_sem=recv_sems.at[(NDEV - 1) * t + k - 1],
                    device_id=(px, py, pc), device_id_type=MESH).start()
        token[...] = jnp.zeros_like(token)

    return _split_start(body, grads, lands, after, NDEV - 1, name)


def _direct_exchange_wait(send_sems, recv_sems, grads, lands, after, name):
    nt = len(grads)

    def body(*refs):
        ins, lnd = refs[:nt], refs[nt:2 * nt]
        s_sems, r_sems = refs[2 * nt], refs[2 * nt + 1]
        x, y, c = _my_place()
        for t in range(nt):
            for k in range(1, NDEV):
                cp = pltpu.make_async_remote_copy(
                    src_ref=ins[t].at[0], dst_ref=lnd[t].at[k - 1], send_sem=s_sems.at[(NDEV - 1) * t + k - 1],
                    recv_sem=r_sems.at[(NDEV - 1) * t + k - 1], device_id=(x, y, c), device_id_type=MESH)
                cp.wait_send()
                cp.wait_recv()

    return _split_wait(body, send_sems, recv_sems, grads, lands, after, name)


def _place_own(shards, axes, name):
    nt = len(shards)
    out_shapes = [SDS(tuple(s * NDEV if a == ax else s for a, s in enumerate(sh.shape)), sh.dtype)
                  for sh, ax in zip(shards, axes)]

    def body(*refs):
        ins, outs, sems = refs[:nt], refs[nt:2 * nt], refs[2 * nt]
        x, y, c = _my_place()
        copies = [pltpu.make_async_copy(ins[t], _slab(outs[t], axes[t], ins[t].shape[axes[t]], 4 * x + 2 * y + c), sems.at[t])
                  for t in range(nt)]
        for cp in copies:
            cp.start()
        for cp in copies:
            cp.wait()

    return pl.pallas_call(
        body, name=name, out_shape=out_shapes, in_specs=[VMEM_SPEC] * nt, out_specs=[HBM_SPEC] * nt,
        scratch_shapes=[pltpu.SemaphoreType.DMA((nt,))],
        compiler_params=pltpu.CompilerParams(vmem_limit_bytes=V7X_VMEM_LIMIT),
    )(*shards)


def _small_gather_start(v, me, after, name):
    land = lax.dynamic_update_slice(lax.empty((NDEV,) + v.shape, v.dtype), v[None], (me, 0, 0))

    def body(*refs):
        src, lnd = refs[0], refs[1]
        send_sems, recv_sems, token = refs[3], refs[4], refs[-1]
        x, y, c = _my_place()
        for k in range(1, NDEV):
            peer = (_flip(x, k & 4), _flip(y, k & 2), _flip(c, k & 1))
            pltpu.make_async_remote_copy(src_ref=src, dst_ref=lnd.at[4 * x + 2 * y + c], send_sem=send_sems.at[k - 1],
                                         recv_sem=recv_sems.at[k - 1], device_id=peer, device_id_type=MESH).start()
        token[...] = jnp.zeros_like(token)

    return _split_start(body, [v], [land], after, NDEV - 1, name)


def _small_gather_wait(send_sems, recv_sems, srcs, lands, after, name):
    def body(*refs):
        src, lnd, s_sems, r_sems = refs[0], refs[1], refs[2], refs[3]
        x, y, c = _my_place()
        for k in range(1, NDEV):
            sender = 4 * _flip(x, k & 4) + 2 * _flip(y, k & 2) + _flip(c, k & 1)
            cp = pltpu.make_async_remote_copy(src_ref=src, dst_ref=lnd.at[sender], send_sem=s_sems.at[k - 1],
                                              recv_sem=r_sems.at[k - 1], device_id=(x, y, c), device_id_type=MESH)
            cp.wait_send()
            cp.wait_recv()

    return _split_wait(body, send_sems, recv_sems, srcs, lands, after, name)[1][0]


def _gather_start(shards, axes, me, after, name):
    nt = len(shards)
    fulls = _place_own(shards, axes, name + "_own")

    def body(*refs):
        ins, outs = refs[:nt], refs[nt:2 * nt]
        send_sems, recv_sems, token = refs[2 * nt + 1], refs[2 * nt + 2], refs[-1]
        x, y, c = _my_place()
        dev = 4 * x + 2 * y + c
        peers = [(x, y, 1 - c), (1 - x, y, c), (x, 1 - y, c), (1 - x, 1 - y, c)]
        for t in range(nt):
            dst = _slab(outs[t], axes[t], ins[t].shape[axes[t]], dev)
            for k, peer in enumerate(peers):
                pltpu.make_async_remote_copy(src_ref=ins[t], dst_ref=dst, send_sem=send_sems.at[4 * t + k],
                                             recv_sem=recv_sems.at[4 * t + k], device_id=peer, device_id_type=MESH).start()
        token[...] = jnp.zeros_like(token)

    return _split_start(body, shards, fulls, after, 4, name)


def _gather_wait(send_sems, recv_sems, shards, fulls, axes, after, name):
    nt = len(shards)

    def body(*refs):
        ins, outs = refs[:nt], refs[nt:2 * nt]
        s_sems, r_sems = refs[2 * nt], refs[2 * nt + 1]
        x, y, c = _my_place()
        senders = [4 * x + 2 * y + (1 - c), 4 * (1 - x) + 2 * y + c, 4 * x + 2 * (1 - y) + c, 4 * (1 - x) + 2 * (1 - y) + c]
        for t in range(nt):
            for k, src_dev in enumerate(senders):
                cp = pltpu.make_async_remote_copy(
                    src_ref=ins[t], dst_ref=_slab(outs[t], axes[t], ins[t].shape[axes[t]], src_dev),
                    send_sem=s_sems.at[4 * t + k], recv_sem=r_sems.at[4 * t + k], device_id=(x, y, c), device_id_type=MESH)
                cp.wait_send()
                cp.wait_recv()

    return _split_wait(body, send_sems, recv_sems, shards, fulls, after, name)[1]


def _gather_pass_on(fulls, axes, name):
    nt = len(fulls)

    def body(*refs):
        outs = refs[nt:2 * nt]
        send_sems, recv_sems = refs[2 * nt:]
        x, y, c = _my_place()
        chips = [(1 - x, y), (x, 1 - y), (1 - x, 1 - y)]

        def copy(t, j, pc):
            cx, cy = chips[j]
            blk = _slab(outs[t], axes[t], outs[t].shape[axes[t]] // NDEV, 4 * cx + 2 * cy + pc)
            return pltpu.make_async_remote_copy(src_ref=blk, dst_ref=blk, send_sem=send_sems.at[t, j],
                                                recv_sem=recv_sems.at[t, j], device_id=(x, y, 1 - c), device_id_type=MESH)

        sends = [copy(t, j, c) for t in range(nt) for j in range(3)]
        for cp in sends:
            cp.start()
        for t in range(nt):
            for j in range(3):
                copy(t, j, 1 - c).wait_recv()
        for cp in sends:
            cp.wait_send()

    return pl.pallas_call(
        body, name=name, out_shape=[SDS(f.shape, f.dtype) for f in fulls], in_specs=[HBM_SPEC] * nt,
        out_specs=[HBM_SPEC] * nt, input_output_aliases={t: t for t in range(nt)},
        scratch_shapes=[pltpu.SemaphoreType.DMA((nt, 3)), pltpu.SemaphoreType.DMA((nt, 3))],
    )(*fulls)


def _row_tile(R, C, itemsize=4, target=1 << 20):
    best = R
    for tr in range(8, R, 8):
        if R % tr == 0 and tr * C * itemsize <= target:
            best = tr
    return best if best * C * itemsize <= target or best == R else R


def _as2d(a):
    return a.reshape(-1, a.shape[-1])


def _add_pairs(grad, land, c_me, name):
    blk = grad.shape[1:]
    NCH = NDEV // 2
    C = blk[-1]
    R = math.prod(blk[:-1])
    tr = _row_tile(R, C, 2)

    def body(c_ref, a_ref, b_ref, o_ref):
        o_ref[...] = (a_ref[...].astype(F32) + b_ref[...].astype(F32)).astype(o_ref.dtype)

    out = pl.pallas_call(
        body, name=name, out_shape=SDS((NCH, R, C), grad.dtype),
        grid_spec=pltpu.PrefetchScalarGridSpec(
            num_scalar_prefetch=1, grid=(NCH, R // tr),
            in_specs=[pl.BlockSpec((None, None, tr, C), lambda k, i, c: (k, c[0], i, 0)),
                      pl.BlockSpec((None, tr, C), lambda k, i, c: (k, i, 0))],
            out_specs=pl.BlockSpec((None, tr, C), lambda k, i, c: (k, i, 0))),
        compiler_params=_params(("parallel", "parallel")),
    )(c_me.reshape(1).astype(jnp.int32), grad.reshape(NCH, 2, R, C), land.reshape(NCH, R, C))
    return out.reshape((NCH,) + blk)


def _adamw_math(g, w, m, v):
    m2 = ADAM_B1 * m + (1.0 - ADAM_B1) * g
    v2 = ADAM_B2 * v + (1.0 - ADAM_B2) * (g * g)
    m_hat = m2 / (1.0 - ADAM_B1 ** ADAM_STEP)
    v_hat = v2 / (1.0 - ADAM_B2 ** ADAM_STEP)
    delta = -ADAM_LR * (m_hat / (jnp.sqrt(v_hat) + ADAM_EPS) + ADAM_WD * w)
    return delta, m2, v2


def _adamw(parts, w, m, v, name, layer=None, into=None):
    shp = w.shape
    w2, m2, v2 = _as2d(w), _as2d(m), _as2d(v)
    nlay = 1 if layer is None else shp[0]
    j = 0 if layer is None else layer
    RT, C = w2.shape
    R = RT // nlay
    tr = _row_tile(R, C)
    nblk = R // tr
    p3, specs, picks = [], [], []
    for p in parts:
        if isinstance(p, tuple):
            specs.append(pl.BlockSpec((1, tr, C), lambda i, s, k=len(picks): (s[k], i, 0)))
            picks.append(p[1])
            p = p[0]
        else:
            specs.append(pl.BlockSpec((p.size // (R * C), tr, C), lambda i, s: (0, i, 0)))
        p3.append(p.reshape((-1, R, C)))
    npart = len(p3)
    picks = jnp.stack(picks).astype(jnp.int32) if picks else jnp.zeros((1,), jnp.int32)
    rows = pl.BlockSpec((tr, C), lambda i, s: (j * nblk + i, 0))
    prior = [] if into is None else [_as2d(o) for o in into]

    def body(s_ref, *refs):
        prefs = refs[:npart]
        w_ref, m_ref, v_ref = refs[npart:npart + 3]
        g_ref, d_ref, nm_ref, nv_ref = refs[npart + 3 + len(prior):]
        g = None
        for pr in prefs:
            for k in range(pr.shape[0]):
                term = pr[k].astype(F32)
                g = term if g is None else g + term
        g_ref[...] = g
        d_ref[...], nm_ref[...], nv_ref[...] = _adamw_math(g, w_ref[...], m_ref[...], v_ref[...])

    outs = pl.pallas_call(
        body, name=name, out_shape=[SDS((RT, C), F32)] * 4,
        grid_spec=pltpu.PrefetchScalarGridSpec(
            num_scalar_prefetch=1, grid=(nblk,), in_specs=specs + [rows] * 3 + [ANY_SPEC] * len(prior), out_specs=[rows] * 4),
        input_output_aliases={1 + npart + 3 + k: k for k in range(len(prior))},
        compiler_params=_params(("parallel",)),
    )(picks, *p3, w2, m2, v2, *prior)
    return [o.reshape(shp) for o in outs]


def _ada_fwd(c_all, ada_w, ada_b_mine, name):
    L, D, Wc = ada_w.shape

    def body(c_ref, w_ref, b_ref, o_ref):
        cv = c_ref[...]
        act = cv * (1.0 / (1.0 + jnp.exp(-cv)))
        o_ref[0] = jnp.dot(act, w_ref[0], preferred_element_type=F32, precision=lax.Precision.HIGHEST) + b_ref[0]

    return pl.pallas_call(
        body, name=name, grid=(L,),
        in_specs=[_resident(c_all.shape), pl.BlockSpec((1, D, Wc), lambda l: (l, 0, 0)), pl.BlockSpec((1, 1, Wc), lambda l: (l, 0, 0))],
        out_specs=pl.BlockSpec((1, NDEV, Wc), lambda l: (l, 0, 0)), out_shape=SDS((L, NDEV, Wc), F32),
        compiler_params=_params(("parallel",)),
    )(c_all, ada_w, ada_b_mine.reshape(L, 1, Wc))


def _ada_bwd(c_all, dmod_mine, name):
    L, _, Wc = dmod_mine.shape
    D = c_all.shape[1]

    def body(c_ref, d_ref, o_ref):
        cv = c_ref[...]
        act = cv * (1.0 / (1.0 + jnp.exp(-cv)))
        o_ref[0] = lax.dot_general(act, d_ref[0], (((0,), (0,)), ((), ())), preferred_element_type=F32,
                                   precision=lax.Precision.HIGHEST)

    return pl.pallas_call(
        body, name=name, grid=(L,),
        in_specs=[_resident(c_all.shape), pl.BlockSpec((1, NDEV, Wc), lambda l: (l, 0, 0))],
        out_specs=pl.BlockSpec((1, D, Wc), lambda l: (l, 0, 0)), out_shape=SDS((L, D, Wc), F32),
        compiler_params=_params(("parallel",)),
    )(c_all, dmod_mine)


WEIGHT_NAMES = ['ada_w', 'ada_b', 'norm_mix_g', 'norm_mlp_g', 'pool_w', 'pool_scale', 'sgu_w_in', 'sgu_ln_g', 'sgu_ln_b',
                'sgu_w_s', 'sgu_b_s', 'sgu_w_out', 'mla_w_dq_dkv', 'mla_q_norm_g', 'mla_kv_norm_g', 'mla_w_uq', 'mla_w_ukv',
                'mla_w_o', 'mlp_w1', 'mlp_w2', 'final_g']
REPLICATED_EARLY = ['sgu_ln_g', 'sgu_ln_b', 'sgu_w_s', 'sgu_b_s', 'mla_kv_norm_g']
REPLICATED = ['ada_b', 'norm_mix_g', 'norm_mlp_g', 'final_g']
PACK_ROWS = 64
DIRECT_FROM = 1
Q_HEAD = MLA_NOPE + MLA_ROPE


def _layer_matrices(i):
    kind, j = i % 3, i // 3
    if kind == 0:
        mats = [("pool_w", j, 1)]
    elif kind == 1:
        mats = [("sgu_w_in", j, 1), ("sgu_w_out", j, 0)]
    else:
        mats = [("mla_w_dq_dkv", j, 0), ("mla_w_uq", j, 1), ("mla_w_ukv", j, 1), ("mla_w_o", j, 0)]
    return mats + [("mlp_w1", i, 1), ("mlp_w2", i, 0)]


def _pack(arrays):
    flat = jnp.concatenate([a.reshape(-1).astype(F32) for a in arrays])
    rows = -(-flat.size // (LANES * PACK_ROWS)) * PACK_ROWS
    return jnp.pad(flat, (0, rows * LANES - flat.size)).reshape(rows, LANES)


def kernel(x, c, positions, ada_w, ada_b, norm_mix_g, norm_mlp_g, pool_w, pool_scale, sgu_w_in, sgu_ln_g, sgu_ln_b, sgu_w_s, sgu_b_s, sgu_w_out, mla_w_dq_dkv, mla_q_norm_g, mla_kv_norm_g, mla_w_uq, mla_w_ukv, mla_w_o, mlp_w1, mlp_w2, final_g, loss_target, m_ada_w, m_ada_b, m_norm_mix_g, m_norm_mlp_g, m_pool_w, m_pool_scale, m_sgu_w_in, m_sgu_ln_g, m_sgu_ln_b, m_sgu_w_s, m_sgu_b_s, m_sgu_w_out, m_mla_w_dq_dkv, m_mla_q_norm_g, m_mla_kv_norm_g, m_mla_w_uq, m_mla_w_ukv, m_mla_w_o, m_mlp_w1, m_mlp_w2, m_final_g, v_ada_w, v_ada_b, v_norm_mix_g, v_norm_mlp_g, v_pool_w, v_pool_scale, v_sgu_w_in, v_sgu_ln_g, v_sgu_ln_b, v_sgu_w_s, v_sgu_b_s, v_sgu_w_out, v_mla_w_dq_dkv, v_mla_q_norm_g, v_mla_kv_norm_g, v_mla_w_uq, v_mla_w_ukv, v_mla_w_o, v_mlp_w1, v_mlp_w2, v_final_g):
    a = dict(locals())
    S, D = x.shape[1], x.shape[2]
    L = ada_w.shape[0]
    Wc = ada_w.shape[2]
    me = 4 * lax.axis_index("x") + 2 * lax.axis_index("y") + lax.axis_index("c")
    my_chip = 2 * lax.axis_index("x") + lax.axis_index("y")

    v0 = _pack([c, pool_scale, mla_q_norm_g])
    g0 = _small_all_gather(v0, "gather_c").reshape(NDEV, -1)
    n_ps, n_qg = pool_scale.size, mla_q_norm_g.size
    c_all = g0[:, :D]
    ps_w = pool_scale.shape[1]
    ps_full = g0[:, D:D + n_ps].reshape(NDEV, -1, ps_w).transpose(1, 0, 2).reshape(-1, 1, D)
    qg_full = g0[:, D + n_ps:D + n_ps + n_qg].reshape(1, -1)

    ada_b_mine = lax.dynamic_slice_in_dim(ada_b, me * Wc, Wc, axis=1)
    modp = _ada_fwd(c_all, ada_w, ada_b_mine, "ada_fwd")
    ga = _small_all_gather(modp.reshape(-1, LANES), "gather_mod").reshape(NDEV, L, NDEV, Wc)
    mod = lax.dynamic_index_in_dim(ga, me, axis=2, keepdims=False).transpose(1, 0, 2).reshape(L, 6, D)
    mod8 = jnp.pad(mod, ((0, 0), (0, 2), (0, 0)))

    small = {"pool_scale": ps_full, "sgu_ln_g": sgu_ln_g, "sgu_ln_b": sgu_ln_b, "sgu_w_s": sgu_w_s[0],
             "sgu_b_s_t": sgu_b_s[0].T, "mla_q_norm_g": qg_full, "mla_kv_norm_g": mla_kv_norm_g}
    gmix, gmlp = norm_mix_g.reshape(L, 1, D), norm_mlp_g.reshape(L, 1, D)
    tables = _rope_tables(positions, S)

    def shards_of(mats):
        return [a[n][j].astype(BF16) for n, j, _ in mats], [ax for _, _, ax in mats]

    def as_weights(mats, fulls):
        w = {n: f for (n, _, _), f in zip(mats, fulls)}
        if "mla_w_uq" in w:
            lat_w = w["mla_w_dq_dkv"].shape[1]
            w["mla_wd"] = jnp.pad(w.pop("mla_w_dq_dkv"), ((0, 0), (0, -lat_w % LANES)))
            w["mla_wq"] = _pad_heads(w.pop("mla_w_uq"), Q_HEAD)
        return w

    xc = x[0]
    wts, saved, flying = [], [], None
    for i in range(L):
        mats = _layer_matrices(i)
        mod_i = mod8[i]
        if flying is None:
            now, late = mats[:-1], mats[-1:]
            shards, axes = shards_of(now)
            fulls = _all_gather_group(shards, axes, mod8, f"gather_w_{i}")
            w = as_weights(now, fulls)
            late_axes = shards_of(late)[1]
            late_fly = _gather_start(*shards_of(late), me, fulls[0], f"gather_start_{i}b")
            mod_i = mod_i + late_fly[4][0, 0]
            w[late[0][0]] = lambda after, fly=late_fly, axes=late_axes, i=i: _gather_pass_on(
                _gather_wait(*fly[:4], axes, after, f"gather_wait_{i}b"), axes, f"gather_pass_{i}b")[0]
            order = late_fly[4]
        else:
            axes = shards_of(mats)[1]
            fulls = _gather_wait(*flying[:4], axes, xc, f"gather_wait_{i}")
            fulls = _gather_pass_on(fulls, axes, f"gather_pass_{i}")
            w = as_weights(mats, fulls)
            order = fulls[0]
        wts.append(w)
        if i + 1 < L:
            flying = _gather_start(*shards_of(_layer_matrices(i + 1)), me, order, f"gather_start_{i + 1}")
            mod_i = mod_i + flying[4][0, 0]
        xc, sv = _layer_forward(i, xc, mod_i, gmix[i], gmlp[i], wts[i], small, tables)
        saved.append(sv)
    dx, loss_acc = _loss_head(xc, loss_target[0], final_g.reshape(1, D), _tiles(S)[0], "loss_head")

    res = {}
    c_me = lax.axis_index("c")

    def start_reduce(i, mats, g, after, tag=""):
        g = dict(g)
        if "mla_wq" in g:
            g["mla_w_dq_dkv"] = g.pop("mla_wd")[:, :mla_w_dq_dkv.shape[2]]
            g["mla_w_uq"] = _unpad_heads(g.pop("mla_wq"), Q_HEAD)
        gl = []
        for n, j, ax in mats:
            gm, blk = g[n].astype(BF16), a[n][j].shape
            if gm.shape != (NDEV,) + blk:
                gm = jnp.moveaxis(gm.reshape(blk[:ax] + (NDEV,) + blk[ax:]), ax, 0)
            gl.append(gm)
        if i >= DIRECT_FROM:
            return _direct_exchange_start(gl, after, f"rs_direct_start_{i}{tag}") + (mats, f"{i}{tag}", True)
        lands = _reduce_scatter_sibling(gl, f"rs_sibling_{i}{tag}")
        parts = [_add_pairs(gm, l, c_me, f"rs_add_{i}_{n}") for gm, l, (n, _, _) in zip(gl, lands, mats)]
        return _chips_exchange_start(parts, after, f"rs_chips_start_{i}{tag}") + (mats, f"{i}{tag}", False)

    def finish_reduce(fly, after):
        if fly[7]:
            parts, recv = _direct_exchange_wait(*fly[:4], after, f"rs_direct_wait_{fly[6]}")
            mine = me
        else:
            parts, recv = _chips_exchange_wait(*fly[:4], after, f"rs_chips_wait_{fly[6]}")
            mine = my_chip
        for (n, j, _), p, r in zip(fly[5], parts, recv):
            res[n] = _adamw([(p, mine), r], a[n], a["m_" + n], a["v_" + n], f"adamw_{n}_{j}", layer=j, into=res.get(n))

    def update_replicated(gathered, names, tail, label):
        zeros_tail = [jnp.zeros_like(t) for t in tail]
        packs = _adamw([gathered], _pack([a[n] for n in names] + zeros_tail), _pack([a["m_" + n] for n in names] + zeros_tail),
                       _pack([a["v_" + n] for n in names] + zeros_tail), label)
        flat = [t.reshape(-1) for t in packs]
        off = 0
        for n in names:
            res[n] = [f[off:off + a[n].size].reshape(a[n].shape) for f in flat]
            off += a[n].size
        sums = []
        for t in tail:
            sums.append(flat[0][off:off + t.size].reshape(t.shape))
            off += t.size
        return sums, packs[0]

    stats, sgrads, flying, early = [None] * L, {}, [], None
    for i in reversed(range(L)):
        mats = _layer_matrices(i)
        mod_i = mod8[i]
        for fly in flying:
            mod_i = mod_i + fly[4][0, 0]
        if early is not None:
            mod_i = mod_i + early[4][0, 0]
        sent = []

        def hook(g_mlp, i=i, mats=mats, dx=dx, sent=sent):
            sent.append(start_reduce(i, mats[-2:], g_mlp, dx, "a"))
            return sent[0][4][0, 0]

        dx, stats[i], g, sgr = _layer_backward(i, dx, saved[i], mod_i, gmix[i], gmlp[i], wts[i], small, tables, hook)
        sgrads.update(sgr)
        for fly in flying:
            finish_reduce(fly, dx)
        flying = sent + [start_reduce(i, mats[:-2], g, dx)]
        if i == 1:
            early_tail = [sgrads["mla_q_norm_g"]]
            early = _small_gather_start(_pack([sgrads[n] for n in REPLICATED_EARLY] + early_tail), me, flying[-1][4],
                                        "gather_small_start")
    (g_qg,), _ = update_replicated(_small_gather_wait(*early[:4], dx, "gather_small_wait"), REPLICATED_EARLY, early_tail,
                                   "adamw_replicated_early")

    sg = {"ada_b": jnp.stack([s[0:6] for s in stats]), "norm_mix_g": jnp.stack([s[6] for s in stats]),
          "norm_mlp_g": jnp.stack([s[7] for s in stats]), "final_g": loss_acc[0]}
    ps_grad = jnp.concatenate([sgrads[f"pool_scale_{j}"] for j in range(pool_scale.shape[0])])
    tail = [ps_grad, loss_acc[2, :LANES]]
    packed = _pack([sg[n] for n in REPLICATED] + tail) + flying[-1][4][0, 0]
    gathered = _small_all_gather(packed, "gather_small")
    (g_ps, loss_lanes), g_p = update_replicated(gathered, REPLICATED, tail, "adamw_replicated")
    for fly in flying:
        finish_reduce(fly, g_p)
    loss = loss_lanes[0]
    res["pool_scale"] = _adamw([lax.dynamic_slice_in_dim(g_ps, me * ps_w, ps_w, axis=1)], pool_scale, m_pool_scale,
                               v_pool_scale, "adamw_pool_scale")
    qg_w = mla_q_norm_g.shape[1]
    res["mla_q_norm_g"] = _adamw([lax.dynamic_slice_in_dim(g_qg, me * qg_w, qg_w, axis=1)], mla_q_norm_g, m_mla_q_norm_g,
                                 v_mla_q_norm_g, "adamw_q_norm_g")

    n_mod = L * 6 * D
    dmod_all = gathered.reshape(NDEV, -1)[:, :n_mod].reshape(NDEV, L, 6 * D)
    dmod_mine = lax.dynamic_slice_in_dim(dmod_all, me * Wc, Wc, axis=2).transpose(1, 0, 2)
    res["ada_w"] = _adamw([_ada_bwd(c_all, dmod_mine, "ada_bwd")], ada_w, m_ada_w, v_ada_w, "adamw_ada_w")

    return (loss, dx.reshape(x.shape), *[res[n][k] for k in range(4) for n in WEIGHT_NAMES])
```

```python
import math

import jax
import jax.numpy as jnp
from jax import lax
from jax.experimental import pallas as pl
from jax.experimental.pallas import tpu as pltpu

F32 = jnp.float32
BF16 = jnp.bfloat16
SDS = jax.ShapeDtypeStruct
MESH = pl.DeviceIdType.MESH

NDEV = 8
V7X_VMEM_LIMIT = 56 << 20
LANES = 128
RMS_EPS = 1e-6
LN_EPS = 1e-5
POOL_WINDOWS = (2, 4, 8, 16)
HALO = 16
SGU_CHUNK = 128
SGU_HEAD = 128
MLA_NOPE, MLA_ROPE, MLA_V = 128, 64, 128
MLA_Q_LORA, MLA_KV_LORA = 256, 128
MLA_HEAD_PAD = 256
ROPE_THETA = 10000.0
SM_SCALE = (MLA_NOPE + MLA_ROPE) ** -0.5
LOG2E, LN2 = 1.0 / math.log(2.0), math.log(2.0)
Q_SCALE = SM_SCALE * LOG2E
NEG = -1e30
ADAM_LR, ADAM_B1, ADAM_B2, ADAM_EPS, ADAM_WD, ADAM_STEP = 0.001, 0.9, 0.999, 1e-08, 0.01, 10
INV_SQRT2 = 1.0 / math.sqrt(2.0)
INV_SQRT_2PI = 1.0 / math.sqrt(2.0 * math.pi)
SH1, SC1, G1, SH2, SC2, G2 = 0, 1, 2, 3, 4, 5


def _params(sem=None, vmem=V7X_VMEM_LIMIT):
    return pltpu.CompilerParams(dimension_semantics=sem, vmem_limit_bytes=vmem)


def _resident(shape):
    nd = len(shape)
    return pl.BlockSpec(shape, lambda *_: (0,) * nd, pipeline_mode=pl.Buffered(1))


def _rows1(tm, w):
    return pl.BlockSpec((tm, w), lambda i: (i, 0))


def _rms(x):
    r = lax.rsqrt(jnp.mean(x * x, axis=-1, keepdims=True) + RMS_EPS)
    return x * r, r


def _colsum(v):
    return jnp.sum(v, axis=0, keepdims=True)


def _normmod_bwd(dh, n, r, a):
    dn = dh * a
    return r * (dn - n * jnp.mean(dn * n, axis=-1, keepdims=True))


def _dot(a, b):
    return jnp.dot(a, b, preferred_element_type=F32)


def _dot_nt(a, b):
    return lax.dot_general(a, b, (((1,), (1,)), ((), ())), preferred_element_type=F32)


def _dot_tn(a, b):
    return lax.dot_general(a, b, (((0,), (0,)), ((), ())), preferred_element_type=F32)


def _gelu(x):
    return 0.5 * x * (1.0 + lax.erf(x * INV_SQRT2))


def _gelu_grad(x):
    return 0.5 * (1.0 + lax.erf(x * INV_SQRT2)) + x * jnp.exp(-0.5 * x * x) * INV_SQRT_2PI


def _swap_halves(v):
    lane = lax.broadcasted_iota(jnp.int32, v.shape, 1)
    half = MLA_ROPE // 2
    return jnp.where(lane < half, pltpu.roll(v, LANES - half, 1),
                     jnp.where(lane < MLA_ROPE, pltpu.roll(v, half, 1), 0.0))


def _mlp_up(x1, mod, gn, w1, tm, tn, name):
    S, D = x1.shape
    Fh = w1.shape[1]

    def body(x_ref, mod_ref, gn_ref, w_ref, h_ref, r_ref):
        n, _ = _rms(x_ref[...])
        a = gn_ref[...] * (1.0 + mod_ref[SC2:SC2 + 1, :])
        h = (n * a + mod_ref[SH2:SH2 + 1, :]).astype(BF16)
        h_ref[...] = h
        for j in range(Fh // tn):
            cols = slice(j * tn, (j + 1) * tn)
            r_ref[:, cols] = jnp.maximum(_dot(h, w_ref[:, cols]), 0.0).astype(BF16)

    return pl.pallas_call(
        body, name=name, grid=(S // tm,),
        in_specs=[_rows1(tm, D), _resident(mod.shape), _resident(gn.shape), _resident(w1.shape)],
        out_specs=[_rows1(tm, D), _rows1(tm, Fh)],
        out_shape=[SDS((S, D), BF16), SDS((S, Fh), BF16)],
        compiler_params=_params(("parallel",)),
    )(x1, mod, gn, w1)


def _mlp_down(r, w2, x1, mod, tm, name):
    S, Fh = r.shape
    D = w2.shape[1]

    def body(r_ref, w_ref, x_ref, mod_ref, x2_ref, o_ref):
        rv = r_ref[...]
        o = _dot(rv * rv, w_ref[...])
        o_ref[...] = o.astype(BF16)
        x2_ref[...] = x_ref[...] + mod_ref[G2:G2 + 1, :] * o

    return pl.pallas_call(
        body, name=name, grid=(S // tm,),
        in_specs=[_rows1(tm, Fh), _resident(w2.shape), _rows1(tm, D), _resident(mod.shape)],
        out_specs=[_rows1(tm, D), _rows1(tm, D)],
        out_shape=[SDS((S, D), F32), SDS((S, D), BF16)],
        compiler_params=_params(("parallel",)),
    )(r, w2, x1, mod)


def _mlp_bwd_a(dx2, o, mod, w2, r, tm, tn, name):
    S, D = dx2.shape
    Fh = r.shape[1]

    def body(dx_ref, o_ref, mod_ref, w_ref, r_ref, da_ref, do_ref, st_ref):
        @pl.when(pl.program_id(0) == 0)
        def _():
            st_ref[...] = jnp.zeros_like(st_ref)

        dx = dx_ref[...]
        d_o = (dx * mod_ref[G2:G2 + 1, :]).astype(BF16)
        do_ref[...] = d_o
        st_ref[G2:G2 + 1, :] += _colsum(dx * o_ref[...].astype(F32))
        for j in range(Fh // tn):
            cols = slice(j * tn, (j + 1) * tn)
            dz = _dot_nt(d_o, w_ref[cols, :])
            da_ref[:, cols] = (dz * (2.0 * r_ref[:, cols].astype(F32))).astype(BF16)

    return pl.pallas_call(
        body, name=name, grid=(S // tm,),
        in_specs=[_rows1(tm, D), _rows1(tm, D), _resident(mod.shape), _resident(w2.shape), _rows1(tm, Fh)],
        out_specs=[_rows1(tm, Fh), _rows1(tm, D), pl.BlockSpec((8, D), lambda i: (0, 0))],
        out_shape=[SDS((S, Fh), BF16), SDS((S, D), BF16), SDS((8, D), F32)],
        compiler_params=_params(("arbitrary",)),
    )(dx2, o, mod, w2, r)


def _mlp_bwd_b(d_a, w1, x1, dx2, mod, gn, tm, name):
    S, Fh = d_a.shape
    D = w1.shape[0]

    def body(da_ref, w_ref, x_ref, dx_ref, mod_ref, gn_ref, dx1_ref, st_ref):
        @pl.when(pl.program_id(0) == 0)
        def _():
            st_ref[...] = jnp.zeros_like(st_ref)

        dh = _dot_nt(da_ref[...], w_ref[...])
        n, rr = _rms(x_ref[...])
        gn_v = gn_ref[...]
        sc1p = 1.0 + mod_ref[SC2:SC2 + 1, :]
        t = _colsum(dh * n)
        st_ref[SH2:SH2 + 1, :] += _colsum(dh)
        st_ref[SC2:SC2 + 1, :] += t * gn_v
        st_ref[6:7, :] += t * sc1p
        dx1_ref[...] = dx_ref[...] + _normmod_bwd(dh, n, rr, gn_v * sc1p)

    return pl.pallas_call(
        body, name=name, grid=(S // tm,),
        in_specs=[_rows1(tm, Fh), _resident(w1.shape), _rows1(tm, D), _rows1(tm, D), _resident(mod.shape),
                  _resident(gn.shape)],
        out_specs=[_rows1(tm, D), pl.BlockSpec((8, D), lambda i: (0, 0))],
        out_shape=[SDS((S, D), F32), SDS((8, D), F32)],
        compiler_params=_params(("arbitrary",)),
    )(d_a, w1, x1, dx2, mod, gn)


def _mm_tn(a, g, tk, tn, name, square_a=False, col_shards=False):
    S, K1 = a.shape
    N = g.shape[1]
    w = N // NDEV
    per = tn // w if col_shards else 1

    def body(a_ref, g_ref, o_ref):
        av = a_ref[...]
        if square_a:
            av = av * av
        res = _dot_tn(av, g_ref[...]).astype(BF16)
        if col_shards:
            for s in range(per):
                o_ref[s] = res[:, s * w:(s + 1) * w]
        else:
            o_ref[...] = res

    if col_shards:
        out_spec, out_shape = pl.BlockSpec((per, tk, w), lambda i, j: (j, i, 0)), SDS((NDEV, K1, w), BF16)
    else:
        out_spec, out_shape = pl.BlockSpec((tk, tn), lambda i, j: (i, j)), SDS((K1, N), BF16)
    return pl.pallas_call(
        body, name=name, grid=(K1 // tk, N // tn),
        in_specs=[pl.BlockSpec((S, tk), lambda i, j: (0, i)), pl.BlockSpec((S, tn), lambda i, j: (0, j))],
        out_specs=out_spec, out_shape=out_shape,
        compiler_params=_params(("parallel", "parallel")),
    )(a, g)


def _pool_h_ext(x_ref, xp_ref, mod_ref, gn_ref, i, tm):
    ext = jnp.concatenate([xp_ref[...], x_ref[...]], axis=0)
    n, r = _rms(ext)
    a = gn_ref[...] * (1.0 + mod_ref[SC1:SC1 + 1, :])
    h = n * a + mod_ref[SH1:SH1 + 1, :]
    row = lax.broadcasted_iota(jnp.int32, (tm + HALO, 1), 0)
    h = jnp.where(jnp.logical_and(i == 0, row < HALO), 0.0, h)
    return h, n[HALO:], r[HALO:], a


def _trailing_sum(v, win):
    k = 1
    while k < win:
        v = v + pltpu.roll(v, k, 0)
        k *= 2
    return v


def _leading_sum(v, win):
    k = 1
    while k < win:
        v = v + pltpu.roll(v, v.shape[0] - k, 0)
        k *= 2
    return v


def _pool_fwd(x, mod, gn, pw, ps, tm, name):
    S, D = x.shape
    C = D // len(POOL_WINDOWS)
    hb = tm // HALO

    def body(x_ref, xp_ref, mod_ref, gn_ref, pw_ref, ps_ref, x1_ref):
        i = pl.program_id(0)
        h, _, _, _ = _pool_h_ext(x_ref, xp_ref, mod_ref, gn_ref, i, tm)
        t1 = (i * tm + lax.broadcasted_iota(jnp.int32, (tm, 1), 0)).astype(F32) + 1.0
        for g, win in enumerate(POOL_WINDOWS):
            cols = slice(g * C, (g + 1) * C)
            hg = h[:, cols]
            inv = 1.0 / jnp.minimum(t1, float(win))
            pooled = (_trailing_sum(hg, win)[HALO:] * inv - hg[HALO:]).astype(BF16)
            y = _dot(pooled, pw_ref[g]) * ps_ref[:, cols]
            x1_ref[:, cols] = x_ref[:, cols] + mod_ref[G1:G1 + 1, cols] * y

    return pl.pallas_call(
        body, name=name, grid=(S // tm,),
        in_specs=[_rows1(tm, D), pl.BlockSpec((HALO, D), lambda i: (jnp.maximum(i * hb - 1, 0), 0)),
                  _resident(mod.shape), _resident(gn.shape), _resident(pw.shape), _resident(ps.shape)],
        out_specs=_rows1(tm, D),
        out_shape=SDS((S, D), F32),
        compiler_params=_params(("parallel",)),
    )(x, x, mod, gn, pw, ps)


def _pool_bwd(x, dx1, mod, gn, pw, ps, tm, name):
    S, D = x.shape
    G = len(POOL_WINDOWS)
    C = D // G
    hb = tm // HALO
    nt = S // tm

    def body(x_ref, xp_ref, d1_ref, dn_ref, mod_ref, gn_ref, pw_ref, ps_ref, dx_ref, st_ref, dpw_ref):
        i = pl.program_id(0)

        @pl.when(i == 0)
        def _():
            st_ref[...] = jnp.zeros_like(st_ref)
            dpw_ref[...] = jnp.zeros_like(dpw_ref)

        h, n, rr, a = _pool_h_ext(x_ref, xp_ref, mod_ref, gn_ref, i, tm)
        g1 = mod_ref[G1:G1 + 1, :]
        ps_v = ps_ref[...]
        d1 = d1_ref[...]
        d1n = jnp.where(i == nt - 1, 0.0, dn_ref[...])
        dyr = (jnp.concatenate([d1, d1n], axis=0) * (g1 * ps_v)).astype(BF16)
        t1 = (i * tm + lax.broadcasted_iota(jnp.int32, (tm + HALO, 1), 0)).astype(F32) + 1.0
        parts = []
        for g, win in enumerate(POOL_WINDOWS):
            cols = slice(g * C, (g + 1) * C)
            hg = h[:, cols]
            inv = 1.0 / jnp.minimum(t1, float(win))
            pooled = (_trailing_sum(hg, win)[HALO:] * inv[:tm] - hg[HALO:]).astype(BF16)
            yraw = _dot(pooled, pw_ref[g])
            st_ref[G1:G1 + 1, cols] += _colsum(d1[:, cols] * (yraw * ps_v[:, cols]))
            st_ref[4:5, cols] += _colsum(d1[:, cols] * g1[:, cols] * yraw)
            dpw_ref[g] += _dot_tn(pooled, dyr[:tm, cols])
            dpool = _dot_nt(dyr[:, cols], pw_ref[g])
            parts.append(_leading_sum(dpool * inv, win)[:tm] - dpool[:tm])
        dh = jnp.concatenate(parts, axis=1)
        t = _colsum(dh * n)
        st_ref[SH1:SH1 + 1, :] += _colsum(dh)
        st_ref[SC1:SC1 + 1, :] += t * gn_ref[...]
        st_ref[3:4, :] += t * (1.0 + mod_ref[SC1:SC1 + 1, :])
        dx_ref[...] = d1 + _normmod_bwd(dh, n, rr, a)

    return pl.pallas_call(
        body, name=name, grid=(nt,),
        in_specs=[_rows1(tm, D), pl.BlockSpec((HALO, D), lambda i: (jnp.maximum(i * hb - 1, 0), 0)),
                  _rows1(tm, D), pl.BlockSpec((HALO, D), lambda i: (jnp.minimum((i + 1) * hb, S // HALO - 1), 0)),
                  _resident(mod.shape), _resident(gn.shape), _resident(pw.shape), _resident(ps.shape)],
        out_specs=[_rows1(tm, D), pl.BlockSpec((8, D), lambda i: (0, 0)), pl.BlockSpec((G, C, C), lambda i: (0, 0, 0))],
        out_shape=[SDS((S, D), F32), SDS((8, D), F32), SDS((G, C, C), F32)],
        compiler_params=_params(("arbitrary",)),
    )(x, x, dx1, dx1, mod, gn, pw, ps)


def _tril_bf16(w):
    row = lax.broadcasted_iota(jnp.int32, w.shape, 0)
    col = lax.broadcasted_iota(jnp.int32, w.shape, 1)
    return jnp.where(col <= row, w, 0.0).astype(BF16)


def _sgu_front(pre, lng_ref, lnb_ref, W):
    z = _gelu(pre)
    u, v = z[:, :W], z[:, W:]
    mu = jnp.mean(v, axis=-1, keepdims=True)
    xc = v - mu
    rstd = lax.rsqrt(jnp.mean(xc * xc, axis=-1, keepdims=True) + LN_EPS)
    vhat = xc * rstd
    return u, vhat, rstd, vhat * lng_ref[...] + lnb_ref[...]


def _sgu_mix(vn, ws_ref, bst_ref, mix_s, tm, W):
    for hd in range(W // SGU_HEAD):
        wm = _tril_bf16(ws_ref[hd])
        bcol = bst_ref[:, hd:hd + 1]
        for ci in range(tm // SGU_CHUNK):
            rs, cs = slice(ci * SGU_CHUNK, (ci + 1) * SGU_CHUNK), slice(hd * SGU_HEAD, (hd + 1) * SGU_HEAD)
            mix_s[rs, cs] = _dot(wm, vn[rs, cs].astype(BF16)) + bcol


def _sgu_fwd(x, mod, gn, w_in, lng, lnb, ws, bst, w_out, tm, name):
    S, D = x.shape
    W = w_out.shape[0]

    def body(x_ref, mod_ref, gn_ref, win_ref, lng_ref, lnb_ref, ws_ref, bst_ref, wout_ref,
             x1_ref, h_ref, pre_ref, y_ref, mix_s):
        n, _ = _rms(x_ref[...])
        a = gn_ref[...] * (1.0 + mod_ref[SC1:SC1 + 1, :])
        h = (n * a + mod_ref[SH1:SH1 + 1, :]).astype(BF16)
        h_ref[...] = h
        pre = _dot(h, win_ref[...])
        pre_ref[...] = pre.astype(BF16)
        u, _, _, vn = _sgu_front(pre, lng_ref, lnb_ref, W)
        _sgu_mix(vn, ws_ref, bst_ref, mix_s, tm, W)
        y = _dot((u * mix_s[...]).astype(BF16), wout_ref[...])
        y_ref[...] = y.astype(BF16)
        x1_ref[...] = x_ref[...] + mod_ref[G1:G1 + 1, :] * y

    return pl.pallas_call(
        body, name=name, grid=(S // tm,),
        in_specs=[_rows1(tm, D), _resident(mod.shape), _resident(gn.shape), _resident(w_in.shape),
                  _resident(lng.shape), _resident(lnb.shape), _resident(ws.shape), _resident(bst.shape),
                  _resident(w_out.shape)],
        out_specs=[_rows1(tm, D), _rows1(tm, D), _rows1(tm, 2 * W), _rows1(tm, D)],
        out_shape=[SDS((S, D), F32), SDS((S, D), BF16), SDS((S, 2 * W), BF16), SDS((S, D), BF16)],
        scratch_shapes=[pltpu.VMEM((tm, W), F32)],
        compiler_params=_params(("parallel",)),
    )(x, mod, gn, w_in, lng, lnb, ws, bst, w_out)


def _sgu_bwd(x, dx1, pre, y, mod, gn, w_in, lng, lnb, ws, bst, w_out, tm, name):
    S, D = x.shape
    W = w_out.shape[0]
    H = W // SGU_HEAD
    nt = S // tm

    def body(x_ref, d1_ref, pre_ref, y_ref, mod_ref, gn_ref, win_ref, lng_ref, lnb_ref, ws_ref, bst_ref, wout_ref,
             dx_ref, dy_ref, gt_ref, dpre_ref, st_ref, dws_ref, dbs_ref, mix_s, dvn_s):
        i = pl.program_id(0)

        @pl.when(i == 0)
        def _():
            st_ref[...] = jnp.zeros_like(st_ref)
            dws_ref[...] = jnp.zeros_like(dws_ref)
            dbs_ref[...] = jnp.zeros_like(dbs_ref)

        d1 = d1_ref[...]
        pre = pre_ref[...].astype(F32)
        u, vhat, rstd, vn = _sgu_front(pre, lng_ref, lnb_ref, W)
        _sgu_mix(vn, ws_ref, bst_ref, mix_s, tm, W)
        mixed = mix_s[...]
        gt_ref[...] = (u * mixed).astype(BF16)
        dyb = (d1 * mod_ref[G1:G1 + 1, :]).astype(BF16)
        dy_ref[...] = dyb
        st_ref[G1:G1 + 1, :] += _colsum(d1 * y_ref[...].astype(F32))
        dgt = _dot_nt(dyb, wout_ref[...])
        du = dgt * mixed
        dmix = dgt * u
        for hd in range(H):
            wm = _tril_bf16(ws_ref[hd])
            for ci in range(tm // SGU_CHUNK):
                rs, cs = slice(ci * SGU_CHUNK, (ci + 1) * SGU_CHUNK), slice(hd * SGU_HEAD, (hd + 1) * SGU_HEAD)
                dm = dmix[rs, cs]
                dmb = dm.astype(BF16)
                dbs_ref[hd] += jnp.broadcast_to(jnp.sum(dm, axis=1, keepdims=True), (SGU_CHUNK, LANES))
                dws_ref[hd] += _dot_nt(dmb, vn[rs, cs].astype(BF16))
                dvn_s[rs, cs] = _dot_tn(wm, dmb)
        dvn = dvn_s[...]
        st_ref[4:5, :] += _colsum(dvn * vhat)
        st_ref[5:6, :] += _colsum(dvn)
        dvh = dvn * lng_ref[...]
        dv = rstd * (dvh - jnp.mean(dvh, axis=-1, keepdims=True) - vhat * jnp.mean(dvh * vhat, axis=-1, keepdims=True))
        dpre_u = (du * _gelu_grad(pre[:, :W])).astype(BF16)
        dpre_v = (dv * _gelu_grad(pre[:, W:])).astype(BF16)
        dpre_ref[:, :W] = dpre_u
        dpre_ref[:, W:] = dpre_v
        dh = _dot_nt(dpre_u, win_ref[:, :W]) + _dot_nt(dpre_v, win_ref[:, W:])
        n, rr = _rms(x_ref[...])
        gn_v = gn_ref[...]
        sc1p = 1.0 + mod_ref[SC1:SC1 + 1, :]
        t = _colsum(dh * n)
        st_ref[SH1:SH1 + 1, :] += _colsum(dh)
        st_ref[SC1:SC1 + 1, :] += t * gn_v
        st_ref[3:4, :] += t * sc1p
        dx_ref[...] = d1 + _normmod_bwd(dh, n, rr, gn_v * sc1p)

        @pl.when(i == nt - 1)
        def _():
            for hd in range(H):
                row = lax.broadcasted_iota(jnp.int32, (SGU_CHUNK, SGU_CHUNK), 0)
                col = lax.broadcasted_iota(jnp.int32, (SGU_CHUNK, SGU_CHUNK), 1)
                dws_ref[hd] = jnp.where(col <= row, dws_ref[hd], 0.0)

    return pl.pallas_call(
        body, name=name, grid=(nt,),
        in_specs=[_rows1(tm, D), _rows1(tm, D), _rows1(tm, 2 * W), _rows1(tm, D), _resident(mod.shape),
                  _resident(gn.shape), _resident(w_in.shape), _resident(lng.shape), _resident(lnb.shape),
                  _resident(ws.shape), _resident(bst.shape), _resident(w_out.shape)],
        out_specs=[_rows1(tm, D), _rows1(tm, D), _rows1(tm, W), _rows1(tm, 2 * W),
                   pl.BlockSpec((8, D), lambda i: (0, 0)), pl.BlockSpec((H, SGU_CHUNK, SGU_CHUNK), lambda i: (0, 0, 0)),
                   pl.BlockSpec((H, SGU_CHUNK, LANES), lambda i: (0, 0, 0))],
        out_shape=[SDS((S, D), F32), SDS((S, D), BF16), SDS((S, W), BF16), SDS((S, 2 * W), BF16),
                   SDS((8, D), F32), SDS((H, SGU_CHUNK, SGU_CHUNK), F32), SDS((H, SGU_CHUNK, LANES), F32)],
        scratch_shapes=[pltpu.VMEM((tm, W), F32), pltpu.VMEM((tm, W), F32)],
        compiler_params=_params(("arbitrary",)),
    )(x, dx1, pre, y, mod, gn, w_in, lng, lnb, ws, bst, w_out)


def _mla_lat(x, mod, gn, wd, qg, kvg, cos_t, sin_t, tm, name):
    S, D = x.shape
    LW = wd.shape[1]
    QL, KL = MLA_Q_LORA, MLA_KV_LORA

    def body(x_ref, mod_ref, gn_ref, wd_ref, qg_ref, kvg_ref, c_ref, s_ref, h_ref, lat_ref, cq_ref, ckv_ref, kr_ref):
        n, _ = _rms(x_ref[...])
        a = gn_ref[...] * (1.0 + mod_ref[SC1:SC1 + 1, :])
        h = (n * a + mod_ref[SH1:SH1 + 1, :]).astype(BF16)
        h_ref[...] = h
        lat = _dot(h, wd_ref[...])
        lat_ref[...] = lat
        nq, _ = _rms(lat[:, :QL])
        cq_ref[...] = (nq * qg_ref[...]).astype(BF16)
        nkv, _ = _rms(lat[:, QL:QL + KL])
        ckv_ref[...] = (nkv * kvg_ref[...]).astype(BF16)
        kr = lat[:, QL + KL:]
        kr_ref[...] = (kr * c_ref[...] + _swap_halves(kr) * s_ref[...]).astype(BF16)

    return pl.pallas_call(
        body, name=name, grid=(S // tm,),
        in_specs=[_rows1(tm, D), _resident(mod.shape), _resident(gn.shape), _resident(wd.shape), _resident(qg.shape),
                  _resident(kvg.shape), _rows1(tm, LANES), _rows1(tm, LANES)],
        out_specs=[_rows1(tm, D), _rows1(tm, LW), _rows1(tm, QL), _rows1(tm, KL), _rows1(tm, LANES)],
        out_shape=[SDS((S, D), BF16), SDS((S, LW), F32), SDS((S, QL), BF16), SDS((S, KL), BF16), SDS((S, LANES), BF16)],
        compiler_params=_params(("parallel",)),
    )(x, mod, gn, wd, qg, kvg, cos_t, sin_t)


def _mla_qkv(cq, ckv, krp, wq, wukv, cos_t, sin_t, tm, name):
    S = cq.shape[0]
    H = wq.shape[1] // MLA_HEAD_PAD
    HP = MLA_HEAD_PAD

    def body(cq_ref, ckv_ref, kr_ref, wq_ref, wkv_ref, c_ref, s_ref, q_ref, k_ref, v_ref):
        q = _dot(cq_ref[...], wq_ref[...])
        kv = _dot(ckv_ref[...], wkv_ref[...])
        cv, sv, krv = c_ref[...], s_ref[...], kr_ref[...]
        for h in range(H):
            qr = q[:, h * HP + MLA_NOPE:(h + 1) * HP]
            q_ref[:, h * HP:h * HP + MLA_NOPE] = (q[:, h * HP:h * HP + MLA_NOPE] * Q_SCALE).astype(BF16)
            q_ref[:, h * HP + MLA_NOPE:(h + 1) * HP] = ((qr * cv + _swap_halves(qr) * sv) * Q_SCALE).astype(BF16)
            k_ref[:, h * HP:h * HP + MLA_NOPE] = kv[:, h * HP:h * HP + MLA_NOPE].astype(BF16)
            k_ref[:, h * HP + MLA_NOPE:(h + 1) * HP] = krv
            v_ref[:, h * MLA_V:(h + 1) * MLA_V] = kv[:, h * HP + MLA_NOPE:(h + 1) * HP].astype(BF16)

    return pl.pallas_call(
        body, name=name, grid=(S // tm,),
        in_specs=[_rows1(tm, MLA_Q_LORA), _rows1(tm, MLA_KV_LORA), _rows1(tm, LANES), _resident(wq.shape),
                  _resident(wukv.shape), _rows1(tm, LANES), _rows1(tm, LANES)],
        out_specs=[_rows1(tm, H * HP), _rows1(tm, H * HP), _rows1(tm, H * MLA_V)],
        out_shape=[SDS((S, H * HP), BF16), SDS((S, H * HP), BF16), SDS((S, H * MLA_V), BF16)],
        compiler_params=_params(("parallel",)),
    )(cq, ckv, krp, wq, wukv, cos_t, sin_t)


def _causal_mask(nr, nc):
    row = lax.broadcasted_iota(jnp.int32, (nr, nc), 0)
    col = lax.broadcasted_iota(jnp.int32, (nr, nc), 1)
    return col <= row


def _attn_fwd(q, k, v, tq, name):
    S = q.shape[0]
    HP = MLA_HEAD_PAD
    H = q.shape[1] // HP
    nq = S // tq

    def body(q_ref, k_ref, v_ref, o_ref, lse_ref, v1):
        v1[:, :MLA_V] = v_ref[...]
        v1[:, MLA_V:] = jnp.ones((S, LANES), BF16)
        def update(qv, j, carry, masked):
            m, acc = carry
            krows = slice(j * tq, (j + 1) * tq)
            s = _dot_nt(qv, k_ref[krows, :])
            if masked:
                s = jnp.where(_causal_mask(tq, tq), s, NEG)
            m_new = jnp.maximum(m, jnp.max(s, axis=1, keepdims=True))
            p = jnp.exp2(s - m_new)
            return m_new, jnp.exp2(m - m_new) * acc + _dot(p.astype(BF16), v1[krows, :])

        for i in range(nq):
            rows = slice(i * tq, (i + 1) * tq)
            qv = q_ref[rows, :]
            carry = (jnp.full((tq, 1), NEG, F32), jnp.zeros((tq, MLA_V + LANES), F32))
            for j in range(i):
                carry = update(qv, j, carry, False)
            m, acc = update(qv, i, carry, True)
            l = acc[:, MLA_V:MLA_V + 1]
            o_ref[rows, :] = (acc[:, :MLA_V] / l).astype(BF16)
            lse_ref[0, rows, :] = jnp.broadcast_to(m + jnp.log2(l), (tq, LANES))

    return pl.pallas_call(
        body, name=name, grid=(H,),
        in_specs=[pl.BlockSpec((S, HP), lambda h: (0, h)), pl.BlockSpec((S, HP), lambda h: (0, h)),
                  pl.BlockSpec((S, MLA_V), lambda h: (0, h))],
        out_specs=[pl.BlockSpec((S, MLA_V), lambda h: (0, h)), pl.BlockSpec((1, S, LANES), lambda h: (h, 0, 0))],
        out_shape=[SDS((S, H * MLA_V), BF16), SDS((H, S, LANES), F32)],
        scratch_shapes=[pltpu.VMEM((S, MLA_V + LANES), BF16)],
        compiler_params=_params(("parallel",)),
    )(q, k, v)


def _attn_bwd(q, k, v, o, do, lse, cos_t, sin_t, tq, name):
    S = q.shape[0]
    HP = MLA_HEAD_PAD
    H = q.shape[1] // HP
    nq = S // tq

    def body(q_ref, k_ref, v_ref, o_ref, do_ref, lse_ref, c_ref, s_ref, dq_ref, dkv_ref, dkr_ref, dq_acc, dl_s):
        @pl.when(pl.program_id(0) == 0)
        def _():
            dkr_ref[...] = jnp.zeros_like(dkr_ref)

        dq_acc[...] = jnp.zeros_like(dq_acc)

        def delta_tile(i, _):
            rows = pl.ds(pl.multiple_of(i * tq, tq), tq)
            d = jnp.sum(do_ref[rows, :].astype(F32) * o_ref[rows, :].astype(F32), axis=1, keepdims=True)
            dl_s[rows, :] = jnp.broadcast_to(d, (tq, LANES))
            return 0

        lax.fori_loop(0, nq, delta_tile, 0)

        def update(i, kv_k, kv_v, carry, masked):
            dk, dv = carry
            rows = slice(i * tq, (i + 1) * tq)
            qv = q_ref[rows, :]
            dov = do_ref[rows, :]
            s = _dot_nt(qv, kv_k)
            if masked:
                s = jnp.where(_causal_mask(tq, tq), s, NEG)
            p = jnp.exp2(s - lse_ref[0, rows, 0:1])
            dv = dv + _dot_tn(p.astype(BF16), dov)
            dp = _dot_nt(dov, kv_v)
            ds = (p * (dp - dl_s[rows, 0:1])).astype(BF16)
            dk = dk + _dot_tn(ds, qv)
            dq_acc[rows, :] += _dot(ds, kv_k)
            return dk, dv

        for j in range(nq):
            krows = slice(j * tq, (j + 1) * tq)
            kv_k = k_ref[krows, :]
            kv_v = v_ref[krows, :]
            carry = update(j, kv_k, kv_v, (jnp.zeros((tq, HP), F32), jnp.zeros((tq, MLA_V), F32)), True)
            for i in range(j + 1, nq):
                carry = update(i, kv_k, kv_v, carry, False)
            dk, dv = carry
            dk = dk * LN2
            dkv_ref[krows, :MLA_NOPE] = dk[:, :MLA_NOPE].astype(BF16)
            dkv_ref[krows, MLA_NOPE:] = dv.astype(BF16)
            dkr_ref[krows, :] += dk[:, MLA_NOPE:]

        def out_tile(i, _):
            rows = pl.ds(pl.multiple_of(i * tq, tq), tq)
            dq = dq_acc[rows, :] * SM_SCALE
            dqr = dq[:, MLA_NOPE:]
            dq_ref[rows, :MLA_NOPE] = dq[:, :MLA_NOPE].astype(BF16)
            dq_ref[rows, MLA_NOPE:] = (dqr * c_ref[rows, :] + _swap_halves(dqr * s_ref[rows, :])).astype(BF16)
            return 0

        lax.fori_loop(0, nq, out_tile, 0)

    return pl.pallas_call(
        body, name=name, grid=(H,),
        in_specs=[pl.BlockSpec((S, HP), lambda h: (0, h)), pl.BlockSpec((S, HP), lambda h: (0, h)),
                  pl.BlockSpec((S, MLA_V), lambda h: (0, h)), pl.BlockSpec((S, MLA_V), lambda h: (0, h)),
                  pl.BlockSpec((S, MLA_V), lambda h: (0, h)), pl.BlockSpec((1, S, LANES), lambda h: (h, 0, 0)),
                  _resident(cos_t.shape), _resident(sin_t.shape)],
        out_specs=[pl.BlockSpec((S, HP), lambda h: (0, h)), pl.BlockSpec((S, HP), lambda h: (0, h)),
                   pl.BlockSpec((S, LANES), lambda h: (0, 0))],
        out_shape=[SDS((S, H * HP), BF16), SDS((S, H * HP), BF16), SDS((S, LANES), F32)],
        scratch_shapes=[pltpu.VMEM((S, HP), F32), pltpu.VMEM((S, LANES), F32)],
        compiler_params=_params(("arbitrary",)),
    )(q, k, v, o, do, lse, cos_t, sin_t)


def _mla_out(o, w_o, x, mod, tm, name):
    S, KO = o.shape
    D = w_o.shape[1]

    def body(o_ref, w_ref, x_ref, mod_ref, x1_ref, y_ref):
        y = _dot(o_ref[...], w_ref[...])
        y_ref[...] = y.astype(BF16)
        x1_ref[...] = x_ref[...] + mod_ref[G1:G1 + 1, :] * y

    return pl.pallas_call(
        body, name=name, grid=(S // tm,),
        in_specs=[_rows1(tm, KO), _resident(w_o.shape), _rows1(tm, D), _resident(mod.shape)],
        out_specs=[_rows1(tm, D), _rows1(tm, D)],
        out_shape=[SDS((S, D), F32), SDS((S, D), BF16)],
        compiler_params=_params(("parallel",)),
    )(o, w_o, x, mod)


def _mla_bwd_o(dx1, y, mod, w_o, tm, name):
    S, D = dx1.shape
    KO = w_o.shape[0]

    def body(d1_ref, y_ref, mod_ref, w_ref, dy_ref, do_ref, st_ref):
        @pl.when(pl.program_id(0) == 0)
        def _():
            st_ref[...] = jnp.zeros_like(st_ref)

        d1 = d1_ref[...]
        dyb = (d1 * mod_ref[G1:G1 + 1, :]).astype(BF16)
        dy_ref[...] = dyb
        st_ref[G1:G1 + 1, :] += _colsum(d1 * y_ref[...].astype(F32))
        do_ref[...] = _dot_nt(dyb, w_ref[...]).astype(BF16)

    return pl.pallas_call(
        body, name=name, grid=(S // tm,),
        in_specs=[_rows1(tm, D), _rows1(tm, D), _resident(mod.shape), _resident(w_o.shape)],
        out_specs=[_rows1(tm, D), _rows1(tm, KO), pl.BlockSpec((8, D), lambda i: (0, 0))],
        out_shape=[SDS((S, D), BF16), SDS((S, KO), BF16), SDS((8, D), F32)],
        compiler_params=_params(("arbitrary",)),
    )(dx1, y, mod, w_o)


def _mla_bwd_lat(dq, dkv, dkr, lat, x, dx1, mod, gn, qg, kvg, wq, wukv, wd, cos_t, sin_t, tm, name):
    S, D = x.shape
    LW = wd.shape[1]
    QL, KL = MLA_Q_LORA, MLA_KV_LORA

    def body(dq_ref, dkv_ref, dkr_ref, lat_ref, x_ref, d1_ref, mod_ref, gn_ref, qg_ref, kvg_ref, wq_ref, wkv_ref, wd_ref,
             c_ref, s_ref, dx_ref, dlat_ref, st_ref, dqg_ref, dkvg_ref):
        @pl.when(pl.program_id(0) == 0)
        def _():
            st_ref[...] = jnp.zeros_like(st_ref)
            dqg_ref[...] = jnp.zeros_like(dqg_ref)
            dkvg_ref[...] = jnp.zeros_like(dkvg_ref)

        lat = lat_ref[...]
        dcq = _dot_nt(dq_ref[...], wq_ref[...])
        nq, rq = _rms(lat[:, :QL])
        dqg_ref[0:1, :] += _colsum(dcq * nq)
        dlat_q = _normmod_bwd(dcq, nq, rq, qg_ref[...]).astype(BF16)
        dckv = _dot_nt(dkv_ref[...], wkv_ref[...])
        nkv, rkv = _rms(lat[:, QL:QL + KL])
        dkvg_ref[0:1, :] += _colsum(dckv * nkv)
        dlat_kv = _normmod_bwd(dckv, nkv, rkv, kvg_ref[...]).astype(BF16)
        dkr = dkr_ref[...]
        dlat_kr = (dkr * c_ref[...] + _swap_halves(dkr * s_ref[...])).astype(BF16)
        dlat_ref[:, :QL] = dlat_q
        dlat_ref[:, QL:QL + KL] = dlat_kv
        dlat_ref[:, QL + KL:] = dlat_kr
        dh = (_dot_nt(dlat_q, wd_ref[:, :QL]) + _dot_nt(dlat_kv, wd_ref[:, QL:QL + KL])
              + _dot_nt(dlat_kr, wd_ref[:, QL + KL:]))
        n, rr = _rms(x_ref[...])
        gn_v = gn_ref[...]
        sc1p = 1.0 + mod_ref[SC1:SC1 + 1, :]
        t = _colsum(dh * n)
        st_ref[SH1:SH1 + 1, :] += _colsum(dh)
        st_ref[SC1:SC1 + 1, :] += t * gn_v
        st_ref[3:4, :] += t * sc1p
        dx_ref[...] = d1_ref[...] + _normmod_bwd(dh, n, rr, gn_v * sc1p)

    HW = wq.shape[1]
    return pl.pallas_call(
        body, name=name, grid=(S // tm,),
        in_specs=[_rows1(tm, HW), _rows1(tm, HW), _rows1(tm, LANES), _rows1(tm, LW), _rows1(tm, D), _rows1(tm, D),
                  _resident(mod.shape), _resident(gn.shape), _resident(qg.shape), _resident(kvg.shape),
                  _resident(wq.shape), _resident(wukv.shape), _resident(wd.shape), _rows1(tm, LANES), _rows1(tm, LANES)],
        out_specs=[_rows1(tm, D), _rows1(tm, LW), pl.BlockSpec((8, D), lambda i: (0, 0)),
                   pl.BlockSpec((8, QL), lambda i: (0, 0)), pl.BlockSpec((8, KL), lambda i: (0, 0))],
        out_shape=[SDS((S, D), F32), SDS((S, LW), BF16), SDS((8, D), F32), SDS((8, QL), F32), SDS((8, KL), F32)],
        compiler_params=_params(("arbitrary",)),
    )(dq, dkv, dkr, lat, x, dx1, mod, gn, qg, kvg, wq, wukv, wd, cos_t, sin_t)


def _loss_head(x, tgt, fg, tm, name):
    S, D = x.shape
    nt = S // tm

    def body(x_ref, t_ref, g_ref, dx_ref, acc_ref):
        i = pl.program_id(0)

        @pl.when(i == 0)
        def _():
            acc_ref[...] = jnp.zeros_like(acc_ref)

        n, rr = _rms(x_ref[...])
        g = g_ref[...]
        err = n * g - t_ref[...]
        acc_ref[1:2, :] += _colsum(err * err) * (0.5 / D)
        dy = err * (1.0 / D)
        acc_ref[0:1, :] += _colsum(dy * n)
        dx_ref[...] = _normmod_bwd(dy, n, rr, g)

        @pl.when(i == nt - 1)
        def _():
            acc_ref[2:3, :] = jnp.broadcast_to(jnp.sum(acc_ref[1:2, :], axis=1, keepdims=True), (1, D))

    return pl.pallas_call(
        body, name=name, grid=(nt,),
        in_specs=[_rows1(tm, D), _rows1(tm, D), _resident(fg.shape)],
        out_specs=[_rows1(tm, D), pl.BlockSpec((8, D), lambda i: (0, 0))],
        out_shape=[SDS((S, D), F32), SDS((8, D), F32)],
        compiler_params=_params(("arbitrary",)),
    )(x, tgt, fg)


def _rope_tables(positions, S):
    inv_freq = ROPE_THETA ** (-jnp.arange(0, MLA_ROPE, 2, dtype=F32) / MLA_ROPE)
    ang = positions.reshape(S, 1).astype(F32) * inv_freq
    cos, sin = jnp.cos(ang), jnp.sin(ang)
    z = jnp.zeros((S, LANES - MLA_ROPE), F32)
    return jnp.concatenate([cos, cos, z], axis=1), jnp.concatenate([-sin, sin, z], axis=1)


def _pad_heads(w, per_head):
    K = w.shape[0]
    H = w.shape[1] // per_head
    w3 = w.reshape(K, H, per_head)
    return jnp.pad(w3, ((0, 0), (0, 0), (0, MLA_HEAD_PAD - per_head))).reshape(K, H * MLA_HEAD_PAD)


def _unpad_heads(w, per_head):
    K = w.shape[0]
    H = w.shape[1] // MLA_HEAD_PAD
    return w.reshape(K, H, MLA_HEAD_PAD)[:, :, :per_head].reshape(K, H * per_head)


def _tiles(S):
    return min(512, S), min(256, S), min(512, S)


def _layer_forward(i, x, mod, gmix, gmlp, w, small, tables):
    S, D = x.shape
    tm, tms, tq = _tiles(S)
    cos_t, sin_t = tables
    kind = i % 3
    sv = {"x": x}
    if kind == 0:
        x1 = _pool_fwd(x, mod, gmix, w["pool_w"], small["pool_scale"][i // 3], tm, f"pool_fwd_{i}")
    elif kind == 1:
        x1, sv["h"], sv["pre"], sv["y"] = _sgu_fwd(
            x, mod, gmix, w["sgu_w_in"], small["sgu_ln_g"], small["sgu_ln_b"], small["sgu_w_s"],
            small["sgu_b_s_t"], w["sgu_w_out"], tms, f"sgu_fwd_{i}")
    else:
        sv["h"], sv["lat"], sv["cq"], sv["ckv"], krp = _mla_lat(
            x, mod, gmix, w["mla_wd"], small["mla_q_norm_g"], small["mla_kv_norm_g"], cos_t, sin_t, tm,
            f"mla_lat_{i}")
        sv["q"], sv["k"], sv["v"] = _mla_qkv(sv["cq"], sv["ckv"], krp, w["mla_wq"], w["mla_w_ukv"], cos_t, sin_t,
                                             tm, f"mla_qkv_{i}")
        sv["o"], sv["lse"] = _attn_fwd(sv["q"], sv["k"], sv["v"], tq, f"attn_fwd_{i}")
        x1, sv["y"] = _mla_out(sv["o"], w["mla_w_o"], x, mod, tm, f"mla_out_{i}")
    sv["x1"] = x1
    if callable(w["mlp_w1"]):
        w.update(w["mlp_w1"](x1))
    Fh = w["mlp_w1"].shape[1]
    sv["h2"], sv["r"] = _mlp_up(x1, mod, gmlp, w["mlp_w1"], tm, min(2048, Fh), f"mlp_up_{i}")
    if callable(w["mlp_w2"]):
        w.update(w["mlp_w2"](sv["r"]))
    x, sv["o2"] = _mlp_down(sv["r"], w["mlp_w2"], x1, mod, tm, f"mlp_down_{i}")
    return x, sv


def _layer_backward(i, dx, sv, mod, gmix, gmlp, w, small, tables, on_mlp_grads=None):
    S, D = dx.shape
    tm, tms, tq = _tiles(S)
    cos_t, sin_t = tables
    kind = i % 3
    sgrads = {}
    Fh = w["mlp_w1"].shape[1]
    g = {}
    d_a, d_o, st_a = _mlp_bwd_a(dx, sv["o2"], mod, w["mlp_w2"], sv["r"], tm, min(2048, Fh), f"mlp_bwd_a_{i}")
    g["mlp_w2"] = _mm_tn(sv["r"], d_o, min(512, Fh), D, f"mlp_dw2_{i}", square_a=True)
    g["mlp_w1"] = _mm_tn(sv["h2"], d_a, D, min(512, Fh), f"mlp_dw1_{i}", col_shards=True)
    if on_mlp_grads is not None:
        mod = mod + on_mlp_grads({n: g.pop(n) for n in ("mlp_w1", "mlp_w2")})
    dx1, st_b = _mlp_bwd_b(d_a, w["mlp_w1"], sv["x1"], dx, mod, gmlp, tm, f"mlp_bwd_b_{i}")
    if kind == 0:
        dx, st_m, dpw = _pool_bwd(sv["x"], dx1, mod, gmix, w["pool_w"], small["pool_scale"][i // 3], tm,
                                  f"pool_bwd_{i}")
        g["pool_w"] = dpw
        sgrads[f"pool_scale_{i // 3}"] = st_m[4:5]
    elif kind == 1:
        dx, dyb, gated, dpre, st_m, dws, dbs = _sgu_bwd(
            sv["x"], dx1, sv["pre"], sv["y"], mod, gmix, w["sgu_w_in"], small["sgu_ln_g"], small["sgu_ln_b"],
            small["sgu_w_s"], small["sgu_b_s_t"], w["sgu_w_out"], tms, f"sgu_bwd_{i}")
        W = gated.shape[1]
        g["sgu_w_out"] = _mm_tn(gated, dyb, min(512, W), D, f"sgu_dwout_{i}")
        g["sgu_w_in"] = _mm_tn(sv["h"], dpre, D, min(512, 2 * W), f"sgu_dwin_{i}", col_shards=True)
        sgrads["sgu_ln_g"], sgrads["sgu_ln_b"] = st_m[4:5], st_m[5:6]
        sgrads["sgu_w_s"], sgrads["sgu_b_s"] = dws, dbs[:, :, 0]
    else:
        dyb, do, st_o = _mla_bwd_o(dx1, sv["y"], mod, w["mla_w_o"], tm, f"mla_bwd_o_{i}")
        KO = do.shape[1]
        g["mla_w_o"] = _mm_tn(sv["o"], dyb, min(512, KO), D, f"mla_dwo_{i}")
        dq, dkv, dkr = _attn_bwd(sv["q"], sv["k"], sv["v"], sv["o"], do, sv["lse"], cos_t, sin_t, tq, f"attn_bwd_{i}")
        dx, dlat, st_m, dqg, dkvg = _mla_bwd_lat(
            dq, dkv, dkr, sv["lat"], sv["x"], dx1, mod, gmix, small["mla_q_norm_g"], small["mla_kv_norm_g"],
            w["mla_wq"], w["mla_w_ukv"], w["mla_wd"], cos_t, sin_t, tm, f"mla_bwd_lat_{i}")
        HW = dq.shape[1]
        g["mla_wq"] = _mm_tn(sv["cq"], dq, MLA_Q_LORA, min(1024, HW), f"mla_dwq_{i}")
        g["mla_w_ukv"] = _mm_tn(sv["ckv"], dkv, MLA_KV_LORA, min(1024, HW), f"mla_dwukv_{i}", col_shards=True)
        g["mla_wd"] = _mm_tn(sv["h"], dlat, D, dlat.shape[1], f"mla_dwd_{i}")
        st_m = jnp.concatenate([st_m[0:2], st_o[2:3], st_m[3:]], axis=0)
        sgrads["mla_q_norm_g"], sgrads["mla_kv_norm_g"] = dqg[0:1], dkvg[0:1]
    stats = jnp.concatenate([st_m[0:3], st_b[3:5], st_a[5:6], st_m[3:4], st_b[6:7]], axis=0)
    return dx, stats, g, sgrads


def _local_step(x, tgt, positions, mod, gmix, gmlp, fg, wts, small):
    S = x.shape[0]
    L = mod.shape[0]
    tables = _rope_tables(positions, S)
    saved = []
    for i in range(L):
        x, sv = _layer_forward(i, x, mod[i], gmix[i], gmlp[i], wts[i], small, tables)
        saved.append(sv)
    dx, loss_acc = _loss_head(x, tgt, fg, _tiles(S)[0], "loss_head")
    stats, grads, sgrads = [None] * L, [None] * L, {}
    for i in reversed(range(L)):
        dx, stats[i], grads[i], sg = _layer_backward(i, dx, saved[i], mod[i], gmix[i], gmlp[i], wts[i], small, tables)
        sgrads.update(sg)
    return loss_acc, dx, stats, grads, sgrads


HBM_SPEC = pl.BlockSpec(memory_space=pltpu.HBM)
VMEM_SPEC = pl.BlockSpec(memory_space=pltpu.VMEM)


def _my_place():
    return lax.axis_index("x"), lax.axis_index("y"), lax.axis_index("c")


def _flip(v, bit):
    return 1 - v if bit else v


def _small_all_gather(v, name):
    R, C = v.shape

    def body(x_ref, out_ref, send_sems, recv_sems):
        x, y, c = _my_place()
        me = 4 * x + 2 * y + c
        out_ref[me] = x_ref[...]
        sends = []
        for k in range(1, NDEV):
            peer = (_flip(x, k & 4), _flip(y, k & 2), _flip(c, k & 1))
            cp = pltpu.make_async_remote_copy(src_ref=x_ref, dst_ref=out_ref.at[me], send_sem=send_sems.at[k - 1],
                                              recv_sem=recv_sems.at[k - 1], device_id=peer, device_id_type=MESH)
            cp.start()
            sends.append(cp)
        for k in range(1, NDEV):
            src = 4 * _flip(x, k & 4) + 2 * _flip(y, k & 2) + _flip(c, k & 1)
            pltpu.make_async_remote_copy(src_ref=x_ref, dst_ref=out_ref.at[src], send_sem=send_sems.at[k - 1],
                                         recv_sem=recv_sems.at[k - 1], device_id=(x, y, c), device_id_type=MESH).wait_recv()
        for cp in sends:
            cp.wait_send()

    return pl.pallas_call(
        body, name=name, out_shape=SDS((NDEV, R, C), v.dtype), in_specs=[VMEM_SPEC], out_specs=VMEM_SPEC,
        scratch_shapes=[pltpu.SemaphoreType.DMA((NDEV - 1,)), pltpu.SemaphoreType.DMA((NDEV - 1,))],
        compiler_params=pltpu.CompilerParams(vmem_limit_bytes=V7X_VMEM_LIMIT),
    )(v)


def _slab(ref, axis, width, dev):
    idx = [slice(None)] * len(ref.shape)
    idx[axis] = pl.ds(pl.multiple_of(dev * width, width), width)
    return ref.at[tuple(idx)]


def _all_gather_group(shards, axes, after, name):
    nt = len(shards)
    out_shapes = [SDS(tuple(s * NDEV if a == ax else s for a, s in enumerate(sh.shape)), sh.dtype)
                  for sh, ax in zip(shards, axes)]

    def body(*refs):
        ins, outs = refs[:nt], refs[nt + 1:2 * nt + 1]
        send_sems, recv_sems, local_sems = refs[2 * nt + 1:]
        x, y, c = _my_place()
        me = 4 * x + 2 * y + c
        sibling = (x, y, 1 - c)
        chips = [(1 - x, y), (x, 1 - y), (1 - x, 1 - y)]

        def block(t, dev):
            return _slab(outs[t], axes[t], ins[t].shape[axes[t]], dev)

        def copy(t, k, dev, to, src=None):
            return pltpu.make_async_remote_copy(
                src_ref=block(t, dev) if src is None else src, dst_ref=block(t, dev), send_sem=send_sems.at[t, k],
                recv_sem=recv_sems.at[t, k], device_id=to, device_id_type=MESH)

        mine = [pltpu.make_async_copy(ins[t], block(t, me), local_sems.at[t]) for t in range(nt)]
        for cp in mine:
            cp.start()
        first = []
        for t in range(nt):
            first.append(copy(t, 0, me, sibling, src=ins[t]))
            first += [copy(t, 1 + j, me, (cx, cy, c), src=ins[t]) for j, (cx, cy) in enumerate(chips)]
        for cp in first:
            cp.start()
        passed = []
        for j, (cx, cy) in enumerate(chips):
            for t in range(nt):
                copy(t, 1 + j, 4 * cx + 2 * cy + c, (x, y, c)).wait_recv()
                cp = copy(t, 4 + j, 4 * cx + 2 * cy + c, sibling)
                cp.start()
                passed.append(cp)
        for t in range(nt):
            copy(t, 0, 4 * x + 2 * y + (1 - c), (x, y, c)).wait_recv()
        for j, (cx, cy) in enumerate(chips):
            for t in range(nt):
                copy(t, 4 + j, 4 * cx + 2 * cy + (1 - c), (x, y, c)).wait_recv()
        for cp in first + passed:
            cp.wait_send()
        for cp in mine:
            cp.wait()

    return pl.pallas_call(
        body, name=name, out_shape=out_shapes, in_specs=[HBM_SPEC] * nt + [ANY_SPEC], out_specs=[HBM_SPEC] * nt,
        scratch_shapes=[pltpu.SemaphoreType.DMA((nt, NDEV - 1)), pltpu.SemaphoreType.DMA((nt, NDEV - 1)),
                        pltpu.SemaphoreType.DMA((nt,))],
    )(*shards, after)


def _reduce_scatter_sibling(grads, name):
    nt = len(grads)
    NCH = NDEV // 2
    out_shapes = [SDS((NCH,) + gr.shape[1:], gr.dtype) for gr in grads]

    def body(*refs):
        ins, lands = refs[:nt], refs[nt:2 * nt]
        send_sems, recv_sems = refs[2 * nt:]
        x, y, c = _my_place()
        sends = []
        for t in range(nt):
            for k in range(NCH):
                cp = pltpu.make_async_remote_copy(
                    src_ref=ins[t].at[2 * k + (1 - c)], dst_ref=lands[t].at[k], send_sem=send_sems.at[t, k],
                    recv_sem=recv_sems.at[t, k], device_id=(x, y, 1 - c), device_id_type=MESH)
                cp.start()
                sends.append(cp)
        for cp in sends:
            cp.wait_recv()
        for cp in sends:
            cp.wait_send()

    return pl.pallas_call(
        body, name=name, out_shape=out_shapes, in_specs=[HBM_SPEC] * nt, out_specs=[HBM_SPEC] * nt,
        scratch_shapes=[pltpu.SemaphoreType.DMA((nt, NCH)), pltpu.SemaphoreType.DMA((nt, NCH))],
    )(*grads)


SEM_SPEC = pl.BlockSpec(memory_space=pltpu.SEMAPHORE)
ANY_SPEC = pl.BlockSpec(memory_space=pl.ANY)
SPLIT_PARAMS = pltpu.CompilerParams(has_side_effects=pltpu.SideEffectType.DATAFLOW_SIDE_EFFECTING)
TOKEN = SDS((8, LANES), F32)


def _in_hbm(arrays):
    return [pltpu.with_memory_space_constraint(v, pltpu.HBM) for v in arrays]


def _split_start(body, srcs, lands, after, n_sem, name):
    ns, nl = len(srcs), len(lands)
    bufs = list(srcs) + list(lands)
    res = pl.pallas_call(
        body, name=name,
        out_shape=(pltpu.SemaphoreType.DMA((ns * n_sem,)), pltpu.SemaphoreType.DMA((ns * n_sem,)),
                   *[pltpu.HBM(v.shape, v.dtype) for v in bufs], TOKEN),
        in_specs=[HBM_SPEC] * (ns + nl) + [ANY_SPEC],
        out_specs=(SEM_SPEC, SEM_SPEC, *[HBM_SPEC] * (ns + nl), VMEM_SPEC),
        input_output_aliases={t: 2 + t for t in range(ns + nl)}, compiler_params=SPLIT_PARAMS,
    )(*_in_hbm(bufs), after)
    return res[0], res[1], list(res[2:2 + ns]), list(res[2 + ns:2 + ns + nl]), res[-1]


def _split_wait(body, send_sems, recv_sems, srcs, lands, after, name):
    ns, nl = len(srcs), len(lands)
    bufs = list(srcs) + list(lands)
    res = pl.pallas_call(
        body, name=name, out_shape=tuple(pltpu.HBM(v.shape, v.dtype) for v in bufs),
        in_specs=[HBM_SPEC] * (ns + nl) + [SEM_SPEC, SEM_SPEC, ANY_SPEC], out_specs=tuple([HBM_SPEC] * (ns + nl)),
        input_output_aliases={t: t for t in range(ns + nl)}, compiler_params=SPLIT_PARAMS,
    )(*bufs, send_sems, recv_sems, after)
    return list(res[:ns]), list(res[ns:])


def _chips_exchange_start(parts, after, name):
    nt = len(parts)
    lands = [lax.empty((3,) + p.shape[1:], p.dtype) for p in parts]

    def body(*refs):
        ins, lnd = refs[:nt], refs[nt:2 * nt]
        send_sems, recv_sems, token = refs[2 * nt + 1], refs[2 * nt + 2], refs[-1]
        x, y, c = _my_place()
        for t in range(nt):
            for m in range(1, 4):
                px, py = _flip(x, m & 2), _flip(y, m & 1)
                pltpu.make_async_remote_copy(
                    src_ref=ins[t].at[2 * px + py], dst_ref=lnd[t].at[m - 1], send_sem=send_sems.at[3 * t + m - 1],
                    recv_sem=recv_sems.at[3 * t + m - 1], device_id=(px, py, c), device_id_type=MESH).start()
        token[...] = jnp.zeros_like(token)

    return _split_start(body, parts, lands, after, 3, name)


def _chips_exchange_wait(send_sems, recv_sems, parts, lands, after, name):
    nt = len(parts)

    def body(*refs):
        ins, lnd = refs[:nt], refs[nt:2 * nt]
        s_sems, r_sems = refs[2 * nt], refs[2 * nt + 1]
        x, y, c = _my_place()
        for t in range(nt):
            for m in range(1, 4):
                cp = pltpu.make_async_remote_copy(
                    src_ref=ins[t].at[0], dst_ref=lnd[t].at[m - 1], send_sem=s_sems.at[3 * t + m - 1],
                    recv_sem=r_sems.at[3 * t + m - 1], device_id=(x, y, c), device_id_type=MESH)
                cp.wait_send()
                cp.wait_recv()

    return _split_wait(body, send_sems, recv_sems, parts, lands, after, name)


def _direct_exchange_start(grads, after, name):
    nt = len(grads)
    lands = [lax.empty((NDEV - 1,) + gr.shape[1:], gr.dtype) for gr in grads]

    def body(*refs):
        ins, lnd = refs[:nt], refs[nt:2 * nt]
        send_sems, recv_sems, token = refs[2 * nt + 1], refs[2 * nt + 2], refs[-1]
        x, y, c = _my_place()
        for t in range(nt):
            for k in range(1, NDEV):
                px, py, pc = _flip(x, k & 4), _flip(y, k & 2), _flip(c, k & 1)
                pltpu.make_async_remote_copy(
                    src_ref=ins[t].at[4 * px + 2 * py + pc], dst_ref=lnd[t].at[k - 1],
                    send_sem=send_sems.at[(NDEV - 1) * t + k - 1], recv_sem=recv_sems.at[(NDEV - 1) * t + k - 1],
                    device_id=(px, py, pc), device_id_type=MESH).start()
        token[...] = jnp.zeros_like(token)

    return _split_start(body, grads, lands, after, NDEV - 1, name)


def _direct_exchange_wait(send_sems, recv_sems, grads, lands, after, name):
    nt = len(grads)

    def body(*refs):
        ins, lnd = refs[:nt], refs[nt:2 * nt]
        s_sems, r_sems = refs[2 * nt], refs[2 * nt + 1]
        x, y, c = _my_place()
        for t in range(nt):
            for k in range(1, NDEV):
                cp = pltpu.make_async_remote_copy(
                    src_ref=ins[t].at[0], dst_ref=lnd[t].at[k - 1], send_sem=s_sems.at[(NDEV - 1) * t + k - 1],
                    recv_sem=r_sems.at[(NDEV - 1) * t + k - 1], device_id=(x, y, c), device_id_type=MESH)
                cp.wait_send()
                cp.wait_recv()

    return _split_wait(body, send_sems, recv_sems, grads, lands, after, name)


def _place_own(shards, axes, name):
    nt = len(shards)
    out_shapes = [SDS(tuple(s * NDEV if a == ax else s for a, s in enumerate(sh.shape)), sh.dtype)
                  for sh, ax in zip(shards, axes)]

    def body(*refs):
        ins, outs, sems = refs[:nt], refs[nt:2 * nt], refs[2 * nt]
        x, y, c = _my_place()
        copies = [pltpu.make_async_copy(ins[t], _slab(outs[t], axes[t], ins[t].shape[axes[t]], 4 * x + 2 * y + c), sems.at[t])
                  for t in range(nt)]
        for cp in copies:
            cp.start()
        for cp in copies:
            cp.wait()

    return pl.pallas_call(
        body, name=name, out_shape=out_shapes, in_specs=[VMEM_SPEC] * nt, out_specs=[HBM_SPEC] * nt,
        scratch_shapes=[pltpu.SemaphoreType.DMA((nt,))],
        compiler_params=pltpu.CompilerParams(vmem_limit_bytes=V7X_VMEM_LIMIT),
    )(*shards)


def _small_gather_start(v, me, after, name):
    land = lax.dynamic_update_slice(lax.empty((NDEV,) + v.shape, v.dtype), v[None], (me, 0, 0))

    def body(*refs):
        src, lnd = refs[0], refs[1]
        send_sems, recv_sems, token = refs[3], refs[4], refs[-1]
        x, y, c = _my_place()
        for k in range(1, NDEV):
            peer = (_flip(x, k & 4), _flip(y, k & 2), _flip(c, k & 1))
            pltpu.make_async_remote_copy(src_ref=src, dst_ref=lnd.at[4 * x + 2 * y + c], send_sem=send_sems.at[k - 1],
                                         recv_sem=recv_sems.at[k - 1], device_id=peer, device_id_type=MESH).start()
        token[...] = jnp.zeros_like(token)

    return _split_start(body, [v], [land], after, NDEV - 1, name)


def _small_gather_wait(send_sems, recv_sems, srcs, lands, after, name):
    def body(*refs):
        src, lnd, s_sems, r_sems = refs[0], refs[1], refs[2], refs[3]
        x, y, c = _my_place()
        for k in range(1, NDEV):
            sender = 4 * _flip(x, k & 4) + 2 * _flip(y, k & 2) + _flip(c, k & 1)
            cp = pltpu.make_async_remote_copy(src_ref=src, dst_ref=lnd.at[sender], send_sem=s_sems.at[k - 1],
                                              recv_sem=r_sems.at[k - 1], device_id=(x, y, c), device_id_type=MESH)
            cp.wait_send()
            cp.wait_recv()

    return _split_wait(body, send_sems, recv_sems, srcs, lands, after, name)[1][0]


def _gather_start(shards, axes, after, name):
    nt = len(shards)
    fulls = _place_own(shards, axes, name + "_own")

    def body(*refs):
        ins, outs = refs[:nt], refs[nt:2 * nt]
        send_sems, recv_sems, token = refs[2 * nt + 1], refs[2 * nt + 2], refs[-1]
        x, y, c = _my_place()
        dev = 4 * x + 2 * y + c
        peers = [(x, y, 1 - c), (1 - x, y, c), (x, 1 - y, c), (1 - x, 1 - y, c)]
        for t in range(nt):
            dst = _slab(outs[t], axes[t], ins[t].shape[axes[t]], dev)
            for k, peer in enumerate(peers):
                pltpu.make_async_remote_copy(src_ref=ins[t], dst_ref=dst, send_sem=send_sems.at[4 * t + k],
                                             recv_sem=recv_sems.at[4 * t + k], device_id=peer, device_id_type=MESH).start()
        token[...] = jnp.zeros_like(token)

    return _split_start(body, shards, fulls, after, 4, name)


def _gather_wait(send_sems, recv_sems, shards, fulls, axes, after, name):
    nt = len(shards)

    def body(*refs):
        ins, outs = refs[:nt], refs[nt:2 * nt]
        s_sems, r_sems = refs[2 * nt], refs[2 * nt + 1]
        x, y, c = _my_place()
        senders = [4 * x + 2 * y + (1 - c), 4 * (1 - x) + 2 * y + c, 4 * x + 2 * (1 - y) + c, 4 * (1 - x) + 2 * (1 - y) + c]
        for t in range(nt):
            for k, src_dev in enumerate(senders):
                cp = pltpu.make_async_remote_copy(
                    src_ref=ins[t], dst_ref=_slab(outs[t], axes[t], ins[t].shape[axes[t]], src_dev),
                    send_sem=s_sems.at[4 * t + k], recv_sem=r_sems.at[4 * t + k], device_id=(x, y, c), device_id_type=MESH)
                cp.wait_send()
                cp.wait_recv()

    return _split_wait(body, send_sems, recv_sems, shards, fulls, after, name)[1]


def _gather_pass_on(fulls, axes, name):
    nt = len(fulls)

    def body(*refs):
        outs = refs[nt:2 * nt]
        send_sems, recv_sems = refs[2 * nt:]
        x, y, c = _my_place()
        chips = [(1 - x, y), (x, 1 - y), (1 - x, 1 - y)]

        def copy(t, j, pc):
            cx, cy = chips[j]
            blk = _slab(outs[t], axes[t], outs[t].shape[axes[t]] // NDEV, 4 * cx + 2 * cy + pc)
            return pltpu.make_async_remote_copy(src_ref=blk, dst_ref=blk, send_sem=send_sems.at[t, j],
                                                recv_sem=recv_sems.at[t, j], device_id=(x, y, 1 - c), device_id_type=MESH)

        sends = [copy(t, j, c) for t in range(nt) for j in range(3)]
        for cp in sends:
            cp.start()
        for t in range(nt):
            for j in range(3):
                copy(t, j, 1 - c).wait_recv()
        for cp in sends:
            cp.wait_send()

    return pl.pallas_call(
        body, name=name, out_shape=[SDS(f.shape, f.dtype) for f in fulls], in_specs=[HBM_SPEC] * nt,
        out_specs=[HBM_SPEC] * nt, input_output_aliases={t: t for t in range(nt)},
        scratch_shapes=[pltpu.SemaphoreType.DMA((nt, 3)), pltpu.SemaphoreType.DMA((nt, 3))],
    )(*fulls)


def _row_tile(R, C, itemsize=4, target=1 << 20):
    best = R
    for tr in range(8, R, 8):
        if R % tr == 0 and tr * C * itemsize <= target:
            best = tr
    return best if best * C * itemsize <= target or best == R else R


def _as2d(a):
    return a.reshape(-1, a.shape[-1])


def _add_pairs(grad, land, c_me, name):
    blk = grad.shape[1:]
    NCH = NDEV // 2
    C = blk[-1]
    R = math.prod(blk[:-1])
    tr = _row_tile(R, C, 2)

    def body(c_ref, a_ref, b_ref, o_ref):
        o_ref[...] = (a_ref[...].astype(F32) + b_ref[...].astype(F32)).astype(o_ref.dtype)

    out = pl.pallas_call(
        body, name=name, out_shape=SDS((NCH, R, C), grad.dtype),
        grid_spec=pltpu.PrefetchScalarGridSpec(
            num_scalar_prefetch=1, grid=(NCH, R // tr),
            in_specs=[pl.BlockSpec((None, None, tr, C), lambda k, i, c: (k, c[0], i, 0)),
                      pl.BlockSpec((None, tr, C), lambda k, i, c: (k, i, 0))],
            out_specs=pl.BlockSpec((None, tr, C), lambda k, i, c: (k, i, 0))),
        compiler_params=_params(("parallel", "parallel")),
    )(c_me.reshape(1).astype(jnp.int32), grad.reshape(NCH, 2, R, C), land.reshape(NCH, R, C))
    return out.reshape((NCH,) + blk)


def _adamw_math(g, w, m, v):
    m2 = ADAM_B1 * m + (1.0 - ADAM_B1) * g
    v2 = ADAM_B2 * v + (1.0 - ADAM_B2) * (g * g)
    m_hat = m2 / (1.0 - ADAM_B1 ** ADAM_STEP)
    v_hat = v2 / (1.0 - ADAM_B2 ** ADAM_STEP)
    delta = -ADAM_LR * (m_hat / (jnp.sqrt(v_hat) + ADAM_EPS) + ADAM_WD * w)
    return delta, m2, v2


def _adamw(parts, w, m, v, name, layer=None, into=None):
    shp = w.shape
    w2, m2, v2 = _as2d(w), _as2d(m), _as2d(v)
    nlay = 1 if layer is None else shp[0]
    j = 0 if layer is None else layer
    RT, C = w2.shape
    R = RT // nlay
    tr = _row_tile(R, C)
    nblk = R // tr
    p3, specs, picks = [], [], []
    for p in parts:
        if isinstance(p, tuple):
            specs.append(pl.BlockSpec((1, tr, C), lambda i, s, k=len(picks): (s[k], i, 0)))
            picks.append(p[1])
            p = p[0]
        else:
            specs.append(pl.BlockSpec((p.size // (R * C), tr, C), lambda i, s: (0, i, 0)))
        p3.append(p.reshape((-1, R, C)))
    npart = len(p3)
    picks = jnp.stack(picks).astype(jnp.int32) if picks else jnp.zeros((1,), jnp.int32)
    rows = pl.BlockSpec((tr, C), lambda i, s: (j * nblk + i, 0))
    prior = [] if into is None else [_as2d(o) for o in into]

    def body(s_ref, *refs):
        prefs = refs[:npart]
        w_ref, m_ref, v_ref = refs[npart:npart + 3]
        g_ref, d_ref, nm_ref, nv_ref = refs[npart + 3 + len(prior):]
        g = None
        for pr in prefs:
            for k in range(pr.shape[0]):
                term = pr[k].astype(F32)
                g = term if g is None else g + term
        g_ref[...] = g
        d_ref[...], nm_ref[...], nv_ref[...] = _adamw_math(g, w_ref[...], m_ref[...], v_ref[...])

    outs = pl.pallas_call(
        body, name=name, out_shape=[SDS((RT, C), F32)] * 4,
        grid_spec=pltpu.PrefetchScalarGridSpec(
            num_scalar_prefetch=1, grid=(nblk,), in_specs=specs + [rows] * 3 + [ANY_SPEC] * len(prior), out_specs=[rows] * 4),
        input_output_aliases={1 + npart + 3 + k: k for k in range(len(prior))},
        compiler_params=_params(("parallel",)),
    )(picks, *p3, w2, m2, v2, *prior)
    return [o.reshape(shp) for o in outs]


def _ada_fwd(c_all, ada_w, ada_b_mine, name):
    L, D, Wc = ada_w.shape

    def body(c_ref, w_ref, b_ref, o_ref):
        cv = c_ref[...]
        act = cv * (1.0 / (1.0 + jnp.exp(-cv)))
        o_ref[0] = jnp.dot(act, w_ref[0], preferred_element_type=F32, precision=lax.Precision.HIGHEST) + b_ref[0]

    return pl.pallas_call(
        body, name=name, grid=(L,),
        in_specs=[_resident(c_all.shape), pl.BlockSpec((1, D, Wc), lambda l: (l, 0, 0)), pl.BlockSpec((1, 1, Wc), lambda l: (l, 0, 0))],
        out_specs=pl.BlockSpec((1, NDEV, Wc), lambda l: (l, 0, 0)), out_shape=SDS((L, NDEV, Wc), F32),
        compiler_params=_params(("parallel",)),
    )(c_all, ada_w, ada_b_mine.reshape(L, 1, Wc))


def _ada_bwd(c_all, dmod_mine, name):
    L, _, Wc = dmod_mine.shape
    D = c_all.shape[1]

    def body(c_ref, d_ref, o_ref):
        cv = c_ref[...]
        act = cv * (1.0 / (1.0 + jnp.exp(-cv)))
        o_ref[0] = lax.dot_general(act, d_ref[0], (((0,), (0,)), ((), ())), preferred_element_type=F32,
                                   precision=lax.Precision.HIGHEST)

    return pl.pallas_call(
        body, name=name, grid=(L,),
        in_specs=[_resident(c_all.shape), pl.BlockSpec((1, NDEV, Wc), lambda l: (l, 0, 0))],
        out_specs=pl.BlockSpec((1, D, Wc), lambda l: (l, 0, 0)), out_shape=SDS((L, D, Wc), F32),
        compiler_params=_params(("parallel",)),
    )(c_all, dmod_mine)


WEIGHT_NAMES = ['ada_w', 'ada_b', 'norm_mix_g', 'norm_mlp_g', 'pool_w', 'pool_scale', 'sgu_w_in', 'sgu_ln_g', 'sgu_ln_b',
                'sgu_w_s', 'sgu_b_s', 'sgu_w_out', 'mla_w_dq_dkv', 'mla_q_norm_g', 'mla_kv_norm_g', 'mla_w_uq', 'mla_w_ukv',
                'mla_w_o', 'mlp_w1', 'mlp_w2', 'final_g']
REPLICATED_EARLY = ['sgu_ln_g', 'sgu_ln_b', 'sgu_w_s', 'sgu_b_s', 'mla_kv_norm_g']
REPLICATED = ['ada_b', 'norm_mix_g', 'norm_mlp_g', 'final_g']
PACK_ROWS = 64
DIRECT_FROM = 1
LATE_MLP_LAYERS = (1,)
Q_HEAD = MLA_NOPE + MLA_ROPE


def _layer_matrices(i):
    kind, j = i % 3, i // 3
    if kind == 0:
        mats = [("pool_w", j, 1)]
    elif kind == 1:
        mats = [("sgu_w_in", j, 1), ("sgu_w_out", j, 0)]
    else:
        mats = [("mla_w_dq_dkv", j, 0), ("mla_w_uq", j, 1), ("mla_w_ukv", j, 1), ("mla_w_o", j, 0)]
    return mats + [("mlp_w1", i, 1), ("mlp_w2", i, 0)]


def _pack(arrays):
    flat = jnp.concatenate([a.reshape(-1).astype(F32) for a in arrays])
    rows = -(-flat.size // (LANES * PACK_ROWS)) * PACK_ROWS
    return jnp.pad(flat, (0, rows * LANES - flat.size)).reshape(rows, LANES)


def kernel(x, c, positions, ada_w, ada_b, norm_mix_g, norm_mlp_g, pool_w, pool_scale, sgu_w_in, sgu_ln_g, sgu_ln_b, sgu_w_s, sgu_b_s, sgu_w_out, mla_w_dq_dkv, mla_q_norm_g, mla_kv_norm_g, mla_w_uq, mla_w_ukv, mla_w_o, mlp_w1, mlp_w2, final_g, loss_target, m_ada_w, m_ada_b, m_norm_mix_g, m_norm_mlp_g, m_pool_w, m_pool_scale, m_sgu_w_in, m_sgu_ln_g, m_sgu_ln_b, m_sgu_w_s, m_sgu_b_s, m_sgu_w_out, m_mla_w_dq_dkv, m_mla_q_norm_g, m_mla_kv_norm_g, m_mla_w_uq, m_mla_w_ukv, m_mla_w_o, m_mlp_w1, m_mlp_w2, m_final_g, v_ada_w, v_ada_b, v_norm_mix_g, v_norm_mlp_g, v_pool_w, v_pool_scale, v_sgu_w_in, v_sgu_ln_g, v_sgu_ln_b, v_sgu_w_s, v_sgu_b_s, v_sgu_w_out, v_mla_w_dq_dkv, v_mla_q_norm_g, v_mla_kv_norm_g, v_mla_w_uq, v_mla_w_ukv, v_mla_w_o, v_mlp_w1, v_mlp_w2, v_final_g):
    a = dict(locals())
    S, D = x.shape[1], x.shape[2]
    L = ada_w.shape[0]
    Wc = ada_w.shape[2]
    me = 4 * lax.axis_index("x") + 2 * lax.axis_index("y") + lax.axis_index("c")
    my_chip = 2 * lax.axis_index("x") + lax.axis_index("y")

    v0 = _pack([c, pool_scale, mla_q_norm_g])
    g0 = _small_all_gather(v0, "gather_c").reshape(NDEV, -1)
    n_ps, n_qg = pool_scale.size, mla_q_norm_g.size
    c_all = g0[:, :D]
    ps_w = pool_scale.shape[1]
    ps_full = g0[:, D:D + n_ps].reshape(NDEV, -1, ps_w).transpose(1, 0, 2).reshape(-1, 1, D)
    qg_full = g0[:, D + n_ps:D + n_ps + n_qg].reshape(1, -1)

    ada_b_mine = lax.dynamic_slice_in_dim(ada_b, me * Wc, Wc, axis=1)
    modp = _ada_fwd(c_all, ada_w, ada_b_mine, "ada_fwd")
    ga = _small_all_gather(modp.reshape(-1, LANES), "gather_mod").reshape(NDEV, L, NDEV, Wc)
    mod = lax.dynamic_index_in_dim(ga, me, axis=2, keepdims=False).transpose(1, 0, 2).reshape(L, 6, D)
    mod8 = jnp.pad(mod, ((0, 0), (0, 2), (0, 0)))

    small = {"pool_scale": ps_full, "sgu_ln_g": sgu_ln_g, "sgu_ln_b": sgu_ln_b, "sgu_w_s": sgu_w_s[0],
             "sgu_b_s_t": sgu_b_s[0].T, "mla_q_norm_g": qg_full, "mla_kv_norm_g": mla_kv_norm_g}
    gmix, gmlp = norm_mix_g.reshape(L, 1, D), norm_mlp_g.reshape(L, 1, D)
    tables = _rope_tables(positions, S)

    def shards_of(mats):
        return [a[n][j].astype(BF16) for n, j, _ in mats], [ax for _, _, ax in mats]

    def as_weights(mats, fulls):
        w = {n: f for (n, _, _), f in zip(mats, fulls)}
        if "mla_w_uq" in w:
            lat_w = w["mla_w_dq_dkv"].shape[1]
            w["mla_wd"] = jnp.pad(w.pop("mla_w_dq_dkv"), ((0, 0), (0, -lat_w % LANES)))
            w["mla_wq"] = _pad_heads(w.pop("mla_w_uq"), Q_HEAD)
        return w

    def travelling(group, order, tag):
        shards, axes = shards_of(group)
        fly = _gather_start(shards, axes, order, f"gather_start_{tag}")

        def arrive(after):
            fulls = _gather_pass_on(_gather_wait(*fly[:4], axes, after, f"gather_wait_{tag}"), axes, f"gather_pass_{tag}")
            return as_weights(group, fulls), fulls[0]

        return fly[4], arrive

    xc = x[0]
    wts, saved, arrivals = [], [], []
    for i in range(L):
        mats = _layer_matrices(i)
        mod_i = mod8[i]
        if i == 0:
            fulls = _all_gather_group(*shards_of(mats[:-1]), mod8, "gather_w_0")
            w, order = as_weights(mats[:-1], fulls), fulls[0]
            token, arrive = travelling(mats[-1:], order, "0b")
            mod_i, order = mod_i + token[0, 0], token
            lazy = [(mats[-1:], arrive)]
        else:
            w, order = arrivals[0][1](xc)
            lazy = arrivals[1:]
        for group, arrive in lazy:
            for n, _, _ in group:
                w[n] = lambda after, arrive=arrive: arrive(after)[0]
        wts.append(w)
        arrivals = []
        if i + 1 < L:
            nxt = _layer_matrices(i + 1)
            for k, group in enumerate([nxt[:-2], nxt[-2:]] if i + 1 in LATE_MLP_LAYERS else [nxt]):
                token, arrive = travelling(group, order, f"{i + 1}" + ("m" if k else ""))
                mod_i, order = mod_i + token[0, 0], token
                arrivals.append((group, arrive))
        xc, sv = _layer_forward(i, xc, mod_i, gmix[i], gmlp[i], wts[i], small, tables)
        saved.append(sv)
    dx, loss_acc = _loss_head(xc, loss_target[0], final_g.reshape(1, D), _tiles(S)[0], "loss_head")

    res = {}
    c_me = lax.axis_index("c")

    def start_reduce(i, mats, g, after, tag=""):
        g = dict(g)
        if "mla_wq" in g:
            g["mla_w_dq_dkv"] = g.pop("mla_wd")[:, :mla_w_dq_dkv.shape[2]]
            g["mla_w_uq"] = _unpad_heads(g.pop("mla_wq"), Q_HEAD)
        gl = []
        for n, j, ax in mats:
            gm, blk = g[n].astype(BF16), a[n][j].shape
            if gm.shape != (NDEV,) + blk:
                gm = jnp.moveaxis(gm.reshape(blk[:ax] + (NDEV,) + blk[ax:]), ax, 0)
            gl.append(gm)
        if i >= DIRECT_FROM:
            return _direct_exchange_start(gl, after, f"rs_direct_start_{i}{tag}") + (mats, f"{i}{tag}", True)
        lands = _reduce_scatter_sibling(gl, f"rs_sibling_{i}{tag}")
        parts = [_add_pairs(gm, l, c_me, f"rs_add_{i}_{n}") for gm, l, (n, _, _) in zip(gl, lands, mats)]
        return _chips_exchange_start(parts, after, f"rs_chips_start_{i}{tag}") + (mats, f"{i}{tag}", False)

    def finish_reduce(fly, after):
        if fly[7]:
            parts, recv = _direct_exchange_wait(*fly[:4], after, f"rs_direct_wait_{fly[6]}")
            mine = me
        else:
            parts, recv = _chips_exchange_wait(*fly[:4], after, f"rs_chips_wait_{fly[6]}")
            mine = my_chip
        for (n, j, _), p, r in zip(fly[5], parts, recv):
            res[n] = _adamw([(p, mine), r], a[n], a["m_" + n], a["v_" + n], f"adamw_{n}_{j}", layer=j, into=res.get(n))

    def update_replicated(gathered, names, tail, label):
        zeros_tail = [jnp.zeros_like(t) for t in tail]
        packs = _adamw([gathered], _pack([a[n] for n in names] + zeros_tail), _pack([a["m_" + n] for n in names] + zeros_tail),
                       _pack([a["v_" + n] for n in names] + zeros_tail), label)
        flat = [t.reshape(-1) for t in packs]
        off = 0
        for n in names:
            res[n] = [f[off:off + a[n].size].reshape(a[n].shape) for f in flat]
            off += a[n].size
        sums = []
        for t in tail:
            sums.append(flat[0][off:off + t.size].reshape(t.shape))
            off += t.size
        return sums, packs[0]

    stats, sgrads, flying, early = [None] * L, {}, [], None
    for i in reversed(range(L)):
        mats = _layer_matrices(i)
        mod_i = mod8[i]
        for fly in flying:
            mod_i = mod_i + fly[4][0, 0]
        if early is not None:
            mod_i = mod_i + early[4][0, 0]
        sent = []

        def hook(g_mlp, i=i, mats=mats, dx=dx, sent=sent):
            sent.append(start_reduce(i, mats[-2:], g_mlp, dx, "a"))
            return sent[0][4][0, 0]

        dx, stats[i], g, sgr = _layer_backward(i, dx, saved[i], mod_i, gmix[i], gmlp[i], wts[i], small, tables, hook)
        sgrads.update(sgr)
        for fly in flying:
            finish_reduce(fly, dx)
        flying = sent + [start_reduce(i, mats[:-2], g, dx)]
        if i == 1:
            early_tail = [sgrads["mla_q_norm_g"]]
            early = _small_gather_start(_pack([sgrads[n] for n in REPLICATED_EARLY] + early_tail), me, flying[-1][4],
                                        "gather_small_start")
    (g_qg,), _ = update_replicated(_small_gather_wait(*early[:4], dx, "gather_small_wait"), REPLICATED_EARLY, early_tail,
                                   "adamw_replicated_early")

    sg = {"ada_b": jnp.stack([s[0:6] for s in stats]), "norm_mix_g": jnp.stack([s[6] for s in stats]),
          "norm_mlp_g": jnp.stack([s[7] for s in stats]), "final_g": loss_acc[0]}
    ps_grad = jnp.concatenate([sgrads[f"pool_scale_{j}"] for j in range(pool_scale.shape[0])])
    tail = [ps_grad, loss_acc[2, :LANES]]
    packed = _pack([sg[n] for n in REPLICATED] + tail) + flying[-1][4][0, 0]
    gathered = _small_all_gather(packed, "gather_small")
    (g_ps, loss_lanes), g_p = update_replicated(gathered, REPLICATED, tail, "adamw_replicated")
    for fly in flying:
        finish_reduce(fly, g_p)
    loss = loss_lanes[0]
    res["pool_scale"] = _adamw([lax.dynamic_slice_in_dim(g_ps, me * ps_w, ps_w, axis=1)], pool_scale, m_pool_scale,
                               v_pool_scale, "adamw_pool_scale")
    qg_w = mla_q_norm_g.shape[1]
    res["mla_q_norm_g"] = _adamw([lax.dynamic_slice_in_dim(g_qg, me * qg_w, qg_w, axis=1)], mla_q_norm_g, m_mla_q_norm_g,
                                 v_mla_q_norm_g, "adamw_q_norm_g")

    n_mod = L * 6 * D
    dmod_all = gathered.reshape(NDEV, -1)[:, :n_mod].reshape(NDEV, L, 6 * D)
    dmod_mine = lax.dynamic_slice_in_dim(dmod_all, me * Wc, Wc, axis=2).transpose(1, 0, 2)
    res["ada_w"] = _adamw([_ada_bwd(c_all, dmod_mine, "ada_bwd")], ada_w, m_ada_w, v_ada_w, "adamw_ada_w")

    return (loss, dx.reshape(x.shape), *[res[n][k] for k in range(4) for n in WEIGHT_NAMES])
```

```python
import math

import jax
import jax.numpy as jnp
from jax import lax
from jax.experimental import pallas as pl
from jax.experimental.pallas import tpu as pltpu

F32 = jnp.float32
BF16 = jnp.bfloat16
SDS = jax.ShapeDtypeStruct
MESH = pl.DeviceIdType.MESH

NDEV = 8
V7X_VMEM_LIMIT = 56 << 20
LANES = 128
RMS_EPS = 1e-6
LN_EPS = 1e-5
POOL_WINDOWS = (2, 4, 8, 16)
HALO = 16
SGU_CHUNK = 128
SGU_HEAD = 128
MLA_NOPE, MLA_ROPE, MLA_V = 128, 64, 128
MLA_Q_LORA, MLA_KV_LORA = 256, 128
MLA_HEAD_PAD = 256
ROPE_THETA = 10000.0
SM_SCALE = (MLA_NOPE + MLA_ROPE) ** -0.5
LOG2E, LN2 = 1.0 / math.log(2.0), math.log(2.0)
Q_SCALE = SM_SCALE * LOG2E
NEG = -1e30
ADAM_LR, ADAM_B1, ADAM_B2, ADAM_EPS, ADAM_WD, ADAM_STEP = 0.001, 0.9, 0.999, 1e-08, 0.01, 10
INV_SQRT2 = 1.0 / math.sqrt(2.0)
INV_SQRT_2PI = 1.0 / math.sqrt(2.0 * math.pi)
SH1, SC1, G1, SH2, SC2, G2 = 0, 1, 2, 3, 4, 5


def _params(sem=None, vmem=V7X_VMEM_LIMIT):
    return pltpu.CompilerParams(dimension_semantics=sem, vmem_limit_bytes=vmem)


def _resident(shape):
    nd = len(shape)
    return pl.BlockSpec(shape, lambda *_: (0,) * nd, pipeline_mode=pl.Buffered(1))


def _rows1(tm, w):
    return pl.BlockSpec((tm, w), lambda i: (i, 0))


def _rms(x):
    r = lax.rsqrt(jnp.mean(x * x, axis=-1, keepdims=True) + RMS_EPS)
    return x * r, r


def _colsum(v):
    return jnp.sum(v, axis=0, keepdims=True)


def _normmod_bwd(dh, n, r, a):
    dn = dh * a
    return r * (dn - n * jnp.mean(dn * n, axis=-1, keepdims=True))


def _dot(a, b):
    return jnp.dot(a, b, preferred_element_type=F32)


def _dot_nt(a, b):
    return lax.dot_general(a, b, (((1,), (1,)), ((), ())), preferred_element_type=F32)


def _dot_tn(a, b):
    return lax.dot_general(a, b, (((0,), (0,)), ((), ())), preferred_element_type=F32)


def _normal_cdf(x):
    return 0.5 * lax.erf(x * INV_SQRT2) + 0.5


def _gelu(x):
    return x * _normal_cdf(x)


def _gelu_grad(x):
    return _normal_cdf(x) + (x * INV_SQRT_2PI) * jnp.exp2(x * x * (-0.5 * LOG2E))


def _swap_halves(v):
    lane = lax.broadcasted_iota(jnp.int32, v.shape, 1)
    half = MLA_ROPE // 2
    return jnp.where(lane < half, pltpu.roll(v, LANES - half, 1),
                     jnp.where(lane < MLA_ROPE, pltpu.roll(v, half, 1), 0.0))


def _mlp_up(x1, mod, gn, w1, tm, tn, name):
    S, D = x1.shape
    Fh = w1.shape[1]

    def body(x_ref, mod_ref, gn_ref, w_ref, h_ref, r_ref):
        n, _ = _rms(x_ref[...])
        a = gn_ref[...] * (1.0 + mod_ref[SC2:SC2 + 1, :])
        h = (n * a + mod_ref[SH2:SH2 + 1, :]).astype(BF16)
        h_ref[...] = h
        for j in range(Fh // tn):
            cols = slice(j * tn, (j + 1) * tn)
            r_ref[:, cols] = jnp.maximum(_dot(h, w_ref[:, cols]), 0.0).astype(BF16)

    return pl.pallas_call(
        body, name=name, grid=(S // tm,),
        in_specs=[_rows1(tm, D), _resident(mod.shape), _resident(gn.shape), _resident(w1.shape)],
        out_specs=[_rows1(tm, D), _rows1(tm, Fh)],
        out_shape=[SDS((S, D), BF16), SDS((S, Fh), BF16)],
        compiler_params=_params(("parallel",)),
    )(x1, mod, gn, w1)


def _mlp_down(r, w2, x1, mod, tm, name):
    S, Fh = r.shape
    D = w2.shape[1]

    def body(r_ref, w_ref, x_ref, mod_ref, x2_ref, o_ref):
        rv = r_ref[...]
        o = _dot(rv * rv, w_ref[...])
        o_ref[...] = o.astype(BF16)
        x2_ref[...] = x_ref[...] + mod_ref[G2:G2 + 1, :] * o

    return pl.pallas_call(
        body, name=name, grid=(S // tm,),
        in_specs=[_rows1(tm, Fh), _resident(w2.shape), _rows1(tm, D), _resident(mod.shape)],
        out_specs=[_rows1(tm, D), _rows1(tm, D)],
        out_shape=[SDS((S, D), F32), SDS((S, D), BF16)],
        compiler_params=_params(("parallel",)),
    )(r, w2, x1, mod)


def _mlp_bwd_a(dx2, o, mod, w2, r, tm, tn, name):
    S, D = dx2.shape
    Fh = r.shape[1]

    def body(dx_ref, o_ref, mod_ref, w_ref, r_ref, da_ref, do_ref, st_ref):
        @pl.when(pl.program_id(0) == 0)
        def _():
            st_ref[...] = jnp.zeros_like(st_ref)

        dx = dx_ref[...]
        d_o = (dx * mod_ref[G2:G2 + 1, :]).astype(BF16)
        do_ref[...] = d_o
        st_ref[G2:G2 + 1, :] += _colsum(dx * o_ref[...].astype(F32))
        for j in range(Fh // tn):
            cols = slice(j * tn, (j + 1) * tn)
            dz = _dot_nt(d_o, w_ref[cols, :])
            da_ref[:, cols] = (dz * (2.0 * r_ref[:, cols].astype(F32))).astype(BF16)

    return pl.pallas_call(
        body, name=name, grid=(S // tm,),
        in_specs=[_rows1(tm, D), _rows1(tm, D), _resident(mod.shape), _resident(w2.shape), _rows1(tm, Fh)],
        out_specs=[_rows1(tm, Fh), _rows1(tm, D), pl.BlockSpec((8, D), lambda i: (0, 0))],
        out_shape=[SDS((S, Fh), BF16), SDS((S, D), BF16), SDS((8, D), F32)],
        compiler_params=_params(("arbitrary",)),
    )(dx2, o, mod, w2, r)


def _mlp_bwd_b(d_a, w1, x1, dx2, mod, gn, tm, name):
    S, Fh = d_a.shape
    D = w1.shape[0]

    def body(da_ref, w_ref, x_ref, dx_ref, mod_ref, gn_ref, dx1_ref, st_ref):
        @pl.when(pl.program_id(0) == 0)
        def _():
            st_ref[...] = jnp.zeros_like(st_ref)

        dh = _dot_nt(da_ref[...], w_ref[...])
        n, rr = _rms(x_ref[...])
        gn_v = gn_ref[...]
        sc1p = 1.0 + mod_ref[SC2:SC2 + 1, :]
        t = _colsum(dh * n)
        st_ref[SH2:SH2 + 1, :] += _colsum(dh)
        st_ref[SC2:SC2 + 1, :] += t * gn_v
        st_ref[6:7, :] += t * sc1p
        dx1_ref[...] = dx_ref[...] + _normmod_bwd(dh, n, rr, gn_v * sc1p)

    return pl.pallas_call(
        body, name=name, grid=(S // tm,),
        in_specs=[_rows1(tm, Fh), _resident(w1.shape), _rows1(tm, D), _rows1(tm, D), _resident(mod.shape),
                  _resident(gn.shape)],
        out_specs=[_rows1(tm, D), pl.BlockSpec((8, D), lambda i: (0, 0))],
        out_shape=[SDS((S, D), F32), SDS((8, D), F32)],
        compiler_params=_params(("arbitrary",)),
    )(d_a, w1, x1, dx2, mod, gn)


def _mm_tn(a, g, tk, tn, name, square_a=False, col_shards=False):
    S, K1 = a.shape
    N = g.shape[1]
    w = N // NDEV
    per = tn // w if col_shards else 1

    def body(a_ref, g_ref, o_ref):
        av = a_ref[...]
        if square_a:
            av = av * av
        res = _dot_tn(av, g_ref[...]).astype(BF16)
        if col_shards:
            for s in range(per):
                o_ref[s] = res[:, s * w:(s + 1) * w]
        else:
            o_ref[...] = res

    if col_shards:
        out_spec, out_shape = pl.BlockSpec((per, tk, w), lambda i, j: (j, i, 0)), SDS((NDEV, K1, w), BF16)
    else:
        out_spec, out_shape = pl.BlockSpec((tk, tn), lambda i, j: (i, j)), SDS((K1, N), BF16)
    return pl.pallas_call(
        body, name=name, grid=(K1 // tk, N // tn),
        in_specs=[pl.BlockSpec((S, tk), lambda i, j: (0, i)), pl.BlockSpec((S, tn), lambda i, j: (0, j))],
        out_specs=out_spec, out_shape=out_shape,
        compiler_params=_params(("parallel", "parallel")),
    )(a, g)


def _pool_h_ext(x_ref, xp_ref, mod_ref, gn_ref, i, tm):
    ext = jnp.concatenate([xp_ref[...], x_ref[...]], axis=0)
    n, r = _rms(ext)
    a = gn_ref[...] * (1.0 + mod_ref[SC1:SC1 + 1, :])
    h = n * a + mod_ref[SH1:SH1 + 1, :]
    row = lax.broadcasted_iota(jnp.int32, (tm + HALO, 1), 0)
    h = jnp.where(jnp.logical_and(i == 0, row < HALO), 0.0, h)
    return h, n[HALO:], r[HALO:], a


def _trailing_sum(v, win):
    k = 1
    while k < win:
        v = v + pltpu.roll(v, k, 0)
        k *= 2
    return v


def _leading_sum(v, win):
    k = 1
    while k < win:
        v = v + pltpu.roll(v, v.shape[0] - k, 0)
        k *= 2
    return v


def _pool_fwd(x, mod, gn, pw, ps, tm, name):
    S, D = x.shape
    C = D // len(POOL_WINDOWS)
    hb = tm // HALO

    def body(x_ref, xp_ref, mod_ref, gn_ref, pw_ref, ps_ref, x1_ref):
        i = pl.program_id(0)
        h, _, _, _ = _pool_h_ext(x_ref, xp_ref, mod_ref, gn_ref, i, tm)
        t1 = (i * tm + lax.broadcasted_iota(jnp.int32, (tm, 1), 0)).astype(F32) + 1.0
        for g, win in enumerate(POOL_WINDOWS):
            cols = slice(g * C, (g + 1) * C)
            hg = h[:, cols]
            inv = 1.0 / jnp.minimum(t1, float(win))
            pooled = (_trailing_sum(hg, win)[HALO:] * inv - hg[HALO:]).astype(BF16)
            y = _dot(pooled, pw_ref[g]) * ps_ref[:, cols]
            x1_ref[:, cols] = x_ref[:, cols] + mod_ref[G1:G1 + 1, cols] * y

    return pl.pallas_call(
        body, name=name, grid=(S // tm,),
        in_specs=[_rows1(tm, D), pl.BlockSpec((HALO, D), lambda i: (jnp.maximum(i * hb - 1, 0), 0)),
                  _resident(mod.shape), _resident(gn.shape), _resident(pw.shape), _resident(ps.shape)],
        out_specs=_rows1(tm, D),
        out_shape=SDS((S, D), F32),
        compiler_params=_params(("parallel",)),
    )(x, x, mod, gn, pw, ps)


def _pool_bwd(x, dx1, mod, gn, pw, ps, tm, name):
    S, D = x.shape
    G = len(POOL_WINDOWS)
    C = D // G
    hb = tm // HALO
    nt = S // tm

    def body(x_ref, xp_ref, d1_ref, dn_ref, mod_ref, gn_ref, pw_ref, ps_ref, dx_ref, st_ref, dpw_ref):
        i = pl.program_id(0)

        @pl.when(i == 0)
        def _():
            st_ref[...] = jnp.zeros_like(st_ref)
            dpw_ref[...] = jnp.zeros_like(dpw_ref)

        h, n, rr, a = _pool_h_ext(x_ref, xp_ref, mod_ref, gn_ref, i, tm)
        g1 = mod_ref[G1:G1 + 1, :]
        ps_v = ps_ref[...]
        d1 = d1_ref[...]
        d1n = jnp.where(i == nt - 1, 0.0, dn_ref[...])
        dyr = (jnp.concatenate([d1, d1n], axis=0) * (g1 * ps_v)).astype(BF16)
        t1 = (i * tm + lax.broadcasted_iota(jnp.int32, (tm + HALO, 1), 0)).astype(F32) + 1.0
        parts = []
        for g, win in enumerate(POOL_WINDOWS):
            cols = slice(g * C, (g + 1) * C)
            hg = h[:, cols]
            inv = 1.0 / jnp.minimum(t1, float(win))
            pooled = (_trailing_sum(hg, win)[HALO:] * inv[:tm] - hg[HALO:]).astype(BF16)
            yraw = _dot(pooled, pw_ref[g])
            st_ref[G1:G1 + 1, cols] += _colsum(d1[:, cols] * (yraw * ps_v[:, cols]))
            st_ref[4:5, cols] += _colsum(d1[:, cols] * g1[:, cols] * yraw)
            dpw_ref[g] += _dot_tn(pooled, dyr[:tm, cols])
            dpool = _dot_nt(dyr[:, cols], pw_ref[g])
            parts.append(_leading_sum(dpool * inv, win)[:tm] - dpool[:tm])
        dh = jnp.concatenate(parts, axis=1)
        t = _colsum(dh * n)
        st_ref[SH1:SH1 + 1, :] += _colsum(dh)
        st_ref[SC1:SC1 + 1, :] += t * gn_ref[...]
        st_ref[3:4, :] += t * (1.0 + mod_ref[SC1:SC1 + 1, :])
        dx_ref[...] = d1 + _normmod_bwd(dh, n, rr, a)

    return pl.pallas_call(
        body, name=name, grid=(nt,),
        in_specs=[_rows1(tm, D), pl.BlockSpec((HALO, D), lambda i: (jnp.maximum(i * hb - 1, 0), 0)),
                  _rows1(tm, D), pl.BlockSpec((HALO, D), lambda i: (jnp.minimum((i + 1) * hb, S // HALO - 1), 0)),
                  _resident(mod.shape), _resident(gn.shape), _resident(pw.shape), _resident(ps.shape)],
        out_specs=[_rows1(tm, D), pl.BlockSpec((8, D), lambda i: (0, 0)), pl.BlockSpec((G, C, C), lambda i: (0, 0, 0))],
        out_shape=[SDS((S, D), F32), SDS((8, D), F32), SDS((G, C, C), F32)],
        compiler_params=_params(("arbitrary",)),
    )(x, x, dx1, dx1, mod, gn, pw, ps)


def _tril_bf16(w):
    row = lax.broadcasted_iota(jnp.int32, w.shape, 0)
    col = lax.broadcasted_iota(jnp.int32, w.shape, 1)
    return jnp.where(col <= row, w, 0.0).astype(BF16)


def _sgu_front(pre, lng_ref, lnb_ref, W):
    z = _gelu(pre)
    u, v = z[:, :W], z[:, W:]
    mu = jnp.mean(v, axis=-1, keepdims=True)
    xc = v - mu
    rstd = lax.rsqrt(jnp.mean(xc * xc, axis=-1, keepdims=True) + LN_EPS)
    vhat = xc * rstd
    return u, vhat, rstd, vhat * lng_ref[...] + lnb_ref[...]


def _sgu_mix(vn, ws_ref, bst_ref, mix_s, tm, W):
    for hd in range(W // SGU_HEAD):
        wm = _tril_bf16(ws_ref[hd])
        bcol = bst_ref[:, hd:hd + 1]
        for ci in range(tm // SGU_CHUNK):
            rs, cs = slice(ci * SGU_CHUNK, (ci + 1) * SGU_CHUNK), slice(hd * SGU_HEAD, (hd + 1) * SGU_HEAD)
            mix_s[rs, cs] = _dot(wm, vn[rs, cs].astype(BF16)) + bcol


def _sgu_fwd(x, mod, gn, w_in, lng, lnb, ws, bst, w_out, tm, name):
    S, D = x.shape
    W = w_out.shape[0]

    def body(x_ref, mod_ref, gn_ref, win_ref, lng_ref, lnb_ref, ws_ref, bst_ref, wout_ref,
             x1_ref, h_ref, pre_ref, y_ref, mix_s):
        n, _ = _rms(x_ref[...])
        a = gn_ref[...] * (1.0 + mod_ref[SC1:SC1 + 1, :])
        h = (n * a + mod_ref[SH1:SH1 + 1, :]).astype(BF16)
        h_ref[...] = h
        pre = _dot(h, win_ref[...])
        pre_ref[...] = pre.astype(BF16)
        u, _, _, vn = _sgu_front(pre, lng_ref, lnb_ref, W)
        _sgu_mix(vn, ws_ref, bst_ref, mix_s, tm, W)
        y = _dot((u * mix_s[...]).astype(BF16), wout_ref[...])
        y_ref[...] = y.astype(BF16)
        x1_ref[...] = x_ref[...] + mod_ref[G1:G1 + 1, :] * y

    return pl.pallas_call(
        body, name=name, grid=(S // tm,),
        in_specs=[_rows1(tm, D), _resident(mod.shape), _resident(gn.shape), _resident(w_in.shape),
                  _resident(lng.shape), _resident(lnb.shape), _resident(ws.shape), _resident(bst.shape),
                  _resident(w_out.shape)],
        out_specs=[_rows1(tm, D), _rows1(tm, D), _rows1(tm, 2 * W), _rows1(tm, D)],
        out_shape=[SDS((S, D), F32), SDS((S, D), BF16), SDS((S, 2 * W), BF16), SDS((S, D), BF16)],
        scratch_shapes=[pltpu.VMEM((tm, W), F32)],
        compiler_params=_params(("parallel",)),
    )(x, mod, gn, w_in, lng, lnb, ws, bst, w_out)


def _sgu_bwd(x, dx1, pre, y, mod, gn, w_in, lng, lnb, ws, bst, w_out, tm, name):
    S, D = x.shape
    W = w_out.shape[0]
    H = W // SGU_HEAD
    nt = S // tm

    def body(x_ref, d1_ref, pre_ref, y_ref, mod_ref, gn_ref, win_ref, lng_ref, lnb_ref, ws_ref, bst_ref, wout_ref,
             dx_ref, dy_ref, gt_ref, dpre_ref, st_ref, dws_ref, dbs_ref, mix_s, dvn_s):
        i = pl.program_id(0)

        @pl.when(i == 0)
        def _():
            st_ref[...] = jnp.zeros_like(st_ref)
            dws_ref[...] = jnp.zeros_like(dws_ref)
            dbs_ref[...] = jnp.zeros_like(dbs_ref)

        d1 = d1_ref[...]
        pre = pre_ref[...].astype(F32)
        u, vhat, rstd, vn = _sgu_front(pre, lng_ref, lnb_ref, W)
        _sgu_mix(vn, ws_ref, bst_ref, mix_s, tm, W)
        mixed = mix_s[...]
        gt_ref[...] = (u * mixed).astype(BF16)
        dyb = (d1 * mod_ref[G1:G1 + 1, :]).astype(BF16)
        dy_ref[...] = dyb
        st_ref[G1:G1 + 1, :] += _colsum(d1 * y_ref[...].astype(F32))
        dgt = _dot_nt(dyb, wout_ref[...])
        du = dgt * mixed
        dmix = dgt * u
        for hd in range(H):
            wm = _tril_bf16(ws_ref[hd])
            for ci in range(tm // SGU_CHUNK):
                rs, cs = slice(ci * SGU_CHUNK, (ci + 1) * SGU_CHUNK), slice(hd * SGU_HEAD, (hd + 1) * SGU_HEAD)
                dm = dmix[rs, cs]
                dmb = dm.astype(BF16)
                dbs_ref[hd] += jnp.broadcast_to(jnp.sum(dm, axis=1, keepdims=True), (SGU_CHUNK, LANES))
                dws_ref[hd] += _dot_nt(dmb, vn[rs, cs].astype(BF16))
                dvn_s[rs, cs] = _dot_tn(wm, dmb)
        dvn = dvn_s[...]
        st_ref[4:5, :] += _colsum(dvn * vhat)
        st_ref[5:6, :] += _colsum(dvn)
        dvh = dvn * lng_ref[...]
        dv = rstd * (dvh - jnp.mean(dvh, axis=-1, keepdims=True) - vhat * jnp.mean(dvh * vhat, axis=-1, keepdims=True))
        dpre_u = (du * _gelu_grad(pre[:, :W])).astype(BF16)
        dpre_v = (dv * _gelu_grad(pre[:, W:])).astype(BF16)
        dpre_ref[:, :W] = dpre_u
        dpre_ref[:, W:] = dpre_v
        dh = _dot_nt(dpre_u, win_ref[:, :W]) + _dot_nt(dpre_v, win_ref[:, W:])
        n, rr = _rms(x_ref[...])
        gn_v = gn_ref[...]
        sc1p = 1.0 + mod_ref[SC1:SC1 + 1, :]
        t = _colsum(dh * n)
        st_ref[SH1:SH1 + 1, :] += _colsum(dh)
        st_ref[SC1:SC1 + 1, :] += t * gn_v
        st_ref[3:4, :] += t * sc1p
        dx_ref[...] = d1 + _normmod_bwd(dh, n, rr, gn_v * sc1p)

        @pl.when(i == nt - 1)
        def _():
            for hd in range(H):
                row = lax.broadcasted_iota(jnp.int32, (SGU_CHUNK, SGU_CHUNK), 0)
                col = lax.broadcasted_iota(jnp.int32, (SGU_CHUNK, SGU_CHUNK), 1)
                dws_ref[hd] = jnp.where(col <= row, dws_ref[hd], 0.0)

    return pl.pallas_call(
        body, name=name, grid=(nt,),
        in_specs=[_rows1(tm, D), _rows1(tm, D), _rows1(tm, 2 * W), _rows1(tm, D), _resident(mod.shape),
                  _resident(gn.shape), _resident(w_in.shape), _resident(lng.shape), _resident(lnb.shape),
                  _resident(ws.shape), _resident(bst.shape), _resident(w_out.shape)],
        out_specs=[_rows1(tm, D), _rows1(tm, D), _rows1(tm, W), _rows1(tm, 2 * W),
                   pl.BlockSpec((8, D), lambda i: (0, 0)), pl.BlockSpec((H, SGU_CHUNK, SGU_CHUNK), lambda i: (0, 0, 0)),
                   pl.BlockSpec((H, SGU_CHUNK, LANES), lambda i: (0, 0, 0))],
        out_shape=[SDS((S, D), F32), SDS((S, D), BF16), SDS((S, W), BF16), SDS((S, 2 * W), BF16),
                   SDS((8, D), F32), SDS((H, SGU_CHUNK, SGU_CHUNK), F32), SDS((H, SGU_CHUNK, LANES), F32)],
        scratch_shapes=[pltpu.VMEM((tm, W), F32), pltpu.VMEM((tm, W), F32)],
        compiler_params=_params(("arbitrary",)),
    )(x, dx1, pre, y, mod, gn, w_in, lng, lnb, ws, bst, w_out)


def _mla_lat(x, mod, gn, wd, qg, kvg, cos_t, sin_t, tm, name):
    S, D = x.shape
    LW = wd.shape[1]
    QL, KL = MLA_Q_LORA, MLA_KV_LORA

    def body(x_ref, mod_ref, gn_ref, wd_ref, qg_ref, kvg_ref, c_ref, s_ref, h_ref, lat_ref, cq_ref, ckv_ref, kr_ref):
        n, _ = _rms(x_ref[...])
        a = gn_ref[...] * (1.0 + mod_ref[SC1:SC1 + 1, :])
        h = (n * a + mod_ref[SH1:SH1 + 1, :]).astype(BF16)
        h_ref[...] = h
        lat = _dot(h, wd_ref[...])
        lat_ref[...] = lat
        nq, _ = _rms(lat[:, :QL])
        cq_ref[...] = (nq * qg_ref[...]).astype(BF16)
        nkv, _ = _rms(lat[:, QL:QL + KL])
        ckv_ref[...] = (nkv * kvg_ref[...]).astype(BF16)
        kr = lat[:, QL + KL:]
        kr_ref[...] = (kr * c_ref[...] + _swap_halves(kr) * s_ref[...]).astype(BF16)

    return pl.pallas_call(
        body, name=name, grid=(S // tm,),
        in_specs=[_rows1(tm, D), _resident(mod.shape), _resident(gn.shape), _resident(wd.shape), _resident(qg.shape),
                  _resident(kvg.shape), _rows1(tm, LANES), _rows1(tm, LANES)],
        out_specs=[_rows1(tm, D), _rows1(tm, LW), _rows1(tm, QL), _rows1(tm, KL), _rows1(tm, LANES)],
        out_shape=[SDS((S, D), BF16), SDS((S, LW), F32), SDS((S, QL), BF16), SDS((S, KL), BF16), SDS((S, LANES), BF16)],
        compiler_params=_params(("parallel",)),
    )(x, mod, gn, wd, qg, kvg, cos_t, sin_t)


def _mla_qkv(cq, ckv, krp, wq, wukv, cos_t, sin_t, tm, name):
    S = cq.shape[0]
    H = wq.shape[1] // MLA_HEAD_PAD
    HP = MLA_HEAD_PAD

    def body(cq_ref, ckv_ref, kr_ref, wq_ref, wkv_ref, c_ref, s_ref, q_ref, k_ref, v_ref):
        q = _dot(cq_ref[...], wq_ref[...])
        kv = _dot(ckv_ref[...], wkv_ref[...])
        cv, sv, krv = c_ref[...], s_ref[...], kr_ref[...]
        for h in range(H):
            qr = q[:, h * HP + MLA_NOPE:(h + 1) * HP]
            q_ref[:, h * HP:h * HP + MLA_NOPE] = (q[:, h * HP:h * HP + MLA_NOPE] * Q_SCALE).astype(BF16)
            q_ref[:, h * HP + MLA_NOPE:(h + 1) * HP] = ((qr * cv + _swap_halves(qr) * sv) * Q_SCALE).astype(BF16)
            k_ref[:, h * HP:h * HP + MLA_NOPE] = kv[:, h * HP:h * HP + MLA_NOPE].astype(BF16)
            k_ref[:, h * HP + MLA_NOPE:(h + 1) * HP] = krv
            v_ref[:, h * MLA_V:(h + 1) * MLA_V] = kv[:, h * HP + MLA_NOPE:(h + 1) * HP].astype(BF16)

    return pl.pallas_call(
        body, name=name, grid=(S // tm,),
        in_specs=[_rows1(tm, MLA_Q_LORA), _rows1(tm, MLA_KV_LORA), _rows1(tm, LANES), _resident(wq.shape),
                  _resident(wukv.shape), _rows1(tm, LANES), _rows1(tm, LANES)],
        out_specs=[_rows1(tm, H * HP), _rows1(tm, H * HP), _rows1(tm, H * MLA_V)],
        out_shape=[SDS((S, H * HP), BF16), SDS((S, H * HP), BF16), SDS((S, H * MLA_V), BF16)],
        compiler_params=_params(("parallel",)),
    )(cq, ckv, krp, wq, wukv, cos_t, sin_t)


def _causal_mask(nr, nc):
    row = lax.broadcasted_iota(jnp.int32, (nr, nc), 0)
    col = lax.broadcasted_iota(jnp.int32, (nr, nc), 1)
    return col <= row


def _attn_fwd(q, k, v, tq, name):
    S = q.shape[0]
    HP = MLA_HEAD_PAD
    H = q.shape[1] // HP
    nq = S // tq

    def body(q_ref, k_ref, v_ref, o_ref, lse_ref, v1):
        v1[:, :MLA_V] = v_ref[...]
        v1[:, MLA_V:] = jnp.ones((S, LANES), BF16)
        def update(qv, j, carry, masked):
            m, acc = carry
            krows = slice(j * tq, (j + 1) * tq)
            s = _dot_nt(qv, k_ref[krows, :])
            if masked:
                s = jnp.where(_causal_mask(tq, tq), s, NEG)
            m_new = jnp.maximum(m, jnp.max(s, axis=1, keepdims=True))
            p = jnp.exp2(s - m_new)
            return m_new, jnp.exp2(m - m_new) * acc + _dot(p.astype(BF16), v1[krows, :])

        for i in range(nq):
            rows = slice(i * tq, (i + 1) * tq)
            qv = q_ref[rows, :]
            carry = (jnp.full((tq, 1), NEG, F32), jnp.zeros((tq, MLA_V + LANES), F32))
            for j in range(i):
                carry = update(qv, j, carry, False)
            m, acc = update(qv, i, carry, True)
            l = acc[:, MLA_V:MLA_V + 1]
            o_ref[rows, :] = (acc[:, :MLA_V] / l).astype(BF16)
            lse_ref[0, rows, :] = jnp.broadcast_to(m + jnp.log2(l), (tq, LANES))

    return pl.pallas_call(
        body, name=name, grid=(H,),
        in_specs=[pl.BlockSpec((S, HP), lambda h: (0, h)), pl.BlockSpec((S, HP), lambda h: (0, h)),
                  pl.BlockSpec((S, MLA_V), lambda h: (0, h))],
        out_specs=[pl.BlockSpec((S, MLA_V), lambda h: (0, h)), pl.BlockSpec((1, S, LANES), lambda h: (h, 0, 0))],
        out_shape=[SDS((S, H * MLA_V), BF16), SDS((H, S, LANES), F32)],
        scratch_shapes=[pltpu.VMEM((S, MLA_V + LANES), BF16)],
        compiler_params=_params(("parallel",)),
    )(q, k, v)


def _attn_bwd(q, k, v, o, do, lse, cos_t, sin_t, tq, name):
    S = q.shape[0]
    HP = MLA_HEAD_PAD
    H = q.shape[1] // HP
    nq = S // tq

    def body(q_ref, k_ref, v_ref, o_ref, do_ref, lse_ref, c_ref, s_ref, dq_ref, dkv_ref, dkr_ref, dq_acc, dl_s):
        @pl.when(pl.program_id(0) == 0)
        def _():
            dkr_ref[...] = jnp.zeros_like(dkr_ref)

        dq_acc[...] = jnp.zeros_like(dq_acc)

        def delta_tile(i, _):
            rows = pl.ds(pl.multiple_of(i * tq, tq), tq)
            d = jnp.sum(do_ref[rows, :].astype(F32) * o_ref[rows, :].astype(F32), axis=1, keepdims=True)
            dl_s[rows, :] = jnp.broadcast_to(d, (tq, LANES))
            return 0

        lax.fori_loop(0, nq, delta_tile, 0)

        def update(i, kv_k, kv_v, carry, masked):
            dk, dv = carry
            rows = slice(i * tq, (i + 1) * tq)
            qv = q_ref[rows, :]
            dov = do_ref[rows, :]
            s = _dot_nt(qv, kv_k)
            if masked:
                s = jnp.where(_causal_mask(tq, tq), s, NEG)
            p = jnp.exp2(s - lse_ref[0, rows, 0:1])
            dv = dv + _dot_tn(p.astype(BF16), dov)
            dp = _dot_nt(dov, kv_v)
            ds = (p * (dp - dl_s[rows, 0:1])).astype(BF16)
            dk = dk + _dot_tn(ds, qv)
            dq_acc[rows, :] += _dot(ds, kv_k)
            return dk, dv

        for j in range(nq):
            krows = slice(j * tq, (j + 1) * tq)
            kv_k = k_ref[krows, :]
            kv_v = v_ref[krows, :]
            carry = update(j, kv_k, kv_v, (jnp.zeros((tq, HP), F32), jnp.zeros((tq, MLA_V), F32)), True)
            for i in range(j + 1, nq):
                carry = update(i, kv_k, kv_v, carry, False)
            dk, dv = carry
            dk = dk * LN2
            dkv_ref[krows, :MLA_NOPE] = dk[:, :MLA_NOPE].astype(BF16)
            dkv_ref[krows, MLA_NOPE:] = dv.astype(BF16)
            dkr_ref[krows, :] += dk[:, MLA_NOPE:]

        def out_tile(i, _):
            rows = pl.ds(pl.multiple_of(i * tq, tq), tq)
            dq = dq_acc[rows, :] * SM_SCALE
            dqr = dq[:, MLA_NOPE:]
            dq_ref[rows, :MLA_NOPE] = dq[:, :MLA_NOPE].astype(BF16)
            dq_ref[rows, MLA_NOPE:] = (dqr * c_ref[rows, :] + _swap_halves(dqr * s_ref[rows, :])).astype(BF16)
            return 0

        lax.fori_loop(0, nq, out_tile, 0)

    return pl.pallas_call(
        body, name=name, grid=(H,),
        in_specs=[pl.BlockSpec((S, HP), lambda h: (0, h)), pl.BlockSpec((S, HP), lambda h: (0, h)),
                  pl.BlockSpec((S, MLA_V), lambda h: (0, h)), pl.BlockSpec((S, MLA_V), lambda h: (0, h)),
                  pl.BlockSpec((S, MLA_V), lambda h: (0, h)), pl.BlockSpec((1, S, LANES), lambda h: (h, 0, 0)),
                  _resident(cos_t.shape), _resident(sin_t.shape)],
        out_specs=[pl.BlockSpec((S, HP), lambda h: (0, h)), pl.BlockSpec((S, HP), lambda h: (0, h)),
                   pl.BlockSpec((S, LANES), lambda h: (0, 0))],
        out_shape=[SDS((S, H * HP), BF16), SDS((S, H * HP), BF16), SDS((S, LANES), F32)],
        scratch_shapes=[pltpu.VMEM((S, HP), F32), pltpu.VMEM((S, LANES), F32)],
        compiler_params=_params(("arbitrary",)),
    )(q, k, v, o, do, lse, cos_t, sin_t)


def _mla_out(o, w_o, x, mod, tm, name):
    S, KO = o.shape
    D = w_o.shape[1]

    def body(o_ref, w_ref, x_ref, mod_ref, x1_ref, y_ref):
        y = _dot(o_ref[...], w_ref[...])
        y_ref[...] = y.astype(BF16)
        x1_ref[...] = x_ref[...] + mod_ref[G1:G1 + 1, :] * y

    return pl.pallas_call(
        body, name=name, grid=(S // tm,),
        in_specs=[_rows1(tm, KO), _resident(w_o.shape), _rows1(tm, D), _resident(mod.shape)],
        out_specs=[_rows1(tm, D), _rows1(tm, D)],
        out_shape=[SDS((S, D), F32), SDS((S, D), BF16)],
        compiler_params=_params(("parallel",)),
    )(o, w_o, x, mod)


def _mla_bwd_o(dx1, y, mod, w_o, tm, name):
    S, D = dx1.shape
    KO = w_o.shape[0]

    def body(d1_ref, y_ref, mod_ref, w_ref, dy_ref, do_ref, st_ref):
        @pl.when(pl.program_id(0) == 0)
        def _():
            st_ref[...] = jnp.zeros_like(st_ref)

        d1 = d1_ref[...]
        dyb = (d1 * mod_ref[G1:G1 + 1, :]).astype(BF16)
        dy_ref[...] = dyb
        st_ref[G1:G1 + 1, :] += _colsum(d1 * y_ref[...].astype(F32))
        do_ref[...] = _dot_nt(dyb, w_ref[...]).astype(BF16)

    return pl.pallas_call(
        body, name=name, grid=(S // tm,),
        in_specs=[_rows1(tm, D), _rows1(tm, D), _resident(mod.shape), _resident(w_o.shape)],
        out_specs=[_rows1(tm, D), _rows1(tm, KO), pl.BlockSpec((8, D), lambda i: (0, 0))],
        out_shape=[SDS((S, D), BF16), SDS((S, KO), BF16), SDS((8, D), F32)],
        compiler_params=_params(("arbitrary",)),
    )(dx1, y, mod, w_o)


def _mla_bwd_lat(dq, dkv, dkr, lat, x, dx1, mod, gn, qg, kvg, wq, wukv, wd, cos_t, sin_t, tm, name):
    S, D = x.shape
    LW = wd.shape[1]
    QL, KL = MLA_Q_LORA, MLA_KV_LORA

    def body(dq_ref, dkv_ref, dkr_ref, lat_ref, x_ref, d1_ref, mod_ref, gn_ref, qg_ref, kvg_ref, wq_ref, wkv_ref, wd_ref,
             c_ref, s_ref, dx_ref, dlat_ref, st_ref, dqg_ref, dkvg_ref):
        @pl.when(pl.program_id(0) == 0)
        def _():
            st_ref[...] = jnp.zeros_like(st_ref)
            dqg_ref[...] = jnp.zeros_like(dqg_ref)
            dkvg_ref[...] = jnp.zeros_like(dkvg_ref)

        lat = lat_ref[...]
        dcq = _dot_nt(dq_ref[...], wq_ref[...])
        nq, rq = _rms(lat[:, :QL])
        dqg_ref[0:1, :] += _colsum(dcq * nq)
        dlat_q = _normmod_bwd(dcq, nq, rq, qg_ref[...]).astype(BF16)
        dckv = _dot_nt(dkv_ref[...], wkv_ref[...])
        nkv, rkv = _rms(lat[:, QL:QL + KL])
        dkvg_ref[0:1, :] += _colsum(dckv * nkv)
        dlat_kv = _normmod_bwd(dckv, nkv, rkv, kvg_ref[...]).astype(BF16)
        dkr = dkr_ref[...]
        dlat_kr = (dkr * c_ref[...] + _swap_halves(dkr * s_ref[...])).astype(BF16)
        dlat_ref[:, :QL] = dlat_q
        dlat_ref[:, QL:QL + KL] = dlat_kv
        dlat_ref[:, QL + KL:] = dlat_kr
        dh = (_dot_nt(dlat_q, wd_ref[:, :QL]) + _dot_nt(dlat_kv, wd_ref[:, QL:QL + KL])
              + _dot_nt(dlat_kr, wd_ref[:, QL + KL:]))
        n, rr = _rms(x_ref[...])
        gn_v = gn_ref[...]
        sc1p = 1.0 + mod_ref[SC1:SC1 + 1, :]
        t = _colsum(dh * n)
        st_ref[SH1:SH1 + 1, :] += _colsum(dh)
        st_ref[SC1:SC1 + 1, :] += t * gn_v
        st_ref[3:4, :] += t * sc1p
        dx_ref[...] = d1_ref[...] + _normmod_bwd(dh, n, rr, gn_v * sc1p)

    HW = wq.shape[1]
    return pl.pallas_call(
        body, name=name, grid=(S // tm,),
        in_specs=[_rows1(tm, HW), _rows1(tm, HW), _rows1(tm, LANES), _rows1(tm, LW), _rows1(tm, D), _rows1(tm, D),
                  _resident(mod.shape), _resident(gn.shape), _resident(qg.shape), _resident(kvg.shape),
                  _resident(wq.shape), _resident(wukv.shape), _resident(wd.shape), _rows1(tm, LANES), _rows1(tm, LANES)],
        out_specs=[_rows1(tm, D), _rows1(tm, LW), pl.BlockSpec((8, D), lambda i: (0, 0)),
                   pl.BlockSpec((8, QL), lambda i: (0, 0)), pl.BlockSpec((8, KL), lambda i: (0, 0))],
        out_shape=[SDS((S, D), F32), SDS((S, LW), BF16), SDS((8, D), F32), SDS((8, QL), F32), SDS((8, KL), F32)],
        compiler_params=_params(("arbitrary",)),
    )(dq, dkv, dkr, lat, x, dx1, mod, gn, qg, kvg, wq, wukv, wd, cos_t, sin_t)


def _loss_head(x, tgt, fg, tm, name):
    S, D = x.shape
    nt = S // tm

    def body(x_ref, t_ref, g_ref, dx_ref, acc_ref):
        i = pl.program_id(0)

        @pl.when(i == 0)
        def _():
            acc_ref[...] = jnp.zeros_like(acc_ref)

        n, rr = _rms(x_ref[...])
        g = g_ref[...]
        err = n * g - t_ref[...]
        acc_ref[1:2, :] += _colsum(err * err) * (0.5 / D)
        dy = err * (1.0 / D)
        acc_ref[0:1, :] += _colsum(dy * n)
        dx_ref[...] = _normmod_bwd(dy, n, rr, g)

        @pl.when(i == nt - 1)
        def _():
            acc_ref[2:3, :] = jnp.broadcast_to(jnp.sum(acc_ref[1:2, :], axis=1, keepdims=True), (1, D))

    return pl.pallas_call(
        body, name=name, grid=(nt,),
        in_specs=[_rows1(tm, D), _rows1(tm, D), _resident(fg.shape)],
        out_specs=[_rows1(tm, D), pl.BlockSpec((8, D), lambda i: (0, 0))],
        out_shape=[SDS((S, D), F32), SDS((8, D), F32)],
        compiler_params=_params(("arbitrary",)),
    )(x, tgt, fg)


def _rope_tables(positions, S):
    inv_freq = ROPE_THETA ** (-jnp.arange(0, MLA_ROPE, 2, dtype=F32) / MLA_ROPE)
    ang = positions.reshape(S, 1).astype(F32) * inv_freq
    cos, sin = jnp.cos(ang), jnp.sin(ang)
    z = jnp.zeros((S, LANES - MLA_ROPE), F32)
    return jnp.concatenate([cos, cos, z], axis=1), jnp.concatenate([-sin, sin, z], axis=1)


def _pad_heads(w, per_head):
    K = w.shape[0]
    H = w.shape[1] // per_head
    w3 = w.reshape(K, H, per_head)
    return jnp.pad(w3, ((0, 0), (0, 0), (0, MLA_HEAD_PAD - per_head))).reshape(K, H * MLA_HEAD_PAD)


def _unpad_heads(w, per_head):
    K = w.shape[0]
    H = w.shape[1] // MLA_HEAD_PAD
    return w.reshape(K, H, MLA_HEAD_PAD)[:, :, :per_head].reshape(K, H * per_head)


def _tiles(S):
    return min(512, S), min(256, S), min(512, S)


def _layer_forward(i, x, mod, gmix, gmlp, w, small, tables):
    S, D = x.shape
    tm, tms, tq = _tiles(S)
    cos_t, sin_t = tables
    kind = i % 3
    sv = {"x": x}
    if kind == 0:
        x1 = _pool_fwd(x, mod, gmix, w["pool_w"], small["pool_scale"][i // 3], tm, f"pool_fwd_{i}")
    elif kind == 1:
        x1, sv["h"], sv["pre"], sv["y"] = _sgu_fwd(
            x, mod, gmix, w["sgu_w_in"], small["sgu_ln_g"], small["sgu_ln_b"], small["sgu_w_s"],
            small["sgu_b_s_t"], w["sgu_w_out"], tms, f"sgu_fwd_{i}")
    else:
        sv["h"], sv["lat"], sv["cq"], sv["ckv"], krp = _mla_lat(
            x, mod, gmix, w["mla_wd"], small["mla_q_norm_g"], small["mla_kv_norm_g"], cos_t, sin_t, tm,
            f"mla_lat_{i}")
        sv["q"], sv["k"], sv["v"] = _mla_qkv(sv["cq"], sv["ckv"], krp, w["mla_wq"], w["mla_w_ukv"], cos_t, sin_t,
                                             tm, f"mla_qkv_{i}")
        sv["o"], sv["lse"] = _attn_fwd(sv["q"], sv["k"], sv["v"], tq, f"attn_fwd_{i}")
        x1, sv["y"] = _mla_out(sv["o"], w["mla_w_o"], x, mod, tm, f"mla_out_{i}")
    sv["x1"] = x1
    if callable(w["mlp_w1"]):
        w.update(w["mlp_w1"](x1))
    Fh = w["mlp_w1"].shape[1]
    sv["h2"], sv["r"] = _mlp_up(x1, mod, gmlp, w["mlp_w1"], tm, min(2048, Fh), f"mlp_up_{i}")
    if callable(w["mlp_w2"]):
        w.update(w["mlp_w2"](sv["r"]))
    x, sv["o2"] = _mlp_down(sv["r"], w["mlp_w2"], x1, mod, tm, f"mlp_down_{i}")
    return x, sv


def _layer_backward(i, dx, sv, mod, gmix, gmlp, w, small, tables, on_mlp_grads=None):
    S, D = dx.shape
    tm, tms, tq = _tiles(S)
    cos_t, sin_t = tables
    kind = i % 3
    sgrads = {}
    Fh = w["mlp_w1"].shape[1]
    g = {}
    d_a, d_o, st_a = _mlp_bwd_a(dx, sv["o2"], mod, w["mlp_w2"], sv["r"], tm, min(2048, Fh), f"mlp_bwd_a_{i}")
    g["mlp_w2"] = _mm_tn(sv["r"], d_o, min(512, Fh), D, f"mlp_dw2_{i}", square_a=True)
    g["mlp_w1"] = _mm_tn(sv["h2"], d_a, D, min(512, Fh), f"mlp_dw1_{i}", col_shards=True)
    if on_mlp_grads is not None:
        mod = mod + on_mlp_grads({n: g.pop(n) for n in ("mlp_w1", "mlp_w2")})
    dx1, st_b = _mlp_bwd_b(d_a, w["mlp_w1"], sv["x1"], dx, mod, gmlp, tm, f"mlp_bwd_b_{i}")
    if kind == 0:
        dx, st_m, dpw = _pool_bwd(sv["x"], dx1, mod, gmix, w["pool_w"], small["pool_scale"][i // 3], tm,
                                  f"pool_bwd_{i}")
        g["pool_w"] = dpw
        sgrads[f"pool_scale_{i // 3}"] = st_m[4:5]
    elif kind == 1:
        dx, dyb, gated, dpre, st_m, dws, dbs = _sgu_bwd(
            sv["x"], dx1, sv["pre"], sv["y"], mod, gmix, w["sgu_w_in"], small["sgu_ln_g"], small["sgu_ln_b"],
            small["sgu_w_s"], small["sgu_b_s_t"], w["sgu_w_out"], tms, f"sgu_bwd_{i}")
        W = gated.shape[1]
        g["sgu_w_out"] = _mm_tn(gated, dyb, min(512, W), D, f"sgu_dwout_{i}")
        g["sgu_w_in"] = _mm_tn(sv["h"], dpre, D, min(512, 2 * W), f"sgu_dwin_{i}", col_shards=True)
        sgrads["sgu_ln_g"], sgrads["sgu_ln_b"] = st_m[4:5], st_m[5:6]
        sgrads["sgu_w_s"], sgrads["sgu_b_s"] = dws, dbs[:, :, 0]
    else:
        dyb, do, st_o = _mla_bwd_o(dx1, sv["y"], mod, w["mla_w_o"], tm, f"mla_bwd_o_{i}")
        KO = do.shape[1]
        g["mla_w_o"] = _mm_tn(sv["o"], dyb, min(512, KO), D, f"mla_dwo_{i}")
        dq, dkv, dkr = _attn_bwd(sv["q"], sv["k"], sv["v"], sv["o"], do, sv["lse"], cos_t, sin_t, tq, f"attn_bwd_{i}")
        dx, dlat, st_m, dqg, dkvg = _mla_bwd_lat(
            dq, dkv, dkr, sv["lat"], sv["x"], dx1, mod, gmix, small["mla_q_norm_g"], small["mla_kv_norm_g"],
            w["mla_wq"], w["mla_w_ukv"], w["mla_wd"], cos_t, sin_t, tm, f"mla_bwd_lat_{i}")
        HW = dq.shape[1]
        g["mla_wq"] = _mm_tn(sv["cq"], dq, MLA_Q_LORA, min(1024, HW), f"mla_dwq_{i}")
        g["mla_w_ukv"] = _mm_tn(sv["ckv"], dkv, MLA_KV_LORA, min(1024, HW), f"mla_dwukv_{i}", col_shards=True)
        g["mla_wd"] = _mm_tn(sv["h"], dlat, D, dlat.shape[1], f"mla_dwd_{i}")
        st_m = jnp.concatenate([st_m[0:2], st_o[2:3], st_m[3:]], axis=0)
        sgrads["mla_q_norm_g"], sgrads["mla_kv_norm_g"] = dqg[0:1], dkvg[0:1]
    stats = jnp.concatenate([st_m[0:3], st_b[3:5], st_a[5:6], st_m[3:4], st_b[6:7]], axis=0)
    return dx, stats, g, sgrads


def _local_step(x, tgt, positions, mod, gmix, gmlp, fg, wts, small):
    S = x.shape[0]
    L = mod.shape[0]
    tables = _rope_tables(positions, S)
    saved = []
    for i in range(L):
        x, sv = _layer_forward(i, x, mod[i], gmix[i], gmlp[i], wts[i], small, tables)
        saved.append(sv)
    dx, loss_acc = _loss_head(x, tgt, fg, _tiles(S)[0], "loss_head")
    stats, grads, sgrads = [None] * L, [None] * L, {}
    for i in reversed(range(L)):
        dx, stats[i], grads[i], sg = _layer_backward(i, dx, saved[i], mod[i], gmix[i], gmlp[i], wts[i], small, tables)
        sgrads.update(sg)
    return loss_acc, dx, stats, grads, sgrads


HBM_SPEC = pl.BlockSpec(memory_space=pltpu.HBM)
VMEM_SPEC = pl.BlockSpec(memory_space=pltpu.VMEM)


def _my_place():
    return lax.axis_index("x"), lax.axis_index("y"), lax.axis_index("c")


def _flip(v, bit):
    return 1 - v if bit else v


def _small_all_gather(v, name):
    R, C = v.shape

    def body(x_ref, out_ref, send_sems, recv_sems):
        x, y, c = _my_place()
        me = 4 * x + 2 * y + c
        out_ref[me] = x_ref[...]
        sends = []
        for k in range(1, NDEV):
            peer = (_flip(x, k & 4), _flip(y, k & 2), _flip(c, k & 1))
            cp = pltpu.make_async_remote_copy(src_ref=x_ref, dst_ref=out_ref.at[me], send_sem=send_sems.at[k - 1],
                                              recv_sem=recv_sems.at[k - 1], device_id=peer, device_id_type=MESH)
            cp.start()
            sends.append(cp)
        for k in range(1, NDEV):
            src = 4 * _flip(x, k & 4) + 2 * _flip(y, k & 2) + _flip(c, k & 1)
            pltpu.make_async_remote_copy(src_ref=x_ref, dst_ref=out_ref.at[src], send_sem=send_sems.at[k - 1],
                                         recv_sem=recv_sems.at[k - 1], device_id=(x, y, c), device_id_type=MESH).wait_recv()
        for cp in sends:
            cp.wait_send()

    return pl.pallas_call(
        body, name=name, out_shape=SDS((NDEV, R, C), v.dtype), in_specs=[VMEM_SPEC], out_specs=VMEM_SPEC,
        scratch_shapes=[pltpu.SemaphoreType.DMA((NDEV - 1,)), pltpu.SemaphoreType.DMA((NDEV - 1,))],
        compiler_params=pltpu.CompilerParams(vmem_limit_bytes=V7X_VMEM_LIMIT),
    )(v)


def _slab(ref, axis, width, dev):
    idx = [slice(None)] * len(ref.shape)
    idx[axis] = pl.ds(pl.multiple_of(dev * width, width), width)
    return ref.at[tuple(idx)]


def _all_gather_group(shards, axes, after, name):
    nt = len(shards)
    out_shapes = [SDS(tuple(s * NDEV if a == ax else s for a, s in enumerate(sh.shape)), sh.dtype)
                  for sh, ax in zip(shards, axes)]

    def body(*refs):
        ins, outs = refs[:nt], refs[nt + 1:2 * nt + 1]
        send_sems, recv_sems, local_sems = refs[2 * nt + 1:]
        x, y, c = _my_place()
        me = 4 * x + 2 * y + c
        sibling = (x, y, 1 - c)
        chips = [(1 - x, y), (x, 1 - y), (1 - x, 1 - y)]

        def block(t, dev):
            return _slab(outs[t], axes[t], ins[t].shape[axes[t]], dev)

        def copy(t, k, dev, to, src=None):
            return pltpu.make_async_remote_copy(
                src_ref=block(t, dev) if src is None else src, dst_ref=block(t, dev), send_sem=send_sems.at[t, k],
                recv_sem=recv_sems.at[t, k], device_id=to, device_id_type=MESH)

        mine = [pltpu.make_async_copy(ins[t], block(t, me), local_sems.at[t]) for t in range(nt)]
        for cp in mine:
            cp.start()
        first = []
        for t in range(nt):
            first.append(copy(t, 0, me, sibling, src=ins[t]))
            first += [copy(t, 1 + j, me, (cx, cy, c), src=ins[t]) for j, (cx, cy) in enumerate(chips)]
        for cp in first:
            cp.start()
        passed = []
        for j, (cx, cy) in enumerate(chips):
            for t in range(nt):
                copy(t, 1 + j, 4 * cx + 2 * cy + c, (x, y, c)).wait_recv()
                cp = copy(t, 4 + j, 4 * cx + 2 * cy + c, sibling)
                cp.start()
                passed.append(cp)
        for t in range(nt):
            copy(t, 0, 4 * x + 2 * y + (1 - c), (x, y, c)).wait_recv()
        for j, (cx, cy) in enumerate(chips):
            for t in range(nt):
                copy(t, 4 + j, 4 * cx + 2 * cy + (1 - c), (x, y, c)).wait_recv()
        for cp in first + passed:
            cp.wait_send()
        for cp in mine:
            cp.wait()

    return pl.pallas_call(
        body, name=name, out_shape=out_shapes, in_specs=[HBM_SPEC] * nt + [ANY_SPEC], out_specs=[HBM_SPEC] * nt,
        scratch_shapes=[pltpu.SemaphoreType.DMA((nt, NDEV - 1)), pltpu.SemaphoreType.DMA((nt, NDEV - 1)),
                        pltpu.SemaphoreType.DMA((nt,))],
    )(*shards, after)


def _reduce_scatter_sibling(grads, name):
    nt = len(grads)
    NCH = NDEV // 2
    out_shapes = [SDS((NCH,) + gr.shape[1:], gr.dtype) for gr in grads]

    def body(*refs):
        ins, lands = refs[:nt], refs[nt:2 * nt]
        send_sems, recv_sems = refs[2 * nt:]
        x, y, c = _my_place()
        sends = []
        for t in range(nt):
            for k in range(NCH):
                cp = pltpu.make_async_remote_copy(
                    src_ref=ins[t].at[2 * k + (1 - c)], dst_ref=lands[t].at[k], send_sem=send_sems.at[t, k],
                    recv_sem=recv_sems.at[t, k], device_id=(x, y, 1 - c), device_id_type=MESH)
                cp.start()
                sends.append(cp)
        for cp in sends:
            cp.wait_recv()
        for cp in sends:
            cp.wait_send()

    return pl.pallas_call(
        body, name=name, out_shape=out_shapes, in_specs=[HBM_SPEC] * nt, out_specs=[HBM_SPEC] * nt,
        scratch_shapes=[pltpu.SemaphoreType.DMA((nt, NCH)), pltpu.SemaphoreType.DMA((nt, NCH))],
    )(*grads)


SEM_SPEC = pl.BlockSpec(memory_space=pltpu.SEMAPHORE)
ANY_SPEC = pl.BlockSpec(memory_space=pl.ANY)
SPLIT_PARAMS = pltpu.CompilerParams(has_side_effects=pltpu.SideEffectType.DATAFLOW_SIDE_EFFECTING)
TOKEN = SDS((8, LANES), F32)


def _in_hbm(arrays):
    return [pltpu.with_memory_space_constraint(v, pltpu.HBM) for v in arrays]


def _split_start(body, srcs, lands, after, n_sem, name):
    ns, nl = len(srcs), len(lands)
    bufs = list(srcs) + list(lands)
    res = pl.pallas_call(
        body, name=name,
        out_shape=(pltpu.SemaphoreType.DMA((ns * n_sem,)), pltpu.SemaphoreType.DMA((ns * n_sem,)),
                   *[pltpu.HBM(v.shape, v.dtype) for v in bufs], TOKEN),
        in_specs=[HBM_SPEC] * (ns + nl) + [ANY_SPEC],
        out_specs=(SEM_SPEC, SEM_SPEC, *[HBM_SPEC] * (ns + nl), VMEM_SPEC),
        input_output_aliases={t: 2 + t for t in range(ns + nl)}, compiler_params=SPLIT_PARAMS,
    )(*_in_hbm(bufs), after)
    return res[0], res[1], list(res[2:2 + ns]), list(res[2 + ns:2 + ns + nl]), res[-1]


def _split_wait(body, send_sems, recv_sems, srcs, lands, after, name):
    ns, nl = len(srcs), len(lands)
    bufs = list(srcs) + list(lands)
    res = pl.pallas_call(
        body, name=name, out_shape=tuple(pltpu.HBM(v.shape, v.dtype) for v in bufs),
        in_specs=[HBM_SPEC] * (ns + nl) + [SEM_SPEC, SEM_SPEC, ANY_SPEC], out_specs=tuple([HBM_SPEC] * (ns + nl)),
        input_output_aliases={t: t for t in range(ns + nl)}, compiler_params=SPLIT_PARAMS,
    )(*bufs, send_sems, recv_sems, after)
    return list(res[:ns]), list(res[ns:])


def _chips_exchange_start(parts, after, name):
    nt = len(parts)
    lands = [lax.empty((3,) + p.shape[1:], p.dtype) for p in parts]

    def body(*refs):
        ins, lnd = refs[:nt], refs[nt:2 * nt]
        send_sems, recv_sems, token = refs[2 * nt + 1], refs[2 * nt + 2], refs[-1]
        x, y, c = _my_place()
        for t in range(nt):
            for m in range(1, 4):
                px, py = _flip(x, m & 2), _flip(y, m & 1)
                pltpu.make_async_remote_copy(
                    src_ref=ins[t].at[2 * px + py], dst_ref=lnd[t].at[m - 1], send_sem=send_sems.at[3 * t + m - 1],
                    recv_sem=recv_sems.at[3 * t + m - 1], device_id=(px, py, c), device_id_type=MESH).start()
        token[...] = jnp.zeros_like(token)

    return _split_start(body, parts, lands, after, 3, name)


def _chips_exchange_wait(send_sems, recv_sems, parts, lands, after, name):
    nt = len(parts)

    def body(*refs):
        ins, lnd = refs[:nt], refs[nt:2 * nt]
        s_sems, r_sems = refs[2 * nt], refs[2 * nt + 1]
        x, y, c = _my_place()
        for t in range(nt):
            for m in range(1, 4):
                cp = pltpu.make_async_remote_copy(
                    src_ref=ins[t].at[0], dst_ref=lnd[t].at[m - 1], send_sem=s_sems.at[3 * t + m - 1],
                    recv_sem=r_sems.at[3 * t + m - 1], device_id=(x, y, c), device_id_type=MESH)
                cp.wait_send()
                cp.wait_recv()

    return _split_wait(body, send_sems, recv_sems, parts, lands, after, name)


def _direct_exchange_start(grads, after, name):
    nt = len(grads)
    lands = [lax.empty((NDEV - 1,) + gr.shape[1:], gr.dtype) for gr in grads]

    def body(*refs):
        ins, lnd = refs[:nt], refs[nt:2 * nt]
        send_sems, recv_sems, token = refs[2 * nt + 1], refs[2 * nt + 2], refs[-1]
        x, y, c = _my_place()
        for t in range(nt):
            for k in range(1, NDEV):
                px, py, pc = _flip(x, k & 4), _flip(y, k & 2), _flip(c, k & 1)
                pltpu.make_async_remote_copy(
                    src_ref=ins[t].at[4 * px + 2 * py + pc], dst_ref=lnd[t].at[k - 1],
                    send_sem=send_sems.at[(NDEV - 1) * t + k - 1], recv_sem=recv_sems.at[(NDEV - 1) * t + k - 1],
                    device_id=(px, py, pc), device_id_type=MESH).start()
        token[...] = jnp.zeros_like(token)

    return _split_start(body, grads, lands, after, NDEV - 1, name)


def _direct_exchange_wait(send_sems, recv_sems, grads, lands, after, name):
    nt = len(grads)

    def body(*refs):
        ins, lnd = refs[:nt], refs[nt:2 * nt]
        s_sems, r_sems = refs[2 * nt], refs[2 * nt + 1]
        x, y, c = _my_place()
        for t in range(nt):
            for k in range(1, NDEV):
                cp = pltpu.make_async_remote_copy(
                    src_ref=ins[t].at[0], dst_ref=lnd[t].at[k - 1], send_sem=s_sems.at[(NDEV - 1) * t + k - 1],
                    recv_sem=r_sems.at[(NDEV - 1) * t + k - 1], device_id=(x, y, c), device_id_type=MESH)
                cp.wait_send()
                cp.wait_recv()

    return _split_wait(body, send_sems, recv_sems, grads, lands, after, name)


def _place_own(shards, axes, name):
    nt = len(shards)
    out_shapes = [SDS(tuple(s * NDEV if a == ax else s for a, s in enumerate(sh.shape)), sh.dtype)
                  for sh, ax in zip(shards, axes)]

    def body(*refs):
        ins, outs, sems = refs[:nt], refs[nt:2 * nt], refs[2 * nt]
        x, y, c = _my_place()
        copies = [pltpu.make_async_copy(ins[t], _slab(outs[t], axes[t], ins[t].shape[axes[t]], 4 * x + 2 * y + c), sems.at[t])
                  for t in range(nt)]
        for cp in copies:
            cp.start()
        for cp in copies:
            cp.wait()

    return pl.pallas_call(
        body, name=name, out_shape=out_shapes, in_specs=[VMEM_SPEC] * nt, out_specs=[HBM_SPEC] * nt,
        scratch_shapes=[pltpu.SemaphoreType.DMA((nt,))],
        compiler_params=pltpu.CompilerParams(vmem_limit_bytes=V7X_VMEM_LIMIT),
    )(*shards)


def _small_gather_start(v, me, after, name):
    land = lax.dynamic_update_slice(lax.empty((NDEV,) + v.shape, v.dtype), v[None], (me, 0, 0))

    def body(*refs):
        src, lnd = refs[0], refs[1]
        send_sems, recv_sems, token = refs[3], refs[4], refs[-1]
        x, y, c = _my_place()
        for k in range(1, NDEV):
            peer = (_flip(x, k & 4), _flip(y, k & 2), _flip(c, k & 1))
            pltpu.make_async_remote_copy(src_ref=src, dst_ref=lnd.at[4 * x + 2 * y + c], send_sem=send_sems.at[k - 1],
                                         recv_sem=recv_sems.at[k - 1], device_id=peer, device_id_type=MESH).start()
        token[...] = jnp.zeros_like(token)

    return _split_start(body, [v], [land], after, NDEV - 1, name)


def _small_gather_wait(send_sems, recv_sems, srcs, lands, after, name):
    def body(*refs):
        src, lnd, s_sems, r_sems = refs[0], refs[1], refs[2], refs[3]
        x, y, c = _my_place()
        for k in range(1, NDEV):
            sender = 4 * _flip(x, k & 4) + 2 * _flip(y, k & 2) + _flip(c, k & 1)
            cp = pltpu.make_async_remote_copy(src_ref=src, dst_ref=lnd.at[sender], send_sem=s_sems.at[k - 1],
                                              recv_sem=r_sems.at[k - 1], device_id=(x, y, c), device_id_type=MESH)
            cp.wait_send()
            cp.wait_recv()

    return _split_wait(body, send_sems, recv_sems, srcs, lands, after, name)[1][0]


def _gather_start(shards, axes, after, name):
    nt = len(shards)
    fulls = _place_own(shards, axes, name + "_own")

    def body(*refs):
        ins, outs = refs[:nt], refs[nt:2 * nt]
        send_sems, recv_sems, token = refs[2 * nt + 1], refs[2 * nt + 2], refs[-1]
        x, y, c = _my_place()
        dev = 4 * x + 2 * y + c
        peers = [(x, y, 1 - c), (1 - x, y, c), (x, 1 - y, c), (1 - x, 1 - y, c)]
        for t in range(nt):
            dst = _slab(outs[t], axes[t], ins[t].shape[axes[t]], dev)
            for k, peer in enumerate(peers):
                pltpu.make_async_remote_copy(src_ref=ins[t], dst_ref=dst, send_sem=send_sems.at[4 * t + k],
                                             recv_sem=recv_sems.at[4 * t + k], device_id=peer, device_id_type=MESH).start()
        token[...] = jnp.zeros_like(token)

    return _split_start(body, shards, fulls, after, 4, name)


def _gather_wait(send_sems, recv_sems, shards, fulls, axes, after, name):
    nt = len(shards)

    def body(*refs):
        ins, outs = refs[:nt], refs[nt:2 * nt]
        s_sems, r_sems = refs[2 * nt], refs[2 * nt + 1]
        x, y, c = _my_place()
        senders = [4 * x + 2 * y + (1 - c), 4 * (1 - x) + 2 * y + c, 4 * x + 2 * (1 - y) + c, 4 * (1 - x) + 2 * (1 - y) + c]
        for t in range(nt):
            for k, src_dev in enumerate(senders):
                cp = pltpu.make_async_remote_copy(
                    src_ref=ins[t], dst_ref=_slab(outs[t], axes[t], ins[t].shape[axes[t]], src_dev),
                    send_sem=s_sems.at[4 * t + k], recv_sem=r_sems.at[4 * t + k], device_id=(x, y, c), device_id_type=MESH)
                cp.wait_send()
                cp.wait_recv()

    return _split_wait(body, send_sems, recv_sems, shards, fulls, after, name)[1]


def _gather_pass_on(fulls, axes, name):
    nt = len(fulls)

    def body(*refs):
        outs = refs[nt:2 * nt]
        send_sems, recv_sems = refs[2 * nt:]
        x, y, c = _my_place()
        chips = [(1 - x, y), (x, 1 - y), (1 - x, 1 - y)]

        def copy(t, j, pc):
            cx, cy = chips[j]
            blk = _slab(outs[t], axes[t], outs[t].shape[axes[t]] // NDEV, 4 * cx + 2 * cy + pc)
            return pltpu.make_async_remote_copy(src_ref=blk, dst_ref=blk, send_sem=send_sems.at[t, j],
                                                recv_sem=recv_sems.at[t, j], device_id=(x, y, 1 - c), device_id_type=MESH)

        sends = [copy(t, j, c) for t in range(nt) for j in range(3)]
        for cp in sends:
            cp.start()
        for t in range(nt):
            for j in range(3):
                copy(t, j, 1 - c).wait_recv()
        for cp in sends:
            cp.wait_send()

    return pl.pallas_call(
        body, name=name, out_shape=[SDS(f.shape, f.dtype) for f in fulls], in_specs=[HBM_SPEC] * nt,
        out_specs=[HBM_SPEC] * nt, input_output_aliases={t: t for t in range(nt)},
        scratch_shapes=[pltpu.SemaphoreType.DMA((nt, 3)), pltpu.SemaphoreType.DMA((nt, 3))],
    )(*fulls)


def _row_tile(R, C, itemsize=4, target=1 << 20):
    best = R
    for tr in range(8, R, 8):
        if R % tr == 0 and tr * C * itemsize <= target:
            best = tr
    return best if best * C * itemsize <= target or best == R else R


def _as2d(a):
    return a.reshape(-1, a.shape[-1])


def _add_pairs(grad, land, c_me, name):
    blk = grad.shape[1:]
    NCH = NDEV // 2
    C = blk[-1]
    R = math.prod(blk[:-1])
    tr = _row_tile(R, C, 2)

    def body(c_ref, a_ref, b_ref, o_ref):
        o_ref[...] = (a_ref[...].astype(F32) + b_ref[...].astype(F32)).astype(o_ref.dtype)

    out = pl.pallas_call(
        body, name=name, out_shape=SDS((NCH, R, C), grad.dtype),
        grid_spec=pltpu.PrefetchScalarGridSpec(
            num_scalar_prefetch=1, grid=(NCH, R // tr),
            in_specs=[pl.BlockSpec((None, None, tr, C), lambda k, i, c: (k, c[0], i, 0)),
                      pl.BlockSpec((None, tr, C), lambda k, i, c: (k, i, 0))],
            out_specs=pl.BlockSpec((None, tr, C), lambda k, i, c: (k, i, 0))),
        compiler_params=_params(("parallel", "parallel")),
    )(c_me.reshape(1).astype(jnp.int32), grad.reshape(NCH, 2, R, C), land.reshape(NCH, R, C))
    return out.reshape((NCH,) + blk)


def _adamw_math(g, w, m, v):
    m2 = ADAM_B1 * m + (1.0 - ADAM_B1) * g
    v2 = ADAM_B2 * v + (1.0 - ADAM_B2) * (g * g)
    m_hat = m2 / (1.0 - ADAM_B1 ** ADAM_STEP)
    v_hat = v2 / (1.0 - ADAM_B2 ** ADAM_STEP)
    delta = -ADAM_LR * (m_hat / (jnp.sqrt(v_hat) + ADAM_EPS) + ADAM_WD * w)
    return delta, m2, v2


def _adamw(parts, w, m, v, name, layer=None, into=None):
    shp = w.shape
    w2, m2, v2 = _as2d(w), _as2d(m), _as2d(v)
    nlay = 1 if layer is None else shp[0]
    j = 0 if layer is None else layer
    RT, C = w2.shape
    R = RT // nlay
    tr = _row_tile(R, C)
    nblk = R // tr
    p3, specs, picks = [], [], []
    for p in parts:
        if isinstance(p, tuple):
            specs.append(pl.BlockSpec((1, tr, C), lambda i, s, k=len(picks): (s[k], i, 0)))
            picks.append(p[1])
            p = p[0]
        else:
            specs.append(pl.BlockSpec((p.size // (R * C), tr, C), lambda i, s: (0, i, 0)))
        p3.append(p.reshape((-1, R, C)))
    npart = len(p3)
    picks = jnp.stack(picks).astype(jnp.int32) if picks else jnp.zeros((1,), jnp.int32)
    rows = pl.BlockSpec((tr, C), lambda i, s: (j * nblk + i, 0))
    prior = [] if into is None else [_as2d(o) for o in into]

    def body(s_ref, *refs):
        prefs = refs[:npart]
        w_ref, m_ref, v_ref = refs[npart:npart + 3]
        g_ref, d_ref, nm_ref, nv_ref = refs[npart + 3 + len(prior):]
        g = None
        for pr in prefs:
            for k in range(pr.shape[0]):
                term = pr[k].astype(F32)
                g = term if g is None else g + term
        g_ref[...] = g
        d_ref[...], nm_ref[...], nv_ref[...] = _adamw_math(g, w_ref[...], m_ref[...], v_ref[...])

    outs = pl.pallas_call(
        body, name=name, out_shape=[SDS((RT, C), F32)] * 4,
        grid_spec=pltpu.PrefetchScalarGridSpec(
            num_scalar_prefetch=1, grid=(nblk,), in_specs=specs + [rows] * 3 + [ANY_SPEC] * len(prior), out_specs=[rows] * 4),
        input_output_aliases={1 + npart + 3 + k: k for k in range(len(prior))},
        compiler_params=_params(("parallel",)),
    )(picks, *p3, w2, m2, v2, *prior)
    return [o.reshape(shp) for o in outs]


def _ada_fwd(c_all, ada_w, ada_b_mine, name):
    L, D, Wc = ada_w.shape

    def body(c_ref, w_ref, b_ref, o_ref):
        cv = c_ref[...]
        act = cv * (1.0 / (1.0 + jnp.exp(-cv)))
        o_ref[0] = jnp.dot(act, w_ref[0], preferred_element_type=F32, precision=lax.Precision.HIGHEST) + b_ref[0]

    return pl.pallas_call(
        body, name=name, grid=(L,),
        in_specs=[_resident(c_all.shape), pl.BlockSpec((1, D, Wc), lambda l: (l, 0, 0)), pl.BlockSpec((1, 1, Wc), lambda l: (l, 0, 0))],
        out_specs=pl.BlockSpec((1, NDEV, Wc), lambda l: (l, 0, 0)), out_shape=SDS((L, NDEV, Wc), F32),
        compiler_params=_params(("parallel",)),
    )(c_all, ada_w, ada_b_mine.reshape(L, 1, Wc))


def _ada_update(c_all, dmod_mine, w, m, v, name):
    L, D, Wc = w.shape
    tr = _row_tile(D, Wc)

    def body(c_ref, d_ref, w_ref, m_ref, v_ref, g_ref, dl_ref, nm_ref, nv_ref):
        cv = c_ref[...]
        act = cv * (1.0 / (1.0 + jnp.exp(-cv)))
        g = lax.dot_general(act, d_ref[0], (((0,), (0,)), ((), ())), preferred_element_type=F32,
                            precision=lax.Precision.HIGHEST)
        g_ref[0] = g
        dl_ref[0], nm_ref[0], nv_ref[0] = _adamw_math(g, w_ref[0], m_ref[0], v_ref[0])

    blk = pl.BlockSpec((1, tr, Wc), lambda l, r: (l, r, 0))
    return pl.pallas_call(
        body, name=name, grid=(L, D // tr),
        in_specs=[pl.BlockSpec((NDEV, tr), lambda l, r: (0, r)), pl.BlockSpec((1, NDEV, Wc), lambda l, r: (l, 0, 0)),
                  blk, blk, blk],
        out_specs=[blk] * 4, out_shape=[SDS((L, D, Wc), F32)] * 4,
        compiler_params=_params(("parallel", "parallel")),
    )(c_all, dmod_mine, w, m, v)


WEIGHT_NAMES = ['ada_w', 'ada_b', 'norm_mix_g', 'norm_mlp_g', 'pool_w', 'pool_scale', 'sgu_w_in', 'sgu_ln_g', 'sgu_ln_b',
                'sgu_w_s', 'sgu_b_s', 'sgu_w_out', 'mla_w_dq_dkv', 'mla_q_norm_g', 'mla_kv_norm_g', 'mla_w_uq', 'mla_w_ukv',
                'mla_w_o', 'mlp_w1', 'mlp_w2', 'final_g']
REPLICATED_EARLY = ['sgu_ln_g', 'sgu_ln_b', 'sgu_w_s', 'sgu_b_s', 'mla_kv_norm_g']
REPLICATED = ['ada_b', 'norm_mix_g', 'norm_mlp_g', 'final_g']
PACK_ROWS = 64
DIRECT_FROM = 1
LATE_MLP_LAYERS = (1,)
Q_HEAD = MLA_NOPE + MLA_ROPE


def _layer_matrices(i):
    kind, j = i % 3, i // 3
    if kind == 0:
        mats = [("pool_w", j, 1)]
    elif kind == 1:
        mats = [("sgu_w_in", j, 1), ("sgu_w_out", j, 0)]
    else:
        mats = [("mla_w_dq_dkv", j, 0), ("mla_w_uq", j, 1), ("mla_w_ukv", j, 1), ("mla_w_o", j, 0)]
    return mats + [("mlp_w1", i, 1), ("mlp_w2", i, 0)]


def _pack(arrays):
    flat = jnp.concatenate([a.reshape(-1).astype(F32) for a in arrays])
    rows = -(-flat.size // (LANES * PACK_ROWS)) * PACK_ROWS
    return jnp.pad(flat, (0, rows * LANES - flat.size)).reshape(rows, LANES)


def kernel(x, c, positions, ada_w, ada_b, norm_mix_g, norm_mlp_g, pool_w, pool_scale, sgu_w_in, sgu_ln_g, sgu_ln_b, sgu_w_s, sgu_b_s, sgu_w_out, mla_w_dq_dkv, mla_q_norm_g, mla_kv_norm_g, mla_w_uq, mla_w_ukv, mla_w_o, mlp_w1, mlp_w2, final_g, loss_target, m_ada_w, m_ada_b, m_norm_mix_g, m_norm_mlp_g, m_pool_w, m_pool_scale, m_sgu_w_in, m_sgu_ln_g, m_sgu_ln_b, m_sgu_w_s, m_sgu_b_s, m_sgu_w_out, m_mla_w_dq_dkv, m_mla_q_norm_g, m_mla_kv_norm_g, m_mla_w_uq, m_mla_w_ukv, m_mla_w_o, m_mlp_w1, m_mlp_w2, m_final_g, v_ada_w, v_ada_b, v_norm_mix_g, v_norm_mlp_g, v_pool_w, v_pool_scale, v_sgu_w_in, v_sgu_ln_g, v_sgu_ln_b, v_sgu_w_s, v_sgu_b_s, v_sgu_w_out, v_mla_w_dq_dkv, v_mla_q_norm_g, v_mla_kv_norm_g, v_mla_w_uq, v_mla_w_ukv, v_mla_w_o, v_mlp_w1, v_mlp_w2, v_final_g):
    a = dict(locals())
    S, D = x.shape[1], x.shape[2]
    L = ada_w.shape[0]
    Wc = ada_w.shape[2]
    me = 4 * lax.axis_index("x") + 2 * lax.axis_index("y") + lax.axis_index("c")
    my_chip = 2 * lax.axis_index("x") + lax.axis_index("y")

    v0 = _pack([c, pool_scale, mla_q_norm_g])
    g0 = _small_all_gather(v0, "gather_c").reshape(NDEV, -1)
    n_ps, n_qg = pool_scale.size, mla_q_norm_g.size
    c_all = g0[:, :D]
    ps_w = pool_scale.shape[1]
    ps_full = g0[:, D:D + n_ps].reshape(NDEV, -1, ps_w).transpose(1, 0, 2).reshape(-1, 1, D)
    qg_full = g0[:, D + n_ps:D + n_ps + n_qg].reshape(1, -1)

    ada_b_mine = lax.dynamic_slice_in_dim(ada_b, me * Wc, Wc, axis=1)
    modp = _ada_fwd(c_all, ada_w, ada_b_mine, "ada_fwd")
    ga = _small_all_gather(modp.reshape(-1, LANES), "gather_mod").reshape(NDEV, L, NDEV, Wc)
    mod = lax.dynamic_index_in_dim(ga, me, axis=2, keepdims=False).transpose(1, 0, 2).reshape(L, 6, D)
    mod8 = jnp.pad(mod, ((0, 0), (0, 2), (0, 0)))

    small = {"pool_scale": ps_full, "sgu_ln_g": sgu_ln_g, "sgu_ln_b": sgu_ln_b, "sgu_w_s": sgu_w_s[0],
             "sgu_b_s_t": sgu_b_s[0].T, "mla_q_norm_g": qg_full, "mla_kv_norm_g": mla_kv_norm_g}
    gmix, gmlp = norm_mix_g.reshape(L, 1, D), norm_mlp_g.reshape(L, 1, D)
    tables = _rope_tables(positions, S)

    def shards_of(mats):
        return [a[n][j].astype(BF16) for n, j, _ in mats], [ax for _, _, ax in mats]

    def as_weights(mats, fulls):
        w = {n: f for (n, _, _), f in zip(mats, fulls)}
        if "mla_w_uq" in w:
            lat_w = w["mla_w_dq_dkv"].shape[1]
            w["mla_wd"] = jnp.pad(w.pop("mla_w_dq_dkv"), ((0, 0), (0, -lat_w % LANES)))
            w["mla_wq"] = _pad_heads(w.pop("mla_w_uq"), Q_HEAD)
        return w

    def travelling(group, order, tag):
        shards, axes = shards_of(group)
        fly = _gather_start(shards, axes, order, f"gather_start_{tag}")

        def arrive(after):
            fulls = _gather_pass_on(_gather_wait(*fly[:4], axes, after, f"gather_wait_{tag}"), axes, f"gather_pass_{tag}")
            return as_weights(group, fulls), fulls[0]

        return fly[4], arrive

    xc = x[0]
    wts, saved, arrivals = [], [], []
    for i in range(L):
        mats = _layer_matrices(i)
        mod_i = mod8[i]
        if i == 0:
            fulls = _all_gather_group(*shards_of(mats[:-1]), mod8, "gather_w_0")
            w, order = as_weights(mats[:-1], fulls), fulls[0]
            token, arrive = travelling(mats[-1:], order, "0b")
            mod_i, order = mod_i + token[0, 0], token
            lazy = [(mats[-1:], arrive)]
        else:
            w, order = arrivals[0][1](xc)
            lazy = arrivals[1:]
        for group, arrive in lazy:
            for n, _, _ in group:
                w[n] = lambda after, arrive=arrive: arrive(after)[0]
        wts.append(w)
        arrivals = []
        if i + 1 < L:
            nxt = _layer_matrices(i + 1)
            for k, group in enumerate([nxt[:-2], nxt[-2:]] if i + 1 in LATE_MLP_LAYERS else [nxt]):
                token, arrive = travelling(group, order, f"{i + 1}" + ("m" if k else ""))
                mod_i, order = mod_i + token[0, 0], token
                arrivals.append((group, arrive))
        xc, sv = _layer_forward(i, xc, mod_i, gmix[i], gmlp[i], wts[i], small, tables)
        saved.append(sv)
    dx, loss_acc = _loss_head(xc, loss_target[0], final_g.reshape(1, D), _tiles(S)[0], "loss_head")

    res = {}
    c_me = lax.axis_index("c")

    def start_reduce(i, mats, g, after, tag=""):
        g = dict(g)
        if "mla_wq" in g:
            g["mla_w_dq_dkv"] = g.pop("mla_wd")[:, :mla_w_dq_dkv.shape[2]]
            g["mla_w_uq"] = _unpad_heads(g.pop("mla_wq"), Q_HEAD)
        gl = []
        for n, j, ax in mats:
            gm, blk = g[n].astype(BF16), a[n][j].shape
            if gm.shape != (NDEV,) + blk:
                gm = jnp.moveaxis(gm.reshape(blk[:ax] + (NDEV,) + blk[ax:]), ax, 0)
            gl.append(gm)
        if i >= DIRECT_FROM:
            return _direct_exchange_start(gl, after, f"rs_direct_start_{i}{tag}") + (mats, f"{i}{tag}", True)
        lands = _reduce_scatter_sibling(gl, f"rs_sibling_{i}{tag}")
        parts = [_add_pairs(gm, l, c_me, f"rs_add_{i}_{n}") for gm, l, (n, _, _) in zip(gl, lands, mats)]
        return _chips_exchange_start(parts, after, f"rs_chips_start_{i}{tag}") + (mats, f"{i}{tag}", False)

    def finish_reduce(fly, after):
        if fly[7]:
            parts, recv = _direct_exchange_wait(*fly[:4], after, f"rs_direct_wait_{fly[6]}")
            mine = me
        else:
            parts, recv = _chips_exchange_wait(*fly[:4], after, f"rs_chips_wait_{fly[6]}")
            mine = my_chip
        for (n, j, _), p, r in zip(fly[5], parts, recv):
            res[n] = _adamw([(p, mine), r], a[n], a["m_" + n], a["v_" + n], f"adamw_{n}_{j}", layer=j, into=res.get(n))

    def update_replicated(gathered, names, tail, label):
        zeros_tail = [jnp.zeros_like(t) for t in tail]
        packs = _adamw([gathered], _pack([a[n] for n in names] + zeros_tail), _pack([a["m_" + n] for n in names] + zeros_tail),
                       _pack([a["v_" + n] for n in names] + zeros_tail), label)
        flat = [t.reshape(-1) for t in packs]
        off = 0
        for n in names:
            res[n] = [f[off:off + a[n].size].reshape(a[n].shape) for f in flat]
            off += a[n].size
        sums = []
        for t in tail:
            sums.append(flat[0][off:off + t.size].reshape(t.shape))
            off += t.size
        return sums, packs[0]

    stats, sgrads, flying, early = [None] * L, {}, [], None
    for i in reversed(range(L)):
        mats = _layer_matrices(i)
        mod_i = mod8[i]
        for fly in flying:
            mod_i = mod_i + fly[4][0, 0]
        if early is not None:
            mod_i = mod_i + early[4][0, 0]
        sent = []

        def hook(g_mlp, i=i, mats=mats, dx=dx, sent=sent):
            sent.append(start_reduce(i, mats[-2:], g_mlp, dx, "a"))
            return sent[0][4][0, 0]

        dx, stats[i], g, sgr = _layer_backward(i, dx, saved[i], mod_i, gmix[i], gmlp[i], wts[i], small, tables, hook)
        sgrads.update(sgr)
        for fly in flying:
            finish_reduce(fly, dx)
        flying = sent + [start_reduce(i, mats[:-2], g, dx)]
        if i == 1:
            early_tail = [sgrads["mla_q_norm_g"]]
            early = _small_gather_start(_pack([sgrads[n] for n in REPLICATED_EARLY] + early_tail), me, flying[-1][4],
                                        "gather_small_start")
    (g_qg,), _ = update_replicated(_small_gather_wait(*early[:4], dx, "gather_small_wait"), REPLICATED_EARLY, early_tail,
                                   "adamw_replicated_early")

    sg = {"ada_b": jnp.stack([s[0:6] for s in stats]), "norm_mix_g": jnp.stack([s[6] for s in stats]),
          "norm_mlp_g": jnp.stack([s[7] for s in stats]), "final_g": loss_acc[0]}
    ps_grad = jnp.concatenate([sgrads[f"pool_scale_{j}"] for j in range(pool_scale.shape[0])])
    tail = [ps_grad, loss_acc[2, :LANES]]
    packed = _pack([sg[n] for n in REPLICATED] + tail) + flying[-1][4][0, 0]
    gathered = _small_all_gather(packed, "gather_small")
    (g_ps, loss_lanes), g_p = update_replicated(gathered, REPLICATED, tail, "adamw_replicated")
    for fly in flying:
        finish_reduce(fly, g_p)
    loss = loss_lanes[0]
    res["pool_scale"] = _adamw([lax.dynamic_slice_in_dim(g_ps, me * ps_w, ps_w, axis=1)], pool_scale, m_pool_scale,
                               v_pool_scale, "adamw_pool_scale")
    qg_w = mla_q_norm_g.shape[1]
    res["mla_q_norm_g"] = _adamw([lax.dynamic_slice_in_dim(g_qg, me * qg_w, qg_w, axis=1)], mla_q_norm_g, m_mla_q_norm_g,
                                 v_mla_q_norm_g, "adamw_q_norm_g")

    n_mod = L * 6 * D
    dmod_all = gathered.reshape(NDEV, -1)[:, :n_mod].reshape(NDEV, L, 6 * D)
    dmod_mine = lax.dynamic_slice_in_dim(dmod_all, me * Wc, Wc, axis=2).transpose(1, 0, 2)
    res["ada_w"] = _ada_update(c_all, dmod_mine, ada_w, m_ada_w, v_ada_w, "adamw_ada_w")

    return (loss, dx.reshape(x.shape), *[res[n][k] for k in range(4) for n in WEIGHT_NAMES])
```

```python
import math

import jax
import jax.numpy as jnp
from jax import lax
from jax.experimental import pallas as pl
from jax.experimental.pallas import tpu as pltpu

F32 = jnp.float32
BF16 = jnp.bfloat16
SDS = jax.ShapeDtypeStruct
MESH = pl.DeviceIdType.MESH

NDEV = 8
V7X_VMEM_LIMIT = 56 << 20
LANES = 128
RMS_EPS = 1e-6
LN_EPS = 1e-5
POOL_WINDOWS = (2, 4, 8, 16)
HALO = 16
SGU_CHUNK = 128
SGU_HEAD = 128
MLA_NOPE, MLA_ROPE, MLA_V = 128, 64, 128
MLA_Q_LORA, MLA_KV_LORA = 256, 128
MLA_HEAD_PAD = 256
ROPE_THETA = 10000.0
SM_SCALE = (MLA_NOPE + MLA_ROPE) ** -0.5
LOG2E, LN2 = 1.0 / math.log(2.0), math.log(2.0)
Q_SCALE = SM_SCALE * LOG2E
NEG = -1e30
ADAM_LR, ADAM_B1, ADAM_B2, ADAM_EPS, ADAM_WD, ADAM_STEP = 0.001, 0.9, 0.999, 1e-08, 0.01, 10
INV_SQRT2 = 1.0 / math.sqrt(2.0)
INV_SQRT_2PI = 1.0 / math.sqrt(2.0 * math.pi)
SH1, SC1, G1, SH2, SC2, G2 = 0, 1, 2, 3, 4, 5


def _params(sem=None, vmem=V7X_VMEM_LIMIT):
    return pltpu.CompilerParams(dimension_semantics=sem, vmem_limit_bytes=vmem)


def _resident(shape):
    nd = len(shape)
    return pl.BlockSpec(shape, lambda *_: (0,) * nd, pipeline_mode=pl.Buffered(1))


def _rows1(tm, w):
    return pl.BlockSpec((tm, w), lambda i: (i, 0))


def _rms(x):
    r = lax.rsqrt(jnp.mean(x * x, axis=-1, keepdims=True) + RMS_EPS)
    return x * r, r


def _colsum(v):
    return jnp.sum(v, axis=0, keepdims=True)


def _normmod_bwd(dh, n, r, a):
    dn = dh * a
    return r * (dn - n * jnp.mean(dn * n, axis=-1, keepdims=True))


def _dot(a, b):
    return jnp.dot(a, b, preferred_element_type=F32)


def _dot_nt(a, b):
    return lax.dot_general(a, b, (((1,), (1,)), ((), ())), preferred_element_type=F32)


def _dot_tn(a, b):
    return lax.dot_general(a, b, (((0,), (0,)), ((), ())), preferred_element_type=F32)


def _normal_cdf(x):
    return 0.5 * lax.erf(x * INV_SQRT2) + 0.5


def _gelu(x):
    return x * _normal_cdf(x)


def _gelu_grad(x):
    return _normal_cdf(x) + (x * INV_SQRT_2PI) * jnp.exp2(x * x * (-0.5 * LOG2E))


def _swap_halves(v):
    lane = lax.broadcasted_iota(jnp.int32, v.shape, 1)
    half = MLA_ROPE // 2
    return jnp.where(lane < half, pltpu.roll(v, LANES - half, 1),
                     jnp.where(lane < MLA_ROPE, pltpu.roll(v, half, 1), 0.0))


def _mlp_up(x1, mod, gn, w1, tm, tn, name):
    S, D = x1.shape
    Fh = w1.shape[1]

    def body(x_ref, mod_ref, gn_ref, w_ref, h_ref, r_ref):
        n, _ = _rms(x_ref[...])
        a = gn_ref[...] * (1.0 + mod_ref[SC2:SC2 + 1, :])
        h = (n * a + mod_ref[SH2:SH2 + 1, :]).astype(BF16)
        h_ref[...] = h
        for j in range(Fh // tn):
            cols = slice(j * tn, (j + 1) * tn)
            r_ref[:, cols] = jnp.maximum(_dot(h, w_ref[:, cols]), 0.0).astype(BF16)

    return pl.pallas_call(
        body, name=name, grid=(S // tm,),
        in_specs=[_rows1(tm, D), _resident(mod.shape), _resident(gn.shape), _resident(w1.shape)],
        out_specs=[_rows1(tm, D), _rows1(tm, Fh)],
        out_shape=[SDS((S, D), BF16), SDS((S, Fh), BF16)],
        compiler_params=_params(("parallel",)),
    )(x1, mod, gn, w1)


def _mlp_down(r, w2, x1, mod, tm, name):
    S, Fh = r.shape
    D = w2.shape[1]

    def body(r_ref, w_ref, x_ref, mod_ref, x2_ref, o_ref):
        rv = r_ref[...]
        o = _dot(rv * rv, w_ref[...])
        o_ref[...] = o.astype(BF16)
        x2_ref[...] = x_ref[...] + mod_ref[G2:G2 + 1, :] * o

    return pl.pallas_call(
        body, name=name, grid=(S // tm,),
        in_specs=[_rows1(tm, Fh), _resident(w2.shape), _rows1(tm, D), _resident(mod.shape)],
        out_specs=[_rows1(tm, D), _rows1(tm, D)],
        out_shape=[SDS((S, D), F32), SDS((S, D), BF16)],
        compiler_params=_params(("parallel",)),
    )(r, w2, x1, mod)


def _mlp_bwd_a(dx2, o, mod, w2, r, tm, tn, name):
    S, D = dx2.shape
    Fh = r.shape[1]

    def body(dx_ref, o_ref, mod_ref, w_ref, r_ref, da_ref, do_ref, st_ref):
        @pl.when(pl.program_id(0) == 0)
        def _():
            st_ref[...] = jnp.zeros_like(st_ref)

        dx = dx_ref[...]
        d_o = (dx * mod_ref[G2:G2 + 1, :]).astype(BF16)
        do_ref[...] = d_o
        st_ref[G2:G2 + 1, :] += _colsum(dx * o_ref[...].astype(F32))
        for j in range(Fh // tn):
            cols = slice(j * tn, (j + 1) * tn)
            dz = _dot_nt(d_o, w_ref[cols, :])
            da_ref[:, cols] = (dz * (2.0 * r_ref[:, cols].astype(F32))).astype(BF16)

    return pl.pallas_call(
        body, name=name, grid=(S // tm,),
        in_specs=[_rows1(tm, D), _rows1(tm, D), _resident(mod.shape), _resident(w2.shape), _rows1(tm, Fh)],
        out_specs=[_rows1(tm, Fh), _rows1(tm, D), pl.BlockSpec((8, D), lambda i: (0, 0))],
        out_shape=[SDS((S, Fh), BF16), SDS((S, D), BF16), SDS((8, D), F32)],
        compiler_params=_params(("arbitrary",)),
    )(dx2, o, mod, w2, r)


def _mlp_bwd_b(d_a, w1, x1, dx2, mod, gn, tm, name):
    S, Fh = d_a.shape
    D = w1.shape[0]

    def body(da_ref, w_ref, x_ref, dx_ref, mod_ref, gn_ref, dx1_ref, st_ref):
        @pl.when(pl.program_id(0) == 0)
        def _():
            st_ref[...] = jnp.zeros_like(st_ref)

        dh = _dot_nt(da_ref[...], w_ref[...])
        n, rr = _rms(x_ref[...])
        gn_v = gn_ref[...]
        sc1p = 1.0 + mod_ref[SC2:SC2 + 1, :]
        t = _colsum(dh * n)
        st_ref[SH2:SH2 + 1, :] += _colsum(dh)
        st_ref[SC2:SC2 + 1, :] += t * gn_v
        st_ref[6:7, :] += t * sc1p
        dx1_ref[...] = dx_ref[...] + _normmod_bwd(dh, n, rr, gn_v * sc1p)

    return pl.pallas_call(
        body, name=name, grid=(S // tm,),
        in_specs=[_rows1(tm, Fh), _resident(w1.shape), _rows1(tm, D), _rows1(tm, D), _resident(mod.shape),
                  _resident(gn.shape)],
        out_specs=[_rows1(tm, D), pl.BlockSpec((8, D), lambda i: (0, 0))],
        out_shape=[SDS((S, D), F32), SDS((8, D), F32)],
        compiler_params=_params(("arbitrary",)),
    )(d_a, w1, x1, dx2, mod, gn)


def _mm_tn(a, g, tk, tn, name, square_a=False, col_shards=False):
    S, K1 = a.shape
    N = g.shape[1]
    w = N // NDEV
    per = tn // w if col_shards else 1

    def body(a_ref, g_ref, o_ref):
        av = a_ref[...]
        if square_a:
            av = av * av
        res = _dot_tn(av, g_ref[...]).astype(BF16)
        if col_shards:
            for s in range(per):
                o_ref[s] = res[:, s * w:(s + 1) * w]
        else:
            o_ref[...] = res

    if col_shards:
        out_spec, out_shape = pl.BlockSpec((per, tk, w), lambda i, j: (j, i, 0)), SDS((NDEV, K1, w), BF16)
    else:
        out_spec, out_shape = pl.BlockSpec((tk, tn), lambda i, j: (i, j)), SDS((K1, N), BF16)
    return pl.pallas_call(
        body, name=name, grid=(K1 // tk, N // tn),
        in_specs=[pl.BlockSpec((S, tk), lambda i, j: (0, i)), pl.BlockSpec((S, tn), lambda i, j: (0, j))],
        out_specs=out_spec, out_shape=out_shape,
        compiler_params=_params(("parallel", "parallel")),
    )(a, g)


def _pool_h_ext(x_ref, xp_ref, mod_ref, gn_ref, i, tm):
    ext = jnp.concatenate([xp_ref[...], x_ref[...]], axis=0)
    n, r = _rms(ext)
    a = gn_ref[...] * (1.0 + mod_ref[SC1:SC1 + 1, :])
    h = n * a + mod_ref[SH1:SH1 + 1, :]
    row = lax.broadcasted_iota(jnp.int32, (tm + HALO, 1), 0)
    h = jnp.where(jnp.logical_and(i == 0, row < HALO), 0.0, h)
    return h, n[HALO:], r[HALO:], a


def _trailing_sum(v, win):
    k = 1
    while k < win:
        v = v + pltpu.roll(v, k, 0)
        k *= 2
    return v


def _leading_sum(v, win):
    k = 1
    while k < win:
        v = v + pltpu.roll(v, v.shape[0] - k, 0)
        k *= 2
    return v


def _pool_fwd(x, mod, gn, pw, ps, tm, name):
    S, D = x.shape
    C = D // len(POOL_WINDOWS)
    hb = tm // HALO

    def body(x_ref, xp_ref, mod_ref, gn_ref, pw_ref, ps_ref, x1_ref):
        i = pl.program_id(0)
        h, _, _, _ = _pool_h_ext(x_ref, xp_ref, mod_ref, gn_ref, i, tm)
        t1 = (i * tm + lax.broadcasted_iota(jnp.int32, (tm, 1), 0)).astype(F32) + 1.0
        for g, win in enumerate(POOL_WINDOWS):
            cols = slice(g * C, (g + 1) * C)
            hg = h[:, cols]
            inv = 1.0 / jnp.minimum(t1, float(win))
            pooled = (_trailing_sum(hg, win)[HALO:] * inv - hg[HALO:]).astype(BF16)
            y = _dot(pooled, pw_ref[g]) * ps_ref[:, cols]
            x1_ref[:, cols] = x_ref[:, cols] + mod_ref[G1:G1 + 1, cols] * y

    return pl.pallas_call(
        body, name=name, grid=(S // tm,),
        in_specs=[_rows1(tm, D), pl.BlockSpec((HALO, D), lambda i: (jnp.maximum(i * hb - 1, 0), 0)),
                  _resident(mod.shape), _resident(gn.shape), _resident(pw.shape), _resident(ps.shape)],
        out_specs=_rows1(tm, D),
        out_shape=SDS((S, D), F32),
        compiler_params=_params(("parallel",)),
    )(x, x, mod, gn, pw, ps)


def _pool_bwd(x, dx1, mod, gn, pw, ps, tm, name):
    S, D = x.shape
    G = len(POOL_WINDOWS)
    C = D // G
    hb = tm // HALO
    nt = S // tm

    def body(x_ref, xp_ref, d1_ref, dn_ref, mod_ref, gn_ref, pw_ref, ps_ref, dx_ref, st_ref, dpw_ref):
        i = pl.program_id(0)

        @pl.when(i == 0)
        def _():
            st_ref[...] = jnp.zeros_like(st_ref)
            dpw_ref[...] = jnp.zeros_like(dpw_ref)

        h, n, rr, a = _pool_h_ext(x_ref, xp_ref, mod_ref, gn_ref, i, tm)
        g1 = mod_ref[G1:G1 + 1, :]
        ps_v = ps_ref[...]
        d1 = d1_ref[...]
        d1n = jnp.where(i == nt - 1, 0.0, dn_ref[...])
        dyr = (jnp.concatenate([d1, d1n], axis=0) * (g1 * ps_v)).astype(BF16)
        t1 = (i * tm + lax.broadcasted_iota(jnp.int32, (tm + HALO, 1), 0)).astype(F32) + 1.0
        parts = []
        for g, win in enumerate(POOL_WINDOWS):
            cols = slice(g * C, (g + 1) * C)
            hg = h[:, cols]
            inv = 1.0 / jnp.minimum(t1, float(win))
            pooled = (_trailing_sum(hg, win)[HALO:] * inv[:tm] - hg[HALO:]).astype(BF16)
            yraw = _dot(pooled, pw_ref[g])
            st_ref[G1:G1 + 1, cols] += _colsum(d1[:, cols] * (yraw * ps_v[:, cols]))
            st_ref[4:5, cols] += _colsum(d1[:, cols] * g1[:, cols] * yraw)
            dpw_ref[g] += _dot_tn(pooled, dyr[:tm, cols])
            dpool = _dot_nt(dyr[:, cols], pw_ref[g])
            parts.append(_leading_sum(dpool * inv, win)[:tm] - dpool[:tm])
        dh = jnp.concatenate(parts, axis=1)
        t = _colsum(dh * n)
        st_ref[SH1:SH1 + 1, :] += _colsum(dh)
        st_ref[SC1:SC1 + 1, :] += t * gn_ref[...]
        st_ref[3:4, :] += t * (1.0 + mod_ref[SC1:SC1 + 1, :])
        dx_ref[...] = d1 + _normmod_bwd(dh, n, rr, a)

    return pl.pallas_call(
        body, name=name, grid=(nt,),
        in_specs=[_rows1(tm, D), pl.BlockSpec((HALO, D), lambda i: (jnp.maximum(i * hb - 1, 0), 0)),
                  _rows1(tm, D), pl.BlockSpec((HALO, D), lambda i: (jnp.minimum((i + 1) * hb, S // HALO - 1), 0)),
                  _resident(mod.shape), _resident(gn.shape), _resident(pw.shape), _resident(ps.shape)],
        out_specs=[_rows1(tm, D), pl.BlockSpec((8, D), lambda i: (0, 0)), pl.BlockSpec((G, C, C), lambda i: (0, 0, 0))],
        out_shape=[SDS((S, D), F32), SDS((8, D), F32), SDS((G, C, C), F32)],
        compiler_params=_params(("arbitrary",)),
    )(x, x, dx1, dx1, mod, gn, pw, ps)


def _tril_bf16(w):
    row = lax.broadcasted_iota(jnp.int32, w.shape, 0)
    col = lax.broadcasted_iota(jnp.int32, w.shape, 1)
    return jnp.where(col <= row, w, 0.0).astype(BF16)


def _sgu_front(pre, lng_ref, lnb_ref, W):
    z = _gelu(pre)
    u, v = z[:, :W], z[:, W:]
    mu = jnp.mean(v, axis=-1, keepdims=True)
    xc = v - mu
    rstd = lax.rsqrt(jnp.mean(xc * xc, axis=-1, keepdims=True) + LN_EPS)
    vhat = xc * rstd
    return u, vhat, rstd, vhat * lng_ref[...] + lnb_ref[...]


def _sgu_mix(vn, ws_ref, bst_ref, mix_s, tm, W):
    for hd in range(W // SGU_HEAD):
        wm = _tril_bf16(ws_ref[hd])
        bcol = bst_ref[:, hd:hd + 1]
        for ci in range(tm // SGU_CHUNK):
            rs, cs = slice(ci * SGU_CHUNK, (ci + 1) * SGU_CHUNK), slice(hd * SGU_HEAD, (hd + 1) * SGU_HEAD)
            mix_s[rs, cs] = _dot(wm, vn[rs, cs].astype(BF16)) + bcol


def _sgu_fwd(x, mod, gn, w_in, lng, lnb, ws, bst, w_out, tm, name):
    S, D = x.shape
    W = w_out.shape[0]

    def body(x_ref, mod_ref, gn_ref, win_ref, lng_ref, lnb_ref, ws_ref, bst_ref, wout_ref,
             x1_ref, h_ref, pre_ref, y_ref, mix_s):
        n, _ = _rms(x_ref[...])
        a = gn_ref[...] * (1.0 + mod_ref[SC1:SC1 + 1, :])
        h = (n * a + mod_ref[SH1:SH1 + 1, :]).astype(BF16)
        h_ref[...] = h
        pre = _dot(h, win_ref[...])
        pre_ref[...] = pre.astype(BF16)
        u, _, _, vn = _sgu_front(pre, lng_ref, lnb_ref, W)
        _sgu_mix(vn, ws_ref, bst_ref, mix_s, tm, W)
        y = _dot((u * mix_s[...]).astype(BF16), wout_ref[...])
        y_ref[...] = y.astype(BF16)
        x1_ref[...] = x_ref[...] + mod_ref[G1:G1 + 1, :] * y

    return pl.pallas_call(
        body, name=name, grid=(S // tm,),
        in_specs=[_rows1(tm, D), _resident(mod.shape), _resident(gn.shape), _resident(w_in.shape),
                  _resident(lng.shape), _resident(lnb.shape), _resident(ws.shape), _resident(bst.shape),
                  _resident(w_out.shape)],
        out_specs=[_rows1(tm, D), _rows1(tm, D), _rows1(tm, 2 * W), _rows1(tm, D)],
        out_shape=[SDS((S, D), F32), SDS((S, D), BF16), SDS((S, 2 * W), BF16), SDS((S, D), BF16)],
        scratch_shapes=[pltpu.VMEM((tm, W), F32)],
        compiler_params=_params(("parallel",)),
    )(x, mod, gn, w_in, lng, lnb, ws, bst, w_out)


def _sgu_bwd(x, dx1, pre, y, mod, gn, w_in, lng, lnb, ws, bst, w_out, tm, name):
    S, D = x.shape
    W = w_out.shape[0]
    H = W // SGU_HEAD
    nt = S // tm

    def body(x_ref, d1_ref, pre_ref, y_ref, mod_ref, gn_ref, win_ref, lng_ref, lnb_ref, ws_ref, bst_ref, wout_ref,
             dx_ref, dy_ref, gt_ref, dpre_ref, st_ref, dws_ref, dbs_ref, mix_s, dvn_s):
        i = pl.program_id(0)

        @pl.when(i == 0)
        def _():
            st_ref[...] = jnp.zeros_like(st_ref)
            dws_ref[...] = jnp.zeros_like(dws_ref)
            dbs_ref[...] = jnp.zeros_like(dbs_ref)

        d1 = d1_ref[...]
        pre = pre_ref[...].astype(F32)
        u, vhat, rstd, vn = _sgu_front(pre, lng_ref, lnb_ref, W)
        _sgu_mix(vn, ws_ref, bst_ref, mix_s, tm, W)
        mixed = mix_s[...]
        gt_ref[...] = (u * mixed).astype(BF16)
        dyb = (d1 * mod_ref[G1:G1 + 1, :]).astype(BF16)
        dy_ref[...] = dyb
        st_ref[G1:G1 + 1, :] += _colsum(d1 * y_ref[...].astype(F32))
        dgt = _dot_nt(dyb, wout_ref[...])
        du = dgt * mixed
        dmix = dgt * u
        for hd in range(H):
            wm = _tril_bf16(ws_ref[hd])
            for ci in range(tm // SGU_CHUNK):
                rs, cs = slice(ci * SGU_CHUNK, (ci + 1) * SGU_CHUNK), slice(hd * SGU_HEAD, (hd + 1) * SGU_HEAD)
                dm = dmix[rs, cs]
                dmb = dm.astype(BF16)
                dbs_ref[hd] += jnp.broadcast_to(jnp.sum(dm, axis=1, keepdims=True), (SGU_CHUNK, LANES))
                dws_ref[hd] += _dot_nt(dmb, vn[rs, cs].astype(BF16))
                dvn_s[rs, cs] = _dot_tn(wm, dmb)
        dvn = dvn_s[...]
        st_ref[4:5, :] += _colsum(dvn * vhat)
        st_ref[5:6, :] += _colsum(dvn)
        dvh = dvn * lng_ref[...]
        dv = rstd * (dvh - jnp.mean(dvh, axis=-1, keepdims=True) - vhat * jnp.mean(dvh * vhat, axis=-1, keepdims=True))
        dpre_u = (du * _gelu_grad(pre[:, :W])).astype(BF16)
        dpre_v = (dv * _gelu_grad(pre[:, W:])).astype(BF16)
        dpre_ref[:, :W] = dpre_u
        dpre_ref[:, W:] = dpre_v
        dh = _dot_nt(dpre_u, win_ref[:, :W]) + _dot_nt(dpre_v, win_ref[:, W:])
        n, rr = _rms(x_ref[...])
        gn_v = gn_ref[...]
        sc1p = 1.0 + mod_ref[SC1:SC1 + 1, :]
        t = _colsum(dh * n)
        st_ref[SH1:SH1 + 1, :] += _colsum(dh)
        st_ref[SC1:SC1 + 1, :] += t * gn_v
        st_ref[3:4, :] += t * sc1p
        dx_ref[...] = d1 + _normmod_bwd(dh, n, rr, gn_v * sc1p)

        @pl.when(i == nt - 1)
        def _():
            for hd in range(H):
                row = lax.broadcasted_iota(jnp.int32, (SGU_CHUNK, SGU_CHUNK), 0)
                col = lax.broadcasted_iota(jnp.int32, (SGU_CHUNK, SGU_CHUNK), 1)
                dws_ref[hd] = jnp.where(col <= row, dws_ref[hd], 0.0)

    return pl.pallas_call(
        body, name=name, grid=(nt,),
        in_specs=[_rows1(tm, D), _rows1(tm, D), _rows1(tm, 2 * W), _rows1(tm, D), _resident(mod.shape),
                  _resident(gn.shape), _resident(w_in.shape), _resident(lng.shape), _resident(lnb.shape),
                  _resident(ws.shape), _resident(bst.shape), _resident(w_out.shape)],
        out_specs=[_rows1(tm, D), _rows1(tm, D), _rows1(tm, W), _rows1(tm, 2 * W),
                   pl.BlockSpec((8, D), lambda i: (0, 0)), pl.BlockSpec((H, SGU_CHUNK, SGU_CHUNK), lambda i: (0, 0, 0)),
                   pl.BlockSpec((H, SGU_CHUNK, LANES), lambda i: (0, 0, 0))],
        out_shape=[SDS((S, D), F32), SDS((S, D), BF16), SDS((S, W), BF16), SDS((S, 2 * W), BF16),
                   SDS((8, D), F32), SDS((H, SGU_CHUNK, SGU_CHUNK), F32), SDS((H, SGU_CHUNK, LANES), F32)],
        scratch_shapes=[pltpu.VMEM((tm, W), F32), pltpu.VMEM((tm, W), F32)],
        compiler_params=_params(("arbitrary",)),
    )(x, dx1, pre, y, mod, gn, w_in, lng, lnb, ws, bst, w_out)


def _mla_lat(x, mod, gn, wd, qg, kvg, cos_t, sin_t, tm, name):
    S, D = x.shape
    LW = wd.shape[1]
    QL, KL = MLA_Q_LORA, MLA_KV_LORA

    def body(x_ref, mod_ref, gn_ref, wd_ref, qg_ref, kvg_ref, c_ref, s_ref, h_ref, lat_ref, cq_ref, ckv_ref, kr_ref):
        n, _ = _rms(x_ref[...])
        a = gn_ref[...] * (1.0 + mod_ref[SC1:SC1 + 1, :])
        h = (n * a + mod_ref[SH1:SH1 + 1, :]).astype(BF16)
        h_ref[...] = h
        lat = _dot(h, wd_ref[...])
        lat_ref[...] = lat
        nq, _ = _rms(lat[:, :QL])
        cq_ref[...] = (nq * qg_ref[...]).astype(BF16)
        nkv, _ = _rms(lat[:, QL:QL + KL])
        ckv_ref[...] = (nkv * kvg_ref[...]).astype(BF16)
        kr = lat[:, QL + KL:]
        kr_ref[...] = (kr * c_ref[...] + _swap_halves(kr) * s_ref[...]).astype(BF16)

    return pl.pallas_call(
        body, name=name, grid=(S // tm,),
        in_specs=[_rows1(tm, D), _resident(mod.shape), _resident(gn.shape), _resident(wd.shape), _resident(qg.shape),
                  _resident(kvg.shape), _rows1(tm, LANES), _rows1(tm, LANES)],
        out_specs=[_rows1(tm, D), _rows1(tm, LW), _rows1(tm, QL), _rows1(tm, KL), _rows1(tm, LANES)],
        out_shape=[SDS((S, D), BF16), SDS((S, LW), F32), SDS((S, QL), BF16), SDS((S, KL), BF16), SDS((S, LANES), BF16)],
        compiler_params=_params(("parallel",)),
    )(x, mod, gn, wd, qg, kvg, cos_t, sin_t)


def _mla_qkv(cq, ckv, krp, wq, wukv, cos_t, sin_t, tm, name):
    S = cq.shape[0]
    H = wq.shape[1] // MLA_HEAD_PAD
    HP = MLA_HEAD_PAD

    def body(cq_ref, ckv_ref, kr_ref, wq_ref, wkv_ref, c_ref, s_ref, q_ref, k_ref, v_ref):
        q = _dot(cq_ref[...], wq_ref[...])
        kv = _dot(ckv_ref[...], wkv_ref[...])
        cv, sv, krv = c_ref[...], s_ref[...], kr_ref[...]
        for h in range(H):
            qr = q[:, h * HP + MLA_NOPE:(h + 1) * HP]
            q_ref[:, h * HP:h * HP + MLA_NOPE] = (q[:, h * HP:h * HP + MLA_NOPE] * Q_SCALE).astype(BF16)
            q_ref[:, h * HP + MLA_NOPE:(h + 1) * HP] = ((qr * cv + _swap_halves(qr) * sv) * Q_SCALE).astype(BF16)
            k_ref[:, h * HP:h * HP + MLA_NOPE] = kv[:, h * HP:h * HP + MLA_NOPE].astype(BF16)
            k_ref[:, h * HP + MLA_NOPE:(h + 1) * HP] = krv
            v_ref[:, h * MLA_V:(h + 1) * MLA_V] = kv[:, h * HP + MLA_NOPE:(h + 1) * HP].astype(BF16)

    return pl.pallas_call(
        body, name=name, grid=(S // tm,),
        in_specs=[_rows1(tm, MLA_Q_LORA), _rows1(tm, MLA_KV_LORA), _rows1(tm, LANES), _resident(wq.shape),
                  _resident(wukv.shape), _rows1(tm, LANES), _rows1(tm, LANES)],
        out_specs=[_rows1(tm, H * HP), _rows1(tm, H * HP), _rows1(tm, H * MLA_V)],
        out_shape=[SDS((S, H * HP), BF16), SDS((S, H * HP), BF16), SDS((S, H * MLA_V), BF16)],
        compiler_params=_params(("parallel",)),
    )(cq, ckv, krp, wq, wukv, cos_t, sin_t)


def _causal_mask(nr, nc):
    row = lax.broadcasted_iota(jnp.int32, (nr, nc), 0)
    col = lax.broadcasted_iota(jnp.int32, (nr, nc), 1)
    return col <= row


def _attn_fwd(q, k, v, tq, name):
    S = q.shape[0]
    HP = MLA_HEAD_PAD
    H = q.shape[1] // HP
    nq = S // tq

    def body(q_ref, k_ref, v_ref, o_ref, lse_ref, v1):
        v1[:, :MLA_V] = v_ref[...]
        v1[:, MLA_V:] = jnp.ones((S, LANES), BF16)
        def update(qv, j, carry, masked):
            m, acc = carry
            krows = slice(j * tq, (j + 1) * tq)
            s = _dot_nt(qv, k_ref[krows, :])
            if masked:
                s = jnp.where(_causal_mask(tq, tq), s, NEG)
            m_new = jnp.maximum(m, jnp.max(s, axis=1, keepdims=True))
            p = jnp.exp2(s - m_new)
            return m_new, jnp.exp2(m - m_new) * acc + _dot(p.astype(BF16), v1[krows, :])

        for i in range(nq):
            rows = slice(i * tq, (i + 1) * tq)
            qv = q_ref[rows, :]
            carry = (jnp.full((tq, 1), NEG, F32), jnp.zeros((tq, MLA_V + LANES), F32))
            for j in range(i):
                carry = update(qv, j, carry, False)
            m, acc = update(qv, i, carry, True)
            l = acc[:, MLA_V:MLA_V + 1]
            o_ref[rows, :] = (acc[:, :MLA_V] / l).astype(BF16)
            lse_ref[0, rows, :] = jnp.broadcast_to(m + jnp.log2(l), (tq, LANES))

    return pl.pallas_call(
        body, name=name, grid=(H,),
        in_specs=[pl.BlockSpec((S, HP), lambda h: (0, h)), pl.BlockSpec((S, HP), lambda h: (0, h)),
                  pl.BlockSpec((S, MLA_V), lambda h: (0, h))],
        out_specs=[pl.BlockSpec((S, MLA_V), lambda h: (0, h)), pl.BlockSpec((1, S, LANES), lambda h: (h, 0, 0))],
        out_shape=[SDS((S, H * MLA_V), BF16), SDS((H, S, LANES), F32)],
        scratch_shapes=[pltpu.VMEM((S, MLA_V + LANES), BF16)],
        compiler_params=_params(("parallel",)),
    )(q, k, v)


def _attn_bwd(q, k, v, o, do, lse, cos_t, sin_t, tq, name):
    S = q.shape[0]
    HP = MLA_HEAD_PAD
    H = q.shape[1] // HP
    nq = S // tq

    def body(q_ref, k_ref, v_ref, o_ref, do_ref, lse_ref, c_ref, s_ref, dq_ref, dkv_ref, dkr_ref, dq_acc, dl_s):
        @pl.when(pl.program_id(0) == 0)
        def _():
            dkr_ref[...] = jnp.zeros_like(dkr_ref)

        dq_acc[...] = jnp.zeros_like(dq_acc)

        def delta_tile(i, _):
            rows = pl.ds(pl.multiple_of(i * tq, tq), tq)
            d = jnp.sum(do_ref[rows, :].astype(F32) * o_ref[rows, :].astype(F32), axis=1, keepdims=True)
            dl_s[rows, :] = jnp.broadcast_to(d, (tq, LANES))
            return 0

        lax.fori_loop(0, nq, delta_tile, 0)

        def update(i, kv_k, kv_v, carry, masked):
            dk, dv = carry
            rows = slice(i * tq, (i + 1) * tq)
            qv = q_ref[rows, :]
            dov = do_ref[rows, :]
            s = _dot_nt(qv, kv_k)
            if masked:
                s = jnp.where(_causal_mask(tq, tq), s, NEG)
            p = jnp.exp2(s - lse_ref[0, rows, 0:1])
            dv = dv + _dot_tn(p.astype(BF16), dov)
            dp = _dot_nt(dov, kv_v)
            ds = (p * (dp - dl_s[rows, 0:1])).astype(BF16)
            dk = dk + _dot_tn(ds, qv)
            dq_acc[rows, :] += _dot(ds, kv_k)
            return dk, dv

        for j in range(nq):
            krows = slice(j * tq, (j + 1) * tq)
            kv_k = k_ref[krows, :]
            kv_v = v_ref[krows, :]
            carry = update(j, kv_k, kv_v, (jnp.zeros((tq, HP), F32), jnp.zeros((tq, MLA_V), F32)), True)
            for i in range(j + 1, nq):
                carry = update(i, kv_k, kv_v, carry, False)
            dk, dv = carry
            dk = dk * LN2
            dkv_ref[krows, :MLA_NOPE] = dk[:, :MLA_NOPE].astype(BF16)
            dkv_ref[krows, MLA_NOPE:] = dv.astype(BF16)
            dkr_ref[krows, :] += dk[:, MLA_NOPE:]

        def out_tile(i, _):
            rows = pl.ds(pl.multiple_of(i * tq, tq), tq)
            dq = dq_acc[rows, :] * SM_SCALE
            dqr = dq[:, MLA_NOPE:]
            dq_ref[rows, :MLA_NOPE] = dq[:, :MLA_NOPE].astype(BF16)
            dq_ref[rows, MLA_NOPE:] = (dqr * c_ref[rows, :] + _swap_halves(dqr * s_ref[rows, :])).astype(BF16)
            return 0

        lax.fori_loop(0, nq, out_tile, 0)

    return pl.pallas_call(
        body, name=name, grid=(H,),
        in_specs=[pl.BlockSpec((S, HP), lambda h: (0, h)), pl.BlockSpec((S, HP), lambda h: (0, h)),
                  pl.BlockSpec((S, MLA_V), lambda h: (0, h)), pl.BlockSpec((S, MLA_V), lambda h: (0, h)),
                  pl.BlockSpec((S, MLA_V), lambda h: (0, h)), pl.BlockSpec((1, S, LANES), lambda h: (h, 0, 0)),
                  _resident(cos_t.shape), _resident(sin_t.shape)],
        out_specs=[pl.BlockSpec((S, HP), lambda h: (0, h)), pl.BlockSpec((S, HP), lambda h: (0, h)),
                   pl.BlockSpec((S, LANES), lambda h: (0, 0))],
        out_shape=[SDS((S, H * HP), BF16), SDS((S, H * HP), BF16), SDS((S, LANES), F32)],
        scratch_shapes=[pltpu.VMEM((S, HP), F32), pltpu.VMEM((S, LANES), F32)],
        compiler_params=_params(("arbitrary",)),
    )(q, k, v, o, do, lse, cos_t, sin_t)


def _mla_out(o, w_o, x, mod, tm, name):
    S, KO = o.shape
    D = w_o.shape[1]

    def body(o_ref, w_ref, x_ref, mod_ref, x1_ref, y_ref):
        y = _dot(o_ref[...], w_ref[...])
        y_ref[...] = y.astype(BF16)
        x1_ref[...] = x_ref[...] + mod_ref[G1:G1 + 1, :] * y

    return pl.pallas_call(
        body, name=name, grid=(S // tm,),
        in_specs=[_rows1(tm, KO), _resident(w_o.shape), _rows1(tm, D), _resident(mod.shape)],
        out_specs=[_rows1(tm, D), _rows1(tm, D)],
        out_shape=[SDS((S, D), F32), SDS((S, D), BF16)],
        compiler_params=_params(("parallel",)),
    )(o, w_o, x, mod)


def _mla_bwd_o(dx1, y, mod, w_o, tm, name):
    S, D = dx1.shape
    KO = w_o.shape[0]

    def body(d1_ref, y_ref, mod_ref, w_ref, dy_ref, do_ref, st_ref):
        @pl.when(pl.program_id(0) == 0)
        def _():
            st_ref[...] = jnp.zeros_like(st_ref)

        d1 = d1_ref[...]
        dyb = (d1 * mod_ref[G1:G1 + 1, :]).astype(BF16)
        dy_ref[...] = dyb
        st_ref[G1:G1 + 1, :] += _colsum(d1 * y_ref[...].astype(F32))
        do_ref[...] = _dot_nt(dyb, w_ref[...]).astype(BF16)

    return pl.pallas_call(
        body, name=name, grid=(S // tm,),
        in_specs=[_rows1(tm, D), _rows1(tm, D), _resident(mod.shape), _resident(w_o.shape)],
        out_specs=[_rows1(tm, D), _rows1(tm, KO), pl.BlockSpec((8, D), lambda i: (0, 0))],
        out_shape=[SDS((S, D), BF16), SDS((S, KO), BF16), SDS((8, D), F32)],
        compiler_params=_params(("arbitrary",)),
    )(dx1, y, mod, w_o)


def _mla_bwd_lat(dq, dkv, dkr, lat, x, dx1, mod, gn, qg, kvg, wq, wukv, wd, cos_t, sin_t, tm, name):
    S, D = x.shape
    LW = wd.shape[1]
    QL, KL = MLA_Q_LORA, MLA_KV_LORA

    def body(dq_ref, dkv_ref, dkr_ref, lat_ref, x_ref, d1_ref, mod_ref, gn_ref, qg_ref, kvg_ref, wq_ref, wkv_ref, wd_ref,
             c_ref, s_ref, dx_ref, dlat_ref, st_ref, dqg_ref, dkvg_ref):
        @pl.when(pl.program_id(0) == 0)
        def _():
            st_ref[...] = jnp.zeros_like(st_ref)
            dqg_ref[...] = jnp.zeros_like(dqg_ref)
            dkvg_ref[...] = jnp.zeros_like(dkvg_ref)

        lat = lat_ref[...]
        dcq = _dot_nt(dq_ref[...], wq_ref[...])
        nq, rq = _rms(lat[:, :QL])
        dqg_ref[0:1, :] += _colsum(dcq * nq)
        dlat_q = _normmod_bwd(dcq, nq, rq, qg_ref[...]).astype(BF16)
        dckv = _dot_nt(dkv_ref[...], wkv_ref[...])
        nkv, rkv = _rms(lat[:, QL:QL + KL])
        dkvg_ref[0:1, :] += _colsum(dckv * nkv)
        dlat_kv = _normmod_bwd(dckv, nkv, rkv, kvg_ref[...]).astype(BF16)
        dkr = dkr_ref[...]
        dlat_kr = (dkr * c_ref[...] + _swap_halves(dkr * s_ref[...])).astype(BF16)
        dlat_ref[:, :QL] = dlat_q
        dlat_ref[:, QL:QL + KL] = dlat_kv
        dlat_ref[:, QL + KL:] = dlat_kr
        dh = (_dot_nt(dlat_q, wd_ref[:, :QL]) + _dot_nt(dlat_kv, wd_ref[:, QL:QL + KL])
              + _dot_nt(dlat_kr, wd_ref[:, QL + KL:]))
        n, rr = _rms(x_ref[...])
        gn_v = gn_ref[...]
        sc1p = 1.0 + mod_ref[SC1:SC1 + 1, :]
        t = _colsum(dh * n)
        st_ref[SH1:SH1 + 1, :] += _colsum(dh)
        st_ref[SC1:SC1 + 1, :] += t * gn_v
        st_ref[3:4, :] += t * sc1p
        dx_ref[...] = d1_ref[...] + _normmod_bwd(dh, n, rr, gn_v * sc1p)

    HW = wq.shape[1]
    return pl.pallas_call(
        body, name=name, grid=(S // tm,),
        in_specs=[_rows1(tm, HW), _rows1(tm, HW), _rows1(tm, LANES), _rows1(tm, LW), _rows1(tm, D), _rows1(tm, D),
                  _resident(mod.shape), _resident(gn.shape), _resident(qg.shape), _resident(kvg.shape),
                  _resident(wq.shape), _resident(wukv.shape), _resident(wd.shape), _rows1(tm, LANES), _rows1(tm, LANES)],
        out_specs=[_rows1(tm, D), _rows1(tm, LW), pl.BlockSpec((8, D), lambda i: (0, 0)),
                   pl.BlockSpec((8, QL), lambda i: (0, 0)), pl.BlockSpec((8, KL), lambda i: (0, 0))],
        out_shape=[SDS((S, D), F32), SDS((S, LW), BF16), SDS((8, D), F32), SDS((8, QL), F32), SDS((8, KL), F32)],
        compiler_params=_params(("arbitrary",)),
    )(dq, dkv, dkr, lat, x, dx1, mod, gn, qg, kvg, wq, wukv, wd, cos_t, sin_t)


def _loss_head(x, tgt, fg, tm, name):
    S, D = x.shape
    nt = S // tm

    def body(x_ref, t_ref, g_ref, dx_ref, acc_ref):
        i = pl.program_id(0)

        @pl.when(i == 0)
        def _():
            acc_ref[...] = jnp.zeros_like(acc_ref)

        n, rr = _rms(x_ref[...])
        g = g_ref[...]
        err = n * g - t_ref[...]
        acc_ref[1:2, :] += _colsum(err * err) * (0.5 / D)
        dy = err * (1.0 / D)
        acc_ref[0:1, :] += _colsum(dy * n)
        dx_ref[...] = _normmod_bwd(dy, n, rr, g)

        @pl.when(i == nt - 1)
        def _():
            acc_ref[2:3, :] = jnp.broadcast_to(jnp.sum(acc_ref[1:2, :], axis=1, keepdims=True), (1, D))

    return pl.pallas_call(
        body, name=name, grid=(nt,),
        in_specs=[_rows1(tm, D), _rows1(tm, D), _resident(fg.shape)],
        out_specs=[_rows1(tm, D), pl.BlockSpec((8, D), lambda i: (0, 0))],
        out_shape=[SDS((S, D), F32), SDS((8, D), F32)],
        compiler_params=_params(("arbitrary",)),
    )(x, tgt, fg)


def _rope_tables(positions, S):
    inv_freq = ROPE_THETA ** (-jnp.arange(0, MLA_ROPE, 2, dtype=F32) / MLA_ROPE)
    ang = positions.reshape(S, 1).astype(F32) * inv_freq
    cos, sin = jnp.cos(ang), jnp.sin(ang)
    z = jnp.zeros((S, LANES - MLA_ROPE), F32)
    return jnp.concatenate([cos, cos, z], axis=1), jnp.concatenate([-sin, sin, z], axis=1)


def _pad_heads(w, per_head):
    K = w.shape[0]
    H = w.shape[1] // per_head
    w3 = w.reshape(K, H, per_head)
    return jnp.pad(w3, ((0, 0), (0, 0), (0, MLA_HEAD_PAD - per_head))).reshape(K, H * MLA_HEAD_PAD)


def _unpad_heads(w, per_head):
    K = w.shape[0]
    H = w.shape[1] // MLA_HEAD_PAD
    return w.reshape(K, H, MLA_HEAD_PAD)[:, :, :per_head].reshape(K, H * per_head)


def _tiles(S):
    return min(512, S), min(256, S), min(512, S)


def _layer_forward(i, x, mod, gmix, gmlp, w, small, tables):
    S, D = x.shape
    tm, tms, tq = _tiles(S)
    cos_t, sin_t = tables
    kind = i % 3
    sv = {"x": x}
    if kind == 0:
        x1 = _pool_fwd(x, mod, gmix, w["pool_w"], small["pool_scale"][i // 3], tm, f"pool_fwd_{i}")
    elif kind == 1:
        x1, sv["h"], sv["pre"], sv["y"] = _sgu_fwd(
            x, mod, gmix, w["sgu_w_in"], small["sgu_ln_g"], small["sgu_ln_b"], small["sgu_w_s"],
            small["sgu_b_s_t"], w["sgu_w_out"], tms, f"sgu_fwd_{i}")
    else:
        sv["h"], sv["lat"], sv["cq"], sv["ckv"], krp = _mla_lat(
            x, mod, gmix, w["mla_wd"], small["mla_q_norm_g"], small["mla_kv_norm_g"], cos_t, sin_t, tm,
            f"mla_lat_{i}")
        sv["q"], sv["k"], sv["v"] = _mla_qkv(sv["cq"], sv["ckv"], krp, w["mla_wq"], w["mla_w_ukv"], cos_t, sin_t,
                                             tm, f"mla_qkv_{i}")
        sv["o"], sv["lse"] = _attn_fwd(sv["q"], sv["k"], sv["v"], tq, f"attn_fwd_{i}")
        x1, sv["y"] = _mla_out(sv["o"], w["mla_w_o"], x, mod, tm, f"mla_out_{i}")
    sv["x1"] = x1
    if callable(w["mlp_w1"]):
        w.update(w["mlp_w1"](x1))
    Fh = w["mlp_w1"].shape[1]
    sv["h2"], sv["r"] = _mlp_up(x1, mod, gmlp, w["mlp_w1"], tm, min(2048, Fh), f"mlp_up_{i}")
    if callable(w["mlp_w2"]):
        w.update(w["mlp_w2"](sv["r"]))
    x, sv["o2"] = _mlp_down(sv["r"], w["mlp_w2"], x1, mod, tm, f"mlp_down_{i}")
    return x, sv


def _layer_backward(i, dx, sv, mod, gmix, gmlp, w, small, tables, on_mlp_grads=None):
    S, D = dx.shape
    tm, tms, tq = _tiles(S)
    cos_t, sin_t = tables
    kind = i % 3
    sgrads = {}
    Fh = w["mlp_w1"].shape[1]
    g = {}
    d_a, d_o, st_a = _mlp_bwd_a(dx, sv["o2"], mod, w["mlp_w2"], sv["r"], tm, min(2048, Fh), f"mlp_bwd_a_{i}")
    g["mlp_w2"] = _mm_tn(sv["r"], d_o, min(512, Fh), D, f"mlp_dw2_{i}", square_a=True)
    g["mlp_w1"] = _mm_tn(sv["h2"], d_a, D, min(512, Fh), f"mlp_dw1_{i}", col_shards=True)
    if on_mlp_grads is not None:
        mod = mod + on_mlp_grads({n: g.pop(n) for n in ("mlp_w1", "mlp_w2")})
    dx1, st_b = _mlp_bwd_b(d_a, w["mlp_w1"], sv["x1"], dx, mod, gmlp, tm, f"mlp_bwd_b_{i}")
    if kind == 0:
        dx, st_m, dpw = _pool_bwd(sv["x"], dx1, mod, gmix, w["pool_w"], small["pool_scale"][i // 3], tm,
                                  f"pool_bwd_{i}")
        g["pool_w"] = dpw
        sgrads[f"pool_scale_{i // 3}"] = st_m[4:5]
    elif kind == 1:
        dx, dyb, gated, dpre, st_m, dws, dbs = _sgu_bwd(
            sv["x"], dx1, sv["pre"], sv["y"], mod, gmix, w["sgu_w_in"], small["sgu_ln_g"], small["sgu_ln_b"],
            small["sgu_w_s"], small["sgu_b_s_t"], w["sgu_w_out"], tms, f"sgu_bwd_{i}")
        W = gated.shape[1]
        g["sgu_w_out"] = _mm_tn(gated, dyb, min(512, W), D, f"sgu_dwout_{i}")
        g["sgu_w_in"] = _mm_tn(sv["h"], dpre, D, min(512, 2 * W), f"sgu_dwin_{i}", col_shards=True)
        sgrads["sgu_ln_g"], sgrads["sgu_ln_b"] = st_m[4:5], st_m[5:6]
        sgrads["sgu_w_s"], sgrads["sgu_b_s"] = dws, dbs[:, :, 0]
    else:
        dyb, do, st_o = _mla_bwd_o(dx1, sv["y"], mod, w["mla_w_o"], tm, f"mla_bwd_o_{i}")
        KO = do.shape[1]
        g["mla_w_o"] = _mm_tn(sv["o"], dyb, min(512, KO), D, f"mla_dwo_{i}")
        dq, dkv, dkr = _attn_bwd(sv["q"], sv["k"], sv["v"], sv["o"], do, sv["lse"], cos_t, sin_t, tq, f"attn_bwd_{i}")
        dx, dlat, st_m, dqg, dkvg = _mla_bwd_lat(
            dq, dkv, dkr, sv["lat"], sv["x"], dx1, mod, gmix, small["mla_q_norm_g"], small["mla_kv_norm_g"],
            w["mla_wq"], w["mla_w_ukv"], w["mla_wd"], cos_t, sin_t, tm, f"mla_bwd_lat_{i}")
        HW = dq.shape[1]
        g["mla_wq"] = _mm_tn(sv["cq"], dq, MLA_Q_LORA, min(1024, HW), f"mla_dwq_{i}")
        g["mla_w_ukv"] = _mm_tn(sv["ckv"], dkv, MLA_KV_LORA, min(1024, HW), f"mla_dwukv_{i}", col_shards=True)
        g["mla_wd"] = _mm_tn(sv["h"], dlat, D, dlat.shape[1], f"mla_dwd_{i}")
        st_m = jnp.concatenate([st_m[0:2], st_o[2:3], st_m[3:]], axis=0)
        sgrads["mla_q_norm_g"], sgrads["mla_kv_norm_g"] = dqg[0:1], dkvg[0:1]
    stats = jnp.concatenate([st_m[0:3], st_b[3:5], st_a[5:6], st_m[3:4], st_b[6:7]], axis=0)
    return dx, stats, g, sgrads


def _local_step(x, tgt, positions, mod, gmix, gmlp, fg, wts, small):
    S = x.shape[0]
    L = mod.shape[0]
    tables = _rope_tables(positions, S)
    saved = []
    for i in range(L):
        x, sv = _layer_forward(i, x, mod[i], gmix[i], gmlp[i], wts[i], small, tables)
        saved.append(sv)
    dx, loss_acc = _loss_head(x, tgt, fg, _tiles(S)[0], "loss_head")
    stats, grads, sgrads = [None] * L, [None] * L, {}
    for i in reversed(range(L)):
        dx, stats[i], grads[i], sg = _layer_backward(i, dx, saved[i], mod[i], gmix[i], gmlp[i], wts[i], small, tables)
        sgrads.update(sg)
    return loss_acc, dx, stats, grads, sgrads


HBM_SPEC = pl.BlockSpec(memory_space=pltpu.HBM)
VMEM_SPEC = pl.BlockSpec(memory_space=pltpu.VMEM)


def _my_place():
    return lax.axis_index("x"), lax.axis_index("y"), lax.axis_index("c")


def _flip(v, bit):
    return 1 - v if bit else v


def _small_all_gather(v, name):
    R, C = v.shape

    def body(x_ref, out_ref, send_sems, recv_sems):
        x, y, c = _my_place()
        me = 4 * x + 2 * y + c
        out_ref[me] = x_ref[...]
        sends = []
        for k in range(1, NDEV):
            peer = (_flip(x, k & 4), _flip(y, k & 2), _flip(c, k & 1))
            cp = pltpu.make_async_remote_copy(src_ref=x_ref, dst_ref=out_ref.at[me], send_sem=send_sems.at[k - 1],
                                              recv_sem=recv_sems.at[k - 1], device_id=peer, device_id_type=MESH)
            cp.start()
            sends.append(cp)
        for k in range(1, NDEV):
            src = 4 * _flip(x, k & 4) + 2 * _flip(y, k & 2) + _flip(c, k & 1)
            pltpu.make_async_remote_copy(src_ref=x_ref, dst_ref=out_ref.at[src], send_sem=send_sems.at[k - 1],
                                         recv_sem=recv_sems.at[k - 1], device_id=(x, y, c), device_id_type=MESH).wait_recv()
        for cp in sends:
            cp.wait_send()

    return pl.pallas_call(
        body, name=name, out_shape=SDS((NDEV, R, C), v.dtype), in_specs=[VMEM_SPEC], out_specs=VMEM_SPEC,
        scratch_shapes=[pltpu.SemaphoreType.DMA((NDEV - 1,)), pltpu.SemaphoreType.DMA((NDEV - 1,))],
        compiler_params=pltpu.CompilerParams(vmem_limit_bytes=V7X_VMEM_LIMIT),
    )(v)


def _slab(ref, axis, width, dev):
    idx = [slice(None)] * len(ref.shape)
    idx[axis] = pl.ds(pl.multiple_of(dev * width, width), width)
    return ref.at[tuple(idx)]


def _all_gather_group(shards, axes, after, name):
    nt = len(shards)
    out_shapes = [SDS(tuple(s * NDEV if a == ax else s for a, s in enumerate(sh.shape)), sh.dtype)
                  for sh, ax in zip(shards, axes)]

    def body(*refs):
        ins, outs = refs[:nt], refs[nt + 1:2 * nt + 1]
        send_sems, recv_sems, local_sems = refs[2 * nt + 1:]
        x, y, c = _my_place()
        me = 4 * x + 2 * y + c
        sibling = (x, y, 1 - c)
        chips = [(1 - x, y), (x, 1 - y), (1 - x, 1 - y)]

        def block(t, dev):
            return _slab(outs[t], axes[t], ins[t].shape[axes[t]], dev)

        def copy(t, k, dev, to, src=None):
            return pltpu.make_async_remote_copy(
                src_ref=block(t, dev) if src is None else src, dst_ref=block(t, dev), send_sem=send_sems.at[t, k],
                recv_sem=recv_sems.at[t, k], device_id=to, device_id_type=MESH)

        mine = [pltpu.make_async_copy(ins[t], block(t, me), local_sems.at[t]) for t in range(nt)]
        for cp in mine:
            cp.start()
        first = []
        for t in range(nt):
            first.append(copy(t, 0, me, sibling, src=ins[t]))
            first += [copy(t, 1 + j, me, (cx, cy, c), src=ins[t]) for j, (cx, cy) in enumerate(chips)]
        for cp in first:
            cp.start()
        passed = []
        for j, (cx, cy) in enumerate(chips):
            for t in range(nt):
                copy(t, 1 + j, 4 * cx + 2 * cy + c, (x, y, c)).wait_recv()
                cp = copy(t, 4 + j, 4 * cx + 2 * cy + c, sibling)
                cp.start()
                passed.append(cp)
        for t in range(nt):
            copy(t, 0, 4 * x + 2 * y + (1 - c), (x, y, c)).wait_recv()
        for j, (cx, cy) in enumerate(chips):
            for t in range(nt):
                copy(t, 4 + j, 4 * cx + 2 * cy + (1 - c), (x, y, c)).wait_recv()
        for cp in first + passed:
            cp.wait_send()
        for cp in mine:
            cp.wait()

    return pl.pallas_call(
        body, name=name, out_shape=out_shapes, in_specs=[HBM_SPEC] * nt + [ANY_SPEC], out_specs=[HBM_SPEC] * nt,
        scratch_shapes=[pltpu.SemaphoreType.DMA((nt, NDEV - 1)), pltpu.SemaphoreType.DMA((nt, NDEV - 1)),
                        pltpu.SemaphoreType.DMA((nt,))],
    )(*shards, after)


SEM_SPEC = pl.BlockSpec(memory_space=pltpu.SEMAPHORE)
ANY_SPEC = pl.BlockSpec(memory_space=pl.ANY)
SPLIT_PARAMS = pltpu.CompilerParams(has_side_effects=pltpu.SideEffectType.DATAFLOW_SIDE_EFFECTING)
TOKEN = SDS((8, LANES), F32)


def _in_hbm(arrays):
    return [pltpu.with_memory_space_constraint(v, pltpu.HBM) for v in arrays]


def _split_start(body, srcs, lands, after, n_sem, name):
    ns, nl = len(srcs), len(lands)
    bufs = list(srcs) + list(lands)
    res = pl.pallas_call(
        body, name=name,
        out_shape=(pltpu.SemaphoreType.DMA((ns * n_sem,)), pltpu.SemaphoreType.DMA((ns * n_sem,)),
                   *[pltpu.HBM(v.shape, v.dtype) for v in bufs], TOKEN),
        in_specs=[HBM_SPEC] * (ns + nl) + [ANY_SPEC],
        out_specs=(SEM_SPEC, SEM_SPEC, *[HBM_SPEC] * (ns + nl), VMEM_SPEC),
        input_output_aliases={t: 2 + t for t in range(ns + nl)}, compiler_params=SPLIT_PARAMS,
    )(*_in_hbm(bufs), after)
    return res[0], res[1], list(res[2:2 + ns]), list(res[2 + ns:2 + ns + nl]), res[-1]


def _split_wait(body, send_sems, recv_sems, srcs, lands, after, name):
    ns, nl = len(srcs), len(lands)
    bufs = list(srcs) + list(lands)
    res = pl.pallas_call(
        body, name=name, out_shape=tuple(pltpu.HBM(v.shape, v.dtype) for v in bufs),
        in_specs=[HBM_SPEC] * (ns + nl) + [SEM_SPEC, SEM_SPEC, ANY_SPEC], out_specs=tuple([HBM_SPEC] * (ns + nl)),
        input_output_aliases={t: t for t in range(ns + nl)}, compiler_params=SPLIT_PARAMS,
    )(*bufs, send_sems, recv_sems, after)
    return list(res[:ns]), list(res[ns:])


def _direct_exchange_start(grads, after, name):
    nt = len(grads)
    lands = [lax.empty((NDEV - 1,) + gr.shape[1:], gr.dtype) for gr in grads]

    def body(*refs):
        ins, lnd = refs[:nt], refs[nt:2 * nt]
        send_sems, recv_sems, token = refs[2 * nt + 1], refs[2 * nt + 2], refs[-1]
        x, y, c = _my_place()
        for t in range(nt):
            for k in range(1, NDEV):
                px, py, pc = _flip(x, k & 4), _flip(y, k & 2), _flip(c, k & 1)
                pltpu.make_async_remote_copy(
                    src_ref=ins[t].at[4 * px + 2 * py + pc], dst_ref=lnd[t].at[k - 1],
                    send_sem=send_sems.at[(NDEV - 1) * t + k - 1], recv_sem=recv_sems.at[(NDEV - 1) * t + k - 1],
                    device_id=(px, py, pc), device_id_type=MESH).start()
        token[...] = jnp.zeros_like(token)

    return _split_start(body, grads, lands, after, NDEV - 1, name)


def _direct_exchange_wait(send_sems, recv_sems, grads, lands, after, name):
    nt = len(grads)

    def body(*refs):
        ins, lnd = refs[:nt], refs[nt:2 * nt]
        s_sems, r_sems = refs[2 * nt], refs[2 * nt + 1]
        x, y, c = _my_place()
        for t in range(nt):
            for k in range(1, NDEV):
                cp = pltpu.make_async_remote_copy(
                    src_ref=ins[t].at[0], dst_ref=lnd[t].at[k - 1], send_sem=s_sems.at[(NDEV - 1) * t + k - 1],
                    recv_sem=r_sems.at[(NDEV - 1) * t + k - 1], device_id=(x, y, c), device_id_type=MESH)
                cp.wait_send()
                cp.wait_recv()

    return _split_wait(body, send_sems, recv_sems, grads, lands, after, name)


def _place_own(shards, axes, name):
    nt = len(shards)
    out_shapes = [SDS(tuple(s * NDEV if a == ax else s for a, s in enumerate(sh.shape)), sh.dtype)
                  for sh, ax in zip(shards, axes)]

    def body(*refs):
        ins, outs, sems = refs[:nt], refs[nt:2 * nt], refs[2 * nt]
        x, y, c = _my_place()
        copies = [pltpu.make_async_copy(ins[t], _slab(outs[t], axes[t], ins[t].shape[axes[t]], 4 * x + 2 * y + c), sems.at[t])
                  for t in range(nt)]
        for cp in copies:
            cp.start()
        for cp in copies:
            cp.wait()

    return pl.pallas_call(
        body, name=name, out_shape=out_shapes, in_specs=[VMEM_SPEC] * nt, out_specs=[HBM_SPEC] * nt,
        scratch_shapes=[pltpu.SemaphoreType.DMA((nt,))],
        compiler_params=pltpu.CompilerParams(vmem_limit_bytes=V7X_VMEM_LIMIT),
    )(*shards)


def _small_gather_start(v, me, after, name):
    land = lax.dynamic_update_slice(lax.empty((NDEV,) + v.shape, v.dtype), v[None], (me, 0, 0))

    def body(*refs):
        src, lnd = refs[0], refs[1]
        send_sems, recv_sems, token = refs[3], refs[4], refs[-1]
        x, y, c = _my_place()
        for k in range(1, NDEV):
            peer = (_flip(x, k & 4), _flip(y, k & 2), _flip(c, k & 1))
            pltpu.make_async_remote_copy(src_ref=src, dst_ref=lnd.at[4 * x + 2 * y + c], send_sem=send_sems.at[k - 1],
                                         recv_sem=recv_sems.at[k - 1], device_id=peer, device_id_type=MESH).start()
        token[...] = jnp.zeros_like(token)

    return _split_start(body, [v], [land], after, NDEV - 1, name)


def _small_gather_wait(send_sems, recv_sems, srcs, lands, after, name):
    def body(*refs):
        src, lnd, s_sems, r_sems = refs[0], refs[1], refs[2], refs[3]
        x, y, c = _my_place()
        for k in range(1, NDEV):
            sender = 4 * _flip(x, k & 4) + 2 * _flip(y, k & 2) + _flip(c, k & 1)
            cp = pltpu.make_async_remote_copy(src_ref=src, dst_ref=lnd.at[sender], send_sem=s_sems.at[k - 1],
                                              recv_sem=r_sems.at[k - 1], device_id=(x, y, c), device_id_type=MESH)
            cp.wait_send()
            cp.wait_recv()

    return _split_wait(body, send_sems, recv_sems, srcs, lands, after, name)[1][0]


def _gather_start(shards, axes, after, name):
    nt = len(shards)
    fulls = _place_own(shards, axes, name + "_own")

    def body(*refs):
        ins, outs = refs[:nt], refs[nt:2 * nt]
        send_sems, recv_sems, token = refs[2 * nt + 1], refs[2 * nt + 2], refs[-1]
        x, y, c = _my_place()
        dev = 4 * x + 2 * y + c
        peers = [(x, y, 1 - c), (1 - x, y, c), (x, 1 - y, c), (1 - x, 1 - y, c)]
        for t in range(nt):
            dst = _slab(outs[t], axes[t], ins[t].shape[axes[t]], dev)
            for k, peer in enumerate(peers):
                pltpu.make_async_remote_copy(src_ref=ins[t], dst_ref=dst, send_sem=send_sems.at[4 * t + k],
                                             recv_sem=recv_sems.at[4 * t + k], device_id=peer, device_id_type=MESH).start()
        token[...] = jnp.zeros_like(token)

    return _split_start(body, shards, fulls, after, 4, name)


def _gather_wait(send_sems, recv_sems, shards, fulls, axes, after, name):
    nt = len(shards)

    def body(*refs):
        ins, outs = refs[:nt], refs[nt:2 * nt]
        s_sems, r_sems = refs[2 * nt], refs[2 * nt + 1]
        x, y, c = _my_place()
        senders = [4 * x + 2 * y + (1 - c), 4 * (1 - x) + 2 * y + c, 4 * x + 2 * (1 - y) + c, 4 * (1 - x) + 2 * (1 - y) + c]
        for t in range(nt):
            for k, src_dev in enumerate(senders):
                cp = pltpu.make_async_remote_copy(
                    src_ref=ins[t], dst_ref=_slab(outs[t], axes[t], ins[t].shape[axes[t]], src_dev),
                    send_sem=s_sems.at[4 * t + k], recv_sem=r_sems.at[4 * t + k], device_id=(x, y, c), device_id_type=MESH)
                cp.wait_send()
                cp.wait_recv()

    return _split_wait(body, send_sems, recv_sems, shards, fulls, after, name)[1]


def _gather_pass_on(fulls, axes, name):
    nt = len(fulls)

    def body(*refs):
        outs = refs[nt:2 * nt]
        send_sems, recv_sems = refs[2 * nt:]
        x, y, c = _my_place()
        chips = [(1 - x, y), (x, 1 - y), (1 - x, 1 - y)]

        def copy(t, j, pc):
            cx, cy = chips[j]
            blk = _slab(outs[t], axes[t], outs[t].shape[axes[t]] // NDEV, 4 * cx + 2 * cy + pc)
            return pltpu.make_async_remote_copy(src_ref=blk, dst_ref=blk, send_sem=send_sems.at[t, j],
                                                recv_sem=recv_sems.at[t, j], device_id=(x, y, 1 - c), device_id_type=MESH)

        sends = [copy(t, j, c) for t in range(nt) for j in range(3)]
        for cp in sends:
            cp.start()
        for t in range(nt):
            for j in range(3):
                copy(t, j, 1 - c).wait_recv()
        for cp in sends:
            cp.wait_send()

    return pl.pallas_call(
        body, name=name, out_shape=[SDS(f.shape, f.dtype) for f in fulls], in_specs=[HBM_SPEC] * nt,
        out_specs=[HBM_SPEC] * nt, input_output_aliases={t: t for t in range(nt)},
        scratch_shapes=[pltpu.SemaphoreType.DMA((nt, 3)), pltpu.SemaphoreType.DMA((nt, 3))],
    )(*fulls)


def _row_tile(R, C, itemsize=4, target=1 << 20):
    best = R
    for tr in range(8, R, 8):
        if R % tr == 0 and tr * C * itemsize <= target:
            best = tr
    return best if best * C * itemsize <= target or best == R else R


def _as2d(a):
    return a.reshape(-1, a.shape[-1])


def _adamw_math(g, w, m, v):
    m2 = ADAM_B1 * m + (1.0 - ADAM_B1) * g
    v2 = ADAM_B2 * v + (1.0 - ADAM_B2) * (g * g)
    m_hat = m2 / (1.0 - ADAM_B1 ** ADAM_STEP)
    v_hat = v2 / (1.0 - ADAM_B2 ** ADAM_STEP)
    delta = -ADAM_LR * (m_hat / (jnp.sqrt(v_hat) + ADAM_EPS) + ADAM_WD * w)
    return delta, m2, v2


def _adamw(parts, w, m, v, name, layer=None, into=None):
    shp = w.shape
    w2, m2, v2 = _as2d(w), _as2d(m), _as2d(v)
    nlay = 1 if layer is None else shp[0]
    j = 0 if layer is None else layer
    RT, C = w2.shape
    R = RT // nlay
    tr = _row_tile(R, C)
    nblk = R // tr
    p3, specs, picks = [], [], []
    for p in parts:
        if isinstance(p, tuple):
            specs.append(pl.BlockSpec((1, tr, C), lambda i, s, k=len(picks): (s[k], i, 0)))
            picks.append(p[1])
            p = p[0]
        else:
            specs.append(pl.BlockSpec((p.size // (R * C), tr, C), lambda i, s: (0, i, 0)))
        p3.append(p.reshape((-1, R, C)))
    npart = len(p3)
    picks = jnp.stack(picks).astype(jnp.int32) if picks else jnp.zeros((1,), jnp.int32)
    rows = pl.BlockSpec((tr, C), lambda i, s: (j * nblk + i, 0))
    prior = [] if into is None else [_as2d(o) for o in into]

    def body(s_ref, *refs):
        prefs = refs[:npart]
        w_ref, m_ref, v_ref = refs[npart:npart + 3]
        g_ref, d_ref, nm_ref, nv_ref = refs[npart + 3 + len(prior):]
        g = None
        for pr in prefs:
            for k in range(pr.shape[0]):
                term = pr[k].astype(F32)
                g = term if g is None else g + term
        g_ref[...] = g
        d_ref[...], nm_ref[...], nv_ref[...] = _adamw_math(g, w_ref[...], m_ref[...], v_ref[...])

    outs = pl.pallas_call(
        body, name=name, out_shape=[SDS((RT, C), F32)] * 4,
        grid_spec=pltpu.PrefetchScalarGridSpec(
            num_scalar_prefetch=1, grid=(nblk,), in_specs=specs + [rows] * 3 + [ANY_SPEC] * len(prior), out_specs=[rows] * 4),
        input_output_aliases={1 + npart + 3 + k: k for k in range(len(prior))},
        compiler_params=_params(("parallel",)),
    )(picks, *p3, w2, m2, v2, *prior)
    return [o.reshape(shp) for o in outs]


def _ada_fwd(c_all, ada_w, ada_b_mine, name):
    L, D, Wc = ada_w.shape

    def body(c_ref, w_ref, b_ref, o_ref):
        cv = c_ref[...]
        act = cv * (1.0 / (1.0 + jnp.exp(-cv)))
        o_ref[0] = jnp.dot(act, w_ref[0], preferred_element_type=F32, precision=lax.Precision.HIGHEST) + b_ref[0]

    return pl.pallas_call(
        body, name=name, grid=(L,),
        in_specs=[_resident(c_all.shape), pl.BlockSpec((1, D, Wc), lambda l: (l, 0, 0)), pl.BlockSpec((1, 1, Wc), lambda l: (l, 0, 0))],
        out_specs=pl.BlockSpec((1, NDEV, Wc), lambda l: (l, 0, 0)), out_shape=SDS((L, NDEV, Wc), F32),
        compiler_params=_params(("parallel",)),
    )(c_all, ada_w, ada_b_mine.reshape(L, 1, Wc))


def _ada_update(c_all, dmod_mine, w, m, v, name):
    L, D, Wc = w.shape
    tr = _row_tile(D, Wc)

    def body(c_ref, d_ref, w_ref, m_ref, v_ref, g_ref, dl_ref, nm_ref, nv_ref):
        cv = c_ref[...]
        act = cv * (1.0 / (1.0 + jnp.exp(-cv)))
        g = lax.dot_general(act, d_ref[0], (((0,), (0,)), ((), ())), preferred_element_type=F32,
                            precision=lax.Precision.HIGHEST)
        g_ref[0] = g
        dl_ref[0], nm_ref[0], nv_ref[0] = _adamw_math(g, w_ref[0], m_ref[0], v_ref[0])

    blk = pl.BlockSpec((1, tr, Wc), lambda l, r: (l, r, 0))
    return pl.pallas_call(
        body, name=name, grid=(L, D // tr),
        in_specs=[pl.BlockSpec((NDEV, tr), lambda l, r: (0, r)), pl.BlockSpec((1, NDEV, Wc), lambda l, r: (l, 0, 0)),
                  blk, blk, blk],
        out_specs=[blk] * 4, out_shape=[SDS((L, D, Wc), F32)] * 4,
        compiler_params=_params(("parallel", "parallel")),
    )(c_all, dmod_mine, w, m, v)


WEIGHT_NAMES = ['ada_w', 'ada_b', 'norm_mix_g', 'norm_mlp_g', 'pool_w', 'pool_scale', 'sgu_w_in', 'sgu_ln_g', 'sgu_ln_b',
                'sgu_w_s', 'sgu_b_s', 'sgu_w_out', 'mla_w_dq_dkv', 'mla_q_norm_g', 'mla_kv_norm_g', 'mla_w_uq', 'mla_w_ukv',
                'mla_w_o', 'mlp_w1', 'mlp_w2', 'final_g']
REPLICATED_EARLY = ['sgu_ln_g', 'sgu_ln_b', 'sgu_w_s', 'sgu_b_s', 'mla_kv_norm_g']
REPLICATED = ['ada_b', 'norm_mix_g', 'norm_mlp_g', 'final_g']
PACK_ROWS = 64
LATE_MLP_LAYERS = (1,)
Q_HEAD = MLA_NOPE + MLA_ROPE


def _layer_matrices(i):
    kind, j = i % 3, i // 3
    if kind == 0:
        mats = [("pool_w", j, 1)]
    elif kind == 1:
        mats = [("sgu_w_in", j, 1), ("sgu_w_out", j, 0)]
    else:
        mats = [("mla_w_dq_dkv", j, 0), ("mla_w_uq", j, 1), ("mla_w_ukv", j, 1), ("mla_w_o", j, 0)]
    return mats + [("mlp_w1", i, 1), ("mlp_w2", i, 0)]


def _pack(arrays):
    flat = jnp.concatenate([a.reshape(-1).astype(F32) for a in arrays])
    rows = -(-flat.size // (LANES * PACK_ROWS)) * PACK_ROWS
    return jnp.pad(flat, (0, rows * LANES - flat.size)).reshape(rows, LANES)


def kernel(x, c, positions, ada_w, ada_b, norm_mix_g, norm_mlp_g, pool_w, pool_scale, sgu_w_in, sgu_ln_g, sgu_ln_b, sgu_w_s, sgu_b_s, sgu_w_out, mla_w_dq_dkv, mla_q_norm_g, mla_kv_norm_g, mla_w_uq, mla_w_ukv, mla_w_o, mlp_w1, mlp_w2, final_g, loss_target, m_ada_w, m_ada_b, m_norm_mix_g, m_norm_mlp_g, m_pool_w, m_pool_scale, m_sgu_w_in, m_sgu_ln_g, m_sgu_ln_b, m_sgu_w_s, m_sgu_b_s, m_sgu_w_out, m_mla_w_dq_dkv, m_mla_q_norm_g, m_mla_kv_norm_g, m_mla_w_uq, m_mla_w_ukv, m_mla_w_o, m_mlp_w1, m_mlp_w2, m_final_g, v_ada_w, v_ada_b, v_norm_mix_g, v_norm_mlp_g, v_pool_w, v_pool_scale, v_sgu_w_in, v_sgu_ln_g, v_sgu_ln_b, v_sgu_w_s, v_sgu_b_s, v_sgu_w_out, v_mla_w_dq_dkv, v_mla_q_norm_g, v_mla_kv_norm_g, v_mla_w_uq, v_mla_w_ukv, v_mla_w_o, v_mlp_w1, v_mlp_w2, v_final_g):
    a = dict(locals())
    S, D = x.shape[1], x.shape[2]
    L = ada_w.shape[0]
    Wc = ada_w.shape[2]
    me = 4 * lax.axis_index("x") + 2 * lax.axis_index("y") + lax.axis_index("c")

    v0 = _pack([c, pool_scale, mla_q_norm_g])
    g0 = _small_all_gather(v0, "gather_c").reshape(NDEV, -1)
    n_ps, n_qg = pool_scale.size, mla_q_norm_g.size
    c_all = g0[:, :D]
    ps_w = pool_scale.shape[1]
    ps_full = g0[:, D:D + n_ps].reshape(NDEV, -1, ps_w).transpose(1, 0, 2).reshape(-1, 1, D)
    qg_full = g0[:, D + n_ps:D + n_ps + n_qg].reshape(1, -1)

    ada_b_mine = lax.dynamic_slice_in_dim(ada_b, me * Wc, Wc, axis=1)
    modp = _ada_fwd(c_all, ada_w, ada_b_mine, "ada_fwd")
    ga = _small_all_gather(modp.reshape(-1, LANES), "gather_mod").reshape(NDEV, L, NDEV, Wc)
    mod = lax.dynamic_index_in_dim(ga, me, axis=2, keepdims=False).transpose(1, 0, 2).reshape(L, 6, D)
    mod8 = jnp.pad(mod, ((0, 0), (0, 2), (0, 0)))

    small = {"pool_scale": ps_full, "sgu_ln_g": sgu_ln_g, "sgu_ln_b": sgu_ln_b, "sgu_w_s": sgu_w_s[0],
             "sgu_b_s_t": sgu_b_s[0].T, "mla_q_norm_g": qg_full, "mla_kv_norm_g": mla_kv_norm_g}
    gmix, gmlp = norm_mix_g.reshape(L, 1, D), norm_mlp_g.reshape(L, 1, D)
    tables = _rope_tables(positions, S)

    def shards_of(mats):
        return [a[n][j].astype(BF16) for n, j, _ in mats], [ax for _, _, ax in mats]

    def as_weights(mats, fulls):
        w = {n: f for (n, _, _), f in zip(mats, fulls)}
        if "mla_w_uq" in w:
            lat_w = w["mla_w_dq_dkv"].shape[1]
            w["mla_wd"] = jnp.pad(w.pop("mla_w_dq_dkv"), ((0, 0), (0, -lat_w % LANES)))
            w["mla_wq"] = _pad_heads(w.pop("mla_w_uq"), Q_HEAD)
        return w

    def travelling(group, order, tag):
        shards, axes = shards_of(group)
        fly = _gather_start(shards, axes, order, f"gather_start_{tag}")

        def arrive(after):
            fulls = _gather_pass_on(_gather_wait(*fly[:4], axes, after, f"gather_wait_{tag}"), axes, f"gather_pass_{tag}")
            return as_weights(group, fulls), fulls[0]

        return fly[4], arrive

    xc = x[0]
    wts, saved, arrivals = [], [], []
    for i in range(L):
        mats = _layer_matrices(i)
        mod_i = mod8[i]
        if i == 0:
            fulls = _all_gather_group(*shards_of(mats[:-1]), mod8, "gather_w_0")
            w, order = as_weights(mats[:-1], fulls), fulls[0]
            token, arrive = travelling(mats[-1:], order, "0b")
            mod_i, order = mod_i + token[0, 0], token
            lazy = [(mats[-1:], arrive)]
        else:
            w, order = arrivals[0][1](xc)
            lazy = arrivals[1:]
        for group, arrive in lazy:
            for n, _, _ in group:
                w[n] = lambda after, arrive=arrive: arrive(after)[0]
        wts.append(w)
        arrivals = []
        if i + 1 < L:
            nxt = _layer_matrices(i + 1)
            for k, group in enumerate([nxt[:-2], nxt[-2:]] if i + 1 in LATE_MLP_LAYERS else [nxt]):
                token, arrive = travelling(group, order, f"{i + 1}" + ("m" if k else ""))
                mod_i, order = mod_i + token[0, 0], token
                arrivals.append((group, arrive))
        xc, sv = _layer_forward(i, xc, mod_i, gmix[i], gmlp[i], wts[i], small, tables)
        saved.append(sv)
    dx, loss_acc = _loss_head(xc, loss_target[0], final_g.reshape(1, D), _tiles(S)[0], "loss_head")

    res = {}

    def start_reduce(i, mats, g, after, tag=""):
        g = dict(g)
        if "mla_wq" in g:
            g["mla_w_dq_dkv"] = g.pop("mla_wd")[:, :mla_w_dq_dkv.shape[2]]
            g["mla_w_uq"] = _unpad_heads(g.pop("mla_wq"), Q_HEAD)
        gl = []
        for n, j, ax in mats:
            gm, blk = g[n].astype(BF16), a[n][j].shape
            if gm.shape != (NDEV,) + blk:
                gm = jnp.moveaxis(gm.reshape(blk[:ax] + (NDEV,) + blk[ax:]), ax, 0)
            gl.append(gm)
        return _direct_exchange_start(gl, after, f"rs_direct_start_{i}{tag}") + (mats, f"{i}{tag}")

    def finish_reduce(fly, after):
        own, recv = _direct_exchange_wait(*fly[:4], after, f"rs_direct_wait_{fly[6]}")
        for (n, j, _), p, r in zip(fly[5], own, recv):
            res[n] = _adamw([(p, me), r], a[n], a["m_" + n], a["v_" + n], f"adamw_{n}_{j}", layer=j, into=res.get(n))

    def update_replicated(gathered, names, tail, label):
        zeros_tail = [jnp.zeros_like(t) for t in tail]
        packs = _adamw([gathered], _pack([a[n] for n in names] + zeros_tail), _pack([a["m_" + n] for n in names] + zeros_tail),
                       _pack([a["v_" + n] for n in names] + zeros_tail), label)
        flat = [t.reshape(-1) for t in packs]
        off = 0
        for n in names:
            res[n] = [f[off:off + a[n].size].reshape(a[n].shape) for f in flat]
            off += a[n].size
        sums = []
        for t in tail:
            sums.append(flat[0][off:off + t.size].reshape(t.shape))
            off += t.size
        return sums, packs[0]

    stats, sgrads, flying, early = [None] * L, {}, [], None
    for i in reversed(range(L)):
        mats = _layer_matrices(i)
        mod_i = mod8[i]
        for fly in flying:
            mod_i = mod_i + fly[4][0, 0]
        if early is not None:
            mod_i = mod_i + early[4][0, 0]
        sent = []

        def hook(g_mlp, i=i, mats=mats, dx=dx, sent=sent):
            sent.append(start_reduce(i, mats[-2:], g_mlp, dx, "a"))
            return sent[0][4][0, 0]

        dx, stats[i], g, sgr = _layer_backward(i, dx, saved[i], mod_i, gmix[i], gmlp[i], wts[i], small, tables, hook)
        sgrads.update(sgr)
        for fly in flying:
            finish_reduce(fly, dx)
        flying = sent + [start_reduce(i, mats[:-2], g, dx)]
        if i == 1:
            early_tail = [sgrads["mla_q_norm_g"]]
            early = _small_gather_start(_pack([sgrads[n] for n in REPLICATED_EARLY] + early_tail), me, flying[-1][4],
                                        "gather_small_start")
    (g_qg,), _ = update_replicated(_small_gather_wait(*early[:4], dx, "gather_small_wait"), REPLICATED_EARLY, early_tail,
                                   "adamw_replicated_early")

    sg = {"ada_b": jnp.stack([s[0:6] for s in stats]), "norm_mix_g": jnp.stack([s[6] for s in stats]),
          "norm_mlp_g": jnp.stack([s[7] for s in stats]), "final_g": loss_acc[0]}
    ps_grad = jnp.concatenate([sgrads[f"pool_scale_{j}"] for j in range(pool_scale.shape[0])])
    tail = [ps_grad, loss_acc[2, :LANES]]
    packed = _pack([sg[n] for n in REPLICATED] + tail) + flying[-1][4][0, 0]
    gathered = _small_all_gather(packed, "gather_small")
    (g_ps, loss_lanes), g_p = update_replicated(gathered, REPLICATED, tail, "adamw_replicated")
    for fly in flying:
        finish_reduce(fly, g_p)
    loss = loss_lanes[0]
    res["pool_scale"] = _adamw([lax.dynamic_slice_in_dim(g_ps, me * ps_w, ps_w, axis=1)], pool_scale, m_pool_scale,
                               v_pool_scale, "adamw_pool_scale")
    qg_w = mla_q_norm_g.shape[1]
    res["mla_q_norm_g"] = _adamw([lax.dynamic_slice_in_dim(g_qg, me * qg_w, qg_w, axis=1)], mla_q_norm_g, m_mla_q_norm_g,
                                 v_mla_q_norm_g, "adamw_q_norm_g")

    n_mod = L * 6 * D
    dmod_all = gathered.reshape(NDEV, -1)[:, :n_mod].reshape(NDEV, L, 6 * D)
    dmod_mine = lax.dynamic_slice_in_dim(dmod_all, me * Wc, Wc, axis=2).transpose(1, 0, 2)
    res["ada_w"] = _ada_update(c_all, dmod_mine, ada_w, m_ada_w, v_ada_w, "adamw_ada_w")

    return (loss, dx.reshape(x.shape), *[res[n][k] for k in range(4) for n in WEIGHT_NAMES])
```
